```python
import math
import jax, jax.numpy as jnp
from jax import lax
import numpy as np

D_MODEL = 2048
BATCH = 8
SEQ = 2048
DEPTH = 1

CHUNK = 64
N_META = 16
Q_BLOCK = 128
EPS = 1e-6

D_SSM = D_MODEL // 2
SSM_GROUP = 16
N_SSM_GROUPS = D_SSM // SSM_GROUP
SSM_STATE = 64
DT_MIN = 1e-3
DT_MAX = 1e-1

MLA_HEADS = 8
QK_NOPE = 128
QK_ROPE = 64
V_HEAD = 128
Q_LORA = 512
KV_LORA = 256
D_ATTN = MLA_HEADS * V_HEAD
ROPE_BASE = 10000.0

D_MIX = D_SSM + D_ATTN
D_IN = D_SSM + Q_LORA + KV_LORA + QK_ROPE

D_FF = 5504
CONV_W = 3

kernel_name = "hybrid_s5_mla_convffn_block"


def rmsnorm(x, g):
    xf = x.astype(jnp.float32)
    y = xf * lax.rsqrt(jnp.mean(xf * xf, axis=-1, keepdims=True) + EPS)
    return (y * g.astype(jnp.float32)).astype(x.dtype)


def rotary(x, cos, sin):
    x1, x2 = jnp.split(x, 2, axis=-1)
    return jnp.concatenate([x1 * cos - x2 * sin, x2 * cos + x1 * sin], axis=-1)


def s5_mixer(u, lam_re, lam_im, log_dt, b_re, b_im, c_re, c_im, d_skip, w_glu, b_glu):
    bsz, L, _ = u.shape
    f32 = jnp.float32
    uf = u.astype(f32).reshape(bsz, L, N_SSM_GROUPS, SSM_GROUP)
    lam = lax.complex(lam_re.astype(f32), lam_im.astype(f32))
    dt = jnp.exp(log_dt.astype(f32))[:, None]
    lam_bar = jnp.exp(lam * dt)
    b = lax.complex(b_re.astype(f32), b_im.astype(f32))
    b_bar = ((lam_bar - 1.0) / lam)[..., None] * b
    bu = jnp.einsum('blgc,gpc->blgp', uf.astype(jnp.complex64), b_bar)
    a = jnp.broadcast_to(lam_bar, bu.shape)

    def combine(e1, e2):
        a1, s1 = e1
        a2, s2 = e2
        return a1 * a2, a2 * s1 + s2

    _, h = lax.associative_scan(combine, (a, bu), axis=1)
    c = lax.complex(c_re.astype(f32), c_im.astype(f32))
    y = jnp.real(jnp.einsum('blgp,gcp->blgc', h, c))
    y = y + d_skip.astype(f32).reshape(N_SSM_GROUPS, SSM_GROUP) * uf
    y = y.reshape(bsz, L, D_SSM)
    g = jax.nn.gelu(y)
    out = g * jax.nn.sigmoid(g @ w_glu.astype(f32) + b_glu.astype(f32))
    return out.astype(u.dtype)


def mla_mixer(q_a, kv_a, k_pe, q_a_norm, w_q_b, kv_a_norm, w_kv_b, cos, sin, chunk_id):
    bsz, L, _ = q_a.shape
    q = (rmsnorm(q_a, q_a_norm) @ w_q_b).reshape(bsz, L, MLA_HEADS, QK_NOPE + QK_ROPE)
    q_nope, q_pe = q[..., :QK_NOPE], q[..., QK_NOPE:]
    q_pe = rotary(q_pe, cos[:, None, :], sin[:, None, :])
    kv = (rmsnorm(kv_a, kv_a_norm) @ w_kv_b).reshape(bsz, L, MLA_HEADS, QK_NOPE + V_HEAD)
    k_nope, v = kv[..., :QK_NOPE], kv[..., QK_NOPE:]
    k_pe = rotary(k_pe, cos, sin)
    scale = 1.0 / math.sqrt(QK_NOPE + QK_ROPE)

    n_blk = -(-L // Q_BLOCK)
    pad = n_blk * Q_BLOCK - L

    def to_blocks(t):
        t = jnp.pad(t, ((0, 0), (0, pad)) + ((0, 0),) * (t.ndim - 2))
        return jnp.moveaxis(t.reshape(bsz, n_blk, Q_BLOCK, *t.shape[2:]), 1, 0)

    q_cid = jnp.pad(chunk_id, (0, pad), constant_values=2 ** 30).reshape(n_blk, Q_BLOCK)

    def attend(args):
        qn, qp, qc = args
        s = jnp.einsum('bqhd,bkhd->bhqk', qn, k_nope, preferred_element_type=jnp.float32)
        s = s + jnp.einsum('bqhr,bkr->bhqk', qp, k_pe, preferred_element_type=jnp.float32)
        mask = chunk_id[None, :] <= qc[:, None]
        s = jnp.where(mask[None, None], s * scale, jnp.finfo(jnp.float32).min)
        p = jax.nn.softmax(s, axis=-1).astype(v.dtype)
        return jnp.einsum('bhqk,bkhd->bqhd', p, v)

    o = lax.map(attend, (to_blocks(q_nope), to_blocks(q_pe), q_cid))
    o = jnp.moveaxis(o, 0, 1).reshape(bsz, n_blk * Q_BLOCK, D_ATTN)[:, :L]
    return o


def conv_ffn(x, w_up, conv_w, conv_b, w_down):
    L = x.shape[1]
    gate, val = jnp.split(x @ w_up, 2, axis=-1)
    gp = jnp.pad(gate, ((0, 0), (CONV_W - 1, 0), (0, 0)))
    gate = sum(conv_w[k] * gp[:, k:k + L] for k in range(CONV_W)) + conv_b
    return (jax.nn.silu(gate) * val) @ w_down


def _fwd_setup_inputs(seed: int = 0) -> dict:
    key = jax.random.key(seed)
    ks = jax.random.split(key, 32)
    f32 = jnp.float32
    nrm = lambda k, shape, s: jax.random.normal(k, shape, f32) * s
    gain = lambda k, shape: 1.0 + 0.01 * jax.random.normal(k, shape, f32)
    G, P, C = N_SSM_GROUPS, SSM_STATE, SSM_GROUP
    lam_re = -0.5 + 0.01 * jax.random.normal(ks[4], (DEPTH, G, P), f32)
    lam_im = jnp.pi * jnp.arange(P, dtype=f32)[None, None, :] + 0.01 * jax.random.normal(ks[5], (DEPTH, G, P), f32)
    log_dt = jax.random.uniform(ks[6], (DEPTH, G), f32, math.log(DT_MIN), math.log(DT_MAX))
    return {
        "x": jax.random.normal(ks[0], (BATCH, SEQ, D_MODEL), f32),
        "meta_tokens": nrm(ks[1], (N_META, D_MODEL), 1.0),
        "mix_norm": gain(ks[2], (DEPTH, D_MODEL)),
        "w_in": nrm(ks[3], (DEPTH, D_MODEL, D_IN), D_MODEL ** -0.5),
        "lam_re": lam_re,
        "lam_im": lam_im,
        "log_dt": log_dt,
        "b_re": nrm(ks[7], (DEPTH, G, P, C), (2 * C) ** -0.5),
        "b_im": nrm(ks[8], (DEPTH, G, P, C), (2 * C) ** -0.5),
        "c_re": nrm(ks[9], (DEPTH, G, C, P), (2 * P) ** -0.5),
        "c_im": nrm(ks[10], (DEPTH, G, C, P), (2 * P) ** -0.5),
        "d_skip": nrm(ks[11], (DEPTH, D_SSM), 1.0),
        "w_glu": nrm(ks[12], (DEPTH, D_SSM, D_SSM), D_SSM ** -0.5),
        "b_glu": nrm(ks[13], (DEPTH, D_SSM), 0.01),
        "q_a_norm": gain(ks[14], (DEPTH, Q_LORA)),
        "w_q_b": nrm(ks[15], (DEPTH, Q_LORA, MLA_HEADS * (QK_NOPE + QK_ROPE)), Q_LORA ** -0.5),
        "kv_a_norm": gain(ks[16], (DEPTH, KV_LORA)),
        "w_kv_b": nrm(ks[17], (DEPTH, KV_LORA, MLA_HEADS * (QK_NOPE + V_HEAD)), KV_LORA ** -0.5),
        "out_norm_ssm": gain(ks[18], (DEPTH, D_SSM)),
        "out_norm_attn": gain(ks[19], (DEPTH, D_ATTN)),
        "w_out": nrm(ks[20], (DEPTH, D_MIX, D_MODEL), D_MIX ** -0.5),
        "ffn_norm": gain(ks[21], (DEPTH, D_MODEL)),
        "w_up": nrm(ks[22], (DEPTH, D_MODEL, 2 * D_FF), D_MODEL ** -0.5),
        "conv_w": nrm(ks[23], (DEPTH, CONV_W, D_FF), CONV_W ** -0.5),
        "conv_b": nrm(ks[24], (DEPTH, D_FF), 0.01),
        "w_down": nrm(ks[25], (DEPTH, D_FF, D_MODEL), D_FF ** -0.5),
        "final_norm": gain(ks[26], (D_MODEL,)),
    }


def _fwd_reference(x, meta_tokens, mix_norm, w_in, lam_re, lam_im, log_dt, b_re, b_im, c_re, c_im,
              d_skip, w_glu, b_glu, q_a_norm, w_q_b, kv_a_norm, w_kv_b, out_norm_ssm,
              out_norm_attn, w_out, ffn_norm, w_up, conv_w, conv_b, w_down, final_norm):
    bsz = x.shape[0]
    meta = jnp.broadcast_to(meta_tokens.astype(x.dtype)[None], (bsz, N_META, D_MODEL))
    h = jnp.concatenate([meta, x], axis=1)
    L = h.shape[1]

    pos = jnp.arange(L, dtype=jnp.int32)
    chunk_id = jnp.where(pos < N_META, 0, 1 + (pos - N_META) // CHUNK)
    inv_freq = 1.0 / (ROPE_BASE ** (jnp.arange(0, QK_ROPE, 2, dtype=jnp.float32) / QK_ROPE))
    ang = pos.astype(jnp.float32)[:, None] * inv_freq[None, :]
    cos = jnp.cos(ang).astype(x.dtype)
    sin = jnp.sin(ang).astype(x.dtype)

    for i in range(DEPTH):
        xn = rmsnorm(h, mix_norm[i])
        z = xn @ w_in[i]
        o1 = D_SSM
        o2 = o1 + Q_LORA
        o3 = o2 + KV_LORA
        u, q_a, kv_a, k_pe = z[..., :o1], z[..., o1:o2], z[..., o2:o3], z[..., o3:]
        ya = s5_mixer(u, lam_re[i], lam_im[i], log_dt[i], b_re[i], b_im[i], c_re[i], c_im[i],
                      d_skip[i], w_glu[i], b_glu[i])
        yb = mla_mixer(q_a, kv_a, k_pe, q_a_norm[i], w_q_b[i], kv_a_norm[i], w_kv_b[i],
                       cos, sin, chunk_id)
        y = jnp.concatenate([rmsnorm(ya, out_norm_ssm[i]), rmsnorm(yb, out_norm_attn[i])], axis=-1)
        h = h + y @ w_out[i]
        h = h + conv_ffn(rmsnorm(h, ffn_norm[i]), w_up[i], conv_w[i], conv_b[i], w_down[i])

    return rmsnorm(h, final_norm)[:, N_META:]


import jax as _jax
import jax.numpy as _jnp

TWIN_FORMAT = 'train_step'
FWD_PARAMS = ['x', 'meta_tokens', 'mix_norm', 'w_in', 'lam_re', 'lam_im', 'log_dt', 'b_re', 'b_im', 'c_re', 'c_im', 'd_skip', 'w_glu', 'b_glu', 'q_a_norm', 'w_q_b', 'kv_a_norm', 'w_kv_b', 'out_norm_ssm', 'out_norm_attn', 'w_out', 'ffn_norm', 'w_up', 'conv_w', 'conv_b', 'w_down', 'final_norm']
TWIN_WEIGHTS = ['meta_tokens', 'mix_norm', 'w_in', 'lam_re', 'lam_im', 'log_dt', 'b_re', 'b_im', 'c_re', 'c_im', 'd_skip', 'w_glu', 'b_glu', 'q_a_norm', 'w_q_b', 'kv_a_norm', 'w_kv_b', 'out_norm_ssm', 'out_norm_attn', 'w_out', 'ffn_norm', 'w_up', 'conv_w', 'conv_b', 'w_down', 'final_norm']
TWIN_DIFF_INPUT = 'x'
TWIN_INPUTS = ['x', 'meta_tokens', 'mix_norm', 'w_in', 'lam_re', 'lam_im', 'log_dt', 'b_re', 'b_im', 'c_re', 'c_im', 'd_skip', 'w_glu', 'b_glu', 'q_a_norm', 'w_q_b', 'kv_a_norm', 'w_kv_b', 'out_norm_ssm', 'out_norm_attn', 'w_out', 'ffn_norm', 'w_up', 'conv_w', 'conv_b', 'w_down', 'final_norm', 'loss_target', 'm_meta_tokens', 'm_mix_norm', 'm_w_in', 'm_lam_re', 'm_lam_im', 'm_log_dt', 'm_b_re', 'm_b_im', 'm_c_re', 'm_c_im', 'm_d_skip', 'm_w_glu', 'm_b_glu', 'm_q_a_norm', 'm_w_q_b', 'm_kv_a_norm', 'm_w_kv_b', 'm_out_norm_ssm', 'm_out_norm_attn', 'm_w_out', 'm_ffn_norm', 'm_w_up', 'm_conv_w', 'm_conv_b', 'm_w_down', 'm_final_norm', 'v_meta_tokens', 'v_mix_norm', 'v_w_in', 'v_lam_re', 'v_lam_im', 'v_log_dt', 'v_b_re', 'v_b_im', 'v_c_re', 'v_c_im', 'v_d_skip', 'v_w_glu', 'v_b_glu', 'v_q_a_norm', 'v_w_q_b', 'v_kv_a_norm', 'v_w_kv_b', 'v_out_norm_ssm', 'v_out_norm_attn', 'v_w_out', 'v_ffn_norm', 'v_w_up', 'v_conv_w', 'v_conv_b', 'v_w_down', 'v_final_norm']
TWIN_OUTPUTS = ['loss', 'grad_x', 'grad_meta_tokens', 'grad_mix_norm', 'grad_w_in', 'grad_lam_re', 'grad_lam_im', 'grad_log_dt', 'grad_b_re', 'grad_b_im', 'grad_c_re', 'grad_c_im', 'grad_d_skip', 'grad_w_glu', 'grad_b_glu', 'grad_q_a_norm', 'grad_w_q_b', 'grad_kv_a_norm', 'grad_w_kv_b', 'grad_out_norm_ssm', 'grad_out_norm_attn', 'grad_w_out', 'grad_ffn_norm', 'grad_w_up', 'grad_conv_w', 'grad_conv_b', 'grad_w_down', 'grad_final_norm', 'delta_meta_tokens', 'delta_mix_norm', 'delta_w_in', 'delta_lam_re', 'delta_lam_im', 'delta_log_dt', 'delta_b_re', 'delta_b_im', 'delta_c_re', 'delta_c_im', 'delta_d_skip', 'delta_w_glu', 'delta_b_glu', 'delta_q_a_norm', 'delta_w_q_b', 'delta_kv_a_norm', 'delta_w_kv_b', 'delta_out_norm_ssm', 'delta_out_norm_attn', 'delta_w_out', 'delta_ffn_norm', 'delta_w_up', 'delta_conv_w', 'delta_conv_b', 'delta_w_down', 'delta_final_norm', 'new_m_meta_tokens', 'new_m_mix_norm', 'new_m_w_in', 'new_m_lam_re', 'new_m_lam_im', 'new_m_log_dt', 'new_m_b_re', 'new_m_b_im', 'new_m_c_re', 'new_m_c_im', 'new_m_d_skip', 'new_m_w_glu', 'new_m_b_glu', 'new_m_q_a_norm', 'new_m_w_q_b', 'new_m_kv_a_norm', 'new_m_w_kv_b', 'new_m_out_norm_ssm', 'new_m_out_norm_attn', 'new_m_w_out', 'new_m_ffn_norm', 'new_m_w_up', 'new_m_conv_w', 'new_m_conv_b', 'new_m_w_down', 'new_m_final_norm', 'new_v_meta_tokens', 'new_v_mix_norm', 'new_v_w_in', 'new_v_lam_re', 'new_v_lam_im', 'new_v_log_dt', 'new_v_b_re', 'new_v_b_im', 'new_v_c_re', 'new_v_c_im', 'new_v_d_skip', 'new_v_w_glu', 'new_v_b_glu', 'new_v_q_a_norm', 'new_v_w_q_b', 'new_v_kv_a_norm', 'new_v_w_kv_b', 'new_v_out_norm_ssm', 'new_v_out_norm_attn', 'new_v_w_out', 'new_v_ffn_norm', 'new_v_w_up', 'new_v_conv_w', 'new_v_conv_b', 'new_v_w_down', 'new_v_final_norm']
TWIN_LEAF_KINDS = {'loss': 'loss', 'grad_x': 'grad_x', 'grad_meta_tokens': 'grad_w', 'grad_mix_norm': 'grad_w', 'grad_w_in': 'grad_w', 'grad_lam_re': 'grad_w', 'grad_lam_im': 'grad_w', 'grad_log_dt': 'grad_w', 'grad_b_re': 'grad_w', 'grad_b_im': 'grad_w', 'grad_c_re': 'grad_w', 'grad_c_im': 'grad_w', 'grad_d_skip': 'grad_w', 'grad_w_glu': 'grad_w', 'grad_b_glu': 'grad_w', 'grad_q_a_norm': 'grad_w', 'grad_w_q_b': 'grad_w', 'grad_kv_a_norm': 'grad_w', 'grad_w_kv_b': 'grad_w', 'grad_out_norm_ssm': 'grad_w', 'grad_out_norm_attn': 'grad_w', 'grad_w_out': 'grad_w', 'grad_ffn_norm': 'grad_w', 'grad_w_up': 'grad_w', 'grad_conv_w': 'grad_w', 'grad_conv_b': 'grad_w', 'grad_w_down': 'grad_w', 'grad_final_norm': 'grad_w', 'delta_meta_tokens': 'delta_w', 'delta_mix_norm': 'delta_w', 'delta_w_in': 'delta_w', 'delta_lam_re': 'delta_w', 'delta_lam_im': 'delta_w', 'delta_log_dt': 'delta_w', 'delta_b_re': 'delta_w', 'delta_b_im': 'delta_w', 'delta_c_re': 'delta_w', 'delta_c_im': 'delta_w', 'delta_d_skip': 'delta_w', 'delta_w_glu': 'delta_w', 'delta_b_glu': 'delta_w', 'delta_q_a_norm': 'delta_w', 'delta_w_q_b': 'delta_w', 'delta_kv_a_norm': 'delta_w', 'delta_w_kv_b': 'delta_w', 'delta_out_norm_ssm': 'delta_w', 'delta_out_norm_attn': 'delta_w', 'delta_w_out': 'delta_w', 'delta_ffn_norm': 'delta_w', 'delta_w_up': 'delta_w', 'delta_conv_w': 'delta_w', 'delta_conv_b': 'delta_w', 'delta_w_down': 'delta_w', 'delta_final_norm': 'delta_w', 'new_m_meta_tokens': 'new_m', 'new_m_mix_norm': 'new_m', 'new_m_w_in': 'new_m', 'new_m_lam_re': 'new_m', 'new_m_lam_im': 'new_m', 'new_m_log_dt': 'new_m', 'new_m_b_re': 'new_m', 'new_m_b_im': 'new_m', 'new_m_c_re': 'new_m', 'new_m_c_im': 'new_m', 'new_m_d_skip': 'new_m', 'new_m_w_glu': 'new_m', 'new_m_b_glu': 'new_m', 'new_m_q_a_norm': 'new_m', 'new_m_w_q_b': 'new_m', 'new_m_kv_a_norm': 'new_m', 'new_m_w_kv_b': 'new_m', 'new_m_out_norm_ssm': 'new_m', 'new_m_out_norm_attn': 'new_m', 'new_m_w_out': 'new_m', 'new_m_ffn_norm': 'new_m', 'new_m_w_up': 'new_m', 'new_m_conv_w': 'new_m', 'new_m_conv_b': 'new_m', 'new_m_w_down': 'new_m', 'new_m_final_norm': 'new_m', 'new_v_meta_tokens': 'new_v', 'new_v_mix_norm': 'new_v', 'new_v_w_in': 'new_v', 'new_v_lam_re': 'new_v', 'new_v_lam_im': 'new_v', 'new_v_log_dt': 'new_v', 'new_v_b_re': 'new_v', 'new_v_b_im': 'new_v', 'new_v_c_re': 'new_v', 'new_v_c_im': 'new_v', 'new_v_d_skip': 'new_v', 'new_v_w_glu': 'new_v', 'new_v_b_glu': 'new_v', 'new_v_q_a_norm': 'new_v', 'new_v_w_q_b': 'new_v', 'new_v_kv_a_norm': 'new_v', 'new_v_w_kv_b': 'new_v', 'new_v_out_norm_ssm': 'new_v', 'new_v_out_norm_attn': 'new_v', 'new_v_w_out': 'new_v', 'new_v_ffn_norm': 'new_v', 'new_v_w_up': 'new_v', 'new_v_conv_w': 'new_v', 'new_v_conv_b': 'new_v', 'new_v_w_down': 'new_v', 'new_v_final_norm': 'new_v'}


def _forward(args):
    return _fwd_reference(*[args[k] for k in FWD_PARAMS])


def _output_shape():
    out = _jax.eval_shape(lambda: _forward(_fwd_setup_inputs(0)))
    return out.shape, out.dtype

N_MICROBATCH = 1
ADAM_LR = 0.001
ADAM_B1 = 0.9
ADAM_B2 = 0.999
ADAM_EPS = 1e-08
ADAM_WD = 0.01
ADAM_STEP = 10
PER_EXAMPLE_BATCH_AXIS = {'x': 0, 'loss_target': 0}
SHARED_INPUTS = []
_WEIGHT_DTYPES = {'meta_tokens': _jnp.float32, 'mix_norm': _jnp.float32, 'w_in': _jnp.float32, 'lam_re': _jnp.float32, 'lam_im': _jnp.float32, 'log_dt': _jnp.float32, 'b_re': _jnp.float32, 'b_im': _jnp.float32, 'c_re': _jnp.float32, 'c_im': _jnp.float32, 'd_skip': _jnp.float32, 'w_glu': _jnp.float32, 'b_glu': _jnp.float32, 'q_a_norm': _jnp.float32, 'w_q_b': _jnp.float32, 'kv_a_norm': _jnp.float32, 'w_kv_b': _jnp.float32, 'out_norm_ssm': _jnp.float32, 'out_norm_attn': _jnp.float32, 'w_out': _jnp.float32, 'ffn_norm': _jnp.float32, 'w_up': _jnp.float32, 'conv_w': _jnp.float32, 'conv_b': _jnp.float32, 'w_down': _jnp.float32, 'final_norm': _jnp.float32}
MOMENT_SCALE = {'meta_tokens': 3.642886e-03, 'mix_norm': 6.539534e-02, 'w_in': 6.887540e-02, 'lam_re': 2.245174e-03, 'lam_im': 2.441625e-03, 'log_dt': 2.379970e+00, 'b_re': 1.641651e-03, 'b_im': 1.653824e-03, 'c_re': 3.211165e-03, 'c_im': 3.194102e-03, 'd_skip': 5.844810e-02, 'w_glu': 1.438019e-02, 'b_glu': 2.445420e-02, 'q_a_norm': 7.689705e-02, 'w_q_b': 3.945592e-02, 'kv_a_norm': 1.523933e-01, 'w_kv_b': 4.642859e-02, 'out_norm_ssm': 5.631848e-02, 'out_norm_attn': 4.769157e-02, 'w_out': 4.876266e-02, 'ffn_norm': 3.628824e-02, 'w_up': 1.556001e-02, 'conv_w': 1.584983e-02, 'conv_b': 1.521699e-02, 'w_down': 2.501769e-02, 'final_norm': 8.011271e+00}


def _to_microbatches(a, axis):
    t = _jnp.moveaxis(a, axis, 0)
    t = t.reshape((N_MICROBATCH, t.shape[0] // N_MICROBATCH) + t.shape[1:])
    return _jnp.moveaxis(t, 1, axis + 1)


def setup_inputs(seed: int = 0) -> dict:
    inp = _fwd_setup_inputs(seed)
    key = _jax.random.fold_in(_jax.random.key(seed), 7919)
    shape, _ = _output_shape()
    out = dict(inp)
    out["loss_target"] = _jax.random.normal(_jax.random.fold_in(key, 0), shape, _jnp.float32)
    for i, name in enumerate(TWIN_WEIGHTS):
        w = inp[name].astype(_jnp.float32)
        if MOMENT_SCALE is None:
            s = _jnp.sqrt(_jnp.mean(_jnp.square(w)) + 1e-30)
        else:
            s = MOMENT_SCALE[name]
        km, kv = _jax.random.split(_jax.random.fold_in(key, i + 1))
        out[name] = w
        out["m_" + name] = s * _jax.random.normal(km, w.shape, _jnp.float32)
        out["v_" + name] = (s * s) * _jax.random.uniform(kv, w.shape, _jnp.float32, 0.5, 1.5)
    if N_MICROBATCH > 1:
        for name, axis in PER_EXAMPLE_BATCH_AXIS.items():
            out[name] = _to_microbatches(out[name], axis)
    return {'x': out['x'], 'meta_tokens': out['meta_tokens'], 'mix_norm': out['mix_norm'], 'w_in': out['w_in'], 'lam_re': out['lam_re'], 'lam_im': out['lam_im'], 'log_dt': out['log_dt'], 'b_re': out['b_re'], 'b_im': out['b_im'], 'c_re': out['c_re'], 'c_im': out['c_im'], 'd_skip': out['d_skip'], 'w_glu': out['w_glu'], 'b_glu': out['b_glu'], 'q_a_norm': out['q_a_norm'], 'w_q_b': out['w_q_b'], 'kv_a_norm': out['kv_a_norm'], 'w_kv_b': out['w_kv_b'], 'out_norm_ssm': out['out_norm_ssm'], 'out_norm_attn': out['out_norm_attn'], 'w_out': out['w_out'], 'ffn_norm': out['ffn_norm'], 'w_up': out['w_up'], 'conv_w': out['conv_w'], 'conv_b': out['conv_b'], 'w_down': out['w_down'], 'final_norm': out['final_norm'], 'loss_target': out['loss_target'], 'm_meta_tokens': out['m_meta_tokens'], 'm_mix_norm': out['m_mix_norm'], 'm_w_in': out['m_w_in'], 'm_lam_re': out['m_lam_re'], 'm_lam_im': out['m_lam_im'], 'm_log_dt': out['m_log_dt'], 'm_b_re': out['m_b_re'], 'm_b_im': out['m_b_im'], 'm_c_re': out['m_c_re'], 'm_c_im': out['m_c_im'], 'm_d_skip': out['m_d_skip'], 'm_w_glu': out['m_w_glu'], 'm_b_glu': out['m_b_glu'], 'm_q_a_norm': out['m_q_a_norm'], 'm_w_q_b': out['m_w_q_b'], 'm_kv_a_norm': out['m_kv_a_norm'], 'm_w_kv_b': out['m_w_kv_b'], 'm_out_norm_ssm': out['m_out_norm_ssm'], 'm_out_norm_attn': out['m_out_norm_attn'], 'm_w_out': out['m_w_out'], 'm_ffn_norm': out['m_ffn_norm'], 'm_w_up': out['m_w_up'], 'm_conv_w': out['m_conv_w'], 'm_conv_b': out['m_conv_b'], 'm_w_down': out['m_w_down'], 'm_final_norm': out['m_final_norm'], 'v_meta_tokens': out['v_meta_tokens'], 'v_mix_norm': out['v_mix_norm'], 'v_w_in': out['v_w_in'], 'v_lam_re': out['v_lam_re'], 'v_lam_im': out['v_lam_im'], 'v_log_dt': out['v_log_dt'], 'v_b_re': out['v_b_re'], 'v_b_im': out['v_b_im'], 'v_c_re': out['v_c_re'], 'v_c_im': out['v_c_im'], 'v_d_skip': out['v_d_skip'], 'v_w_glu': out['v_w_glu'], 'v_b_glu': out['v_b_glu'], 'v_q_a_norm': out['v_q_a_norm'], 'v_w_q_b': out['v_w_q_b'], 'v_kv_a_norm': out['v_kv_a_norm'], 'v_w_kv_b': out['v_w_kv_b'], 'v_out_norm_ssm': out['v_out_norm_ssm'], 'v_out_norm_attn': out['v_out_norm_attn'], 'v_w_out': out['v_w_out'], 'v_ffn_norm': out['v_ffn_norm'], 'v_w_up': out['v_w_up'], 'v_conv_w': out['v_conv_w'], 'v_conv_b': out['v_conv_b'], 'v_w_down': out['v_w_down'], 'v_final_norm': out['v_final_norm']}


def _loss(weights, diff, rest, loss_target):
    with _jax.named_scope("forward"):
        args = {**rest, TWIN_DIFF_INPUT: diff, **{k: w.astype(_WEIGHT_DTYPES[k]) for k, w in weights.items()}}
        y = _forward(args)
    with _jax.named_scope("loss_head"):
        err = _jnp.square(y.astype(_jnp.float32) - loss_target)
        return 0.5 * _jnp.sum(_jnp.mean(err, axis=-1)) if err.ndim else 0.5 * err


def _adamw(w, g, m, v):
    m = ADAM_B1 * m + (1.0 - ADAM_B1) * g
    v = ADAM_B2 * v + (1.0 - ADAM_B2) * _jnp.square(g)
    m_hat = m / (1.0 - ADAM_B1 ** ADAM_STEP)
    v_hat = v / (1.0 - ADAM_B2 ** ADAM_STEP)
    delta = -ADAM_LR * (m_hat / (_jnp.sqrt(v_hat) + ADAM_EPS) + ADAM_WD * w)
    return delta, m, v


def reference(x, meta_tokens, mix_norm, w_in, lam_re, lam_im, log_dt, b_re, b_im, c_re, c_im, d_skip, w_glu, b_glu, q_a_norm, w_q_b, kv_a_norm, w_kv_b, out_norm_ssm, out_norm_attn, w_out, ffn_norm, w_up, conv_w, conv_b, w_down, final_norm, loss_target, m_meta_tokens, m_mix_norm, m_w_in, m_lam_re, m_lam_im, m_log_dt, m_b_re, m_b_im, m_c_re, m_c_im, m_d_skip, m_w_glu, m_b_glu, m_q_a_norm, m_w_q_b, m_kv_a_norm, m_w_kv_b, m_out_norm_ssm, m_out_norm_attn, m_w_out, m_ffn_norm, m_w_up, m_conv_w, m_conv_b, m_w_down, m_final_norm, v_meta_tokens, v_mix_norm, v_w_in, v_lam_re, v_lam_im, v_log_dt, v_b_re, v_b_im, v_c_re, v_c_im, v_d_skip, v_w_glu, v_b_glu, v_q_a_norm, v_w_q_b, v_kv_a_norm, v_w_kv_b, v_out_norm_ssm, v_out_norm_attn, v_w_out, v_ffn_norm, v_w_up, v_conv_w, v_conv_b, v_w_down, v_final_norm):
    given = dict(x=x, meta_tokens=meta_tokens, mix_norm=mix_norm, w_in=w_in, lam_re=lam_re, lam_im=lam_im, log_dt=log_dt, b_re=b_re, b_im=b_im, c_re=c_re, c_im=c_im, d_skip=d_skip, w_glu=w_glu, b_glu=b_glu, q_a_norm=q_a_norm, w_q_b=w_q_b, kv_a_norm=kv_a_norm, w_kv_b=w_kv_b, out_norm_ssm=out_norm_ssm, out_norm_attn=out_norm_attn, w_out=w_out, ffn_norm=ffn_norm, w_up=w_up, conv_w=conv_w, conv_b=conv_b, w_down=w_down, final_norm=final_norm, loss_target=loss_target, m_meta_tokens=m_meta_tokens, m_mix_norm=m_mix_norm, m_w_in=m_w_in, m_lam_re=m_lam_re, m_lam_im=m_lam_im, m_log_dt=m_log_dt, m_b_re=m_b_re, m_b_im=m_b_im, m_c_re=m_c_re, m_c_im=m_c_im, m_d_skip=m_d_skip, m_w_glu=m_w_glu, m_b_glu=m_b_glu, m_q_a_norm=m_q_a_norm, m_w_q_b=m_w_q_b, m_kv_a_norm=m_kv_a_norm, m_w_kv_b=m_w_kv_b, m_out_norm_ssm=m_out_norm_ssm, m_out_norm_attn=m_out_norm_attn, m_w_out=m_w_out, m_ffn_norm=m_ffn_norm, m_w_up=m_w_up, m_conv_w=m_conv_w, m_conv_b=m_conv_b, m_w_down=m_w_down, m_final_norm=m_final_norm, v_meta_tokens=v_meta_tokens, v_mix_norm=v_mix_norm, v_w_in=v_w_in, v_lam_re=v_lam_re, v_lam_im=v_lam_im, v_log_dt=v_log_dt, v_b_re=v_b_re, v_b_im=v_b_im, v_c_re=v_c_re, v_c_im=v_c_im, v_d_skip=v_d_skip, v_w_glu=v_w_glu, v_b_glu=v_b_glu, v_q_a_norm=v_q_a_norm, v_w_q_b=v_w_q_b, v_kv_a_norm=v_kv_a_norm, v_w_kv_b=v_w_kv_b, v_out_norm_ssm=v_out_norm_ssm, v_out_norm_attn=v_out_norm_attn, v_w_out=v_w_out, v_ffn_norm=v_ffn_norm, v_w_up=v_w_up, v_conv_w=v_conv_w, v_conv_b=v_conv_b, v_w_down=v_w_down, v_final_norm=v_final_norm)
    weights = {n: given[n] for n in TWIN_WEIGHTS}
    shared = {n: given[n] for n in SHARED_INPUTS}
    per_example = {n: given[n] for n in ['x']}
    grad_fn = _jax.value_and_grad(_loss, argnums=(0, 1))

    def one_microbatch(ex, loss_target):
        ex = dict(ex)
        diff = ex.pop(TWIN_DIFF_INPUT)
        return grad_fn(weights, diff, {**shared, **ex}, loss_target)

    if N_MICROBATCH == 1:
        loss, (grad_w, grad_x) = one_microbatch(per_example, given["loss_target"])
    else:
        def body(carry, xs):
            loss_sum, grad_sum = carry
            l_k, (gw_k, gx_k) = one_microbatch(xs[0], xs[1])
            with _jax.named_scope("update"):
                return (loss_sum + l_k, _jax.tree.map(_jnp.add, grad_sum, gw_k)), gx_k

        init = (_jnp.zeros((), _jnp.float32), _jax.tree.map(_jnp.zeros_like, weights))
        (loss, grad_w), grad_x = _jax.lax.scan(body, init, (per_example, given["loss_target"]))
    with _jax.named_scope("update"):
        delta_w, new_m, new_v = {}, {}, {}
        for n in TWIN_WEIGHTS:
            delta_w[n], new_m[n], new_v[n] = _adamw(weights[n], grad_w[n], given["m_" + n], given["v_" + n])
    return (loss, grad_x, *[grad_w[n] for n in TWIN_WEIGHTS], *[delta_w[n] for n in TWIN_WEIGHTS],
            *[new_m[n] for n in TWIN_WEIGHTS], *[new_v[n] for n in TWIN_WEIGHTS])
```

```python
import functools
import math

import jax
import jax.numpy as jnp
from jax import lax
from jax.experimental import pallas as pl
from jax.experimental.pallas import tpu as pltpu

F32, BF16 = jnp.float32, jnp.bfloat16
HIGHEST = lax.Precision.HIGHEST

N_DEV = 8
N_META, SEQ, D = 16, 2048, 2048
T = N_META + SEQ
TR = 688
EPS = 1e-6
D_SSM, Q_LORA, KV_LORA, QK_ROPE = 1024, 512, 256, 64
D_IN = D_SSM + Q_LORA + KV_LORA + QK_ROPE
D_IN_PAD = 1920
N_HEADS, QK_NOPE, V_HEAD = 8, 128, 128
HEAD_PAD = 256
SM_SCALE = 1.0 / math.sqrt(QK_NOPE + QK_ROPE)
CHUNK = 64
N_GROUPS, SSM_GROUP, SSM_STATE = 64, 16, 64
N_SEG = 8
SEG = T // N_SEG
GQ = 8
D_FF = 5504
FF_PIECE = 1376
ROPE_BASE = 10000.0
LR, B1, B2, ADAM_EPS, WD, STEP = 0.001, 0.9, 0.999, 1e-08, 0.01, 10
VMEM_CAP = 60 * 1024 * 1024
MESH = pl.DeviceIdType.MESH

WEIGHTS = ['meta_tokens', 'mix_norm', 'w_in', 'lam_re', 'lam_im', 'log_dt', 'b_re', 'b_im', 'c_re', 'c_im',
           'd_skip', 'w_glu', 'b_glu', 'q_a_norm', 'w_q_b', 'kv_a_norm', 'w_kv_b', 'out_norm_ssm',
           'out_norm_attn', 'w_out', 'ffn_norm', 'w_up', 'conv_w', 'conv_b', 'w_down', 'final_norm']
BIG = ['w_in', 'w_glu', 'w_q_b', 'w_kv_b', 'w_out', 'w_up', 'w_down']
SMALL = [n for n in WEIGHTS if n not in BIG and n not in ('meta_tokens', 'conv_w')]


def _nbytes(shape, dtype):
    return math.prod(shape) * jnp.dtype(dtype).itemsize


def _params(sem, need):
    return pltpu.CompilerParams(dimension_semantics=("arbitrary",) * sem,
                                vmem_limit_bytes=int(min(VMEM_CAP, max(need, 16 * 1024 * 1024))))


_DIMS = {'nn': (((1,), (0,)), ((), ())), 'nt': (((1,), (1,)), ((), ())), 'tn': (((0,), (0,)), ((), ()))}


def _mm(name, a, b, mode, tm, tn, tk, out_dtype, stack=None, res=None):
    sa, sb, so = stack in ('a_out', 'ab_red'), stack in ('b_out', 'ab_red'), stack in ('a_out', 'b_out')
    nj = a.shape[0] if sa else (b.shape[0] if sb else 1)
    a2, b2 = a.shape[-2:], b.shape[-2:]
    if mode == 'nn':
        (m, k), (k2, n) = a2, b2
    elif mode == 'nt':
        (m, k), (n, k2) = a2, b2
    else:
        (k, m), (k2, n) = a2, b2
    assert k == k2 and m % tm == 0 and n % tn == 0 and k % tk == 0, (name, a.shape, b.shape)
    n_jo, n_jr, n_k = (nj if so else 1), (nj if stack == 'ab_red' else 1), k // tk
    grid = (n_jo, m // tm, n // tn, n_jr, n_k)
    a_blk, a_idx = ((tk, tm), lambda i, j, kk: (kk, i)) if mode == 'tn' else ((tm, tk), lambda i, j, kk: (i, kk))
    b_blk, b_idx = ((tn, tk), lambda i, j, kk: (j, kk)) if mode == 'nt' else ((tk, tn), lambda i, j, kk: (kk, j))

    def spec(blk, idx, stacked, on_out):
        if not stacked:
            return pl.BlockSpec(blk, lambda jo, i, j, jr, kk: idx(i, j, kk))
        return pl.BlockSpec((None,) + blk, lambda jo, i, j, jr, kk: ((jo if on_out else jr),) + idx(i, j, kk))

    o_idx = lambda i, j, kk: (i, j)
    in_specs = [spec(a_blk, a_idx, sa, stack == 'a_out'), spec(b_blk, b_idx, sb, stack == 'b_out')]
    operands = [a, b]
    if res is not None:
        in_specs.append(spec((tm, tn), o_idx, False, False))
        operands.append(res)
    n_red = n_jr * n_k
    dims = _DIMS[mode]

    def body(*refs):
        a_ref, b_ref = refs[0], refs[1]
        res_ref = refs[2] if res is not None else None
        o_ref = refs[3] if res is not None else refs[2]
        part = lax.dot_general(a_ref[...].astype(BF16), b_ref[...].astype(BF16), dims,
                               preferred_element_type=F32)

        def finish(total):
            if res_ref is not None:
                total = total + res_ref[...]
            o_ref[...] = total.astype(o_ref.dtype)

        if n_red == 1:
            finish(part)
        else:
            acc_ref = refs[-1]
            step = pl.program_id(3) * n_k + pl.program_id(4)

            @pl.when(step == 0)
            def _():
                acc_ref[...] = part

            @pl.when(step > 0)
            def _():
                acc_ref[...] += part

            @pl.when(step == n_red - 1)
            def _():
                finish(acc_ref[...])

    out_shape = ((nj,) if so else ()) + (m, n)
    need = 2 * (_nbytes(a_blk, a.dtype) + _nbytes(b_blk, b.dtype) + _nbytes((tm, tn), out_dtype)
                + (_nbytes((tm, tn), F32) if res is not None else 0)) + 3 * _nbytes((tm, tn), F32) + (4 << 20)
    return pl.pallas_call(
        body, name=name, grid=grid, in_specs=in_specs, out_specs=spec((tm, tn), o_idx, so, True),
        out_shape=jax.ShapeDtypeStruct(out_shape, out_dtype),
        scratch_shapes=[pltpu.VMEM((tm, tn), F32)] if n_red > 1 else [],
        compiler_params=_params(5, need),
    )(*operands)


def _rowwise(name, fn, row_ins, const_ins, row_outs, acc_outs, tr=TR):
    n_ri, n_ci, n_ro, n_ao = len(row_ins), len(const_ins), len(row_outs), len(acc_outs)

    def body(*refs):
        ri, ci = refs[:n_ri], refs[n_ri:n_ri + n_ci]
        ro, ao = refs[n_ri + n_ci:n_ri + n_ci + n_ro], refs[n_ri + n_ci + n_ro:]
        i = pl.program_id(0)
        outs = fn(i * tr, *[r[...] for r in ri], *[r[...] for r in ci])
        for r, v in zip(ro, outs[:n_ro]):
            r[...] = v.astype(r.dtype)
        if n_ao:
            @pl.when(i == 0)
            def _():
                for r, v in zip(ao, outs[n_ro:]):
                    r[...] = v

            @pl.when(i > 0)
            def _():
                for r, v in zip(ao, outs[n_ro:]):
                    r[...] += v

    in_specs = [pl.BlockSpec((tr, w), functools.partial(lambda cb, i: (i, cb), cb)) for _, w, cb in row_ins]
    in_specs += [pl.BlockSpec(c.shape, functools.partial(lambda nd, i: (0,) * nd, c.ndim)) for c in const_ins]
    out_specs = [pl.BlockSpec((tr, w), lambda i: (i, 0)) for w, _ in row_outs]
    out_specs += [pl.BlockSpec((1, w), lambda i: (0, 0)) for w in acc_outs]
    out_shape = [jax.ShapeDtypeStruct((T, w), dt) for w, dt in row_outs]
    out_shape += [jax.ShapeDtypeStruct((1, w), F32) for w in acc_outs]
    need = 2 * (sum(_nbytes((tr, w), a.dtype) for a, w, _ in row_ins) + sum(_nbytes((tr, w), dt) for w, dt in row_outs))
    need = 3 * need + (8 << 20)
    return pl.pallas_call(
        body, name=name, grid=(T // tr,), in_specs=in_specs, out_specs=out_specs, out_shape=out_shape,
        compiler_params=_params(1, need),
    )(*[a for a, _, _ in row_ins], *const_ins)


def _rms_fwd_fn(row0, h, g):
    h = h.astype(F32)
    r = lax.rsqrt(jnp.mean(h * h, axis=-1, keepdims=True) + EPS)
    return (h * r) * g, r


def _rms_bwd(dxn, h, r, g):
    xhat = h * r
    dxg = dxn * g
    dh = r * (dxg - xhat * jnp.mean(dxg * xhat, axis=-1, keepdims=True))
    return dh, jnp.sum(dxn * xhat, axis=0, keepdims=True)


def _rms_bwd_fn(row0, dxn, h, r, g):
    return _rms_bwd(dxn.astype(F32), h.astype(F32), r, g)


def _rms_bwd_res_fn(row0, dxn, h, r, res, g):
    dh, dg = _rms_bwd(dxn.astype(F32), h.astype(F32), r, g)
    return dh + res, dg


def _gelu_fn(row0, y):
    return (jax.nn.gelu(y),)


def _glu_fwd_fn(row0, y, gl, b, gain):
    ya = jax.nn.gelu(y) * jax.nn.sigmoid(gl + b)
    return _rms_fwd_fn(row0, ya, gain)


def _glu_bwd_fn(row0, dyan, y, gl, ra, b, gain):
    g = jax.nn.gelu(y)
    s = jax.nn.sigmoid(gl + b)
    dya, dgain = _rms_bwd(dyan, g * s, ra, gain)
    dgl = dya * g * s * (1.0 - s)
    return dgl, dya * s, dgain, jnp.sum(dgl, axis=0, keepdims=True)


def _gelu_bwd_fn(row0, dg, y):
    return (jax.vjp(jax.nn.gelu, y)[1](dg)[0],)


def _loss_fn(row0, h, tgt, gain):
    r = lax.rsqrt(jnp.mean(h * h, axis=-1, keepdims=True) + EPS)
    xhat = h * r
    rows = row0 + lax.broadcasted_iota(jnp.int32, h.shape, 0)
    err = jnp.where(rows >= N_META, xhat * gain - tgt, 0.0)
    loss = jnp.full((1, 128), jnp.sum(err * err) * (0.5 / D), F32)
    dh, dgain = _rms_bwd(err * (1.0 / D), h, r, gain)
    return dh, dgain, loss


def _rope_tables():
    pos = jnp.arange(T, dtype=jnp.int32)
    inv_freq = 1.0 / (ROPE_BASE ** (jnp.arange(0, QK_ROPE, 2, dtype=F32) / QK_ROPE))
    ang = pos.astype(F32)[:, None] * inv_freq[None, :]
    cos, sin, z32, z64 = jnp.cos(ang), jnp.sin(ang), jnp.zeros((T, 32), F32), jnp.zeros((T, 64), F32)
    return (jnp.concatenate([cos, cos, z64], 1), jnp.concatenate([-sin, z32, z64], 1),
            jnp.concatenate([z32, sin, z64], 1))


def _rot(x, cc, s1, s2):
    return x * cc + pltpu.roll(x, 96, 1) * s1 + pltpu.roll(x, 32, 1) * s2


def _derot(d, cc, s1, s2):
    return d * cc + pltpu.roll(d * s1, 32, 1) + pltpu.roll(d * s2, 96, 1)


def _chunk_of(pos):
    return jnp.where(pos < N_META, 0, 1 + ((pos - N_META) >> 6))


def _attn_probs(i, q_ref, kv_ref, kpe_ref, tq, tk):
    q = q_ref[...]
    qn = q[:, :QK_NOPE].astype(BF16)
    qp = _rot(q[:, QK_NOPE:], tq[0][...], tq[1][...], tq[2][...]).astype(BF16)
    kn = kv_ref[:, :QK_NOPE].astype(BF16)
    v = kv_ref[:, QK_NOPE:].astype(BF16)
    kp = _rot(kpe_ref[...], tk[0][...], tk[1][...], tk[2][...]).astype(BF16)
    s = lax.dot_general(qn, kn, _DIMS['nt'], preferred_element_type=F32)
    s = s + lax.dot_general(qp, kp, _DIMS['nt'], preferred_element_type=F32)
    qc = _chunk_of(i * TR + lax.broadcasted_iota(jnp.int32, (TR, 1), 0))
    kc = _chunk_of(lax.broadcasted_iota(jnp.int32, (1, T), 1))
    s = jnp.where(kc <= qc, s * SM_SCALE, jnp.finfo(F32).min)
    p = jnp.exp(s - jnp.max(s, axis=-1, keepdims=True))
    p = p / jnp.sum(p, axis=-1, keepdims=True)
    return qn, qp, kn, kp, v, p


def _attn_specs(z, tabs):
    q_spec = pl.BlockSpec((None, TR, HEAD_PAD), lambda h, i: (h, i, 0))
    kv_spec = pl.BlockSpec((None, T, HEAD_PAD), lambda h, i: (h, 0, 0))
    kpe_spec = pl.BlockSpec((T, 128), lambda h, i: (0, (D_IN - QK_ROPE) // 128))
    tq = [pl.BlockSpec((TR, 128), lambda h, i: (i, 0))] * 3
    tk = [pl.BlockSpec((T, 128), lambda h, i: (0, 0))] * 3
    return [q_spec, kv_spec, kpe_spec] + tq + tk


def _attn_fwd(q, kv, z, tabs):
    def body(q_ref, kv_ref, kpe_ref, c1, c2, c3, k1, k2, k3, o_ref):
        _, _, _, _, v, p = _attn_probs(pl.program_id(1), q_ref, kv_ref, kpe_ref, (c1, c2, c3), (k1, k2, k3))
        o_ref[...] = jnp.dot(p.astype(BF16), v, preferred_element_type=F32)

    return pl.pallas_call(
        body, name="attn_fwd", grid=(N_HEADS, T // TR), in_specs=_attn_specs(z, tabs),
        out_specs=pl.BlockSpec((TR, V_HEAD), lambda h, i: (i, h)),
        out_shape=jax.ShapeDtypeStruct((T, N_HEADS * V_HEAD), F32),
        compiler_params=_params(2, 48 << 20),
    )(q, kv, z, *tabs, *tabs)


def _attn_bwd(q, kv, z, tabs, dyb):
    def body(q_ref, kv_ref, kpe_ref, c1, c2, c3, k1, k2, k3, do_ref, dq_ref, dkv_ref, dkpe_ref):
        i = pl.program_id(1)
        qn, qp, kn, kp, v, p = _attn_probs(i, q_ref, kv_ref, kpe_ref, (c1, c2, c3), (k1, k2, k3))
        do = do_ref[...].astype(BF16)
        dp = lax.dot_general(do, v, _DIMS['nt'], preferred_element_type=F32)
        ds = (p * (dp - jnp.sum(p * dp, axis=-1, keepdims=True)) * SM_SCALE).astype(BF16)
        pb = p.astype(BF16)
        dq_ref[:, :QK_NOPE] = jnp.dot(ds, kn, preferred_element_type=F32).astype(dq_ref.dtype)
        dqp = _derot(jnp.dot(ds, kp, preferred_element_type=F32), c1[...], c2[...], c3[...])
        dq_ref[:, QK_NOPE:] = dqp.astype(dq_ref.dtype)
        dkn = lax.dot_general(ds, qn, _DIMS['tn'], preferred_element_type=F32)
        dv = lax.dot_general(pb, do, _DIMS['tn'], preferred_element_type=F32)
        dkp = lax.dot_general(ds, qp, _DIMS['tn'], preferred_element_type=F32)

        @pl.when(i == 0)
        def _():
            dkv_ref[:, :QK_NOPE] = dkn
            dkv_ref[:, QK_NOPE:] = dv
            dkpe_ref[...] = dkp

        @pl.when(i > 0)
        def _():
            dkv_ref[:, :QK_NOPE] += dkn
            dkv_ref[:, QK_NOPE:] += dv
            dkpe_ref[...] += dkp

    return pl.pallas_call(
        body, name="attn_bwd", grid=(N_HEADS, T // TR),
        in_specs=_attn_specs(z, tabs) + [pl.BlockSpec((TR, V_HEAD), lambda h, i: (i, h))],
        out_specs=[pl.BlockSpec((None, TR, HEAD_PAD), lambda h, i: (h, i, 0)),
                   pl.BlockSpec((None, T, HEAD_PAD), lambda h, i: (h, 0, 0)),
                   pl.BlockSpec((None, T, 128), lambda h, i: (h, 0, 0))],
        out_shape=[jax.ShapeDtypeStruct((N_HEADS, T, HEAD_PAD), BF16),
                   jax.ShapeDtypeStruct((N_HEADS, T, HEAD_PAD), F32),
                   jax.ShapeDtypeStruct((N_HEADS, T, 128), F32)],
        compiler_params=_params(2, 56 << 20),
    )(q, kv, z, *tabs, *tabs, dyb)


def _kpe_bwd(dkpe, tabs):
    def fn(row0, *a):
        d = a[0]
        for h in range(1, N_HEADS):
            d = d + a[h]
        return (_derot(d, a[N_HEADS], a[N_HEADS + 1], a[N_HEADS + 2]),)

    flat = dkpe.transpose(1, 0, 2).reshape(T, N_HEADS * 128)
    rows = [(flat, 128, h) for h in range(N_HEADS)] + [(t, 128, 0) for t in tabs]
    return _rowwise("kpe_bwd", fn, rows, [], [(128, BF16)], [])[0]


def _s5_prep_math(lam_re, lam_im, log_dt, bt_re, bt_im):
    dt = jnp.exp(log_dt)
    mag = jnp.exp(lam_re * dt)
    a_re, a_im = mag * jnp.cos(lam_im * dt), mag * jnp.sin(lam_im * dt)
    den = lam_re * lam_re + lam_im * lam_im
    w_re = ((a_re - 1.0) * lam_re + a_im * lam_im) / den
    w_im = (a_im * lam_re - (a_re - 1.0) * lam_im) / den
    return a_re, a_im, w_re[None] * bt_re - w_im[None] * bt_im, w_re[None] * bt_im + w_im[None] * bt_re


def _s5_prep(lam_re, lam_im, log_dt, bt_re, bt_im):
    def body(lr, li, ld, br, bi, ar, ai, bbr, bbi):
        ar[...], ai[...], bbr[...], bbi[...] = _s5_prep_math(lr[...], li[...], ld[...], br[...], bi[...])

    gp, cgp = jax.ShapeDtypeStruct((N_GROUPS, SSM_STATE), F32), jax.ShapeDtypeStruct(bt_re.shape, F32)
    return pl.pallas_call(body, name="s5_prep", out_shape=[gp, gp, cgp, cgp])(lam_re, lam_im, log_dt, bt_re, bt_im)


def _s5_prep_bwd(lam_re, lam_im, log_dt, bt_re, bt_im, da_re, da_im, dbb_re, dbb_im):
    def body(lr, li, ld, br, bi, dar, dai, dbr, dbi, o1, o2, o3, o4, o5):
        _, vjp = jax.vjp(_s5_prep_math, lr[...], li[...], ld[...], br[...], bi[...])
        o1[...], o2[...], o3[...], o4[...], o5[...] = vjp((dar[...], dai[...], dbr[...], dbi[...]))

    ins = (lam_re, lam_im, log_dt, bt_re, bt_im)
    return pl.pallas_call(body, name="s5_prep_bwd", out_shape=[jax.ShapeDtypeStruct(a.shape, F32) for a in ins])(
        *ins, da_re, da_im, dbb_re, dbb_im)


def _cmul(ar, ai, br, bi):
    return ar * br - ai * bi, ar * bi + ai * br


def _seg_rows(i):
    return pl.ds(pl.multiple_of(i * N_SEG, N_SEG), N_SEG)


def _scan(xr, xi, ar, ai, reverse):
    zero = jnp.zeros(ar.shape, F32)

    def local(j, carry):
        rows = _seg_rows(SEG - 1 - j if reverse else j)
        nr, ni = _cmul(ar, ai, *carry)
        nr, ni = nr + xr[rows, :], ni + xi[rows, :]
        xr[rows, :], xi[rows, :] = nr, ni
        return nr, ni

    er, ei = lax.fori_loop(0, SEG, local, (zero, zero))
    pr, pi = ar, ai
    for _ in range(8):
        pr, pi = _cmul(pr, pi, pr, pi)
    pr, pi = _cmul(*_cmul(pr, pi, ar, ai), ar, ai)
    row = lax.broadcasted_iota(jnp.int32, ar.shape, 0)
    edge, shift = (N_SEG - 1, N_SEG - 1) if reverse else (0, 1)
    hr, hi = zero, zero
    for _ in range(N_SEG - 1):
        tr_, ti_ = _cmul(pr, pi, hr, hi)
        hr = jnp.where(row == edge, 0.0, pltpu.roll(tr_ + er, shift, 0))
        hi = jnp.where(row == edge, 0.0, pltpu.roll(ti_ + ei, shift, 0))

    def fix(j, carry):
        rows = _seg_rows(SEG - 1 - j if reverse else j)
        xr[rows, :] += carry[0]
        xi[rows, :] += carry[1]
        return _cmul(ar, ai, *carry)

    lax.fori_loop(0, SEG, fix, _cmul(ar, ai, hr, hi))
    return hr, hi


def _s5_hidden(u, a_ref, bre_ref, bim_ref, hre, him):
    hre[...] = jnp.dot(u, bre_ref[...], precision=HIGHEST, preferred_element_type=F32)
    him[...] = jnp.dot(u, bim_ref[...], precision=HIGHEST, preferred_element_type=F32)
    return _scan(hre, him, a_ref[0], a_ref[1], reverse=False)


def _s5_specs():
    col = pl.BlockSpec((T, 128), lambda q: (0, q))
    a_spec = pl.BlockSpec((None, 2, N_SEG, GQ * SSM_STATE), lambda q: (q, 0, 0, 0))
    wide = pl.BlockSpec((None, 128, GQ * SSM_STATE), lambda q: (q, 0, 0))
    tall = pl.BlockSpec((None, GQ * SSM_STATE, 128), lambda q: (q, 0, 0))
    d_spec = pl.BlockSpec((1, 128), lambda q: (0, q))
    return col, a_spec, wide, tall, d_spec


def _s5_fwd(u, a8, bre, bim, cre, cim, dsk):
    def body(u_ref, a_ref, bre_ref, bim_ref, cre_ref, cim_ref, d_ref, y_ref, hre, him):
        u_ = u_ref[...]
        _s5_hidden(u_, a_ref, bre_ref, bim_ref, hre, him)
        y_ref[...] = (jnp.dot(hre[...], cre_ref[...], precision=HIGHEST, preferred_element_type=F32)
                      - jnp.dot(him[...], cim_ref[...], precision=HIGHEST, preferred_element_type=F32)
                      + d_ref[...] * u_)

    col, a_spec, wide, tall, d_spec = _s5_specs()
    return pl.pallas_call(
        body, name="s5_fwd", grid=(N_GROUPS // GQ,), in_specs=[col, a_spec, wide, wide, tall, tall, d_spec],
        out_specs=col, out_shape=jax.ShapeDtypeStruct((T, D_SSM), F32),
        scratch_shapes=[pltpu.VMEM((T, GQ * SSM_STATE), F32)] * 2,
        compiler_params=_params(1, 40 << 20),
    )(u, a8, bre, bim, cre, cim, dsk)


def _s5_bwd(u, dy, a8, bre, bim, bre_t, bim_t, cre_t, cim_t, dsk):
    def body(u_ref, dy_ref, a_ref, bre_ref, bim_ref, bret_ref, bimt_ref, cret_ref, cimt_ref, d_ref,
             du_ref, dbre_ref, dbim_ref, dcre_ref, dcim_ref, da_ref, dd_ref, hre, him, lre, lim):
        u_, dy_ = u_ref[...], dy_ref[...]
        ar, ai = a_ref[0], a_ref[1]
        h0r, h0i = _s5_hidden(u_, a_ref, bre_ref, bim_ref, hre, him)
        lre[...] = jnp.dot(dy_, cret_ref[...], precision=HIGHEST, preferred_element_type=F32)
        lim[...] = -jnp.dot(dy_, cimt_ref[...], precision=HIGHEST, preferred_element_type=F32)
        _scan(lre, lim, ar, -ai, reverse=True)

        def acc_da(i, carry):
            lr, li = lre[_seg_rows(i), :], lim[_seg_rows(i), :]
            pr, pi = hre[_seg_rows(i - 1), :], him[_seg_rows(i - 1), :]
            return carry[0] + lr * pr + li * pi, carry[1] + li * pr - lr * pi

        lr, li = lre[_seg_rows(0), :], lim[_seg_rows(0), :]
        dar, dai = lax.fori_loop(1, SEG, acc_da, (lr * h0r + li * h0i, li * h0r - lr * h0i))
        da_ref[0:1, :] = jnp.sum(dar, axis=0, keepdims=True)
        da_ref[1:2, :] = jnp.sum(dai, axis=0, keepdims=True)
        lr_, li_ = lre[...], lim[...]
        du_ref[...] = (dy_ * d_ref[...]
                       + jnp.dot(lr_, bret_ref[...], precision=HIGHEST, preferred_element_type=F32)
                       + jnp.dot(li_, bimt_ref[...], precision=HIGHEST, preferred_element_type=F32))
        dbre_ref[...] = lax.dot_general(u_, lr_, _DIMS['tn'], precision=HIGHEST, preferred_element_type=F32)
        dbim_ref[...] = lax.dot_general(u_, li_, _DIMS['tn'], precision=HIGHEST, preferred_element_type=F32)
        dcre_ref[...] = lax.dot_general(hre[...], dy_, _DIMS['tn'], precision=HIGHEST, preferred_element_type=F32)
        dcim_ref[...] = -lax.dot_general(him[...], dy_, _DIMS['tn'], precision=HIGHEST, preferred_element_type=F32)
        dd_ref[...] = jnp.sum(dy_ * u_, axis=0, keepdims=True)

    col, a_spec, wide, tall, d_spec = _s5_specs()
    nq, ns = N_GROUPS // GQ, GQ * SSM_STATE
    return pl.pallas_call(
        body, name="s5_bwd", grid=(nq,),
        in_specs=[col, col, a_spec, wide, wide, tall, tall, wide, wide, d_spec],
        out_specs=[col, wide, wide, tall, tall, pl.BlockSpec((None, 2, ns), lambda q: (q, 0, 0)), d_spec],
        out_shape=[jax.ShapeDtypeStruct((T, D_SSM), F32),
                   jax.ShapeDtypeStruct((nq, 128, ns), F32), jax.ShapeDtypeStruct((nq, 128, ns), F32),
                   jax.ShapeDtypeStruct((nq, ns, 128), F32), jax.ShapeDtypeStruct((nq, ns, 128), F32),
                   jax.ShapeDtypeStruct((nq, 2, ns), F32), jax.ShapeDtypeStruct((1, D_SSM), F32)],
        scratch_shapes=[pltpu.VMEM((T, ns), F32)] * 4,
        compiler_params=_params(1, 56 << 20),
    )(u, dy, a8, bre, bim, bre_t, bim_t, cre_t, cim_t, dsk)


def _to_segments(a):
    return a.reshape(N_SEG, SEG, a.shape[-1]).transpose(1, 0, 2).reshape(T, a.shape[-1])


def _from_segments(a):
    return a.reshape(SEG, N_SEG, a.shape[-1]).transpose(1, 0, 2).reshape(T, a.shape[-1])


def _block_diag_b(bbt):
    nq = N_GROUPS // GQ
    x = bbt.transpose(1, 0, 2).reshape(nq, GQ, SSM_GROUP, 1, SSM_STATE)
    eye = jnp.eye(GQ, dtype=F32)[None, :, None, :, None]
    return (x * eye).reshape(nq, GQ * SSM_GROUP, GQ * SSM_STATE)


def _block_diag_c(c):
    nq = N_GROUPS // GQ
    x = c.reshape(nq, GQ, SSM_GROUP, SSM_STATE).transpose(0, 1, 3, 2)[:, :, :, None, :]
    eye = jnp.eye(GQ, dtype=F32)[None, :, None, :, None]
    return (x * eye).reshape(nq, GQ * SSM_STATE, GQ * SSM_GROUP)


def _ff_specs():
    up_spec = pl.BlockSpec((2, None, T, 128), lambda p, cb: (0, p, 0, cb))
    piece = pl.BlockSpec((None, T, 128), lambda p, cb: (p, 0, cb))
    cw_spec = pl.BlockSpec((None, 3, 128), lambda p, cb: (p, 0, cb))
    cb_spec = pl.BlockSpec((None, 1, 128), lambda p, cb: (p, 0, cb))
    return up_spec, piece, cw_spec, cb_spec


def _conv_gate(gate, w, cb, pad_ref):
    pad_ref[0:8, :] = jnp.zeros((8, 128), F32)
    pad_ref[8:8 + T, :] = gate
    g1, g2 = pad_ref[pl.ds(7, T), :], pad_ref[pl.ds(6, T), :]
    return w[0:1, :] * g2 + w[1:2, :] * g1 + w[2:3, :] * gate + cb, g1, g2


def _ff_act(up, cw4, cb4):
    def body(up_ref, w_ref, b_ref, o_ref, pad_ref):
        gc, _, _ = _conv_gate(up_ref[0], w_ref[...], b_ref[...], pad_ref)
        o_ref[...] = (jax.nn.silu(gc) * up_ref[1]).astype(o_ref.dtype)

    up_spec, piece, cw_spec, cb_spec = _ff_specs()
    n_cb = pl.cdiv(FF_PIECE, 128)
    return pl.pallas_call(
        body, name="ff_act", grid=(4, n_cb), in_specs=[up_spec, cw_spec, cb_spec], out_specs=piece,
        out_shape=jax.ShapeDtypeStruct((4, T, FF_PIECE), BF16),
        scratch_shapes=[pltpu.VMEM((T + 8, 128), F32)],
        compiler_params=_params(2, 24 << 20),
    )(up.reshape(2, 4, T, FF_PIECE), cw4, cb4)


def _ff_act_bwd(up, dact, cw4, cb4):
    def body(up_ref, da_ref, w_ref, b_ref, dup_ref, dw_ref, db_ref, pad_ref, pad2_ref):
        gate, val, w = up_ref[0], up_ref[1], w_ref[...]
        gc, g1, g2 = _conv_gate(gate, w, b_ref[...], pad_ref)
        sg = jax.nn.sigmoid(gc)
        da = da_ref[...]
        dup_ref[1] = (da * gc * sg).astype(dup_ref.dtype)
        dgc = da * val * sg * (1.0 + gc * (1.0 - sg))
        db_ref[...] = jnp.sum(dgc, axis=0, keepdims=True)
        dw_ref[0:1, :] = jnp.sum(dgc * g2, axis=0, keepdims=True)
        dw_ref[1:2, :] = jnp.sum(dgc * g1, axis=0, keepdims=True)
        dw_ref[2:3, :] = jnp.sum(dgc * gate, axis=0, keepdims=True)
        pad2_ref[0:T, :] = dgc
        pad2_ref[T:T + 8, :] = jnp.zeros((8, 128), F32)
        d1, d2 = pad2_ref[pl.ds(1, T), :], pad2_ref[pl.ds(2, T), :]
        dup_ref[0] = (w[2:3, :] * dgc + w[1:2, :] * d1 + w[0:1, :] * d2).astype(dup_ref.dtype)

    up_spec, piece, cw_spec, cb_spec = _ff_specs()
    n_cb = pl.cdiv(FF_PIECE, 128)
    dup, dw, db = pl.pallas_call(
        body, name="ff_act_bwd", grid=(4, n_cb), in_specs=[up_spec, piece, cw_spec, cb_spec],
        out_specs=[up_spec, cw_spec, cb_spec],
        out_shape=[jax.ShapeDtypeStruct((2, 4, T, FF_PIECE), BF16), jax.ShapeDtypeStruct((4, 3, FF_PIECE), F32),
                   jax.ShapeDtypeStruct((4, 1, FF_PIECE), F32)],
        scratch_shapes=[pltpu.VMEM((T + 8, 128), F32)] * 2,
        compiler_params=_params(2, 32 << 20),
    )(up.reshape(2, 4, T, FF_PIECE), dact, cw4, cb4)
    return dup.reshape(8, T, FF_PIECE), dw, db


def _local_step(x, tgt, meta, p, wf):
    sm = {}
    h0 = jnp.concatenate([meta, x], axis=0)
    tgt_pad = jnp.concatenate([jnp.zeros((N_META, D), F32), tgt], axis=0)
    tabs = _rope_tables()
    row = lambda a: a.reshape(1, -1)

    xn, r0 = _rowwise("rms_mix", _rms_fwd_fn, [(h0, D, 0)], [row(p['mix_norm'])], [(D, BF16), (1, F32)], [])
    z = _mm("in_proj", xn, wf['w_in'], 'nn', TR, 640, D, F32)
    lam_re, lam_im = p['lam_re'].reshape(N_GROUPS, SSM_STATE), p['lam_im'].reshape(N_GROUPS, SSM_STATE)
    log_dt = p['log_dt'].reshape(N_GROUPS, 1)
    bt_re = p['b_re'].reshape(N_GROUPS, SSM_STATE, SSM_GROUP).transpose(2, 0, 1)
    bt_im = p['b_im'].reshape(N_GROUPS, SSM_STATE, SSM_GROUP).transpose(2, 0, 1)
    c_re = p['c_re'].reshape(N_GROUPS, SSM_GROUP, SSM_STATE)
    c_im = p['c_im'].reshape(N_GROUPS, SSM_GROUP, SSM_STATE)
    a_re, a_im, bbt_re, bbt_im = _s5_prep(lam_re, lam_im, log_dt, bt_re, bt_im)
    nq, ns = N_GROUPS // GQ, GQ * SSM_STATE
    a8 = jnp.broadcast_to(jnp.stack([a_re.reshape(nq, ns), a_im.reshape(nq, ns)], 1)[:, :, None, :],
                          (nq, 2, N_SEG, ns))
    bre, bim = _block_diag_b(bbt_re), _block_diag_b(bbt_im)
    cre, cim = _block_diag_c(c_re), _block_diag_c(c_im)
    dsk = row(p['d_skip'])
    u_seg = _to_segments(z[:, :D_SSM])
    y_ssm = _from_segments(_s5_fwd(u_seg, a8, bre, bim, cre, cim, dsk))
    g, = _rowwise("gelu", _gelu_fn, [(y_ssm, D_SSM, 0)], [], [(D_SSM, BF16)], [])
    gl = _mm("glu_proj", g, wf['w_glu'], 'nn', TR, D_SSM, D_SSM, F32)
    yan, ra = _rowwise("glu_norm", _glu_fwd_fn, [(y_ssm, D_SSM, 0), (gl, D_SSM, 0)],
                       [row(p['b_glu']), row(p['out_norm_ssm'])], [(D_SSM, BF16), (1, F32)], [])
    qn, rq = _rowwise("rms_q", _rms_fwd_fn, [(z, Q_LORA, D_SSM // Q_LORA)], [row(p['q_a_norm'])],
                      [(Q_LORA, BF16), (1, F32)], [])
    kvn, rkv = _rowwise("rms_kv", _rms_fwd_fn, [(z, KV_LORA, (D_SSM + Q_LORA) // KV_LORA)], [row(p['kv_a_norm'])],
                        [(KV_LORA, BF16), (1, F32)], [])
    q = _mm("q_proj", qn, wf['w_q_b'], 'nn', T, HEAD_PAD, Q_LORA, F32, stack='b_out')
    kv = _mm("kv_proj", kvn, wf['w_kv_b'], 'nn', T, HEAD_PAD, KV_LORA, F32, stack='b_out')
    yb = _attn_fwd(q, kv, z, tabs)
    ybn, rb = _rowwise("rms_attn", _rms_fwd_fn, [(yb, D_SSM, 0)], [row(p['out_norm_attn'])],
                       [(D_SSM, BF16), (1, F32)], [])
    y = jnp.concatenate([yan, ybn], axis=1)
    h1 = _mm("out_proj", y, wf['w_out'], 'nn', TR, 1024, D, F32, res=h0)
    xn2, r1 = _rowwise("rms_ffn", _rms_fwd_fn, [(h1, D, 0)], [row(p['ffn_norm'])], [(D, BF16), (1, F32)], [])
    up = _mm("up_proj", xn2, wf['w_up'], 'nn', TR, FF_PIECE, D, F32, stack='b_out')
    cw4 = wf['conv_w'].reshape(3, 4, FF_PIECE).transpose(1, 0, 2)
    cb4 = p['conv_b'].reshape(4, 1, FF_PIECE)
    act = _ff_act(up, cw4, cb4)
    wd4 = wf['w_down'].reshape(4, FF_PIECE, D)
    h2 = _mm("down_proj", act, wd4, 'nn', TR, 1024, FF_PIECE, F32, stack='ab_red', res=h1)
    dh2, sm['final_norm'], loss = _rowwise("loss", _loss_fn, [(h2, D, 0), (tgt_pad, D, 0)], [row(p['final_norm'])],
                                           [(D, F32)], [D, 128])

    big = {}
    dact = _mm("down_bwd_x", dh2, wd4, 'nt', TR, FF_PIECE, D, F32, stack='b_out')
    big['w_down'] = _mm("down_bwd_w", act, dh2, 'tn', FF_PIECE, 1024, TR, BF16, stack='a_out').reshape(
        N_DEV, D_FF // N_DEV, D)
    dup, dcw4, dcb4 = _ff_act_bwd(up, dact, cw4, cb4)
    dconv_w = dcw4.transpose(1, 0, 2).reshape(3, D_FF)
    sm['conv_b'] = dcb4.reshape(1, D_FF)
    dxn2 = _mm("up_bwd_x", dup, wf['w_up'], 'nt', TR, 1024, FF_PIECE, F32, stack='ab_red')
    big['w_up'] = _mm("up_bwd_w", xn2, dup, 'tn', 1024, FF_PIECE, TR, BF16, stack='b_out')
    dh1, sm['ffn_norm'] = _rowwise("rms_ffn_bwd", _rms_bwd_res_fn,
                                   [(dxn2, D, 0), (h1, D, 0), (r1, 1, 0), (dh2, D, 0)], [row(p['ffn_norm'])],
                                   [(D, F32)], [D])
    dy = _mm("out_bwd_x", dh1, wf['w_out'], 'nt', TR, 1024, D, F32)
    big['w_out'] = _mm("out_bwd_w", y, dh1, 'tn', 1024, 1024, TR, BF16).reshape(N_DEV, D // N_DEV, D)
    dgl, dgd, sm['out_norm_ssm'], sm['b_glu'] = _rowwise(
        "glu_bwd", _glu_bwd_fn, [(dy, D_SSM, 0), (y_ssm, D_SSM, 0), (gl, D_SSM, 0), (ra, 1, 0)],
        [row(p['b_glu']), row(p['out_norm_ssm'])], [(D_SSM, BF16), (D_SSM, F32)], [D_SSM, D_SSM])
    dg = _mm("glu_bwd_x", dgl, wf['w_glu'], 'nt', TR, D_SSM, D_SSM, F32, res=dgd)
    big['w_glu'] = _mm("glu_bwd_w", g, dgl, 'tn', D_SSM, D_SSM, TR, BF16).reshape(N_DEV, D_SSM // N_DEV, D_SSM)
    dy_ssm, = _rowwise("gelu_bwd", _gelu_bwd_fn, [(dg, D_SSM, 0), (y_ssm, D_SSM, 0)], [], [(D_SSM, F32)], [])
    tr3 = lambda a: a.transpose(0, 2, 1)
    du_seg, dbre, dbim, dcre, dcim, da, sm['d_skip'] = _s5_bwd(
        u_seg, _to_segments(dy_ssm), a8, bre, bim, tr3(bre), tr3(bim), tr3(cre), tr3(cim), dsk)
    du = _from_segments(du_seg).astype(BF16)
    diag_b = lambda d: jnp.einsum('qgcgp->qgcp', d.reshape(nq, GQ, SSM_GROUP, GQ, SSM_STATE)).transpose(
        2, 0, 1, 3).reshape(SSM_GROUP, N_GROUPS, SSM_STATE)
    diag_c = lambda d: jnp.einsum('qgpgc->qgcp', d.reshape(nq, GQ, SSM_STATE, GQ, SSM_GROUP)).reshape(
        N_GROUPS, SSM_GROUP, SSM_STATE)
    dlam_re, dlam_im, dlog_dt, dbt_re, dbt_im = _s5_prep_bwd(
        lam_re, lam_im, log_dt, bt_re, bt_im, da[:, 0, :].reshape(N_GROUPS, SSM_STATE),
        da[:, 1, :].reshape(N_GROUPS, SSM_STATE), diag_b(dbre), diag_b(dbim))
    sm['lam_re'], sm['lam_im'], sm['log_dt'] = dlam_re, dlam_im, dlog_dt
    sm['b_re'], sm['b_im'] = dbt_re.transpose(1, 2, 0), dbt_im.transpose(1, 2, 0)
    sm['c_re'], sm['c_im'] = diag_c(dcre), diag_c(dcim)
    dyb, sm['out_norm_attn'] = _rowwise("rms_attn_bwd", _rms_bwd_fn, [(dy, D_SSM, 1), (yb, D_SSM, 0), (rb, 1, 0)],
                                        [row(p['out_norm_attn'])], [(D_SSM, F32)], [D_SSM])
    dq, dkv, dkpe = _attn_bwd(q, kv, z, tabs, dyb)
    big['w_q_b'] = _mm("q_bwd_w", qn, dq, 'tn', Q_LORA, HEAD_PAD, T, BF16, stack='b_out')
    dqn = _mm("q_bwd_x", dq, wf['w_q_b'], 'nt', TR, Q_LORA, HEAD_PAD, F32, stack='ab_red')
    big['w_kv_b'] = _mm("kv_bwd_w", kvn, dkv, 'tn', KV_LORA, HEAD_PAD, T, BF16, stack='b_out')
    dkvn = _mm("kv_bwd_x", dkv, wf['w_kv_b'], 'nt', TR, KV_LORA, HEAD_PAD, F32, stack='ab_red')
    dq_a, sm['q_a_norm'] = _rowwise("rms_q_bwd", _rms_bwd_fn,
                                    [(dqn, Q_LORA, 0), (z, Q_LORA, D_SSM // Q_LORA), (rq, 1, 0)],
                                    [row(p['q_a_norm'])], [(Q_LORA, BF16)], [Q_LORA])
    dkv_a, sm['kv_a_norm'] = _rowwise("rms_kv_bwd", _rms_bwd_fn,
                                      [(dkvn, KV_LORA, 0), (z, KV_LORA, (D_SSM + Q_LORA) // KV_LORA), (rkv, 1, 0)],
                                      [row(p['kv_a_norm'])], [(KV_LORA, BF16)], [KV_LORA])
    dk_pe = _kpe_bwd(dkpe, tabs)
    dz = jnp.concatenate([du, dq_a, dkv_a, dk_pe], axis=1)
    dxn = _mm("in_bwd_x", dz, wf['w_in'], 'nt', TR, 1024, D_IN_PAD, F32)
    big['w_in'] = _mm("in_bwd_w", xn, dz, 'tn', 1024, 640, TR, BF16).reshape(N_DEV, D // N_DEV, D_IN_PAD)
    dh0, sm['mix_norm'] = _rowwise("rms_mix_bwd", _rms_bwd_res_fn,
                                   [(dxn, D, 0), (h0, D, 0), (r0, 1, 0), (dh1, D, 0)], [row(p['mix_norm'])],
                                   [(D, F32)], [D])
    return loss, dh0[N_META:], dh0[:N_META], dconv_w, big, sm


def _place():
    x, y, c = lax.axis_index("x"), lax.axis_index("y"), lax.axis_index("c")
    return x, y, c, [(1 - x, y), (x, 1 - y), (1 - x, 1 - y)]


_HBM = pl.BlockSpec(memory_space=pltpu.HBM)


def _all_gather(name, arrs):
    n = len(arrs)

    def body(*refs):
        ins, outs = refs[:n], refs[n:2 * n]
        send, recv, loc = refs[2 * n:]
        x, y, c, chips = _place()
        me, sib = (x, y, c), (x, y, 1 - c)

        def cp(a, k, blk, to, src=None):
            dst = outs[a].at[4 * blk[0] + 2 * blk[1] + blk[2]]
            return pltpu.make_async_remote_copy(src_ref=dst if src is None else src, dst_ref=dst,
                                                send_sem=send.at[a, k], recv_sem=recv.at[a, k],
                                                device_id=to, device_id_type=MESH)

        mine = [pltpu.make_async_copy(ins[a], outs[a].at[4 * x + 2 * y + c], loc.at[a]) for a in range(n)]
        for m in mine:
            m.start()
        first = []
        for a in range(n):
            first.append(cp(a, 0, me, sib, src=ins[a]))
            first += [cp(a, 1 + j, me, (*ch, c), src=ins[a]) for j, ch in enumerate(chips)]
        for f in first:
            f.start()
        passed = []
        for a in range(n):
            for j, ch in enumerate(chips):
                cp(a, 1 + j, (*ch, c), me).wait_recv()
                passed.append(cp(a, 4 + j, (*ch, c), sib))
                passed[-1].start()
        for a in range(n):
            cp(a, 0, sib, me).wait_recv()
            for j, ch in enumerate(chips):
                cp(a, 4 + j, (*ch, 1 - c), me).wait_recv()
        for f in first + passed:
            f.wait_send()
        for m in mine:
            m.wait()

    return pl.pallas_call(
        body, name=name, in_specs=[_HBM] * n, out_specs=[_HBM] * n,
        out_shape=[jax.ShapeDtypeStruct((N_DEV,) + a.shape, a.dtype) for a in arrs],
        scratch_shapes=[pltpu.SemaphoreType.DMA((n, 7)), pltpu.SemaphoreType.DMA((n, 7)),
                        pltpu.SemaphoreType.DMA((n,))],
    )(*arrs)


def _exchange_cores(name, arrs):
    n = len(arrs)

    def body(*refs):
        ins, outs = refs[:n], refs[n:2 * n]
        send, recv, loc = refs[2 * n:]
        x, y, c, _ = _place()
        remote = [pltpu.make_async_remote_copy(src_ref=ins[a].at[2 * k + 1 - c], dst_ref=outs[a].at[k],
                                               send_sem=send.at[a, k], recv_sem=recv.at[a, k],
                                               device_id=(x, y, 1 - c), device_id_type=MESH)
                  for a in range(n) for k in range(4)]
        local = [pltpu.make_async_copy(ins[a].at[2 * k + c], outs[a].at[4 + k], loc.at[a, k])
                 for a in range(n) for k in range(4)]
        for d in remote + local:
            d.start()
        for d in remote + local:
            d.wait()

    return pl.pallas_call(
        body, name=name, in_specs=[_HBM] * n, out_specs=[_HBM] * n,
        out_shape=[jax.ShapeDtypeStruct(a.shape, a.dtype) for a in arrs],
        scratch_shapes=[pltpu.SemaphoreType.DMA((n, 4))] * 3,
    )(*arrs)


def _exchange_chips(name, arrs):
    n = len(arrs)

    def body(*refs):
        ins, outs = refs[:n], refs[n:2 * n]
        send, recv, loc = refs[2 * n:]
        x, y, c, chips = _place()
        remote = [pltpu.make_async_remote_copy(src_ref=ins[a].at[2 * ch[0] + ch[1]], dst_ref=outs[a].at[j],
                                               send_sem=send.at[a, j], recv_sem=recv.at[a, j],
                                               device_id=(*ch, c), device_id_type=MESH)
                  for a in range(n) for j, ch in enumerate(chips)]
        local = [pltpu.make_async_copy(ins[a].at[2 * x + y], outs[a].at[3], loc.at[a]) for a in range(n)]
        for d in remote + local:
            d.start()
        for d in remote + local:
            d.wait()

    return pl.pallas_call(
        body, name=name, in_specs=[_HBM] * n, out_specs=[_HBM] * n,
        out_shape=[jax.ShapeDtypeStruct(a.shape, a.dtype) for a in arrs],
        scratch_shapes=[pltpu.SemaphoreType.DMA((n, 3)), pltpu.SemaphoreType.DMA((n, 3)),
                        pltpu.SemaphoreType.DMA((n,))],
    )(*arrs)


def _blocks_of(r, c):
    if r * c * 4 <= (2 << 20):
        return r, c
    if r % 128 == 0:
        return 128, c
    return r, 256


def _pair_sum(name, a):
    _, r, c = a.shape
    rb, cb = _blocks_of(r, c)

    def body(a_ref, o_ref):
        o_ref[...] = (a_ref[0].astype(F32) + a_ref[1].astype(F32)).astype(o_ref.dtype)

    return pl.pallas_call(
        body, name=name, grid=(4, r // rb, c // cb),
        in_specs=[pl.BlockSpec((2, None, rb, cb), lambda k, i, j: (0, k, i, j))],
        out_specs=pl.BlockSpec((None, rb, cb), lambda k, i, j: (k, i, j)),
        out_shape=jax.ShapeDtypeStruct((4, r, c), a.dtype),
        compiler_params=_params(3, 24 << 20),
    )(a.reshape(2, 4, r, c))


def _adamw_math(w, g, m, v):
    m = B1 * m + (1.0 - B1) * g
    v = B2 * v + (1.0 - B2) * (g * g)
    m_hat = m / (1.0 - B1 ** STEP)
    v_hat = v / (1.0 - B2 ** STEP)
    return -LR * (m_hat / (jnp.sqrt(v_hat) + ADAM_EPS) + WD * w), m, v


def _adamw_parts(name, parts, w, m, v):
    n, r, c = parts.shape
    rb, cb = _blocks_of(r, c)

    def body(p_ref, w_ref, m_ref, v_ref, g_ref, d_ref, nm_ref, nv_ref):
        g = p_ref[n - 1].astype(F32)
        for k in range(n - 1):
            g = g + p_ref[k].astype(F32)
        g_ref[...] = g
        d_ref[...], nm_ref[...], nv_ref[...] = _adamw_math(w_ref[...], g, m_ref[...], v_ref[...])

    blk = pl.BlockSpec((rb, cb), lambda i, j: (i, j))
    return pl.pallas_call(
        body, name=name, grid=(r // rb, c // cb),
        in_specs=[pl.BlockSpec((n, rb, cb), lambda i, j: (0, i, j)), blk, blk, blk], out_specs=[blk] * 4,
        out_shape=[jax.ShapeDtypeStruct((r, c), F32)] * 4,
        compiler_params=_params(2, 40 << 20),
    )(parts, w, m, v)


def _pack(arrs):
    flat = []
    for a in arrs:
        a = a.reshape(-1).astype(F32)
        flat.append(jnp.pad(a, (0, -a.size % 1024)))
    return jnp.concatenate(flat).reshape(-1, 128)


def _unpack(packed, shapes):
    out, row = [], 0
    for s in shapes:
        size = math.prod(s)
        out.append(packed[row:row + pl.cdiv(size, 128)].reshape(-1)[:size].reshape(s))
        row += pl.cdiv(size, 1024) * 8
    return out


def kernel(x, meta_tokens, mix_norm, w_in, lam_re, lam_im, log_dt, b_re, b_im, c_re, c_im, d_skip, w_glu, b_glu, q_a_norm, w_q_b, kv_a_norm, w_kv_b, out_norm_ssm, out_norm_attn, w_out, ffn_norm, w_up, conv_w, conv_b, w_down, final_norm, loss_target, m_meta_tokens, m_mix_norm, m_w_in, m_lam_re, m_lam_im, m_log_dt, m_b_re, m_b_im, m_c_re, m_c_im, m_d_skip, m_w_glu, m_b_glu, m_q_a_norm, m_w_q_b, m_kv_a_norm, m_w_kv_b, m_out_norm_ssm, m_out_norm_attn, m_w_out, m_ffn_norm, m_w_up, m_conv_w, m_conv_b, m_w_down, m_final_norm, v_meta_tokens, v_mix_norm, v_w_in, v_lam_re, v_lam_im, v_log_dt, v_b_re, v_b_im, v_c_re, v_c_im, v_d_skip, v_w_glu, v_b_glu, v_q_a_norm, v_w_q_b, v_kv_a_norm, v_w_kv_b, v_out_norm_ssm, v_out_norm_attn, v_w_out, v_ffn_norm, v_w_up, v_conv_w, v_conv_b, v_w_down, v_final_norm):
    given = dict(locals())
    w = {n: given[n] for n in WEIGHTS}
    m = {n: given['m_' + n] for n in WEIGHTS}
    v = {n: given['v_' + n] for n in WEIGHTS}
    dev = 4 * lax.axis_index("x") + 2 * lax.axis_index("y") + lax.axis_index("c")

    shard = {
        'w_in': jnp.pad(w_in[0], ((0, 0), (0, D_IN_PAD - D_IN))),
        'w_glu': w_glu[0],
        'w_q_b': jnp.pad(w_q_b[0], ((0, 0), (0, HEAD_PAD - QK_NOPE - QK_ROPE))),
        'w_kv_b': w_kv_b[0],
        'w_out': w_out[0],
        'w_up': w_up[0],
        'w_down': w_down[0],
    }
    gathered = _all_gather("gather_weights", [shard[n].astype(BF16) for n in BIG]
                           + [meta_tokens, jnp.pad(conv_w[0], ((0, 5), (0, 0)))])
    wf = dict(zip(BIG, gathered[:len(BIG)]))
    for n in ('w_in', 'w_glu', 'w_out', 'w_down'):
        wf[n] = wf[n].reshape(-1, wf[n].shape[-1])
    meta = gathered[-2].transpose(1, 0, 2).reshape(N_META, D)
    wf['conv_w'] = gathered[-1][:, :3].transpose(1, 0, 2).reshape(3, D_FF)

    loss_part, grad_x, dmeta, dconv_w, big, sm = _local_step(x[0], loss_target[0], meta, {n: w[n] for n in SMALL}, wf)

    halves = _exchange_cores("reduce_cores", [big[n] for n in BIG])
    chip_sums = [_pair_sum("sum_cores_" + n, h) for n, h in zip(BIG, halves)]
    quarters = dict(zip(BIG, _exchange_chips("reduce_chips", chip_sums)))

    small_parts = [sm[n] for n in SMALL] + [dmeta, dconv_w, loss_part[0, 0]]
    small_all, = _all_gather("gather_small", [_pack(small_parts)])
    zero_like = lambda d: [jnp.zeros((N_META, D), F32), jnp.zeros((3, D_FF), F32), jnp.zeros((), F32)]
    g_pk, d_pk, m_pk, v_pk = _adamw_parts("adamw_small", small_all, _pack([w[n] for n in SMALL] + zero_like(w)),
                                          _pack([m[n] for n in SMALL] + zero_like(m)),
                                          _pack([v[n] for n in SMALL] + zero_like(v)))
    shapes = [w[n].shape for n in SMALL] + [(N_META, D), (3, D_FF), ()]
    g_sm, d_sm, m_sm, v_sm = (_unpack(t, shapes) for t in (g_pk, d_pk, m_pk, v_pk))
    grads, deltas, new_m, new_v = {}, {}, {}, {}
    for i, n in enumerate(SMALL):
        grads[n], deltas[n], new_m[n], new_v[n] = g_sm[i], d_sm[i], m_sm[i], v_sm[i]
    loss = g_sm[-1]

    g_meta = lax.dynamic_slice(g_sm[-3], (0, dev * (D // N_DEV)), (N_META, D // N_DEV))
    g_conv = lax.dynamic_slice(g_sm[-2], (0, dev * (D_FF // N_DEV)), (3, D_FF // N_DEV))
    for n, g, pad_rows in (('meta_tokens', g_meta, 0), ('conv_w', g_conv, 5)):
        two_d = lambda a: jnp.pad(a.reshape(g.shape), ((0, pad_rows), (0, 0)))
        outs = _adamw_parts("adamw_" + n, two_d(g)[None], two_d(w[n]), two_d(m[n]), two_d(v[n]))
        grads[n], deltas[n], new_m[n], new_v[n] = (o[:g.shape[0]].reshape(w[n].shape) for o in outs)

    for n in BIG:
        r, c = w[n].shape[1:]
        outs = _adamw_parts("adamw_" + n, quarters[n][:, :, :c], w[n][0], m[n][0], v[n][0])
        grads[n], deltas[n], new_m[n], new_v[n] = (o.reshape(w[n].shape) for o in outs)

    return (loss, grad_x[None], *[grads[n] for n in WEIGHTS], *[deltas[n] for n in WEIGHTS],
            *[new_m[n] for n in WEIGHTS], *[new_v[n] for n in WEIGHTS])
```

```python
import functools
import math

import jax
import jax.numpy as jnp
from jax import lax
from jax.experimental import pallas as pl
from jax.experimental.pallas import tpu as pltpu

F32, BF16 = jnp.float32, jnp.bfloat16

N_DEV = 8
N_META, SEQ, D = 16, 2048, 2048
T = N_META + SEQ
TR = 688
EPS = 1e-6
D_SSM, Q_LORA, KV_LORA, QK_ROPE = 1024, 512, 256, 64
D_IN = D_SSM + Q_LORA + KV_LORA + QK_ROPE
D_IN_PAD = 1920
N_HEADS, QK_NOPE, V_HEAD = 8, 128, 128
HEAD_PAD = 256
SM_SCALE = 1.0 / math.sqrt(QK_NOPE + QK_ROPE)
CHUNK = 64
N_GROUPS, SSM_GROUP, SSM_STATE = 64, 16, 64
N_SEG = 8
SEG = T // N_SEG
GQ = 8
D_FF = 5504
FF_PIECE = 1376
ROPE_BASE = 10000.0
LR, B1, B2, ADAM_EPS, WD, STEP = 0.001, 0.9, 0.999, 1e-08, 0.01, 10
VMEM_CAP = 60 * 1024 * 1024
MESH = pl.DeviceIdType.MESH

WEIGHTS = ['meta_tokens', 'mix_norm', 'w_in', 'lam_re', 'lam_im', 'log_dt', 'b_re', 'b_im', 'c_re', 'c_im',
           'd_skip', 'w_glu', 'b_glu', 'q_a_norm', 'w_q_b', 'kv_a_norm', 'w_kv_b', 'out_norm_ssm',
           'out_norm_attn', 'w_out', 'ffn_norm', 'w_up', 'conv_w', 'conv_b', 'w_down', 'final_norm']
BIG = ['w_in', 'w_glu', 'w_q_b', 'w_kv_b', 'w_out', 'w_up', 'w_down']
SMALL = [n for n in WEIGHTS if n not in BIG and n not in ('meta_tokens', 'conv_w')]


def _nbytes(shape, dtype):
    return math.prod(shape) * jnp.dtype(dtype).itemsize


def _params(sem, need):
    return pltpu.CompilerParams(dimension_semantics=("arbitrary",) * sem,
                                vmem_limit_bytes=int(min(VMEM_CAP, max(need, 16 * 1024 * 1024))))


_DIMS = {'nn': (((1,), (0,)), ((), ())), 'nt': (((1,), (1,)), ((), ())), 'tn': (((0,), (0,)), ((), ()))}


def _mm(name, a, b, mode, tm, tn, tk, out_dtype, stack=None, res=None):
    sa, sb, so = stack in ('a_out', 'ab_red'), stack in ('b_out', 'ab_red'), stack in ('a_out', 'b_out')
    nj = a.shape[0] if sa else (b.shape[0] if sb else 1)
    a2, b2 = a.shape[-2:], b.shape[-2:]
    if mode == 'nn':
        (m, k), (k2, n) = a2, b2
    elif mode == 'nt':
        (m, k), (n, k2) = a2, b2
    else:
        (k, m), (k2, n) = a2, b2
    assert k == k2 and m % tm == 0 and n % tn == 0 and k % tk == 0, (name, a.shape, b.shape)
    n_jo, n_jr, n_k = (nj if so else 1), (nj if stack == 'ab_red' else 1), k // tk
    grid = (n_jo, m // tm, n // tn, n_jr, n_k)
    a_blk, a_idx = ((tk, tm), lambda i, j, kk: (kk, i)) if mode == 'tn' else ((tm, tk), lambda i, j, kk: (i, kk))
    b_blk, b_idx = ((tn, tk), lambda i, j, kk: (j, kk)) if mode == 'nt' else ((tk, tn), lambda i, j, kk: (kk, j))

    def spec(blk, idx, stacked, on_out):
        if not stacked:
            return pl.BlockSpec(blk, lambda jo, i, j, jr, kk: idx(i, j, kk))
        return pl.BlockSpec((None,) + blk, lambda jo, i, j, jr, kk: ((jo if on_out else jr),) + idx(i, j, kk))

    o_idx = lambda i, j, kk: (i, j)
    in_specs = [spec(a_blk, a_idx, sa, stack == 'a_out'), spec(b_blk, b_idx, sb, stack == 'b_out')]
    operands = [a, b]
    if res is not None:
        in_specs.append(spec((tm, tn), o_idx, False, False))
        operands.append(res)
    n_red = n_jr * n_k
    dims = _DIMS[mode]

    def body(*refs):
        a_ref, b_ref = refs[0], refs[1]
        res_ref = refs[2] if res is not None else None
        o_ref = refs[3] if res is not None else refs[2]
        part = lax.dot_general(a_ref[...].astype(BF16), b_ref[...].astype(BF16), dims,
                               preferred_element_type=F32)

        def finish(total):
            if res_ref is not None:
                total = total + res_ref[...]
            o_ref[...] = total.astype(o_ref.dtype)

        if n_red == 1:
            finish(part)
        else:
            acc_ref = refs[-1]
            step = pl.program_id(3) * n_k + pl.program_id(4)

            @pl.when(step == 0)
            def _():
                acc_ref[...] = part

            @pl.when(step > 0)
            def _():
                acc_ref[...] += part

            @pl.when(step == n_red - 1)
            def _():
                finish(acc_ref[...])

    out_shape = ((nj,) if so else ()) + (m, n)
    need = 2 * (_nbytes(a_blk, a.dtype) + _nbytes(b_blk, b.dtype) + _nbytes((tm, tn), out_dtype)
                + (_nbytes((tm, tn), F32) if res is not None else 0)) + 3 * _nbytes((tm, tn), F32) + (4 << 20)
    return pl.pallas_call(
        body, name=name, grid=grid, in_specs=in_specs, out_specs=spec((tm, tn), o_idx, so, True),
        out_shape=jax.ShapeDtypeStruct(out_shape, out_dtype),
        scratch_shapes=[pltpu.VMEM((tm, tn), F32)] if n_red > 1 else [],
        compiler_params=_params(5, need),
    )(*operands)


def _rowwise(name, fn, row_ins, const_ins, row_outs, acc_outs, tr=TR):
    n_ri, n_ci, n_ro, n_ao = len(row_ins), len(const_ins), len(row_outs), len(acc_outs)

    def body(*refs):
        ri, ci = refs[:n_ri], refs[n_ri:n_ri + n_ci]
        ro, ao = refs[n_ri + n_ci:n_ri + n_ci + n_ro], refs[n_ri + n_ci + n_ro:]
        i = pl.program_id(0)
        outs = fn(i * tr, *[r[...] for r in ri], *[r[...] for r in ci])
        for r, v in zip(ro, outs[:n_ro]):
            r[...] = v.astype(r.dtype)
        if n_ao:
            @pl.when(i == 0)
            def _():
                for r, v in zip(ao, outs[n_ro:]):
                    r[...] = v

            @pl.when(i > 0)
            def _():
                for r, v in zip(ao, outs[n_ro:]):
                    r[...] += v

    in_specs = [pl.BlockSpec((tr, w), functools.partial(lambda cb, i: (i, cb), cb)) for _, w, cb in row_ins]
    in_specs += [pl.BlockSpec(c.shape, functools.partial(lambda nd, i: (0,) * nd, c.ndim)) for c in const_ins]
    out_specs = [pl.BlockSpec((tr, w), lambda i: (i, 0)) for w, _ in row_outs]
    out_specs += [pl.BlockSpec((1, w), lambda i: (0, 0)) for w in acc_outs]
    out_shape = [jax.ShapeDtypeStruct((T, w), dt) for w, dt in row_outs]
    out_shape += [jax.ShapeDtypeStruct((1, w), F32) for w in acc_outs]
    need = 2 * (sum(_nbytes((tr, w), a.dtype) for a, w, _ in row_ins) + sum(_nbytes((tr, w), dt) for w, dt in row_outs))
    need = 3 * need + (8 << 20)
    return pl.pallas_call(
        body, name=name, grid=(T // tr,), in_specs=in_specs, out_specs=out_specs, out_shape=out_shape,
        compiler_params=_params(1, need),
    )(*[a for a, _, _ in row_ins], *const_ins)


def _rms_fwd_fn(row0, h, g):
    h = h.astype(F32)
    r = lax.rsqrt(jnp.mean(h * h, axis=-1, keepdims=True) + EPS)
    return (h * r) * g, r


def _rms_bwd(dxn, h, r, g):
    xhat = h * r
    dxg = dxn * g
    dh = r * (dxg - xhat * jnp.mean(dxg * xhat, axis=-1, keepdims=True))
    return dh, jnp.sum(dxn * xhat, axis=0, keepdims=True)


def _rms_bwd_fn(row0, dxn, h, r, g):
    return _rms_bwd(dxn.astype(F32), h.astype(F32), r, g)


def _rms_bwd_res_fn(row0, dxn, h, r, res, g):
    dh, dg = _rms_bwd(dxn.astype(F32), h.astype(F32), r, g)
    return dh + res, dg


def _gelu_fn(row0, y):
    return (jax.nn.gelu(y),)


def _glu_fwd_fn(row0, y, gl, b, gain):
    ya = jax.nn.gelu(y) * jax.nn.sigmoid(gl + b)
    return _rms_fwd_fn(row0, ya, gain)


def _glu_bwd_fn(row0, dyan, y, gl, ra, b, gain):
    g = jax.nn.gelu(y)
    s = jax.nn.sigmoid(gl + b)
    dya, dgain = _rms_bwd(dyan, g * s, ra, gain)
    dgl = dya * g * s * (1.0 - s)
    return dgl, dya * s, dgain, jnp.sum(dgl, axis=0, keepdims=True)


def _gelu_bwd_fn(row0, dg, y):
    return (jax.vjp(jax.nn.gelu, y)[1](dg)[0],)


def _loss_fn(row0, h, tgt, gain):
    r = lax.rsqrt(jnp.mean(h * h, axis=-1, keepdims=True) + EPS)
    xhat = h * r
    rows = row0 + lax.broadcasted_iota(jnp.int32, h.shape, 0)
    err = jnp.where(rows >= N_META, xhat * gain - tgt, 0.0)
    loss = jnp.full((1, 128), jnp.sum(err * err) * (0.5 / D), F32)
    dh, dgain = _rms_bwd(err * (1.0 / D), h, r, gain)
    return dh, dgain, loss


def _rope_tables():
    pos = jnp.arange(T, dtype=jnp.int32)
    inv_freq = 1.0 / (ROPE_BASE ** (jnp.arange(0, QK_ROPE, 2, dtype=F32) / QK_ROPE))
    ang = pos.astype(F32)[:, None] * inv_freq[None, :]
    cos, sin, z32, z64 = jnp.cos(ang), jnp.sin(ang), jnp.zeros((T, 32), F32), jnp.zeros((T, 64), F32)
    return (jnp.concatenate([cos, cos, z64], 1), jnp.concatenate([-sin, z32, z64], 1),
            jnp.concatenate([z32, sin, z64], 1))


def _rot(x, cc, s1, s2):
    return x * cc + pltpu.roll(x, 96, 1) * s1 + pltpu.roll(x, 32, 1) * s2


def _derot(d, cc, s1, s2):
    return d * cc + pltpu.roll(d * s1, 32, 1) + pltpu.roll(d * s2, 96, 1)


def _chunk_of(pos):
    return jnp.where(pos < N_META, 0, 1 + ((pos - N_META) >> 6))


def _attn_probs(i, q_ref, kv_ref, kpe_ref, tq, tk):
    q = q_ref[...]
    qn = q[:, :QK_NOPE].astype(BF16)
    qp = _rot(q[:, QK_NOPE:], tq[0][...], tq[1][...], tq[2][...]).astype(BF16)
    kn = kv_ref[:, :QK_NOPE].astype(BF16)
    v = kv_ref[:, QK_NOPE:].astype(BF16)
    kp = _rot(kpe_ref[...], tk[0][...], tk[1][...], tk[2][...]).astype(BF16)
    s = lax.dot_general(qn, kn, _DIMS['nt'], preferred_element_type=F32)
    s = s + lax.dot_general(qp, kp, _DIMS['nt'], preferred_element_type=F32)
    qc = _chunk_of(i * TR + lax.broadcasted_iota(jnp.int32, (TR, 1), 0))
    kc = _chunk_of(lax.broadcasted_iota(jnp.int32, (1, T), 1))
    s = jnp.where(kc <= qc, s * SM_SCALE, jnp.finfo(F32).min)
    p = jnp.exp(s - jnp.max(s, axis=-1, keepdims=True))
    p = p / jnp.sum(p, axis=-1, keepdims=True)
    return qn, qp, kn, kp, v, p


def _attn_specs(z, tabs):
    q_spec = pl.BlockSpec((None, TR, HEAD_PAD), lambda h, i: (h, i, 0))
    kv_spec = pl.BlockSpec((None, T, HEAD_PAD), lambda h, i: (h, 0, 0))
    kpe_spec = pl.BlockSpec((T, 128), lambda h, i: (0, (D_IN - QK_ROPE) // 128))
    tq = [pl.BlockSpec((TR, 128), lambda h, i: (i, 0))] * 3
    tk = [pl.BlockSpec((T, 128), lambda h, i: (0, 0))] * 3
    return [q_spec, kv_spec, kpe_spec] + tq + tk


def _attn_fwd(q, kv, z, tabs):
    def body(q_ref, kv_ref, kpe_ref, c1, c2, c3, k1, k2, k3, o_ref):
        _, _, _, _, v, p = _attn_probs(pl.program_id(1), q_ref, kv_ref, kpe_ref, (c1, c2, c3), (k1, k2, k3))
        o_ref[...] = jnp.dot(p.astype(BF16), v, preferred_element_type=F32)

    return pl.pallas_call(
        body, name="attn_fwd", grid=(N_HEADS, T // TR), in_specs=_attn_specs(z, tabs),
        out_specs=pl.BlockSpec((TR, V_HEAD), lambda h, i: (i, h)),
        out_shape=jax.ShapeDtypeStruct((T, N_HEADS * V_HEAD), F32),
        compiler_params=_params(2, 48 << 20),
    )(q, kv, z, *tabs, *tabs)


def _attn_bwd(q, kv, z, tabs, dyb):
    def body(q_ref, kv_ref, kpe_ref, c1, c2, c3, k1, k2, k3, do_ref, dq_ref, dkv_ref, dkpe_ref):
        i = pl.program_id(1)
        qn, qp, kn, kp, v, p = _attn_probs(i, q_ref, kv_ref, kpe_ref, (c1, c2, c3), (k1, k2, k3))
        do = do_ref[...].astype(BF16)
        dp = lax.dot_general(do, v, _DIMS['nt'], preferred_element_type=F32)
        ds = (p * (dp - jnp.sum(p * dp, axis=-1, keepdims=True)) * SM_SCALE).astype(BF16)
        pb = p.astype(BF16)
        dq_ref[:, :QK_NOPE] = jnp.dot(ds, kn, preferred_element_type=F32).astype(dq_ref.dtype)
        dqp = _derot(jnp.dot(ds, kp, preferred_element_type=F32), c1[...], c2[...], c3[...])
        dq_ref[:, QK_NOPE:] = dqp.astype(dq_ref.dtype)
        dkn = lax.dot_general(ds, qn, _DIMS['tn'], preferred_element_type=F32)
        dv = lax.dot_general(pb, do, _DIMS['tn'], preferred_element_type=F32)
        dkp = lax.dot_general(ds, qp, _DIMS['tn'], preferred_element_type=F32)

        @pl.when(i == 0)
        def _():
            dkv_ref[:, :QK_NOPE] = dkn
            dkv_ref[:, QK_NOPE:] = dv
            dkpe_ref[...] = dkp

        @pl.when(i > 0)
        def _():
            dkv_ref[:, :QK_NOPE] += dkn
            dkv_ref[:, QK_NOPE:] += dv
            dkpe_ref[...] += dkp

    return pl.pallas_call(
        body, name="attn_bwd", grid=(N_HEADS, T // TR),
        in_specs=_attn_specs(z, tabs) + [pl.BlockSpec((TR, V_HEAD), lambda h, i: (i, h))],
        out_specs=[pl.BlockSpec((None, TR, HEAD_PAD), lambda h, i: (h, i, 0)),
                   pl.BlockSpec((None, T, HEAD_PAD), lambda h, i: (h, 0, 0)),
                   pl.BlockSpec((None, T, 128), lambda h, i: (h, 0, 0))],
        out_shape=[jax.ShapeDtypeStruct((N_HEADS, T, HEAD_PAD), BF16),
                   jax.ShapeDtypeStruct((N_HEADS, T, HEAD_PAD), F32),
                   jax.ShapeDtypeStruct((N_HEADS, T, 128), F32)],
        compiler_params=_params(2, 56 << 20),
    )(q, kv, z, *tabs, *tabs, dyb)


def _kpe_bwd(dkpe, tabs):
    def body(d_ref, c1, c2, c3, o_ref):
        d = d_ref[0]
        for h in range(1, N_HEADS):
            d = d + d_ref[h]
        o_ref[...] = _derot(d, c1[...], c2[...], c3[...]).astype(o_ref.dtype)

    tile = pl.BlockSpec((TR, 128), lambda i: (i, 0))
    return pl.pallas_call(
        body, name="kpe_bwd", grid=(T // TR,),
        in_specs=[pl.BlockSpec((N_HEADS, TR, 128), lambda i: (0, i, 0)), tile, tile, tile], out_specs=tile,
        out_shape=jax.ShapeDtypeStruct((T, 128), BF16), compiler_params=_params(1, 16 << 20),
    )(dkpe, *tabs)


def _s5_prep_math(lam_re, lam_im, log_dt, bt_re, bt_im):
    dt = jnp.exp(log_dt)
    mag = jnp.exp(lam_re * dt)
    a_re, a_im = mag * jnp.cos(lam_im * dt), mag * jnp.sin(lam_im * dt)
    den = lam_re * lam_re + lam_im * lam_im
    w_re = ((a_re - 1.0) * lam_re + a_im * lam_im) / den
    w_im = (a_im * lam_re - (a_re - 1.0) * lam_im) / den
    return a_re, a_im, w_re[None] * bt_re - w_im[None] * bt_im, w_re[None] * bt_im + w_im[None] * bt_re


def _eye_groups():
    return (lax.broadcasted_iota(jnp.int32, (N_GROUPS, N_GROUPS), 0)
            == lax.broadcasted_iota(jnp.int32, (N_GROUPS, N_GROUPS), 1)).astype(F32)


def _row_to_col(row):
    return jnp.sum(_eye_groups() * row, axis=1, keepdims=True)


def _s5_prep(lam_re, lam_im, log_dt, bt_re, bt_im):
    def body(lr, li, ld, br, bi, ar, ai, bbr, bbi):
        ar[...], ai[...], bbr[...], bbi[...] = _s5_prep_math(lr[...], li[...], _row_to_col(ld[...]), br[...], bi[...])

    gp, cgp = jax.ShapeDtypeStruct((N_GROUPS, SSM_STATE), F32), jax.ShapeDtypeStruct(bt_re.shape, F32)
    return pl.pallas_call(body, name="s5_prep", out_shape=[gp, gp, cgp, cgp])(lam_re, lam_im, log_dt, bt_re, bt_im)


def _s5_prep_bwd(lam_re, lam_im, log_dt, bt_re, bt_im, da_re, da_im, dbb_re, dbb_im):
    def body(lr, li, ld, br, bi, dar, dai, dbr, dbi, o1, o2, o3, o4, o5):
        _, vjp = jax.vjp(_s5_prep_math, lr[...], li[...], _row_to_col(ld[...]), br[...], bi[...])
        o1[...], o2[...], dld, o4[...], o5[...] = vjp((dar[...], dai[...], dbr[...], dbi[...]))
        o3[...] = jnp.sum(_eye_groups() * dld, axis=0, keepdims=True)

    ins = (lam_re, lam_im, log_dt, bt_re, bt_im)
    return pl.pallas_call(body, name="s5_prep_bwd", out_shape=[jax.ShapeDtypeStruct(a.shape, F32) for a in ins])(
        *ins, da_re, da_im, dbb_re, dbb_im)


def _cmul(ar, ai, br, bi):
    return ar * br - ai * bi, ar * bi + ai * br


def _seg_rows(i):
    return pl.ds(pl.multiple_of(i * N_SEG, N_SEG), N_SEG)


def _scan(xr, xi, ar, ai, reverse):
    zero = jnp.zeros(ar.shape, F32)

    def local(j, carry):
        rows = _seg_rows(SEG - 1 - j if reverse else j)
        nr, ni = _cmul(ar, ai, *carry)
        nr, ni = nr + xr[rows, :], ni + xi[rows, :]
        xr[rows, :], xi[rows, :] = nr, ni
        return nr, ni

    er, ei = lax.fori_loop(0, SEG, local, (zero, zero))
    pr, pi = ar, ai
    for _ in range(8):
        pr, pi = _cmul(pr, pi, pr, pi)
    pr, pi = _cmul(*_cmul(pr, pi, ar, ai), ar, ai)
    row = lax.broadcasted_iota(jnp.int32, ar.shape, 0)
    edge, shift = (N_SEG - 1, N_SEG - 1) if reverse else (0, 1)
    hr, hi = zero, zero
    for _ in range(N_SEG - 1):
        tr_, ti_ = _cmul(pr, pi, hr, hi)
        hr = jnp.where(row == edge, 0.0, pltpu.roll(tr_ + er, shift, 0))
        hi = jnp.where(row == edge, 0.0, pltpu.roll(ti_ + ei, shift, 0))

    def fix(j, carry):
        rows = _seg_rows(SEG - 1 - j if reverse else j)
        xr[rows, :] += carry[0]
        xi[rows, :] += carry[1]
        return _cmul(ar, ai, *carry)

    lax.fori_loop(0, SEG, fix, _cmul(ar, ai, hr, hi))
    return hr, hi


def _split(x):
    hi = x.astype(BF16)
    return hi, (x - hi.astype(F32)).astype(BF16)


def _dot3(xs, ys, mode='nn'):
    d = lambda p, q: lax.dot_general(p, q, _DIMS[mode], preferred_element_type=F32)
    return d(xs[0], ys[0]) + (d(xs[1], ys[0]) + d(xs[0], ys[1]))


def _s5_hidden(us, a_ref, bre_ref, bim_ref, hre, him):
    hre[...] = _dot3(us, _split(bre_ref[...]))
    him[...] = _dot3(us, _split(bim_ref[...]))
    return _scan(hre, him, a_ref[0], a_ref[1], reverse=False)


def _s5_specs():
    col = pl.BlockSpec((T, 128), lambda q: (0, q))
    a_spec = pl.BlockSpec((None, 2, N_SEG, GQ * SSM_STATE), lambda q: (q, 0, 0, 0))
    wide = pl.BlockSpec((None, 128, GQ * SSM_STATE), lambda q: (q, 0, 0))
    tall = pl.BlockSpec((None, GQ * SSM_STATE, 128), lambda q: (q, 0, 0))
    d_spec = pl.BlockSpec((1, 128), lambda q: (0, q))
    return col, a_spec, wide, tall, d_spec


def _s5_fwd(u, a8, bre, bim, cre, cim, dsk):
    def body(u_ref, a_ref, bre_ref, bim_ref, cre_ref, cim_ref, d_ref, y_ref, hre, him):
        u_ = u_ref[...]
        _s5_hidden(_split(u_), a_ref, bre_ref, bim_ref, hre, him)
        y_ref[...] = (_dot3(_split(hre[...]), _split(cre_ref[...]))
                      - _dot3(_split(him[...]), _split(cim_ref[...])) + d_ref[...] * u_)

    col, a_spec, wide, tall, d_spec = _s5_specs()
    return pl.pallas_call(
        body, name="s5_fwd", grid=(N_GROUPS // GQ,), in_specs=[col, a_spec, wide, wide, tall, tall, d_spec],
        out_specs=col, out_shape=jax.ShapeDtypeStruct((T, D_SSM), F32),
        scratch_shapes=[pltpu.VMEM((T, GQ * SSM_STATE), F32)] * 2,
        compiler_params=_params(1, 40 << 20),
    )(u, a8, bre, bim, cre, cim, dsk)


def _s5_bwd(u, dy, a8, bre, bim, bre_t, bim_t, cre_t, cim_t, dsk):
    def body(u_ref, dy_ref, a_ref, bre_ref, bim_ref, bret_ref, bimt_ref, cret_ref, cimt_ref, d_ref,
             du_ref, dbre_ref, dbim_ref, dcre_ref, dcim_ref, da_ref, dd_ref, hre, him, lre, lim):
        u_, dy_ = u_ref[...], dy_ref[...]
        us, dys = _split(u_), _split(dy_)
        ar, ai = a_ref[0], a_ref[1]
        h0r, h0i = _s5_hidden(us, a_ref, bre_ref, bim_ref, hre, him)
        lre[...] = _dot3(dys, _split(cret_ref[...]))
        lim[...] = -_dot3(dys, _split(cimt_ref[...]))
        _scan(lre, lim, ar, -ai, reverse=True)

        def acc_da(i, carry):
            lr, li = lre[_seg_rows(i), :], lim[_seg_rows(i), :]
            pr, pi = hre[_seg_rows(i - 1), :], him[_seg_rows(i - 1), :]
            return carry[0] + lr * pr + li * pi, carry[1] + li * pr - lr * pi

        lr, li = lre[_seg_rows(0), :], lim[_seg_rows(0), :]
        dar, dai = lax.fori_loop(1, SEG, acc_da, (lr * h0r + li * h0i, li * h0r - lr * h0i))
        da_ref[0:1, :] = jnp.sum(dar, axis=0, keepdims=True)
        da_ref[1:2, :] = jnp.sum(dai, axis=0, keepdims=True)
        lrs, lis = _split(lre[...]), _split(lim[...])
        du_ref[...] = (dy_ * d_ref[...] + _dot3(lrs, _split(bret_ref[...])) + _dot3(lis, _split(bimt_ref[...])))
        dbre_ref[...] = _dot3(us, lrs, 'tn')
        dbim_ref[...] = _dot3(us, lis, 'tn')
        dcre_ref[...] = _dot3(_split(hre[...]), dys, 'tn')
        dcim_ref[...] = -_dot3(_split(him[...]), dys, 'tn')
        dd_ref[...] = jnp.sum(dy_ * u_, axis=0, keepdims=True)

    col, a_spec, wide, tall, d_spec = _s5_specs()
    nq, ns = N_GROUPS // GQ, GQ * SSM_STATE
    return pl.pallas_call(
        body, name="s5_bwd", grid=(nq,),
        in_specs=[col, col, a_spec, wide, wide, tall, tall, wide, wide, d_spec],
        out_specs=[col, wide, wide, tall, tall, pl.BlockSpec((None, 2, ns), lambda q: (q, 0, 0)), d_spec],
        out_shape=[jax.ShapeDtypeStruct((T, D_SSM), F32),
                   jax.ShapeDtypeStruct((nq, 128, ns), F32), jax.ShapeDtypeStruct((nq, 128, ns), F32),
                   jax.ShapeDtypeStruct((nq, ns, 128), F32), jax.ShapeDtypeStruct((nq, ns, 128), F32),
                   jax.ShapeDtypeStruct((nq, 2, ns), F32), jax.ShapeDtypeStruct((1, D_SSM), F32)],
        scratch_shapes=[pltpu.VMEM((T, ns), F32)] * 4,
        compiler_params=_params(1, 56 << 20),
    )(u, dy, a8, bre, bim, bre_t, bim_t, cre_t, cim_t, dsk)


def _to_segments(a):
    return a.reshape(N_SEG, SEG, a.shape[-1]).transpose(1, 0, 2).reshape(T, a.shape[-1])


def _from_segments(a):
    return a.reshape(SEG, N_SEG, a.shape[-1]).transpose(1, 0, 2).reshape(T, a.shape[-1])


def _block_diag_b(bbt):
    nq = N_GROUPS // GQ
    x = bbt.transpose(1, 0, 2).reshape(nq, GQ, SSM_GROUP, 1, SSM_STATE)
    eye = jnp.eye(GQ, dtype=F32)[None, :, None, :, None]
    return (x * eye).reshape(nq, GQ * SSM_GROUP, GQ * SSM_STATE)


def _block_diag_c(c):
    nq = N_GROUPS // GQ
    x = c.reshape(nq, GQ, SSM_GROUP, SSM_STATE).transpose(0, 1, 3, 2)[:, :, :, None, :]
    eye = jnp.eye(GQ, dtype=F32)[None, :, None, :, None]
    return (x * eye).reshape(nq, GQ * SSM_STATE, GQ * SSM_GROUP)


def _ff_specs():
    up_spec = pl.BlockSpec((2, None, T, 128), lambda p, cb: (0, p, 0, cb))
    piece = pl.BlockSpec((None, T, 128), lambda p, cb: (p, 0, cb))
    cw_spec = pl.BlockSpec((None, 3, 128), lambda p, cb: (p, 0, cb))
    cb_spec = pl.BlockSpec((None, 1, 128), lambda p, cb: (p, 0, cb))
    return up_spec, piece, cw_spec, cb_spec


def _conv_gate(gate, w, cb, pad_ref):
    pad_ref[0:8, :] = jnp.zeros((8, 128), F32)
    pad_ref[8:8 + T, :] = gate
    g1, g2 = pad_ref[pl.ds(7, T), :], pad_ref[pl.ds(6, T), :]
    return w[0:1, :] * g2 + w[1:2, :] * g1 + w[2:3, :] * gate + cb, g1, g2


def _ff_act(up, cw4, cb4):
    def body(up_ref, w_ref, b_ref, o_ref, pad_ref):
        gc, _, _ = _conv_gate(up_ref[0], w_ref[...], b_ref[...], pad_ref)
        o_ref[...] = (jax.nn.silu(gc) * up_ref[1]).astype(o_ref.dtype)

    up_spec, piece, cw_spec, cb_spec = _ff_specs()
    n_cb = pl.cdiv(FF_PIECE, 128)
    return pl.pallas_call(
        body, name="ff_act", grid=(4, n_cb), in_specs=[up_spec, cw_spec, cb_spec], out_specs=piece,
        out_shape=jax.ShapeDtypeStruct((4, T, FF_PIECE), BF16),
        scratch_shapes=[pltpu.VMEM((T + 8, 128), F32)],
        compiler_params=_params(2, 24 << 20),
    )(up.reshape(2, 4, T, FF_PIECE), cw4, cb4)


def _ff_act_bwd(up, dact, cw4, cb4):
    def body(up_ref, da_ref, w_ref, b_ref, dup_ref, dw_ref, db_ref, pad_ref, pad2_ref):
        gate, val, w = up_ref[0], up_ref[1], w_ref[...]
        gc, g1, g2 = _conv_gate(gate, w, b_ref[...], pad_ref)
        sg = jax.nn.sigmoid(gc)
        da = da_ref[...]
        dup_ref[1] = (da * gc * sg).astype(dup_ref.dtype)
        dgc = da * val * sg * (1.0 + gc * (1.0 - sg))
        db_ref[...] = jnp.sum(dgc, axis=0, keepdims=True)
        dw_ref[0:1, :] = jnp.sum(dgc * g2, axis=0, keepdims=True)
        dw_ref[1:2, :] = jnp.sum(dgc * g1, axis=0, keepdims=True)
        dw_ref[2:3, :] = jnp.sum(dgc * gate, axis=0, keepdims=True)
        pad2_ref[0:T, :] = dgc
        pad2_ref[T:T + 8, :] = jnp.zeros((8, 128), F32)
        d1, d2 = pad2_ref[pl.ds(1, T), :], pad2_ref[pl.ds(2, T), :]
        dup_ref[0] = (w[2:3, :] * dgc + w[1:2, :] * d1 + w[0:1, :] * d2).astype(dup_ref.dtype)

    up_spec, piece, cw_spec, cb_spec = _ff_specs()
    n_cb = pl.cdiv(FF_PIECE, 128)
    dup, dw, db = pl.pallas_call(
        body, name="ff_act_bwd", grid=(4, n_cb), in_specs=[up_spec, piece, cw_spec, cb_spec],
        out_specs=[up_spec, cw_spec, cb_spec],
        out_shape=[jax.ShapeDtypeStruct((2, 4, T, FF_PIECE), BF16), jax.ShapeDtypeStruct((4, 3, FF_PIECE), F32),
                   jax.ShapeDtypeStruct((4, 1, FF_PIECE), F32)],
        scratch_shapes=[pltpu.VMEM((T + 8, 128), F32)] * 2,
        compiler_params=_params(2, 32 << 20),
    )(up.reshape(2, 4, T, FF_PIECE), dact, cw4, cb4)
    return dup.reshape(8, T, FF_PIECE), dw, db


def _local_step(x, tgt, meta, p, wf):
    sm = {}
    h0 = jnp.concatenate([meta, x], axis=0)
    tgt_pad = jnp.concatenate([jnp.zeros((N_META, D), F32), tgt], axis=0)
    tabs = _rope_tables()
    row = lambda a: a.reshape(1, -1)

    xn, r0 = _rowwise("rms_mix", _rms_fwd_fn, [(h0, D, 0)], [row(p['mix_norm'])], [(D, BF16), (1, F32)], [])
    z = _mm("in_proj", xn, wf['w_in'], 'nn', TR, 640, D, F32)
    lam_re, lam_im = p['lam_re'].reshape(N_GROUPS, SSM_STATE), p['lam_im'].reshape(N_GROUPS, SSM_STATE)
    log_dt = p['log_dt'].reshape(1, N_GROUPS)
    bt_re = p['b_re'].reshape(N_GROUPS, SSM_STATE, SSM_GROUP).transpose(2, 0, 1)
    bt_im = p['b_im'].reshape(N_GROUPS, SSM_STATE, SSM_GROUP).transpose(2, 0, 1)
    c_re = p['c_re'].reshape(N_GROUPS, SSM_GROUP, SSM_STATE)
    c_im = p['c_im'].reshape(N_GROUPS, SSM_GROUP, SSM_STATE)
    a_re, a_im, bbt_re, bbt_im = _s5_prep(lam_re, lam_im, log_dt, bt_re, bt_im)
    nq, ns = N_GROUPS // GQ, GQ * SSM_STATE
    a8 = jnp.broadcast_to(jnp.stack([a_re.reshape(nq, ns), a_im.reshape(nq, ns)], 1)[:, :, None, :],
                          (nq, 2, N_SEG, ns))
    bre, bim = _block_diag_b(bbt_re), _block_diag_b(bbt_im)
    cre, cim = _block_diag_c(c_re), _block_diag_c(c_im)
    dsk = row(p['d_skip'])
    u_seg = _to_segments(z[:, :D_SSM])
    y_ssm = _from_segments(_s5_fwd(u_seg, a8, bre, bim, cre, cim, dsk))
    g, = _rowwise("gelu", _gelu_fn, [(y_ssm, D_SSM, 0)], [], [(D_SSM, BF16)], [])
    gl = _mm("glu_proj", g, wf['w_glu'], 'nn', TR, D_SSM, D_SSM, F32)
    yan, ra = _rowwise("glu_norm", _glu_fwd_fn, [(y_ssm, D_SSM, 0), (gl, D_SSM, 0)],
                       [row(p['b_glu']), row(p['out_norm_ssm'])], [(D_SSM, BF16), (1, F32)], [])
    qn, rq = _rowwise("rms_q", _rms_fwd_fn, [(z, Q_LORA, D_SSM // Q_LORA)], [row(p['q_a_norm'])],
                      [(Q_LORA, BF16), (1, F32)], [])
    kvn, rkv = _rowwise("rms_kv", _rms_fwd_fn, [(z, KV_LORA, (D_SSM + Q_LORA) // KV_LORA)], [row(p['kv_a_norm'])],
                        [(KV_LORA, BF16), (1, F32)], [])
    q = _mm("q_proj", qn, wf['w_q_b'], 'nn', T, HEAD_PAD, Q_LORA, F32, stack='b_out')
    kv = _mm("kv_proj", kvn, wf['w_kv_b'], 'nn', T, HEAD_PAD, KV_LORA, F32, stack='b_out')
    yb = _attn_fwd(q, kv, z, tabs)
    ybn, rb = _rowwise("rms_attn", _rms_fwd_fn, [(yb, D_SSM, 0)], [row(p['out_norm_attn'])],
                       [(D_SSM, BF16), (1, F32)], [])
    y = jnp.concatenate([yan, ybn], axis=1)
    h1 = _mm("out_proj", y, wf['w_out'], 'nn', TR, 1024, D, F32, res=h0)
    xn2, r1 = _rowwise("rms_ffn", _rms_fwd_fn, [(h1, D, 0)], [row(p['ffn_norm'])], [(D, BF16), (1, F32)], [])
    up = _mm("up_proj", xn2, wf['w_up'], 'nn', TR, FF_PIECE, D, F32, stack='b_out')
    cw4 = wf['conv_w'].reshape(3, 4, FF_PIECE).transpose(1, 0, 2)
    cb4 = p['conv_b'].reshape(4, 1, FF_PIECE)
    act = _ff_act(up, cw4, cb4)
    wd4 = wf['w_down'].reshape(4, FF_PIECE, D)
    h2 = _mm("down_proj", act, wd4, 'nn', TR, 1024, FF_PIECE, F32, stack='ab_red', res=h1)
    dh2, sm['final_norm'], loss = _rowwise("loss", _loss_fn, [(h2, D, 0), (tgt_pad, D, 0)], [row(p['final_norm'])],
                                           [(D, F32)], [D, 128])

    big = {}
    dact = _mm("down_bwd_x", dh2, wd4, 'nt', TR, FF_PIECE, D, F32, stack='b_out')
    big['w_down'] = _mm("down_bwd_w", act, dh2, 'tn', FF_PIECE, 1024, TR, BF16, stack='a_out').reshape(
        N_DEV, D_FF // N_DEV, D)
    dup, dcw4, sm['conv_b'] = _ff_act_bwd(up, dact, cw4, cb4)
    dxn2 = _mm("up_bwd_x", dup, wf['w_up'], 'nt', TR, 1024, FF_PIECE, F32, stack='ab_red')
    big['w_up'] = _mm("up_bwd_w", xn2, dup, 'tn', 1024, FF_PIECE, TR, BF16, stack='b_out')
    dh1, sm['ffn_norm'] = _rowwise("rms_ffn_bwd", _rms_bwd_res_fn,
                                   [(dxn2, D, 0), (h1, D, 0), (r1, 1, 0), (dh2, D, 0)], [row(p['ffn_norm'])],
                                   [(D, F32)], [D])
    dy = _mm("out_bwd_x", dh1, wf['w_out'], 'nt', TR, 1024, D, F32)
    big['w_out'] = _mm("out_bwd_w", y, dh1, 'tn', 1024, 1024, TR, BF16).reshape(N_DEV, D // N_DEV, D)
    dgl, dgd, sm['out_norm_ssm'], sm['b_glu'] = _rowwise(
        "glu_bwd", _glu_bwd_fn, [(dy, D_SSM, 0), (y_ssm, D_SSM, 0), (gl, D_SSM, 0), (ra, 1, 0)],
        [row(p['b_glu']), row(p['out_norm_ssm'])], [(D_SSM, BF16), (D_SSM, F32)], [D_SSM, D_SSM])
    dg = _mm("glu_bwd_x", dgl, wf['w_glu'], 'nt', TR, D_SSM, D_SSM, F32, res=dgd)
    big['w_glu'] = _mm("glu_bwd_w", g, dgl, 'tn', D_SSM, D_SSM, TR, BF16).reshape(N_DEV, D_SSM // N_DEV, D_SSM)
    dy_ssm, = _rowwise("gelu_bwd", _gelu_bwd_fn, [(dg, D_SSM, 0), (y_ssm, D_SSM, 0)], [], [(D_SSM, F32)], [])
    tr3 = lambda a: a.transpose(0, 2, 1)
    du_seg, dbre, dbim, dcre, dcim, da, sm['d_skip'] = _s5_bwd(
        u_seg, _to_segments(dy_ssm), a8, bre, bim, tr3(bre), tr3(bim), tr3(cre), tr3(cim), dsk)
    du = _from_segments(du_seg).astype(BF16)
    diag_b = lambda d: jnp.einsum('qgcgp->qgcp', d.reshape(nq, GQ, SSM_GROUP, GQ, SSM_STATE)).transpose(
        2, 0, 1, 3).reshape(SSM_GROUP, N_GROUPS, SSM_STATE)
    diag_c = lambda d: jnp.einsum('qgpgc->qgcp', d.reshape(nq, GQ, SSM_STATE, GQ, SSM_GROUP)).reshape(
        N_GROUPS * SSM_GROUP, SSM_STATE)
    dlam_re, dlam_im, dlog_dt, dbt_re, dbt_im = _s5_prep_bwd(
        lam_re, lam_im, log_dt, bt_re, bt_im, da[:, 0, :].reshape(N_GROUPS, SSM_STATE),
        da[:, 1, :].reshape(N_GROUPS, SSM_STATE), diag_b(dbre), diag_b(dbim))
    sm['lam_re'], sm['lam_im'], sm['log_dt'] = dlam_re, dlam_im, dlog_dt
    gpc = lambda d: d.transpose(1, 2, 0).reshape(N_GROUPS * SSM_STATE, SSM_GROUP)
    sm['b_re'], sm['b_im'] = gpc(dbt_re), gpc(dbt_im)
    sm['c_re'], sm['c_im'] = diag_c(dcre), diag_c(dcim)
    dyb, sm['out_norm_attn'] = _rowwise("rms_attn_bwd", _rms_bwd_fn, [(dy, D_SSM, 1), (yb, D_SSM, 0), (rb, 1, 0)],
                                        [row(p['out_norm_attn'])], [(D_SSM, F32)], [D_SSM])
    dq, dkv, dkpe = _attn_bwd(q, kv, z, tabs, dyb)
    big['w_q_b'] = _mm("q_bwd_w", qn, dq, 'tn', Q_LORA, HEAD_PAD, T, BF16, stack='b_out')
    dqn = _mm("q_bwd_x", dq, wf['w_q_b'], 'nt', TR, Q_LORA, HEAD_PAD, F32, stack='ab_red')
    big['w_kv_b'] = _mm("kv_bwd_w", kvn, dkv, 'tn', KV_LORA, HEAD_PAD, T, BF16, stack='b_out')
    dkvn = _mm("kv_bwd_x", dkv, wf['w_kv_b'], 'nt', TR, KV_LORA, HEAD_PAD, F32, stack='ab_red')
    dq_a, sm['q_a_norm'] = _rowwise("rms_q_bwd", _rms_bwd_fn,
                                    [(dqn, Q_LORA, 0), (z, Q_LORA, D_SSM // Q_LORA), (rq, 1, 0)],
                                    [row(p['q_a_norm'])], [(Q_LORA, BF16)], [Q_LORA])
    dkv_a, sm['kv_a_norm'] = _rowwise("rms_kv_bwd", _rms_bwd_fn,
                                      [(dkvn, KV_LORA, 0), (z, KV_LORA, (D_SSM + Q_LORA) // KV_LORA), (rkv, 1, 0)],
                                      [row(p['kv_a_norm'])], [(KV_LORA, BF16)], [KV_LORA])
    dk_pe = _kpe_bwd(dkpe, tabs)
    dz = jnp.concatenate([du, dq_a, dkv_a, dk_pe], axis=1)
    dxn = _mm("in_bwd_x", dz, wf['w_in'], 'nt', TR, 1024, D_IN_PAD, F32)
    big['w_in'] = _mm("in_bwd_w", xn, dz, 'tn', 1024, 640, TR, BF16).reshape(N_DEV, D // N_DEV, D_IN_PAD)
    dh0, sm['mix_norm'] = _rowwise("rms_mix_bwd", _rms_bwd_res_fn,
                                   [(dxn, D, 0), (h0, D, 0), (r0, 1, 0), (dh1, D, 0)], [row(p['mix_norm'])],
                                   [(D, F32)], [D])
    return loss, dh0[N_META:], dh0[:N_META], dcw4, big, sm


def _place():
    x, y, c = lax.axis_index("x"), lax.axis_index("y"), lax.axis_index("c")
    return x, y, c, [(1 - x, y), (x, 1 - y), (1 - x, 1 - y)]


_HBM = pl.BlockSpec(memory_space=pltpu.HBM)


def _all_gather(name, arrs):
    n = len(arrs)

    def body(*refs):
        ins, outs = refs[:n], refs[n:2 * n]
        send, recv, loc = refs[2 * n:]
        x, y, c, chips = _place()
        me, sib = (x, y, c), (x, y, 1 - c)

        def cp(a, k, blk, to, src=None):
            dst = outs[a].at[4 * blk[0] + 2 * blk[1] + blk[2]]
            return pltpu.make_async_remote_copy(src_ref=dst if src is None else src, dst_ref=dst,
                                                send_sem=send.at[a, k], recv_sem=recv.at[a, k],
                                                device_id=to, device_id_type=MESH)

        mine = [pltpu.make_async_copy(ins[a], outs[a].at[4 * x + 2 * y + c], loc.at[a]) for a in range(n)]
        for m in mine:
            m.start()
        first = []
        for a in range(n):
            first.append(cp(a, 0, me, sib, src=ins[a]))
            first += [cp(a, 1 + j, me, (*ch, c), src=ins[a]) for j, ch in enumerate(chips)]
        for f in first:
            f.start()
        passed = []
        for a in range(n):
            for j, ch in enumerate(chips):
                cp(a, 1 + j, (*ch, c), me).wait_recv()
                passed.append(cp(a, 4 + j, (*ch, c), sib))
                passed[-1].start()
        for a in range(n):
            cp(a, 0, sib, me).wait_recv()
            for j, ch in enumerate(chips):
                cp(a, 4 + j, (*ch, 1 - c), me).wait_recv()
        for f in first + passed:
            f.wait_send()
        for m in mine:
            m.wait()

    return pl.pallas_call(
        body, name=name, in_specs=[_HBM] * n, out_specs=[_HBM] * n,
        out_shape=[jax.ShapeDtypeStruct((N_DEV,) + a.shape, a.dtype) for a in arrs],
        scratch_shapes=[pltpu.SemaphoreType.DMA((n, 7)), pltpu.SemaphoreType.DMA((n, 7)),
                        pltpu.SemaphoreType.DMA((n,))],
    )(*arrs)


def _exchange_cores(name, arrs):
    n = len(arrs)

    def body(*refs):
        ins, outs = refs[:n], refs[n:2 * n]
        send, recv = refs[2 * n:]
        x, y, c, _ = _place()
        remote = [pltpu.make_async_remote_copy(src_ref=ins[a].at[2 * k + 1 - c], dst_ref=outs[a].at[k],
                                               send_sem=send.at[a, k], recv_sem=recv.at[a, k],
                                               device_id=(x, y, 1 - c), device_id_type=MESH)
                  for a in range(n) for k in range(4)]
        for d in remote:
            d.start()
        for d in remote:
            d.wait()

    return pl.pallas_call(
        body, name=name, in_specs=[_HBM] * n, out_specs=[_HBM] * n,
        out_shape=[jax.ShapeDtypeStruct((4,) + a.shape[1:], a.dtype) for a in arrs],
        scratch_shapes=[pltpu.SemaphoreType.DMA((n, 4))] * 2,
    )(*arrs)


def _exchange_chips(name, arrs):
    n = len(arrs)

    def body(*refs):
        ins, outs = refs[:n], refs[n:2 * n]
        send, recv = refs[2 * n:]
        x, y, c, chips = _place()
        remote = [pltpu.make_async_remote_copy(src_ref=ins[a].at[2 * ch[0] + ch[1]], dst_ref=outs[a].at[j],
                                               send_sem=send.at[a, j], recv_sem=recv.at[a, j],
                                               device_id=(*ch, c), device_id_type=MESH)
                  for a in range(n) for j, ch in enumerate(chips)]
        for d in remote:
            d.start()
        for d in remote:
            d.wait()

    return pl.pallas_call(
        body, name=name, in_specs=[_HBM] * n, out_specs=[_HBM] * n,
        out_shape=[jax.ShapeDtypeStruct((3,) + a.shape[1:], a.dtype) for a in arrs],
        scratch_shapes=[pltpu.SemaphoreType.DMA((n, 3))] * 2,
    )(*arrs)


def _blocks_of(r, c):
    if r * c * 4 <= (2 << 20):
        return r, c
    if r % 128 == 0:
        return 128, c
    return r, 256


def _pair_sum(name, own, got, core):
    _, r, c = got.shape
    rb, cb = _blocks_of(r, c)

    def body(s_ref, a_ref, b_ref, o_ref):
        o_ref[...] = (a_ref[...].astype(F32) + b_ref[...].astype(F32)).astype(o_ref.dtype)

    return pl.pallas_call(
        body, name=name, out_shape=jax.ShapeDtypeStruct((4, r, c), got.dtype),
        grid_spec=pltpu.PrefetchScalarGridSpec(
            num_scalar_prefetch=1, grid=(4, r // rb, c // cb),
            in_specs=[pl.BlockSpec((None, rb, cb), lambda k, i, j, s: (2 * k + s[0], i, j)),
                      pl.BlockSpec((None, rb, cb), lambda k, i, j, s: (k, i, j))],
            out_specs=pl.BlockSpec((None, rb, cb), lambda k, i, j, s: (k, i, j))),
        compiler_params=_params(3, 24 << 20),
    )(core, own, got)


def _adamw_math(w, g, m, v):
    m = B1 * m + (1.0 - B1) * g
    v = B2 * v + (1.0 - B2) * (g * g)
    m_hat = m / (1.0 - B1 ** STEP)
    v_hat = v / (1.0 - B2 ** STEP)
    return -LR * (m_hat / (jnp.sqrt(v_hat) + ADAM_EPS) + WD * w), m, v


def _adamw_big(name, own, got, chip, w, m, v):
    r, c = w.shape
    rb, cb = _blocks_of(r, c)

    def body(s_ref, o_ref, p_ref, w_ref, m_ref, v_ref, g_ref, d_ref, nm_ref, nv_ref):
        g = o_ref[...].astype(F32)
        for k in range(3):
            g = g + p_ref[k].astype(F32)
        g_ref[...] = g
        d_ref[...], nm_ref[...], nv_ref[...] = _adamw_math(w_ref[...], g, m_ref[...], v_ref[...])

    blk = pl.BlockSpec((rb, cb), lambda i, j, s: (i, j))
    return pl.pallas_call(
        body, name=name, out_shape=[jax.ShapeDtypeStruct((r, c), F32)] * 4,
        grid_spec=pltpu.PrefetchScalarGridSpec(
            num_scalar_prefetch=1, grid=(r // rb, c // cb),
            in_specs=[pl.BlockSpec((None, rb, cb), lambda i, j, s: (s[0], i, j)),
                      pl.BlockSpec((3, rb, cb), lambda i, j, s: (0, i, j)), blk, blk, blk],
            out_specs=[blk] * 4),
        compiler_params=_params(2, 40 << 20),
    )(chip, own, got, w, m, v)


def _adamw_multi(name, items, sums=(), row_blocks=1):
    n, ns = len(items), len(sums)

    def total(ref):
        g = ref[0]
        for d in range(1, ref.shape[0]):
            g = g + ref[d]
        return g

    def body(*refs):
        ins, outs = refs[:4 * n + ns], refs[4 * n + ns:]
        for t in range(n):
            p_ref, w_ref, m_ref, v_ref = ins[4 * t:4 * t + 4]
            g = total(p_ref)
            outs[4 * t][...] = g
            outs[4 * t + 1][...], outs[4 * t + 2][...], outs[4 * t + 3][...] = _adamw_math(
                w_ref[...], g, m_ref[...], v_ref[...])
        for t in range(ns):
            outs[4 * n + t][...] = total(ins[4 * n + t])

    def spec(shape, lead):
        blk = (shape[0] // row_blocks,) + tuple(shape[1:])
        nd = len(shape)
        if lead:
            return pl.BlockSpec((lead,) + blk, lambda i: (0, i) + (0,) * (nd - 1))
        return pl.BlockSpec(blk, lambda i: (i,) + (0,) * (nd - 1))

    operands, in_specs, out_specs, out_shape = [], [], [], []
    for parts, w, m, v in items:
        assert parts.shape[1:] == w.shape == m.shape == v.shape, (name, parts.shape, w.shape)
        operands += [parts, w, m, v]
        in_specs += [spec(w.shape, parts.shape[0])] + [spec(w.shape, 0)] * 3
        out_specs += [spec(w.shape, 0)] * 4
        out_shape += [jax.ShapeDtypeStruct(w.shape, F32)] * 4
    for parts in sums:
        operands.append(parts)
        in_specs.append(spec(parts.shape[1:], parts.shape[0]))
        out_specs.append(spec(parts.shape[1:], False))
        out_shape.append(jax.ShapeDtypeStruct(parts.shape[1:], F32))
    return pl.pallas_call(
        body, name=name, grid=(row_blocks,), in_specs=in_specs, out_specs=out_specs, out_shape=out_shape,
        compiler_params=_params(1, 56 << 20),
    )(*operands)


def kernel(x, meta_tokens, mix_norm, w_in, lam_re, lam_im, log_dt, b_re, b_im, c_re, c_im, d_skip, w_glu, b_glu, q_a_norm, w_q_b, kv_a_norm, w_kv_b, out_norm_ssm, out_norm_attn, w_out, ffn_norm, w_up, conv_w, conv_b, w_down, final_norm, loss_target, m_meta_tokens, m_mix_norm, m_w_in, m_lam_re, m_lam_im, m_log_dt, m_b_re, m_b_im, m_c_re, m_c_im, m_d_skip, m_w_glu, m_b_glu, m_q_a_norm, m_w_q_b, m_kv_a_norm, m_w_kv_b, m_out_norm_ssm, m_out_norm_attn, m_w_out, m_ffn_norm, m_w_up, m_conv_w, m_conv_b, m_w_down, m_final_norm, v_meta_tokens, v_mix_norm, v_w_in, v_lam_re, v_lam_im, v_log_dt, v_b_re, v_b_im, v_c_re, v_c_im, v_d_skip, v_w_glu, v_b_glu, v_q_a_norm, v_w_q_b, v_kv_a_norm, v_w_kv_b, v_out_norm_ssm, v_out_norm_attn, v_w_out, v_ffn_norm, v_w_up, v_conv_w, v_conv_b, v_w_down, v_final_norm):
    given = dict(locals())
    w = {n: given[n] for n in WEIGHTS}
    m = {n: given['m_' + n] for n in WEIGHTS}
    v = {n: given['v_' + n] for n in WEIGHTS}
    dev = 4 * lax.axis_index("x") + 2 * lax.axis_index("y") + lax.axis_index("c")

    shard = {
        'w_in': jnp.pad(w_in[0], ((0, 0), (0, D_IN_PAD - D_IN))),
        'w_glu': w_glu[0],
        'w_q_b': jnp.pad(w_q_b[0], ((0, 0), (0, HEAD_PAD - QK_NOPE - QK_ROPE))),
        'w_kv_b': w_kv_b[0],
        'w_out': w_out[0],
        'w_up': w_up[0],
        'w_down': w_down[0],
    }
    gathered = _all_gather("gather_weights", [shard[n].astype(BF16) for n in BIG]
                           + [meta_tokens, jnp.pad(conv_w[0], ((0, 5), (0, 0)))])
    wf = dict(zip(BIG, gathered[:len(BIG)]))
    for n in ('w_in', 'w_glu', 'w_out', 'w_down'):
        wf[n] = wf[n].reshape(-1, wf[n].shape[-1])
    meta = gathered[-2].transpose(1, 0, 2).reshape(N_META, D)
    wf['conv_w'] = gathered[-1][:, :3].transpose(1, 0, 2).reshape(3, D_FF)

    loss_part, grad_x, dmeta, dcw4, big, sm = _local_step(x[0], loss_target[0], meta, {n: w[n] for n in SMALL}, wf)
    grads, deltas, new_m, new_v = {}, {}, {}, {}

    def keep(n, outs):
        grads[n], deltas[n], new_m[n], new_v[n] = (o.reshape(w[n].shape) for o in outs)

    core = lax.axis_index("c").astype(jnp.int32).reshape(1)
    chip = (2 * lax.axis_index("x") + lax.axis_index("y")).astype(jnp.int32).reshape(1)
    from_core = _exchange_cores("reduce_cores", [big[n] for n in BIG])
    chip_sums = [_pair_sum("sum_cores_" + n, big[n], got, core) for n, got in zip(BIG, from_core)]
    from_chips = _exchange_chips("reduce_chips", chip_sums)
    for n, own, got in zip(BIG, chip_sums, from_chips):
        cols = w[n].shape[2]
        keep(n, _adamw_big("adamw_" + n, own[:, :, :cols], got[:, :, :cols], chip, w[n][0], m[n][0], v[n][0]))

    parts = _all_gather("gather_small", [sm[n] for n in SMALL] + [dmeta, dcw4, loss_part])
    items = {n: (parts[i],) + tuple(t[n].reshape(sm[n].shape) for t in (w, m, v)) for i, n in enumerate(SMALL)}
    wide = [n for n in SMALL if n not in ('b_re', 'b_im')]
    outs = _adamw_multi("adamw_small", [items[n] for n in wide], sums=parts[-3:])
    for i, n in enumerate(wide):
        keep(n, outs[4 * i:4 * i + 4])
    g_meta, g_cw4, loss = outs[-3:]
    outs = _adamw_multi("adamw_b", [items['b_re'], items['b_im']], row_blocks=8)
    keep('b_re', outs[:4])
    keep('b_im', outs[4:])

    g_meta = lax.dynamic_slice(g_meta, (0, dev * (D // N_DEV)), (N_META, D // N_DEV))
    g_conv = lax.dynamic_slice(g_cw4.transpose(1, 0, 2).reshape(3, D_FF), (0, dev * (D_FF // N_DEV)),
                               (3, D_FF // N_DEV))
    rows8 = lambda a: jnp.pad(a.reshape(3, D_FF // N_DEV), ((0, 5), (0, 0)))
    outs = _adamw_multi("adamw_cols", [(g_meta[None], meta_tokens, m['meta_tokens'], v['meta_tokens']),
                                       (rows8(g_conv)[None], rows8(conv_w), rows8(m['conv_w']), rows8(v['conv_w']))])
    keep('meta_tokens', outs[:4])
    keep('conv_w', [o[:3] for o in outs[4:]])

    return (loss[0, 0], grad_x[None], *[grads[n] for n in WEIGHTS], *[deltas[n] for n in WEIGHTS],
            *[new_m[n] for n in WEIGHTS], *[new_v[n] for n in WEIGHTS])
```

```python
import functools
import math

import jax
import jax.numpy as jnp
from jax import lax
from jax.experimental import pallas as pl
from jax.experimental.pallas import tpu as pltpu

F32, BF16 = jnp.float32, jnp.bfloat16

N_DEV = 8
N_META, SEQ, D = 16, 2048, 2048
T = N_META + SEQ
TR = 688
EPS = 1e-6
D_SSM, Q_LORA, KV_LORA, QK_ROPE = 1024, 512, 256, 64
D_IN = D_SSM + Q_LORA + KV_LORA + QK_ROPE
D_IN_PAD = 1920
N_HEADS, QK_NOPE, V_HEAD = 8, 128, 128
HEAD_PAD = 256
SM_SCALE = 1.0 / math.sqrt(QK_NOPE + QK_ROPE)
CHUNK = 64
N_GROUPS, SSM_GROUP, SSM_STATE = 64, 16, 64
N_SEG = 8
SEG = T // N_SEG
GQ = 8
D_FF = 5504
FF_PIECE = 1376
ROPE_BASE = 10000.0
LR, B1, B2, ADAM_EPS, WD, STEP = 0.001, 0.9, 0.999, 1e-08, 0.01, 10
VMEM_CAP = 60 * 1024 * 1024
MESH = pl.DeviceIdType.MESH

WEIGHTS = ['meta_tokens', 'mix_norm', 'w_in', 'lam_re', 'lam_im', 'log_dt', 'b_re', 'b_im', 'c_re', 'c_im',
           'd_skip', 'w_glu', 'b_glu', 'q_a_norm', 'w_q_b', 'kv_a_norm', 'w_kv_b', 'out_norm_ssm',
           'out_norm_attn', 'w_out', 'ffn_norm', 'w_up', 'conv_w', 'conv_b', 'w_down', 'final_norm']
BIG = ['w_in', 'w_glu', 'w_q_b', 'w_kv_b', 'w_out', 'w_up', 'w_down']
SMALL = [n for n in WEIGHTS if n not in BIG and n not in ('meta_tokens', 'conv_w')]


def _nbytes(shape, dtype):
    return math.prod(shape) * jnp.dtype(dtype).itemsize


def _params(sem, need):
    return pltpu.CompilerParams(dimension_semantics=("arbitrary",) * sem,
                                vmem_limit_bytes=int(min(VMEM_CAP, max(need, 16 * 1024 * 1024))))


_DIMS = {'nn': (((1,), (0,)), ((), ())), 'nt': (((1,), (1,)), ((), ())), 'tn': (((0,), (0,)), ((), ()))}


def _mm(name, a, b, mode, tm, tn, tk, out_dtype, stack=None, res=None):
    sa, sb, so = stack in ('a_out', 'ab_red'), stack in ('b_out', 'ab_red'), stack in ('a_out', 'b_out')
    nj = a.shape[0] if sa else (b.shape[0] if sb else 1)
    a2, b2 = a.shape[-2:], b.shape[-2:]
    if mode == 'nn':
        (m, k), (k2, n) = a2, b2
    elif mode == 'nt':
        (m, k), (n, k2) = a2, b2
    else:
        (k, m), (k2, n) = a2, b2
    assert k == k2 and m % tm == 0 and n % tn == 0 and k % tk == 0, (name, a.shape, b.shape)
    n_jo, n_jr, n_k = (nj if so else 1), (nj if stack == 'ab_red' else 1), k // tk
    grid = (n_jo, m // tm, n // tn, n_jr, n_k)
    a_blk, a_idx = ((tk, tm), lambda i, j, kk: (kk, i)) if mode == 'tn' else ((tm, tk), lambda i, j, kk: (i, kk))
    b_blk, b_idx = ((tn, tk), lambda i, j, kk: (j, kk)) if mode == 'nt' else ((tk, tn), lambda i, j, kk: (kk, j))

    def spec(blk, idx, stacked, on_out):
        if not stacked:
            return pl.BlockSpec(blk, lambda jo, i, j, jr, kk: idx(i, j, kk))
        return pl.BlockSpec((None,) + blk, lambda jo, i, j, jr, kk: ((jo if on_out else jr),) + idx(i, j, kk))

    o_idx = lambda i, j, kk: (i, j)
    in_specs = [spec(a_blk, a_idx, sa, stack == 'a_out'), spec(b_blk, b_idx, sb, stack == 'b_out')]
    operands = [a, b]
    if res is not None:
        in_specs.append(spec((tm, tn), o_idx, False, False))
        operands.append(res)
    n_red = n_jr * n_k
    dims = _DIMS[mode]

    def body(*refs):
        a_ref, b_ref = refs[0], refs[1]
        res_ref = refs[2] if res is not None else None
        o_ref = refs[3] if res is not None else refs[2]
        part = lax.dot_general(a_ref[...].astype(BF16), b_ref[...].astype(BF16), dims,
                               preferred_element_type=F32)

        def finish(total):
            if res_ref is not None:
                total = total + res_ref[...]
            o_ref[...] = total.astype(o_ref.dtype)

        if n_red == 1:
            finish(part)
        else:
            acc_ref = refs[-1]
            step = pl.program_id(3) * n_k + pl.program_id(4)

            @pl.when(step == 0)
            def _():
                acc_ref[...] = part

            @pl.when(step > 0)
            def _():
                acc_ref[...] += part

            @pl.when(step == n_red - 1)
            def _():
                finish(acc_ref[...])

    out_shape = ((nj,) if so else ()) + (m, n)
    need = 2 * (_nbytes(a_blk, a.dtype) + _nbytes(b_blk, b.dtype) + _nbytes((tm, tn), out_dtype)
                + (_nbytes((tm, tn), F32) if res is not None else 0)) + 3 * _nbytes((tm, tn), F32) + (4 << 20)
    return pl.pallas_call(
        body, name=name, grid=grid, in_specs=in_specs, out_specs=spec((tm, tn), o_idx, so, True),
        out_shape=jax.ShapeDtypeStruct(out_shape, out_dtype),
        scratch_shapes=[pltpu.VMEM((tm, tn), F32)] if n_red > 1 else [],
        compiler_params=_params(5, need),
    )(*operands)


def _rowwise(name, fn, row_ins, const_ins, row_outs, acc_outs, tr=TR):
    n_ri, n_ci, n_ro, n_ao = len(row_ins), len(const_ins), len(row_outs), len(acc_outs)

    def body(*refs):
        ri, ci = refs[:n_ri], refs[n_ri:n_ri + n_ci]
        ro, ao = refs[n_ri + n_ci:n_ri + n_ci + n_ro], refs[n_ri + n_ci + n_ro:]
        i = pl.program_id(0)
        outs = fn(i * tr, *[r[...] for r in ri], *[r[...] for r in ci])
        for r, v in zip(ro, outs[:n_ro]):
            r[...] = v.astype(r.dtype)
        if n_ao:
            @pl.when(i == 0)
            def _():
                for r, v in zip(ao, outs[n_ro:]):
                    r[...] = v

            @pl.when(i > 0)
            def _():
                for r, v in zip(ao, outs[n_ro:]):
                    r[...] += v

    in_specs = [pl.BlockSpec((tr, w), functools.partial(lambda cb, i: (i, cb), cb)) for _, w, cb in row_ins]
    in_specs += [pl.BlockSpec(c.shape, functools.partial(lambda nd, i: (0,) * nd, c.ndim)) for c in const_ins]
    out_specs = [pl.BlockSpec((tr, w), lambda i: (i, 0)) for w, _ in row_outs]
    out_specs += [pl.BlockSpec((1, w), lambda i: (0, 0)) for w in acc_outs]
    out_shape = [jax.ShapeDtypeStruct((T, w), dt) for w, dt in row_outs]
    out_shape += [jax.ShapeDtypeStruct((1, w), F32) for w in acc_outs]
    need = 2 * (sum(_nbytes((tr, w), a.dtype) for a, w, _ in row_ins) + sum(_nbytes((tr, w), dt) for w, dt in row_outs))
    need = 3 * need + (8 << 20)
    return pl.pallas_call(
        body, name=name, grid=(T // tr,), in_specs=in_specs, out_specs=out_specs, out_shape=out_shape,
        compiler_params=_params(1, need),
    )(*[a for a, _, _ in row_ins], *const_ins)


def _rms_fwd_fn(row0, h, g):
    h = h.astype(F32)
    r = lax.rsqrt(jnp.mean(h * h, axis=-1, keepdims=True) + EPS)
    return (h * r) * g, r


def _rms_bwd(dxn, h, r, g):
    xhat = h * r
    dxg = dxn * g
    dh = r * (dxg - xhat * jnp.mean(dxg * xhat, axis=-1, keepdims=True))
    return dh, jnp.sum(dxn * xhat, axis=0, keepdims=True)


def _rms_bwd_fn(row0, dxn, h, r, g):
    return _rms_bwd(dxn.astype(F32), h.astype(F32), r, g)


def _rms_bwd_res_fn(row0, dxn, h, r, res, g):
    dh, dg = _rms_bwd(dxn.astype(F32), h.astype(F32), r, g)
    return dh + res, dg


def _gelu_fn(row0, y):
    return (jax.nn.gelu(y),)


def _glu_fwd_fn(row0, y, gl, b, gain):
    ya = jax.nn.gelu(y) * jax.nn.sigmoid(gl + b)
    return _rms_fwd_fn(row0, ya, gain)


def _glu_bwd_fn(row0, dyan, y, gl, ra, b, gain):
    g = jax.nn.gelu(y)
    s = jax.nn.sigmoid(gl + b)
    dya, dgain = _rms_bwd(dyan, g * s, ra, gain)
    dgl = dya * g * s * (1.0 - s)
    return dgl, dya * s, dgain, jnp.sum(dgl, axis=0, keepdims=True)


def _gelu_bwd_fn(row0, dg, y):
    return (jax.vjp(jax.nn.gelu, y)[1](dg)[0],)


def _loss_fn(row0, h, tgt, gain):
    r = lax.rsqrt(jnp.mean(h * h, axis=-1, keepdims=True) + EPS)
    xhat = h * r
    rows = row0 + lax.broadcasted_iota(jnp.int32, h.shape, 0)
    err = jnp.where(rows >= N_META, xhat * gain - tgt, 0.0)
    loss = jnp.full((1, 128), jnp.sum(err * err) * (0.5 / D), F32)
    dh, dgain = _rms_bwd(err * (1.0 / D), h, r, gain)
    return dh, dgain, loss


def _rope_tables():
    pos = jnp.arange(T, dtype=jnp.int32)
    inv_freq = 1.0 / (ROPE_BASE ** (jnp.arange(0, QK_ROPE, 2, dtype=F32) / QK_ROPE))
    ang = pos.astype(F32)[:, None] * inv_freq[None, :]
    cos, sin, z32, z64 = jnp.cos(ang), jnp.sin(ang), jnp.zeros((T, 32), F32), jnp.zeros((T, 64), F32)
    return (jnp.concatenate([cos, cos, z64], 1), jnp.concatenate([-sin, z32, z64], 1),
            jnp.concatenate([z32, sin, z64], 1))


def _rot(x, cc, s1, s2):
    return x * cc + pltpu.roll(x, 96, 1) * s1 + pltpu.roll(x, 32, 1) * s2


def _derot(d, cc, s1, s2):
    return d * cc + pltpu.roll(d * s1, 32, 1) + pltpu.roll(d * s2, 96, 1)


def _chunk_of(pos):
    return jnp.where(pos < N_META, 0, 1 + ((pos - N_META) >> 6))


def _key_len(t):
    last_chunk = 1 + ((t + 1) * TR - 1 - N_META) // CHUNK
    return min(T, pl.cdiv(N_META + last_chunk * CHUNK, 128) * 128)


def _attn_probs(t, q_ref, kv_ref, kpe_ref, tq, tk):
    kl = _key_len(t)
    q = q_ref[...]
    qn = q[:, :QK_NOPE].astype(BF16)
    qp = _rot(q[:, QK_NOPE:], tq[0][...], tq[1][...], tq[2][...]).astype(BF16)
    kn = kv_ref[:kl, :QK_NOPE].astype(BF16)
    v = kv_ref[:kl, QK_NOPE:].astype(BF16)
    kp = _rot(kpe_ref[:kl, :], tk[0][:kl, :], tk[1][:kl, :], tk[2][:kl, :]).astype(BF16)
    s = lax.dot_general(qn, kn, _DIMS['nt'], preferred_element_type=F32)
    s = s + lax.dot_general(qp, kp, _DIMS['nt'], preferred_element_type=F32)
    qc = _chunk_of(t * TR + lax.broadcasted_iota(jnp.int32, (TR, 1), 0))
    kc = _chunk_of(lax.broadcasted_iota(jnp.int32, (1, kl), 1))
    s = jnp.where(kc <= qc, s * SM_SCALE, jnp.finfo(F32).min)
    p = jnp.exp(s - jnp.max(s, axis=-1, keepdims=True))
    p = p / jnp.sum(p, axis=-1, keepdims=True)
    return qn, qp, kn, kp, v, p


def _per_q_tile(fn):
    for t in range(T // TR):
        pl.when(pl.program_id(1) == t)(functools.partial(fn, t))


def _attn_specs(z, tabs):
    q_spec = pl.BlockSpec((None, TR, HEAD_PAD), lambda h, i: (h, i, 0))
    kv_spec = pl.BlockSpec((None, T, HEAD_PAD), lambda h, i: (h, 0, 0))
    kpe_spec = pl.BlockSpec((T, 128), lambda h, i: (0, (D_IN - QK_ROPE) // 128))
    tq = [pl.BlockSpec((TR, 128), lambda h, i: (i, 0))] * 3
    tk = [pl.BlockSpec((T, 128), lambda h, i: (0, 0))] * 3
    return [q_spec, kv_spec, kpe_spec] + tq + tk


def _attn_fwd(q, kv, z, tabs):
    def body(q_ref, kv_ref, kpe_ref, c1, c2, c3, k1, k2, k3, o_ref):
        def tile(t):
            _, _, _, _, v, p = _attn_probs(t, q_ref, kv_ref, kpe_ref, (c1, c2, c3), (k1, k2, k3))
            o_ref[...] = jnp.dot(p.astype(BF16), v, preferred_element_type=F32)

        _per_q_tile(tile)

    return pl.pallas_call(
        body, name="attn_fwd", grid=(N_HEADS, T // TR), in_specs=_attn_specs(z, tabs),
        out_specs=pl.BlockSpec((TR, V_HEAD), lambda h, i: (i, h)),
        out_shape=jax.ShapeDtypeStruct((T, N_HEADS * V_HEAD), F32),
        compiler_params=_params(2, 48 << 20),
    )(q, kv, z, *tabs, *tabs)


def _attn_bwd(q, kv, z, tabs, dyb):
    def body(q_ref, kv_ref, kpe_ref, c1, c2, c3, k1, k2, k3, do_ref, dq_ref, dkv_ref, dkpe_ref):
        @pl.when(pl.program_id(1) == 0)
        def _():
            dkv_ref[...] = jnp.zeros(dkv_ref.shape, F32)
            dkpe_ref[...] = jnp.zeros(dkpe_ref.shape, F32)

        def tile(t):
            kl = _key_len(t)
            qn, qp, kn, kp, v, p = _attn_probs(t, q_ref, kv_ref, kpe_ref, (c1, c2, c3), (k1, k2, k3))
            do = do_ref[...].astype(BF16)
            dp = lax.dot_general(do, v, _DIMS['nt'], preferred_element_type=F32)
            ds = (p * (dp - jnp.sum(p * dp, axis=-1, keepdims=True)) * SM_SCALE).astype(BF16)
            pb = p.astype(BF16)
            dq_ref[:, :QK_NOPE] = jnp.dot(ds, kn, preferred_element_type=F32).astype(dq_ref.dtype)
            dqp = _derot(jnp.dot(ds, kp, preferred_element_type=F32), c1[...], c2[...], c3[...])
            dq_ref[:, QK_NOPE:] = dqp.astype(dq_ref.dtype)
            dkv_ref[:kl, :QK_NOPE] += lax.dot_general(ds, qn, _DIMS['tn'], preferred_element_type=F32)
            dkv_ref[:kl, QK_NOPE:] += lax.dot_general(pb, do, _DIMS['tn'], preferred_element_type=F32)
            dkpe_ref[:kl, :] += lax.dot_general(ds, qp, _DIMS['tn'], preferred_element_type=F32)

        _per_q_tile(tile)

    return pl.pallas_call(
        body, name="attn_bwd", grid=(N_HEADS, T // TR),
        in_specs=_attn_specs(z, tabs) + [pl.BlockSpec((TR, V_HEAD), lambda h, i: (i, h))],
        out_specs=[pl.BlockSpec((None, TR, HEAD_PAD), lambda h, i: (h, i, 0)),
                   pl.BlockSpec((None, T, HEAD_PAD), lambda h, i: (h, 0, 0)),
                   pl.BlockSpec((None, T, 128), lambda h, i: (h, 0, 0))],
        out_shape=[jax.ShapeDtypeStruct((N_HEADS, T, HEAD_PAD), BF16),
                   jax.ShapeDtypeStruct((N_HEADS, T, HEAD_PAD), F32),
                   jax.ShapeDtypeStruct((N_HEADS, T, 128), F32)],
        compiler_params=_params(2, 56 << 20),
    )(q, kv, z, *tabs, *tabs, dyb)


def _kpe_bwd(dkpe, tabs):
    def body(d_ref, c1, c2, c3, o_ref):
        d = d_ref[0]
        for h in range(1, N_HEADS):
            d = d + d_ref[h]
        o_ref[...] = _derot(d, c1[...], c2[...], c3[...]).astype(o_ref.dtype)

    tile = pl.BlockSpec((TR, 128), lambda i: (i, 0))
    return pl.pallas_call(
        body, name="kpe_bwd", grid=(T // TR,),
        in_specs=[pl.BlockSpec((N_HEADS, TR, 128), lambda i: (0, i, 0)), tile, tile, tile], out_specs=tile,
        out_shape=jax.ShapeDtypeStruct((T, 128), BF16), compiler_params=_params(1, 16 << 20),
    )(dkpe, *tabs)


def _s5_prep_math(lam_re, lam_im, log_dt, bt_re, bt_im):
    dt = jnp.exp(log_dt)
    mag = jnp.exp(lam_re * dt)
    a_re, a_im = mag * jnp.cos(lam_im * dt), mag * jnp.sin(lam_im * dt)
    den = lam_re * lam_re + lam_im * lam_im
    w_re = ((a_re - 1.0) * lam_re + a_im * lam_im) / den
    w_im = (a_im * lam_re - (a_re - 1.0) * lam_im) / den
    return a_re, a_im, w_re[None] * bt_re - w_im[None] * bt_im, w_re[None] * bt_im + w_im[None] * bt_re


def _eye_groups():
    return (lax.broadcasted_iota(jnp.int32, (N_GROUPS, N_GROUPS), 0)
            == lax.broadcasted_iota(jnp.int32, (N_GROUPS, N_GROUPS), 1)).astype(F32)


def _row_to_col(row):
    return jnp.sum(_eye_groups() * row, axis=1, keepdims=True)


def _s5_prep(lam_re, lam_im, log_dt, bt_re, bt_im):
    def body(lr, li, ld, br, bi, ar, ai, bbr, bbi):
        ar[...], ai[...], bbr[...], bbi[...] = _s5_prep_math(lr[...], li[...], _row_to_col(ld[...]), br[...], bi[...])

    gp, cgp = jax.ShapeDtypeStruct((N_GROUPS, SSM_STATE), F32), jax.ShapeDtypeStruct(bt_re.shape, F32)
    return pl.pallas_call(body, name="s5_prep", out_shape=[gp, gp, cgp, cgp])(lam_re, lam_im, log_dt, bt_re, bt_im)


def _s5_prep_bwd(lam_re, lam_im, log_dt, bt_re, bt_im, da_re, da_im, dbb_re, dbb_im):
    def body(lr, li, ld, br, bi, dar, dai, dbr, dbi, o1, o2, o3, o4, o5):
        _, vjp = jax.vjp(_s5_prep_math, lr[...], li[...], _row_to_col(ld[...]), br[...], bi[...])
        o1[...], o2[...], dld, o4[...], o5[...] = vjp((dar[...], dai[...], dbr[...], dbi[...]))
        o3[...] = jnp.sum(_eye_groups() * dld, axis=0, keepdims=True)

    ins = (lam_re, lam_im, log_dt, bt_re, bt_im)
    return pl.pallas_call(body, name="s5_prep_bwd", out_shape=[jax.ShapeDtypeStruct(a.shape, F32) for a in ins])(
        *ins, da_re, da_im, dbb_re, dbb_im)


def _cmul(ar, ai, br, bi):
    return ar * br - ai * bi, ar * bi + ai * br


def _seg_rows(i):
    return pl.ds(pl.multiple_of(i * N_SEG, N_SEG), N_SEG)


def _scan(xr, xi, ar, ai, reverse):
    zero = jnp.zeros(ar.shape, F32)

    def local(j, carry):
        rows = _seg_rows(SEG - 1 - j if reverse else j)
        nr, ni = _cmul(ar, ai, *carry)
        nr, ni = nr + xr[rows, :], ni + xi[rows, :]
        xr[rows, :], xi[rows, :] = nr, ni
        return nr, ni

    er, ei = lax.fori_loop(0, SEG, local, (zero, zero))
    pr, pi = ar, ai
    for _ in range(8):
        pr, pi = _cmul(pr, pi, pr, pi)
    pr, pi = _cmul(*_cmul(pr, pi, ar, ai), ar, ai)
    row = lax.broadcasted_iota(jnp.int32, ar.shape, 0)
    edge, shift = (N_SEG - 1, N_SEG - 1) if reverse else (0, 1)
    hr, hi = zero, zero
    for _ in range(N_SEG - 1):
        tr_, ti_ = _cmul(pr, pi, hr, hi)
        hr = jnp.where(row == edge, 0.0, pltpu.roll(tr_ + er, shift, 0))
        hi = jnp.where(row == edge, 0.0, pltpu.roll(ti_ + ei, shift, 0))

    def fix(j, carry):
        rows = _seg_rows(SEG - 1 - j if reverse else j)
        xr[rows, :] += carry[0]
        xi[rows, :] += carry[1]
        return _cmul(ar, ai, *carry)

    lax.fori_loop(0, SEG, fix, _cmul(ar, ai, hr, hi))
    return hr, hi


def _split(x):
    hi = x.astype(BF16)
    return hi, (x - hi.astype(F32)).astype(BF16)


def _dot3(xs, ys, mode='nn'):
    d = lambda p, q: lax.dot_general(p, q, _DIMS[mode], preferred_element_type=F32)
    return d(xs[0], ys[0]) + (d(xs[1], ys[0]) + d(xs[0], ys[1]))


def _s5_hidden(us, a_ref, bre_ref, bim_ref, hre, him):
    hre[...] = _dot3(us, _split(bre_ref[...]))
    him[...] = _dot3(us, _split(bim_ref[...]))
    return _scan(hre, him, a_ref[0], a_ref[1], reverse=False)


def _s5_specs():
    col = pl.BlockSpec((T, 128), lambda q: (0, q))
    a_spec = pl.BlockSpec((None, 2, N_SEG, GQ * SSM_STATE), lambda q: (q, 0, 0, 0))
    wide = pl.BlockSpec((None, 128, GQ * SSM_STATE), lambda q: (q, 0, 0))
    tall = pl.BlockSpec((None, GQ * SSM_STATE, 128), lambda q: (q, 0, 0))
    d_spec = pl.BlockSpec((1, 128), lambda q: (0, q))
    return col, a_spec, wide, tall, d_spec


def _s5_fwd(u, a8, bre, bim, cre, cim, dsk):
    def body(u_ref, a_ref, bre_ref, bim_ref, cre_ref, cim_ref, d_ref, y_ref, hre, him):
        u_ = u_ref[...]
        _s5_hidden(_split(u_), a_ref, bre_ref, bim_ref, hre, him)
        y_ref[...] = (_dot3(_split(hre[...]), _split(cre_ref[...]))
                      - _dot3(_split(him[...]), _split(cim_ref[...])) + d_ref[...] * u_)

    col, a_spec, wide, tall, d_spec = _s5_specs()
    return pl.pallas_call(
        body, name="s5_fwd", grid=(N_GROUPS // GQ,), in_specs=[col, a_spec, wide, wide, tall, tall, d_spec],
        out_specs=col, out_shape=jax.ShapeDtypeStruct((T, D_SSM), F32),
        scratch_shapes=[pltpu.VMEM((T, GQ * SSM_STATE), F32)] * 2,
        compiler_params=_params(1, 40 << 20),
    )(u, a8, bre, bim, cre, cim, dsk)


def _s5_bwd(u, dy, a8, bre, bim, bre_t, bim_t, cre_t, cim_t, dsk):
    def body(u_ref, dy_ref, a_ref, bre_ref, bim_ref, bret_ref, bimt_ref, cret_ref, cimt_ref, d_ref,
             du_ref, dbre_ref, dbim_ref, dcre_ref, dcim_ref, da_ref, dd_ref, hre, him, lre, lim):
        u_, dy_ = u_ref[...], dy_ref[...]
        us, dys = _split(u_), _split(dy_)
        ar, ai = a_ref[0], a_ref[1]
        h0r, h0i = _s5_hidden(us, a_ref, bre_ref, bim_ref, hre, him)
        lre[...] = _dot3(dys, _split(cret_ref[...]))
        lim[...] = -_dot3(dys, _split(cimt_ref[...]))
        _scan(lre, lim, ar, -ai, reverse=True)

        def acc_da(i, carry):
            lr, li = lre[_seg_rows(i), :], lim[_seg_rows(i), :]
            pr, pi = hre[_seg_rows(i - 1), :], him[_seg_rows(i - 1), :]
            return carry[0] + lr * pr + li * pi, carry[1] + li * pr - lr * pi

        lr, li = lre[_seg_rows(0), :], lim[_seg_rows(0), :]
        dar, dai = lax.fori_loop(1, SEG, acc_da, (lr * h0r + li * h0i, li * h0r - lr * h0i))
        da_ref[0:1, :] = jnp.sum(dar, axis=0, keepdims=True)
        da_ref[1:2, :] = jnp.sum(dai, axis=0, keepdims=True)
        lrs, lis = _split(lre[...]), _split(lim[...])
        du_ref[...] = (dy_ * d_ref[...] + _dot3(lrs, _split(bret_ref[...])) + _dot3(lis, _split(bimt_ref[...])))
        dbre_ref[...] = _dot3(us, lrs, 'tn')
        dbim_ref[...] = _dot3(us, lis, 'tn')
        dcre_ref[...] = _dot3(_split(hre[...]), dys, 'tn')
        dcim_ref[...] = -_dot3(_split(him[...]), dys, 'tn')
        dd_ref[...] = jnp.sum(dy_ * u_, axis=0, keepdims=True)

    col, a_spec, wide, tall, d_spec = _s5_specs()
    nq, ns = N_GROUPS // GQ, GQ * SSM_STATE
    return pl.pallas_call(
        body, name="s5_bwd", grid=(nq,),
        in_specs=[col, col, a_spec, wide, wide, tall, tall, wide, wide, d_spec],
        out_specs=[col, wide, wide, tall, tall, pl.BlockSpec((None, 2, ns), lambda q: (q, 0, 0)), d_spec],
        out_shape=[jax.ShapeDtypeStruct((T, D_SSM), F32),
                   jax.ShapeDtypeStruct((nq, 128, ns), F32), jax.ShapeDtypeStruct((nq, 128, ns), F32),
                   jax.ShapeDtypeStruct((nq, ns, 128), F32), jax.ShapeDtypeStruct((nq, ns, 128), F32),
                   jax.ShapeDtypeStruct((nq, 2, ns), F32), jax.ShapeDtypeStruct((1, D_SSM), F32)],
        scratch_shapes=[pltpu.VMEM((T, ns), F32)] * 4,
        compiler_params=_params(1, 56 << 20),
    )(u, dy, a8, bre, bim, bre_t, bim_t, cre_t, cim_t, dsk)


def _to_segments(a):
    return a.reshape(N_SEG, SEG, a.shape[-1]).transpose(1, 0, 2).reshape(T, a.shape[-1])


def _from_segments(a):
    return a.reshape(SEG, N_SEG, a.shape[-1]).transpose(1, 0, 2).reshape(T, a.shape[-1])


def _block_diag_b(bbt):
    nq = N_GROUPS // GQ
    x = bbt.transpose(1, 0, 2).reshape(nq, GQ, SSM_GROUP, 1, SSM_STATE)
    eye = jnp.eye(GQ, dtype=F32)[None, :, None, :, None]
    return (x * eye).reshape(nq, GQ * SSM_GROUP, GQ * SSM_STATE)


def _block_diag_c(c):
    nq = N_GROUPS // GQ
    x = c.reshape(nq, GQ, SSM_GROUP, SSM_STATE).transpose(0, 1, 3, 2)[:, :, :, None, :]
    eye = jnp.eye(GQ, dtype=F32)[None, :, None, :, None]
    return (x * eye).reshape(nq, GQ * SSM_STATE, GQ * SSM_GROUP)


def _ff_specs():
    up_spec = pl.BlockSpec((2, None, T, 128), lambda p, cb: (0, p, 0, cb))
    piece = pl.BlockSpec((None, T, 128), lambda p, cb: (p, 0, cb))
    cw_spec = pl.BlockSpec((None, 3, 128), lambda p, cb: (p, 0, cb))
    cb_spec = pl.BlockSpec((None, 1, 128), lambda p, cb: (p, 0, cb))
    return up_spec, piece, cw_spec, cb_spec


def _conv_gate(gate, w, cb, pad_ref):
    pad_ref[0:8, :] = jnp.zeros((8, 128), F32)
    pad_ref[8:8 + T, :] = gate
    g1, g2 = pad_ref[pl.ds(7, T), :], pad_ref[pl.ds(6, T), :]
    return w[0:1, :] * g2 + w[1:2, :] * g1 + w[2:3, :] * gate + cb, g1, g2


def _ff_act(up, cw4, cb4):
    def body(up_ref, w_ref, b_ref, o_ref, pad_ref):
        gc, _, _ = _conv_gate(up_ref[0], w_ref[...], b_ref[...], pad_ref)
        o_ref[...] = (jax.nn.silu(gc) * up_ref[1]).astype(o_ref.dtype)

    up_spec, piece, cw_spec, cb_spec = _ff_specs()
    n_cb = pl.cdiv(FF_PIECE, 128)
    return pl.pallas_call(
        body, name="ff_act", grid=(4, n_cb), in_specs=[up_spec, cw_spec, cb_spec], out_specs=piece,
        out_shape=jax.ShapeDtypeStruct((4, T, FF_PIECE), BF16),
        scratch_shapes=[pltpu.VMEM((T + 8, 128), F32)],
        compiler_params=_params(2, 24 << 20),
    )(up.reshape(2, 4, T, FF_PIECE), cw4, cb4)


def _ff_act_bwd(up, dact, cw4, cb4):
    def body(up_ref, da_ref, w_ref, b_ref, dup_ref, dw_ref, db_ref, pad_ref, pad2_ref):
        gate, val, w = up_ref[0], up_ref[1], w_ref[...]
        gc, g1, g2 = _conv_gate(gate, w, b_ref[...], pad_ref)
        sg = jax.nn.sigmoid(gc)
        da = da_ref[...]
        dup_ref[1] = (da * gc * sg).astype(dup_ref.dtype)
        dgc = da * val * sg * (1.0 + gc * (1.0 - sg))
        db_ref[...] = jnp.sum(dgc, axis=0, keepdims=True)
        dw_ref[0:1, :] = jnp.sum(dgc * g2, axis=0, keepdims=True)
        dw_ref[1:2, :] = jnp.sum(dgc * g1, axis=0, keepdims=True)
        dw_ref[2:3, :] = jnp.sum(dgc * gate, axis=0, keepdims=True)
        pad2_ref[0:T, :] = dgc
        pad2_ref[T:T + 8, :] = jnp.zeros((8, 128), F32)
        d1, d2 = pad2_ref[pl.ds(1, T), :], pad2_ref[pl.ds(2, T), :]
        dup_ref[0] = (w[2:3, :] * dgc + w[1:2, :] * d1 + w[0:1, :] * d2).astype(dup_ref.dtype)

    up_spec, piece, cw_spec, cb_spec = _ff_specs()
    n_cb = pl.cdiv(FF_PIECE, 128)
    dup, dw, db = pl.pallas_call(
        body, name="ff_act_bwd", grid=(4, n_cb), in_specs=[up_spec, piece, cw_spec, cb_spec],
        out_specs=[up_spec, cw_spec, cb_spec],
        out_shape=[jax.ShapeDtypeStruct((2, 4, T, FF_PIECE), BF16), jax.ShapeDtypeStruct((4, 3, FF_PIECE), F32),
                   jax.ShapeDtypeStruct((4, 1, FF_PIECE), F32)],
        scratch_shapes=[pltpu.VMEM((T + 8, 128), F32)] * 2,
        compiler_params=_params(2, 32 << 20),
    )(up.reshape(2, 4, T, FF_PIECE), dact, cw4, cb4)
    return dup.reshape(8, T, FF_PIECE), dw, db


def _local_step(x, tgt, meta, p, wf):
    sm = {}
    h0 = jnp.concatenate([meta, x], axis=0)
    tgt_pad = jnp.concatenate([jnp.zeros((N_META, D), F32), tgt], axis=0)
    tabs = _rope_tables()
    row = lambda a: a.reshape(1, -1)

    xn, r0 = _rowwise("rms_mix", _rms_fwd_fn, [(h0, D, 0)], [row(p['mix_norm'])], [(D, BF16), (1, F32)], [])
    z = _mm("in_proj", xn, wf['w_in'], 'nn', TR, 640, D, F32)
    lam_re, lam_im = p['lam_re'].reshape(N_GROUPS, SSM_STATE), p['lam_im'].reshape(N_GROUPS, SSM_STATE)
    log_dt = p['log_dt'].reshape(1, N_GROUPS)
    bt_re = p['b_re'].reshape(N_GROUPS, SSM_STATE, SSM_GROUP).transpose(2, 0, 1)
    bt_im = p['b_im'].reshape(N_GROUPS, SSM_STATE, SSM_GROUP).transpose(2, 0, 1)
    c_re = p['c_re'].reshape(N_GROUPS, SSM_GROUP, SSM_STATE)
    c_im = p['c_im'].reshape(N_GROUPS, SSM_GROUP, SSM_STATE)
    a_re, a_im, bbt_re, bbt_im = _s5_prep(lam_re, lam_im, log_dt, bt_re, bt_im)
    nq, ns = N_GROUPS // GQ, GQ * SSM_STATE
    a8 = jnp.broadcast_to(jnp.stack([a_re.reshape(nq, ns), a_im.reshape(nq, ns)], 1)[:, :, None, :],
                          (nq, 2, N_SEG, ns))
    bre, bim = _block_diag_b(bbt_re), _block_diag_b(bbt_im)
    cre, cim = _block_diag_c(c_re), _block_diag_c(c_im)
    dsk = row(p['d_skip'])
    u_seg = _to_segments(z[:, :D_SSM])
    y_ssm = _from_segments(_s5_fwd(u_seg, a8, bre, bim, cre, cim, dsk))
    g, = _rowwise("gelu", _gelu_fn, [(y_ssm, D_SSM, 0)], [], [(D_SSM, BF16)], [])
    gl = _mm("glu_proj", g, wf['w_glu'], 'nn', TR, D_SSM, D_SSM, F32)
    yan, ra = _rowwise("glu_norm", _glu_fwd_fn, [(y_ssm, D_SSM, 0), (gl, D_SSM, 0)],
                       [row(p['b_glu']), row(p['out_norm_ssm'])], [(D_SSM, BF16), (1, F32)], [])
    qn, rq = _rowwise("rms_q", _rms_fwd_fn, [(z, Q_LORA, D_SSM // Q_LORA)], [row(p['q_a_norm'])],
                      [(Q_LORA, BF16), (1, F32)], [])
    kvn, rkv = _rowwise("rms_kv", _rms_fwd_fn, [(z, KV_LORA, (D_SSM + Q_LORA) // KV_LORA)], [row(p['kv_a_norm'])],
                        [(KV_LORA, BF16), (1, F32)], [])
    q = _mm("q_proj", qn, wf['w_q_b'], 'nn', T, HEAD_PAD, Q_LORA, F32, stack='b_out')
    kv = _mm("kv_proj", kvn, wf['w_kv_b'], 'nn', T, HEAD_PAD, KV_LORA, F32, stack='b_out')
    yb = _attn_fwd(q, kv, z, tabs)
    ybn, rb = _rowwise("rms_attn", _rms_fwd_fn, [(yb, D_SSM, 0)], [row(p['out_norm_attn'])],
                       [(D_SSM, BF16), (1, F32)], [])
    y = jnp.concatenate([yan, ybn], axis=1)
    h1 = _mm("out_proj", y, wf['w_out'], 'nn', TR, 1024, D, F32, res=h0)
    xn2, r1 = _rowwise("rms_ffn", _rms_fwd_fn, [(h1, D, 0)], [row(p['ffn_norm'])], [(D, BF16), (1, F32)], [])
    up = _mm("up_proj", xn2, wf['w_up'], 'nn', TR, FF_PIECE, D, F32, stack='b_out')
    cw4 = wf['conv_w'].reshape(3, 4, FF_PIECE).transpose(1, 0, 2)
    cb4 = p['conv_b'].reshape(4, 1, FF_PIECE)
    act = _ff_act(up, cw4, cb4)
    wd4 = wf['w_down'].reshape(4, FF_PIECE, D)
    h2 = _mm("down_proj", act, wd4, 'nn', TR, 1024, FF_PIECE, F32, stack='ab_red', res=h1)
    dh2, sm['final_norm'], loss = _rowwise("loss", _loss_fn, [(h2, D, 0), (tgt_pad, D, 0)], [row(p['final_norm'])],
                                           [(D, F32)], [D, 128])

    big = {}
    dact = _mm("down_bwd_x", dh2, wd4, 'nt', TR, FF_PIECE, D, F32, stack='b_out')
    big['w_down'] = _mm("down_bwd_w", act, dh2, 'tn', FF_PIECE, 1024, TR, BF16, stack='a_out').reshape(
        N_DEV, D_FF // N_DEV, D)
    dup, dcw4, sm['conv_b'] = _ff_act_bwd(up, dact, cw4, cb4)
    dxn2 = _mm("up_bwd_x", dup, wf['w_up'], 'nt', TR, 1024, FF_PIECE, F32, stack='ab_red')
    big['w_up'] = _mm("up_bwd_w", xn2, dup, 'tn', 1024, FF_PIECE, TR, BF16, stack='b_out')
    dh1, sm['ffn_norm'] = _rowwise("rms_ffn_bwd", _rms_bwd_res_fn,
                                   [(dxn2, D, 0), (h1, D, 0), (r1, 1, 0), (dh2, D, 0)], [row(p['ffn_norm'])],
                                   [(D, F32)], [D])
    dy = _mm("out_bwd_x", dh1, wf['w_out'], 'nt', TR, 1024, D, F32)
    big['w_out'] = _mm("out_bwd_w", y, dh1, 'tn', 1024, 1024, TR, BF16).reshape(N_DEV, D // N_DEV, D)
    dgl, dgd, sm['out_norm_ssm'], sm['b_glu'] = _rowwise(
        "glu_bwd", _glu_bwd_fn, [(dy, D_SSM, 0), (y_ssm, D_SSM, 0), (gl, D_SSM, 0), (ra, 1, 0)],
        [row(p['b_glu']), row(p['out_norm_ssm'])], [(D_SSM, BF16), (D_SSM, F32)], [D_SSM, D_SSM])
    dg = _mm("glu_bwd_x", dgl, wf['w_glu'], 'nt', TR, D_SSM, D_SSM, F32, res=dgd)
    big['w_glu'] = _mm("glu_bwd_w", g, dgl, 'tn', D_SSM, D_SSM, TR, BF16).reshape(N_DEV, D_SSM // N_DEV, D_SSM)
    dy_ssm, = _rowwise("gelu_bwd", _gelu_bwd_fn, [(dg, D_SSM, 0), (y_ssm, D_SSM, 0)], [], [(D_SSM, F32)], [])
    tr3 = lambda a: a.transpose(0, 2, 1)
    du_seg, dbre, dbim, dcre, dcim, da, sm['d_skip'] = _s5_bwd(
        u_seg, _to_segments(dy_ssm), a8, bre, bim, tr3(bre), tr3(bim), tr3(cre), tr3(cim), dsk)
    du = _from_segments(du_seg).astype(BF16)
    diag_b = lambda d: jnp.einsum('qgcgp->qgcp', d.reshape(nq, GQ, SSM_GROUP, GQ, SSM_STATE)).transpose(
        2, 0, 1, 3).reshape(SSM_GROUP, N_GROUPS, SSM_STATE)
    diag_c = lambda d: jnp.einsum('qgpgc->qgcp', d.reshape(nq, GQ, SSM_STATE, GQ, SSM_GROUP)).reshape(
        N_GROUPS * SSM_GROUP, SSM_STATE)
    dlam_re, dlam_im, dlog_dt, dbt_re, dbt_im = _s5_prep_bwd(
        lam_re, lam_im, log_dt, bt_re, bt_im, da[:, 0, :].reshape(N_GROUPS, SSM_STATE),
        da[:, 1, :].reshape(N_GROUPS, SSM_STATE), diag_b(dbre), diag_b(dbim))
    sm['lam_re'], sm['lam_im'], sm['log_dt'] = dlam_re, dlam_im, dlog_dt
    cgp = lambda d: d.reshape(SSM_GROUP * N_GROUPS, SSM_STATE)
    sm['b_re'], sm['b_im'] = cgp(dbt_re), cgp(dbt_im)
    sm['c_re'], sm['c_im'] = diag_c(dcre), diag_c(dcim)
    dyb, sm['out_norm_attn'] = _rowwise("rms_attn_bwd", _rms_bwd_fn, [(dy, D_SSM, 1), (yb, D_SSM, 0), (rb, 1, 0)],
                                        [row(p['out_norm_attn'])], [(D_SSM, F32)], [D_SSM])
    dq, dkv, dkpe = _attn_bwd(q, kv, z, tabs, dyb)
    big['w_q_b'] = _mm("q_bwd_w", qn, dq, 'tn', Q_LORA, HEAD_PAD, T, BF16, stack='b_out')
    dqn = _mm("q_bwd_x", dq, wf['w_q_b'], 'nt', TR, Q_LORA, HEAD_PAD, F32, stack='ab_red')
    big['w_kv_b'] = _mm("kv_bwd_w", kvn, dkv, 'tn', KV_LORA, HEAD_PAD, T, BF16, stack='b_out')
    dkvn = _mm("kv_bwd_x", dkv, wf['w_kv_b'], 'nt', TR, KV_LORA, HEAD_PAD, F32, stack='ab_red')
    dq_a, sm['q_a_norm'] = _rowwise("rms_q_bwd", _rms_bwd_fn,
                                    [(dqn, Q_LORA, 0), (z, Q_LORA, D_SSM // Q_LORA), (rq, 1, 0)],
                                    [row(p['q_a_norm'])], [(Q_LORA, BF16)], [Q_LORA])
    dkv_a, sm['kv_a_norm'] = _rowwise("rms_kv_bwd", _rms_bwd_fn,
                                      [(dkvn, KV_LORA, 0), (z, KV_LORA, (D_SSM + Q_LORA) // KV_LORA), (rkv, 1, 0)],
                                      [row(p['kv_a_norm'])], [(KV_LORA, BF16)], [KV_LORA])
    dk_pe = _kpe_bwd(dkpe, tabs)
    dz = jnp.concatenate([du, dq_a, dkv_a, dk_pe], axis=1)
    dxn = _mm("in_bwd_x", dz, wf['w_in'], 'nt', TR, 1024, D_IN_PAD, F32)
    big['w_in'] = _mm("in_bwd_w", xn, dz, 'tn', 1024, 640, TR, BF16).reshape(N_DEV, D // N_DEV, D_IN_PAD)
    dh0, sm['mix_norm'] = _rowwise("rms_mix_bwd", _rms_bwd_res_fn,
                                   [(dxn, D, 0), (h0, D, 0), (r0, 1, 0), (dh1, D, 0)], [row(p['mix_norm'])],
                                   [(D, F32)], [D])
    return loss, dh0[N_META:], dh0[:N_META], dcw4, big, sm


def _place():
    x, y, c = lax.axis_index("x"), lax.axis_index("y"), lax.axis_index("c")
    return x, y, c, [(1 - x, y), (x, 1 - y), (1 - x, 1 - y)]


_HBM = pl.BlockSpec(memory_space=pltpu.HBM)


def _all_gather(name, arrs):
    n = len(arrs)

    def body(*refs):
        ins, outs = refs[:n], refs[n:2 * n]
        send, recv, loc = refs[2 * n:]
        x, y, c, chips = _place()
        me, sib = (x, y, c), (x, y, 1 - c)

        def cp(a, k, blk, to, src=None):
            dst = outs[a].at[4 * blk[0] + 2 * blk[1] + blk[2]]
            return pltpu.make_async_remote_copy(src_ref=dst if src is None else src, dst_ref=dst,
                                                send_sem=send.at[a, k], recv_sem=recv.at[a, k],
                                                device_id=to, device_id_type=MESH)

        mine = [pltpu.make_async_copy(ins[a], outs[a].at[4 * x + 2 * y + c], loc.at[a]) for a in range(n)]
        for m in mine:
            m.start()
        first = []
        for a in range(n):
            first.append(cp(a, 0, me, sib, src=ins[a]))
            first += [cp(a, 1 + j, me, (*ch, c), src=ins[a]) for j, ch in enumerate(chips)]
        for f in first:
            f.start()
        passed = []
        for a in range(n):
            for j, ch in enumerate(chips):
                cp(a, 1 + j, (*ch, c), me).wait_recv()
                passed.append(cp(a, 4 + j, (*ch, c), sib))
                passed[-1].start()
        for a in range(n):
            cp(a, 0, sib, me).wait_recv()
            for j, ch in enumerate(chips):
                cp(a, 4 + j, (*ch, 1 - c), me).wait_recv()
        for f in first + passed:
            f.wait_send()
        for m in mine:
            m.wait()

    return pl.pallas_call(
        body, name=name, in_specs=[_HBM] * n, out_specs=[_HBM] * n,
        out_shape=[jax.ShapeDtypeStruct((N_DEV,) + a.shape, a.dtype) for a in arrs],
        scratch_shapes=[pltpu.SemaphoreType.DMA((n, 7)), pltpu.SemaphoreType.DMA((n, 7)),
                        pltpu.SemaphoreType.DMA((n,))],
    )(*arrs)


def _exchange_cores(name, arrs):
    n = len(arrs)

    def body(*refs):
        ins, outs = refs[:n], refs[n:2 * n]
        send, recv = refs[2 * n:]
        x, y, c, _ = _place()
        remote = [pltpu.make_async_remote_copy(src_ref=ins[a].at[2 * k + 1 - c], dst_ref=outs[a].at[k],
                                               send_sem=send.at[a, k], recv_sem=recv.at[a, k],
                                               device_id=(x, y, 1 - c), device_id_type=MESH)
                  for a in range(n) for k in range(4)]
        for d in remote:
            d.start()
        for d in remote:
            d.wait()

    return pl.pallas_call(
        body, name=name, in_specs=[_HBM] * n, out_specs=[_HBM] * n,
        out_shape=[jax.ShapeDtypeStruct((4,) + a.shape[1:], a.dtype) for a in arrs],
        scratch_shapes=[pltpu.SemaphoreType.DMA((n, 4))] * 2,
    )(*arrs)


def _exchange_chips(name, arrs):
    n = len(arrs)

    def body(*refs):
        ins, outs = refs[:n], refs[n:2 * n]
        send, recv = refs[2 * n:]
        x, y, c, chips = _place()
        remote = [pltpu.make_async_remote_copy(src_ref=ins[a].at[2 * ch[0] + ch[1]], dst_ref=outs[a].at[j],
                                               send_sem=send.at[a, j], recv_sem=recv.at[a, j],
                                               device_id=(*ch, c), device_id_type=MESH)
                  for a in range(n) for j, ch in enumerate(chips)]
        for d in remote:
            d.start()
        for d in remote:
            d.wait()

    return pl.pallas_call(
        body, name=name, in_specs=[_HBM] * n, out_specs=[_HBM] * n,
        out_shape=[jax.ShapeDtypeStruct((3,) + a.shape[1:], a.dtype) for a in arrs],
        scratch_shapes=[pltpu.SemaphoreType.DMA((n, 3))] * 2,
    )(*arrs)


def _blocks_of(r, c):
    if r * c * 4 <= (2 << 20):
        return r, c
    if r % 128 == 0:
        return 128, c
    return r, 256


def _pair_sum(name, own, got, core):
    _, r, c = got.shape
    rb, cb = _blocks_of(r, c)

    def body(s_ref, a_ref, b_ref, o_ref):
        o_ref[...] = (a_ref[...].astype(F32) + b_ref[...].astype(F32)).astype(o_ref.dtype)

    return pl.pallas_call(
        body, name=name, out_shape=jax.ShapeDtypeStruct((4, r, c), got.dtype),
        grid_spec=pltpu.PrefetchScalarGridSpec(
            num_scalar_prefetch=1, grid=(4, r // rb, c // cb),
            in_specs=[pl.BlockSpec((None, rb, cb), lambda k, i, j, s: (2 * k + s[0], i, j)),
                      pl.BlockSpec((None, rb, cb), lambda k, i, j, s: (k, i, j))],
            out_specs=pl.BlockSpec((None, rb, cb), lambda k, i, j, s: (k, i, j))),
        compiler_params=_params(3, 24 << 20),
    )(core, own, got)


def _adamw_math(w, g, m, v):
    m = B1 * m + (1.0 - B1) * g
    v = B2 * v + (1.0 - B2) * (g * g)
    m_hat = m / (1.0 - B1 ** STEP)
    v_hat = v / (1.0 - B2 ** STEP)
    return -LR * (m_hat / (jnp.sqrt(v_hat) + ADAM_EPS) + WD * w), m, v


def _adamw_big(name, own, got, chip, w, m, v):
    r, c = w.shape
    rb, cb = _blocks_of(r, c)

    def body(s_ref, o_ref, p_ref, w_ref, m_ref, v_ref, g_ref, d_ref, nm_ref, nv_ref):
        g = o_ref[...].astype(F32)
        for k in range(3):
            g = g + p_ref[k].astype(F32)
        g_ref[...] = g
        d_ref[...], nm_ref[...], nv_ref[...] = _adamw_math(w_ref[...], g, m_ref[...], v_ref[...])

    blk = pl.BlockSpec((rb, cb), lambda i, j, s: (i, j))
    return pl.pallas_call(
        body, name=name, out_shape=[jax.ShapeDtypeStruct((r, c), F32)] * 4,
        grid_spec=pltpu.PrefetchScalarGridSpec(
            num_scalar_prefetch=1, grid=(r // rb, c // cb),
            in_specs=[pl.BlockSpec((None, rb, cb), lambda i, j, s: (s[0], i, j)),
                      pl.BlockSpec((3, rb, cb), lambda i, j, s: (0, i, j)), blk, blk, blk],
            out_specs=[blk] * 4),
        compiler_params=_params(2, 40 << 20),
    )(chip, own, got, w, m, v)


def _adamw_multi(name, items, sums=(), row_blocks=1):
    n, ns = len(items), len(sums)

    def total(ref):
        g = ref[0]
        for d in range(1, ref.shape[0]):
            g = g + ref[d]
        return g

    def body(*refs):
        ins, outs = refs[:4 * n + ns], refs[4 * n + ns:]
        for t in range(n):
            p_ref, w_ref, m_ref, v_ref = ins[4 * t:4 * t + 4]
            g = total(p_ref)
            outs[4 * t][...] = g
            outs[4 * t + 1][...], outs[4 * t + 2][...], outs[4 * t + 3][...] = _adamw_math(
                w_ref[...], g, m_ref[...], v_ref[...])
        for t in range(ns):
            outs[4 * n + t][...] = total(ins[4 * n + t])

    def spec(shape, lead):
        blk = (shape[0] // row_blocks,) + tuple(shape[1:])
        nd = len(shape)
        if lead:
            return pl.BlockSpec((lead,) + blk, lambda i: (0, i) + (0,) * (nd - 1))
        return pl.BlockSpec(blk, lambda i: (i,) + (0,) * (nd - 1))

    operands, in_specs, out_specs, out_shape = [], [], [], []
    for parts, w, m, v in items:
        assert parts.shape[1:] == w.shape == m.shape == v.shape, (name, parts.shape, w.shape)
        operands += [parts, w, m, v]
        in_specs += [spec(w.shape, parts.shape[0])] + [spec(w.shape, 0)] * 3
        out_specs += [spec(w.shape, 0)] * 4
        out_shape += [jax.ShapeDtypeStruct(w.shape, F32)] * 4
    for parts in sums:
        operands.append(parts)
        in_specs.append(spec(parts.shape[1:], parts.shape[0]))
        out_specs.append(spec(parts.shape[1:], False))
        out_shape.append(jax.ShapeDtypeStruct(parts.shape[1:], F32))
    return pl.pallas_call(
        body, name=name, grid=(row_blocks,), in_specs=in_specs, out_specs=out_specs, out_shape=out_shape,
        compiler_params=_params(1, 56 << 20),
    )(*operands)


def kernel(x, meta_tokens, mix_norm, w_in, lam_re, lam_im, log_dt, b_re, b_im, c_re, c_im, d_skip, w_glu, b_glu, q_a_norm, w_q_b, kv_a_norm, w_kv_b, out_norm_ssm, out_norm_attn, w_out, ffn_norm, w_up, conv_w, conv_b, w_down, final_norm, loss_target, m_meta_tokens, m_mix_norm, m_w_in, m_lam_re, m_lam_im, m_log_dt, m_b_re, m_b_im, m_c_re, m_c_im, m_d_skip, m_w_glu, m_b_glu, m_q_a_norm, m_w_q_b, m_kv_a_norm, m_w_kv_b, m_out_norm_ssm, m_out_norm_attn, m_w_out, m_ffn_norm, m_w_up, m_conv_w, m_conv_b, m_w_down, m_final_norm, v_meta_tokens, v_mix_norm, v_w_in, v_lam_re, v_lam_im, v_log_dt, v_b_re, v_b_im, v_c_re, v_c_im, v_d_skip, v_w_glu, v_b_glu, v_q_a_norm, v_w_q_b, v_kv_a_norm, v_w_kv_b, v_out_norm_ssm, v_out_norm_attn, v_w_out, v_ffn_norm, v_w_up, v_conv_w, v_conv_b, v_w_down, v_final_norm):
    given = dict(locals())
    w = {n: given[n] for n in WEIGHTS}
    m = {n: given['m_' + n] for n in WEIGHTS}
    v = {n: given['v_' + n] for n in WEIGHTS}
    dev = 4 * lax.axis_index("x") + 2 * lax.axis_index("y") + lax.axis_index("c")

    shard = {
        'w_in': jnp.pad(w_in[0], ((0, 0), (0, D_IN_PAD - D_IN))),
        'w_glu': w_glu[0],
        'w_q_b': jnp.pad(w_q_b[0], ((0, 0), (0, HEAD_PAD - QK_NOPE - QK_ROPE))),
        'w_kv_b': w_kv_b[0],
        'w_out': w_out[0],
        'w_up': w_up[0],
        'w_down': w_down[0],
    }
    gathered = _all_gather("gather_weights", [shard[n].astype(BF16) for n in BIG]
                           + [meta_tokens, jnp.pad(conv_w[0], ((0, 5), (0, 0)))])
    wf = dict(zip(BIG, gathered[:len(BIG)]))
    for n in ('w_in', 'w_glu', 'w_out', 'w_down'):
        wf[n] = wf[n].reshape(-1, wf[n].shape[-1])
    meta = gathered[-2].transpose(1, 0, 2).reshape(N_META, D)
    wf['conv_w'] = gathered[-1][:, :3].transpose(1, 0, 2).reshape(3, D_FF)

    loss_part, grad_x, dmeta, dcw4, big, sm = _local_step(x[0], loss_target[0], meta, {n: w[n] for n in SMALL}, wf)
    grads, deltas, new_m, new_v = {}, {}, {}, {}

    def keep(n, outs):
        grads[n], deltas[n], new_m[n], new_v[n] = (o.reshape(w[n].shape) for o in outs)

    core = lax.axis_index("c").astype(jnp.int32).reshape(1)
    chip = (2 * lax.axis_index("x") + lax.axis_index("y")).astype(jnp.int32).reshape(1)
    from_core = _exchange_cores("reduce_cores", [big[n] for n in BIG])
    chip_sums = [_pair_sum("sum_cores_" + n, big[n], got, core) for n, got in zip(BIG, from_core)]
    from_chips = _exchange_chips("reduce_chips", chip_sums)
    for n, own, got in zip(BIG, chip_sums, from_chips):
        cols = w[n].shape[2]
        keep(n, _adamw_big("adamw_" + n, own[:, :, :cols], got[:, :, :cols], chip, w[n][0], m[n][0], v[n][0]))

    parts = dict(zip(SMALL + ['meta_tokens', 'conv_w', 'loss'],
                     _all_gather("gather_small", [sm[n] for n in SMALL] + [dmeta, dcw4, loss_part])))
    wide = [n for n in SMALL if n not in ('b_re', 'b_im')]
    outs = _adamw_multi("adamw_small",
                        [(parts[n],) + tuple(t[n].reshape(sm[n].shape) for t in (w, m, v)) for n in wide],
                        sums=[parts[n] for n in ('meta_tokens', 'conv_w', 'loss', 'b_re', 'b_im')])
    for i, n in enumerate(wide):
        keep(n, outs[4 * i:4 * i + 4])
    g_meta, g_cw4, loss, g_bre, g_bim = outs[-5:]
    gpc = lambda g: g.reshape(SSM_GROUP, N_GROUPS, SSM_STATE).transpose(1, 2, 0).reshape(-1, SSM_GROUP)[None]
    outs = _adamw_multi("adamw_b", [(gpc(g),) + tuple(t[n].reshape(-1, SSM_GROUP) for t in (w, m, v))
                                    for n, g in (('b_re', g_bre), ('b_im', g_bim))], row_blocks=8)
    keep('b_re', outs[:4])
    keep('b_im', outs[4:])

    g_meta = lax.dynamic_slice(g_meta, (0, dev * (D // N_DEV)), (N_META, D // N_DEV))
    g_conv = lax.dynamic_slice(g_cw4.transpose(1, 0, 2).reshape(3, D_FF), (0, dev * (D_FF // N_DEV)),
                               (3, D_FF // N_DEV))
    rows8 = lambda a: jnp.pad(a.reshape(3, D_FF // N_DEV), ((0, 5), (0, 0)))
    outs = _adamw_multi("adamw_cols", [(g_meta[None], meta_tokens, m['meta_tokens'], v['meta_tokens']),
                                       (rows8(g_conv)[None], rows8(conv_w), rows8(m['conv_w']), rows8(v['conv_w']))])
    keep('meta_tokens', outs[:4])
    keep('conv_w', [o[:3] for o in outs[4:]])

    return (loss[0, 0], grad_x[None], *[grads[n] for n in WEIGHTS], *[deltas[n] for n in WEIGHTS],
            *[new_m[n] for n in WEIGHTS], *[new_v[n] for n in WEIGHTS])
```

```python
import functools
import math

import jax
import jax.numpy as jnp
from jax import lax
from jax.experimental import pallas as pl
from jax.experimental.pallas import tpu as pltpu

F32, BF16 = jnp.float32, jnp.bfloat16

N_DEV = 8
N_META, SEQ, D = 16, 2048, 2048
T = N_META + SEQ
TR = 688
EPS = 1e-6
D_SSM, Q_LORA, KV_LORA, QK_ROPE = 1024, 512, 256, 64
D_IN = D_SSM + Q_LORA + KV_LORA + QK_ROPE
D_IN_PAD = 1920
N_HEADS, QK_NOPE, V_HEAD = 8, 128, 128
HEAD_PAD = 256
SM_SCALE = 1.0 / math.sqrt(QK_NOPE + QK_ROPE)
CHUNK = 64
N_GROUPS, SSM_GROUP, SSM_STATE = 64, 16, 64
N_SEG = 8
SEG = T // N_SEG
GQ = 8
D_FF = 5504
FF_PIECE = 1376
ROPE_BASE = 10000.0
LR, B1, B2, ADAM_EPS, WD, STEP = 0.001, 0.9, 0.999, 1e-08, 0.01, 10
VMEM_CAP = 60 * 1024 * 1024
MESH = pl.DeviceIdType.MESH

WEIGHTS = ['meta_tokens', 'mix_norm', 'w_in', 'lam_re', 'lam_im', 'log_dt', 'b_re', 'b_im', 'c_re', 'c_im',
           'd_skip', 'w_glu', 'b_glu', 'q_a_norm', 'w_q_b', 'kv_a_norm', 'w_kv_b', 'out_norm_ssm',
           'out_norm_attn', 'w_out', 'ffn_norm', 'w_up', 'conv_w', 'conv_b', 'w_down', 'final_norm']
BIG = ['w_in', 'w_glu', 'w_q_b', 'w_kv_b', 'w_out', 'w_up', 'w_down']
SMALL = [n for n in WEIGHTS if n not in BIG and n not in ('meta_tokens', 'conv_w')]


def _nbytes(shape, dtype):
    return math.prod(shape) * jnp.dtype(dtype).itemsize


def _params(sem, need):
    return pltpu.CompilerParams(dimension_semantics=("arbitrary",) * sem,
                                vmem_limit_bytes=int(min(VMEM_CAP, max(need, 16 * 1024 * 1024))))


_DIMS = {'nn': (((1,), (0,)), ((), ())), 'nt': (((1,), (1,)), ((), ())), 'tn': (((0,), (0,)), ((), ()))}


def _mm(name, a, b, mode, tm, tn, tk, out_dtype, stack=None, res=None):
    sa, sb, so = stack in ('a_out', 'ab_red'), stack in ('b_out', 'ab_red'), stack in ('a_out', 'b_out')
    nj = a.shape[0] if sa else (b.shape[0] if sb else 1)
    a2, b2 = a.shape[-2:], b.shape[-2:]
    if mode == 'nn':
        (m, k), (k2, n) = a2, b2
    elif mode == 'nt':
        (m, k), (n, k2) = a2, b2
    else:
        (k, m), (k2, n) = a2, b2
    assert k == k2 and m % tm == 0 and n % tn == 0 and k % tk == 0, (name, a.shape, b.shape)
    n_jo, n_jr, n_k = (nj if so else 1), (nj if stack == 'ab_red' else 1), k // tk
    grid = (n_jo, m // tm, n // tn, n_jr, n_k)
    a_blk, a_idx = ((tk, tm), lambda i, j, kk: (kk, i)) if mode == 'tn' else ((tm, tk), lambda i, j, kk: (i, kk))
    b_blk, b_idx = ((tn, tk), lambda i, j, kk: (j, kk)) if mode == 'nt' else ((tk, tn), lambda i, j, kk: (kk, j))

    def spec(blk, idx, stacked, on_out):
        if not stacked:
            return pl.BlockSpec(blk, lambda jo, i, j, jr, kk: idx(i, j, kk))
        return pl.BlockSpec((None,) + blk, lambda jo, i, j, jr, kk: ((jo if on_out else jr),) + idx(i, j, kk))

    o_idx = lambda i, j, kk: (i, j)
    in_specs = [spec(a_blk, a_idx, sa, stack == 'a_out'), spec(b_blk, b_idx, sb, stack == 'b_out')]
    operands = [a, b]
    if res is not None:
        in_specs.append(spec((tm, tn), o_idx, False, False))
        operands.append(res)
    n_red = n_jr * n_k
    dims = _DIMS[mode]

    def body(*refs):
        a_ref, b_ref = refs[0], refs[1]
        res_ref = refs[2] if res is not None else None
        o_ref = refs[3] if res is not None else refs[2]
        part = lax.dot_general(a_ref[...].astype(BF16), b_ref[...].astype(BF16), dims,
                               preferred_element_type=F32)

        def finish(total):
            if res_ref is not None:
                total = total + res_ref[...]
            o_ref[...] = total.astype(o_ref.dtype)

        if n_red == 1:
            finish(part)
        else:
            acc_ref = refs[-1]
            step = pl.program_id(3) * n_k + pl.program_id(4)

            @pl.when(step == 0)
            def _():
                acc_ref[...] = part

            @pl.when(step > 0)
            def _():
                acc_ref[...] += part

            @pl.when(step == n_red - 1)
            def _():
                finish(acc_ref[...])

    out_shape = ((nj,) if so else ()) + (m, n)
    need = 2 * (_nbytes(a_blk, a.dtype) + _nbytes(b_blk, b.dtype) + _nbytes((tm, tn), out_dtype)
                + (_nbytes((tm, tn), F32) if res is not None else 0)) + 3 * _nbytes((tm, tn), F32) + (4 << 20)
    return pl.pallas_call(
        body, name=name, grid=grid, in_specs=in_specs, out_specs=spec((tm, tn), o_idx, so, True),
        out_shape=jax.ShapeDtypeStruct(out_shape, out_dtype),
        scratch_shapes=[pltpu.VMEM((tm, tn), F32)] if n_red > 1 else [],
        compiler_params=_params(5, need),
    )(*operands)


def _rowwise(name, fn, row_ins, const_ins, row_outs, acc_outs, tr=TR):
    n_ri, n_ci, n_ro, n_ao = len(row_ins), len(const_ins), len(row_outs), len(acc_outs)

    def body(*refs):
        ri, ci = refs[:n_ri], refs[n_ri:n_ri + n_ci]
        ro, ao = refs[n_ri + n_ci:n_ri + n_ci + n_ro], refs[n_ri + n_ci + n_ro:]
        i = pl.program_id(0)
        outs = fn(i * tr, *[r[...] for r in ri], *[r[...] for r in ci])
        for r, v in zip(ro, outs[:n_ro]):
            r[...] = v.astype(r.dtype)
        if n_ao:
            @pl.when(i == 0)
            def _():
                for r, v in zip(ao, outs[n_ro:]):
                    r[...] = v

            @pl.when(i > 0)
            def _():
                for r, v in zip(ao, outs[n_ro:]):
                    r[...] += v

    in_specs = [pl.BlockSpec((tr, w), functools.partial(lambda cb, i: (i, cb), cb)) for _, w, cb in row_ins]
    in_specs += [pl.BlockSpec(c.shape, functools.partial(lambda nd, i: (0,) * nd, c.ndim)) for c in const_ins]
    out_specs = [pl.BlockSpec((tr, w), lambda i: (i, 0)) for w, _ in row_outs]
    out_specs += [pl.BlockSpec((1, w), lambda i: (0, 0)) for w in acc_outs]
    out_shape = [jax.ShapeDtypeStruct((T, w), dt) for w, dt in row_outs]
    out_shape += [jax.ShapeDtypeStruct((1, w), F32) for w in acc_outs]
    need = 2 * (sum(_nbytes((tr, w), a.dtype) for a, w, _ in row_ins) + sum(_nbytes((tr, w), dt) for w, dt in row_outs))
    need = 3 * need + (8 << 20)
    return pl.pallas_call(
        body, name=name, grid=(T // tr,), in_specs=in_specs, out_specs=out_specs, out_shape=out_shape,
        compiler_params=_params(1, need),
    )(*[a for a, _, _ in row_ins], *const_ins)


def _rms_fwd_fn(row0, h, g):
    h = h.astype(F32)
    r = lax.rsqrt(jnp.mean(h * h, axis=-1, keepdims=True) + EPS)
    return (h * r) * g, r


def _rms_bwd(dxn, h, r, g):
    xhat = h * r
    dxg = dxn * g
    dh = r * (dxg - xhat * jnp.mean(dxg * xhat, axis=-1, keepdims=True))
    return dh, jnp.sum(dxn * xhat, axis=0, keepdims=True)


def _rms_bwd_fn(row0, dxn, h, r, g):
    return _rms_bwd(dxn.astype(F32), h.astype(F32), r, g)


def _rms_bwd_res_fn(row0, dxn, h, r, res, g):
    dh, dg = _rms_bwd(dxn.astype(F32), h.astype(F32), r, g)
    return dh + res, dg


def _gelu_fn(row0, y):
    return (jax.nn.gelu(y),)


def _glu_fwd_fn(row0, y, gl, b, gain):
    ya = jax.nn.gelu(y) * jax.nn.sigmoid(gl + b)
    return _rms_fwd_fn(row0, ya, gain)


def _glu_bwd_fn(row0, dyan, y, gl, ra, b, gain):
    g = jax.nn.gelu(y)
    s = jax.nn.sigmoid(gl + b)
    dya, dgain = _rms_bwd(dyan, g * s, ra, gain)
    dgl = dya * g * s * (1.0 - s)
    return dgl, dya * s, dgain, jnp.sum(dgl, axis=0, keepdims=True)


def _gelu_bwd_fn(row0, dg, y):
    return (jax.vjp(jax.nn.gelu, y)[1](dg)[0],)


def _loss_fn(row0, h, tgt, gain):
    r = lax.rsqrt(jnp.mean(h * h, axis=-1, keepdims=True) + EPS)
    xhat = h * r
    rows = row0 + lax.broadcasted_iota(jnp.int32, h.shape, 0)
    err = jnp.where(rows >= N_META, xhat * gain - tgt, 0.0)
    loss = jnp.full((1, 128), jnp.sum(err * err) * (0.5 / D), F32)
    dh, dgain = _rms_bwd(err * (1.0 / D), h, r, gain)
    return dh, dgain, loss


def _rope_tables():
    pos = jnp.arange(T, dtype=jnp.int32)
    inv_freq = 1.0 / (ROPE_BASE ** (jnp.arange(0, QK_ROPE, 2, dtype=F32) / QK_ROPE))
    ang = pos.astype(F32)[:, None] * inv_freq[None, :]
    cos, sin, z32, z64 = jnp.cos(ang), jnp.sin(ang), jnp.zeros((T, 32), F32), jnp.zeros((T, 64), F32)
    return (jnp.concatenate([cos, cos, z64], 1), jnp.concatenate([-sin, z32, z64], 1),
            jnp.concatenate([z32, sin, z64], 1))


def _rot(x, cc, s1, s2):
    return x * cc + pltpu.roll(x, 96, 1) * s1 + pltpu.roll(x, 32, 1) * s2


def _derot(d, cc, s1, s2):
    return d * cc + pltpu.roll(d * s1, 32, 1) + pltpu.roll(d * s2, 96, 1)


def _chunk_of(pos):
    return jnp.where(pos < N_META, 0, 1 + ((pos - N_META) >> 6))


def _key_len(t):
    last_chunk = 1 + ((t + 1) * TR - 1 - N_META) // CHUNK
    return min(T, pl.cdiv(N_META + last_chunk * CHUNK, 128) * 128)


def _attn_probs(t, q_ref, kv_ref, kpe_ref, tq, tk):
    kl = _key_len(t)
    q = q_ref[...]
    qn = q[:, :QK_NOPE].astype(BF16)
    qp = _rot(q[:, QK_NOPE:], tq[0][...], tq[1][...], tq[2][...]).astype(BF16)
    kn = kv_ref[:kl, :QK_NOPE].astype(BF16)
    v = kv_ref[:kl, QK_NOPE:].astype(BF16)
    kp = _rot(kpe_ref[:kl, :], tk[0][:kl, :], tk[1][:kl, :], tk[2][:kl, :]).astype(BF16)
    s = lax.dot_general(qn, kn, _DIMS['nt'], preferred_element_type=F32)
    s = s + lax.dot_general(qp, kp, _DIMS['nt'], preferred_element_type=F32)
    qc = _chunk_of(t * TR + lax.broadcasted_iota(jnp.int32, (TR, 1), 0))
    kc = _chunk_of(lax.broadcasted_iota(jnp.int32, (1, kl), 1))
    s = jnp.where(kc <= qc, s * SM_SCALE, jnp.finfo(F32).min)
    p = jnp.exp(s - jnp.max(s, axis=-1, keepdims=True))
    p = p / jnp.sum(p, axis=-1, keepdims=True)
    return qn, qp, kn, kp, v, p


def _per_q_tile(fn):
    for t in range(T // TR):
        pl.when(pl.program_id(1) == t)(functools.partial(fn, t))


def _attn_specs(z, tabs):
    q_spec = pl.BlockSpec((None, TR, HEAD_PAD), lambda h, i: (h, i, 0))
    kv_spec = pl.BlockSpec((None, T, HEAD_PAD), lambda h, i: (h, 0, 0))
    kpe_spec = pl.BlockSpec((T, 128), lambda h, i: (0, (D_IN - QK_ROPE) // 128))
    tq = [pl.BlockSpec((TR, 128), lambda h, i: (i, 0))] * 3
    tk = [pl.BlockSpec((T, 128), lambda h, i: (0, 0))] * 3
    return [q_spec, kv_spec, kpe_spec] + tq + tk


def _attn_fwd(q, kv, z, tabs):
    def body(q_ref, kv_ref, kpe_ref, c1, c2, c3, k1, k2, k3, o_ref):
        def tile(t):
            _, _, _, _, v, p = _attn_probs(t, q_ref, kv_ref, kpe_ref, (c1, c2, c3), (k1, k2, k3))
            o_ref[...] = jnp.dot(p.astype(BF16), v, preferred_element_type=F32)

        _per_q_tile(tile)

    return pl.pallas_call(
        body, name="attn_fwd", grid=(N_HEADS, T // TR), in_specs=_attn_specs(z, tabs),
        out_specs=pl.BlockSpec((TR, V_HEAD), lambda h, i: (i, h)),
        out_shape=jax.ShapeDtypeStruct((T, N_HEADS * V_HEAD), F32),
        compiler_params=_params(2, 48 << 20),
    )(q, kv, z, *tabs, *tabs)


def _attn_bwd(q, kv, z, tabs, dyb):
    def body(q_ref, kv_ref, kpe_ref, c1, c2, c3, k1, k2, k3, do_ref, dq_ref, dkv_ref, dkpe_ref):
        @pl.when(pl.program_id(1) == 0)
        def _():
            dkv_ref[...] = jnp.zeros(dkv_ref.shape, F32)
            dkpe_ref[...] = jnp.zeros(dkpe_ref.shape, F32)

        def tile(t):
            kl = _key_len(t)
            qn, qp, kn, kp, v, p = _attn_probs(t, q_ref, kv_ref, kpe_ref, (c1, c2, c3), (k1, k2, k3))
            do = do_ref[...].astype(BF16)
            dp = lax.dot_general(do, v, _DIMS['nt'], preferred_element_type=F32)
            ds = (p * (dp - jnp.sum(p * dp, axis=-1, keepdims=True)) * SM_SCALE).astype(BF16)
            pb = p.astype(BF16)
            dq_ref[:, :QK_NOPE] = jnp.dot(ds, kn, preferred_element_type=F32).astype(dq_ref.dtype)
            dqp = _derot(jnp.dot(ds, kp, preferred_element_type=F32), c1[...], c2[...], c3[...])
            dq_ref[:, QK_NOPE:] = dqp.astype(dq_ref.dtype)
            dkv_ref[:kl, :QK_NOPE] += lax.dot_general(ds, qn, _DIMS['tn'], preferred_element_type=F32)
            dkv_ref[:kl, QK_NOPE:] += lax.dot_general(pb, do, _DIMS['tn'], preferred_element_type=F32)
            dkpe_ref[:kl, :] += lax.dot_general(ds, qp, _DIMS['tn'], preferred_element_type=F32)

        _per_q_tile(tile)

    return pl.pallas_call(
        body, name="attn_bwd", grid=(N_HEADS, T // TR),
        in_specs=_attn_specs(z, tabs) + [pl.BlockSpec((TR, V_HEAD), lambda h, i: (i, h))],
        out_specs=[pl.BlockSpec((None, TR, HEAD_PAD), lambda h, i: (h, i, 0)),
                   pl.BlockSpec((None, T, HEAD_PAD), lambda h, i: (h, 0, 0)),
                   pl.BlockSpec((None, T, 128), lambda h, i: (h, 0, 0))],
        out_shape=[jax.ShapeDtypeStruct((N_HEADS, T, HEAD_PAD), BF16),
                   jax.ShapeDtypeStruct((N_HEADS, T, HEAD_PAD), F32),
                   jax.ShapeDtypeStruct((N_HEADS, T, 128), F32)],
        compiler_params=_params(2, 56 << 20),
    )(q, kv, z, *tabs, *tabs, dyb)


def _kpe_bwd(dkpe, tabs):
    def body(d_ref, c1, c2, c3, o_ref):
        d = d_ref[0]
        for h in range(1, N_HEADS):
            d = d + d_ref[h]
        o_ref[...] = _derot(d, c1[...], c2[...], c3[...]).astype(o_ref.dtype)

    tile = pl.BlockSpec((TR, 128), lambda i: (i, 0))
    return pl.pallas_call(
        body, name="kpe_bwd", grid=(T // TR,),
        in_specs=[pl.BlockSpec((N_HEADS, TR, 128), lambda i: (0, i, 0)), tile, tile, tile], out_specs=tile,
        out_shape=jax.ShapeDtypeStruct((T, 128), BF16), compiler_params=_params(1, 16 << 20),
    )(dkpe, *tabs)


def _s5_prep_math(lam_re, lam_im, log_dt, bt_re, bt_im):
    dt = jnp.exp(log_dt)
    mag = jnp.exp(lam_re * dt)
    a_re, a_im = mag * jnp.cos(lam_im * dt), mag * jnp.sin(lam_im * dt)
    den = lam_re * lam_re + lam_im * lam_im
    w_re = ((a_re - 1.0) * lam_re + a_im * lam_im) / den
    w_im = (a_im * lam_re - (a_re - 1.0) * lam_im) / den
    return a_re, a_im, w_re[None] * bt_re - w_im[None] * bt_im, w_re[None] * bt_im + w_im[None] * bt_re


def _eye_groups():
    return (lax.broadcasted_iota(jnp.int32, (N_GROUPS, N_GROUPS), 0)
            == lax.broadcasted_iota(jnp.int32, (N_GROUPS, N_GROUPS), 1)).astype(F32)


def _row_to_col(row):
    return jnp.sum(_eye_groups() * row, axis=1, keepdims=True)


def _s5_prep(lam_re, lam_im, log_dt, bt_re, bt_im):
    def body(lr, li, ld, br, bi, ar, ai, bbr, bbi):
        ar[...], ai[...], bbr[...], bbi[...] = _s5_prep_math(lr[...], li[...], _row_to_col(ld[...]), br[...], bi[...])

    gp, cgp = jax.ShapeDtypeStruct((N_GROUPS, SSM_STATE), F32), jax.ShapeDtypeStruct(bt_re.shape, F32)
    return pl.pallas_call(body, name="s5_prep", out_shape=[gp, gp, cgp, cgp])(lam_re, lam_im, log_dt, bt_re, bt_im)


def _s5_prep_bwd(lam_re, lam_im, log_dt, bt_re, bt_im, da_re, da_im, dbb_re, dbb_im):
    def body(lr, li, ld, br, bi, dar, dai, dbr, dbi, o1, o2, o3, o4, o5):
        _, vjp = jax.vjp(_s5_prep_math, lr[...], li[...], _row_to_col(ld[...]), br[...], bi[...])
        o1[...], o2[...], dld, o4[...], o5[...] = vjp((dar[...], dai[...], dbr[...], dbi[...]))
        o3[...] = jnp.sum(_eye_groups() * dld, axis=0, keepdims=True)

    ins = (lam_re, lam_im, log_dt, bt_re, bt_im)
    return pl.pallas_call(body, name="s5_prep_bwd", out_shape=[jax.ShapeDtypeStruct(a.shape, F32) for a in ins])(
        *ins, da_re, da_im, dbb_re, dbb_im)


def _cmul(ar, ai, br, bi):
    return ar * br - ai * bi, ar * bi + ai * br


def _seg_rows(i):
    return pl.ds(pl.multiple_of(i * N_SEG, N_SEG), N_SEG)


def _scan(xr, xi, ar, ai, reverse):
    zero = jnp.zeros(ar.shape, F32)

    def local(j, carry):
        rows = _seg_rows(SEG - 1 - j if reverse else j)
        nr, ni = _cmul(ar, ai, *carry)
        nr, ni = nr + xr[rows, :], ni + xi[rows, :]
        xr[rows, :], xi[rows, :] = nr, ni
        return nr, ni

    er, ei = lax.fori_loop(0, SEG, local, (zero, zero))
    pr, pi = ar, ai
    for _ in range(8):
        pr, pi = _cmul(pr, pi, pr, pi)
    pr, pi = _cmul(*_cmul(pr, pi, ar, ai), ar, ai)
    row = lax.broadcasted_iota(jnp.int32, ar.shape, 0)
    edge, shift = (N_SEG - 1, N_SEG - 1) if reverse else (0, 1)
    hr, hi = zero, zero
    for _ in range(N_SEG - 1):
        tr_, ti_ = _cmul(pr, pi, hr, hi)
        hr = jnp.where(row == edge, 0.0, pltpu.roll(tr_ + er, shift, 0))
        hi = jnp.where(row == edge, 0.0, pltpu.roll(ti_ + ei, shift, 0))

    def fix(j, carry):
        rows = _seg_rows(SEG - 1 - j if reverse else j)
        xr[rows, :] += carry[0]
        xi[rows, :] += carry[1]
        return _cmul(ar, ai, *carry)

    lax.fori_loop(0, SEG, fix, _cmul(ar, ai, hr, hi))
    return hr, hi


def _split(x):
    hi = x.astype(BF16)
    return hi, (x - hi.astype(F32)).astype(BF16)


def _dot3(xs, ys, mode='nn'):
    d = lambda p, q: lax.dot_general(p, q, _DIMS[mode], preferred_element_type=F32)
    return d(xs[0], ys[0]) + (d(xs[1], ys[0]) + d(xs[0], ys[1]))


def _s5_hidden(us, a_ref, bre_ref, bim_ref, hre, him):
    hre[...] = _dot3(us, _split(bre_ref[...]))
    him[...] = _dot3(us, _split(bim_ref[...]))
    return _scan(hre, him, a_ref[0], a_ref[1], reverse=False)


def _s5_specs():
    col = pl.BlockSpec((T, 128), lambda q: (0, q))
    a_spec = pl.BlockSpec((None, 2, N_SEG, GQ * SSM_STATE), lambda q: (q, 0, 0, 0))
    wide = pl.BlockSpec((None, 128, GQ * SSM_STATE), lambda q: (q, 0, 0))
    tall = pl.BlockSpec((None, GQ * SSM_STATE, 128), lambda q: (q, 0, 0))
    d_spec = pl.BlockSpec((1, 128), lambda q: (0, q))
    return col, a_spec, wide, tall, d_spec


def _s5_fwd(u, a8, bre, bim, cre, cim, dsk):
    def body(u_ref, a_ref, bre_ref, bim_ref, cre_ref, cim_ref, d_ref, y_ref, hre, him):
        u_ = u_ref[...]
        _s5_hidden(_split(u_), a_ref, bre_ref, bim_ref, hre, him)
        y_ref[...] = (_dot3(_split(hre[...]), _split(cre_ref[...]))
                      - _dot3(_split(him[...]), _split(cim_ref[...])) + d_ref[...] * u_)

    col, a_spec, wide, tall, d_spec = _s5_specs()
    return pl.pallas_call(
        body, name="s5_fwd", grid=(N_GROUPS // GQ,), in_specs=[col, a_spec, wide, wide, tall, tall, d_spec],
        out_specs=col, out_shape=jax.ShapeDtypeStruct((T, D_SSM), F32),
        scratch_shapes=[pltpu.VMEM((T, GQ * SSM_STATE), F32)] * 2,
        compiler_params=_params(1, 40 << 20),
    )(u, a8, bre, bim, cre, cim, dsk)


def _s5_bwd(u, dy, a8, bre, bim, bre_t, bim_t, cre_t, cim_t, dsk):
    def body(u_ref, dy_ref, a_ref, bre_ref, bim_ref, bret_ref, bimt_ref, cret_ref, cimt_ref, d_ref,
             du_ref, dbre_ref, dbim_ref, dcre_ref, dcim_ref, da_ref, dd_ref, hre, him, lre, lim):
        u_, dy_ = u_ref[...], dy_ref[...]
        us, dys = _split(u_), _split(dy_)
        ar, ai = a_ref[0], a_ref[1]
        h0r, h0i = _s5_hidden(us, a_ref, bre_ref, bim_ref, hre, him)
        lre[...] = _dot3(dys, _split(cret_ref[...]))
        lim[...] = -_dot3(dys, _split(cimt_ref[...]))
        _scan(lre, lim, ar, -ai, reverse=True)

        def acc_da(i, carry):
            lr, li = lre[_seg_rows(i), :], lim[_seg_rows(i), :]
            pr, pi = hre[_seg_rows(i - 1), :], him[_seg_rows(i - 1), :]
            return carry[0] + lr * pr + li * pi, carry[1] + li * pr - lr * pi

        lr, li = lre[_seg_rows(0), :], lim[_seg_rows(0), :]
        dar, dai = lax.fori_loop(1, SEG, acc_da, (lr * h0r + li * h0i, li * h0r - lr * h0i))
        da_ref[0:1, :] = jnp.sum(dar, axis=0, keepdims=True)
        da_ref[1:2, :] = jnp.sum(dai, axis=0, keepdims=True)
        lrs, lis = _split(lre[...]), _split(lim[...])
        du_ref[...] = (dy_ * d_ref[...] + _dot3(lrs, _split(bret_ref[...])) + _dot3(lis, _split(bimt_ref[...])))
        dbre_ref[...] = _dot3(us, lrs, 'tn')
        dbim_ref[...] = _dot3(us, lis, 'tn')
        dcre_ref[...] = _dot3(_split(hre[...]), dys, 'tn')
        dcim_ref[...] = -_dot3(_split(him[...]), dys, 'tn')
        dd_ref[...] = jnp.sum(dy_ * u_, axis=0, keepdims=True)

    col, a_spec, wide, tall, d_spec = _s5_specs()
    nq, ns = N_GROUPS // GQ, GQ * SSM_STATE
    return pl.pallas_call(
        body, name="s5_bwd", grid=(nq,),
        in_specs=[col, col, a_spec, wide, wide, tall, tall, wide, wide, d_spec],
        out_specs=[col, wide, wide, tall, tall, pl.BlockSpec((None, 2, ns), lambda q: (q, 0, 0)), d_spec],
        out_shape=[jax.ShapeDtypeStruct((T, D_SSM), F32),
                   jax.ShapeDtypeStruct((nq, 128, ns), F32), jax.ShapeDtypeStruct((nq, 128, ns), F32),
                   jax.ShapeDtypeStruct((nq, ns, 128), F32), jax.ShapeDtypeStruct((nq, ns, 128), F32),
                   jax.ShapeDtypeStruct((nq, 2, ns), F32), jax.ShapeDtypeStruct((1, D_SSM), F32)],
        scratch_shapes=[pltpu.VMEM((T, ns), F32)] * 4,
        compiler_params=_params(1, 56 << 20),
    )(u, dy, a8, bre, bim, bre_t, bim_t, cre_t, cim_t, dsk)


def _to_segments(a):
    return a.reshape(N_SEG, SEG, a.shape[-1]).transpose(1, 0, 2).reshape(T, a.shape[-1])


def _from_segments(a):
    return a.reshape(SEG, N_SEG, a.shape[-1]).transpose(1, 0, 2).reshape(T, a.shape[-1])


def _block_diag_b(bbt):
    nq = N_GROUPS // GQ
    x = bbt.transpose(1, 0, 2).reshape(nq, GQ, SSM_GROUP, 1, SSM_STATE)
    eye = jnp.eye(GQ, dtype=F32)[None, :, None, :, None]
    return (x * eye).reshape(nq, GQ * SSM_GROUP, GQ * SSM_STATE)


def _block_diag_c(c):
    nq = N_GROUPS // GQ
    x = c.reshape(nq, GQ, SSM_GROUP, SSM_STATE).transpose(0, 1, 3, 2)[:, :, :, None, :]
    eye = jnp.eye(GQ, dtype=F32)[None, :, None, :, None]
    return (x * eye).reshape(nq, GQ * SSM_STATE, GQ * SSM_GROUP)


def _ff_specs():
    up_spec = pl.BlockSpec((2, None, T, 128), lambda p, cb: (0, p, 0, cb))
    piece = pl.BlockSpec((None, T, 128), lambda p, cb: (p, 0, cb))
    cw_spec = pl.BlockSpec((None, 3, 128), lambda p, cb: (p, 0, cb))
    cb_spec = pl.BlockSpec((None, 1, 128), lambda p, cb: (p, 0, cb))
    return up_spec, piece, cw_spec, cb_spec


def _conv_gate(gate, w, cb, pad_ref):
    pad_ref[0:8, :] = jnp.zeros((8, 128), F32)
    pad_ref[8:8 + T, :] = gate
    g1, g2 = pad_ref[pl.ds(7, T), :], pad_ref[pl.ds(6, T), :]
    return w[0:1, :] * g2 + w[1:2, :] * g1 + w[2:3, :] * gate + cb, g1, g2


def _ff_act(up, cw4, cb4):
    def body(up_ref, w_ref, b_ref, o_ref, pad_ref):
        gc, _, _ = _conv_gate(up_ref[0], w_ref[...], b_ref[...], pad_ref)
        o_ref[...] = (jax.nn.silu(gc) * up_ref[1]).astype(o_ref.dtype)

    up_spec, piece, cw_spec, cb_spec = _ff_specs()
    n_cb = pl.cdiv(FF_PIECE, 128)
    return pl.pallas_call(
        body, name="ff_act", grid=(4, n_cb), in_specs=[up_spec, cw_spec, cb_spec], out_specs=piece,
        out_shape=jax.ShapeDtypeStruct((4, T, FF_PIECE), BF16),
        scratch_shapes=[pltpu.VMEM((T + 8, 128), F32)],
        compiler_params=_params(2, 24 << 20),
    )(up.reshape(2, 4, T, FF_PIECE), cw4, cb4)


def _ff_act_bwd(up, dact, cw4, cb4):
    def body(up_ref, da_ref, w_ref, b_ref, dup_ref, dw_ref, db_ref, pad_ref, pad2_ref):
        gate, val, w = up_ref[0], up_ref[1], w_ref[...]
        gc, g1, g2 = _conv_gate(gate, w, b_ref[...], pad_ref)
        sg = jax.nn.sigmoid(gc)
        da = da_ref[...]
        dup_ref[1] = (da * gc * sg).astype(dup_ref.dtype)
        dgc = da * val * sg * (1.0 + gc * (1.0 - sg))
        db_ref[...] = jnp.sum(dgc, axis=0, keepdims=True)
        dw_ref[0:1, :] = jnp.sum(dgc * g2, axis=0, keepdims=True)
        dw_ref[1:2, :] = jnp.sum(dgc * g1, axis=0, keepdims=True)
        dw_ref[2:3, :] = jnp.sum(dgc * gate, axis=0, keepdims=True)
        pad2_ref[0:T, :] = dgc
        pad2_ref[T:T + 8, :] = jnp.zeros((8, 128), F32)
        d1, d2 = pad2_ref[pl.ds(1, T), :], pad2_ref[pl.ds(2, T), :]
        dup_ref[0] = (w[2:3, :] * dgc + w[1:2, :] * d1 + w[0:1, :] * d2).astype(dup_ref.dtype)

    up_spec, piece, cw_spec, cb_spec = _ff_specs()
    n_cb = pl.cdiv(FF_PIECE, 128)
    dup, dw, db = pl.pallas_call(
        body, name="ff_act_bwd", grid=(4, n_cb), in_specs=[up_spec, piece, cw_spec, cb_spec],
        out_specs=[up_spec, cw_spec, cb_spec],
        out_shape=[jax.ShapeDtypeStruct((2, 4, T, FF_PIECE), BF16), jax.ShapeDtypeStruct((4, 3, FF_PIECE), F32),
                   jax.ShapeDtypeStruct((4, 1, FF_PIECE), F32)],
        scratch_shapes=[pltpu.VMEM((T + 8, 128), F32)] * 2,
        compiler_params=_params(2, 32 << 20),
    )(up.reshape(2, 4, T, FF_PIECE), dact, cw4, cb4)
    return dup.reshape(8, T, FF_PIECE), dw, db


def _local_step(x, tgt, meta, p, wf, late_weights, send_grads):
    sm = {}
    h0 = jnp.concatenate([meta, x], axis=0)
    tgt_pad = jnp.concatenate([jnp.zeros((N_META, D), F32), tgt], axis=0)
    tabs = _rope_tables()
    row = lambda a: a.reshape(1, -1)

    xn, r0 = _rowwise("rms_mix", _rms_fwd_fn, [(h0, D, 0)], [row(p['mix_norm'])], [(D, BF16), (1, F32)], [])
    z = _mm("in_proj", xn, wf['w_in'], 'nn', TR, 640, D, F32)
    lam_re, lam_im = p['lam_re'].reshape(N_GROUPS, SSM_STATE), p['lam_im'].reshape(N_GROUPS, SSM_STATE)
    log_dt = p['log_dt'].reshape(1, N_GROUPS)
    bt_re = p['b_re'].reshape(N_GROUPS, SSM_STATE, SSM_GROUP).transpose(2, 0, 1)
    bt_im = p['b_im'].reshape(N_GROUPS, SSM_STATE, SSM_GROUP).transpose(2, 0, 1)
    c_re = p['c_re'].reshape(N_GROUPS, SSM_GROUP, SSM_STATE)
    c_im = p['c_im'].reshape(N_GROUPS, SSM_GROUP, SSM_STATE)
    a_re, a_im, bbt_re, bbt_im = _s5_prep(lam_re, lam_im, log_dt, bt_re, bt_im)
    nq, ns = N_GROUPS // GQ, GQ * SSM_STATE
    a8 = jnp.broadcast_to(jnp.stack([a_re.reshape(nq, ns), a_im.reshape(nq, ns)], 1)[:, :, None, :],
                          (nq, 2, N_SEG, ns))
    bre, bim = _block_diag_b(bbt_re), _block_diag_b(bbt_im)
    cre, cim = _block_diag_c(c_re), _block_diag_c(c_im)
    dsk = row(p['d_skip'])
    u_seg = _to_segments(z[:, :D_SSM])
    y_ssm = _from_segments(_s5_fwd(u_seg, a8, bre, bim, cre, cim, dsk))
    g, = _rowwise("gelu", _gelu_fn, [(y_ssm, D_SSM, 0)], [], [(D_SSM, BF16)], [])
    gl = _mm("glu_proj", g, wf['w_glu'], 'nn', TR, D_SSM, D_SSM, F32)
    yan, ra = _rowwise("glu_norm", _glu_fwd_fn, [(y_ssm, D_SSM, 0), (gl, D_SSM, 0)],
                       [row(p['b_glu']), row(p['out_norm_ssm'])], [(D_SSM, BF16), (1, F32)], [])
    qn, rq = _rowwise("rms_q", _rms_fwd_fn, [(z, Q_LORA, D_SSM // Q_LORA)], [row(p['q_a_norm'])],
                      [(Q_LORA, BF16), (1, F32)], [])
    kvn, rkv = _rowwise("rms_kv", _rms_fwd_fn, [(z, KV_LORA, (D_SSM + Q_LORA) // KV_LORA)], [row(p['kv_a_norm'])],
                        [(KV_LORA, BF16), (1, F32)], [])
    q = _mm("q_proj", qn, wf['w_q_b'], 'nn', T, HEAD_PAD, Q_LORA, F32, stack='b_out')
    kv = _mm("kv_proj", kvn, wf['w_kv_b'], 'nn', T, HEAD_PAD, KV_LORA, F32, stack='b_out')
    yb = _attn_fwd(q, kv, z, tabs)
    ybn, rb = _rowwise("rms_attn", _rms_fwd_fn, [(yb, D_SSM, 0)], [row(p['out_norm_attn'])],
                       [(D_SSM, BF16), (1, F32)], [])
    y = jnp.concatenate([yan, ybn], axis=1)
    wf = dict(wf)
    wf['w_out'], wf['w_up'], wf['w_down'] = late_weights(y)
    h1 = _mm("out_proj", y, wf['w_out'], 'nn', TR, 1024, D, F32, res=h0)
    xn2, r1 = _rowwise("rms_ffn", _rms_fwd_fn, [(h1, D, 0)], [row(p['ffn_norm'])], [(D, BF16), (1, F32)], [])
    up = _mm("up_proj", xn2, wf['w_up'], 'nn', TR, FF_PIECE, D, F32, stack='b_out')
    cw4 = wf['conv_w'].reshape(3, 4, FF_PIECE).transpose(1, 0, 2)
    cb4 = p['conv_b'].reshape(4, 1, FF_PIECE)
    act = _ff_act(up, cw4, cb4)
    wd4 = wf['w_down'].reshape(4, FF_PIECE, D)
    h2 = _mm("down_proj", act, wd4, 'nn', TR, 1024, FF_PIECE, F32, stack='ab_red', res=h1)
    dh2, sm['final_norm'], loss = _rowwise("loss", _loss_fn, [(h2, D, 0), (tgt_pad, D, 0)], [row(p['final_norm'])],
                                           [(D, F32)], [D, 128])

    big = {}
    dact = _mm("down_bwd_x", dh2, wd4, 'nt', TR, FF_PIECE, D, F32, stack='b_out')
    g_down = _mm("down_bwd_w", act, dh2, 'tn', FF_PIECE, 1024, TR, BF16, stack='a_out').reshape(
        N_DEV, D_FF // N_DEV, D)
    dup, dcw4, sm['conv_b'] = _ff_act_bwd(up, dact, cw4, cb4)
    dxn2 = _mm("up_bwd_x", dup, wf['w_up'], 'nt', TR, 1024, FF_PIECE, F32, stack='ab_red')
    g_up = _mm("up_bwd_w", xn2, dup, 'tn', 1024, FF_PIECE, TR, BF16, stack='b_out')
    token = send_grads([g_down, g_up])
    dh1, sm['ffn_norm'] = _rowwise("rms_ffn_bwd", _rms_bwd_res_fn,
                                   [(dxn2, D, 0), (h1, D, 0), (r1, 1, 0), (dh2, D, 0)],
                                   [row(p['ffn_norm']) + token[0:1, 0:1]], [(D, F32)], [D])
    dy = _mm("out_bwd_x", dh1, wf['w_out'], 'nt', TR, 1024, D, F32)
    big['w_out'] = _mm("out_bwd_w", y, dh1, 'tn', 1024, 1024, TR, BF16).reshape(N_DEV, D // N_DEV, D)
    dgl, dgd, sm['out_norm_ssm'], sm['b_glu'] = _rowwise(
        "glu_bwd", _glu_bwd_fn, [(dy, D_SSM, 0), (y_ssm, D_SSM, 0), (gl, D_SSM, 0), (ra, 1, 0)],
        [row(p['b_glu']), row(p['out_norm_ssm'])], [(D_SSM, BF16), (D_SSM, F32)], [D_SSM, D_SSM])
    dg = _mm("glu_bwd_x", dgl, wf['w_glu'], 'nt', TR, D_SSM, D_SSM, F32, res=dgd)
    big['w_glu'] = _mm("glu_bwd_w", g, dgl, 'tn', D_SSM, D_SSM, TR, BF16).reshape(N_DEV, D_SSM // N_DEV, D_SSM)
    dy_ssm, = _rowwise("gelu_bwd", _gelu_bwd_fn, [(dg, D_SSM, 0), (y_ssm, D_SSM, 0)], [], [(D_SSM, F32)], [])
    tr3 = lambda a: a.transpose(0, 2, 1)
    du_seg, dbre, dbim, dcre, dcim, da, sm['d_skip'] = _s5_bwd(
        u_seg, _to_segments(dy_ssm), a8, bre, bim, tr3(bre), tr3(bim), tr3(cre), tr3(cim), dsk)
    du = _from_segments(du_seg).astype(BF16)
    diag_b = lambda d: jnp.einsum('qgcgp->qgcp', d.reshape(nq, GQ, SSM_GROUP, GQ, SSM_STATE)).transpose(
        2, 0, 1, 3).reshape(SSM_GROUP, N_GROUPS, SSM_STATE)
    diag_c = lambda d: jnp.einsum('qgpgc->qgcp', d.reshape(nq, GQ, SSM_STATE, GQ, SSM_GROUP)).reshape(
        N_GROUPS * SSM_GROUP, SSM_STATE)
    dlam_re, dlam_im, dlog_dt, dbt_re, dbt_im = _s5_prep_bwd(
        lam_re, lam_im, log_dt, bt_re, bt_im, da[:, 0, :].reshape(N_GROUPS, SSM_STATE),
        da[:, 1, :].reshape(N_GROUPS, SSM_STATE), diag_b(dbre), diag_b(dbim))
    sm['lam_re'], sm['lam_im'], sm['log_dt'] = dlam_re, dlam_im, dlog_dt
    cgp = lambda d: d.reshape(SSM_GROUP * N_GROUPS, SSM_STATE)
    sm['b_re'], sm['b_im'] = cgp(dbt_re), cgp(dbt_im)
    sm['c_re'], sm['c_im'] = diag_c(dcre), diag_c(dcim)
    dyb, sm['out_norm_attn'] = _rowwise("rms_attn_bwd", _rms_bwd_fn, [(dy, D_SSM, 1), (yb, D_SSM, 0), (rb, 1, 0)],
                                        [row(p['out_norm_attn'])], [(D_SSM, F32)], [D_SSM])
    dq, dkv, dkpe = _attn_bwd(q, kv, z, tabs, dyb)
    big['w_q_b'] = _mm("q_bwd_w", qn, dq, 'tn', Q_LORA, HEAD_PAD, T, BF16, stack='b_out')
    dqn = _mm("q_bwd_x", dq, wf['w_q_b'], 'nt', TR, Q_LORA, HEAD_PAD, F32, stack='ab_red')
    big['w_kv_b'] = _mm("kv_bwd_w", kvn, dkv, 'tn', KV_LORA, HEAD_PAD, T, BF16, stack='b_out')
    dkvn = _mm("kv_bwd_x", dkv, wf['w_kv_b'], 'nt', TR, KV_LORA, HEAD_PAD, F32, stack='ab_red')
    dq_a, sm['q_a_norm'] = _rowwise("rms_q_bwd", _rms_bwd_fn,
                                    [(dqn, Q_LORA, 0), (z, Q_LORA, D_SSM // Q_LORA), (rq, 1, 0)],
                                    [row(p['q_a_norm'])], [(Q_LORA, BF16)], [Q_LORA])
    dkv_a, sm['kv_a_norm'] = _rowwise("rms_kv_bwd", _rms_bwd_fn,
                                      [(dkvn, KV_LORA, 0), (z, KV_LORA, (D_SSM + Q_LORA) // KV_LORA), (rkv, 1, 0)],
                                      [row(p['kv_a_norm'])], [(KV_LORA, BF16)], [KV_LORA])
    dk_pe = _kpe_bwd(dkpe, tabs)
    dz = jnp.concatenate([du, dq_a, dkv_a, dk_pe], axis=1)
    dxn = _mm("in_bwd_x", dz, wf['w_in'], 'nt', TR, 1024, D_IN_PAD, F32)
    big['w_in'] = _mm("in_bwd_w", xn, dz, 'tn', 1024, 640, TR, BF16).reshape(N_DEV, D // N_DEV, D_IN_PAD)
    dh0, sm['mix_norm'] = _rowwise("rms_mix_bwd", _rms_bwd_res_fn,
                                   [(dxn, D, 0), (h0, D, 0), (r0, 1, 0), (dh1, D, 0)], [row(p['mix_norm'])],
                                   [(D, F32)], [D])
    return loss, dh0[N_META:], dh0[:N_META], dcw4, big, sm


def _place():
    x, y, c = lax.axis_index("x"), lax.axis_index("y"), lax.axis_index("c")
    return x, y, c, [(1 - x, y), (x, 1 - y), (1 - x, 1 - y)]


_HBM = pl.BlockSpec(memory_space=pltpu.HBM)


def _all_gather(name, arrs):
    n = len(arrs)

    def body(*refs):
        ins, outs = refs[:n], refs[n:2 * n]
        send, recv, loc = refs[2 * n:]
        x, y, c, chips = _place()
        me, sib = (x, y, c), (x, y, 1 - c)

        def cp(a, k, blk, to, src=None):
            dst = outs[a].at[4 * blk[0] + 2 * blk[1] + blk[2]]
            return pltpu.make_async_remote_copy(src_ref=dst if src is None else src, dst_ref=dst,
                                                send_sem=send.at[a, k], recv_sem=recv.at[a, k],
                                                device_id=to, device_id_type=MESH)

        mine = [pltpu.make_async_copy(ins[a], outs[a].at[4 * x + 2 * y + c], loc.at[a]) for a in range(n)]
        for m in mine:
            m.start()
        first = []
        for a in range(n):
            first.append(cp(a, 0, me, sib, src=ins[a]))
            first += [cp(a, 1 + j, me, (*ch, c), src=ins[a]) for j, ch in enumerate(chips)]
        for f in first:
            f.start()
        passed = []
        for a in range(n):
            for j, ch in enumerate(chips):
                cp(a, 1 + j, (*ch, c), me).wait_recv()
                passed.append(cp(a, 4 + j, (*ch, c), sib))
                passed[-1].start()
        for a in range(n):
            cp(a, 0, sib, me).wait_recv()
            for j, ch in enumerate(chips):
                cp(a, 4 + j, (*ch, 1 - c), me).wait_recv()
        for f in first + passed:
            f.wait_send()
        for m in mine:
            m.wait()

    return pl.pallas_call(
        body, name=name, in_specs=[_HBM] * n, out_specs=[_HBM] * n,
        out_shape=[jax.ShapeDtypeStruct((N_DEV,) + a.shape, a.dtype) for a in arrs],
        scratch_shapes=[pltpu.SemaphoreType.DMA((n, 7)), pltpu.SemaphoreType.DMA((n, 7)),
                        pltpu.SemaphoreType.DMA((n,))],
    )(*arrs)


def _exchange_cores(name, arrs):
    n = len(arrs)

    def body(*refs):
        ins, outs = refs[:n], refs[n:2 * n]
        send, recv = refs[2 * n:]
        x, y, c, _ = _place()
        remote = [pltpu.make_async_remote_copy(src_ref=ins[a].at[2 * k + 1 - c], dst_ref=outs[a].at[k],
                                               send_sem=send.at[a, k], recv_sem=recv.at[a, k],
                                               device_id=(x, y, 1 - c), device_id_type=MESH)
                  for a in range(n) for k in range(4)]
        for d in remote:
            d.start()
        for d in remote:
            d.wait()

    return pl.pallas_call(
        body, name=name, in_specs=[_HBM] * n, out_specs=[_HBM] * n,
        out_shape=[jax.ShapeDtypeStruct((4,) + a.shape[1:], a.dtype) for a in arrs],
        scratch_shapes=[pltpu.SemaphoreType.DMA((n, 4))] * 2,
    )(*arrs)


def _exchange_chips(name, arrs):
    n = len(arrs)

    def body(*refs):
        ins, outs = refs[:n], refs[n:2 * n]
        send, recv = refs[2 * n:]
        x, y, c, chips = _place()
        remote = [pltpu.make_async_remote_copy(src_ref=ins[a].at[2 * ch[0] + ch[1]], dst_ref=outs[a].at[j],
                                               send_sem=send.at[a, j], recv_sem=recv.at[a, j],
                                               device_id=(*ch, c), device_id_type=MESH)
                  for a in range(n) for j, ch in enumerate(chips)]
        for d in remote:
            d.start()
        for d in remote:
            d.wait()

    return pl.pallas_call(
        body, name=name, in_specs=[_HBM] * n, out_specs=[_HBM] * n,
        out_shape=[jax.ShapeDtypeStruct((3,) + a.shape[1:], a.dtype) for a in arrs],
        scratch_shapes=[pltpu.SemaphoreType.DMA((n, 3))] * 2,
    )(*arrs)


_SEM = pl.BlockSpec(memory_space=pltpu.SEMAPHORE)
_EFFECT = pltpu.SideEffectType.DATAFLOW_SIDE_EFFECTING


def _split_exchange(name, srcs, land_shapes, copies_of, n_cp):
    n = len(srcs)
    lands = [pltpu.with_memory_space_constraint(lax.empty(s, a.dtype), pltpu.HBM) for s, a in zip(land_shapes, srcs)]

    n_sem = n * n_cp

    def descriptors(src_refs, land_refs, send, recv, mirror):
        x, y, c, chips = _place()
        out = []
        for a in range(n):
            for k, (src, dst, to, back) in enumerate(copies_of(a, src_refs[a], land_refs[a], x, y, c, chips)):
                out.append(pltpu.make_async_remote_copy(src_ref=src, dst_ref=back if mirror else dst,
                                                        send_sem=send[a * n_cp + k], recv_sem=recv[a * n_cp + k],
                                                        device_id=to, device_id_type=MESH))
        return out

    def start_body(*refs):
        sems = refs[2 * n:2 * n + 2 * n_sem]
        for d in descriptors(refs[:n], refs[n:2 * n], sems[:n_sem], sems[n_sem:], False):
            d.start()
        refs[-1][...] = jnp.zeros((8, 128), F32)

    hbm_like = lambda arrs: [pltpu.HBM(a.shape, a.dtype) for a in arrs]
    res = pl.pallas_call(
        start_body, name=name + "_start", in_specs=[_HBM] * (2 * n),
        out_specs=[_SEM] * (2 * n_sem) + [_HBM] * (2 * n) + [pl.BlockSpec(memory_space=pltpu.VMEM)],
        out_shape=[pltpu.SemaphoreType.DMA(())] * (2 * n_sem)
        + hbm_like(srcs) + hbm_like(lands) + [jax.ShapeDtypeStruct((8, 128), F32)],
        input_output_aliases={i: 2 * n_sem + i for i in range(2 * n)},
        compiler_params=pltpu.CompilerParams(has_side_effects=_EFFECT),
    )(*[pltpu.with_memory_space_constraint(a, pltpu.HBM) for a in srcs], *lands)
    sems, thru, token = res[:2 * n_sem], res[2 * n_sem:2 * n_sem + 2 * n], res[-1]

    def wait(after):
        def wait_body(*refs):
            s = refs[2 * n:2 * n + 2 * n_sem]
            for d in descriptors(refs[:n], refs[n:2 * n], s[:n_sem], s[n_sem:], True):
                d.wait_send()
                d.wait_recv()

        out = pl.pallas_call(
            wait_body, name=name + "_wait",
            in_specs=[_HBM] * (2 * n) + [_SEM] * (2 * n_sem) + [pl.BlockSpec(memory_space=pl.ANY)],
            out_specs=[_HBM] * (2 * n), out_shape=hbm_like(srcs) + hbm_like(lands),
            input_output_aliases={i: i for i in range(2 * n)},
            compiler_params=pltpu.CompilerParams(has_side_effects=_EFFECT),
        )(*thru, *sems, after)
        return out[:n], out[n:]

    return token, wait


def _gather_copies(a, src, land, x, y, c, chips):
    peers = [(x, y, 1 - c)] + [(*ch, c) for ch in chips]
    me = 4 * x + 2 * y + c
    return [(src, land.at[me], to, land.at[4 * to[0] + 2 * to[1] + to[2]]) for to in peers]


def _chips_copies(a, src, land, x, y, c, chips):
    return [(src.at[2 * ch[0] + ch[1]], land.at[j], (*ch, c), land.at[j]) for j, ch in enumerate(chips)]


def _gather_forward(name, arrs):
    n = len(arrs)

    def body(*refs):
        ins, outs = refs[:n], refs[n:2 * n]
        send, recv = refs[2 * n:]
        x, y, c, chips = _place()
        sends, recvs = [], []
        for a in range(n):
            for j, ch in enumerate(chips):
                mine, theirs = 4 * ch[0] + 2 * ch[1] + c, 4 * ch[0] + 2 * ch[1] + 1 - c
                sends.append(pltpu.make_async_remote_copy(src_ref=ins[a].at[mine], dst_ref=outs[a].at[mine],
                                                          send_sem=send.at[a, j], recv_sem=recv.at[a, j],
                                                          device_id=(x, y, 1 - c), device_id_type=MESH))
                recvs.append(pltpu.make_async_remote_copy(src_ref=ins[a].at[theirs], dst_ref=outs[a].at[theirs],
                                                          send_sem=send.at[a, j], recv_sem=recv.at[a, j],
                                                          device_id=(x, y, 1 - c), device_id_type=MESH))
        for d in sends:
            d.start()
        for s, r in zip(sends, recvs):
            s.wait_send()
            r.wait_recv()

    return pl.pallas_call(
        body, name=name, in_specs=[_HBM] * n, out_specs=[_HBM] * n,
        out_shape=[jax.ShapeDtypeStruct(a.shape, a.dtype) for a in arrs],
        input_output_aliases={i: i for i in range(n)},
        scratch_shapes=[pltpu.SemaphoreType.DMA((n, 3))] * 2,
    )(*arrs)


def _blocks_of(r, c):
    if r * c * 4 <= (2 << 20):
        return r, c
    if r % 128 == 0:
        return 128, c
    return r, 256


def _pair_sum(name, own, got, core):
    _, r, c = got.shape
    rb, cb = _blocks_of(r, c)

    def body(s_ref, a_ref, b_ref, o_ref):
        o_ref[...] = (a_ref[...].astype(F32) + b_ref[...].astype(F32)).astype(o_ref.dtype)

    return pl.pallas_call(
        body, name=name, out_shape=jax.ShapeDtypeStruct((4, r, c), got.dtype),
        grid_spec=pltpu.PrefetchScalarGridSpec(
            num_scalar_prefetch=1, grid=(4, r // rb, c // cb),
            in_specs=[pl.BlockSpec((None, rb, cb), lambda k, i, j, s: (2 * k + s[0], i, j)),
                      pl.BlockSpec((None, rb, cb), lambda k, i, j, s: (k, i, j))],
            out_specs=pl.BlockSpec((None, rb, cb), lambda k, i, j, s: (k, i, j))),
        compiler_params=_params(3, 24 << 20),
    )(core, own, got)


def _adamw_math(w, g, m, v):
    m = B1 * m + (1.0 - B1) * g
    v = B2 * v + (1.0 - B2) * (g * g)
    m_hat = m / (1.0 - B1 ** STEP)
    v_hat = v / (1.0 - B2 ** STEP)
    return -LR * (m_hat / (jnp.sqrt(v_hat) + ADAM_EPS) + WD * w), m, v


def _adamw_big(name, own, got, chip, w, m, v):
    r, c = w.shape
    rb, cb = _blocks_of(r, c)

    def body(s_ref, o_ref, p_ref, w_ref, m_ref, v_ref, g_ref, d_ref, nm_ref, nv_ref):
        g = o_ref[...].astype(F32)
        for k in range(3):
            g = g + p_ref[k].astype(F32)
        g_ref[...] = g
        d_ref[...], nm_ref[...], nv_ref[...] = _adamw_math(w_ref[...], g, m_ref[...], v_ref[...])

    blk = pl.BlockSpec((rb, cb), lambda i, j, s: (i, j))
    return pl.pallas_call(
        body, name=name, out_shape=[jax.ShapeDtypeStruct((r, c), F32)] * 4,
        grid_spec=pltpu.PrefetchScalarGridSpec(
            num_scalar_prefetch=1, grid=(r // rb, c // cb),
            in_specs=[pl.BlockSpec((None, rb, cb), lambda i, j, s: (s[0], i, j)),
                      pl.BlockSpec((3, rb, cb), lambda i, j, s: (0, i, j)), blk, blk, blk],
            out_specs=[blk] * 4),
        compiler_params=_params(2, 40 << 20),
    )(chip, own, got, w, m, v)


def _adamw_multi(name, items, sums=(), row_blocks=1):
    n, ns = len(items), len(sums)

    def total(ref):
        g = ref[0]
        for d in range(1, ref.shape[0]):
            g = g + ref[d]
        return g

    def body(*refs):
        ins, outs = refs[:4 * n + ns], refs[4 * n + ns:]
        for t in range(n):
            p_ref, w_ref, m_ref, v_ref = ins[4 * t:4 * t + 4]
            g = total(p_ref)
            outs[4 * t][...] = g
            outs[4 * t + 1][...], outs[4 * t + 2][...], outs[4 * t + 3][...] = _adamw_math(
                w_ref[...], g, m_ref[...], v_ref[...])
        for t in range(ns):
            outs[4 * n + t][...] = total(ins[4 * n + t])

    def spec(shape, lead):
        blk = (shape[0] // row_blocks,) + tuple(shape[1:])
        nd = len(shape)
        if lead:
            return pl.BlockSpec((lead,) + blk, lambda i: (0, i) + (0,) * (nd - 1))
        return pl.BlockSpec(blk, lambda i: (i,) + (0,) * (nd - 1))

    operands, in_specs, out_specs, out_shape = [], [], [], []
    for parts, w, m, v in items:
        assert parts.shape[1:] == w.shape == m.shape == v.shape, (name, parts.shape, w.shape)
        operands += [parts, w, m, v]
        in_specs += [spec(w.shape, parts.shape[0])] + [spec(w.shape, 0)] * 3
        out_specs += [spec(w.shape, 0)] * 4
        out_shape += [jax.ShapeDtypeStruct(w.shape, F32)] * 4
    for parts in sums:
        operands.append(parts)
        in_specs.append(spec(parts.shape[1:], parts.shape[0]))
        out_specs.append(spec(parts.shape[1:], False))
        out_shape.append(jax.ShapeDtypeStruct(parts.shape[1:], F32))
    return pl.pallas_call(
        body, name=name, grid=(row_blocks,), in_specs=in_specs, out_specs=out_specs, out_shape=out_shape,
        compiler_params=_params(1, 56 << 20),
    )(*operands)


def kernel(x, meta_tokens, mix_norm, w_in, lam_re, lam_im, log_dt, b_re, b_im, c_re, c_im, d_skip, w_glu, b_glu, q_a_norm, w_q_b, kv_a_norm, w_kv_b, out_norm_ssm, out_norm_attn, w_out, ffn_norm, w_up, conv_w, conv_b, w_down, final_norm, loss_target, m_meta_tokens, m_mix_norm, m_w_in, m_lam_re, m_lam_im, m_log_dt, m_b_re, m_b_im, m_c_re, m_c_im, m_d_skip, m_w_glu, m_b_glu, m_q_a_norm, m_w_q_b, m_kv_a_norm, m_w_kv_b, m_out_norm_ssm, m_out_norm_attn, m_w_out, m_ffn_norm, m_w_up, m_conv_w, m_conv_b, m_w_down, m_final_norm, v_meta_tokens, v_mix_norm, v_w_in, v_lam_re, v_lam_im, v_log_dt, v_b_re, v_b_im, v_c_re, v_c_im, v_d_skip, v_w_glu, v_b_glu, v_q_a_norm, v_w_q_b, v_kv_a_norm, v_w_kv_b, v_out_norm_ssm, v_out_norm_attn, v_w_out, v_ffn_norm, v_w_up, v_conv_w, v_conv_b, v_w_down, v_final_norm):
    given = dict(locals())
    w = {n: given[n] for n in WEIGHTS}
    m = {n: given['m_' + n] for n in WEIGHTS}
    v = {n: given['v_' + n] for n in WEIGHTS}
    dev = 4 * lax.axis_index("x") + 2 * lax.axis_index("y") + lax.axis_index("c")

    shard = {
        'w_in': jnp.pad(w_in[0], ((0, 0), (0, D_IN_PAD - D_IN))),
        'w_glu': w_glu[0],
        'w_q_b': jnp.pad(w_q_b[0], ((0, 0), (0, HEAD_PAD - QK_NOPE - QK_ROPE))),
        'w_kv_b': w_kv_b[0],
        'w_out': w_out[0],
        'w_up': w_up[0],
        'w_down': w_down[0],
    }
    core = lax.axis_index("c").astype(jnp.int32).reshape(1)
    chip = (2 * lax.axis_index("x") + lax.axis_index("y")).astype(jnp.int32).reshape(1)
    early, late, ffn = ['w_in', 'w_glu', 'w_q_b', 'w_kv_b'], ['w_out', 'w_up', 'w_down'], ['w_down', 'w_up']
    shard = {n: a.astype(BF16) for n, a in shard.items()}

    gathered = _all_gather("gather_early", [shard[n] for n in early]
                           + [meta_tokens, jnp.pad(conv_w[0], ((0, 5), (0, 0)))])
    wf = dict(zip(early, gathered[:len(early)]))
    for n in ('w_in', 'w_glu'):
        wf[n] = wf[n].reshape(-1, wf[n].shape[-1])
    meta = gathered[-2].transpose(1, 0, 2).reshape(N_META, D)
    wf['conv_w'] = gathered[-1][:, :3].transpose(1, 0, 2).reshape(3, D_FF)
    gather_token, wait_late = _split_exchange("gather_late", [shard[n] for n in late],
                                              [(N_DEV,) + shard[n].shape for n in late], _gather_copies, 4)

    def late_weights(after):
        mine, landed = wait_late(after)
        full = _gather_forward("gather_late_forward",
                               [lax.dynamic_update_slice(a, s[None], (dev, 0, 0)) for s, a in zip(mine, landed)])
        return full[0].reshape(-1, D), full[1], full[2].reshape(-1, D)

    pending = []

    def send_grads(arrs):
        got = _exchange_cores("reduce_cores_ffn", arrs)
        sums = [_pair_sum("sum_cores_" + n, a, g, core) for n, a, g in zip(ffn, arrs, got)]
        token, wait = _split_exchange("reduce_chips_ffn", sums, [(3,) + s.shape[1:] for s in sums], _chips_copies, 3)
        pending.append(wait)
        return token

    small = {n: w[n] for n in SMALL}
    small['mix_norm'] = mix_norm + gather_token[0:1, 0:1]
    loss_part, grad_x, dmeta, dcw4, big, sm = _local_step(x[0], loss_target[0], meta, small, wf, late_weights,
                                                          send_grads)
    grads, deltas, new_m, new_v = {}, {}, {}, {}

    def keep(n, outs):
        grads[n], deltas[n], new_m[n], new_v[n] = (o.reshape(w[n].shape) for o in outs)

    rest = [n for n in BIG if n not in ffn]
    from_core = _exchange_cores("reduce_cores", [big[n] for n in rest])
    chip_sums = [_pair_sum("sum_cores_" + n, big[n], got, core) for n, got in zip(rest, from_core)]
    from_chips = _exchange_chips("reduce_chips", chip_sums)
    ffn_sums, ffn_from_chips = pending[0](from_chips[0])
    for n, own, got in zip(rest + ffn, chip_sums + list(ffn_sums), list(from_chips) + list(ffn_from_chips)):
        cols = w[n].shape[2]
        keep(n, _adamw_big("adamw_" + n, own[:, :, :cols], got[:, :, :cols], chip, w[n][0], m[n][0], v[n][0]))

    parts = dict(zip(SMALL + ['meta_tokens', 'conv_w', 'loss'],
                     _all_gather("gather_small", [sm[n] for n in SMALL] + [dmeta, dcw4, loss_part])))
    wide = [n for n in SMALL if n not in ('b_re', 'b_im')]
    outs = _adamw_multi("adamw_small",
                        [(parts[n],) + tuple(t[n].reshape(sm[n].shape) for t in (w, m, v)) for n in wide],
                        sums=[parts[n] for n in ('meta_tokens', 'conv_w', 'loss', 'b_re', 'b_im')])
    for i, n in enumerate(wide):
        keep(n, outs[4 * i:4 * i + 4])
    g_meta, g_cw4, loss, g_bre, g_bim = outs[-5:]
    gpc = lambda g: g.reshape(SSM_GROUP, N_GROUPS, SSM_STATE).transpose(1, 2, 0).reshape(-1, SSM_GROUP)[None]
    outs = _adamw_multi("adamw_b", [(gpc(g),) + tuple(t[n].reshape(-1, SSM_GROUP) for t in (w, m, v))
                                    for n, g in (('b_re', g_bre), ('b_im', g_bim))], row_blocks=8)
    keep('b_re', outs[:4])
    keep('b_im', outs[4:])

    g_meta = lax.dynamic_slice(g_meta, (0, dev * (D // N_DEV)), (N_META, D // N_DEV))
    g_conv = lax.dynamic_slice(g_cw4.transpose(1, 0, 2).reshape(3, D_FF), (0, dev * (D_FF // N_DEV)),
                               (3, D_FF // N_DEV))
    rows8 = lambda a: jnp.pad(a.reshape(3, D_FF // N_DEV), ((0, 5), (0, 0)))
    outs = _adamw_multi("adamw_cols", [(g_meta[None], meta_tokens, m['meta_tokens'], v['meta_tokens']),
                                       (rows8(g_conv)[None], rows8(conv_w), rows8(m['conv_w']), rows8(v['conv_w']))])
    keep('meta_tokens', outs[:4])
    keep('conv_w', [o[:3] for o in outs[4:]])

    return (loss[0, 0], grad_x[None], *[grads[n] for n in WEIGHTS], *[deltas[n] for n in WEIGHTS],
            *[new_m[n] for n in WEIGHTS], *[new_v[n] for n in WEIGHTS])
```

```python
import functools
import math

import jax
import jax.numpy as jnp
from jax import lax
from jax.experimental import pallas as pl
from jax.experimental.pallas import tpu as pltpu

F32, BF16 = jnp.float32, jnp.bfloat16

N_DEV = 8
N_META, SEQ, D = 16, 2048, 2048
T = N_META + SEQ
TR = 688
EPS = 1e-6
D_SSM, Q_LORA, KV_LORA, QK_ROPE = 1024, 512, 256, 64
D_IN = D_SSM + Q_LORA + KV_LORA + QK_ROPE
D_IN_PAD = 1920
N_HEADS, QK_NOPE, V_HEAD = 8, 128, 128
HEAD_PAD = 256
SM_SCALE = 1.0 / math.sqrt(QK_NOPE + QK_ROPE)
CHUNK = 64
N_GROUPS, SSM_GROUP, SSM_STATE = 64, 16, 64
N_SEG = 8
SEG = T // N_SEG
GQ = 8
D_FF = 5504
FF_PIECE = 1376
ROPE_BASE = 10000.0
LR, B1, B2, ADAM_EPS, WD, STEP = 0.001, 0.9, 0.999, 1e-08, 0.01, 10
VMEM_CAP = 60 * 1024 * 1024
MESH = pl.DeviceIdType.MESH

WEIGHTS = ['meta_tokens', 'mix_norm', 'w_in', 'lam_re', 'lam_im', 'log_dt', 'b_re', 'b_im', 'c_re', 'c_im',
           'd_skip', 'w_glu', 'b_glu', 'q_a_norm', 'w_q_b', 'kv_a_norm', 'w_kv_b', 'out_norm_ssm',
           'out_norm_attn', 'w_out', 'ffn_norm', 'w_up', 'conv_w', 'conv_b', 'w_down', 'final_norm']
BIG = ['w_in', 'w_glu', 'w_q_b', 'w_kv_b', 'w_out', 'w_up', 'w_down']
SMALL = [n for n in WEIGHTS if n not in BIG and n not in ('meta_tokens', 'conv_w')]


def _nbytes(shape, dtype):
    return math.prod(shape) * jnp.dtype(dtype).itemsize


def _params(sem, need):
    return pltpu.CompilerParams(dimension_semantics=("arbitrary",) * sem,
                                vmem_limit_bytes=int(min(VMEM_CAP, max(need, 16 * 1024 * 1024))))


_DIMS = {'nn': (((1,), (0,)), ((), ())), 'nt': (((1,), (1,)), ((), ())), 'tn': (((0,), (0,)), ((), ()))}


def _mm(name, a, b, mode, tm, tn, tk, out_dtype, stack=None, res=None):
    sa, sb, so = stack in ('a_out', 'ab_red'), stack in ('b_out', 'ab_red'), stack in ('a_out', 'b_out')
    nj = a.shape[0] if sa else (b.shape[0] if sb else 1)
    a2, b2 = a.shape[-2:], b.shape[-2:]
    if mode == 'nn':
        (m, k), (k2, n) = a2, b2
    elif mode == 'nt':
        (m, k), (n, k2) = a2, b2
    else:
        (k, m), (k2, n) = a2, b2
    assert k == k2 and m % tm == 0 and n % tn == 0 and k % tk == 0, (name, a.shape, b.shape)
    n_jo, n_jr, n_k = (nj if so else 1), (nj if stack == 'ab_red' else 1), k // tk
    grid = (n_jo, m // tm, n // tn, n_jr, n_k)
    a_blk, a_idx = ((tk, tm), lambda i, j, kk: (kk, i)) if mode == 'tn' else ((tm, tk), lambda i, j, kk: (i, kk))
    b_blk, b_idx = ((tn, tk), lambda i, j, kk: (j, kk)) if mode == 'nt' else ((tk, tn), lambda i, j, kk: (kk, j))

    def spec(blk, idx, stacked, on_out):
        if not stacked:
            return pl.BlockSpec(blk, lambda jo, i, j, jr, kk: idx(i, j, kk))
        return pl.BlockSpec((None,) + blk, lambda jo, i, j, jr, kk: ((jo if on_out else jr),) + idx(i, j, kk))

    o_idx = lambda i, j, kk: (i, j)
    in_specs = [spec(a_blk, a_idx, sa, stack == 'a_out'), spec(b_blk, b_idx, sb, stack == 'b_out')]
    operands = [a, b]
    if res is not None:
        in_specs.append(spec((tm, tn), o_idx, False, False))
        operands.append(res)
    n_red = n_jr * n_k
    dims = _DIMS[mode]

    def body(*refs):
        a_ref, b_ref = refs[0], refs[1]
        res_ref = refs[2] if res is not None else None
        o_ref = refs[3] if res is not None else refs[2]
        part = lax.dot_general(a_ref[...].astype(BF16), b_ref[...].astype(BF16), dims,
                               preferred_element_type=F32)

        def finish(total):
            if res_ref is not None:
                total = total + res_ref[...]
            o_ref[...] = total.astype(o_ref.dtype)

        if n_red == 1:
            finish(part)
        else:
            acc_ref = refs[-1]
            step = pl.program_id(3) * n_k + pl.program_id(4)

            @pl.when(step == 0)
            def _():
                acc_ref[...] = part

            @pl.when(step > 0)
            def _():
                acc_ref[...] += part

            @pl.when(step == n_red - 1)
            def _():
                finish(acc_ref[...])

    out_shape = ((nj,) if so else ()) + (m, n)
    need = 2 * (_nbytes(a_blk, a.dtype) + _nbytes(b_blk, b.dtype) + _nbytes((tm, tn), out_dtype)
                + (_nbytes((tm, tn), F32) if res is not None else 0)) + 3 * _nbytes((tm, tn), F32) + (4 << 20)
    return pl.pallas_call(
        body, name=name, grid=grid, in_specs=in_specs, out_specs=spec((tm, tn), o_idx, so, True),
        out_shape=jax.ShapeDtypeStruct(out_shape, out_dtype),
        scratch_shapes=[pltpu.VMEM((tm, tn), F32)] if n_red > 1 else [],
        compiler_params=_params(5, need),
    )(*operands)


def _rowwise(name, fn, row_ins, const_ins, row_outs, acc_outs, tr=TR):
    n_ri, n_ci, n_ro, n_ao = len(row_ins), len(const_ins), len(row_outs), len(acc_outs)

    def body(*refs):
        ri, ci = refs[:n_ri], refs[n_ri:n_ri + n_ci]
        ro, ao = refs[n_ri + n_ci:n_ri + n_ci + n_ro], refs[n_ri + n_ci + n_ro:]
        i = pl.program_id(0)
        outs = fn(i * tr, *[r[...] for r in ri], *[r[...] for r in ci])
        for r, v in zip(ro, outs[:n_ro]):
            r[...] = v.astype(r.dtype)
        if n_ao:
            @pl.when(i == 0)
            def _():
                for r, v in zip(ao, outs[n_ro:]):
                    r[...] = v

            @pl.when(i > 0)
            def _():
                for r, v in zip(ao, outs[n_ro:]):
                    r[...] += v

    in_specs = [pl.BlockSpec((tr, w), functools.partial(lambda cb, i: (i, cb), cb)) for _, w, cb in row_ins]
    in_specs += [pl.BlockSpec(c.shape, functools.partial(lambda nd, i: (0,) * nd, c.ndim)) for c in const_ins]
    out_specs = [pl.BlockSpec((tr, w), lambda i: (i, 0)) for w, _ in row_outs]
    out_specs += [pl.BlockSpec((1, w), lambda i: (0, 0)) for w in acc_outs]
    out_shape = [jax.ShapeDtypeStruct((T, w), dt) for w, dt in row_outs]
    out_shape += [jax.ShapeDtypeStruct((1, w), F32) for w in acc_outs]
    need = 2 * (sum(_nbytes((tr, w), a.dtype) for a, w, _ in row_ins) + sum(_nbytes((tr, w), dt) for w, dt in row_outs))
    need = 3 * need + (8 << 20)
    return pl.pallas_call(
        body, name=name, grid=(T // tr,), in_specs=in_specs, out_specs=out_specs, out_shape=out_shape,
        compiler_params=_params(1, need),
    )(*[a for a, _, _ in row_ins], *const_ins)


def _rms_fwd_fn(row0, h, g):
    h = h.astype(F32)
    r = lax.rsqrt(jnp.mean(h * h, axis=-1, keepdims=True) + EPS)
    return (h * r) * g, r


def _rms_bwd(dxn, h, r, g):
    xhat = h * r
    dxg = dxn * g
    dh = r * (dxg - xhat * jnp.mean(dxg * xhat, axis=-1, keepdims=True))
    return dh, jnp.sum(dxn * xhat, axis=0, keepdims=True)


def _rms_bwd_fn(row0, dxn, h, r, g):
    return _rms_bwd(dxn.astype(F32), h.astype(F32), r, g)


def _rms_bwd_res_fn(row0, dxn, h, r, res, g):
    dh, dg = _rms_bwd(dxn.astype(F32), h.astype(F32), r, g)
    return dh + res, dg


def _gelu_fn(row0, y):
    return (jax.nn.gelu(y),)


def _glu_fwd_fn(row0, y, gl, b, gain):
    ya = jax.nn.gelu(y) * jax.nn.sigmoid(gl + b)
    return _rms_fwd_fn(row0, ya, gain)


def _glu_bwd_fn(row0, dyan, y, gl, ra, b, gain):
    g = jax.nn.gelu(y)
    s = jax.nn.sigmoid(gl + b)
    dya, dgain = _rms_bwd(dyan, g * s, ra, gain)
    dgl = dya * g * s * (1.0 - s)
    return dgl, dya * s, dgain, jnp.sum(dgl, axis=0, keepdims=True)


def _gelu_bwd_fn(row0, dg, y):
    return (jax.vjp(jax.nn.gelu, y)[1](dg)[0],)


def _loss_fn(row0, h, tgt, gain):
    r = lax.rsqrt(jnp.mean(h * h, axis=-1, keepdims=True) + EPS)
    xhat = h * r
    rows = row0 + lax.broadcasted_iota(jnp.int32, h.shape, 0)
    err = jnp.where(rows >= N_META, xhat * gain - tgt, 0.0)
    loss = jnp.full((1, 128), jnp.sum(err * err) * (0.5 / D), F32)
    dh, dgain = _rms_bwd(err * (1.0 / D), h, r, gain)
    return dh, dgain, loss


def _rope_tables():
    pos = jnp.arange(T, dtype=jnp.int32)
    inv_freq = 1.0 / (ROPE_BASE ** (jnp.arange(0, QK_ROPE, 2, dtype=F32) / QK_ROPE))
    ang = pos.astype(F32)[:, None] * inv_freq[None, :]
    cos, sin, z32, z64 = jnp.cos(ang), jnp.sin(ang), jnp.zeros((T, 32), F32), jnp.zeros((T, 64), F32)
    return (jnp.concatenate([cos, cos, z64], 1), jnp.concatenate([-sin, z32, z64], 1),
            jnp.concatenate([z32, sin, z64], 1))


def _rot(x, cc, s1, s2):
    return x * cc + pltpu.roll(x, 96, 1) * s1 + pltpu.roll(x, 32, 1) * s2


def _derot(d, cc, s1, s2):
    return d * cc + pltpu.roll(d * s1, 32, 1) + pltpu.roll(d * s2, 96, 1)


def _chunk_of(pos):
    return jnp.where(pos < N_META, 0, 1 + ((pos - N_META) >> 6))


def _key_len(t):
    last_chunk = 1 + ((t + 1) * TR - 1 - N_META) // CHUNK
    return min(T, pl.cdiv(N_META + last_chunk * CHUNK, 128) * 128)


def _attn_probs(t, q_ref, kv_ref, kpe_ref, tq, tk):
    kl = _key_len(t)
    q = q_ref[...]
    qn = q[:, :QK_NOPE].astype(BF16)
    qp = _rot(q[:, QK_NOPE:], tq[0][...], tq[1][...], tq[2][...]).astype(BF16)
    kn = kv_ref[:kl, :QK_NOPE].astype(BF16)
    v = kv_ref[:kl, QK_NOPE:].astype(BF16)
    kp = _rot(kpe_ref[:kl, :], tk[0][:kl, :], tk[1][:kl, :], tk[2][:kl, :]).astype(BF16)
    s = lax.dot_general(qn, kn, _DIMS['nt'], preferred_element_type=F32)
    s = s + lax.dot_general(qp, kp, _DIMS['nt'], preferred_element_type=F32)
    qc = _chunk_of(t * TR + lax.broadcasted_iota(jnp.int32, (TR, 1), 0))
    kc = _chunk_of(lax.broadcasted_iota(jnp.int32, (1, kl), 1))
    s = jnp.where(kc <= qc, s * SM_SCALE, jnp.finfo(F32).min)
    p = jnp.exp(s - jnp.max(s, axis=-1, keepdims=True))
    p = p / jnp.sum(p, axis=-1, keepdims=True)
    return qn, qp, kn, kp, v, p


def _per_q_tile(fn):
    for t in range(T // TR):
        pl.when(pl.program_id(1) == t)(functools.partial(fn, t))


def _attn_specs(z, tabs):
    q_spec = pl.BlockSpec((None, TR, HEAD_PAD), lambda h, i: (h, i, 0))
    kv_spec = pl.BlockSpec((None, T, HEAD_PAD), lambda h, i: (h, 0, 0))
    kpe_spec = pl.BlockSpec((T, 128), lambda h, i: (0, (D_IN - QK_ROPE) // 128))
    tq = [pl.BlockSpec((TR, 128), lambda h, i: (i, 0))] * 3
    tk = [pl.BlockSpec((T, 128), lambda h, i: (0, 0))] * 3
    return [q_spec, kv_spec, kpe_spec] + tq + tk


def _attn_fwd(q, kv, z, tabs):
    def body(q_ref, kv_ref, kpe_ref, c1, c2, c3, k1, k2, k3, o_ref):
        def tile(t):
            _, _, _, _, v, p = _attn_probs(t, q_ref, kv_ref, kpe_ref, (c1, c2, c3), (k1, k2, k3))
            o_ref[...] = jnp.dot(p.astype(BF16), v, preferred_element_type=F32)

        _per_q_tile(tile)

    return pl.pallas_call(
        body, name="attn_fwd", grid=(N_HEADS, T // TR), in_specs=_attn_specs(z, tabs),
        out_specs=pl.BlockSpec((TR, V_HEAD), lambda h, i: (i, h)),
        out_shape=jax.ShapeDtypeStruct((T, N_HEADS * V_HEAD), F32),
        compiler_params=_params(2, 48 << 20),
    )(q, kv, z, *tabs, *tabs)


def _attn_bwd(q, kv, z, tabs, dyb):
    def body(q_ref, kv_ref, kpe_ref, c1, c2, c3, k1, k2, k3, do_ref, dq_ref, dkv_ref, dkpe_ref):
        @pl.when(pl.program_id(1) == 0)
        def _():
            dkv_ref[...] = jnp.zeros(dkv_ref.shape, F32)
            dkpe_ref[...] = jnp.zeros(dkpe_ref.shape, F32)

        def tile(t):
            kl = _key_len(t)
            qn, qp, kn, kp, v, p = _attn_probs(t, q_ref, kv_ref, kpe_ref, (c1, c2, c3), (k1, k2, k3))
            do = do_ref[...].astype(BF16)
            dp = lax.dot_general(do, v, _DIMS['nt'], preferred_element_type=F32)
            ds = (p * (dp - jnp.sum(p * dp, axis=-1, keepdims=True)) * SM_SCALE).astype(BF16)
            pb = p.astype(BF16)
            dq_ref[:, :QK_NOPE] = jnp.dot(ds, kn, preferred_element_type=F32).astype(dq_ref.dtype)
            dqp = _derot(jnp.dot(ds, kp, preferred_element_type=F32), c1[...], c2[...], c3[...])
            dq_ref[:, QK_NOPE:] = dqp.astype(dq_ref.dtype)
            dkv_ref[:kl, :QK_NOPE] += lax.dot_general(ds, qn, _DIMS['tn'], preferred_element_type=F32)
            dkv_ref[:kl, QK_NOPE:] += lax.dot_general(pb, do, _DIMS['tn'], preferred_element_type=F32)
            dkpe_ref[:kl, :] += lax.dot_general(ds, qp, _DIMS['tn'], preferred_element_type=F32)

        _per_q_tile(tile)

    return pl.pallas_call(
        body, name="attn_bwd", grid=(N_HEADS, T // TR),
        in_specs=_attn_specs(z, tabs) + [pl.BlockSpec((TR, V_HEAD), lambda h, i: (i, h))],
        out_specs=[pl.BlockSpec((None, TR, HEAD_PAD), lambda h, i: (h, i, 0)),
                   pl.BlockSpec((None, T, HEAD_PAD), lambda h, i: (h, 0, 0)),
                   pl.BlockSpec((None, T, 128), lambda h, i: (h, 0, 0))],
        out_shape=[jax.ShapeDtypeStruct((N_HEADS, T, HEAD_PAD), BF16),
                   jax.ShapeDtypeStruct((N_HEADS, T, HEAD_PAD), F32),
                   jax.ShapeDtypeStruct((N_HEADS, T, 128), F32)],
        compiler_params=_params(2, 56 << 20),
    )(q, kv, z, *tabs, *tabs, dyb)


def _kpe_bwd(dkpe, tabs):
    def body(d_ref, c1, c2, c3, o_ref):
        d = d_ref[0]
        for h in range(1, N_HEADS):
            d = d + d_ref[h]
        o_ref[...] = _derot(d, c1[...], c2[...], c3[...]).astype(o_ref.dtype)

    tile = pl.BlockSpec((TR, 128), lambda i: (i, 0))
    return pl.pallas_call(
        body, name="kpe_bwd", grid=(T // TR,),
        in_specs=[pl.BlockSpec((N_HEADS, TR, 128), lambda i: (0, i, 0)), tile, tile, tile], out_specs=tile,
        out_shape=jax.ShapeDtypeStruct((T, 128), BF16), compiler_params=_params(1, 16 << 20),
    )(dkpe, *tabs)


def _s5_prep_math(lam_re, lam_im, log_dt, bt_re, bt_im):
    dt = jnp.exp(log_dt)
    mag = jnp.exp(lam_re * dt)
    a_re, a_im = mag * jnp.cos(lam_im * dt), mag * jnp.sin(lam_im * dt)
    den = lam_re * lam_re + lam_im * lam_im
    w_re = ((a_re - 1.0) * lam_re + a_im * lam_im) / den
    w_im = (a_im * lam_re - (a_re - 1.0) * lam_im) / den
    return a_re, a_im, w_re[None] * bt_re - w_im[None] * bt_im, w_re[None] * bt_im + w_im[None] * bt_re


def _eye_groups():
    return (lax.broadcasted_iota(jnp.int32, (N_GROUPS, N_GROUPS), 0)
            == lax.broadcasted_iota(jnp.int32, (N_GROUPS, N_GROUPS), 1)).astype(F32)


def _row_to_col(row):
    return jnp.sum(_eye_groups() * row, axis=1, keepdims=True)


def _s5_prep(lam_re, lam_im, log_dt, bt_re, bt_im):
    def body(lr, li, ld, br, bi, ar, ai, bbr, bbi):
        ar[...], ai[...], bbr[...], bbi[...] = _s5_prep_math(lr[...], li[...], _row_to_col(ld[...]), br[...], bi[...])

    gp, cgp = jax.ShapeDtypeStruct((N_GROUPS, SSM_STATE), F32), jax.ShapeDtypeStruct(bt_re.shape, F32)
    return pl.pallas_call(body, name="s5_prep", out_shape=[gp, gp, cgp, cgp])(lam_re, lam_im, log_dt, bt_re, bt_im)


def _s5_prep_bwd(lam_re, lam_im, log_dt, bt_re, bt_im, da_re, da_im, dbb_re, dbb_im):
    def body(lr, li, ld, br, bi, dar, dai, dbr, dbi, o1, o2, o3, o4, o5):
        _, vjp = jax.vjp(_s5_prep_math, lr[...], li[...], _row_to_col(ld[...]), br[...], bi[...])
        o1[...], o2[...], dld, o4[...], o5[...] = vjp((dar[...], dai[...], dbr[...], dbi[...]))
        o3[...] = jnp.sum(_eye_groups() * dld, axis=0, keepdims=True)

    ins = (lam_re, lam_im, log_dt, bt_re, bt_im)
    return pl.pallas_call(body, name="s5_prep_bwd", out_shape=[jax.ShapeDtypeStruct(a.shape, F32) for a in ins])(
        *ins, da_re, da_im, dbb_re, dbb_im)


def _cmul(ar, ai, br, bi):
    return ar * br - ai * bi, ar * bi + ai * br


def _seg_rows(i):
    return pl.ds(pl.multiple_of(i * N_SEG, N_SEG), N_SEG)


def _scan(xr, xi, ar, ai, reverse):
    zero = jnp.zeros(ar.shape, F32)

    def local(j, carry):
        rows = _seg_rows(SEG - 1 - j if reverse else j)
        nr, ni = _cmul(ar, ai, *carry)
        nr, ni = nr + xr[rows, :], ni + xi[rows, :]
        xr[rows, :], xi[rows, :] = nr, ni
        return nr, ni

    er, ei = lax.fori_loop(0, SEG, local, (zero, zero))
    pr, pi = ar, ai
    for _ in range(8):
        pr, pi = _cmul(pr, pi, pr, pi)
    pr, pi = _cmul(*_cmul(pr, pi, ar, ai), ar, ai)
    row = lax.broadcasted_iota(jnp.int32, ar.shape, 0)
    edge, shift = (N_SEG - 1, N_SEG - 1) if reverse else (0, 1)
    hr, hi = zero, zero
    for _ in range(N_SEG - 1):
        tr_, ti_ = _cmul(pr, pi, hr, hi)
        hr = jnp.where(row == edge, 0.0, pltpu.roll(tr_ + er, shift, 0))
        hi = jnp.where(row == edge, 0.0, pltpu.roll(ti_ + ei, shift, 0))

    def fix(j, carry):
        rows = _seg_rows(SEG - 1 - j if reverse else j)
        xr[rows, :] += carry[0]
        xi[rows, :] += carry[1]
        return _cmul(ar, ai, *carry)

    lax.fori_loop(0, SEG, fix, _cmul(ar, ai, hr, hi))
    return hr, hi


def _split(x):
    hi = x.astype(BF16)
    return hi, (x - hi.astype(F32)).astype(BF16)


def _dot3(xs, ys, mode='nn'):
    d = lambda p, q: lax.dot_general(p, q, _DIMS[mode], preferred_element_type=F32)
    return d(xs[0], ys[0]) + (d(xs[1], ys[0]) + d(xs[0], ys[1]))


def _s5_hidden(us, a_ref, bre_ref, bim_ref, hre, him):
    hre[...] = _dot3(us, _split(bre_ref[...]))
    him[...] = _dot3(us, _split(bim_ref[...]))
    return _scan(hre, him, a_ref[0], a_ref[1], reverse=False)


def _s5_specs():
    col = pl.BlockSpec((T, 128), lambda q: (0, q))
    a_spec = pl.BlockSpec((None, 2, N_SEG, GQ * SSM_STATE), lambda q: (q, 0, 0, 0))
    wide = pl.BlockSpec((None, 128, GQ * SSM_STATE), lambda q: (q, 0, 0))
    tall = pl.BlockSpec((None, GQ * SSM_STATE, 128), lambda q: (q, 0, 0))
    d_spec = pl.BlockSpec((1, 128), lambda q: (0, q))
    return col, a_spec, wide, tall, d_spec


def _s5_fwd(u, a8, bre, bim, cre, cim, dsk):
    def body(u_ref, a_ref, bre_ref, bim_ref, cre_ref, cim_ref, d_ref, y_ref, hre, him):
        u_ = u_ref[...]
        _s5_hidden(_split(u_), a_ref, bre_ref, bim_ref, hre, him)
        y_ref[...] = (_dot3(_split(hre[...]), _split(cre_ref[...]))
                      - _dot3(_split(him[...]), _split(cim_ref[...])) + d_ref[...] * u_)

    col, a_spec, wide, tall, d_spec = _s5_specs()
    return pl.pallas_call(
        body, name="s5_fwd", grid=(N_GROUPS // GQ,), in_specs=[col, a_spec, wide, wide, tall, tall, d_spec],
        out_specs=col, out_shape=jax.ShapeDtypeStruct((T, D_SSM), F32),
        scratch_shapes=[pltpu.VMEM((T, GQ * SSM_STATE), F32)] * 2,
        compiler_params=_params(1, 40 << 20),
    )(u, a8, bre, bim, cre, cim, dsk)


def _s5_bwd(u, dy, a8, bre, bim, bre_t, bim_t, cre_t, cim_t, dsk):
    def body(u_ref, dy_ref, a_ref, bre_ref, bim_ref, bret_ref, bimt_ref, cret_ref, cimt_ref, d_ref,
             du_ref, dbre_ref, dbim_ref, dcre_ref, dcim_ref, da_ref, dd_ref, hre, him, lre, lim):
        u_, dy_ = u_ref[...], dy_ref[...]
        us, dys = _split(u_), _split(dy_)
        ar, ai = a_ref[0], a_ref[1]
        h0r, h0i = _s5_hidden(us, a_ref, bre_ref, bim_ref, hre, him)
        lre[...] = _dot3(dys, _split(cret_ref[...]))
        lim[...] = -_dot3(dys, _split(cimt_ref[...]))
        _scan(lre, lim, ar, -ai, reverse=True)

        def acc_da(i, carry):
            lr, li = lre[_seg_rows(i), :], lim[_seg_rows(i), :]
            pr, pi = hre[_seg_rows(i - 1), :], him[_seg_rows(i - 1), :]
            return carry[0] + lr * pr + li * pi, carry[1] + li * pr - lr * pi

        lr, li = lre[_seg_rows(0), :], lim[_seg_rows(0), :]
        dar, dai = lax.fori_loop(1, SEG, acc_da, (lr * h0r + li * h0i, li * h0r - lr * h0i))
        da_ref[0:1, :] = jnp.sum(dar, axis=0, keepdims=True)
        da_ref[1:2, :] = jnp.sum(dai, axis=0, keepdims=True)
        lrs, lis = _split(lre[...]), _split(lim[...])
        du_ref[...] = (dy_ * d_ref[...] + _dot3(lrs, _split(bret_ref[...])) + _dot3(lis, _split(bimt_ref[...])))
        dbre_ref[...] = _dot3(us, lrs, 'tn')
        dbim_ref[...] = _dot3(us, lis, 'tn')
        dcre_ref[...] = _dot3(_split(hre[...]), dys, 'tn')
        dcim_ref[...] = -_dot3(_split(him[...]), dys, 'tn')
        dd_ref[...] = jnp.sum(dy_ * u_, axis=0, keepdims=True)

    col, a_spec, wide, tall, d_spec = _s5_specs()
    nq, ns = N_GROUPS // GQ, GQ * SSM_STATE
    return pl.pallas_call(
        body, name="s5_bwd", grid=(nq,),
        in_specs=[col, col, a_spec, wide, wide, tall, tall, wide, wide, d_spec],
        out_specs=[col, wide, wide, tall, tall, pl.BlockSpec((None, 2, ns), lambda q: (q, 0, 0)), d_spec],
        out_shape=[jax.ShapeDtypeStruct((T, D_SSM), F32),
                   jax.ShapeDtypeStruct((nq, 128, ns), F32), jax.ShapeDtypeStruct((nq, 128, ns), F32),
                   jax.ShapeDtypeStruct((nq, ns, 128), F32), jax.ShapeDtypeStruct((nq, ns, 128), F32),
                   jax.ShapeDtypeStruct((nq, 2, ns), F32), jax.ShapeDtypeStruct((1, D_SSM), F32)],
        scratch_shapes=[pltpu.VMEM((T, ns), F32)] * 4,
        compiler_params=_params(1, 56 << 20),
    )(u, dy, a8, bre, bim, bre_t, bim_t, cre_t, cim_t, dsk)


def _to_segments(a):
    return a.reshape(N_SEG, SEG, a.shape[-1]).transpose(1, 0, 2).reshape(T, a.shape[-1])


def _from_segments(a):
    return a.reshape(SEG, N_SEG, a.shape[-1]).transpose(1, 0, 2).reshape(T, a.shape[-1])


def _block_diag_b(bbt):
    nq = N_GROUPS // GQ
    x = bbt.transpose(1, 0, 2).reshape(nq, GQ, SSM_GROUP, 1, SSM_STATE)
    eye = jnp.eye(GQ, dtype=F32)[None, :, None, :, None]
    return (x * eye).reshape(nq, GQ * SSM_GROUP, GQ * SSM_STATE)


def _block_diag_c(c):
    nq = N_GROUPS // GQ
    x = c.reshape(nq, GQ, SSM_GROUP, SSM_STATE).transpose(0, 1, 3, 2)[:, :, :, None, :]
    eye = jnp.eye(GQ, dtype=F32)[None, :, None, :, None]
    return (x * eye).reshape(nq, GQ * SSM_STATE, GQ * SSM_GROUP)


def _ff_specs():
    up_spec = pl.BlockSpec((2, None, T, 128), lambda p, cb: (0, p, 0, cb))
    piece = pl.BlockSpec((None, T, 128), lambda p, cb: (p, 0, cb))
    cw_spec = pl.BlockSpec((None, 3, 128), lambda p, cb: (p, 0, cb))
    cb_spec = pl.BlockSpec((None, 1, 128), lambda p, cb: (p, 0, cb))
    return up_spec, piece, cw_spec, cb_spec


def _conv_gate(gate, w, cb, pad_ref):
    pad_ref[0:8, :] = jnp.zeros((8, 128), F32)
    pad_ref[8:8 + T, :] = gate
    g1, g2 = pad_ref[pl.ds(7, T), :], pad_ref[pl.ds(6, T), :]
    return w[0:1, :] * g2 + w[1:2, :] * g1 + w[2:3, :] * gate + cb, g1, g2


def _ff_act(up, cw4, cb4):
    def body(up_ref, w_ref, b_ref, o_ref, pad_ref):
        gc, _, _ = _conv_gate(up_ref[0], w_ref[...], b_ref[...], pad_ref)
        o_ref[...] = (jax.nn.silu(gc) * up_ref[1]).astype(o_ref.dtype)

    up_spec, piece, cw_spec, cb_spec = _ff_specs()
    n_cb = pl.cdiv(FF_PIECE, 128)
    return pl.pallas_call(
        body, name="ff_act", grid=(4, n_cb), in_specs=[up_spec, cw_spec, cb_spec], out_specs=piece,
        out_shape=jax.ShapeDtypeStruct((4, T, FF_PIECE), BF16),
        scratch_shapes=[pltpu.VMEM((T + 8, 128), F32)],
        compiler_params=_params(2, 24 << 20),
    )(up.reshape(2, 4, T, FF_PIECE), cw4, cb4)


def _ff_act_bwd(up, dact, cw4, cb4):
    def body(up_ref, da_ref, w_ref, b_ref, dup_ref, dw_ref, db_ref, pad_ref, pad2_ref):
        gate, val, w = up_ref[0], up_ref[1], w_ref[...]
        gc, g1, g2 = _conv_gate(gate, w, b_ref[...], pad_ref)
        sg = jax.nn.sigmoid(gc)
        da = da_ref[...]
        dup_ref[1] = (da * gc * sg).astype(dup_ref.dtype)
        dgc = da * val * sg * (1.0 + gc * (1.0 - sg))
        db_ref[...] = jnp.sum(dgc, axis=0, keepdims=True)
        dw_ref[0:1, :] = jnp.sum(dgc * g2, axis=0, keepdims=True)
        dw_ref[1:2, :] = jnp.sum(dgc * g1, axis=0, keepdims=True)
        dw_ref[2:3, :] = jnp.sum(dgc * gate, axis=0, keepdims=True)
        pad2_ref[0:T, :] = dgc
        pad2_ref[T:T + 8, :] = jnp.zeros((8, 128), F32)
        d1, d2 = pad2_ref[pl.ds(1, T), :], pad2_ref[pl.ds(2, T), :]
        dup_ref[0] = (w[2:3, :] * dgc + w[1:2, :] * d1 + w[0:1, :] * d2).astype(dup_ref.dtype)

    up_spec, piece, cw_spec, cb_spec = _ff_specs()
    n_cb = pl.cdiv(FF_PIECE, 128)
    dup, dw, db = pl.pallas_call(
        body, name="ff_act_bwd", grid=(4, n_cb), in_specs=[up_spec, piece, cw_spec, cb_spec],
        out_specs=[up_spec, cw_spec, cb_spec],
        out_shape=[jax.ShapeDtypeStruct((2, 4, T, FF_PIECE), BF16), jax.ShapeDtypeStruct((4, 3, FF_PIECE), F32),
                   jax.ShapeDtypeStruct((4, 1, FF_PIECE), F32)],
        scratch_shapes=[pltpu.VMEM((T + 8, 128), F32)] * 2,
        compiler_params=_params(2, 32 << 20),
    )(up.reshape(2, 4, T, FF_PIECE), dact, cw4, cb4)
    return dup.reshape(8, T, FF_PIECE), dw, db


def _local_step(x, tgt, meta, p, wf, late_weights, send_grads):
    sm = {}
    h0 = jnp.concatenate([meta, x], axis=0)
    tgt_pad = jnp.concatenate([jnp.zeros((N_META, D), F32), tgt], axis=0)
    tabs = _rope_tables()
    row = lambda a: a.reshape(1, -1)

    xn, r0 = _rowwise("rms_mix", _rms_fwd_fn, [(h0, D, 0)], [row(p['mix_norm'])], [(D, BF16), (1, F32)], [])
    z = _mm("in_proj", xn, wf['w_in'], 'nn', TR, 640, D, F32)
    lam_re, lam_im = p['lam_re'].reshape(N_GROUPS, SSM_STATE), p['lam_im'].reshape(N_GROUPS, SSM_STATE)
    log_dt = p['log_dt'].reshape(1, N_GROUPS)
    bt_re = p['b_re'].reshape(N_GROUPS, SSM_STATE, SSM_GROUP).transpose(2, 0, 1)
    bt_im = p['b_im'].reshape(N_GROUPS, SSM_STATE, SSM_GROUP).transpose(2, 0, 1)
    c_re = p['c_re'].reshape(N_GROUPS, SSM_GROUP, SSM_STATE)
    c_im = p['c_im'].reshape(N_GROUPS, SSM_GROUP, SSM_STATE)
    a_re, a_im, bbt_re, bbt_im = _s5_prep(lam_re, lam_im, log_dt, bt_re, bt_im)
    nq, ns = N_GROUPS // GQ, GQ * SSM_STATE
    a8 = jnp.broadcast_to(jnp.stack([a_re.reshape(nq, ns), a_im.reshape(nq, ns)], 1)[:, :, None, :],
                          (nq, 2, N_SEG, ns))
    bre, bim = _block_diag_b(bbt_re), _block_diag_b(bbt_im)
    cre, cim = _block_diag_c(c_re), _block_diag_c(c_im)
    dsk = row(p['d_skip'])
    u_seg = _to_segments(z[:, :D_SSM])
    y_ssm = _from_segments(_s5_fwd(u_seg, a8, bre, bim, cre, cim, dsk))
    g, = _rowwise("gelu", _gelu_fn, [(y_ssm, D_SSM, 0)], [], [(D_SSM, BF16)], [])
    gl = _mm("glu_proj", g, wf['w_glu'], 'nn', TR, D_SSM, D_SSM, F32)
    yan, ra = _rowwise("glu_norm", _glu_fwd_fn, [(y_ssm, D_SSM, 0), (gl, D_SSM, 0)],
                       [row(p['b_glu']), row(p['out_norm_ssm'])], [(D_SSM, BF16), (1, F32)], [])
    qn, rq = _rowwise("rms_q", _rms_fwd_fn, [(z, Q_LORA, D_SSM // Q_LORA)], [row(p['q_a_norm'])],
                      [(Q_LORA, BF16), (1, F32)], [])
    kvn, rkv = _rowwise("rms_kv", _rms_fwd_fn, [(z, KV_LORA, (D_SSM + Q_LORA) // KV_LORA)], [row(p['kv_a_norm'])],
                        [(KV_LORA, BF16), (1, F32)], [])
    q = _mm("q_proj", qn, wf['w_q_b'], 'nn', T, HEAD_PAD, Q_LORA, F32, stack='b_out')
    kv = _mm("kv_proj", kvn, wf['w_kv_b'], 'nn', T, HEAD_PAD, KV_LORA, F32, stack='b_out')
    yb = _attn_fwd(q, kv, z, tabs)
    ybn, rb = _rowwise("rms_attn", _rms_fwd_fn, [(yb, D_SSM, 0)], [row(p['out_norm_attn'])],
                       [(D_SSM, BF16), (1, F32)], [])
    y = jnp.concatenate([yan, ybn], axis=1)
    wf = dict(wf)
    wf['w_out'], wf['w_up'], wf['w_down'] = late_weights(y)
    h1 = _mm("out_proj", y, wf['w_out'], 'nn', TR, 1024, D, F32, res=h0)
    xn2, r1 = _rowwise("rms_ffn", _rms_fwd_fn, [(h1, D, 0)], [row(p['ffn_norm'])], [(D, BF16), (1, F32)], [])
    up = _mm("up_proj", xn2, wf['w_up'], 'nn', TR, FF_PIECE, D, F32, stack='b_out')
    cw4 = wf['conv_w'].reshape(3, 4, FF_PIECE).transpose(1, 0, 2)
    cb4 = p['conv_b'].reshape(4, 1, FF_PIECE)
    act = _ff_act(up, cw4, cb4)
    wd4 = wf['w_down'].reshape(4, FF_PIECE, D)
    h2 = _mm("down_proj", act, wd4, 'nn', TR, 1024, FF_PIECE, F32, stack='ab_red', res=h1)
    dh2, sm['final_norm'], loss = _rowwise("loss", _loss_fn, [(h2, D, 0), (tgt_pad, D, 0)], [row(p['final_norm'])],
                                           [(D, F32)], [D, 128])

    big = {}
    dact = _mm("down_bwd_x", dh2, wd4, 'nt', TR, FF_PIECE, D, F32, stack='b_out')
    g_down = _mm("down_bwd_w", act, dh2, 'tn', FF_PIECE, 1024, TR, BF16, stack='a_out').reshape(
        N_DEV, D_FF // N_DEV, D)
    dup, dcw4, sm['conv_b'] = _ff_act_bwd(up, dact, cw4, cb4)
    dxn2 = _mm("up_bwd_x", dup, wf['w_up'], 'nt', TR, 1024, FF_PIECE, F32, stack='ab_red')
    g_up = _mm("up_bwd_w", xn2, dup, 'tn', 1024, FF_PIECE, TR, BF16, stack='b_out')
    token = send_grads([g_down, g_up])
    dh1, sm['ffn_norm'] = _rowwise("rms_ffn_bwd", _rms_bwd_res_fn,
                                   [(dxn2, D, 0), (h1, D, 0), (r1, 1, 0), (dh2, D, 0)],
                                   [row(p['ffn_norm']) + token[0:1, 0:1]], [(D, F32)], [D])
    dy = _mm("out_bwd_x", dh1, wf['w_out'], 'nt', TR, 1024, D, F32)
    big['w_out'] = _mm("out_bwd_w", y, dh1, 'tn', 1024, 1024, TR, BF16).reshape(N_DEV, D // N_DEV, D)
    dgl, dgd, sm['out_norm_ssm'], sm['b_glu'] = _rowwise(
        "glu_bwd", _glu_bwd_fn, [(dy, D_SSM, 0), (y_ssm, D_SSM, 0), (gl, D_SSM, 0), (ra, 1, 0)],
        [row(p['b_glu']), row(p['out_norm_ssm'])], [(D_SSM, BF16), (D_SSM, F32)], [D_SSM, D_SSM])
    dg = _mm("glu_bwd_x", dgl, wf['w_glu'], 'nt', TR, D_SSM, D_SSM, F32, res=dgd)
    big['w_glu'] = _mm("glu_bwd_w", g, dgl, 'tn', D_SSM, D_SSM, TR, BF16).reshape(N_DEV, D_SSM // N_DEV, D_SSM)
    dy_ssm, = _rowwise("gelu_bwd", _gelu_bwd_fn, [(dg, D_SSM, 0), (y_ssm, D_SSM, 0)], [], [(D_SSM, F32)], [])
    tr3 = lambda a: a.transpose(0, 2, 1)
    du_seg, dbre, dbim, dcre, dcim, da, sm['d_skip'] = _s5_bwd(
        u_seg, _to_segments(dy_ssm), a8, bre, bim, tr3(bre), tr3(bim), tr3(cre), tr3(cim), dsk)
    du = _from_segments(du_seg).astype(BF16)
    diag_b = lambda d: jnp.einsum('qgcgp->qgcp', d.reshape(nq, GQ, SSM_GROUP, GQ, SSM_STATE)).transpose(
        2, 0, 1, 3).reshape(SSM_GROUP, N_GROUPS, SSM_STATE)
    diag_c = lambda d: jnp.einsum('qgpgc->qgcp', d.reshape(nq, GQ, SSM_STATE, GQ, SSM_GROUP)).reshape(
        N_GROUPS * SSM_GROUP, SSM_STATE)
    dlam_re, dlam_im, dlog_dt, dbt_re, dbt_im = _s5_prep_bwd(
        lam_re, lam_im, log_dt, bt_re, bt_im, da[:, 0, :].reshape(N_GROUPS, SSM_STATE),
        da[:, 1, :].reshape(N_GROUPS, SSM_STATE), diag_b(dbre), diag_b(dbim))
    sm['lam_re'], sm['lam_im'], sm['log_dt'] = dlam_re, dlam_im, dlog_dt
    cgp = lambda d: d.reshape(SSM_GROUP * N_GROUPS, SSM_STATE)
    sm['b_re'], sm['b_im'] = cgp(dbt_re), cgp(dbt_im)
    sm['c_re'], sm['c_im'] = diag_c(dcre), diag_c(dcim)
    dyb, sm['out_norm_attn'] = _rowwise("rms_attn_bwd", _rms_bwd_fn, [(dy, D_SSM, 1), (yb, D_SSM, 0), (rb, 1, 0)],
                                        [row(p['out_norm_attn'])], [(D_SSM, F32)], [D_SSM])
    dq, dkv, dkpe = _attn_bwd(q, kv, z, tabs, dyb)
    big['w_q_b'] = _mm("q_bwd_w", qn, dq, 'tn', Q_LORA, HEAD_PAD, T, BF16, stack='b_out')
    dqn = _mm("q_bwd_x", dq, wf['w_q_b'], 'nt', TR, Q_LORA, HEAD_PAD, F32, stack='ab_red')
    big['w_kv_b'] = _mm("kv_bwd_w", kvn, dkv, 'tn', KV_LORA, HEAD_PAD, T, BF16, stack='b_out')
    dkvn = _mm("kv_bwd_x", dkv, wf['w_kv_b'], 'nt', TR, KV_LORA, HEAD_PAD, F32, stack='ab_red')
    dq_a, sm['q_a_norm'] = _rowwise("rms_q_bwd", _rms_bwd_fn,
                                    [(dqn, Q_LORA, 0), (z, Q_LORA, D_SSM // Q_LORA), (rq, 1, 0)],
                                    [row(p['q_a_norm'])], [(Q_LORA, BF16)], [Q_LORA])
    dkv_a, sm['kv_a_norm'] = _rowwise("rms_kv_bwd", _rms_bwd_fn,
                                      [(dkvn, KV_LORA, 0), (z, KV_LORA, (D_SSM + Q_LORA) // KV_LORA), (rkv, 1, 0)],
                                      [row(p['kv_a_norm'])], [(KV_LORA, BF16)], [KV_LORA])
    dk_pe = _kpe_bwd(dkpe, tabs)
    dz = jnp.concatenate([du, dq_a, dkv_a, dk_pe], axis=1)
    dxn = _mm("in_bwd_x", dz, wf['w_in'], 'nt', TR, 1024, D_IN_PAD, F32)
    big['w_in'] = _mm("in_bwd_w", xn, dz, 'tn', 1024, 640, TR, BF16).reshape(N_DEV, D // N_DEV, D_IN_PAD)
    dh0, sm['mix_norm'] = _rowwise("rms_mix_bwd", _rms_bwd_res_fn,
                                   [(dxn, D, 0), (h0, D, 0), (r0, 1, 0), (dh1, D, 0)], [row(p['mix_norm'])],
                                   [(D, F32)], [D])
    return loss, dh0[N_META:], dh0[:N_META], dcw4, big, sm


def _place():
    x, y, c = lax.axis_index("x"), lax.axis_index("y"), lax.axis_index("c")
    return x, y, c, [(1 - x, y), (x, 1 - y), (1 - x, 1 - y)]


_HBM = pl.BlockSpec(memory_space=pltpu.HBM)


def _all_gather(name, arrs, chip_blocks=()):
    n, nc = len(arrs), len(chip_blocks)

    def body(*refs):
        ins, cins, outs, couts = refs[:n], refs[n:n + nc], refs[n + nc:2 * n + nc], refs[2 * n + nc:2 * (n + nc)]
        send, recv, loc, csend, crecv = refs[2 * (n + nc):]
        x, y, c, chips = _place()
        me, sib = (x, y, c), (x, y, 1 - c)

        def cp(a, k, blk, to, src=None):
            dst = outs[a].at[4 * blk[0] + 2 * blk[1] + blk[2]]
            return pltpu.make_async_remote_copy(src_ref=dst if src is None else src, dst_ref=dst,
                                                send_sem=send.at[a, k], recv_sem=recv.at[a, k],
                                                device_id=to, device_id_type=MESH)

        between_chips = [pltpu.make_async_remote_copy(src_ref=cins[a].at[2 * ch[0] + ch[1]], dst_ref=couts[a].at[j],
                                                      send_sem=csend.at[a, j], recv_sem=crecv.at[a, j],
                                                      device_id=(*ch, c), device_id_type=MESH)
                         for a in range(nc) for j, ch in enumerate(chips)]
        mine = [pltpu.make_async_copy(ins[a], outs[a].at[4 * x + 2 * y + c], loc.at[a]) for a in range(n)]
        for m in mine + between_chips:
            m.start()
        first = []
        for a in range(n):
            first.append(cp(a, 0, me, sib, src=ins[a]))
            first += [cp(a, 1 + j, me, (*ch, c), src=ins[a]) for j, ch in enumerate(chips)]
        for f in first:
            f.start()
        passed = []
        for a in range(n):
            for j, ch in enumerate(chips):
                cp(a, 1 + j, (*ch, c), me).wait_recv()
                passed.append(cp(a, 4 + j, (*ch, c), sib))
                passed[-1].start()
        for a in range(n):
            cp(a, 0, sib, me).wait_recv()
            for j, ch in enumerate(chips):
                cp(a, 4 + j, (*ch, 1 - c), me).wait_recv()
        for f in first + passed:
            f.wait_send()
        for m in mine + between_chips:
            m.wait()

    out = pl.pallas_call(
        body, name=name, in_specs=[_HBM] * (n + nc), out_specs=[_HBM] * (n + nc),
        out_shape=[jax.ShapeDtypeStruct((N_DEV,) + a.shape, a.dtype) for a in arrs]
        + [jax.ShapeDtypeStruct((3,) + a.shape[1:], a.dtype) for a in chip_blocks],
        scratch_shapes=[pltpu.SemaphoreType.DMA((n, 7)), pltpu.SemaphoreType.DMA((n, 7)),
                        pltpu.SemaphoreType.DMA((n,)), pltpu.SemaphoreType.DMA((max(nc, 1), 3)),
                        pltpu.SemaphoreType.DMA((max(nc, 1), 3))],
    )(*arrs, *chip_blocks)
    return (out[:n], out[n:]) if nc else out


def _exchange_cores(name, arrs):
    n = len(arrs)

    def body(*refs):
        ins, outs = refs[:n], refs[n:2 * n]
        send, recv = refs[2 * n:]
        x, y, c, _ = _place()
        remote = [pltpu.make_async_remote_copy(src_ref=ins[a].at[2 * k + 1 - c], dst_ref=outs[a].at[k],
                                               send_sem=send.at[a, k], recv_sem=recv.at[a, k],
                                               device_id=(x, y, 1 - c), device_id_type=MESH)
                  for a in range(n) for k in range(4)]
        for d in remote:
            d.start()
        for d in remote:
            d.wait()

    return pl.pallas_call(
        body, name=name, in_specs=[_HBM] * n, out_specs=[_HBM] * n,
        out_shape=[jax.ShapeDtypeStruct((4,) + a.shape[1:], a.dtype) for a in arrs],
        scratch_shapes=[pltpu.SemaphoreType.DMA((n, 4))] * 2,
    )(*arrs)


_SEM = pl.BlockSpec(memory_space=pltpu.SEMAPHORE)
_EFFECT = pltpu.SideEffectType.DATAFLOW_SIDE_EFFECTING


def _split_exchange(name, srcs, land_shapes, copies_of, n_cp, after):
    n = len(srcs)
    lands = [pltpu.with_memory_space_constraint(lax.empty(s, a.dtype), pltpu.HBM) for s, a in zip(land_shapes, srcs)]

    n_sem = n * n_cp

    def descriptors(src_refs, land_refs, send, recv, mirror):
        x, y, c, chips = _place()
        out = []
        for a in range(n):
            for k, (src, dst, to, back) in enumerate(copies_of(a, src_refs[a], land_refs[a], x, y, c, chips)):
                out.append(pltpu.make_async_remote_copy(src_ref=src, dst_ref=back if mirror else dst,
                                                        send_sem=send[a * n_cp + k], recv_sem=recv[a * n_cp + k],
                                                        device_id=to, device_id_type=MESH))
        return out

    def start_body(*refs):
        sems = refs[2 * n + 1:2 * n + 1 + 2 * n_sem]
        for d in descriptors(refs[:n], refs[n:2 * n], sems[:n_sem], sems[n_sem:], False):
            d.start()
        refs[-1][...] = jnp.zeros((8, 128), F32)

    hbm_like = lambda arrs: [pltpu.HBM(a.shape, a.dtype) for a in arrs]
    res = pl.pallas_call(
        start_body, name=name + "_start", in_specs=[_HBM] * (2 * n) + [pl.BlockSpec(memory_space=pl.ANY)],
        out_specs=[_SEM] * (2 * n_sem) + [_HBM] * (2 * n) + [pl.BlockSpec(memory_space=pltpu.VMEM)],
        out_shape=[pltpu.SemaphoreType.DMA(())] * (2 * n_sem)
        + hbm_like(srcs) + hbm_like(lands) + [jax.ShapeDtypeStruct((8, 128), F32)],
        input_output_aliases={i: 2 * n_sem + i for i in range(2 * n)},
        compiler_params=pltpu.CompilerParams(has_side_effects=_EFFECT),
    )(*[pltpu.with_memory_space_constraint(a, pltpu.HBM) for a in srcs], *lands, after)
    sems, thru, token = res[:2 * n_sem], res[2 * n_sem:2 * n_sem + 2 * n], res[-1]

    def wait(after):
        def wait_body(*refs):
            s = refs[2 * n:2 * n + 2 * n_sem]
            for d in descriptors(refs[:n], refs[n:2 * n], s[:n_sem], s[n_sem:], True):
                d.wait_send()
                d.wait_recv()

        out = pl.pallas_call(
            wait_body, name=name + "_wait",
            in_specs=[_HBM] * (2 * n) + [_SEM] * (2 * n_sem) + [pl.BlockSpec(memory_space=pl.ANY)],
            out_specs=[_HBM] * (2 * n), out_shape=hbm_like(srcs) + hbm_like(lands),
            input_output_aliases={i: i for i in range(2 * n)},
            compiler_params=pltpu.CompilerParams(has_side_effects=_EFFECT),
        )(*thru, *sems, after)
        return out[:n], out[n:]

    return token, wait


def _gather_copies(a, src, land, x, y, c, chips):
    peers = [(x, y, 1 - c)] + [(*ch, c) for ch in chips]
    me = 4 * x + 2 * y + c
    return [(src, land.at[me], to, land.at[4 * to[0] + 2 * to[1] + to[2]]) for to in peers]


def _chips_copies(a, src, land, x, y, c, chips):
    return [(src.at[2 * ch[0] + ch[1]], land.at[j], (*ch, c), land.at[j]) for j, ch in enumerate(chips)]


def _gather_forward(name, arrs):
    n = len(arrs)

    def body(*refs):
        ins, outs = refs[:n], refs[n:2 * n]
        send, recv = refs[2 * n:]
        x, y, c, chips = _place()
        sends, recvs = [], []
        for a in range(n):
            for j, ch in enumerate(chips):
                mine, theirs = 4 * ch[0] + 2 * ch[1] + c, 4 * ch[0] + 2 * ch[1] + 1 - c
                sends.append(pltpu.make_async_remote_copy(src_ref=ins[a].at[mine], dst_ref=outs[a].at[mine],
                                                          send_sem=send.at[a, j], recv_sem=recv.at[a, j],
                                                          device_id=(x, y, 1 - c), device_id_type=MESH))
                recvs.append(pltpu.make_async_remote_copy(src_ref=ins[a].at[theirs], dst_ref=outs[a].at[theirs],
                                                          send_sem=send.at[a, j], recv_sem=recv.at[a, j],
                                                          device_id=(x, y, 1 - c), device_id_type=MESH))
        for d in sends:
            d.start()
        for s, r in zip(sends, recvs):
            s.wait_send()
            r.wait_recv()

    return pl.pallas_call(
        body, name=name, in_specs=[_HBM] * n, out_specs=[_HBM] * n,
        out_shape=[jax.ShapeDtypeStruct(a.shape, a.dtype) for a in arrs],
        input_output_aliases={i: i for i in range(n)},
        scratch_shapes=[pltpu.SemaphoreType.DMA((n, 3))] * 2,
    )(*arrs)


def _blocks_of(r, c):
    if r * c * 4 <= (2 << 20):
        return r, c
    if r % 128 == 0:
        return 128, c
    return r, 256


def _pair_sum(name, own, got, core):
    _, r, c = got.shape
    rb, cb = _blocks_of(r, c)

    def body(s_ref, a_ref, b_ref, o_ref):
        o_ref[...] = (a_ref[...].astype(F32) + b_ref[...].astype(F32)).astype(o_ref.dtype)

    return pl.pallas_call(
        body, name=name, out_shape=jax.ShapeDtypeStruct((4, r, c), got.dtype),
        grid_spec=pltpu.PrefetchScalarGridSpec(
            num_scalar_prefetch=1, grid=(4, r // rb, c // cb),
            in_specs=[pl.BlockSpec((None, rb, cb), lambda k, i, j, s: (2 * k + s[0], i, j)),
                      pl.BlockSpec((None, rb, cb), lambda k, i, j, s: (k, i, j))],
            out_specs=pl.BlockSpec((None, rb, cb), lambda k, i, j, s: (k, i, j))),
        compiler_params=_params(3, 24 << 20),
    )(core, own, got)


def _adamw_math(w, g, m, v):
    m = B1 * m + (1.0 - B1) * g
    v = B2 * v + (1.0 - B2) * (g * g)
    m_hat = m / (1.0 - B1 ** STEP)
    v_hat = v / (1.0 - B2 ** STEP)
    return -LR * (m_hat / (jnp.sqrt(v_hat) + ADAM_EPS) + WD * w), m, v


def _adamw_big(name, own, got, chip, w, m, v):
    r, c = w.shape
    rb, cb = _blocks_of(r, c)

    def body(s_ref, o_ref, p_ref, w_ref, m_ref, v_ref, g_ref, d_ref, nm_ref, nv_ref):
        g = o_ref[...].astype(F32)
        for k in range(3):
            g = g + p_ref[k].astype(F32)
        g_ref[...] = g
        d_ref[...], nm_ref[...], nv_ref[...] = _adamw_math(w_ref[...], g, m_ref[...], v_ref[...])

    blk = pl.BlockSpec((rb, cb), lambda i, j, s: (i, j))
    return pl.pallas_call(
        body, name=name, out_shape=[jax.ShapeDtypeStruct((r, c), F32)] * 4,
        grid_spec=pltpu.PrefetchScalarGridSpec(
            num_scalar_prefetch=1, grid=(r // rb, c // cb),
            in_specs=[pl.BlockSpec((None, rb, cb), lambda i, j, s: (s[0], i, j)),
                      pl.BlockSpec((3, rb, cb), lambda i, j, s: (0, i, j)), blk, blk, blk],
            out_specs=[blk] * 4),
        compiler_params=_params(2, 40 << 20),
    )(chip, own, got, w, m, v)


def _adamw_multi(name, items, sums=(), row_blocks=1):
    n, ns = len(items), len(sums)

    def total(ref):
        g = ref[0]
        for d in range(1, ref.shape[0]):
            g = g + ref[d]
        return g

    def body(*refs):
        ins, outs = refs[:4 * n + ns], refs[4 * n + ns:]
        for t in range(n):
            p_ref, w_ref, m_ref, v_ref = ins[4 * t:4 * t + 4]
            g = total(p_ref)
            outs[4 * t][...] = g
            outs[4 * t + 1][...], outs[4 * t + 2][...], outs[4 * t + 3][...] = _adamw_math(
                w_ref[...], g, m_ref[...], v_ref[...])
        for t in range(ns):
            outs[4 * n + t][...] = total(ins[4 * n + t])

    def spec(shape, lead):
        blk = (shape[0] // row_blocks,) + tuple(shape[1:])
        nd = len(shape)
        if lead:
            return pl.BlockSpec((lead,) + blk, lambda i: (0, i) + (0,) * (nd - 1))
        return pl.BlockSpec(blk, lambda i: (i,) + (0,) * (nd - 1))

    operands, in_specs, out_specs, out_shape = [], [], [], []
    for parts, w, m, v in items:
        assert parts.shape[1:] == w.shape == m.shape == v.shape, (name, parts.shape, w.shape)
        operands += [parts, w, m, v]
        in_specs += [spec(w.shape, parts.shape[0])] + [spec(w.shape, 0)] * 3
        out_specs += [spec(w.shape, 0)] * 4
        out_shape += [jax.ShapeDtypeStruct(w.shape, F32)] * 4
    for parts in sums:
        operands.append(parts)
        in_specs.append(spec(parts.shape[1:], parts.shape[0]))
        out_specs.append(spec(parts.shape[1:], False))
        out_shape.append(jax.ShapeDtypeStruct(parts.shape[1:], F32))
    return pl.pallas_call(
        body, name=name, grid=(row_blocks,), in_specs=in_specs, out_specs=out_specs, out_shape=out_shape,
        compiler_params=_params(1, 56 << 20),
    )(*operands)


def kernel(x, meta_tokens, mix_norm, w_in, lam_re, lam_im, log_dt, b_re, b_im, c_re, c_im, d_skip, w_glu, b_glu, q_a_norm, w_q_b, kv_a_norm, w_kv_b, out_norm_ssm, out_norm_attn, w_out, ffn_norm, w_up, conv_w, conv_b, w_down, final_norm, loss_target, m_meta_tokens, m_mix_norm, m_w_in, m_lam_re, m_lam_im, m_log_dt, m_b_re, m_b_im, m_c_re, m_c_im, m_d_skip, m_w_glu, m_b_glu, m_q_a_norm, m_w_q_b, m_kv_a_norm, m_w_kv_b, m_out_norm_ssm, m_out_norm_attn, m_w_out, m_ffn_norm, m_w_up, m_conv_w, m_conv_b, m_w_down, m_final_norm, v_meta_tokens, v_mix_norm, v_w_in, v_lam_re, v_lam_im, v_log_dt, v_b_re, v_b_im, v_c_re, v_c_im, v_d_skip, v_w_glu, v_b_glu, v_q_a_norm, v_w_q_b, v_kv_a_norm, v_w_kv_b, v_out_norm_ssm, v_out_norm_attn, v_w_out, v_ffn_norm, v_w_up, v_conv_w, v_conv_b, v_w_down, v_final_norm):
    given = dict(locals())
    w = {n: given[n] for n in WEIGHTS}
    m = {n: given['m_' + n] for n in WEIGHTS}
    v = {n: given['v_' + n] for n in WEIGHTS}
    dev = 4 * lax.axis_index("x") + 2 * lax.axis_index("y") + lax.axis_index("c")

    shard = {
        'w_in': jnp.pad(w_in[0], ((0, 0), (0, D_IN_PAD - D_IN))),
        'w_glu': w_glu[0],
        'w_q_b': jnp.pad(w_q_b[0], ((0, 0), (0, HEAD_PAD - QK_NOPE - QK_ROPE))),
        'w_kv_b': w_kv_b[0],
        'w_out': w_out[0],
        'w_up': w_up[0],
        'w_down': w_down[0],
    }
    core = lax.axis_index("c").astype(jnp.int32).reshape(1)
    chip = (2 * lax.axis_index("x") + lax.axis_index("y")).astype(jnp.int32).reshape(1)
    early, late, ffn = ['w_in', 'w_glu', 'w_q_b', 'w_kv_b'], ['w_out', 'w_up', 'w_down'], ['w_down', 'w_up']
    shard = {n: a.astype(BF16) for n, a in shard.items()}

    gathered = _all_gather("gather_early", [shard[n] for n in early]
                           + [meta_tokens, jnp.pad(conv_w[0], ((0, 5), (0, 0)))])
    wf = dict(zip(early, gathered[:len(early)]))
    for n in ('w_in', 'w_glu'):
        wf[n] = wf[n].reshape(-1, wf[n].shape[-1])
    meta = gathered[-2].transpose(1, 0, 2).reshape(N_META, D)
    wf['conv_w'] = gathered[-1][:, :3].transpose(1, 0, 2).reshape(3, D_FF)
    gather_token, wait_late = _split_exchange("gather_late", [shard[n] for n in late],
                                              [(N_DEV,) + shard[n].shape for n in late], _gather_copies, 4,
                                              after=gathered[0])

    def late_weights(after):
        mine, landed = wait_late(after)
        full = _gather_forward("gather_late_forward",
                               [lax.dynamic_update_slice(a, s[None], (dev, 0, 0)) for s, a in zip(mine, landed)])
        return full[0].reshape(-1, D), full[1], full[2].reshape(-1, D)

    pending = []

    def send_grads(arrs):
        got = _exchange_cores("reduce_cores_ffn", arrs)
        sums = [_pair_sum("sum_cores_" + n, a, g, core) for n, a, g in zip(ffn, arrs, got)]
        token, wait = _split_exchange("reduce_chips_ffn", sums, [(3,) + s.shape[1:] for s in sums], _chips_copies, 3,
                                      after=sums[0])
        pending.append(wait)
        return token

    small = {n: w[n] for n in SMALL}
    small['mix_norm'] = mix_norm + gather_token[0:1, 0:1]
    loss_part, grad_x, dmeta, dcw4, big, sm = _local_step(x[0], loss_target[0], meta, small, wf, late_weights,
                                                          send_grads)
    grads, deltas, new_m, new_v = {}, {}, {}, {}

    def keep(n, outs):
        grads[n], deltas[n], new_m[n], new_v[n] = (o.reshape(w[n].shape) for o in outs)

    rest = [n for n in BIG if n not in ffn]
    from_core = _exchange_cores("reduce_cores", [big[n] for n in rest])
    chip_sums = [_pair_sum("sum_cores_" + n, big[n], got, core) for n, got in zip(rest, from_core)]
    small_parts, from_chips = _all_gather("reduce_chips_gather_small", [sm[n] for n in SMALL] + [dmeta, dcw4, loss_part],
                                          chip_blocks=chip_sums)
    ffn_sums, ffn_from_chips = pending[0](from_chips[0])
    for n, own, got in zip(rest + ffn, chip_sums + list(ffn_sums), list(from_chips) + list(ffn_from_chips)):
        cols = w[n].shape[2]
        keep(n, _adamw_big("adamw_" + n, own[:, :, :cols], got[:, :, :cols], chip, w[n][0], m[n][0], v[n][0]))

    parts = dict(zip(SMALL + ['meta_tokens', 'conv_w', 'loss'], small_parts))
    wide = [n for n in SMALL if n not in ('b_re', 'b_im')]
    outs = _adamw_multi("adamw_small",
                        [(parts[n],) + tuple(t[n].reshape(sm[n].shape) for t in (w, m, v)) for n in wide],
                        sums=[parts[n] for n in ('meta_tokens', 'conv_w', 'loss', 'b_re', 'b_im')])
    for i, n in enumerate(wide):
        keep(n, outs[4 * i:4 * i + 4])
    g_meta, g_cw4, loss, g_bre, g_bim = outs[-5:]
    gpc = lambda g: g.reshape(SSM_GROUP, N_GROUPS, SSM_STATE).transpose(1, 2, 0).reshape(-1, SSM_GROUP)[None]
    outs = _adamw_multi("adamw_b", [(gpc(g),) + tuple(t[n].reshape(-1, SSM_GROUP) for t in (w, m, v))
                                    for n, g in (('b_re', g_bre), ('b_im', g_bim))], row_blocks=8)
    keep('b_re', outs[:4])
    keep('b_im', outs[4:])

    g_meta = lax.dynamic_slice(g_meta, (0, dev * (D // N_DEV)), (N_META, D // N_DEV))
    g_conv = lax.dynamic_slice(g_cw4.transpose(1, 0, 2).reshape(3, D_FF), (0, dev * (D_FF // N_DEV)),
                               (3, D_FF // N_DEV))
    rows8 = lambda a: jnp.pad(a.reshape(3, D_FF // N_DEV), ((0, 5), (0, 0)))
    outs = _adamw_multi("adamw_cols", [(g_meta[None], meta_tokens, m['meta_tokens'], v['meta_tokens']),
                                       (rows8(g_conv)[None], rows8(conv_w), rows8(m['conv_w']), rows8(v['conv_w']))])
    keep('meta_tokens', outs[:4])
    keep('conv_w', [o[:3] for o in outs[4:]])

    return (loss[0, 0], grad_x[None], *[grads[n] for n in WEIGHTS], *[deltas[n] for n in WEIGHTS],
            *[new_m[n] for n in WEIGHTS], *[new_v[n] for n in WEIGHTS])
```

```python
import functools
import math

import jax
import jax.numpy as jnp
from jax import lax
from jax.experimental import pallas as pl
from jax.experimental.pallas import tpu as pltpu

F32, BF16 = jnp.float32, jnp.bfloat16

N_DEV = 8
N_META, SEQ, D = 16, 2048, 2048
T = N_META + SEQ
TR = 688
EPS = 1e-6
D_SSM, Q_LORA, KV_LORA, QK_ROPE = 1024, 512, 256, 64
D_IN = D_SSM + Q_LORA + KV_LORA + QK_ROPE
D_IN_PAD = 1920
N_HEADS, QK_NOPE, V_HEAD = 8, 128, 128
HEAD_PAD = 256
SM_SCALE = 1.0 / math.sqrt(QK_NOPE + QK_ROPE)
CHUNK = 64
N_GROUPS, SSM_GROUP, SSM_STATE = 64, 16, 64
N_SEG = 8
SEG = T // N_SEG
GQ = 8
D_FF = 5504
FF_PIECE = 1376
ROPE_BASE = 10000.0
LR, B1, B2, ADAM_EPS, WD, STEP = 0.001, 0.9, 0.999, 1e-08, 0.01, 10
VMEM_CAP = 60 * 1024 * 1024
MESH = pl.DeviceIdType.MESH

WEIGHTS = ['meta_tokens', 'mix_norm', 'w_in', 'lam_re', 'lam_im', 'log_dt', 'b_re', 'b_im', 'c_re', 'c_im',
           'd_skip', 'w_glu', 'b_glu', 'q_a_norm', 'w_q_b', 'kv_a_norm', 'w_kv_b', 'out_norm_ssm',
           'out_norm_attn', 'w_out', 'ffn_norm', 'w_up', 'conv_w', 'conv_b', 'w_down', 'final_norm']
BIG = ['w_in', 'w_glu', 'w_q_b', 'w_kv_b', 'w_out', 'w_up', 'w_down']
SMALL = [n for n in WEIGHTS if n not in BIG and n not in ('meta_tokens', 'conv_w')]


def _nbytes(shape, dtype):
    return math.prod(shape) * jnp.dtype(dtype).itemsize


def _params(sem, need):
    return pltpu.CompilerParams(dimension_semantics=("arbitrary",) * sem,
                                vmem_limit_bytes=int(min(VMEM_CAP, max(need, 16 * 1024 * 1024))))


_DIMS = {'nn': (((1,), (0,)), ((), ())), 'nt': (((1,), (1,)), ((), ())), 'tn': (((0,), (0,)), ((), ()))}


def _mm(name, a, b, mode, tm, tn, tk, out_dtype, stack=None, res=None):
    sa, sb, so = stack in ('a_out', 'ab_red'), stack in ('b_out', 'ab_red'), stack in ('a_out', 'b_out')
    nj = a.shape[0] if sa else (b.shape[0] if sb else 1)
    a2, b2 = a.shape[-2:], b.shape[-2:]
    if mode == 'nn':
        (m, k), (k2, n) = a2, b2
    elif mode == 'nt':
        (m, k), (n, k2) = a2, b2
    else:
        (k, m), (k2, n) = a2, b2
    assert k == k2 and m % tm == 0 and n % tn == 0 and k % tk == 0, (name, a.shape, b.shape)
    n_jo, n_jr, n_k = (nj if so else 1), (nj if stack == 'ab_red' else 1), k // tk
    grid = (n_jo, m // tm, n // tn, n_jr, n_k)
    a_blk, a_idx = ((tk, tm), lambda i, j, kk: (kk, i)) if mode == 'tn' else ((tm, tk), lambda i, j, kk: (i, kk))
    b_blk, b_idx = ((tn, tk), lambda i, j, kk: (j, kk)) if mode == 'nt' else ((tk, tn), lambda i, j, kk: (kk, j))

    def spec(blk, idx, stacked, on_out):
        if not stacked:
            return pl.BlockSpec(blk, lambda jo, i, j, jr, kk: idx(i, j, kk))
        return pl.BlockSpec((None,) + blk, lambda jo, i, j, jr, kk: ((jo if on_out else jr),) + idx(i, j, kk))

    o_idx = lambda i, j, kk: (i, j)
    in_specs = [spec(a_blk, a_idx, sa, stack == 'a_out'), spec(b_blk, b_idx, sb, stack == 'b_out')]
    operands = [a, b]
    if res is not None:
        in_specs.append(spec((tm, tn), o_idx, False, False))
        operands.append(res)
    n_red = n_jr * n_k
    dims = _DIMS[mode]

    def body(*refs):
        a_ref, b_ref = refs[0], refs[1]
        res_ref = refs[2] if res is not None else None
        o_ref = refs[3] if res is not None else refs[2]
        part = lax.dot_general(a_ref[...].astype(BF16), b_ref[...].astype(BF16), dims,
                               preferred_element_type=F32)

        def finish(total):
            if res_ref is not None:
                total = total + res_ref[...]
            o_ref[...] = total.astype(o_ref.dtype)

        if n_red == 1:
            finish(part)
        else:
            acc_ref = refs[-1]
            step = pl.program_id(3) * n_k + pl.program_id(4)

            @pl.when(step == 0)
            def _():
                acc_ref[...] = part

            @pl.when(step > 0)
            def _():
                acc_ref[...] += part

            @pl.when(step == n_red - 1)
            def _():
                finish(acc_ref[...])

    out_shape = ((nj,) if so else ()) + (m, n)
    need = 2 * (_nbytes(a_blk, a.dtype) + _nbytes(b_blk, b.dtype) + _nbytes((tm, tn), out_dtype)
                + (_nbytes((tm, tn), F32) if res is not None else 0)) + 3 * _nbytes((tm, tn), F32) + (4 << 20)
    return pl.pallas_call(
        body, name=name, grid=grid, in_specs=in_specs, out_specs=spec((tm, tn), o_idx, so, True),
        out_shape=jax.ShapeDtypeStruct(out_shape, out_dtype),
        scratch_shapes=[pltpu.VMEM((tm, tn), F32)] if n_red > 1 else [],
        compiler_params=_params(5, need),
    )(*operands)


def _rowwise(name, fn, row_ins, const_ins, row_outs, acc_outs, tr=TR):
    n_ri, n_ci, n_ro, n_ao = len(row_ins), len(const_ins), len(row_outs), len(acc_outs)

    def body(*refs):
        ri, ci = refs[:n_ri], refs[n_ri:n_ri + n_ci]
        ro, ao = refs[n_ri + n_ci:n_ri + n_ci + n_ro], refs[n_ri + n_ci + n_ro:]
        i = pl.program_id(0)
        outs = fn(i * tr, *[r[...] for r in ri], *[r[...] for r in ci])
        for r, v in zip(ro, outs[:n_ro]):
            r[...] = v.astype(r.dtype)
        if n_ao:
            @pl.when(i == 0)
            def _():
                for r, v in zip(ao, outs[n_ro:]):
                    r[...] = v

            @pl.when(i > 0)
            def _():
                for r, v in zip(ao, outs[n_ro:]):
                    r[...] += v

    in_specs = [pl.BlockSpec((tr, w), functools.partial(lambda cb, i: (i, cb), cb)) for _, w, cb in row_ins]
    in_specs += [pl.BlockSpec(c.shape, functools.partial(lambda nd, i: (0,) * nd, c.ndim)) for c in const_ins]
    out_specs = [pl.BlockSpec((tr, w), lambda i: (i, 0)) for w, _ in row_outs]
    out_specs += [pl.BlockSpec((1, w), lambda i: (0, 0)) for w in acc_outs]
    out_shape = [jax.ShapeDtypeStruct((T, w), dt) for w, dt in row_outs]
    out_shape += [jax.ShapeDtypeStruct((1, w), F32) for w in acc_outs]
    need = 2 * (sum(_nbytes((tr, w), a.dtype) for a, w, _ in row_ins) + sum(_nbytes((tr, w), dt) for w, dt in row_outs))
    need = 3 * need + (8 << 20)
    return pl.pallas_call(
        body, name=name, grid=(T // tr,), in_specs=in_specs, out_specs=out_specs, out_shape=out_shape,
        compiler_params=_params(1, need),
    )(*[a for a, _, _ in row_ins], *const_ins)


def _rms_fwd_fn(row0, h, g):
    h = h.astype(F32)
    r = lax.rsqrt(jnp.mean(h * h, axis=-1, keepdims=True) + EPS)
    return (h * r) * g, r


def _rms_bwd(dxn, h, r, g):
    xhat = h * r
    dxg = dxn * g
    dh = r * (dxg - xhat * jnp.mean(dxg * xhat, axis=-1, keepdims=True))
    return dh, jnp.sum(dxn * xhat, axis=0, keepdims=True)


def _rms_bwd_fn(row0, dxn, h, r, g):
    return _rms_bwd(dxn.astype(F32), h.astype(F32), r, g)


def _rms_bwd_res_fn(row0, dxn, h, r, res, g):
    dh, dg = _rms_bwd(dxn.astype(F32), h.astype(F32), r, g)
    return dh + res, dg


def _gelu_fn(row0, y):
    return (jax.nn.gelu(y),)


def _glu_fwd_fn(row0, y, gl, b, gain):
    ya = jax.nn.gelu(y) * jax.nn.sigmoid(gl + b)
    return _rms_fwd_fn(row0, ya, gain)


def _glu_bwd_fn(row0, dyan, y, gl, ra, b, gain):
    g = jax.nn.gelu(y)
    s = jax.nn.sigmoid(gl + b)
    dya, dgain = _rms_bwd(dyan, g * s, ra, gain)
    dgl = dya * g * s * (1.0 - s)
    return dgl, dya * s, dgain, jnp.sum(dgl, axis=0, keepdims=True)


def _gelu_bwd_fn(row0, dg, y):
    return (jax.vjp(jax.nn.gelu, y)[1](dg)[0],)


def _loss_fn(row0, h, tgt, gain):
    r = lax.rsqrt(jnp.mean(h * h, axis=-1, keepdims=True) + EPS)
    xhat = h * r
    rows = row0 + lax.broadcasted_iota(jnp.int32, h.shape, 0)
    err = jnp.where(rows >= N_META, xhat * gain - tgt, 0.0)
    loss = jnp.full((1, 128), jnp.sum(err * err) * (0.5 / D), F32)
    dh, dgain = _rms_bwd(err * (1.0 / D), h, r, gain)
    return dh, dgain, loss


def _rope_tables():
    pos = jnp.arange(T, dtype=jnp.int32)
    inv_freq = 1.0 / (ROPE_BASE ** (jnp.arange(0, QK_ROPE, 2, dtype=F32) / QK_ROPE))
    ang = pos.astype(F32)[:, None] * inv_freq[None, :]
    cos, sin, z32, z64 = jnp.cos(ang), jnp.sin(ang), jnp.zeros((T, 32), F32), jnp.zeros((T, 64), F32)
    return (jnp.concatenate([cos, cos, z64], 1), jnp.concatenate([-sin, z32, z64], 1),
            jnp.concatenate([z32, sin, z64], 1))


def _rot(x, cc, s1, s2):
    return x * cc + pltpu.roll(x, 96, 1) * s1 + pltpu.roll(x, 32, 1) * s2


def _derot(d, cc, s1, s2):
    return d * cc + pltpu.roll(d * s1, 32, 1) + pltpu.roll(d * s2, 96, 1)


def _chunk_of(pos):
    return jnp.where(pos < N_META, 0, 1 + ((pos - N_META) >> 6))


def _key_len(t):
    last_chunk = 1 + ((t + 1) * TR - 1 - N_META) // CHUNK
    return min(T, pl.cdiv(N_META + last_chunk * CHUNK, 128) * 128)


def _attn_probs(t, q_ref, kv_ref, kpe_ref, tq, tk):
    kl = _key_len(t)
    q = q_ref[...]
    qn = q[:, :QK_NOPE].astype(BF16)
    qp = _rot(q[:, QK_NOPE:], tq[0][...], tq[1][...], tq[2][...]).astype(BF16)
    kn = kv_ref[:kl, :QK_NOPE].astype(BF16)
    v = kv_ref[:kl, QK_NOPE:].astype(BF16)
    kp = _rot(kpe_ref[:kl, :], tk[0][:kl, :], tk[1][:kl, :], tk[2][:kl, :]).astype(BF16)
    s = lax.dot_general(qn, kn, _DIMS['nt'], preferred_element_type=F32)
    s = s + lax.dot_general(qp, kp, _DIMS['nt'], preferred_element_type=F32)
    qc = _chunk_of(t * TR + lax.broadcasted_iota(jnp.int32, (TR, 1), 0))
    kc = _chunk_of(lax.broadcasted_iota(jnp.int32, (1, kl), 1))
    s = jnp.where(kc <= qc, s * SM_SCALE, jnp.finfo(F32).min)
    p = jnp.exp(s - jnp.max(s, axis=-1, keepdims=True))
    p = p / jnp.sum(p, axis=-1, keepdims=True)
    return qn, qp, kn, kp, v, p


def _per_q_tile(fn):
    for t in range(T // TR):
        pl.when(pl.program_id(1) == t)(functools.partial(fn, t))


def _attn_specs(z, tabs):
    q_spec = pl.BlockSpec((None, TR, HEAD_PAD), lambda h, i: (h, i, 0))
    kv_spec = pl.BlockSpec((None, T, HEAD_PAD), lambda h, i: (h, 0, 0))
    kpe_spec = pl.BlockSpec((T, 128), lambda h, i: (0, (D_IN - QK_ROPE) // 128))
    tq = [pl.BlockSpec((TR, 128), lambda h, i: (i, 0))] * 3
    tk = [pl.BlockSpec((T, 128), lambda h, i: (0, 0))] * 3
    return [q_spec, kv_spec, kpe_spec] + tq + tk


def _attn_fwd(q, kv, z, tabs):
    def body(q_ref, kv_ref, kpe_ref, c1, c2, c3, k1, k2, k3, o_ref):
        def tile(t):
            _, _, _, _, v, p = _attn_probs(t, q_ref, kv_ref, kpe_ref, (c1, c2, c3), (k1, k2, k3))
            o_ref[...] = jnp.dot(p.astype(BF16), v, preferred_element_type=F32)

        _per_q_tile(tile)

    return pl.pallas_call(
        body, name="attn_fwd", grid=(N_HEADS, T // TR), in_specs=_attn_specs(z, tabs),
        out_specs=pl.BlockSpec((TR, V_HEAD), lambda h, i: (i, h)),
        out_shape=jax.ShapeDtypeStruct((T, N_HEADS * V_HEAD), F32),
        compiler_params=_params(2, 48 << 20),
    )(q, kv, z, *tabs, *tabs)


def _attn_bwd(q, kv, z, tabs, dyb):
    def body(q_ref, kv_ref, kpe_ref, c1, c2, c3, k1, k2, k3, do_ref, dq_ref, dkv_ref, dkpe_ref):
        @pl.when(pl.program_id(1) == 0)
        def _():
            dkv_ref[...] = jnp.zeros(dkv_ref.shape, F32)
            dkpe_ref[...] = jnp.zeros(dkpe_ref.shape, F32)

        def tile(t):
            kl = _key_len(t)
            qn, qp, kn, kp, v, p = _attn_probs(t, q_ref, kv_ref, kpe_ref, (c1, c2, c3), (k1, k2, k3))
            do = do_ref[...].astype(BF16)
            dp = lax.dot_general(do, v, _DIMS['nt'], preferred_element_type=F32)
            ds = (p * (dp - jnp.sum(p * dp, axis=-1, keepdims=True)) * SM_SCALE).astype(BF16)
            pb = p.astype(BF16)
            dq_ref[:, :QK_NOPE] = jnp.dot(ds, kn, preferred_element_type=F32).astype(dq_ref.dtype)
            dqp = _derot(jnp.dot(ds, kp, preferred_element_type=F32), c1[...], c2[...], c3[...])
            dq_ref[:, QK_NOPE:] = dqp.astype(dq_ref.dtype)
            dkv_ref[:kl, :QK_NOPE] += lax.dot_general(ds, qn, _DIMS['tn'], preferred_element_type=F32)
            dkv_ref[:kl, QK_NOPE:] += lax.dot_general(pb, do, _DIMS['tn'], preferred_element_type=F32)
            dkpe_ref[:kl, :] += lax.dot_general(ds, qp, _DIMS['tn'], preferred_element_type=F32)

        _per_q_tile(tile)

    return pl.pallas_call(
        body, name="attn_bwd", grid=(N_HEADS, T // TR),
        in_specs=_attn_specs(z, tabs) + [pl.BlockSpec((TR, V_HEAD), lambda h, i: (i, h))],
        out_specs=[pl.BlockSpec((None, TR, HEAD_PAD), lambda h, i: (h, i, 0)),
                   pl.BlockSpec((None, T, HEAD_PAD), lambda h, i: (h, 0, 0)),
                   pl.BlockSpec((None, T, 128), lambda h, i: (h, 0, 0))],
        out_shape=[jax.ShapeDtypeStruct((N_HEADS, T, HEAD_PAD), BF16),
                   jax.ShapeDtypeStruct((N_HEADS, T, HEAD_PAD), F32),
                   jax.ShapeDtypeStruct((N_HEADS, T, 128), F32)],
        compiler_params=_params(2, 56 << 20),
    )(q, kv, z, *tabs, *tabs, dyb)


def _kpe_bwd(dkpe, tabs):
    def body(d_ref, c1, c2, c3, o_ref):
        d = d_ref[0]
        for h in range(1, N_HEADS):
            d = d + d_ref[h]
        o_ref[...] = _derot(d, c1[...], c2[...], c3[...]).astype(o_ref.dtype)

    tile = pl.BlockSpec((TR, 128), lambda i: (i, 0))
    return pl.pallas_call(
        body, name="kpe_bwd", grid=(T // TR,),
        in_specs=[pl.BlockSpec((N_HEADS, TR, 128), lambda i: (0, i, 0)), tile, tile, tile], out_specs=tile,
        out_shape=jax.ShapeDtypeStruct((T, 128), BF16), compiler_params=_params(1, 16 << 20),
    )(dkpe, *tabs)


def _s5_prep_math(lam_re, lam_im, log_dt, bt_re, bt_im):
    dt = jnp.exp(log_dt)
    mag = jnp.exp(lam_re * dt)
    a_re, a_im = mag * jnp.cos(lam_im * dt), mag * jnp.sin(lam_im * dt)
    den = lam_re * lam_re + lam_im * lam_im
    w_re = ((a_re - 1.0) * lam_re + a_im * lam_im) / den
    w_im = (a_im * lam_re - (a_re - 1.0) * lam_im) / den
    return a_re, a_im, w_re[None] * bt_re - w_im[None] * bt_im, w_re[None] * bt_im + w_im[None] * bt_re


def _eye_groups():
    return (lax.broadcasted_iota(jnp.int32, (N_GROUPS, N_GROUPS), 0)
            == lax.broadcasted_iota(jnp.int32, (N_GROUPS, N_GROUPS), 1)).astype(F32)


def _row_to_col(row):
    return jnp.sum(_eye_groups() * row, axis=1, keepdims=True)


def _s5_prep(lam_re, lam_im, log_dt, bt_re, bt_im):
    def body(lr, li, ld, br, bi, ar, ai, bbr, bbi):
        ar[...], ai[...], bbr[...], bbi[...] = _s5_prep_math(lr[...], li[...], _row_to_col(ld[...]), br[...], bi[...])

    gp, cgp = jax.ShapeDtypeStruct((N_GROUPS, SSM_STATE), F32), jax.ShapeDtypeStruct(bt_re.shape, F32)
    return pl.pallas_call(body, name="s5_prep", out_shape=[gp, gp, cgp, cgp])(lam_re, lam_im, log_dt, bt_re, bt_im)


def _s5_prep_bwd(lam_re, lam_im, log_dt, bt_re, bt_im, da_re, da_im, dbb_re, dbb_im):
    def body(lr, li, ld, br, bi, dar, dai, dbr, dbi, o1, o2, o3, o4, o5):
        _, vjp = jax.vjp(_s5_prep_math, lr[...], li[...], _row_to_col(ld[...]), br[...], bi[...])
        o1[...], o2[...], dld, o4[...], o5[...] = vjp((dar[...], dai[...], dbr[...], dbi[...]))
        o3[...] = jnp.sum(_eye_groups() * dld, axis=0, keepdims=True)

    ins = (lam_re, lam_im, log_dt, bt_re, bt_im)
    return pl.pallas_call(body, name="s5_prep_bwd", out_shape=[jax.ShapeDtypeStruct(a.shape, F32) for a in ins])(
        *ins, da_re, da_im, dbb_re, dbb_im)


def _cmul(ar, ai, br, bi):
    return ar * br - ai * bi, ar * bi + ai * br


def _seg_rows(i):
    return pl.ds(pl.multiple_of(i * N_SEG, N_SEG), N_SEG)


def _scan(xr, xi, ar, ai, reverse):
    zero = jnp.zeros(ar.shape, F32)

    def local(j, carry):
        rows = _seg_rows(SEG - 1 - j if reverse else j)
        nr, ni = _cmul(ar, ai, *carry)
        nr, ni = nr + xr[rows, :], ni + xi[rows, :]
        xr[rows, :], xi[rows, :] = nr, ni
        return nr, ni

    er, ei = lax.fori_loop(0, SEG, local, (zero, zero))
    pr, pi = ar, ai
    for _ in range(8):
        pr, pi = _cmul(pr, pi, pr, pi)
    pr, pi = _cmul(*_cmul(pr, pi, ar, ai), ar, ai)
    row = lax.broadcasted_iota(jnp.int32, ar.shape, 0)
    edge, shift = (N_SEG - 1, N_SEG - 1) if reverse else (0, 1)
    hr, hi = zero, zero
    for _ in range(N_SEG - 1):
        tr_, ti_ = _cmul(pr, pi, hr, hi)
        hr = jnp.where(row == edge, 0.0, pltpu.roll(tr_ + er, shift, 0))
        hi = jnp.where(row == edge, 0.0, pltpu.roll(ti_ + ei, shift, 0))

    def fix(j, carry):
        rows = _seg_rows(SEG - 1 - j if reverse else j)
        xr[rows, :] += carry[0]
        xi[rows, :] += carry[1]
        return _cmul(ar, ai, *carry)

    lax.fori_loop(0, SEG, fix, _cmul(ar, ai, hr, hi))
    return hr, hi


def _split(x):
    hi = x.astype(BF16)
    return hi, (x - hi.astype(F32)).astype(BF16)


def _dot3(xs, ys, mode='nn'):
    d = lambda p, q: lax.dot_general(p, q, _DIMS[mode], preferred_element_type=F32)
    return d(xs[0], ys[0]) + (d(xs[1], ys[0]) + d(xs[0], ys[1]))


def _s5_hidden(us, a_ref, bre_ref, bim_ref, hre, him):
    hre[...] = _dot3(us, _split(bre_ref[...]))
    him[...] = _dot3(us, _split(bim_ref[...]))
    return _scan(hre, him, a_ref[0], a_ref[1], reverse=False)


def _s5_specs():
    col = pl.BlockSpec((T, 128), lambda q: (0, q))
    a_spec = pl.BlockSpec((None, 2, N_SEG, GQ * SSM_STATE), lambda q: (q, 0, 0, 0))
    wide = pl.BlockSpec((None, 128, GQ * SSM_STATE), lambda q: (q, 0, 0))
    tall = pl.BlockSpec((None, GQ * SSM_STATE, 128), lambda q: (q, 0, 0))
    d_spec = pl.BlockSpec((1, 128), lambda q: (0, q))
    return col, a_spec, wide, tall, d_spec


def _s5_fwd(u, a8, bre, bim, cre, cim, dsk):
    def body(u_ref, a_ref, bre_ref, bim_ref, cre_ref, cim_ref, d_ref, y_ref, hre, him):
        u_ = u_ref[...]
        _s5_hidden(_split(u_), a_ref, bre_ref, bim_ref, hre, him)
        y_ref[...] = (_dot3(_split(hre[...]), _split(cre_ref[...]))
                      - _dot3(_split(him[...]), _split(cim_ref[...])) + d_ref[...] * u_)

    col, a_spec, wide, tall, d_spec = _s5_specs()
    return pl.pallas_call(
        body, name="s5_fwd", grid=(N_GROUPS // GQ,), in_specs=[col, a_spec, wide, wide, tall, tall, d_spec],
        out_specs=col, out_shape=jax.ShapeDtypeStruct((T, D_SSM), F32),
        scratch_shapes=[pltpu.VMEM((T, GQ * SSM_STATE), F32)] * 2,
        compiler_params=_params(1, 40 << 20),
    )(u, a8, bre, bim, cre, cim, dsk)


def _s5_bwd(u, dy, a8, bre, bim, cre, cim, dsk):
    def body(u_ref, dy_ref, a_ref, bre_ref, bim_ref, cre_ref, cim_ref, d_ref,
             du_ref, dbre_ref, dbim_ref, dcre_ref, dcim_ref, da_ref, dd_ref, hre, him, lre, lim):
        u_, dy_ = u_ref[...], dy_ref[...]
        us, dys = _split(u_), _split(dy_)
        ar, ai = a_ref[0], a_ref[1]
        bres, bims = _split(bre_ref[...]), _split(bim_ref[...])
        hre[...] = _dot3(us, bres)
        him[...] = _dot3(us, bims)
        h0r, h0i = _scan(hre, him, ar, ai, reverse=False)
        lre[...] = _dot3(dys, _split(cre_ref[...]), 'nt')
        lim[...] = -_dot3(dys, _split(cim_ref[...]), 'nt')
        _scan(lre, lim, ar, -ai, reverse=True)

        def acc_da(i, carry):
            lr, li = lre[_seg_rows(i), :], lim[_seg_rows(i), :]
            pr, pi = hre[_seg_rows(i - 1), :], him[_seg_rows(i - 1), :]
            return carry[0] + lr * pr + li * pi, carry[1] + li * pr - lr * pi

        lr, li = lre[_seg_rows(0), :], lim[_seg_rows(0), :]
        dar, dai = lax.fori_loop(1, SEG, acc_da, (lr * h0r + li * h0i, li * h0r - lr * h0i))
        da_ref[0:1, :] = jnp.sum(dar, axis=0, keepdims=True)
        da_ref[1:2, :] = jnp.sum(dai, axis=0, keepdims=True)
        lrs, lis = _split(lre[...]), _split(lim[...])
        du_ref[...] = dy_ * d_ref[...] + _dot3(lrs, bres, 'nt') + _dot3(lis, bims, 'nt')
        dbre_ref[...] = _dot3(us, lrs, 'tn')
        dbim_ref[...] = _dot3(us, lis, 'tn')
        dcre_ref[...] = _dot3(_split(hre[...]), dys, 'tn')
        dcim_ref[...] = -_dot3(_split(him[...]), dys, 'tn')
        dd_ref[...] = jnp.sum(dy_ * u_, axis=0, keepdims=True)

    col, a_spec, wide, tall, d_spec = _s5_specs()
    nq, ns = N_GROUPS // GQ, GQ * SSM_STATE
    return pl.pallas_call(
        body, name="s5_bwd", grid=(nq,),
        in_specs=[col, col, a_spec, wide, wide, tall, tall, d_spec],
        out_specs=[col, wide, wide, tall, tall, pl.BlockSpec((None, 2, ns), lambda q: (q, 0, 0)), d_spec],
        out_shape=[jax.ShapeDtypeStruct((T, D_SSM), F32),
                   jax.ShapeDtypeStruct((nq, 128, ns), F32), jax.ShapeDtypeStruct((nq, 128, ns), F32),
                   jax.ShapeDtypeStruct((nq, ns, 128), F32), jax.ShapeDtypeStruct((nq, ns, 128), F32),
                   jax.ShapeDtypeStruct((nq, 2, ns), F32), jax.ShapeDtypeStruct((1, D_SSM), F32)],
        scratch_shapes=[pltpu.VMEM((T, ns), F32)] * 4,
        compiler_params=_params(1, 56 << 20),
    )(u, dy, a8, bre, bim, cre, cim, dsk)


def _to_segments(a):
    return a.reshape(N_SEG, SEG, a.shape[-1]).transpose(1, 0, 2).reshape(T, a.shape[-1])


def _from_segments(a):
    return a.reshape(SEG, N_SEG, a.shape[-1]).transpose(1, 0, 2).reshape(T, a.shape[-1])


def _block_diag_b(bbt):
    nq = N_GROUPS // GQ
    x = bbt.transpose(1, 0, 2).reshape(nq, GQ, SSM_GROUP, 1, SSM_STATE)
    eye = jnp.eye(GQ, dtype=F32)[None, :, None, :, None]
    return (x * eye).reshape(nq, GQ * SSM_GROUP, GQ * SSM_STATE)


def _block_diag_c(c):
    nq = N_GROUPS // GQ
    x = c.reshape(nq, GQ, SSM_GROUP, SSM_STATE).transpose(0, 1, 3, 2)[:, :, :, None, :]
    eye = jnp.eye(GQ, dtype=F32)[None, :, None, :, None]
    return (x * eye).reshape(nq, GQ * SSM_STATE, GQ * SSM_GROUP)


def _ff_specs():
    up_spec = pl.BlockSpec((2, None, T, 128), lambda p, cb: (0, p, 0, cb))
    piece = pl.BlockSpec((None, T, 128), lambda p, cb: (p, 0, cb))
    cw_spec = pl.BlockSpec((None, 3, 128), lambda p, cb: (p, 0, cb))
    cb_spec = pl.BlockSpec((None, 1, 128), lambda p, cb: (p, 0, cb))
    return up_spec, piece, cw_spec, cb_spec


def _conv_gate(gate, w, cb, pad_ref):
    pad_ref[0:8, :] = jnp.zeros((8, 128), F32)
    pad_ref[8:8 + T, :] = gate
    g1, g2 = pad_ref[pl.ds(7, T), :], pad_ref[pl.ds(6, T), :]
    return w[0:1, :] * g2 + w[1:2, :] * g1 + w[2:3, :] * gate + cb, g1, g2


def _ff_act(up, cw4, cb4):
    def body(up_ref, w_ref, b_ref, o_ref, pad_ref):
        gc, _, _ = _conv_gate(up_ref[0], w_ref[...], b_ref[...], pad_ref)
        o_ref[...] = (jax.nn.silu(gc) * up_ref[1]).astype(o_ref.dtype)

    up_spec, piece, cw_spec, cb_spec = _ff_specs()
    n_cb = pl.cdiv(FF_PIECE, 128)
    return pl.pallas_call(
        body, name="ff_act", grid=(4, n_cb), in_specs=[up_spec, cw_spec, cb_spec], out_specs=piece,
        out_shape=jax.ShapeDtypeStruct((4, T, FF_PIECE), BF16),
        scratch_shapes=[pltpu.VMEM((T + 8, 128), F32)],
        compiler_params=_params(2, 24 << 20),
    )(up.reshape(2, 4, T, FF_PIECE), cw4, cb4)


def _ff_act_bwd(up, dact, cw4, cb4):
    def body(up_ref, da_ref, w_ref, b_ref, dup_ref, dw_ref, db_ref, pad_ref, pad2_ref):
        gate, val, w = up_ref[0], up_ref[1], w_ref[...]
        gc, g1, g2 = _conv_gate(gate, w, b_ref[...], pad_ref)
        sg = jax.nn.sigmoid(gc)
        da = da_ref[...]
        dup_ref[1] = (da * gc * sg).astype(dup_ref.dtype)
        dgc = da * val * sg * (1.0 + gc * (1.0 - sg))
        db_ref[...] = jnp.sum(dgc, axis=0, keepdims=True)
        dw_ref[0:1, :] = jnp.sum(dgc * g2, axis=0, keepdims=True)
        dw_ref[1:2, :] = jnp.sum(dgc * g1, axis=0, keepdims=True)
        dw_ref[2:3, :] = jnp.sum(dgc * gate, axis=0, keepdims=True)
        pad2_ref[0:T, :] = dgc
        pad2_ref[T:T + 8, :] = jnp.zeros((8, 128), F32)
        d1, d2 = pad2_ref[pl.ds(1, T), :], pad2_ref[pl.ds(2, T), :]
        dup_ref[0] = (w[2:3, :] * dgc + w[1:2, :] * d1 + w[0:1, :] * d2).astype(dup_ref.dtype)

    up_spec, piece, cw_spec, cb_spec = _ff_specs()
    n_cb = pl.cdiv(FF_PIECE, 128)
    dup, dw, db = pl.pallas_call(
        body, name="ff_act_bwd", grid=(4, n_cb), in_specs=[up_spec, piece, cw_spec, cb_spec],
        out_specs=[up_spec, cw_spec, cb_spec],
        out_shape=[jax.ShapeDtypeStruct((2, 4, T, FF_PIECE), BF16), jax.ShapeDtypeStruct((4, 3, FF_PIECE), F32),
                   jax.ShapeDtypeStruct((4, 1, FF_PIECE), F32)],
        scratch_shapes=[pltpu.VMEM((T + 8, 128), F32)] * 2,
        compiler_params=_params(2, 32 << 20),
    )(up.reshape(2, 4, T, FF_PIECE), dact, cw4, cb4)
    return dup.reshape(8, T, FF_PIECE), dw, db


def _local_step(x, tgt, meta, p, wf, late_weights, send_grads):
    sm = {}
    h0 = jnp.concatenate([meta, x], axis=0)
    tgt_pad = jnp.concatenate([jnp.zeros((N_META, D), F32), tgt], axis=0)
    tabs = _rope_tables()
    row = lambda a: a.reshape(1, -1)

    xn, r0 = _rowwise("rms_mix", _rms_fwd_fn, [(h0, D, 0)], [row(p['mix_norm'])], [(D, BF16), (1, F32)], [])
    z = _mm("in_proj", xn, wf['w_in'], 'nn', TR, 640, D, F32)
    lam_re, lam_im = p['lam_re'].reshape(N_GROUPS, SSM_STATE), p['lam_im'].reshape(N_GROUPS, SSM_STATE)
    log_dt = p['log_dt'].reshape(1, N_GROUPS)
    bt_re = p['b_re'].reshape(N_GROUPS, SSM_STATE, SSM_GROUP).transpose(2, 0, 1)
    bt_im = p['b_im'].reshape(N_GROUPS, SSM_STATE, SSM_GROUP).transpose(2, 0, 1)
    c_re = p['c_re'].reshape(N_GROUPS, SSM_GROUP, SSM_STATE)
    c_im = p['c_im'].reshape(N_GROUPS, SSM_GROUP, SSM_STATE)
    a_re, a_im, bbt_re, bbt_im = _s5_prep(lam_re, lam_im, log_dt, bt_re, bt_im)
    nq, ns = N_GROUPS // GQ, GQ * SSM_STATE
    a8 = jnp.broadcast_to(jnp.stack([a_re.reshape(nq, ns), a_im.reshape(nq, ns)], 1)[:, :, None, :],
                          (nq, 2, N_SEG, ns))
    bre, bim = _block_diag_b(bbt_re), _block_diag_b(bbt_im)
    cre, cim = _block_diag_c(c_re), _block_diag_c(c_im)
    dsk = row(p['d_skip'])
    u_seg = _to_segments(z[:, :D_SSM])
    y_ssm = _from_segments(_s5_fwd(u_seg, a8, bre, bim, cre, cim, dsk))
    g, = _rowwise("gelu", _gelu_fn, [(y_ssm, D_SSM, 0)], [], [(D_SSM, BF16)], [])
    gl = _mm("glu_proj", g, wf['w_glu'], 'nn', TR, D_SSM, D_SSM, F32)
    yan, ra = _rowwise("glu_norm", _glu_fwd_fn, [(y_ssm, D_SSM, 0), (gl, D_SSM, 0)],
                       [row(p['b_glu']), row(p['out_norm_ssm'])], [(D_SSM, BF16), (1, F32)], [])
    qn, rq = _rowwise("rms_q", _rms_fwd_fn, [(z, Q_LORA, D_SSM // Q_LORA)], [row(p['q_a_norm'])],
                      [(Q_LORA, BF16), (1, F32)], [])
    kvn, rkv = _rowwise("rms_kv", _rms_fwd_fn, [(z, KV_LORA, (D_SSM + Q_LORA) // KV_LORA)], [row(p['kv_a_norm'])],
                        [(KV_LORA, BF16), (1, F32)], [])
    q = _mm("q_proj", qn, wf['w_q_b'], 'nn', T, HEAD_PAD, Q_LORA, F32, stack='b_out')
    kv = _mm("kv_proj", kvn, wf['w_kv_b'], 'nn', T, HEAD_PAD, KV_LORA, F32, stack='b_out')
    yb = _attn_fwd(q, kv, z, tabs)
    ybn, rb = _rowwise("rms_attn", _rms_fwd_fn, [(yb, D_SSM, 0)], [row(p['out_norm_attn'])],
                       [(D_SSM, BF16), (1, F32)], [])
    y = jnp.concatenate([yan, ybn], axis=1)
    wf = dict(wf)
    wf['w_out'], wf['w_up'], wf['w_down'] = late_weights(y)
    h1 = _mm("out_proj", y, wf['w_out'], 'nn', TR, 1024, D, F32, res=h0)
    xn2, r1 = _rowwise("rms_ffn", _rms_fwd_fn, [(h1, D, 0)], [row(p['ffn_norm'])], [(D, BF16), (1, F32)], [])
    up = _mm("up_proj", xn2, wf['w_up'], 'nn', TR, FF_PIECE, D, F32, stack='b_out')
    cw4 = wf['conv_w'].reshape(3, 4, FF_PIECE).transpose(1, 0, 2)
    cb4 = p['conv_b'].reshape(4, 1, FF_PIECE)
    act = _ff_act(up, cw4, cb4)
    wd4 = wf['w_down'].reshape(4, FF_PIECE, D)
    h2 = _mm("down_proj", act, wd4, 'nn', TR, 1024, FF_PIECE, F32, stack='ab_red', res=h1)
    dh2, sm['final_norm'], loss = _rowwise("loss", _loss_fn, [(h2, D, 0), (tgt_pad, D, 0)], [row(p['final_norm'])],
                                           [(D, F32)], [D, 128])

    big = {}
    dact = _mm("down_bwd_x", dh2, wd4, 'nt', TR, FF_PIECE, D, F32, stack='b_out')
    g_down = _mm("down_bwd_w", act, dh2, 'tn', FF_PIECE, 512, T, BF16, stack='a_out').reshape(
        N_DEV, D_FF // N_DEV, D)
    dup, dcw4, sm['conv_b'] = _ff_act_bwd(up, dact, cw4, cb4)
    dxn2 = _mm("up_bwd_x", dup, wf['w_up'], 'nt', TR, 1024, FF_PIECE, F32, stack='ab_red')
    g_up = _mm("up_bwd_w", xn2, dup, 'tn', 1024, FF_PIECE, T, BF16, stack='b_out')
    token = send_grads(['w_down', 'w_up'], [g_down, g_up])
    dh1, sm['ffn_norm'] = _rowwise("rms_ffn_bwd", _rms_bwd_res_fn,
                                   [(dxn2, D, 0), (h1, D, 0), (r1, 1, 0), (dh2, D, 0)],
                                   [row(p['ffn_norm']) + token[0:1, 0:1]], [(D, F32)], [D])
    dy = _mm("out_bwd_x", dh1, wf['w_out'], 'nt', TR, 1024, D, F32)
    g_out = _mm("out_bwd_w", y, dh1, 'tn', 1024, 1024, T, BF16).reshape(N_DEV, D // N_DEV, D)
    dgl, dgd, sm['out_norm_ssm'], sm['b_glu'] = _rowwise(
        "glu_bwd", _glu_bwd_fn, [(dy, D_SSM, 0), (y_ssm, D_SSM, 0), (gl, D_SSM, 0), (ra, 1, 0)],
        [row(p['b_glu']), row(p['out_norm_ssm'])], [(D_SSM, BF16), (D_SSM, F32)], [D_SSM, D_SSM])
    dg = _mm("glu_bwd_x", dgl, wf['w_glu'], 'nt', TR, D_SSM, D_SSM, F32, res=dgd)
    g_glu = _mm("glu_bwd_w", g, dgl, 'tn', D_SSM, D_SSM, T, BF16).reshape(N_DEV, D_SSM // N_DEV, D_SSM)
    dy_ssm, = _rowwise("gelu_bwd", _gelu_bwd_fn, [(dg, D_SSM, 0), (y_ssm, D_SSM, 0)], [], [(D_SSM, F32)], [])
    dyb, sm['out_norm_attn'] = _rowwise("rms_attn_bwd", _rms_bwd_fn, [(dy, D_SSM, 1), (yb, D_SSM, 0), (rb, 1, 0)],
                                        [row(p['out_norm_attn'])], [(D_SSM, F32)], [D_SSM])
    dq, dkv, dkpe = _attn_bwd(q, kv, z, tabs, dyb)
    g_q = _mm("q_bwd_w", qn, dq, 'tn', Q_LORA, HEAD_PAD, T, BF16, stack='b_out')
    dqn = _mm("q_bwd_x", dq, wf['w_q_b'], 'nt', TR, Q_LORA, HEAD_PAD, F32, stack='ab_red')
    g_kv = _mm("kv_bwd_w", kvn, dkv, 'tn', KV_LORA, HEAD_PAD, T, BF16, stack='b_out')
    dkvn = _mm("kv_bwd_x", dkv, wf['w_kv_b'], 'nt', TR, KV_LORA, HEAD_PAD, F32, stack='ab_red')
    token = send_grads(['w_out', 'w_glu', 'w_q_b', 'w_kv_b'], [g_out, g_glu, g_q, g_kv])
    dq_a, sm['q_a_norm'] = _rowwise("rms_q_bwd", _rms_bwd_fn,
                                    [(dqn, Q_LORA, 0), (z, Q_LORA, D_SSM // Q_LORA), (rq, 1, 0)],
                                    [row(p['q_a_norm'])], [(Q_LORA, BF16)], [Q_LORA])
    dkv_a, sm['kv_a_norm'] = _rowwise("rms_kv_bwd", _rms_bwd_fn,
                                      [(dkvn, KV_LORA, 0), (z, KV_LORA, (D_SSM + Q_LORA) // KV_LORA), (rkv, 1, 0)],
                                      [row(p['kv_a_norm'])], [(KV_LORA, BF16)], [KV_LORA])
    dk_pe = _kpe_bwd(dkpe, tabs)
    du_seg, dbre, dbim, dcre, dcim, da, sm['d_skip'] = _s5_bwd(
        u_seg, _to_segments(dy_ssm), a8, bre, bim, cre, cim, dsk + token[0:1, 0:1])
    du = _from_segments(du_seg).astype(BF16)
    diag_b = lambda d: jnp.einsum('qgcgp->qgcp', d.reshape(nq, GQ, SSM_GROUP, GQ, SSM_STATE)).transpose(
        2, 0, 1, 3).reshape(SSM_GROUP, N_GROUPS, SSM_STATE)
    diag_c = lambda d: jnp.einsum('qgpgc->qgcp', d.reshape(nq, GQ, SSM_STATE, GQ, SSM_GROUP)).reshape(
        N_GROUPS * SSM_GROUP, SSM_STATE)
    dlam_re, dlam_im, dlog_dt, dbt_re, dbt_im = _s5_prep_bwd(
        lam_re, lam_im, log_dt, bt_re, bt_im, da[:, 0, :].reshape(N_GROUPS, SSM_STATE),
        da[:, 1, :].reshape(N_GROUPS, SSM_STATE), diag_b(dbre), diag_b(dbim))
    sm['lam_re'], sm['lam_im'], sm['log_dt'] = dlam_re, dlam_im, dlog_dt
    cgp = lambda d: d.reshape(SSM_GROUP * N_GROUPS, SSM_STATE)
    sm['b_re'], sm['b_im'] = cgp(dbt_re), cgp(dbt_im)
    sm['c_re'], sm['c_im'] = diag_c(dcre), diag_c(dcim)
    dz = jnp.concatenate([du, dq_a, dkv_a, dk_pe], axis=1)
    dxn = _mm("in_bwd_x", dz, wf['w_in'], 'nt', TR, 1024, D_IN_PAD, F32)
    big['w_in'] = _mm("in_bwd_w", xn, dz, 'tn', 1024, 640, T, BF16).reshape(N_DEV, D // N_DEV, D_IN_PAD)
    dh0, sm['mix_norm'] = _rowwise("rms_mix_bwd", _rms_bwd_res_fn,
                                   [(dxn, D, 0), (h0, D, 0), (r0, 1, 0), (dh1, D, 0)], [row(p['mix_norm'])],
                                   [(D, F32)], [D])
    return loss, dh0[N_META:], dh0[:N_META], dcw4, big, sm


def _place():
    x, y, c = lax.axis_index("x"), lax.axis_index("y"), lax.axis_index("c")
    return x, y, c, [(1 - x, y), (x, 1 - y), (1 - x, 1 - y)]


_HBM = pl.BlockSpec(memory_space=pltpu.HBM)


def _all_gather(name, arrs, chip_blocks=()):
    n, nc = len(arrs), len(chip_blocks)

    def body(*refs):
        ins, cins, outs, couts = refs[:n], refs[n:n + nc], refs[n + nc:2 * n + nc], refs[2 * n + nc:2 * (n + nc)]
        send, recv, loc, csend, crecv = refs[2 * (n + nc):]
        x, y, c, chips = _place()
        me, sib = (x, y, c), (x, y, 1 - c)

        def cp(a, k, blk, to, src=None):
            dst = outs[a].at[4 * blk[0] + 2 * blk[1] + blk[2]]
            return pltpu.make_async_remote_copy(src_ref=dst if src is None else src, dst_ref=dst,
                                                send_sem=send.at[a, k], recv_sem=recv.at[a, k],
                                                device_id=to, device_id_type=MESH)

        between_chips = [pltpu.make_async_remote_copy(src_ref=cins[a].at[2 * ch[0] + ch[1]], dst_ref=couts[a].at[j],
                                                      send_sem=csend.at[a, j], recv_sem=crecv.at[a, j],
                                                      device_id=(*ch, c), device_id_type=MESH)
                         for a in range(nc) for j, ch in enumerate(chips)]
        mine = [pltpu.make_async_copy(ins[a], outs[a].at[4 * x + 2 * y + c], loc.at[a]) for a in range(n)]
        first = []
        for a in range(n):
            first.append(cp(a, 0, me, sib, src=ins[a]))
            first += [cp(a, 1 + j, me, (*ch, c), src=ins[a]) for j, ch in enumerate(chips)]
        for f in mine + first + between_chips:
            f.start()
        passed = []
        for a in range(n):
            for j, ch in enumerate(chips):
                cp(a, 1 + j, (*ch, c), me).wait_recv()
                passed.append(cp(a, 4 + j, (*ch, c), sib))
                passed[-1].start()
        for a in range(n):
            cp(a, 0, sib, me).wait_recv()
            for j, ch in enumerate(chips):
                cp(a, 4 + j, (*ch, 1 - c), me).wait_recv()
        for f in first + passed:
            f.wait_send()
        for m in mine + between_chips:
            m.wait()

    out = pl.pallas_call(
        body, name=name, in_specs=[_HBM] * (n + nc), out_specs=[_HBM] * (n + nc),
        out_shape=[jax.ShapeDtypeStruct((N_DEV,) + a.shape, a.dtype) for a in arrs]
        + [jax.ShapeDtypeStruct((3,) + a.shape[1:], a.dtype) for a in chip_blocks],
        scratch_shapes=[pltpu.SemaphoreType.DMA((n, 7)), pltpu.SemaphoreType.DMA((n, 7)),
                        pltpu.SemaphoreType.DMA((n,)), pltpu.SemaphoreType.DMA((max(nc, 1), 3)),
                        pltpu.SemaphoreType.DMA((max(nc, 1), 3))],
    )(*arrs, *chip_blocks)
    return (out[:n], out[n:]) if nc else out


def _exchange_cores(name, arrs):
    n = len(arrs)

    def body(*refs):
        ins, outs = refs[:n], refs[n:2 * n]
        send, recv = refs[2 * n:]
        x, y, c, _ = _place()
        remote = [pltpu.make_async_remote_copy(src_ref=ins[a].at[2 * k + 1 - c], dst_ref=outs[a].at[k],
                                               send_sem=send.at[a, k], recv_sem=recv.at[a, k],
                                               device_id=(x, y, 1 - c), device_id_type=MESH)
                  for a in range(n) for k in range(4)]
        for d in remote:
            d.start()
        for d in remote:
            d.wait()

    return pl.pallas_call(
        body, name=name, in_specs=[_HBM] * n, out_specs=[_HBM] * n,
        out_shape=[jax.ShapeDtypeStruct((4,) + a.shape[1:], a.dtype) for a in arrs],
        scratch_shapes=[pltpu.SemaphoreType.DMA((n, 4))] * 2,
    )(*arrs)


_SEM = pl.BlockSpec(memory_space=pltpu.SEMAPHORE)
_EFFECT = pltpu.SideEffectType.DATAFLOW_SIDE_EFFECTING


def _split_exchange(name, srcs, land_shapes, copies_of, n_cp, after):
    n = len(srcs)
    lands = [pltpu.with_memory_space_constraint(lax.empty(s, a.dtype), pltpu.HBM) for s, a in zip(land_shapes, srcs)]

    n_sem = n * n_cp

    def descriptors(src_refs, land_refs, send, recv, mirror):
        x, y, c, chips = _place()
        out = []
        for a in range(n):
            for k, (src, dst, to, back) in enumerate(copies_of(a, src_refs[a], land_refs[a], x, y, c, chips)):
                out.append(pltpu.make_async_remote_copy(src_ref=src, dst_ref=back if mirror else dst,
                                                        send_sem=send[a * n_cp + k], recv_sem=recv[a * n_cp + k],
                                                        device_id=to, device_id_type=MESH))
        return out

    def start_body(*refs):
        sems = refs[2 * n + 1:2 * n + 1 + 2 * n_sem]
        for d in descriptors(refs[:n], refs[n:2 * n], sems[:n_sem], sems[n_sem:], False):
            d.start()
        refs[-1][...] = jnp.zeros((8, 128), F32)

    hbm_like = lambda arrs: [pltpu.HBM(a.shape, a.dtype) for a in arrs]
    res = pl.pallas_call(
        start_body, name=name + "_start", in_specs=[_HBM] * (2 * n) + [pl.BlockSpec(memory_space=pl.ANY)],
        out_specs=[_SEM] * (2 * n_sem) + [_HBM] * (2 * n) + [pl.BlockSpec(memory_space=pltpu.VMEM)],
        out_shape=[pltpu.SemaphoreType.DMA(())] * (2 * n_sem)
        + hbm_like(srcs) + hbm_like(lands) + [jax.ShapeDtypeStruct((8, 128), F32)],
        input_output_aliases={i: 2 * n_sem + i for i in range(2 * n)},
        compiler_params=pltpu.CompilerParams(has_side_effects=_EFFECT),
    )(*[pltpu.with_memory_space_constraint(a, pltpu.HBM) for a in srcs], *lands, after)
    sems, thru, token = res[:2 * n_sem], res[2 * n_sem:2 * n_sem + 2 * n], res[-1]

    def wait(after):
        def wait_body(*refs):
            s = refs[2 * n:2 * n + 2 * n_sem]
            for d in descriptors(refs[:n], refs[n:2 * n], s[:n_sem], s[n_sem:], True):
                d.wait_send()
                d.wait_recv()

        out = pl.pallas_call(
            wait_body, name=name + "_wait",
            in_specs=[_HBM] * (2 * n) + [_SEM] * (2 * n_sem) + [pl.BlockSpec(memory_space=pl.ANY)],
            out_specs=[_HBM] * (2 * n), out_shape=hbm_like(srcs) + hbm_like(lands),
            input_output_aliases={i: i for i in range(2 * n)},
            compiler_params=pltpu.CompilerParams(has_side_effects=_EFFECT),
        )(*thru, *sems, after)
        return out[:n], out[n:]

    return token, wait


def _gather_copies(a, src, land, x, y, c, chips):
    peers = [(x, y, 1 - c)] + [(*ch, c) for ch in chips]
    me = 4 * x + 2 * y + c
    return [(src, land.at[me], to, land.at[4 * to[0] + 2 * to[1] + to[2]]) for to in peers]


def _chips_copies(a, src, land, x, y, c, chips):
    return [(src.at[2 * ch[0] + ch[1]], land.at[j], (*ch, c), land.at[j]) for j, ch in enumerate(chips)]


def _gather_forward(name, arrs):
    n = len(arrs)

    def body(*refs):
        ins, outs = refs[:n], refs[n:2 * n]
        send, recv = refs[2 * n:]
        x, y, c, chips = _place()
        sends, recvs = [], []
        for a in range(n):
            for j, ch in enumerate(chips):
                mine, theirs = 4 * ch[0] + 2 * ch[1] + c, 4 * ch[0] + 2 * ch[1] + 1 - c
                sends.append(pltpu.make_async_remote_copy(src_ref=ins[a].at[mine], dst_ref=outs[a].at[mine],
                                                          send_sem=send.at[a, j], recv_sem=recv.at[a, j],
                                                          device_id=(x, y, 1 - c), device_id_type=MESH))
                recvs.append(pltpu.make_async_remote_copy(src_ref=ins[a].at[theirs], dst_ref=outs[a].at[theirs],
                                                          send_sem=send.at[a, j], recv_sem=recv.at[a, j],
                                                          device_id=(x, y, 1 - c), device_id_type=MESH))
        for d in sends:
            d.start()
        for s, r in zip(sends, recvs):
            s.wait_send()
            r.wait_recv()

    return pl.pallas_call(
        body, name=name, in_specs=[_HBM] * n, out_specs=[_HBM] * n,
        out_shape=[jax.ShapeDtypeStruct(a.shape, a.dtype) for a in arrs],
        input_output_aliases={i: i for i in range(n)},
        scratch_shapes=[pltpu.SemaphoreType.DMA((n, 3))] * 2,
    )(*arrs)


def _blocks_of(r, c):
    if r * c * 4 <= (2 << 20):
        return r, c
    if r % 128 == 0:
        return 128, c
    return r, 256


def _pair_sum(name, own, got, core):
    _, r, c = got.shape
    rb, cb = _blocks_of(r, c)

    def body(s_ref, a_ref, b_ref, o_ref):
        o_ref[...] = (a_ref[...].astype(F32) + b_ref[...].astype(F32)).astype(o_ref.dtype)

    return pl.pallas_call(
        body, name=name, out_shape=jax.ShapeDtypeStruct((4, r, c), got.dtype),
        grid_spec=pltpu.PrefetchScalarGridSpec(
            num_scalar_prefetch=1, grid=(4, r // rb, c // cb),
            in_specs=[pl.BlockSpec((None, rb, cb), lambda k, i, j, s: (2 * k + s[0], i, j)),
                      pl.BlockSpec((None, rb, cb), lambda k, i, j, s: (k, i, j))],
            out_specs=pl.BlockSpec((None, rb, cb), lambda k, i, j, s: (k, i, j))),
        compiler_params=_params(3, 24 << 20),
    )(core, own, got)


def _adamw_math(w, g, m, v):
    m = B1 * m + (1.0 - B1) * g
    v = B2 * v + (1.0 - B2) * (g * g)
    m_hat = m / (1.0 - B1 ** STEP)
    v_hat = v / (1.0 - B2 ** STEP)
    return -LR * (m_hat / (jnp.sqrt(v_hat) + ADAM_EPS) + WD * w), m, v


def _adamw_big(name, own, got, chip, w, m, v):
    r, c = w.shape
    rb, cb = _blocks_of(r, c)

    def body(s_ref, o_ref, p_ref, w_ref, m_ref, v_ref, g_ref, d_ref, nm_ref, nv_ref):
        g = o_ref[...].astype(F32)
        for k in range(3):
            g = g + p_ref[k].astype(F32)
        g_ref[...] = g
        d_ref[...], nm_ref[...], nv_ref[...] = _adamw_math(w_ref[...], g, m_ref[...], v_ref[...])

    blk = pl.BlockSpec((rb, cb), lambda i, j, s: (i, j))
    return pl.pallas_call(
        body, name=name, out_shape=[jax.ShapeDtypeStruct((r, c), F32)] * 4,
        grid_spec=pltpu.PrefetchScalarGridSpec(
            num_scalar_prefetch=1, grid=(r // rb, c // cb),
            in_specs=[pl.BlockSpec((None, rb, cb), lambda i, j, s: (s[0], i, j)),
                      pl.BlockSpec((3, rb, cb), lambda i, j, s: (0, i, j)), blk, blk, blk],
            out_specs=[blk] * 4),
        compiler_params=_params(2, 40 << 20),
    )(chip, own, got, w, m, v)


def _adamw_multi(name, items, sums=(), row_blocks=1):
    n, ns = len(items), len(sums)

    def total(ref):
        g = ref[0]
        for d in range(1, ref.shape[0]):
            g = g + ref[d]
        return g

    def body(*refs):
        ins, outs = refs[:4 * n + ns], refs[4 * n + ns:]
        for t in range(n):
            p_ref, w_ref, m_ref, v_ref = ins[4 * t:4 * t + 4]
            g = total(p_ref)
            outs[4 * t][...] = g
            outs[4 * t + 1][...], outs[4 * t + 2][...], outs[4 * t + 3][...] = _adamw_math(
                w_ref[...], g, m_ref[...], v_ref[...])
        for t in range(ns):
            outs[4 * n + t][...] = total(ins[4 * n + t])

    def spec(shape, lead):
        blk = (shape[0] // row_blocks,) + tuple(shape[1:])
        nd = len(shape)
        if lead:
            return pl.BlockSpec((lead,) + blk, lambda i: (0, i) + (0,) * (nd - 1))
        return pl.BlockSpec(blk, lambda i: (i,) + (0,) * (nd - 1))

    operands, in_specs, out_specs, out_shape = [], [], [], []
    for parts, w, m, v in items:
        assert parts.shape[1:] == w.shape == m.shape == v.shape, (name, parts.shape, w.shape)
        operands += [parts, w, m, v]
        in_specs += [spec(w.shape, parts.shape[0])] + [spec(w.shape, 0)] * 3
        out_specs += [spec(w.shape, 0)] * 4
        out_shape += [jax.ShapeDtypeStruct(w.shape, F32)] * 4
    for parts in sums:
        operands.append(parts)
        in_specs.append(spec(parts.shape[1:], parts.shape[0]))
        out_specs.append(spec(parts.shape[1:], False))
        out_shape.append(jax.ShapeDtypeStruct(parts.shape[1:], F32))
    return pl.pallas_call(
        body, name=name, grid=(row_blocks,), in_specs=in_specs, out_specs=out_specs, out_shape=out_shape,
        compiler_params=_params(1, 56 << 20),
    )(*operands)


def kernel(x, meta_tokens, mix_norm, w_in, lam_re, lam_im, log_dt, b_re, b_im, c_re, c_im, d_skip, w_glu, b_glu, q_a_norm, w_q_b, kv_a_norm, w_kv_b, out_norm_ssm, out_norm_attn, w_out, ffn_norm, w_up, conv_w, conv_b, w_down, final_norm, loss_target, m_meta_tokens, m_mix_norm, m_w_in, m_lam_re, m_lam_im, m_log_dt, m_b_re, m_b_im, m_c_re, m_c_im, m_d_skip, m_w_glu, m_b_glu, m_q_a_norm, m_w_q_b, m_kv_a_norm, m_w_kv_b, m_out_norm_ssm, m_out_norm_attn, m_w_out, m_ffn_norm, m_w_up, m_conv_w, m_conv_b, m_w_down, m_final_norm, v_meta_tokens, v_mix_norm, v_w_in, v_lam_re, v_lam_im, v_log_dt, v_b_re, v_b_im, v_c_re, v_c_im, v_d_skip, v_w_glu, v_b_glu, v_q_a_norm, v_w_q_b, v_kv_a_norm, v_w_kv_b, v_out_norm_ssm, v_out_norm_attn, v_w_out, v_ffn_norm, v_w_up, v_conv_w, v_conv_b, v_w_down, v_final_norm):
    given = dict(locals())
    w = {n: given[n] for n in WEIGHTS}
    m = {n: given['m_' + n] for n in WEIGHTS}
    v = {n: given['v_' + n] for n in WEIGHTS}
    dev = 4 * lax.axis_index("x") + 2 * lax.axis_index("y") + lax.axis_index("c")

    shard = {
        'w_in': jnp.pad(w_in[0], ((0, 0), (0, D_IN_PAD - D_IN))),
        'w_glu': w_glu[0],
        'w_q_b': jnp.pad(w_q_b[0], ((0, 0), (0, HEAD_PAD - QK_NOPE - QK_ROPE))),
        'w_kv_b': w_kv_b[0],
        'w_out': w_out[0],
        'w_up': w_up[0],
        'w_down': w_down[0],
    }
    core = lax.axis_index("c").astype(jnp.int32).reshape(1)
    chip = (2 * lax.axis_index("x") + lax.axis_index("y")).astype(jnp.int32).reshape(1)
    early, late = ['w_in', 'w_glu', 'w_q_b', 'w_kv_b'], ['w_out', 'w_up', 'w_down']
    shard = {n: a.astype(BF16) for n, a in shard.items()}

    gathered = _all_gather("gather_early", [shard[n] for n in early]
                           + [meta_tokens, jnp.pad(conv_w[0], ((0, 5), (0, 0)))])
    wf = dict(zip(early, gathered[:len(early)]))
    for n in ('w_in', 'w_glu'):
        wf[n] = wf[n].reshape(-1, wf[n].shape[-1])
    meta = gathered[-2].transpose(1, 0, 2).reshape(N_META, D)
    wf['conv_w'] = gathered[-1][:, :3].transpose(1, 0, 2).reshape(3, D_FF)
    gather_token, wait_late = _split_exchange("gather_late", [shard[n] for n in late],
                                              [(N_DEV,) + shard[n].shape for n in late], _gather_copies, 4,
                                              after=gathered[0])

    def late_weights(after):
        mine, landed = wait_late(after)
        full = _gather_forward("gather_late_forward",
                               [lax.dynamic_update_slice(a, s[None], (dev, 0, 0)) for s, a in zip(mine, landed)])
        return full[0].reshape(-1, D), full[1], full[2].reshape(-1, D)

    pending = []

    def send_grads(names, arrs):
        got = _exchange_cores("reduce_cores_" + names[0], arrs)
        sums = [_pair_sum("sum_cores_" + n, a, g, core) for n, a, g in zip(names, arrs, got)]
        token, wait = _split_exchange("reduce_chips_" + names[0], sums, [(3,) + s.shape[1:] for s in sums],
                                      _chips_copies, 3, after=sums[-1])
        pending.append((names, wait))
        return token

    small = {n: w[n] for n in SMALL}
    small['mix_norm'] = mix_norm + gather_token[0:1, 0:1]
    loss_part, grad_x, dmeta, dcw4, big, sm = _local_step(x[0], loss_target[0], meta, small, wf, late_weights,
                                                          send_grads)
    grads, deltas, new_m, new_v = {}, {}, {}, {}

    def keep(n, outs):
        grads[n], deltas[n], new_m[n], new_v[n] = (o.reshape(w[n].shape) for o in outs)

    rest = list(big)
    from_core = _exchange_cores("reduce_cores", [big[n] for n in rest])
    chip_sums = [_pair_sum("sum_cores_" + n, big[n], got, core) for n, got in zip(rest, from_core)]
    small_parts, from_chips = _all_gather("reduce_chips_gather_small", [sm[n] for n in SMALL] + [dmeta, dcw4, loss_part],
                                          chip_blocks=chip_sums)
    reduced = list(zip(rest, chip_sums, from_chips))
    for names, wait in pending:
        reduced += zip(names, *wait(from_chips[0]))
    for n, own, got in reduced:
        cols = w[n].shape[2]
        keep(n, _adamw_big("adamw_" + n, own[:, :, :cols], got[:, :, :cols], chip, w[n][0], m[n][0], v[n][0]))

    parts = dict(zip(SMALL + ['meta_tokens', 'conv_w', 'loss'], small_parts))
    wide = [n for n in SMALL if n not in ('b_re', 'b_im')]
    outs = _adamw_multi("adamw_small",
                        [(parts[n],) + tuple(t[n].reshape(sm[n].shape) for t in (w, m, v)) for n in wide],
                        sums=[parts[n] for n in ('meta_tokens', 'conv_w', 'loss', 'b_re', 'b_im')])
    for i, n in enumerate(wide):
        keep(n, outs[4 * i:4 * i + 4])
    g_meta, g_cw4, loss, g_bre, g_bim = outs[-5:]
    gpc = lambda g: g.reshape(SSM_GROUP, N_GROUPS, SSM_STATE).transpose(1, 2, 0).reshape(-1, SSM_GROUP)[None]
    outs = _adamw_multi("adamw_b", [(gpc(g),) + tuple(t[n].reshape(-1, SSM_GROUP) for t in (w, m, v))
                                    for n, g in (('b_re', g_bre), ('b_im', g_bim))], row_blocks=8)
    keep('b_re', outs[:4])
    keep('b_im', outs[4:])

    g_meta = lax.dynamic_slice(g_meta, (0, dev * (D // N_DEV)), (N_META, D // N_DEV))
    g_conv = lax.dynamic_slice(g_cw4.transpose(1, 0, 2).reshape(3, D_FF), (0, dev * (D_FF // N_DEV)),
                               (3, D_FF // N_DEV))
    rows8 = lambda a: jnp.pad(a.reshape(3, D_FF // N_DEV), ((0, 5), (0, 0)))
    outs = _adamw_multi("adamw_cols", [(g_meta[None], meta_tokens, m['meta_tokens'], v['meta_tokens']),
                                       (rows8(g_conv)[None], rows8(conv_w), rows8(m['conv_w']), rows8(v['conv_w']))])
    keep('meta_tokens', outs[:4])
    keep('conv_w', [o[:3] for o in outs[4:]])

    return (loss[0, 0], grad_x[None], *[grads[n] for n in WEIGHTS], *[deltas[n] for n in WEIGHTS],
            *[new_m[n] for n in WEIGHTS], *[new_v[n] for n in WEIGHTS])
```

```python
import functools
import math

import jax
import jax.numpy as jnp
from jax import lax
from jax.experimental import pallas as pl
from jax.experimental.pallas import tpu as pltpu

F32, BF16 = jnp.float32, jnp.bfloat16

N_DEV = 8
N_META, SEQ, D = 16, 2048, 2048
T = N_META + SEQ
TR = 688
EPS = 1e-6
D_SSM, Q_LORA, KV_LORA, QK_ROPE = 1024, 512, 256, 64
D_IN = D_SSM + Q_LORA + KV_LORA + QK_ROPE
D_IN_PAD = 1920
N_HEADS, QK_NOPE, V_HEAD = 8, 128, 128
HEAD_PAD = 256
SM_SCALE = 1.0 / math.sqrt(QK_NOPE + QK_ROPE)
CHUNK = 64
N_GROUPS, SSM_GROUP, SSM_STATE = 64, 16, 64
N_SEG = 8
SEG = T // N_SEG
GQ = 8
D_FF = 5504
FF_PIECE = 1376
ROPE_BASE = 10000.0
LR, B1, B2, ADAM_EPS, WD, STEP = 0.001, 0.9, 0.999, 1e-08, 0.01, 10
VMEM_CAP = 60 * 1024 * 1024
MESH = pl.DeviceIdType.MESH

WEIGHTS = ['meta_tokens', 'mix_norm', 'w_in', 'lam_re', 'lam_im', 'log_dt', 'b_re', 'b_im', 'c_re', 'c_im',
           'd_skip', 'w_glu', 'b_glu', 'q_a_norm', 'w_q_b', 'kv_a_norm', 'w_kv_b', 'out_norm_ssm',
           'out_norm_attn', 'w_out', 'ffn_norm', 'w_up', 'conv_w', 'conv_b', 'w_down', 'final_norm']
BIG = ['w_in', 'w_glu', 'w_q_b', 'w_kv_b', 'w_out', 'w_up', 'w_down']
SMALL = [n for n in WEIGHTS if n not in BIG and n not in ('meta_tokens', 'conv_w')]


def _nbytes(shape, dtype):
    return math.prod(shape) * jnp.dtype(dtype).itemsize


def _in_hbm(a):
    return pltpu.with_memory_space_constraint(a, pltpu.HBM) if a.size * a.dtype.itemsize >= (1 << 20) else a


def _params(sem, need):
    return pltpu.CompilerParams(dimension_semantics=("arbitrary",) * sem,
                                vmem_limit_bytes=int(min(VMEM_CAP, max(need, 16 * 1024 * 1024))))


_DIMS = {'nn': (((1,), (0,)), ((), ())), 'nt': (((1,), (1,)), ((), ())), 'tn': (((0,), (0,)), ((), ()))}


def _mm(name, a, b, mode, tm, tn, tk, out_dtype, stack=None, res=None):
    sa, sb, so = stack in ('a_out', 'ab_red'), stack in ('b_out', 'ab_red'), stack in ('a_out', 'b_out')
    nj = a.shape[0] if sa else (b.shape[0] if sb else 1)
    a2, b2 = a.shape[-2:], b.shape[-2:]
    if mode == 'nn':
        (m, k), (k2, n) = a2, b2
    elif mode == 'nt':
        (m, k), (n, k2) = a2, b2
    else:
        (k, m), (k2, n) = a2, b2
    assert k == k2 and m % tm == 0 and n % tn == 0 and k % tk == 0, (name, a.shape, b.shape)
    n_jo, n_jr, n_k = (nj if so else 1), (nj if stack == 'ab_red' else 1), k // tk
    grid = (n_jo, m // tm, n // tn, n_jr, n_k)
    a_blk, a_idx = ((tk, tm), lambda i, j, kk: (kk, i)) if mode == 'tn' else ((tm, tk), lambda i, j, kk: (i, kk))
    b_blk, b_idx = ((tn, tk), lambda i, j, kk: (j, kk)) if mode == 'nt' else ((tk, tn), lambda i, j, kk: (kk, j))

    def spec(blk, idx, stacked, on_out):
        if not stacked:
            return pl.BlockSpec(blk, lambda jo, i, j, jr, kk: idx(i, j, kk))
        return pl.BlockSpec((None,) + blk, lambda jo, i, j, jr, kk: ((jo if on_out else jr),) + idx(i, j, kk))

    o_idx = lambda i, j, kk: (i, j)
    in_specs = [spec(a_blk, a_idx, sa, stack == 'a_out'), spec(b_blk, b_idx, sb, stack == 'b_out')]
    operands = [a, b]
    if res is not None:
        in_specs.append(spec((tm, tn), o_idx, False, False))
        operands.append(res)
    n_red = n_jr * n_k
    dims = _DIMS[mode]

    def body(*refs):
        a_ref, b_ref = refs[0], refs[1]
        res_ref = refs[2] if res is not None else None
        o_ref = refs[3] if res is not None else refs[2]
        part = lax.dot_general(a_ref[...].astype(BF16), b_ref[...].astype(BF16), dims,
                               preferred_element_type=F32)

        def finish(total):
            if res_ref is not None:
                total = total + res_ref[...]
            o_ref[...] = total.astype(o_ref.dtype)

        if n_red == 1:
            finish(part)
        else:
            acc_ref = refs[-1]
            step = pl.program_id(3) * n_k + pl.program_id(4)

            @pl.when(step == 0)
            def _():
                acc_ref[...] = part

            @pl.when(step > 0)
            def _():
                acc_ref[...] += part

            @pl.when(step == n_red - 1)
            def _():
                finish(acc_ref[...])

    out_shape = ((nj,) if so else ()) + (m, n)
    need = 2 * (_nbytes(a_blk, a.dtype) + _nbytes(b_blk, b.dtype) + _nbytes((tm, tn), out_dtype)
                + (_nbytes((tm, tn), F32) if res is not None else 0)) + 3 * _nbytes((tm, tn), F32) + (4 << 20)
    return pl.pallas_call(
        body, name=name, grid=grid, in_specs=in_specs, out_specs=spec((tm, tn), o_idx, so, True),
        out_shape=jax.ShapeDtypeStruct(out_shape, out_dtype),
        scratch_shapes=[pltpu.VMEM((tm, tn), F32)] if n_red > 1 else [],
        compiler_params=_params(5, need),
    )(*map(_in_hbm, operands))


def _rowwise(name, fn, row_ins, const_ins, row_outs, acc_outs, tr=TR):
    n_ri, n_ci, n_ro, n_ao = len(row_ins), len(const_ins), len(row_outs), len(acc_outs)

    def body(*refs):
        ri, ci = refs[:n_ri], refs[n_ri:n_ri + n_ci]
        ro, ao = refs[n_ri + n_ci:n_ri + n_ci + n_ro], refs[n_ri + n_ci + n_ro:]
        i = pl.program_id(0)
        outs = fn(i * tr, *[r[...] for r in ri], *[r[...] for r in ci])
        for r, v in zip(ro, outs[:n_ro]):
            r[...] = v.astype(r.dtype)
        if n_ao:
            @pl.when(i == 0)
            def _():
                for r, v in zip(ao, outs[n_ro:]):
                    r[...] = v

            @pl.when(i > 0)
            def _():
                for r, v in zip(ao, outs[n_ro:]):
                    r[...] += v

    in_specs = [pl.BlockSpec((tr, w), functools.partial(lambda cb, i: (i, cb), cb)) for _, w, cb in row_ins]
    in_specs += [pl.BlockSpec(c.shape, functools.partial(lambda nd, i: (0,) * nd, c.ndim)) for c in const_ins]
    out_specs = [pl.BlockSpec((tr, w), lambda i: (i, 0)) for w, _ in row_outs]
    out_specs += [pl.BlockSpec((1, w), lambda i: (0, 0)) for w in acc_outs]
    out_shape = [jax.ShapeDtypeStruct((T, w), dt) for w, dt in row_outs]
    out_shape += [jax.ShapeDtypeStruct((1, w), F32) for w in acc_outs]
    need = 2 * (sum(_nbytes((tr, w), a.dtype) for a, w, _ in row_ins) + sum(_nbytes((tr, w), dt) for w, dt in row_outs))
    need = 3 * need + (8 << 20)
    return pl.pallas_call(
        body, name=name, grid=(T // tr,), in_specs=in_specs, out_specs=out_specs, out_shape=out_shape,
        compiler_params=_params(1, need),
    )(*[_in_hbm(a) for a, _, _ in row_ins], *const_ins)


def _rms_fwd_fn(row0, h, g):
    h = h.astype(F32)
    r = lax.rsqrt(jnp.mean(h * h, axis=-1, keepdims=True) + EPS)
    return (h * r) * g, r


def _rms_bwd(dxn, h, r, g):
    xhat = h * r
    dxg = dxn * g
    dh = r * (dxg - xhat * jnp.mean(dxg * xhat, axis=-1, keepdims=True))
    return dh, jnp.sum(dxn * xhat, axis=0, keepdims=True)


def _rms_bwd_fn(row0, dxn, h, r, g):
    return _rms_bwd(dxn.astype(F32), h.astype(F32), r, g)


def _rms_bwd_res_fn(row0, dxn, h, r, res, g):
    dh, dg = _rms_bwd(dxn.astype(F32), h.astype(F32), r, g)
    return dh + res, dg


def _gelu_fn(row0, y):
    return (jax.nn.gelu(y),)


def _glu_fwd_fn(row0, y, gl, b, gain):
    ya = jax.nn.gelu(y) * jax.nn.sigmoid(gl + b)
    return _rms_fwd_fn(row0, ya, gain)


def _glu_bwd_fn(row0, dyan, y, gl, ra, b, gain):
    g = jax.nn.gelu(y)
    s = jax.nn.sigmoid(gl + b)
    dya, dgain = _rms_bwd(dyan, g * s, ra, gain)
    dgl = dya * g * s * (1.0 - s)
    return dgl, dya * s, dgain, jnp.sum(dgl, axis=0, keepdims=True)


def _gelu_bwd_fn(row0, dg, y):
    return (jax.vjp(jax.nn.gelu, y)[1](dg)[0],)


def _loss_fn(row0, h, tgt, gain):
    r = lax.rsqrt(jnp.mean(h * h, axis=-1, keepdims=True) + EPS)
    xhat = h * r
    rows = row0 + lax.broadcasted_iota(jnp.int32, h.shape, 0)
    err = jnp.where(rows >= N_META, xhat * gain - tgt, 0.0)
    loss = jnp.full((1, 128), jnp.sum(err * err) * (0.5 / D), F32)
    dh, dgain = _rms_bwd(err * (1.0 / D), h, r, gain)
    return dh, dgain, loss


def _rope_tables():
    pos = jnp.arange(T, dtype=jnp.int32)
    inv_freq = 1.0 / (ROPE_BASE ** (jnp.arange(0, QK_ROPE, 2, dtype=F32) / QK_ROPE))
    ang = pos.astype(F32)[:, None] * inv_freq[None, :]
    cos, sin, z32, z64 = jnp.cos(ang), jnp.sin(ang), jnp.zeros((T, 32), F32), jnp.zeros((T, 64), F32)
    return (jnp.concatenate([cos, cos, z64], 1), jnp.concatenate([-sin, z32, z64], 1),
            jnp.concatenate([z32, sin, z64], 1))


def _rot(x, cc, s1, s2):
    return x * cc + pltpu.roll(x, 96, 1) * s1 + pltpu.roll(x, 32, 1) * s2


def _derot(d, cc, s1, s2):
    return d * cc + pltpu.roll(d * s1, 32, 1) + pltpu.roll(d * s2, 96, 1)


def _chunk_of(pos):
    return jnp.where(pos < N_META, 0, 1 + ((pos - N_META) >> 6))


def _key_len(t):
    last_chunk = 1 + ((t + 1) * TR - 1 - N_META) // CHUNK
    return min(T, pl.cdiv(N_META + last_chunk * CHUNK, 128) * 128)


def _attn_probs(t, q_ref, kv_ref, kpe_ref, tq, tk):
    kl = _key_len(t)
    q = q_ref[...]
    qn = q[:, :QK_NOPE].astype(BF16)
    qp = _rot(q[:, QK_NOPE:], tq[0][...], tq[1][...], tq[2][...]).astype(BF16)
    kn = kv_ref[:kl, :QK_NOPE].astype(BF16)
    v = kv_ref[:kl, QK_NOPE:].astype(BF16)
    kp = _rot(kpe_ref[:kl, :], tk[0][:kl, :], tk[1][:kl, :], tk[2][:kl, :]).astype(BF16)
    s = lax.dot_general(qn, kn, _DIMS['nt'], preferred_element_type=F32)
    s = s + lax.dot_general(qp, kp, _DIMS['nt'], preferred_element_type=F32)
    qc = _chunk_of(t * TR + lax.broadcasted_iota(jnp.int32, (TR, 1), 0))
    kc = _chunk_of(lax.broadcasted_iota(jnp.int32, (1, kl), 1))
    s = jnp.where(kc <= qc, s * SM_SCALE, jnp.finfo(F32).min)
    p = jnp.exp(s - jnp.max(s, axis=-1, keepdims=True))
    p = p / jnp.sum(p, axis=-1, keepdims=True)
    return qn, qp, kn, kp, v, p


def _per_q_tile(fn):
    for t in range(T // TR):
        pl.when(pl.program_id(1) == t)(functools.partial(fn, t))


def _attn_specs(z, tabs):
    q_spec = pl.BlockSpec((None, TR, HEAD_PAD), lambda h, i: (h, i, 0))
    kv_spec = pl.BlockSpec((None, T, HEAD_PAD), lambda h, i: (h, 0, 0))
    kpe_spec = pl.BlockSpec((T, 128), lambda h, i: (0, (D_IN - QK_ROPE) // 128))
    tq = [pl.BlockSpec((TR, 128), lambda h, i: (i, 0))] * 3
    tk = [pl.BlockSpec((T, 128), lambda h, i: (0, 0))] * 3
    return [q_spec, kv_spec, kpe_spec] + tq + tk


def _attn_fwd(q, kv, z, tabs):
    def body(q_ref, kv_ref, kpe_ref, c1, c2, c3, k1, k2, k3, o_ref):
        def tile(t):
            _, _, _, _, v, p = _attn_probs(t, q_ref, kv_ref, kpe_ref, (c1, c2, c3), (k1, k2, k3))
            o_ref[...] = jnp.dot(p.astype(BF16), v, preferred_element_type=F32)

        _per_q_tile(tile)

    return pl.pallas_call(
        body, name="attn_fwd", grid=(N_HEADS, T // TR), in_specs=_attn_specs(z, tabs),
        out_specs=pl.BlockSpec((TR, V_HEAD), lambda h, i: (i, h)),
        out_shape=jax.ShapeDtypeStruct((T, N_HEADS * V_HEAD), F32),
        compiler_params=_params(2, 48 << 20),
    )(_in_hbm(q), _in_hbm(kv), _in_hbm(z), *tabs, *tabs)


def _attn_bwd(q, kv, z, tabs, dyb):
    def body(q_ref, kv_ref, kpe_ref, c1, c2, c3, k1, k2, k3, do_ref, dq_ref, dkv_ref, dkpe_ref):
        @pl.when(pl.program_id(1) == 0)
        def _():
            dkv_ref[...] = jnp.zeros(dkv_ref.shape, F32)
            dkpe_ref[...] = jnp.zeros(dkpe_ref.shape, F32)

        def tile(t):
            kl = _key_len(t)
            qn, qp, kn, kp, v, p = _attn_probs(t, q_ref, kv_ref, kpe_ref, (c1, c2, c3), (k1, k2, k3))
            do = do_ref[...].astype(BF16)
            dp = lax.dot_general(do, v, _DIMS['nt'], preferred_element_type=F32)
            ds = (p * (dp - jnp.sum(p * dp, axis=-1, keepdims=True)) * SM_SCALE).astype(BF16)
            pb = p.astype(BF16)
            dq_ref[:, :QK_NOPE] = jnp.dot(ds, kn, preferred_element_type=F32).astype(dq_ref.dtype)
            dqp = _derot(jnp.dot(ds, kp, preferred_element_type=F32), c1[...], c2[...], c3[...])
            dq_ref[:, QK_NOPE:] = dqp.astype(dq_ref.dtype)
            dkv_ref[:kl, :QK_NOPE] += lax.dot_general(ds, qn, _DIMS['tn'], preferred_element_type=F32)
            dkv_ref[:kl, QK_NOPE:] += lax.dot_general(pb, do, _DIMS['tn'], preferred_element_type=F32)
            dkpe_ref[:kl, :] += lax.dot_general(ds, qp, _DIMS['tn'], preferred_element_type=F32)

        _per_q_tile(tile)

    return pl.pallas_call(
        body, name="attn_bwd", grid=(N_HEADS, T // TR),
        in_specs=_attn_specs(z, tabs) + [pl.BlockSpec((TR, V_HEAD), lambda h, i: (i, h))],
        out_specs=[pl.BlockSpec((None, TR, HEAD_PAD), lambda h, i: (h, i, 0)),
                   pl.BlockSpec((None, T, HEAD_PAD), lambda h, i: (h, 0, 0)),
                   pl.BlockSpec((None, T, 128), lambda h, i: (h, 0, 0))],
        out_shape=[jax.ShapeDtypeStruct((N_HEADS, T, HEAD_PAD), BF16),
                   jax.ShapeDtypeStruct((N_HEADS, T, HEAD_PAD), F32),
                   jax.ShapeDtypeStruct((N_HEADS, T, 128), F32)],
        compiler_params=_params(2, 56 << 20),
    )(_in_hbm(q), _in_hbm(kv), _in_hbm(z), *tabs, *tabs, _in_hbm(dyb))


def _kpe_bwd(dkpe, tabs):
    def body(d_ref, c1, c2, c3, o_ref):
        d = d_ref[0]
        for h in range(1, N_HEADS):
            d = d + d_ref[h]
        o_ref[...] = _derot(d, c1[...], c2[...], c3[...]).astype(o_ref.dtype)

    tile = pl.BlockSpec((TR, 128), lambda i: (i, 0))
    return pl.pallas_call(
        body, name="kpe_bwd", grid=(T // TR,),
        in_specs=[pl.BlockSpec((N_HEADS, TR, 128), lambda i: (0, i, 0)), tile, tile, tile], out_specs=tile,
        out_shape=jax.ShapeDtypeStruct((T, 128), BF16), compiler_params=_params(1, 16 << 20),
    )(_in_hbm(dkpe), *tabs)


def _s5_prep_math(lam_re, lam_im, log_dt, bt_re, bt_im):
    dt = jnp.exp(log_dt)
    mag = jnp.exp(lam_re * dt)
    a_re, a_im = mag * jnp.cos(lam_im * dt), mag * jnp.sin(lam_im * dt)
    den = lam_re * lam_re + lam_im * lam_im
    w_re = ((a_re - 1.0) * lam_re + a_im * lam_im) / den
    w_im = (a_im * lam_re - (a_re - 1.0) * lam_im) / den
    return a_re, a_im, w_re[None] * bt_re - w_im[None] * bt_im, w_re[None] * bt_im + w_im[None] * bt_re


def _eye_groups():
    return (lax.broadcasted_iota(jnp.int32, (N_GROUPS, N_GROUPS), 0)
            == lax.broadcasted_iota(jnp.int32, (N_GROUPS, N_GROUPS), 1)).astype(F32)


def _row_to_col(row):
    return jnp.sum(_eye_groups() * row, axis=1, keepdims=True)


def _s5_prep(lam_re, lam_im, log_dt, bt_re, bt_im):
    def body(lr, li, ld, br, bi, ar, ai, bbr, bbi):
        ar[...], ai[...], bbr[...], bbi[...] = _s5_prep_math(lr[...], li[...], _row_to_col(ld[...]), br[...], bi[...])

    gp, cgp = jax.ShapeDtypeStruct((N_GROUPS, SSM_STATE), F32), jax.ShapeDtypeStruct(bt_re.shape, F32)
    return pl.pallas_call(body, name="s5_prep", out_shape=[gp, gp, cgp, cgp])(lam_re, lam_im, log_dt, bt_re, bt_im)


def _s5_prep_bwd(lam_re, lam_im, log_dt, bt_re, bt_im, da_re, da_im, dbb_re, dbb_im):
    def body(lr, li, ld, br, bi, dar, dai, dbr, dbi, o1, o2, o3, o4, o5):
        _, vjp = jax.vjp(_s5_prep_math, lr[...], li[...], _row_to_col(ld[...]), br[...], bi[...])
        o1[...], o2[...], dld, o4[...], o5[...] = vjp((dar[...], dai[...], dbr[...], dbi[...]))
        o3[...] = jnp.sum(_eye_groups() * dld, axis=0, keepdims=True)

    ins = (lam_re, lam_im, log_dt, bt_re, bt_im)
    return pl.pallas_call(body, name="s5_prep_bwd", out_shape=[jax.ShapeDtypeStruct(a.shape, F32) for a in ins])(
        *ins, da_re, da_im, dbb_re, dbb_im)


def _cmul(ar, ai, br, bi):
    return ar * br - ai * bi, ar * bi + ai * br


def _seg_rows(i):
    return pl.ds(pl.multiple_of(i * N_SEG, N_SEG), N_SEG)


def _scan(xr, xi, ar, ai, reverse):
    zero = jnp.zeros(ar.shape, F32)

    def local(j, carry):
        rows = _seg_rows(SEG - 1 - j if reverse else j)
        nr, ni = _cmul(ar, ai, *carry)
        nr, ni = nr + xr[rows, :], ni + xi[rows, :]
        xr[rows, :], xi[rows, :] = nr, ni
        return nr, ni

    er, ei = lax.fori_loop(0, SEG, local, (zero, zero))
    pr, pi = ar, ai
    for _ in range(8):
        pr, pi = _cmul(pr, pi, pr, pi)
    pr, pi = _cmul(*_cmul(pr, pi, ar, ai), ar, ai)
    row = lax.broadcasted_iota(jnp.int32, ar.shape, 0)
    edge, shift = (N_SEG - 1, N_SEG - 1) if reverse else (0, 1)
    hr, hi = zero, zero
    for _ in range(N_SEG - 1):
        tr_, ti_ = _cmul(pr, pi, hr, hi)
        hr = jnp.where(row == edge, 0.0, pltpu.roll(tr_ + er, shift, 0))
        hi = jnp.where(row == edge, 0.0, pltpu.roll(ti_ + ei, shift, 0))

    def fix(j, carry):
        rows = _seg_rows(SEG - 1 - j if reverse else j)
        xr[rows, :] += carry[0]
        xi[rows, :] += carry[1]
        return _cmul(ar, ai, *carry)

    lax.fori_loop(0, SEG, fix, _cmul(ar, ai, hr, hi))
    return hr, hi


def _split(x):
    hi = x.astype(BF16)
    return hi, (x - hi.astype(F32)).astype(BF16)


def _dot3(xs, ys, mode='nn'):
    d = lambda p, q: lax.dot_general(p, q, _DIMS[mode], preferred_element_type=F32)
    return d(xs[0], ys[0]) + (d(xs[1], ys[0]) + d(xs[0], ys[1]))


def _s5_hidden(us, a_ref, bre_ref, bim_ref, hre, him):
    hre[...] = _dot3(us, _split(bre_ref[...]))
    him[...] = _dot3(us, _split(bim_ref[...]))
    return _scan(hre, him, a_ref[0], a_ref[1], reverse=False)


def _s5_specs():
    col = pl.BlockSpec((T, 128), lambda q: (0, q))
    a_spec = pl.BlockSpec((None, 2, N_SEG, GQ * SSM_STATE), lambda q: (q, 0, 0, 0))
    wide = pl.BlockSpec((None, 128, GQ * SSM_STATE), lambda q: (q, 0, 0))
    tall = pl.BlockSpec((None, GQ * SSM_STATE, 128), lambda q: (q, 0, 0))
    d_spec = pl.BlockSpec((1, 128), lambda q: (0, q))
    return col, a_spec, wide, tall, d_spec


def _s5_fwd(u, a8, bre, bim, cre, cim, dsk):
    def body(u_ref, a_ref, bre_ref, bim_ref, cre_ref, cim_ref, d_ref, y_ref, hre, him):
        u_ = u_ref[...]
        _s5_hidden(_split(u_), a_ref, bre_ref, bim_ref, hre, him)
        y_ref[...] = (_dot3(_split(hre[...]), _split(cre_ref[...]))
                      - _dot3(_split(him[...]), _split(cim_ref[...])) + d_ref[...] * u_)

    col, a_spec, wide, tall, d_spec = _s5_specs()
    return pl.pallas_call(
        body, name="s5_fwd", grid=(N_GROUPS // GQ,), in_specs=[col, a_spec, wide, wide, tall, tall, d_spec],
        out_specs=col, out_shape=jax.ShapeDtypeStruct((T, D_SSM), F32),
        scratch_shapes=[pltpu.VMEM((T, GQ * SSM_STATE), F32)] * 2,
        compiler_params=_params(1, 40 << 20),
    )(_in_hbm(u), a8, bre, bim, cre, cim, dsk)


def _s5_bwd(u, dy, a8, bre, bim, cre, cim, dsk):
    def body(u_ref, dy_ref, a_ref, bre_ref, bim_ref, cre_ref, cim_ref, d_ref,
             du_ref, dbre_ref, dbim_ref, dcre_ref, dcim_ref, da_ref, dd_ref, hre, him, lre, lim):
        u_, dy_ = u_ref[...], dy_ref[...]
        us, dys = _split(u_), _split(dy_)
        ar, ai = a_ref[0], a_ref[1]
        bres, bims = _split(bre_ref[...]), _split(bim_ref[...])
        hre[...] = _dot3(us, bres)
        him[...] = _dot3(us, bims)
        h0r, h0i = _scan(hre, him, ar, ai, reverse=False)
        lre[...] = _dot3(dys, _split(cre_ref[...]), 'nt')
        lim[...] = -_dot3(dys, _split(cim_ref[...]), 'nt')
        _scan(lre, lim, ar, -ai, reverse=True)

        def acc_da(i, carry):
            lr, li = lre[_seg_rows(i), :], lim[_seg_rows(i), :]
            pr, pi = hre[_seg_rows(i - 1), :], him[_seg_rows(i - 1), :]
            return carry[0] + lr * pr + li * pi, carry[1] + li * pr - lr * pi

        lr, li = lre[_seg_rows(0), :], lim[_seg_rows(0), :]
        dar, dai = lax.fori_loop(1, SEG, acc_da, (lr * h0r + li * h0i, li * h0r - lr * h0i))
        da_ref[0:1, :] = jnp.sum(dar, axis=0, keepdims=True)
        da_ref[1:2, :] = jnp.sum(dai, axis=0, keepdims=True)
        lrs, lis = _split(lre[...]), _split(lim[...])
        du_ref[...] = dy_ * d_ref[...] + _dot3(lrs, bres, 'nt') + _dot3(lis, bims, 'nt')
        dbre_ref[...] = _dot3(us, lrs, 'tn')
        dbim_ref[...] = _dot3(us, lis, 'tn')
        dcre_ref[...] = _dot3(_split(hre[...]), dys, 'tn')
        dcim_ref[...] = -_dot3(_split(him[...]), dys, 'tn')
        dd_ref[...] = jnp.sum(dy_ * u_, axis=0, keepdims=True)

    col, a_spec, wide, tall, d_spec = _s5_specs()
    nq, ns = N_GROUPS // GQ, GQ * SSM_STATE
    return pl.pallas_call(
        body, name="s5_bwd", grid=(nq,),
        in_specs=[col, col, a_spec, wide, wide, tall, tall, d_spec],
        out_specs=[col, wide, wide, tall, tall, pl.BlockSpec((None, 2, ns), lambda q: (q, 0, 0)), d_spec],
        out_shape=[jax.ShapeDtypeStruct((T, D_SSM), F32),
                   jax.ShapeDtypeStruct((nq, 128, ns), F32), jax.ShapeDtypeStruct((nq, 128, ns), F32),
                   jax.ShapeDtypeStruct((nq, ns, 128), F32), jax.ShapeDtypeStruct((nq, ns, 128), F32),
                   jax.ShapeDtypeStruct((nq, 2, ns), F32), jax.ShapeDtypeStruct((1, D_SSM), F32)],
        scratch_shapes=[pltpu.VMEM((T, ns), F32)] * 4,
        compiler_params=_params(1, 56 << 20),
    )(_in_hbm(u), _in_hbm(dy), a8, bre, bim, cre, cim, dsk)


def _to_segments(a):
    return a.reshape(N_SEG, SEG, a.shape[-1]).transpose(1, 0, 2).reshape(T, a.shape[-1])


def _from_segments(a):
    return a.reshape(SEG, N_SEG, a.shape[-1]).transpose(1, 0, 2).reshape(T, a.shape[-1])


def _block_diag_b(bbt):
    nq = N_GROUPS // GQ
    x = bbt.transpose(1, 0, 2).reshape(nq, GQ, SSM_GROUP, 1, SSM_STATE)
    eye = jnp.eye(GQ, dtype=F32)[None, :, None, :, None]
    return (x * eye).reshape(nq, GQ * SSM_GROUP, GQ * SSM_STATE)


def _block_diag_c(c):
    nq = N_GROUPS // GQ
    x = c.reshape(nq, GQ, SSM_GROUP, SSM_STATE).transpose(0, 1, 3, 2)[:, :, :, None, :]
    eye = jnp.eye(GQ, dtype=F32)[None, :, None, :, None]
    return (x * eye).reshape(nq, GQ * SSM_STATE, GQ * SSM_GROUP)


def _ff_specs():
    up_spec = pl.BlockSpec((2, None, T, 128), lambda p, cb: (0, p, 0, cb))
    piece = pl.BlockSpec((None, T, 128), lambda p, cb: (p, 0, cb))
    cw_spec = pl.BlockSpec((None, 3, 128), lambda p, cb: (p, 0, cb))
    cb_spec = pl.BlockSpec((None, 1, 128), lambda p, cb: (p, 0, cb))
    return up_spec, piece, cw_spec, cb_spec


def _conv_gate(gate, w, cb, pad_ref):
    pad_ref[0:8, :] = jnp.zeros((8, 128), F32)
    pad_ref[8:8 + T, :] = gate
    g1, g2 = pad_ref[pl.ds(7, T), :], pad_ref[pl.ds(6, T), :]
    return w[0:1, :] * g2 + w[1:2, :] * g1 + w[2:3, :] * gate + cb, g1, g2


def _ff_act(up, cw4, cb4):
    def body(up_ref, w_ref, b_ref, o_ref, pad_ref):
        gc, _, _ = _conv_gate(up_ref[0].astype(F32), w_ref[...], b_ref[...], pad_ref)
        o_ref[...] = (jax.nn.silu(gc) * up_ref[1].astype(F32)).astype(o_ref.dtype)

    up_spec, piece, cw_spec, cb_spec = _ff_specs()
    n_cb = pl.cdiv(FF_PIECE, 128)
    return pl.pallas_call(
        body, name="ff_act", grid=(4, n_cb), in_specs=[up_spec, cw_spec, cb_spec], out_specs=piece,
        out_shape=jax.ShapeDtypeStruct((4, T, FF_PIECE), BF16),
        scratch_shapes=[pltpu.VMEM((T + 8, 128), F32)],
        compiler_params=_params(2, 24 << 20),
    )(_in_hbm(up.reshape(2, 4, T, FF_PIECE)), cw4, cb4)


def _ff_act_bwd(up, dact, cw4, cb4):
    def body(up_ref, da_ref, w_ref, b_ref, dup_ref, dw_ref, db_ref, pad_ref, pad2_ref):
        gate, val, w = up_ref[0].astype(F32), up_ref[1].astype(F32), w_ref[...]
        gc, g1, g2 = _conv_gate(gate, w, b_ref[...], pad_ref)
        sg = jax.nn.sigmoid(gc)
        da = da_ref[...].astype(F32)
        dup_ref[1] = (da * gc * sg).astype(dup_ref.dtype)
        dgc = da * val * sg * (1.0 + gc * (1.0 - sg))
        db_ref[...] = jnp.sum(dgc, axis=0, keepdims=True)
        dw_ref[0:1, :] = jnp.sum(dgc * g2, axis=0, keepdims=True)
        dw_ref[1:2, :] = jnp.sum(dgc * g1, axis=0, keepdims=True)
        dw_ref[2:3, :] = jnp.sum(dgc * gate, axis=0, keepdims=True)
        pad2_ref[0:T, :] = dgc
        pad2_ref[T:T + 8, :] = jnp.zeros((8, 128), F32)
        d1, d2 = pad2_ref[pl.ds(1, T), :], pad2_ref[pl.ds(2, T), :]
        dup_ref[0] = (w[2:3, :] * dgc + w[1:2, :] * d1 + w[0:1, :] * d2).astype(dup_ref.dtype)

    up_spec, piece, cw_spec, cb_spec = _ff_specs()
    n_cb = pl.cdiv(FF_PIECE, 128)
    dup, dw, db = pl.pallas_call(
        body, name="ff_act_bwd", grid=(4, n_cb), in_specs=[up_spec, piece, cw_spec, cb_spec],
        out_specs=[up_spec, cw_spec, cb_spec],
        out_shape=[jax.ShapeDtypeStruct((2, 4, T, FF_PIECE), BF16), jax.ShapeDtypeStruct((4, 3, FF_PIECE), F32),
                   jax.ShapeDtypeStruct((4, 1, FF_PIECE), F32)],
        scratch_shapes=[pltpu.VMEM((T + 8, 128), F32)] * 2,
        compiler_params=_params(2, 32 << 20),
    )(_in_hbm(up.reshape(2, 4, T, FF_PIECE)), _in_hbm(dact), cw4, cb4)
    return dup.reshape(8, T, FF_PIECE), dw, db


def _local_step(x, tgt, meta, p, wf, late_weights, send_grads):
    sm = {}
    h0 = jnp.concatenate([meta, x], axis=0)
    tgt_pad = jnp.concatenate([jnp.zeros((N_META, D), F32), tgt], axis=0)
    tabs = _rope_tables()
    row = lambda a: a.reshape(1, -1)

    xn, r0 = _rowwise("rms_mix", _rms_fwd_fn, [(h0, D, 0)], [row(p['mix_norm'])], [(D, BF16), (1, F32)], [])
    z = _mm("in_proj", xn, wf['w_in'], 'nn', TR, 640, D, F32)
    lam_re, lam_im = p['lam_re'].reshape(N_GROUPS, SSM_STATE), p['lam_im'].reshape(N_GROUPS, SSM_STATE)
    log_dt = p['log_dt'].reshape(1, N_GROUPS)
    bt_re = p['b_re'].reshape(N_GROUPS, SSM_STATE, SSM_GROUP).transpose(2, 0, 1)
    bt_im = p['b_im'].reshape(N_GROUPS, SSM_STATE, SSM_GROUP).transpose(2, 0, 1)
    c_re = p['c_re'].reshape(N_GROUPS, SSM_GROUP, SSM_STATE)
    c_im = p['c_im'].reshape(N_GROUPS, SSM_GROUP, SSM_STATE)
    a_re, a_im, bbt_re, bbt_im = _s5_prep(lam_re, lam_im, log_dt, bt_re, bt_im)
    nq, ns = N_GROUPS // GQ, GQ * SSM_STATE
    a8 = jnp.broadcast_to(jnp.stack([a_re.reshape(nq, ns), a_im.reshape(nq, ns)], 1)[:, :, None, :],
                          (nq, 2, N_SEG, ns))
    bre, bim = _block_diag_b(bbt_re), _block_diag_b(bbt_im)
    cre, cim = _block_diag_c(c_re), _block_diag_c(c_im)
    dsk = row(p['d_skip'])
    u_seg = _to_segments(z[:, :D_SSM])
    y_ssm = _from_segments(_s5_fwd(u_seg, a8, bre, bim, cre, cim, dsk))
    g, = _rowwise("gelu", _gelu_fn, [(y_ssm, D_SSM, 0)], [], [(D_SSM, BF16)], [])
    gl = _mm("glu_proj", g, wf['w_glu'], 'nn', TR, D_SSM, D_SSM, F32)
    yan, ra = _rowwise("glu_norm", _glu_fwd_fn, [(y_ssm, D_SSM, 0), (gl, D_SSM, 0)],
                       [row(p['b_glu']), row(p['out_norm_ssm'])], [(D_SSM, BF16), (1, F32)], [])
    qn, rq = _rowwise("rms_q", _rms_fwd_fn, [(z, Q_LORA, D_SSM // Q_LORA)], [row(p['q_a_norm'])],
                      [(Q_LORA, BF16), (1, F32)], [])
    kvn, rkv = _rowwise("rms_kv", _rms_fwd_fn, [(z, KV_LORA, (D_SSM + Q_LORA) // KV_LORA)], [row(p['kv_a_norm'])],
                        [(KV_LORA, BF16), (1, F32)], [])
    q = _mm("q_proj", qn, wf['w_q_b'], 'nn', T, HEAD_PAD, Q_LORA, F32, stack='b_out')
    kv = _mm("kv_proj", kvn, wf['w_kv_b'], 'nn', T, HEAD_PAD, KV_LORA, F32, stack='b_out')
    yb = _attn_fwd(q, kv, z, tabs)
    ybn, rb = _rowwise("rms_attn", _rms_fwd_fn, [(yb, D_SSM, 0)], [row(p['out_norm_attn'])],
                       [(D_SSM, BF16), (1, F32)], [])
    y = jnp.concatenate([yan, ybn], axis=1)
    wf = dict(wf)
    wf['w_out'], wf['w_up'], wf['w_down'] = late_weights(y)
    h1 = _mm("out_proj", y, wf['w_out'], 'nn', TR, 1024, D, F32, res=h0)
    xn2, r1 = _rowwise("rms_ffn", _rms_fwd_fn, [(h1, D, 0)], [row(p['ffn_norm'])], [(D, BF16), (1, F32)], [])
    up = _mm("up_proj", xn2, wf['w_up'], 'nn', TR, FF_PIECE, D, BF16, stack='b_out')
    cw4 = wf['conv_w'].reshape(3, 4, FF_PIECE).transpose(1, 0, 2)
    cb4 = p['conv_b'].reshape(4, 1, FF_PIECE)
    act = _ff_act(up, cw4, cb4)
    wd4 = wf['w_down'].reshape(4, FF_PIECE, D)
    h2 = _mm("down_proj", act, wd4, 'nn', TR, 1024, FF_PIECE, F32, stack='ab_red', res=h1)
    dh2, sm['final_norm'], loss = _rowwise("loss", _loss_fn, [(h2, D, 0), (tgt_pad, D, 0)], [row(p['final_norm'])],
                                           [(D, F32)], [D, 128])

    big = {}
    dact = _mm("down_bwd_x", dh2, wd4, 'nt', TR, FF_PIECE, D, BF16, stack='b_out')
    g_down = _mm("down_bwd_w", act, dh2, 'tn', FF_PIECE, 512, T, BF16, stack='a_out').reshape(
        N_DEV, D_FF // N_DEV, D)
    dup, dcw4, sm['conv_b'] = _ff_act_bwd(up, dact, cw4, cb4)
    dxn2 = _mm("up_bwd_x", dup, wf['w_up'], 'nt', TR, 1024, FF_PIECE, F32, stack='ab_red')
    g_up = _mm("up_bwd_w", dup, xn2, 'tn', FF_PIECE, 1024, T, BF16, stack='a_out')
    token = send_grads(['w_down', 'w_up'], [g_down, g_up])
    dh1, sm['ffn_norm'] = _rowwise("rms_ffn_bwd", _rms_bwd_res_fn,
                                   [(dxn2, D, 0), (h1, D, 0), (r1, 1, 0), (dh2, D, 0)],
                                   [row(p['ffn_norm']) + token[0:1, 0:1]], [(D, F32)], [D])
    dy = _mm("out_bwd_x", dh1, wf['w_out'], 'nt', TR, 1024, D, F32)
    g_out = _mm("out_bwd_w", y, dh1, 'tn', 1024, 1024, T, BF16).reshape(N_DEV, D // N_DEV, D)
    dgl, dgd, sm['out_norm_ssm'], sm['b_glu'] = _rowwise(
        "glu_bwd", _glu_bwd_fn, [(dy, D_SSM, 0), (y_ssm, D_SSM, 0), (gl, D_SSM, 0), (ra, 1, 0)],
        [row(p['b_glu']), row(p['out_norm_ssm'])], [(D_SSM, BF16), (D_SSM, F32)], [D_SSM, D_SSM])
    dg = _mm("glu_bwd_x", dgl, wf['w_glu'], 'nt', TR, D_SSM, D_SSM, F32, res=dgd)
    g_glu = _mm("glu_bwd_w", g, dgl, 'tn', D_SSM, D_SSM, T, BF16).reshape(N_DEV, D_SSM // N_DEV, D_SSM)
    dy_ssm, = _rowwise("gelu_bwd", _gelu_bwd_fn, [(dg, D_SSM, 0), (y_ssm, D_SSM, 0)], [], [(D_SSM, F32)], [])
    dyb, sm['out_norm_attn'] = _rowwise("rms_attn_bwd", _rms_bwd_fn, [(dy, D_SSM, 1), (yb, D_SSM, 0), (rb, 1, 0)],
                                        [row(p['out_norm_attn'])], [(D_SSM, F32)], [D_SSM])
    dq, dkv, dkpe = _attn_bwd(q, kv, z, tabs, dyb)
    g_q = _mm("q_bwd_w", qn, dq, 'tn', Q_LORA, HEAD_PAD, T, BF16, stack='b_out')
    dqn = _mm("q_bwd_x", dq, wf['w_q_b'], 'nt', TR, Q_LORA, HEAD_PAD, F32, stack='ab_red')
    g_kv = _mm("kv_bwd_w", kvn, dkv, 'tn', KV_LORA, HEAD_PAD, T, BF16, stack='b_out')
    dkvn = _mm("kv_bwd_x", dkv, wf['w_kv_b'], 'nt', TR, KV_LORA, HEAD_PAD, F32, stack='ab_red')
    token = send_grads(['w_out', 'w_glu', 'w_q_b', 'w_kv_b'], [g_out, g_glu, g_q, g_kv])
    dq_a, sm['q_a_norm'] = _rowwise("rms_q_bwd", _rms_bwd_fn,
                                    [(dqn, Q_LORA, 0), (z, Q_LORA, D_SSM // Q_LORA), (rq, 1, 0)],
                                    [row(p['q_a_norm'])], [(Q_LORA, BF16)], [Q_LORA])
    dkv_a, sm['kv_a_norm'] = _rowwise("rms_kv_bwd", _rms_bwd_fn,
                                      [(dkvn, KV_LORA, 0), (z, KV_LORA, (D_SSM + Q_LORA) // KV_LORA), (rkv, 1, 0)],
                                      [row(p['kv_a_norm'])], [(KV_LORA, BF16)], [KV_LORA])
    dk_pe = _kpe_bwd(dkpe, tabs)
    du_seg, dbre, dbim, dcre, dcim, da, sm['d_skip'] = _s5_bwd(
        u_seg, _to_segments(dy_ssm), a8, bre, bim, cre, cim, dsk + token[0:1, 0:1])
    du = _from_segments(du_seg).astype(BF16)
    diag_b = lambda d: jnp.einsum('qgcgp->qgcp', d.reshape(nq, GQ, SSM_GROUP, GQ, SSM_STATE)).transpose(
        2, 0, 1, 3).reshape(SSM_GROUP, N_GROUPS, SSM_STATE)
    diag_c = lambda d: jnp.einsum('qgpgc->qgcp', d.reshape(nq, GQ, SSM_STATE, GQ, SSM_GROUP)).reshape(
        N_GROUPS * SSM_GROUP, SSM_STATE)
    dlam_re, dlam_im, dlog_dt, dbt_re, dbt_im = _s5_prep_bwd(
        lam_re, lam_im, log_dt, bt_re, bt_im, da[:, 0, :].reshape(N_GROUPS, SSM_STATE),
        da[:, 1, :].reshape(N_GROUPS, SSM_STATE), diag_b(dbre), diag_b(dbim))
    sm['lam_re'], sm['lam_im'], sm['log_dt'] = dlam_re, dlam_im, dlog_dt
    cgp = lambda d: d.reshape(SSM_GROUP * N_GROUPS, SSM_STATE)
    sm['b_re'], sm['b_im'] = cgp(dbt_re), cgp(dbt_im)
    sm['c_re'], sm['c_im'] = diag_c(dcre), diag_c(dcim)
    dz = jnp.concatenate([du, dq_a, dkv_a, dk_pe], axis=1)
    dxn = _mm("in_bwd_x", dz, wf['w_in'], 'nt', TR, 1024, D_IN_PAD, F32)
    big['w_in'] = _mm("in_bwd_w", xn, dz, 'tn', 1024, 640, T, BF16).reshape(N_DEV, D // N_DEV, D_IN_PAD)
    dh0, sm['mix_norm'] = _rowwise("rms_mix_bwd", _rms_bwd_res_fn,
                                   [(dxn, D, 0), (h0, D, 0), (r0, 1, 0), (dh1, D, 0)], [row(p['mix_norm'])],
                                   [(D, F32)], [D])
    return loss, dh0[N_META:], dh0[:N_META], dcw4, big, sm


def _place():
    x, y, c = lax.axis_index("x"), lax.axis_index("y"), lax.axis_index("c")
    return x, y, c, [(1 - x, y), (x, 1 - y), (1 - x, 1 - y)]


_HBM = pl.BlockSpec(memory_space=pltpu.HBM)


def _all_gather(name, arrs, chip_blocks=()):
    n, nc = len(arrs), len(chip_blocks)

    def body(*refs):
        ins, cins, outs, couts = refs[:n], refs[n:n + nc], refs[n + nc:2 * n + nc], refs[2 * n + nc:2 * (n + nc)]
        send, recv, loc, csend, crecv = refs[2 * (n + nc):]
        x, y, c, chips = _place()
        me, sib = (x, y, c), (x, y, 1 - c)

        def cp(a, k, blk, to, src=None):
            dst = outs[a].at[4 * blk[0] + 2 * blk[1] + blk[2]]
            return pltpu.make_async_remote_copy(src_ref=dst if src is None else src, dst_ref=dst,
                                                send_sem=send.at[a, k], recv_sem=recv.at[a, k],
                                                device_id=to, device_id_type=MESH)

        between_chips = [pltpu.make_async_remote_copy(src_ref=cins[a].at[2 * ch[0] + ch[1]], dst_ref=couts[a].at[j],
                                                      send_sem=csend.at[a, j], recv_sem=crecv.at[a, j],
                                                      device_id=(*ch, c), device_id_type=MESH)
                         for a in range(nc) for j, ch in enumerate(chips)]
        mine = [pltpu.make_async_copy(ins[a], outs[a].at[4 * x + 2 * y + c], loc.at[a]) for a in range(n)]
        first = []
        for a in range(n):
            first.append(cp(a, 0, me, sib, src=ins[a]))
            first += [cp(a, 1 + j, me, (*ch, c), src=ins[a]) for j, ch in enumerate(chips)]
        for f in mine + first + between_chips:
            f.start()
        passed = []
        for a in range(n):
            for j, ch in enumerate(chips):
                cp(a, 1 + j, (*ch, c), me).wait_recv()
                passed.append(cp(a, 4 + j, (*ch, c), sib))
                passed[-1].start()
        for a in range(n):
            cp(a, 0, sib, me).wait_recv()
            for j, ch in enumerate(chips):
                cp(a, 4 + j, (*ch, 1 - c), me).wait_recv()
        for f in first + passed:
            f.wait_send()
        for m in mine + between_chips:
            m.wait()

    out = pl.pallas_call(
        body, name=name, in_specs=[_HBM] * (n + nc), out_specs=[_HBM] * (n + nc),
        out_shape=[jax.ShapeDtypeStruct((N_DEV,) + a.shape, a.dtype) for a in arrs]
        + [jax.ShapeDtypeStruct((3,) + a.shape[1:], a.dtype) for a in chip_blocks],
        scratch_shapes=[pltpu.SemaphoreType.DMA((n, 7)), pltpu.SemaphoreType.DMA((n, 7)),
                        pltpu.SemaphoreType.DMA((n,)), pltpu.SemaphoreType.DMA((max(nc, 1), 3)),
                        pltpu.SemaphoreType.DMA((max(nc, 1), 3))],
    )(*arrs, *chip_blocks)
    return (out[:n], out[n:]) if nc else out


def _exchange_cores(name, arrs):
    n = len(arrs)

    def body(*refs):
        ins, outs = refs[:n], refs[n:2 * n]
        send, recv = refs[2 * n:]
        x, y, c, _ = _place()
        remote = [pltpu.make_async_remote_copy(src_ref=ins[a].at[2 * k + 1 - c], dst_ref=outs[a].at[k],
                                               send_sem=send.at[a, k], recv_sem=recv.at[a, k],
                                               device_id=(x, y, 1 - c), device_id_type=MESH)
                  for a in range(n) for k in range(4)]
        for d in remote:
            d.start()
        for d in remote:
            d.wait()

    return pl.pallas_call(
        body, name=name, in_specs=[_HBM] * n, out_specs=[_HBM] * n,
        out_shape=[jax.ShapeDtypeStruct((4,) + a.shape[1:], a.dtype) for a in arrs],
        scratch_shapes=[pltpu.SemaphoreType.DMA((n, 4))] * 2,
    )(*arrs)


_SEM = pl.BlockSpec(memory_space=pltpu.SEMAPHORE)
_EFFECT = pltpu.SideEffectType.DATAFLOW_SIDE_EFFECTING


def _split_exchange(name, srcs, land_shapes, copies_of, n_cp, after):
    n = len(srcs)
    lands = [pltpu.with_memory_space_constraint(lax.empty(s, a.dtype), pltpu.HBM) for s, a in zip(land_shapes, srcs)]

    n_sem = n * n_cp

    def descriptors(src_refs, land_refs, send, recv, mirror):
        x, y, c, chips = _place()
        out = []
        for a in range(n):
            for k, (src, dst, to, back) in enumerate(copies_of(a, src_refs[a], land_refs[a], x, y, c, chips)):
                out.append(pltpu.make_async_remote_copy(src_ref=src, dst_ref=back if mirror else dst,
                                                        send_sem=send[a * n_cp + k], recv_sem=recv[a * n_cp + k],
                                                        device_id=to, device_id_type=MESH))
        return out

    def start_body(*refs):
        sems = refs[2 * n + 1:2 * n + 1 + 2 * n_sem]
        for d in descriptors(refs[:n], refs[n:2 * n], sems[:n_sem], sems[n_sem:], False):
            d.start()
        refs[-1][...] = jnp.zeros((8, 128), F32)

    hbm_like = lambda arrs: [pltpu.HBM(a.shape, a.dtype) for a in arrs]
    res = pl.pallas_call(
        start_body, name=name + "_start", in_specs=[_HBM] * (2 * n) + [pl.BlockSpec(memory_space=pl.ANY)],
        out_specs=[_SEM] * (2 * n_sem) + [_HBM] * (2 * n) + [pl.BlockSpec(memory_space=pltpu.VMEM)],
        out_shape=[pltpu.SemaphoreType.DMA(())] * (2 * n_sem)
        + hbm_like(srcs) + hbm_like(lands) + [jax.ShapeDtypeStruct((8, 128), F32)],
        input_output_aliases={i: 2 * n_sem + i for i in range(2 * n)},
        compiler_params=pltpu.CompilerParams(has_side_effects=_EFFECT),
    )(*[pltpu.with_memory_space_constraint(a, pltpu.HBM) for a in srcs], *lands, after)
    sems, thru, token = res[:2 * n_sem], res[2 * n_sem:2 * n_sem + 2 * n], res[-1]

    def wait(after):
        def wait_body(*refs):
            s = refs[2 * n:2 * n + 2 * n_sem]
            for d in descriptors(refs[:n], refs[n:2 * n], s[:n_sem], s[n_sem:], True):
                d.wait_send()
                d.wait_recv()

        out = pl.pallas_call(
            wait_body, name=name + "_wait",
            in_specs=[_HBM] * (2 * n) + [_SEM] * (2 * n_sem) + [pl.BlockSpec(memory_space=pl.ANY)],
            out_specs=[_HBM] * (2 * n), out_shape=hbm_like(srcs) + hbm_like(lands),
            input_output_aliases={i: i for i in range(2 * n)},
            compiler_params=pltpu.CompilerParams(has_side_effects=_EFFECT),
        )(*thru, *sems, after)
        return out[:n], out[n:]

    return token, wait


def _gather_copies(a, src, land, x, y, c, chips):
    peers = [(x, y, 1 - c)] + [(*ch, c) for ch in chips]
    me = 4 * x + 2 * y + c
    return [(src, land.at[me], to, land.at[4 * to[0] + 2 * to[1] + to[2]]) for to in peers]


def _chips_copies(a, src, land, x, y, c, chips):
    return [(src.at[2 * ch[0] + ch[1]], land.at[j], (*ch, c), land.at[j]) for j, ch in enumerate(chips)]


def _gather_forward(name, arrs):
    n = len(arrs)

    def body(*refs):
        ins, outs = refs[:n], refs[n:2 * n]
        send, recv = refs[2 * n:]
        x, y, c, chips = _place()
        sends, recvs = [], []
        for a in range(n):
            for j, ch in enumerate(chips):
                mine, theirs = 4 * ch[0] + 2 * ch[1] + c, 4 * ch[0] + 2 * ch[1] + 1 - c
                sends.append(pltpu.make_async_remote_copy(src_ref=ins[a].at[mine], dst_ref=outs[a].at[mine],
                                                          send_sem=send.at[a, j], recv_sem=recv.at[a, j],
                                                          device_id=(x, y, 1 - c), device_id_type=MESH))
                recvs.append(pltpu.make_async_remote_copy(src_ref=ins[a].at[theirs], dst_ref=outs[a].at[theirs],
                                                          send_sem=send.at[a, j], recv_sem=recv.at[a, j],
                                                          device_id=(x, y, 1 - c), device_id_type=MESH))
        for d in sends:
            d.start()
        for s, r in zip(sends, recvs):
            s.wait_send()
            r.wait_recv()

    return pl.pallas_call(
        body, name=name, in_specs=[_HBM] * n, out_specs=[_HBM] * n,
        out_shape=[jax.ShapeDtypeStruct(a.shape, a.dtype) for a in arrs],
        input_output_aliases={i: i for i in range(n)},
        scratch_shapes=[pltpu.SemaphoreType.DMA((n, 3))] * 2,
    )(*arrs)


def _blocks_of(r, c):
    if r * c * 4 <= (2 << 20):
        return r, c
    if r % 128 == 0:
        return 128, c
    return r, 256


def _pair_sum(name, own, got, core):
    _, r, c = got.shape
    rb, cb = _blocks_of(r, c)

    def body(s_ref, a_ref, b_ref, o_ref):
        o_ref[...] = (a_ref[...].astype(F32) + b_ref[...].astype(F32)).astype(o_ref.dtype)

    return pl.pallas_call(
        body, name=name, out_shape=jax.ShapeDtypeStruct((4, r, c), got.dtype),
        grid_spec=pltpu.PrefetchScalarGridSpec(
            num_scalar_prefetch=1, grid=(4, r // rb, c // cb),
            in_specs=[pl.BlockSpec((None, rb, cb), lambda k, i, j, s: (2 * k + s[0], i, j)),
                      pl.BlockSpec((None, rb, cb), lambda k, i, j, s: (k, i, j))],
            out_specs=pl.BlockSpec((None, rb, cb), lambda k, i, j, s: (k, i, j))),
        compiler_params=_params(3, 24 << 20),
    )(core, _in_hbm(own), _in_hbm(got))


def _adamw_math(w, g, m, v):
    m = B1 * m + (1.0 - B1) * g
    v = B2 * v + (1.0 - B2) * (g * g)
    m_hat = m / (1.0 - B1 ** STEP)
    v_hat = v / (1.0 - B2 ** STEP)
    return -LR * (m_hat / (jnp.sqrt(v_hat) + ADAM_EPS) + WD * w), m, v


def _adamw_big(name, own, got, chip, w, m, v):
    r, c = w.shape
    rb, cb = _blocks_of(r, c)

    def body(s_ref, o_ref, p_ref, w_ref, m_ref, v_ref, g_ref, d_ref, nm_ref, nv_ref):
        g = o_ref[...].astype(F32)
        for k in range(3):
            g = g + p_ref[k].astype(F32)
        g_ref[...] = g
        d_ref[...], nm_ref[...], nv_ref[...] = _adamw_math(w_ref[...], g, m_ref[...], v_ref[...])

    blk = pl.BlockSpec((rb, cb), lambda i, j, s: (i, j))
    return pl.pallas_call(
        body, name=name, out_shape=[jax.ShapeDtypeStruct((r, c), F32)] * 4,
        grid_spec=pltpu.PrefetchScalarGridSpec(
            num_scalar_prefetch=1, grid=(r // rb, c // cb),
            in_specs=[pl.BlockSpec((None, rb, cb), lambda i, j, s: (s[0], i, j)),
                      pl.BlockSpec((3, rb, cb), lambda i, j, s: (0, i, j)), blk, blk, blk],
            out_specs=[blk] * 4),
        compiler_params=_params(2, 40 << 20),
    )(chip, *map(_in_hbm, (own, got, w, m, v)))


def _adamw_multi(name, items, sums=(), row_blocks=1):
    n, ns = len(items), len(sums)

    def total(ref):
        g = ref[0]
        for d in range(1, ref.shape[0]):
            g = g + ref[d]
        return g

    def body(*refs):
        ins, outs = refs[:4 * n + ns], refs[4 * n + ns:]
        for t in range(n):
            p_ref, w_ref, m_ref, v_ref = ins[4 * t:4 * t + 4]
            g = total(p_ref)
            outs[4 * t][...] = g
            outs[4 * t + 1][...], outs[4 * t + 2][...], outs[4 * t + 3][...] = _adamw_math(
                w_ref[...], g, m_ref[...], v_ref[...])
        for t in range(ns):
            outs[4 * n + t][...] = total(ins[4 * n + t])

    def spec(shape, lead):
        blk = (shape[0] // row_blocks,) + tuple(shape[1:])
        nd = len(shape)
        if lead:
            return pl.BlockSpec((lead,) + blk, lambda i: (0, i) + (0,) * (nd - 1))
        return pl.BlockSpec(blk, lambda i: (i,) + (0,) * (nd - 1))

    operands, in_specs, out_specs, out_shape = [], [], [], []
    for parts, w, m, v in items:
        assert parts.shape[1:] == w.shape == m.shape == v.shape, (name, parts.shape, w.shape)
        operands += [parts, w, m, v]
        in_specs += [spec(w.shape, parts.shape[0])] + [spec(w.shape, 0)] * 3
        out_specs += [spec(w.shape, 0)] * 4
        out_shape += [jax.ShapeDtypeStruct(w.shape, F32)] * 4
    for parts in sums:
        operands.append(parts)
        in_specs.append(spec(parts.shape[1:], parts.shape[0]))
        out_specs.append(spec(parts.shape[1:], False))
        out_shape.append(jax.ShapeDtypeStruct(parts.shape[1:], F32))
    return pl.pallas_call(
        body, name=name, grid=(row_blocks,), in_specs=in_specs, out_specs=out_specs, out_shape=out_shape,
        compiler_params=_params(1, 56 << 20),
    )(*operands)


def kernel(x, meta_tokens, mix_norm, w_in, lam_re, lam_im, log_dt, b_re, b_im, c_re, c_im, d_skip, w_glu, b_glu, q_a_norm, w_q_b, kv_a_norm, w_kv_b, out_norm_ssm, out_norm_attn, w_out, ffn_norm, w_up, conv_w, conv_b, w_down, final_norm, loss_target, m_meta_tokens, m_mix_norm, m_w_in, m_lam_re, m_lam_im, m_log_dt, m_b_re, m_b_im, m_c_re, m_c_im, m_d_skip, m_w_glu, m_b_glu, m_q_a_norm, m_w_q_b, m_kv_a_norm, m_w_kv_b, m_out_norm_ssm, m_out_norm_attn, m_w_out, m_ffn_norm, m_w_up, m_conv_w, m_conv_b, m_w_down, m_final_norm, v_meta_tokens, v_mix_norm, v_w_in, v_lam_re, v_lam_im, v_log_dt, v_b_re, v_b_im, v_c_re, v_c_im, v_d_skip, v_w_glu, v_b_glu, v_q_a_norm, v_w_q_b, v_kv_a_norm, v_w_kv_b, v_out_norm_ssm, v_out_norm_attn, v_w_out, v_ffn_norm, v_w_up, v_conv_w, v_conv_b, v_w_down, v_final_norm):
    given = dict(locals())
    w = {n: given[n] for n in WEIGHTS}
    m = {n: given['m_' + n] for n in WEIGHTS}
    v = {n: given['v_' + n] for n in WEIGHTS}
    dev = 4 * lax.axis_index("x") + 2 * lax.axis_index("y") + lax.axis_index("c")

    shard = {
        'w_in': jnp.pad(w_in[0], ((0, 0), (0, D_IN_PAD - D_IN))),
        'w_glu': w_glu[0],
        'w_q_b': jnp.pad(w_q_b[0], ((0, 0), (0, HEAD_PAD - QK_NOPE - QK_ROPE))),
        'w_kv_b': w_kv_b[0],
        'w_out': w_out[0],
        'w_up': w_up[0],
        'w_down': w_down[0],
    }
    core = lax.axis_index("c").astype(jnp.int32).reshape(1)
    chip = (2 * lax.axis_index("x") + lax.axis_index("y")).astype(jnp.int32).reshape(1)
    early, late = ['w_in', 'w_glu', 'w_q_b', 'w_kv_b'], ['w_out', 'w_up', 'w_down']
    shard = {n: a.astype(BF16) for n, a in shard.items()}

    gathered = _all_gather("gather_early", [shard[n] for n in early]
                           + [meta_tokens, jnp.pad(conv_w[0], ((0, 5), (0, 0)))])
    wf = dict(zip(early, gathered[:len(early)]))
    for n in ('w_in', 'w_glu'):
        wf[n] = wf[n].reshape(-1, wf[n].shape[-1])
    meta = gathered[-2].transpose(1, 0, 2).reshape(N_META, D)
    wf['conv_w'] = gathered[-1][:, :3].transpose(1, 0, 2).reshape(3, D_FF)
    gather_token, wait_late = _split_exchange("gather_late", [shard[n] for n in late],
                                              [(N_DEV,) + shard[n].shape for n in late], _gather_copies, 4,
                                              after=gathered[0])

    def late_weights(after):
        mine, landed = wait_late(after)
        full = _gather_forward("gather_late_forward",
                               [lax.dynamic_update_slice(a, s[None], (dev, 0, 0)) for s, a in zip(mine, landed)])
        return full[0].reshape(-1, D), full[1], full[2].reshape(-1, D)

    pending = []

    def send_grads(names, arrs):
        got = _exchange_cores("reduce_cores_" + names[0], arrs)
        sums = [_pair_sum("sum_cores_" + n, a, g, core) for n, a, g in zip(names, arrs, got)]
        token, wait = _split_exchange("reduce_chips_" + names[0], sums, [(3,) + s.shape[1:] for s in sums],
                                      _chips_copies, 3, after=sums[-1])
        pending.append((names, wait))
        return token

    small = {n: w[n] for n in SMALL}
    small['mix_norm'] = mix_norm + gather_token[0:1, 0:1]
    loss_part, grad_x, dmeta, dcw4, big, sm = _local_step(x[0], loss_target[0], meta, small, wf, late_weights,
                                                          send_grads)
    grads, deltas, new_m, new_v = {}, {}, {}, {}

    def keep(n, outs):
        grads[n], deltas[n], new_m[n], new_v[n] = (o.reshape(w[n].shape) for o in outs)

    rest = list(big)
    from_core = _exchange_cores("reduce_cores", [big[n] for n in rest])
    chip_sums = [_pair_sum("sum_cores_" + n, big[n], got, core) for n, got in zip(rest, from_core)]
    small_parts, from_chips = _all_gather("reduce_chips_gather_small", [sm[n] for n in SMALL] + [dmeta, dcw4, loss_part],
                                          chip_blocks=chip_sums)
    reduced = list(zip(rest, chip_sums, from_chips))
    for names, wait in pending:
        reduced += zip(names, *wait(from_chips[0]))
    for n, own, got in reduced:
        if n == 'w_up':
            outs = _adamw_big("adamw_" + n, own, got, chip, w[n][0].T, m[n][0].T, v[n][0].T)
            keep(n, [o.T for o in outs])
            continue
        cols = w[n].shape[2]
        keep(n, _adamw_big("adamw_" + n, own[:, :, :cols], got[:, :, :cols], chip, w[n][0], m[n][0], v[n][0]))

    parts = dict(zip(SMALL + ['meta_tokens', 'conv_w', 'loss'], small_parts))
    wide = [n for n in SMALL if n not in ('b_re', 'b_im')]
    outs = _adamw_multi("adamw_small",
                        [(parts[n],) + tuple(t[n].reshape(sm[n].shape) for t in (w, m, v)) for n in wide],
                        sums=[parts[n] for n in ('meta_tokens', 'conv_w', 'loss', 'b_re', 'b_im')])
    for i, n in enumerate(wide):
        keep(n, outs[4 * i:4 * i + 4])
    g_meta, g_cw4, loss, g_bre, g_bim = outs[-5:]
    gpc = lambda g: g.reshape(SSM_GROUP, N_GROUPS, SSM_STATE).transpose(1, 2, 0).reshape(-1, SSM_GROUP)[None]
    outs = _adamw_multi("adamw_b", [(gpc(g),) + tuple(t[n].reshape(-1, SSM_GROUP) for t in (w, m, v))
                                    for n, g in (('b_re', g_bre), ('b_im', g_bim))], row_blocks=8)
    keep('b_re', outs[:4])
    keep('b_im', outs[4:])

    g_meta = lax.dynamic_slice(g_meta, (0, dev * (D // N_DEV)), (N_META, D // N_DEV))
    g_conv = lax.dynamic_slice(g_cw4.transpose(1, 0, 2).reshape(3, D_FF), (0, dev * (D_FF // N_DEV)),
                               (3, D_FF // N_DEV))
    rows8 = lambda a: jnp.pad(a.reshape(3, D_FF // N_DEV), ((0, 5), (0, 0)))
    outs = _adamw_multi("adamw_cols", [(g_meta[None], meta_tokens, m['meta_tokens'], v['meta_tokens']),
                                       (rows8(g_conv)[None], rows8(conv_w), rows8(m['conv_w']), rows8(v['conv_w']))])
    keep('meta_tokens', outs[:4])
    keep('conv_w', [o[:3] for o in outs[4:]])

    return (loss[0, 0], grad_x[None], *[grads[n] for n in WEIGHTS], *[deltas[n] for n in WEIGHTS],
            *[new_m[n] for n in WEIGHTS], *[new_v[n] for n in WEIGHTS])
```

```python
import functools
import math

import jax
import jax.numpy as jnp
from jax import lax
from jax.experimental import pallas as pl
from jax.experimental.pallas import tpu as pltpu

F32, BF16 = jnp.float32, jnp.bfloat16

N_DEV = 8
N_META, SEQ, D = 16, 2048, 2048
T = N_META + SEQ
TR = 688
EPS = 1e-6
D_SSM, Q_LORA, KV_LORA, QK_ROPE = 1024, 512, 256, 64
D_IN = D_SSM + Q_LORA + KV_LORA + QK_ROPE
D_IN_PAD = 1920
N_HEADS, QK_NOPE, V_HEAD = 8, 128, 128
HEAD_PAD = 256
SM_SCALE = 1.0 / math.sqrt(QK_NOPE + QK_ROPE)
CHUNK = 64
N_GROUPS, SSM_GROUP, SSM_STATE = 64, 16, 64
N_SEG = 8
SEG = T // N_SEG
GQ = 8
D_FF = 5504
FF_PIECE = 1376
ROPE_BASE = 10000.0
LR, B1, B2, ADAM_EPS, WD, STEP = 0.001, 0.9, 0.999, 1e-08, 0.01, 10
VMEM_CAP = 60 * 1024 * 1024
MESH = pl.DeviceIdType.MESH

WEIGHTS = ['meta_tokens', 'mix_norm', 'w_in', 'lam_re', 'lam_im', 'log_dt', 'b_re', 'b_im', 'c_re', 'c_im',
           'd_skip', 'w_glu', 'b_glu', 'q_a_norm', 'w_q_b', 'kv_a_norm', 'w_kv_b', 'out_norm_ssm',
           'out_norm_attn', 'w_out', 'ffn_norm', 'w_up', 'conv_w', 'conv_b', 'w_down', 'final_norm']
BIG = ['w_in', 'w_glu', 'w_q_b', 'w_kv_b', 'w_out', 'w_up', 'w_down']
SMALL = [n for n in WEIGHTS if n not in BIG and n not in ('meta_tokens', 'conv_w')]


def _nbytes(shape, dtype):
    return math.prod(shape) * jnp.dtype(dtype).itemsize


def _in_hbm(a):
    return pltpu.with_memory_space_constraint(a, pltpu.HBM) if a.size * a.dtype.itemsize >= (1 << 20) else a


def _params(sem, need):
    return pltpu.CompilerParams(dimension_semantics=("arbitrary",) * sem,
                                vmem_limit_bytes=int(min(VMEM_CAP, max(need, 16 * 1024 * 1024))))


_DIMS = {'nn': (((1,), (0,)), ((), ())), 'nt': (((1,), (1,)), ((), ())), 'tn': (((0,), (0,)), ((), ()))}


def _mm(name, a, b, mode, tm, tn, tk, out_dtype, stack=None, res=None):
    sa, sb, so = stack in ('a_out', 'ab_red'), stack in ('b_out', 'ab_red'), stack in ('a_out', 'b_out')
    nj = a.shape[0] if sa else (b.shape[0] if sb else 1)
    a2, b2 = a.shape[-2:], b.shape[-2:]
    if mode == 'nn':
        (m, k), (k2, n) = a2, b2
    elif mode == 'nt':
        (m, k), (n, k2) = a2, b2
    else:
        (k, m), (k2, n) = a2, b2
    assert k == k2 and m % tm == 0 and n % tn == 0 and k % tk == 0, (name, a.shape, b.shape)
    n_jo, n_jr, n_k = (nj if so else 1), (nj if stack == 'ab_red' else 1), k // tk
    grid = (n_jo, m // tm, n // tn, n_jr, n_k)
    a_blk, a_idx = ((tk, tm), lambda i, j, kk: (kk, i)) if mode == 'tn' else ((tm, tk), lambda i, j, kk: (i, kk))
    b_blk, b_idx = ((tn, tk), lambda i, j, kk: (j, kk)) if mode == 'nt' else ((tk, tn), lambda i, j, kk: (kk, j))

    def spec(blk, idx, stacked, on_out):
        if not stacked:
            return pl.BlockSpec(blk, lambda jo, i, j, jr, kk: idx(i, j, kk))
        return pl.BlockSpec((None,) + blk, lambda jo, i, j, jr, kk: ((jo if on_out else jr),) + idx(i, j, kk))

    o_idx = lambda i, j, kk: (i, j)
    in_specs = [spec(a_blk, a_idx, sa, stack == 'a_out'), spec(b_blk, b_idx, sb, stack == 'b_out')]
    operands = [a, b]
    if res is not None:
        in_specs.append(spec((tm, tn), o_idx, False, False))
        operands.append(res)
    n_red = n_jr * n_k
    dims = _DIMS[mode]

    def body(*refs):
        a_ref, b_ref = refs[0], refs[1]
        res_ref = refs[2] if res is not None else None
        o_ref = refs[3] if res is not None else refs[2]
        part = lax.dot_general(a_ref[...].astype(BF16), b_ref[...].astype(BF16), dims,
                               preferred_element_type=F32)

        def finish(total):
            if res_ref is not None:
                total = total + res_ref[...]
            o_ref[...] = total.astype(o_ref.dtype)

        if n_red == 1:
            finish(part)
        else:
            acc_ref = refs[-1]
            step = pl.program_id(3) * n_k + pl.program_id(4)

            @pl.when(step == 0)
            def _():
                acc_ref[...] = part

            @pl.when(step > 0)
            def _():
                acc_ref[...] += part

            @pl.when(step == n_red - 1)
            def _():
                finish(acc_ref[...])

    out_shape = ((nj,) if so else ()) + (m, n)
    need = 2 * (_nbytes(a_blk, a.dtype) + _nbytes(b_blk, b.dtype) + _nbytes((tm, tn), out_dtype)
                + (_nbytes((tm, tn), F32) if res is not None else 0)) + 3 * _nbytes((tm, tn), F32) + (4 << 20)
    return pl.pallas_call(
        body, name=name, grid=grid, in_specs=in_specs, out_specs=spec((tm, tn), o_idx, so, True),
        out_shape=jax.ShapeDtypeStruct(out_shape, out_dtype),
        scratch_shapes=[pltpu.VMEM((tm, tn), F32)] if n_red > 1 else [],
        compiler_params=_params(5, need),
    )(*map(_in_hbm, operands))


def _rowwise(name, fn, row_ins, const_ins, row_outs, acc_outs, tr=TR):
    n_ri, n_ci, n_ro, n_ao = len(row_ins), len(const_ins), len(row_outs), len(acc_outs)

    def body(*refs):
        ri, ci = refs[:n_ri], refs[n_ri:n_ri + n_ci]
        ro, ao = refs[n_ri + n_ci:n_ri + n_ci + n_ro], refs[n_ri + n_ci + n_ro:]
        i = pl.program_id(0)
        outs = fn(i * tr, *[r[...] for r in ri], *[r[...] for r in ci])
        for r, v in zip(ro, outs[:n_ro]):
            r[...] = v.astype(r.dtype)
        if n_ao:
            @pl.when(i == 0)
            def _():
                for r, v in zip(ao, outs[n_ro:]):
                    r[...] = v

            @pl.when(i > 0)
            def _():
                for r, v in zip(ao, outs[n_ro:]):
                    r[...] += v

    in_specs = [pl.BlockSpec((tr, w), functools.partial(lambda cb, i: (i, cb), cb)) for _, w, cb in row_ins]
    in_specs += [pl.BlockSpec(c.shape, functools.partial(lambda nd, i: (0,) * nd, c.ndim)) for c in const_ins]
    out_specs = [pl.BlockSpec((tr, w), lambda i: (i, 0)) for w, _ in row_outs]
    out_specs += [pl.BlockSpec((1, w), lambda i: (0, 0)) for w in acc_outs]
    out_shape = [jax.ShapeDtypeStruct((T, w), dt) for w, dt in row_outs]
    out_shape += [jax.ShapeDtypeStruct((1, w), F32) for w in acc_outs]
    need = 2 * (sum(_nbytes((tr, w), a.dtype) for a, w, _ in row_ins) + sum(_nbytes((tr, w), dt) for w, dt in row_outs))
    need = 3 * need + (8 << 20)
    return pl.pallas_call(
        body, name=name, grid=(T // tr,), in_specs=in_specs, out_specs=out_specs, out_shape=out_shape,
        compiler_params=_params(1, need),
    )(*[_in_hbm(a) for a, _, _ in row_ins], *const_ins)


def _rms_fwd_fn(row0, h, g):
    h = h.astype(F32)
    r = lax.rsqrt(jnp.mean(h * h, axis=-1, keepdims=True) + EPS)
    return (h * r) * g, r


def _rms_bwd(dxn, h, r, g):
    xhat = h * r
    dxg = dxn * g
    dh = r * (dxg - xhat * jnp.mean(dxg * xhat, axis=-1, keepdims=True))
    return dh, jnp.sum(dxn * xhat, axis=0, keepdims=True)


def _rms_bwd_fn(row0, dxn, h, r, g):
    return _rms_bwd(dxn.astype(F32), h.astype(F32), r, g)


def _rms_bwd_res_fn(row0, dxn, h, r, res, g):
    dh, dg = _rms_bwd(dxn.astype(F32), h.astype(F32), r, g)
    return dh + res, dg


def _gelu_fn(row0, y):
    return (jax.nn.gelu(y),)


def _glu_fwd_fn(row0, y, gl, b, gain):
    ya = jax.nn.gelu(y) * jax.nn.sigmoid(gl + b)
    return _rms_fwd_fn(row0, ya, gain)


def _glu_bwd_fn(row0, dyan, y, gl, ra, b, gain):
    g = jax.nn.gelu(y)
    s = jax.nn.sigmoid(gl + b)
    dya, dgain = _rms_bwd(dyan, g * s, ra, gain)
    dgl = dya * g * s * (1.0 - s)
    return dgl, dya * s, dgain, jnp.sum(dgl, axis=0, keepdims=True)


def _gelu_bwd_fn(row0, dg, y):
    return (jax.vjp(jax.nn.gelu, y)[1](dg)[0],)


def _loss_fn(row0, h, tgt, gain):
    r = lax.rsqrt(jnp.mean(h * h, axis=-1, keepdims=True) + EPS)
    xhat = h * r
    rows = row0 + lax.broadcasted_iota(jnp.int32, h.shape, 0)
    err = jnp.where(rows >= N_META, xhat * gain - tgt, 0.0)
    loss = jnp.full((1, 128), jnp.sum(err * err) * (0.5 / D), F32)
    dh, dgain = _rms_bwd(err * (1.0 / D), h, r, gain)
    return dh, dgain, loss


def _rope_tables():
    pos = jnp.arange(T, dtype=jnp.int32)
    inv_freq = 1.0 / (ROPE_BASE ** (jnp.arange(0, QK_ROPE, 2, dtype=F32) / QK_ROPE))
    ang = pos.astype(F32)[:, None] * inv_freq[None, :]
    cos, sin, z32, z64 = jnp.cos(ang), jnp.sin(ang), jnp.zeros((T, 32), F32), jnp.zeros((T, 64), F32)
    return (jnp.concatenate([cos, cos, z64], 1), jnp.concatenate([-sin, z32, z64], 1),
            jnp.concatenate([z32, sin, z64], 1))


def _rot(x, cc, s1, s2):
    return x * cc + pltpu.roll(x, 96, 1) * s1 + pltpu.roll(x, 32, 1) * s2


def _derot(d, cc, s1, s2):
    return d * cc + pltpu.roll(d * s1, 32, 1) + pltpu.roll(d * s2, 96, 1)


def _chunk_of(pos):
    return jnp.where(pos < N_META, 0, 1 + ((pos - N_META) >> 6))


def _key_len(t):
    last_chunk = 1 + ((t + 1) * TR - 1 - N_META) // CHUNK
    return min(T, pl.cdiv(N_META + last_chunk * CHUNK, 128) * 128)


def _attn_probs(t, q_ref, kv_ref, kpe_ref, tq, tk):
    kl = _key_len(t)
    q = q_ref[...]
    qn = q[:, :QK_NOPE].astype(BF16)
    qp = _rot(q[:, QK_NOPE:], tq[0][...], tq[1][...], tq[2][...]).astype(BF16)
    kn = kv_ref[:kl, :QK_NOPE].astype(BF16)
    v = kv_ref[:kl, QK_NOPE:].astype(BF16)
    kp = _rot(kpe_ref[:kl, :], tk[0][:kl, :], tk[1][:kl, :], tk[2][:kl, :]).astype(BF16)
    s = lax.dot_general(qn, kn, _DIMS['nt'], preferred_element_type=F32)
    s = s + lax.dot_general(qp, kp, _DIMS['nt'], preferred_element_type=F32)
    qc = _chunk_of(t * TR + lax.broadcasted_iota(jnp.int32, (TR, 1), 0))
    kc = _chunk_of(lax.broadcasted_iota(jnp.int32, (1, kl), 1))
    s = jnp.where(kc <= qc, s * SM_SCALE, jnp.finfo(F32).min)
    p = jnp.exp(s - jnp.max(s, axis=-1, keepdims=True))
    p = p * (1.0 / jnp.sum(p, axis=-1, keepdims=True))
    return qn, qp, kn, kp, v, p


def _per_q_tile(fn):
    for t in range(T // TR):
        pl.when(pl.program_id(1) == t)(functools.partial(fn, t))


def _attn_specs(z, tabs):
    q_spec = pl.BlockSpec((None, TR, HEAD_PAD), lambda h, i: (h, i, 0))
    kv_spec = pl.BlockSpec((None, T, HEAD_PAD), lambda h, i: (h, 0, 0))
    kpe_spec = pl.BlockSpec((T, 128), lambda h, i: (0, (D_IN - QK_ROPE) // 128))
    tq = [pl.BlockSpec((TR, 128), lambda h, i: (i, 0))] * 3
    tk = [pl.BlockSpec((T, 128), lambda h, i: (0, 0))] * 3
    return [q_spec, kv_spec, kpe_spec] + tq + tk


def _attn_fwd(q, kv, z, tabs):
    def body(q_ref, kv_ref, kpe_ref, c1, c2, c3, k1, k2, k3, o_ref):
        def tile(t):
            _, _, _, _, v, p = _attn_probs(t, q_ref, kv_ref, kpe_ref, (c1, c2, c3), (k1, k2, k3))
            o_ref[...] = jnp.dot(p.astype(BF16), v, preferred_element_type=F32)

        _per_q_tile(tile)

    return pl.pallas_call(
        body, name="attn_fwd", grid=(N_HEADS, T // TR), in_specs=_attn_specs(z, tabs),
        out_specs=pl.BlockSpec((TR, V_HEAD), lambda h, i: (i, h)),
        out_shape=jax.ShapeDtypeStruct((T, N_HEADS * V_HEAD), F32),
        compiler_params=_params(2, 48 << 20),
    )(_in_hbm(q), _in_hbm(kv), _in_hbm(z), *tabs, *tabs)


def _attn_bwd(q, kv, z, tabs, dyb):
    def body(q_ref, kv_ref, kpe_ref, c1, c2, c3, k1, k2, k3, do_ref, dq_ref, dkv_ref, dkpe_ref):
        @pl.when(pl.program_id(1) == 0)
        def _():
            dkv_ref[...] = jnp.zeros(dkv_ref.shape, F32)
            dkpe_ref[...] = jnp.zeros(dkpe_ref.shape, F32)

        def tile(t):
            kl = _key_len(t)
            qn, qp, kn, kp, v, p = _attn_probs(t, q_ref, kv_ref, kpe_ref, (c1, c2, c3), (k1, k2, k3))
            do = do_ref[...].astype(BF16)
            dp = lax.dot_general(do, v, _DIMS['nt'], preferred_element_type=F32)
            ds = (p * (dp - jnp.sum(p * dp, axis=-1, keepdims=True)) * SM_SCALE).astype(BF16)
            pb = p.astype(BF16)
            dq_ref[:, :QK_NOPE] = jnp.dot(ds, kn, preferred_element_type=F32).astype(dq_ref.dtype)
            dqp = _derot(jnp.dot(ds, kp, preferred_element_type=F32), c1[...], c2[...], c3[...])
            dq_ref[:, QK_NOPE:] = dqp.astype(dq_ref.dtype)
            dkv_ref[:kl, :QK_NOPE] += lax.dot_general(ds, qn, _DIMS['tn'], preferred_element_type=F32)
            dkv_ref[:kl, QK_NOPE:] += lax.dot_general(pb, do, _DIMS['tn'], preferred_element_type=F32)
            dkpe_ref[:kl, :] += lax.dot_general(ds, qp, _DIMS['tn'], preferred_element_type=F32)

        _per_q_tile(tile)

    return pl.pallas_call(
        body, name="attn_bwd", grid=(N_HEADS, T // TR),
        in_specs=_attn_specs(z, tabs) + [pl.BlockSpec((TR, V_HEAD), lambda h, i: (i, h))],
        out_specs=[pl.BlockSpec((None, TR, HEAD_PAD), lambda h, i: (h, i, 0)),
                   pl.BlockSpec((None, T, HEAD_PAD), lambda h, i: (h, 0, 0)),
                   pl.BlockSpec((None, T, 128), lambda h, i: (h, 0, 0))],
        out_shape=[jax.ShapeDtypeStruct((N_HEADS, T, HEAD_PAD), BF16),
                   jax.ShapeDtypeStruct((N_HEADS, T, HEAD_PAD), F32),
                   jax.ShapeDtypeStruct((N_HEADS, T, 128), F32)],
        compiler_params=_params(2, 56 << 20),
    )(_in_hbm(q), _in_hbm(kv), _in_hbm(z), *tabs, *tabs, _in_hbm(dyb))


def _kpe_bwd(dkpe, tabs):
    def body(d_ref, c1, c2, c3, o_ref):
        d = d_ref[0]
        for h in range(1, N_HEADS):
            d = d + d_ref[h]
        o_ref[...] = _derot(d, c1[...], c2[...], c3[...]).astype(o_ref.dtype)

    tile = pl.BlockSpec((TR, 128), lambda i: (i, 0))
    return pl.pallas_call(
        body, name="kpe_bwd", grid=(T // TR,),
        in_specs=[pl.BlockSpec((N_HEADS, TR, 128), lambda i: (0, i, 0)), tile, tile, tile], out_specs=tile,
        out_shape=jax.ShapeDtypeStruct((T, 128), BF16), compiler_params=_params(1, 16 << 20),
    )(_in_hbm(dkpe), *tabs)


def _s5_prep_math(lam_re, lam_im, log_dt, bt_re, bt_im):
    dt = jnp.exp(log_dt)
    mag = jnp.exp(lam_re * dt)
    a_re, a_im = mag * jnp.cos(lam_im * dt), mag * jnp.sin(lam_im * dt)
    den = lam_re * lam_re + lam_im * lam_im
    w_re = ((a_re - 1.0) * lam_re + a_im * lam_im) / den
    w_im = (a_im * lam_re - (a_re - 1.0) * lam_im) / den
    return a_re, a_im, w_re[None] * bt_re - w_im[None] * bt_im, w_re[None] * bt_im + w_im[None] * bt_re


def _eye_groups():
    return (lax.broadcasted_iota(jnp.int32, (N_GROUPS, N_GROUPS), 0)
            == lax.broadcasted_iota(jnp.int32, (N_GROUPS, N_GROUPS), 1)).astype(F32)


def _row_to_col(row):
    return jnp.sum(_eye_groups() * row, axis=1, keepdims=True)


def _s5_prep(lam_re, lam_im, log_dt, bt_re, bt_im):
    def body(lr, li, ld, br, bi, ar, ai, bbr, bbi):
        ar[...], ai[...], bbr[...], bbi[...] = _s5_prep_math(lr[...], li[...], _row_to_col(ld[...]), br[...], bi[...])

    gp, cgp = jax.ShapeDtypeStruct((N_GROUPS, SSM_STATE), F32), jax.ShapeDtypeStruct(bt_re.shape, F32)
    return pl.pallas_call(body, name="s5_prep", out_shape=[gp, gp, cgp, cgp])(lam_re, lam_im, log_dt, bt_re, bt_im)


def _s5_prep_bwd(lam_re, lam_im, log_dt, bt_re, bt_im, da_re, da_im, dbb_re, dbb_im):
    def body(lr, li, ld, br, bi, dar, dai, dbr, dbi, o1, o2, o3, o4, o5):
        _, vjp = jax.vjp(_s5_prep_math, lr[...], li[...], _row_to_col(ld[...]), br[...], bi[...])
        o1[...], o2[...], dld, o4[...], o5[...] = vjp((dar[...], dai[...], dbr[...], dbi[...]))
        o3[...] = jnp.sum(_eye_groups() * dld, axis=0, keepdims=True)

    ins = (lam_re, lam_im, log_dt, bt_re, bt_im)
    return pl.pallas_call(body, name="s5_prep_bwd", out_shape=[jax.ShapeDtypeStruct(a.shape, F32) for a in ins])(
        *ins, da_re, da_im, dbb_re, dbb_im)


def _cmul(ar, ai, br, bi):
    return ar * br - ai * bi, ar * bi + ai * br


def _seg_rows(i):
    return pl.ds(pl.multiple_of(i * N_SEG, N_SEG), N_SEG)


def _scan(xr, xi, ar, ai, reverse):
    zero = jnp.zeros(ar.shape, F32)

    def local(j, carry):
        rows = _seg_rows(SEG - 1 - j if reverse else j)
        nr, ni = _cmul(ar, ai, *carry)
        nr, ni = nr + xr[rows, :], ni + xi[rows, :]
        xr[rows, :], xi[rows, :] = nr, ni
        return nr, ni

    er, ei = lax.fori_loop(0, SEG, local, (zero, zero))
    pr, pi = ar, ai
    for _ in range(8):
        pr, pi = _cmul(pr, pi, pr, pi)
    pr, pi = _cmul(*_cmul(pr, pi, ar, ai), ar, ai)
    row = lax.broadcasted_iota(jnp.int32, ar.shape, 0)
    edge, shift = (N_SEG - 1, N_SEG - 1) if reverse else (0, 1)
    hr, hi = zero, zero
    for _ in range(N_SEG - 1):
        tr_, ti_ = _cmul(pr, pi, hr, hi)
        hr = jnp.where(row == edge, 0.0, pltpu.roll(tr_ + er, shift, 0))
        hi = jnp.where(row == edge, 0.0, pltpu.roll(ti_ + ei, shift, 0))

    def fix(j, carry):
        rows = _seg_rows(SEG - 1 - j if reverse else j)
        xr[rows, :] += carry[0]
        xi[rows, :] += carry[1]
        return _cmul(ar, ai, *carry)

    lax.fori_loop(0, SEG, fix, _cmul(ar, ai, hr, hi))
    return hr, hi


def _split(x):
    hi = x.astype(BF16)
    return hi, (x - hi.astype(F32)).astype(BF16)


def _dot3(xs, ys, mode='nn'):
    d = lambda p, q: lax.dot_general(p, q, _DIMS[mode], preferred_element_type=F32)
    return d(xs[0], ys[0]) + (d(xs[1], ys[0]) + d(xs[0], ys[1]))


def _s5_hidden(us, a_ref, bre_ref, bim_ref, hre, him):
    hre[...] = _dot3(us, _split(bre_ref[...]))
    him[...] = _dot3(us, _split(bim_ref[...]))
    return _scan(hre, him, a_ref[0], a_ref[1], reverse=False)


def _s5_specs():
    col = pl.BlockSpec((T, 128), lambda q: (0, q))
    a_spec = pl.BlockSpec((None, 2, N_SEG, GQ * SSM_STATE), lambda q: (q, 0, 0, 0))
    wide = pl.BlockSpec((None, 128, GQ * SSM_STATE), lambda q: (q, 0, 0))
    tall = pl.BlockSpec((None, GQ * SSM_STATE, 128), lambda q: (q, 0, 0))
    d_spec = pl.BlockSpec((1, 128), lambda q: (0, q))
    return col, a_spec, wide, tall, d_spec


def _s5_fwd(u, a8, bre, bim, cre, cim, dsk):
    def body(u_ref, a_ref, bre_ref, bim_ref, cre_ref, cim_ref, d_ref, y_ref, hre, him):
        u_ = u_ref[...]
        _s5_hidden(_split(u_), a_ref, bre_ref, bim_ref, hre, him)
        y_ref[...] = (_dot3(_split(hre[...]), _split(cre_ref[...]))
                      - _dot3(_split(him[...]), _split(cim_ref[...])) + d_ref[...] * u_)

    col, a_spec, wide, tall, d_spec = _s5_specs()
    return pl.pallas_call(
        body, name="s5_fwd", grid=(N_GROUPS // GQ,), in_specs=[col, a_spec, wide, wide, tall, tall, d_spec],
        out_specs=col, out_shape=jax.ShapeDtypeStruct((T, D_SSM), F32),
        scratch_shapes=[pltpu.VMEM((T, GQ * SSM_STATE), F32)] * 2,
        compiler_params=_params(1, 40 << 20),
    )(_in_hbm(u), a8, bre, bim, cre, cim, dsk)


def _s5_bwd(u, dy, a8, bre, bim, cre, cim, dsk):
    def body(u_ref, dy_ref, a_ref, bre_ref, bim_ref, cre_ref, cim_ref, d_ref,
             du_ref, dbre_ref, dbim_ref, dcre_ref, dcim_ref, da_ref, dd_ref, hre, him, lre, lim):
        u_, dy_ = u_ref[...], dy_ref[...]
        us, dys = _split(u_), _split(dy_)
        ar, ai = a_ref[0], a_ref[1]
        bres, bims = _split(bre_ref[...]), _split(bim_ref[...])
        hre[...] = _dot3(us, bres)
        him[...] = _dot3(us, bims)
        h0r, h0i = _scan(hre, him, ar, ai, reverse=False)
        lre[...] = _dot3(dys, _split(cre_ref[...]), 'nt')
        lim[...] = -_dot3(dys, _split(cim_ref[...]), 'nt')
        _scan(lre, lim, ar, -ai, reverse=True)

        def acc_da(i, carry):
            lr, li = lre[_seg_rows(i), :], lim[_seg_rows(i), :]
            pr, pi = hre[_seg_rows(i - 1), :], him[_seg_rows(i - 1), :]
            return carry[0] + lr * pr + li * pi, carry[1] + li * pr - lr * pi

        lr, li = lre[_seg_rows(0), :], lim[_seg_rows(0), :]
        dar, dai = lax.fori_loop(1, SEG, acc_da, (lr * h0r + li * h0i, li * h0r - lr * h0i))
        da_ref[0:1, :] = jnp.sum(dar, axis=0, keepdims=True)
        da_ref[1:2, :] = jnp.sum(dai, axis=0, keepdims=True)
        dot = lambda p, q, mode: lax.dot_general(p, q, _DIMS[mode], preferred_element_type=F32)
        lr1, li1 = lre[...].astype(BF16), lim[...].astype(BF16)
        du_ref[...] = dy_ * d_ref[...] + dot(lr1, bres[0], 'nt') + dot(li1, bims[0], 'nt')
        dbre_ref[...] = dot(us[0], lr1, 'tn')
        dbim_ref[...] = dot(us[0], li1, 'tn')
        dcre_ref[...] = dot(hre[...].astype(BF16), dys[0], 'tn')
        dcim_ref[...] = -dot(him[...].astype(BF16), dys[0], 'tn')
        dd_ref[...] = jnp.sum(dy_ * u_, axis=0, keepdims=True)

    col, a_spec, wide, tall, d_spec = _s5_specs()
    nq, ns = N_GROUPS // GQ, GQ * SSM_STATE
    return pl.pallas_call(
        body, name="s5_bwd", grid=(nq,),
        in_specs=[col, col, a_spec, wide, wide, tall, tall, d_spec],
        out_specs=[col, wide, wide, tall, tall, pl.BlockSpec((None, 2, ns), lambda q: (q, 0, 0)), d_spec],
        out_shape=[jax.ShapeDtypeStruct((T, D_SSM), F32),
                   jax.ShapeDtypeStruct((nq, 128, ns), F32), jax.ShapeDtypeStruct((nq, 128, ns), F32),
                   jax.ShapeDtypeStruct((nq, ns, 128), F32), jax.ShapeDtypeStruct((nq, ns, 128), F32),
                   jax.ShapeDtypeStruct((nq, 2, ns), F32), jax.ShapeDtypeStruct((1, D_SSM), F32)],
        scratch_shapes=[pltpu.VMEM((T, ns), F32)] * 4,
        compiler_params=_params(1, 56 << 20),
    )(_in_hbm(u), _in_hbm(dy), a8, bre, bim, cre, cim, dsk)


def _to_segments(a):
    return a.reshape(N_SEG, SEG, a.shape[-1]).transpose(1, 0, 2).reshape(T, a.shape[-1])


def _from_segments(a):
    return a.reshape(SEG, N_SEG, a.shape[-1]).transpose(1, 0, 2).reshape(T, a.shape[-1])


def _block_diag_b(bbt):
    nq = N_GROUPS // GQ
    x = bbt.transpose(1, 0, 2).reshape(nq, GQ, SSM_GROUP, 1, SSM_STATE)
    eye = jnp.eye(GQ, dtype=F32)[None, :, None, :, None]
    return (x * eye).reshape(nq, GQ * SSM_GROUP, GQ * SSM_STATE)


def _block_diag_c(c):
    nq = N_GROUPS // GQ
    x = c.reshape(nq, GQ, SSM_GROUP, SSM_STATE).transpose(0, 1, 3, 2)[:, :, :, None, :]
    eye = jnp.eye(GQ, dtype=F32)[None, :, None, :, None]
    return (x * eye).reshape(nq, GQ * SSM_STATE, GQ * SSM_GROUP)


def _ff_specs():
    up_spec = pl.BlockSpec((2, None, T, 128), lambda p, cb: (0, p, 0, cb))
    piece = pl.BlockSpec((None, T, 128), lambda p, cb: (p, 0, cb))
    cw_spec = pl.BlockSpec((None, 3, 128), lambda p, cb: (p, 0, cb))
    cb_spec = pl.BlockSpec((None, 1, 128), lambda p, cb: (p, 0, cb))
    return up_spec, piece, cw_spec, cb_spec


def _conv_gate(gate, w, cb, pad_ref):
    pad_ref[0:8, :] = jnp.zeros((8, 128), F32)
    pad_ref[8:8 + T, :] = gate
    g1, g2 = pad_ref[pl.ds(7, T), :], pad_ref[pl.ds(6, T), :]
    return w[0:1, :] * g2 + w[1:2, :] * g1 + w[2:3, :] * gate + cb, g1, g2


def _ff_act(up, cw4, cb4):
    def body(up_ref, w_ref, b_ref, o_ref, pad_ref):
        gc, _, _ = _conv_gate(up_ref[0].astype(F32), w_ref[...], b_ref[...], pad_ref)
        o_ref[...] = (jax.nn.silu(gc) * up_ref[1].astype(F32)).astype(o_ref.dtype)

    up_spec, piece, cw_spec, cb_spec = _ff_specs()
    n_cb = pl.cdiv(FF_PIECE, 128)
    return pl.pallas_call(
        body, name="ff_act", grid=(4, n_cb), in_specs=[up_spec, cw_spec, cb_spec], out_specs=piece,
        out_shape=jax.ShapeDtypeStruct((4, T, FF_PIECE), BF16),
        scratch_shapes=[pltpu.VMEM((T + 8, 128), F32)],
        compiler_params=_params(2, 24 << 20),
    )(_in_hbm(up.reshape(2, 4, T, FF_PIECE)), cw4, cb4)


def _ff_act_bwd(up, dact, cw4, cb4):
    def body(up_ref, da_ref, w_ref, b_ref, dup_ref, dw_ref, db_ref, pad_ref, pad2_ref):
        gate, val, w = up_ref[0].astype(F32), up_ref[1].astype(F32), w_ref[...]
        gc, g1, g2 = _conv_gate(gate, w, b_ref[...], pad_ref)
        sg = jax.nn.sigmoid(gc)
        da = da_ref[...].astype(F32)
        dup_ref[1] = (da * gc * sg).astype(dup_ref.dtype)
        dgc = da * val * sg * (1.0 + gc * (1.0 - sg))
        db_ref[...] = jnp.sum(dgc, axis=0, keepdims=True)
        dw_ref[0:1, :] = jnp.sum(dgc * g2, axis=0, keepdims=True)
        dw_ref[1:2, :] = jnp.sum(dgc * g1, axis=0, keepdims=True)
        dw_ref[2:3, :] = jnp.sum(dgc * gate, axis=0, keepdims=True)
        pad2_ref[0:T, :] = dgc
        pad2_ref[T:T + 8, :] = jnp.zeros((8, 128), F32)
        d1, d2 = pad2_ref[pl.ds(1, T), :], pad2_ref[pl.ds(2, T), :]
        dup_ref[0] = (w[2:3, :] * dgc + w[1:2, :] * d1 + w[0:1, :] * d2).astype(dup_ref.dtype)

    up_spec, piece, cw_spec, cb_spec = _ff_specs()
    n_cb = pl.cdiv(FF_PIECE, 128)
    dup, dw, db = pl.pallas_call(
        body, name="ff_act_bwd", grid=(4, n_cb), in_specs=[up_spec, piece, cw_spec, cb_spec],
        out_specs=[up_spec, cw_spec, cb_spec],
        out_shape=[jax.ShapeDtypeStruct((2, 4, T, FF_PIECE), BF16), jax.ShapeDtypeStruct((4, 3, FF_PIECE), F32),
                   jax.ShapeDtypeStruct((4, 1, FF_PIECE), F32)],
        scratch_shapes=[pltpu.VMEM((T + 8, 128), F32)] * 2,
        compiler_params=_params(2, 32 << 20),
    )(_in_hbm(up.reshape(2, 4, T, FF_PIECE)), _in_hbm(dact), cw4, cb4)
    return dup.reshape(8, T, FF_PIECE), dw, db


def _local_step(x, tgt, meta, p, wf, late_weights, send_grads, send_small):
    sm = {}
    h0 = jnp.concatenate([meta, x], axis=0)
    tgt_pad = jnp.concatenate([jnp.zeros((N_META, D), F32), tgt], axis=0)
    tabs = _rope_tables()
    row = lambda a: a.reshape(1, -1)

    xn, r0 = _rowwise("rms_mix", _rms_fwd_fn, [(h0, D, 0)], [row(p['mix_norm'])], [(D, BF16), (1, F32)], [])
    z = _mm("in_proj", xn, wf['w_in'], 'nn', TR, 640, D, F32)
    lam_re, lam_im = p['lam_re'].reshape(N_GROUPS, SSM_STATE), p['lam_im'].reshape(N_GROUPS, SSM_STATE)
    log_dt = p['log_dt'].reshape(1, N_GROUPS)
    bt_re = p['b_re'].reshape(N_GROUPS, SSM_STATE, SSM_GROUP).transpose(2, 0, 1)
    bt_im = p['b_im'].reshape(N_GROUPS, SSM_STATE, SSM_GROUP).transpose(2, 0, 1)
    c_re = p['c_re'].reshape(N_GROUPS, SSM_GROUP, SSM_STATE)
    c_im = p['c_im'].reshape(N_GROUPS, SSM_GROUP, SSM_STATE)
    a_re, a_im, bbt_re, bbt_im = _s5_prep(lam_re, lam_im, log_dt, bt_re, bt_im)
    nq, ns = N_GROUPS // GQ, GQ * SSM_STATE
    a8 = jnp.broadcast_to(jnp.stack([a_re.reshape(nq, ns), a_im.reshape(nq, ns)], 1)[:, :, None, :],
                          (nq, 2, N_SEG, ns))
    bre, bim = _block_diag_b(bbt_re), _block_diag_b(bbt_im)
    cre, cim = _block_diag_c(c_re), _block_diag_c(c_im)
    dsk = row(p['d_skip'])
    u_seg = _to_segments(z[:, :D_SSM])
    y_ssm = _from_segments(_s5_fwd(u_seg, a8, bre, bim, cre, cim, dsk))
    g, = _rowwise("gelu", _gelu_fn, [(y_ssm, D_SSM, 0)], [], [(D_SSM, BF16)], [])
    gl = _mm("glu_proj", g, wf['w_glu'], 'nn', TR, D_SSM, D_SSM, F32)
    yan, ra = _rowwise("glu_norm", _glu_fwd_fn, [(y_ssm, D_SSM, 0), (gl, D_SSM, 0)],
                       [row(p['b_glu']), row(p['out_norm_ssm'])], [(D_SSM, BF16), (1, F32)], [])
    qn, rq = _rowwise("rms_q", _rms_fwd_fn, [(z, Q_LORA, D_SSM // Q_LORA)], [row(p['q_a_norm'])],
                      [(Q_LORA, BF16), (1, F32)], [])
    kvn, rkv = _rowwise("rms_kv", _rms_fwd_fn, [(z, KV_LORA, (D_SSM + Q_LORA) // KV_LORA)], [row(p['kv_a_norm'])],
                        [(KV_LORA, BF16), (1, F32)], [])
    q = _mm("q_proj", qn, wf['w_q_b'], 'nn', T, HEAD_PAD, Q_LORA, F32, stack='b_out')
    kv = _mm("kv_proj", kvn, wf['w_kv_b'], 'nn', T, HEAD_PAD, KV_LORA, F32, stack='b_out')
    yb = _attn_fwd(q, kv, z, tabs)
    ybn, rb = _rowwise("rms_attn", _rms_fwd_fn, [(yb, D_SSM, 0)], [row(p['out_norm_attn'])],
                       [(D_SSM, BF16), (1, F32)], [])
    y = jnp.concatenate([yan, ybn], axis=1)
    wf = dict(wf)
    wf['w_out'], wf['w_up'], wf['w_down'] = late_weights(y)
    h1 = _mm("out_proj", y, wf['w_out'], 'nn', TR, 1024, D, F32, res=h0)
    xn2, r1 = _rowwise("rms_ffn", _rms_fwd_fn, [(h1, D, 0)], [row(p['ffn_norm'])], [(D, BF16), (1, F32)], [])
    up = _mm("up_proj", xn2, wf['w_up'], 'nn', TR, FF_PIECE, D, BF16, stack='b_out')
    cw4 = wf['conv_w'].reshape(3, 4, FF_PIECE).transpose(1, 0, 2)
    cb4 = p['conv_b'].reshape(4, 1, FF_PIECE)
    act = _ff_act(up, cw4, cb4)
    wd4 = wf['w_down'].reshape(4, FF_PIECE, D)
    h2 = _mm("down_proj", act, wd4, 'nn', TR, 1024, FF_PIECE, F32, stack='ab_red', res=h1)
    dh2, sm['final_norm'], loss = _rowwise("loss", _loss_fn, [(h2, D, 0), (tgt_pad, D, 0)], [row(p['final_norm'])],
                                           [(D, F32)], [D, 128])

    big = {}
    dact = _mm("down_bwd_x", dh2, wd4, 'nt', TR, FF_PIECE, D, BF16, stack='b_out')
    g_down = _mm("down_bwd_w", act, dh2, 'tn', FF_PIECE, 512, T, BF16, stack='a_out').reshape(
        N_DEV, D_FF // N_DEV, D)
    dup, dcw4, sm['conv_b'] = _ff_act_bwd(up, dact, cw4, cb4)
    dxn2 = _mm("up_bwd_x", dup, wf['w_up'], 'nt', TR, 1024, FF_PIECE, F32, stack='ab_red')
    g_up = _mm("up_bwd_w", dup, xn2, 'tn', FF_PIECE, 1024, T, BF16, stack='a_out')
    token = send_grads(['w_down', 'w_up'], [g_down, g_up])
    dh1, sm['ffn_norm'] = _rowwise("rms_ffn_bwd", _rms_bwd_res_fn,
                                   [(dxn2, D, 0), (h1, D, 0), (r1, 1, 0), (dh2, D, 0)],
                                   [row(p['ffn_norm']) + token[0:1, 0:1]], [(D, F32)], [D])
    dy = _mm("out_bwd_x", dh1, wf['w_out'], 'nt', TR, 1024, D, F32)
    g_out = _mm("out_bwd_w", y, dh1, 'tn', 1024, 1024, T, BF16).reshape(N_DEV, D // N_DEV, D)
    dgl, dgd, sm['out_norm_ssm'], sm['b_glu'] = _rowwise(
        "glu_bwd", _glu_bwd_fn, [(dy, D_SSM, 0), (y_ssm, D_SSM, 0), (gl, D_SSM, 0), (ra, 1, 0)],
        [row(p['b_glu']), row(p['out_norm_ssm'])], [(D_SSM, BF16), (D_SSM, F32)], [D_SSM, D_SSM])
    dg = _mm("glu_bwd_x", dgl, wf['w_glu'], 'nt', TR, D_SSM, D_SSM, F32, res=dgd)
    g_glu = _mm("glu_bwd_w", g, dgl, 'tn', D_SSM, D_SSM, T, BF16).reshape(N_DEV, D_SSM // N_DEV, D_SSM)
    dy_ssm, = _rowwise("gelu_bwd", _gelu_bwd_fn, [(dg, D_SSM, 0), (y_ssm, D_SSM, 0)], [], [(D_SSM, F32)], [])
    dyb, sm['out_norm_attn'] = _rowwise("rms_attn_bwd", _rms_bwd_fn, [(dy, D_SSM, 1), (yb, D_SSM, 0), (rb, 1, 0)],
                                        [row(p['out_norm_attn'])], [(D_SSM, F32)], [D_SSM])
    dq, dkv, dkpe = _attn_bwd(q, kv, z, tabs, dyb)
    g_q = _mm("q_bwd_w", qn, dq, 'tn', Q_LORA, HEAD_PAD, T, BF16, stack='b_out')
    dqn = _mm("q_bwd_x", dq, wf['w_q_b'], 'nt', TR, Q_LORA, HEAD_PAD, F32, stack='ab_red')
    g_kv = _mm("kv_bwd_w", kvn, dkv, 'tn', KV_LORA, HEAD_PAD, T, BF16, stack='b_out')
    dkvn = _mm("kv_bwd_x", dkv, wf['w_kv_b'], 'nt', TR, KV_LORA, HEAD_PAD, F32, stack='ab_red')
    token = send_grads(['w_out', 'w_glu', 'w_q_b', 'w_kv_b'], [g_out, g_glu, g_q, g_kv])
    dq_a, sm['q_a_norm'] = _rowwise("rms_q_bwd", _rms_bwd_fn,
                                    [(dqn, Q_LORA, 0), (z, Q_LORA, D_SSM // Q_LORA), (rq, 1, 0)],
                                    [row(p['q_a_norm'])], [(Q_LORA, BF16)], [Q_LORA])
    dkv_a, sm['kv_a_norm'] = _rowwise("rms_kv_bwd", _rms_bwd_fn,
                                      [(dkvn, KV_LORA, 0), (z, KV_LORA, (D_SSM + Q_LORA) // KV_LORA), (rkv, 1, 0)],
                                      [row(p['kv_a_norm'])], [(KV_LORA, BF16)], [KV_LORA])
    dk_pe = _kpe_bwd(dkpe, tabs)
    du_seg, dbre, dbim, dcre, dcim, da, sm['d_skip'] = _s5_bwd(
        u_seg, _to_segments(dy_ssm), a8, bre, bim, cre, cim, dsk + token[0:1, 0:1])
    du = _from_segments(du_seg).astype(BF16)
    diag_b = lambda d: jnp.einsum('qgcgp->qgcp', d.reshape(nq, GQ, SSM_GROUP, GQ, SSM_STATE)).transpose(
        2, 0, 1, 3).reshape(SSM_GROUP, N_GROUPS, SSM_STATE)
    diag_c = lambda d: jnp.einsum('qgpgc->qgcp', d.reshape(nq, GQ, SSM_STATE, GQ, SSM_GROUP)).reshape(
        N_GROUPS * SSM_GROUP, SSM_STATE)
    dlam_re, dlam_im, dlog_dt, dbt_re, dbt_im = _s5_prep_bwd(
        lam_re, lam_im, log_dt, bt_re, bt_im, da[:, 0, :].reshape(N_GROUPS, SSM_STATE),
        da[:, 1, :].reshape(N_GROUPS, SSM_STATE), diag_b(dbre), diag_b(dbim))
    sm['lam_re'], sm['lam_im'], sm['log_dt'] = dlam_re, dlam_im, dlog_dt
    cgp = lambda d: d.reshape(SSM_GROUP * N_GROUPS, SSM_STATE)
    sm['b_re'], sm['b_im'] = cgp(dbt_re), cgp(dbt_im)
    sm['c_re'], sm['c_im'] = diag_c(dcre), diag_c(dcim)
    token = send_small(sm, dcw4, loss)
    dz = jnp.concatenate([du, dq_a, dkv_a, dk_pe], axis=1)
    dxn = _mm("in_bwd_x", dz, wf['w_in'], 'nt', TR, 1024, D_IN_PAD, F32)
    big['w_in'] = _mm("in_bwd_w", xn, dz, 'tn', 1024, 640, T, BF16).reshape(N_DEV, D // N_DEV, D_IN_PAD)
    dh0, d_mix_norm = _rowwise("rms_mix_bwd", _rms_bwd_res_fn,
                               [(dxn, D, 0), (h0, D, 0), (r0, 1, 0), (dh1, D, 0)],
                               [row(p['mix_norm']) + token[0:1, 0:1]], [(D, F32)], [D])
    return dh0[N_META:], dh0[:N_META], d_mix_norm, big


def _place():
    x, y, c = lax.axis_index("x"), lax.axis_index("y"), lax.axis_index("c")
    return x, y, c, [(1 - x, y), (x, 1 - y), (1 - x, 1 - y)]


_HBM = pl.BlockSpec(memory_space=pltpu.HBM)


def _all_gather(name, arrs, chip_blocks=()):
    n, nc = len(arrs), len(chip_blocks)

    def body(*refs):
        ins, cins, outs, couts = refs[:n], refs[n:n + nc], refs[n + nc:2 * n + nc], refs[2 * n + nc:2 * (n + nc)]
        send, recv, loc, csend, crecv = refs[2 * (n + nc):]
        x, y, c, chips = _place()
        me, sib = (x, y, c), (x, y, 1 - c)

        def cp(a, k, blk, to, src=None):
            dst = outs[a].at[4 * blk[0] + 2 * blk[1] + blk[2]]
            return pltpu.make_async_remote_copy(src_ref=dst if src is None else src, dst_ref=dst,
                                                send_sem=send.at[a, k], recv_sem=recv.at[a, k],
                                                device_id=to, device_id_type=MESH)

        between_chips = [pltpu.make_async_remote_copy(src_ref=cins[a].at[2 * ch[0] + ch[1]], dst_ref=couts[a].at[j],
                                                      send_sem=csend.at[a, j], recv_sem=crecv.at[a, j],
                                                      device_id=(*ch, c), device_id_type=MESH)
                         for a in range(nc) for j, ch in enumerate(chips)]
        mine = [pltpu.make_async_copy(ins[a], outs[a].at[4 * x + 2 * y + c], loc.at[a]) for a in range(n)]
        first = []
        for a in range(n):
            first.append(cp(a, 0, me, sib, src=ins[a]))
            first += [cp(a, 1 + j, me, (*ch, c), src=ins[a]) for j, ch in enumerate(chips)]
        for f in mine + first + between_chips:
            f.start()
        passed = []
        for a in range(n):
            for j, ch in enumerate(chips):
                cp(a, 1 + j, (*ch, c), me).wait_recv()
                passed.append(cp(a, 4 + j, (*ch, c), sib))
                passed[-1].start()
        for a in range(n):
            cp(a, 0, sib, me).wait_recv()
            for j, ch in enumerate(chips):
                cp(a, 4 + j, (*ch, 1 - c), me).wait_recv()
        for f in first + passed:
            f.wait_send()
        for m in mine + between_chips:
            m.wait()

    out = pl.pallas_call(
        body, name=name, in_specs=[_HBM] * (n + nc), out_specs=[_HBM] * (n + nc),
        out_shape=[jax.ShapeDtypeStruct((N_DEV,) + a.shape, a.dtype) for a in arrs]
        + [jax.ShapeDtypeStruct((3,) + a.shape[1:], a.dtype) for a in chip_blocks],
        scratch_shapes=[pltpu.SemaphoreType.DMA((n, 7)), pltpu.SemaphoreType.DMA((n, 7)),
                        pltpu.SemaphoreType.DMA((n,)), pltpu.SemaphoreType.DMA((max(nc, 1), 3)),
                        pltpu.SemaphoreType.DMA((max(nc, 1), 3))],
    )(*arrs, *chip_blocks)
    return (out[:n], out[n:]) if nc else out


def _exchange_cores(name, arrs):
    n = len(arrs)

    def body(*refs):
        ins, outs = refs[:n], refs[n:2 * n]
        send, recv = refs[2 * n:]
        x, y, c, _ = _place()
        remote = [pltpu.make_async_remote_copy(src_ref=ins[a].at[2 * k + 1 - c], dst_ref=outs[a].at[k],
                                               send_sem=send.at[a, k], recv_sem=recv.at[a, k],
                                               device_id=(x, y, 1 - c), device_id_type=MESH)
                  for a in range(n) for k in range(4)]
        for d in remote:
            d.start()
        for d in remote:
            d.wait()

    return pl.pallas_call(
        body, name=name, in_specs=[_HBM] * n, out_specs=[_HBM] * n,
        out_shape=[jax.ShapeDtypeStruct((4,) + a.shape[1:], a.dtype) for a in arrs],
        scratch_shapes=[pltpu.SemaphoreType.DMA((n, 4))] * 2,
    )(*arrs)


_SEM = pl.BlockSpec(memory_space=pltpu.SEMAPHORE)
_EFFECT = pltpu.SideEffectType.DATAFLOW_SIDE_EFFECTING


def _split_exchange(name, srcs, land_shapes, copies_of, n_cp, after, all_of=None):
    n = len(srcs)
    lands = [pltpu.with_memory_space_constraint(lax.empty(s, a.dtype), pltpu.HBM) for s, a in zip(land_shapes, srcs)]

    per = 1 if all_of else n_cp
    n_sem = n * per

    def descriptors(src_refs, land_refs, send, recv, mirror):
        x, y, c, chips = _place()
        out = []
        for a in range(n):
            if mirror and all_of:
                whole = all_of(land_refs[a])
                out.append(pltpu.make_async_remote_copy(src_ref=whole, dst_ref=whole, send_sem=send[a],
                                                        recv_sem=recv[a], device_id=(x, y, 1 - c),
                                                        device_id_type=MESH))
                continue
            for k, (src, dst, to, back) in enumerate(copies_of(a, src_refs[a], land_refs[a], x, y, c, chips)):
                out.append(pltpu.make_async_remote_copy(src_ref=src, dst_ref=back if mirror else dst,
                                                        send_sem=send[a * per + k % per],
                                                        recv_sem=recv[a * per + k % per],
                                                        device_id=to, device_id_type=MESH))
        return out

    def start_body(*refs):
        sems = refs[2 * n + 1:2 * n + 1 + 2 * n_sem]
        for d in descriptors(refs[:n], refs[n:2 * n], sems[:n_sem], sems[n_sem:], False):
            d.start()
        refs[-1][...] = jnp.zeros((8, 128), F32)

    hbm_like = lambda arrs: [pltpu.HBM(a.shape, a.dtype) for a in arrs]
    res = pl.pallas_call(
        start_body, name=name + "_start", in_specs=[_HBM] * (2 * n) + [pl.BlockSpec(memory_space=pl.ANY)],
        out_specs=[_SEM] * (2 * n_sem) + [_HBM] * (2 * n) + [pl.BlockSpec(memory_space=pltpu.VMEM)],
        out_shape=[pltpu.SemaphoreType.DMA(())] * (2 * n_sem)
        + hbm_like(srcs) + hbm_like(lands) + [jax.ShapeDtypeStruct((8, 128), F32)],
        input_output_aliases={i: 2 * n_sem + i for i in range(2 * n)},
        compiler_params=pltpu.CompilerParams(has_side_effects=_EFFECT),
    )(*[pltpu.with_memory_space_constraint(a, pltpu.HBM) for a in srcs], *lands, after)
    sems, thru, token = res[:2 * n_sem], res[2 * n_sem:2 * n_sem + 2 * n], res[-1]

    def wait(after):
        def wait_body(*refs):
            s = refs[2 * n:2 * n + 2 * n_sem]
            for d in descriptors(refs[:n], refs[n:2 * n], s[:n_sem], s[n_sem:], True):
                d.wait_send()
                d.wait_recv()

        out = pl.pallas_call(
            wait_body, name=name + "_wait",
            in_specs=[_HBM] * (2 * n) + [_SEM] * (2 * n_sem) + [pl.BlockSpec(memory_space=pl.ANY)],
            out_specs=[_HBM] * (2 * n), out_shape=hbm_like(srcs) + hbm_like(lands),
            input_output_aliases={i: i for i in range(2 * n)},
            compiler_params=pltpu.CompilerParams(has_side_effects=_EFFECT),
        )(*thru, *sems, after)
        return out[:n], out[n:]

    return token, wait


def _gather_copies(a, src, land, x, y, c, chips):
    peers = [(x, y, 1 - c)] + [(*ch, c) for ch in chips]
    me = 4 * x + 2 * y + c
    return [(src, land.at[me], to, land.at[4 * to[0] + 2 * to[1] + to[2]]) for to in peers]


def _to_all_copies(a, src, land, x, y, c, chips):
    peers = [(x, y, 1 - c)] + [(*ch, cc) for ch in chips for cc in (c, 1 - c)]
    me = 4 * x + 2 * y + c
    return [(src, land.at[me], to, land.at[4 * to[0] + 2 * to[1] + to[2]]) for to in peers]


def _chips_copies(a, src, land, x, y, c, chips):
    return [(src.at[2 * ch[0] + ch[1]], land.at[j], (*ch, c), land.at[j]) for j, ch in enumerate(chips)]


def _gather_forward(name, arrs):
    n = len(arrs)

    def body(*refs):
        ins, outs = refs[:n], refs[n:2 * n]
        send, recv = refs[2 * n:]
        x, y, c, chips = _place()
        sends, recvs = [], []
        for a in range(n):
            for j, ch in enumerate(chips):
                mine, theirs = 4 * ch[0] + 2 * ch[1] + c, 4 * ch[0] + 2 * ch[1] + 1 - c
                sends.append(pltpu.make_async_remote_copy(src_ref=ins[a].at[mine], dst_ref=outs[a].at[mine],
                                                          send_sem=send.at[a, j], recv_sem=recv.at[a, j],
                                                          device_id=(x, y, 1 - c), device_id_type=MESH))
                recvs.append(pltpu.make_async_remote_copy(src_ref=ins[a].at[theirs], dst_ref=outs[a].at[theirs],
                                                          send_sem=send.at[a, j], recv_sem=recv.at[a, j],
                                                          device_id=(x, y, 1 - c), device_id_type=MESH))
        for d in sends:
            d.start()
        for s, r in zip(sends, recvs):
            s.wait_send()
            r.wait_recv()

    return pl.pallas_call(
        body, name=name, in_specs=[_HBM] * n, out_specs=[_HBM] * n,
        out_shape=[jax.ShapeDtypeStruct(a.shape, a.dtype) for a in arrs],
        input_output_aliases={i: i for i in range(n)},
        scratch_shapes=[pltpu.SemaphoreType.DMA((n, 3))] * 2,
    )(*arrs)


def _blocks_of(r, c):
    if r * c * 4 <= (2 << 20):
        return r, c
    if r % 128 == 0:
        return 128, c
    return r, 256


def _pair_sum(name, own, got, core):
    _, r, c = got.shape
    rb, cb = _blocks_of(r, c)

    def body(s_ref, a_ref, b_ref, o_ref):
        o_ref[...] = (a_ref[...].astype(F32) + b_ref[...].astype(F32)).astype(o_ref.dtype)

    return pl.pallas_call(
        body, name=name, out_shape=jax.ShapeDtypeStruct((4, r, c), got.dtype),
        grid_spec=pltpu.PrefetchScalarGridSpec(
            num_scalar_prefetch=1, grid=(4, r // rb, c // cb),
            in_specs=[pl.BlockSpec((None, rb, cb), lambda k, i, j, s: (2 * k + s[0], i, j)),
                      pl.BlockSpec((None, rb, cb), lambda k, i, j, s: (k, i, j))],
            out_specs=pl.BlockSpec((None, rb, cb), lambda k, i, j, s: (k, i, j))),
        compiler_params=_params(3, 24 << 20),
    )(core, _in_hbm(own), _in_hbm(got))


def _adamw_math(w, g, m, v):
    m = B1 * m + (1.0 - B1) * g
    v = B2 * v + (1.0 - B2) * (g * g)
    m_hat = m / (1.0 - B1 ** STEP)
    v_hat = v / (1.0 - B2 ** STEP)
    return -LR * (m_hat / (jnp.sqrt(v_hat) + ADAM_EPS) + WD * w), m, v


def _adamw_big(name, own, got, chip, w, m, v):
    r, c = w.shape
    rb, cb = _blocks_of(r, c)

    def body(s_ref, o_ref, p_ref, w_ref, m_ref, v_ref, g_ref, d_ref, nm_ref, nv_ref):
        g = o_ref[...].astype(F32)
        for k in range(3):
            g = g + p_ref[k].astype(F32)
        g_ref[...] = g
        d_ref[...], nm_ref[...], nv_ref[...] = _adamw_math(w_ref[...], g, m_ref[...], v_ref[...])

    blk = pl.BlockSpec((rb, cb), lambda i, j, s: (i, j))
    return pl.pallas_call(
        body, name=name, out_shape=[jax.ShapeDtypeStruct((r, c), F32)] * 4,
        grid_spec=pltpu.PrefetchScalarGridSpec(
            num_scalar_prefetch=1, grid=(r // rb, c // cb),
            in_specs=[pl.BlockSpec((None, rb, cb), lambda i, j, s: (s[0], i, j)),
                      pl.BlockSpec((3, rb, cb), lambda i, j, s: (0, i, j)), blk, blk, blk],
            out_specs=[blk] * 4),
        compiler_params=_params(2, 40 << 20),
    )(chip, *map(_in_hbm, (own, got, w, m, v)))


def _adamw_multi(name, items, sums=(), row_blocks=1, own=(), dev=None):
    n, ns = len(items), len(sums)
    n_own = len(own)
    assert n_own in (0, n + ns)

    def body(*refs):
        dev_ref = refs[0] if n_own else None
        refs = refs[1:] if n_own else refs
        owns, refs = refs[:n_own], refs[n_own:]
        ins, outs = refs[:4 * n + ns], refs[4 * n + ns:]

        def total(ref, t):
            g = None
            for d in range(ref.shape[0]):
                part = jnp.where(dev_ref[0] == d, owns[t][...], ref[d]) if n_own else ref[d]
                g = part if g is None else g + part
            return g

        for t in range(n):
            p_ref, w_ref, m_ref, v_ref = ins[4 * t:4 * t + 4]
            g = total(p_ref, t)
            outs[4 * t][...] = g
            outs[4 * t + 1][...], outs[4 * t + 2][...], outs[4 * t + 3][...] = _adamw_math(
                w_ref[...], g, m_ref[...], v_ref[...])
        for t in range(ns):
            outs[4 * n + t][...] = total(ins[4 * n + t], n + t)

    def spec(shape, lead):
        blk = (shape[0] // row_blocks,) + tuple(shape[1:])
        nd = len(shape)
        if lead:
            return pl.BlockSpec((lead,) + blk, lambda i: (0, i) + (0,) * (nd - 1))
        return pl.BlockSpec(blk, lambda i: (i,) + (0,) * (nd - 1))

    operands, in_specs, out_specs, out_shape = [], [], [], []
    if n_own:
        operands += [dev] + list(own)
        in_specs += [pl.BlockSpec(memory_space=pltpu.SMEM)] + [spec(o.shape, 0) for o in own]
    for parts, w, m, v in items:
        assert parts.shape[1:] == w.shape == m.shape == v.shape, (name, parts.shape, w.shape)
        operands += [parts, w, m, v]
        in_specs += [spec(w.shape, parts.shape[0])] + [spec(w.shape, 0)] * 3
        out_specs += [spec(w.shape, 0)] * 4
        out_shape += [jax.ShapeDtypeStruct(w.shape, F32)] * 4
    for parts in sums:
        operands.append(parts)
        in_specs.append(spec(parts.shape[1:], parts.shape[0]))
        out_specs.append(spec(parts.shape[1:], False))
        out_shape.append(jax.ShapeDtypeStruct(parts.shape[1:], F32))
    return pl.pallas_call(
        body, name=name, grid=(row_blocks,), in_specs=in_specs, out_specs=out_specs, out_shape=out_shape,
        compiler_params=_params(1, 56 << 20),
    )(*operands)


def kernel(x, meta_tokens, mix_norm, w_in, lam_re, lam_im, log_dt, b_re, b_im, c_re, c_im, d_skip, w_glu, b_glu, q_a_norm, w_q_b, kv_a_norm, w_kv_b, out_norm_ssm, out_norm_attn, w_out, ffn_norm, w_up, conv_w, conv_b, w_down, final_norm, loss_target, m_meta_tokens, m_mix_norm, m_w_in, m_lam_re, m_lam_im, m_log_dt, m_b_re, m_b_im, m_c_re, m_c_im, m_d_skip, m_w_glu, m_b_glu, m_q_a_norm, m_w_q_b, m_kv_a_norm, m_w_kv_b, m_out_norm_ssm, m_out_norm_attn, m_w_out, m_ffn_norm, m_w_up, m_conv_w, m_conv_b, m_w_down, m_final_norm, v_meta_tokens, v_mix_norm, v_w_in, v_lam_re, v_lam_im, v_log_dt, v_b_re, v_b_im, v_c_re, v_c_im, v_d_skip, v_w_glu, v_b_glu, v_q_a_norm, v_w_q_b, v_kv_a_norm, v_w_kv_b, v_out_norm_ssm, v_out_norm_attn, v_w_out, v_ffn_norm, v_w_up, v_conv_w, v_conv_b, v_w_down, v_final_norm):
    given = dict(locals())
    w = {n: given[n] for n in WEIGHTS}
    m = {n: given['m_' + n] for n in WEIGHTS}
    v = {n: given['v_' + n] for n in WEIGHTS}
    dev = 4 * lax.axis_index("x") + 2 * lax.axis_index("y") + lax.axis_index("c")

    shard = {
        'w_in': jnp.pad(w_in[0], ((0, 0), (0, D_IN_PAD - D_IN))),
        'w_glu': w_glu[0],
        'w_q_b': jnp.pad(w_q_b[0], ((0, 0), (0, HEAD_PAD - QK_NOPE - QK_ROPE))),
        'w_kv_b': w_kv_b[0],
        'w_out': w_out[0],
        'w_up': w_up[0],
        'w_down': w_down[0],
    }
    core = lax.axis_index("c").astype(jnp.int32).reshape(1)
    chip = (2 * lax.axis_index("x") + lax.axis_index("y")).astype(jnp.int32).reshape(1)
    early, late = ['w_in', 'w_glu', 'w_q_b', 'w_kv_b'], ['w_out', 'w_up', 'w_down']
    shard = {n: a.astype(BF16) for n, a in shard.items()}

    gathered = _all_gather("gather_early", [shard[n] for n in early]
                           + [meta_tokens, jnp.pad(conv_w[0], ((0, 5), (0, 0)))])
    wf = dict(zip(early, gathered[:len(early)]))
    for n in ('w_in', 'w_glu'):
        wf[n] = wf[n].reshape(-1, wf[n].shape[-1])
    meta = gathered[-2].transpose(1, 0, 2).reshape(N_META, D)
    wf['conv_w'] = gathered[-1][:, :3].transpose(1, 0, 2).reshape(3, D_FF)
    gather_token, wait_late = _split_exchange("gather_late", [shard[n] for n in late],
                                              [(N_DEV,) + shard[n].shape for n in late], _gather_copies, 4,
                                              after=gathered[0])

    def late_weights(after):
        mine, landed = wait_late(after)
        full = _gather_forward("gather_late_forward",
                               [lax.dynamic_update_slice(a, s[None], (dev, 0, 0)) for s, a in zip(mine, landed)])
        return full[0].reshape(-1, D), full[1], full[2].reshape(-1, D)

    pending = []

    def send_grads(names, arrs):
        got = _exchange_cores("reduce_cores_" + names[0], arrs)
        sums = [_pair_sum("sum_cores_" + n, a, g, core) for n, a, g in zip(names, arrs, got)]
        token, wait = _split_exchange("reduce_chips_" + names[0], sums, [(3,) + s.shape[1:] for s in sums],
                                      _chips_copies, 3, after=sums[-1])
        pending.append((names, wait))
        return token

    small_sent = []

    def send_small(sm, dcw4, loss_part):
        names = [n for n in SMALL if n in sm]
        arrs = [sm[n] for n in names] + [dcw4, loss_part]
        token, wait = _split_exchange("gather_small", arrs, [(N_DEV,) + a.shape for a in arrs], _to_all_copies, 7,
                                      after=arrs[-1], all_of=lambda land: land.at[pl.ds(0, N_DEV - 1)])
        small_sent.append((names, wait))
        return token

    small = {n: w[n] for n in SMALL}
    small['mix_norm'] = mix_norm + gather_token[0:1, 0:1]
    grad_x, dmeta, d_mix_norm, big = _local_step(x[0], loss_target[0], meta, small, wf, late_weights, send_grads,
                                                  send_small)
    grads, deltas, new_m, new_v = {}, {}, {}, {}

    def keep(n, outs):
        grads[n], deltas[n], new_m[n], new_v[n] = (o.reshape(w[n].shape) for o in outs)

    rest = list(big)
    from_core = _exchange_cores("reduce_cores", [big[n] for n in rest])
    chip_sums = [_pair_sum("sum_cores_" + n, big[n], got, core) for n, got in zip(rest, from_core)]
    (mix_parts, meta_parts), from_chips = _all_gather("reduce_chips_gather_last", [d_mix_norm, dmeta],
                                                      chip_blocks=chip_sums)
    reduced = list(zip(rest, chip_sums, from_chips))
    for names, wait in pending:
        reduced += zip(names, *wait(from_chips[0]))
    for n, own, got in reduced:
        if n == 'w_up':
            outs = _adamw_big("adamw_" + n, own, got, chip, w[n][0].T, m[n][0].T, v[n][0].T)
            keep(n, [o.T for o in outs])
            continue
        cols = w[n].shape[2]
        keep(n, _adamw_big("adamw_" + n, own[:, :, :cols], got[:, :, :cols], chip, w[n][0], m[n][0], v[n][0]))

    outs = _adamw_multi("adamw_last", [(mix_parts, mix_norm, m['mix_norm'], v['mix_norm'])], sums=[meta_parts])
    keep('mix_norm', outs[:4])
    g_meta = outs[4]
    names, wait = small_sent[0]
    mine, landed = wait(grads['w_up'])
    parts, own = dict(zip(names + ['conv_w', 'loss'], landed)), dict(zip(names + ['conv_w', 'loss'], mine))
    wide = [n for n in names if n not in ('b_re', 'b_im')]
    summed = ['conv_w', 'loss', 'b_re', 'b_im']
    outs = _adamw_multi("adamw_small",
                        [(parts[n],) + tuple(t[n].reshape(own[n].shape) for t in (w, m, v)) for n in wide],
                        sums=[parts[n] for n in summed], own=[own[n] for n in wide + summed],
                        dev=dev.astype(jnp.int32).reshape(1))
    for i, n in enumerate(wide):
        keep(n, outs[4 * i:4 * i + 4])
    g_cw4, loss, g_bre, g_bim = outs[-4:]
    gpc = lambda g: g.reshape(SSM_GROUP, N_GROUPS, SSM_STATE).transpose(1, 2, 0).reshape(-1, SSM_GROUP)[None]
    outs = _adamw_multi("adamw_b", [(gpc(g),) + tuple(t[n].reshape(-1, SSM_GROUP) for t in (w, m, v))
                                    for n, g in (('b_re', g_bre), ('b_im', g_bim))], row_blocks=8)
    keep('b_re', outs[:4])
    keep('b_im', outs[4:])

    g_meta = lax.dynamic_slice(g_meta, (0, dev * (D // N_DEV)), (N_META, D // N_DEV))
    g_conv = lax.dynamic_slice(g_cw4.transpose(1, 0, 2).reshape(3, D_FF), (0, dev * (D_FF // N_DEV)),
                               (3, D_FF // N_DEV))
    rows8 = lambda a: jnp.pad(a.reshape(3, D_FF // N_DEV), ((0, 5), (0, 0)))
    outs = _adamw_multi("adamw_cols", [(g_meta[None], meta_tokens, m['meta_tokens'], v['meta_tokens']),
                                       (rows8(g_conv)[None], rows8(conv_w), rows8(m['conv_w']), rows8(v['conv_w']))])
    keep('meta_tokens', outs[:4])
    keep('conv_w', [o[:3] for o in outs[4:]])

    return (loss[0, 0], grad_x[None], *[grads[n] for n in WEIGHTS], *[deltas[n] for n in WEIGHTS],
            *[new_m[n] for n in WEIGHTS], *[new_v[n] for n in WEIGHTS])
```

```python
import functools
import math

import jax
import jax.numpy as jnp
from jax import lax
from jax.experimental import pallas as pl
from jax.experimental.pallas import tpu as pltpu

F32, BF16 = jnp.float32, jnp.bfloat16

N_DEV = 8
N_META, SEQ, D = 16, 2048, 2048
T = N_META + SEQ
TR = 688
EPS = 1e-6
D_SSM, Q_LORA, KV_LORA, QK_ROPE = 1024, 512, 256, 64
D_IN = D_SSM + Q_LORA + KV_LORA + QK_ROPE
D_IN_PAD = 1920
N_HEADS, QK_NOPE, V_HEAD = 8, 128, 128
HEAD_PAD = 256
SM_SCALE = 1.0 / math.sqrt(QK_NOPE + QK_ROPE)
CHUNK = 64
N_GROUPS, SSM_GROUP, SSM_STATE = 64, 16, 64
N_SEG = 8
SEG = T // N_SEG
GQ = 8
D_FF = 5504
FF_PIECE = 1376
ROPE_BASE = 10000.0
LR, B1, B2, ADAM_EPS, WD, STEP = 0.001, 0.9, 0.999, 1e-08, 0.01, 10
VMEM_CAP = 60 * 1024 * 1024
MESH = pl.DeviceIdType.MESH

WEIGHTS = ['meta_tokens', 'mix_norm', 'w_in', 'lam_re', 'lam_im', 'log_dt', 'b_re', 'b_im', 'c_re', 'c_im',
           'd_skip', 'w_glu', 'b_glu', 'q_a_norm', 'w_q_b', 'kv_a_norm', 'w_kv_b', 'out_norm_ssm',
           'out_norm_attn', 'w_out', 'ffn_norm', 'w_up', 'conv_w', 'conv_b', 'w_down', 'final_norm']
BIG = ['w_in', 'w_glu', 'w_q_b', 'w_kv_b', 'w_out', 'w_up', 'w_down']
SMALL = [n for n in WEIGHTS if n not in BIG and n not in ('meta_tokens', 'conv_w')]


def _nbytes(shape, dtype):
    return math.prod(shape) * jnp.dtype(dtype).itemsize


def _in_hbm(a):
    return pltpu.with_memory_space_constraint(a, pltpu.HBM) if a.size * a.dtype.itemsize >= (1 << 20) else a


def _params(sem, need):
    return pltpu.CompilerParams(dimension_semantics=("arbitrary",) * sem,
                                vmem_limit_bytes=int(min(VMEM_CAP, max(need, 16 * 1024 * 1024))))


_DIMS = {'nn': (((1,), (0,)), ((), ())), 'nt': (((1,), (1,)), ((), ())), 'tn': (((0,), (0,)), ((), ()))}


def _mm(name, a, b, mode, tm, tn, tk, out_dtype, stack=None, res=None):
    sa, sb, so = stack in ('a_out', 'ab_red'), stack in ('b_out', 'ab_red'), stack in ('a_out', 'b_out')
    nj = a.shape[0] if sa else (b.shape[0] if sb else 1)
    a2, b2 = a.shape[-2:], b.shape[-2:]
    if mode == 'nn':
        (m, k), (k2, n) = a2, b2
    elif mode == 'nt':
        (m, k), (n, k2) = a2, b2
    else:
        (k, m), (k2, n) = a2, b2
    assert k == k2 and m % tm == 0 and n % tn == 0 and k % tk == 0, (name, a.shape, b.shape)
    n_jo, n_jr, n_k = (nj if so else 1), (nj if stack == 'ab_red' else 1), k // tk
    grid = (n_jo, m // tm, n // tn, n_jr, n_k)
    a_blk, a_idx = ((tk, tm), lambda i, j, kk: (kk, i)) if mode == 'tn' else ((tm, tk), lambda i, j, kk: (i, kk))
    b_blk, b_idx = ((tn, tk), lambda i, j, kk: (j, kk)) if mode == 'nt' else ((tk, tn), lambda i, j, kk: (kk, j))

    def spec(blk, idx, stacked, on_out):
        if not stacked:
            return pl.BlockSpec(blk, lambda jo, i, j, jr, kk: idx(i, j, kk))
        return pl.BlockSpec((None,) + blk, lambda jo, i, j, jr, kk: ((jo if on_out else jr),) + idx(i, j, kk))

    o_idx = lambda i, j, kk: (i, j)
    in_specs = [spec(a_blk, a_idx, sa, stack == 'a_out'), spec(b_blk, b_idx, sb, stack == 'b_out')]
    operands = [a, b]
    if res is not None:
        in_specs.append(spec((tm, tn), o_idx, False, False))
        operands.append(res)
    n_red = n_jr * n_k
    dims = _DIMS[mode]

    def body(*refs):
        a_ref, b_ref = refs[0], refs[1]
        res_ref = refs[2] if res is not None else None
        o_ref = refs[3] if res is not None else refs[2]
        part = lax.dot_general(a_ref[...].astype(BF16), b_ref[...].astype(BF16), dims,
                               preferred_element_type=F32)

        def finish(total):
            if res_ref is not None:
                total = total + res_ref[...]
            o_ref[...] = total.astype(o_ref.dtype)

        if n_red == 1:
            finish(part)
        else:
            acc_ref = refs[-1]
            step = pl.program_id(3) * n_k + pl.program_id(4)

            @pl.when(step == 0)
            def _():
                acc_ref[...] = part

            @pl.when(step > 0)
            def _():
                acc_ref[...] += part

            @pl.when(step == n_red - 1)
            def _():
                finish(acc_ref[...])

    out_shape = ((nj,) if so else ()) + (m, n)
    need = 2 * (_nbytes(a_blk, a.dtype) + _nbytes(b_blk, b.dtype) + _nbytes((tm, tn), out_dtype)
                + (_nbytes((tm, tn), F32) if res is not None else 0)) + 3 * _nbytes((tm, tn), F32) + (4 << 20)
    return pl.pallas_call(
        body, name=name, grid=grid, in_specs=in_specs, out_specs=spec((tm, tn), o_idx, so, True),
        out_shape=jax.ShapeDtypeStruct(out_shape, out_dtype),
        scratch_shapes=[pltpu.VMEM((tm, tn), F32)] if n_red > 1 else [],
        compiler_params=_params(5, need),
    )(*map(_in_hbm, operands))


def _rowwise(name, fn, row_ins, const_ins, row_outs, acc_outs, tr=TR):
    n_ri, n_ci, n_ro, n_ao = len(row_ins), len(const_ins), len(row_outs), len(acc_outs)

    def body(*refs):
        ri, ci = refs[:n_ri], refs[n_ri:n_ri + n_ci]
        ro, ao = refs[n_ri + n_ci:n_ri + n_ci + n_ro], refs[n_ri + n_ci + n_ro:]
        i = pl.program_id(0)
        outs = fn(i * tr, *[r[...] for r in ri], *[r[...] for r in ci])
        for r, v in zip(ro, outs[:n_ro]):
            r[...] = v.astype(r.dtype)
        if n_ao:
            @pl.when(i == 0)
            def _():
                for r, v in zip(ao, outs[n_ro:]):
                    r[...] = v

            @pl.when(i > 0)
            def _():
                for r, v in zip(ao, outs[n_ro:]):
                    r[...] += v

    in_specs = [pl.BlockSpec((tr, w), functools.partial(lambda cb, i: (i, cb), cb)) for _, w, cb in row_ins]
    in_specs += [pl.BlockSpec(c.shape, functools.partial(lambda nd, i: (0,) * nd, c.ndim)) for c in const_ins]
    out_specs = [pl.BlockSpec((tr, w), lambda i: (i, 0)) for w, _ in row_outs]
    out_specs += [pl.BlockSpec((1, w), lambda i: (0, 0)) for w in acc_outs]
    out_shape = [jax.ShapeDtypeStruct((T, w), dt) for w, dt in row_outs]
    out_shape += [jax.ShapeDtypeStruct((1, w), F32) for w in acc_outs]
    need = 2 * (sum(_nbytes((tr, w), a.dtype) for a, w, _ in row_ins) + sum(_nbytes((tr, w), dt) for w, dt in row_outs))
    need = 3 * need + (8 << 20)
    return pl.pallas_call(
        body, name=name, grid=(T // tr,), in_specs=in_specs, out_specs=out_specs, out_shape=out_shape,
        compiler_params=_params(1, need),
    )(*[_in_hbm(a) for a, _, _ in row_ins], *const_ins)


def _rms_fwd_fn(row0, h, g):
    h = h.astype(F32)
    r = lax.rsqrt(jnp.mean(h * h, axis=-1, keepdims=True) + EPS)
    return (h * r) * g, r


def _rms_bwd(dxn, h, r, g):
    xhat = h * r
    dxg = dxn * g
    dh = r * (dxg - xhat * jnp.mean(dxg * xhat, axis=-1, keepdims=True))
    return dh, jnp.sum(dxn * xhat, axis=0, keepdims=True)


def _rms_bwd_fn(row0, dxn, h, r, g):
    return _rms_bwd(dxn.astype(F32), h.astype(F32), r, g)


def _rms_bwd_res_fn(row0, dxn, h, r, res, g):
    dh, dg = _rms_bwd(dxn.astype(F32), h.astype(F32), r, g)
    return dh + res, dg


def _gelu_fn(row0, y):
    return (jax.nn.gelu(y),)


def _glu_fwd_fn(row0, y, gl, b, gain):
    ya = jax.nn.gelu(y) * jax.nn.sigmoid(gl + b)
    return _rms_fwd_fn(row0, ya, gain)


def _glu_bwd_fn(row0, dyan, y, gl, ra, b, gain):
    g = jax.nn.gelu(y)
    s = jax.nn.sigmoid(gl + b)
    dya, dgain = _rms_bwd(dyan, g * s, ra, gain)
    dgl = dya * g * s * (1.0 - s)
    return dgl, dya * s, dgain, jnp.sum(dgl, axis=0, keepdims=True)


def _gelu_bwd_fn(row0, dg, y):
    return (jax.vjp(jax.nn.gelu, y)[1](dg)[0],)


def _loss_fn(row0, h, tgt, gain):
    r = lax.rsqrt(jnp.mean(h * h, axis=-1, keepdims=True) + EPS)
    xhat = h * r
    rows = row0 + lax.broadcasted_iota(jnp.int32, h.shape, 0)
    err = jnp.where(rows >= N_META, xhat * gain - tgt, 0.0)
    loss = jnp.full((1, 128), jnp.sum(err * err) * (0.5 / D), F32)
    dh, dgain = _rms_bwd(err * (1.0 / D), h, r, gain)
    return dh, dgain, loss


def _rope_tables():
    pos = jnp.arange(T, dtype=jnp.int32)
    inv_freq = 1.0 / (ROPE_BASE ** (jnp.arange(0, QK_ROPE, 2, dtype=F32) / QK_ROPE))
    ang = pos.astype(F32)[:, None] * inv_freq[None, :]
    cos, sin, z32, z64 = jnp.cos(ang), jnp.sin(ang), jnp.zeros((T, 32), F32), jnp.zeros((T, 64), F32)
    return (jnp.concatenate([cos, cos, z64], 1), jnp.concatenate([-sin, z32, z64], 1),
            jnp.concatenate([z32, sin, z64], 1))


def _rot(x, cc, s1, s2):
    return x * cc + pltpu.roll(x, 96, 1) * s1 + pltpu.roll(x, 32, 1) * s2


def _derot(d, cc, s1, s2):
    return d * cc + pltpu.roll(d * s1, 32, 1) + pltpu.roll(d * s2, 96, 1)


def _chunk_of(pos):
    return jnp.where(pos < N_META, 0, 1 + ((pos - N_META) >> 6))


def _key_len(t):
    last_chunk = 1 + ((t + 1) * TR - 1 - N_META) // CHUNK
    return min(T, pl.cdiv(N_META + last_chunk * CHUNK, 128) * 128)


def _attn_probs(t, q_ref, kv_ref, kpe_ref, tq, tk):
    kl = _key_len(t)
    q = q_ref[...]
    qn = q[:, :QK_NOPE].astype(BF16)
    qp = _rot(q[:, QK_NOPE:], tq[0][...], tq[1][...], tq[2][...]).astype(BF16)
    kn = kv_ref[:kl, :QK_NOPE].astype(BF16)
    v = kv_ref[:kl, QK_NOPE:].astype(BF16)
    kp = _rot(kpe_ref[:kl, :], tk[0][:kl, :], tk[1][:kl, :], tk[2][:kl, :]).astype(BF16)
    s = lax.dot_general(qn, kn, _DIMS['nt'], preferred_element_type=F32)
    s = s + lax.dot_general(qp, kp, _DIMS['nt'], preferred_element_type=F32)
    qc = _chunk_of(t * TR + lax.broadcasted_iota(jnp.int32, (TR, 1), 0))
    kc = _chunk_of(lax.broadcasted_iota(jnp.int32, (1, kl), 1))
    s = jnp.where(kc <= qc, s * SM_SCALE, jnp.finfo(F32).min)
    p = jnp.exp(s - jnp.max(s, axis=-1, keepdims=True))
    p = p * (1.0 / jnp.sum(p, axis=-1, keepdims=True))
    return qn, qp, kn, kp, v, p


def _per_q_tile(fn):
    for t in range(T // TR):
        pl.when(pl.program_id(1) == t)(functools.partial(fn, t))


def _attn_specs(z, tabs):
    q_spec = pl.BlockSpec((None, TR, HEAD_PAD), lambda h, i: (h, i, 0))
    kv_spec = pl.BlockSpec((None, T, HEAD_PAD), lambda h, i: (h, 0, 0))
    kpe_spec = pl.BlockSpec((T, 128), lambda h, i: (0, (D_IN - QK_ROPE) // 128))
    tq = [pl.BlockSpec((TR, 128), lambda h, i: (i, 0))] * 3
    tk = [pl.BlockSpec((T, 128), lambda h, i: (0, 0))] * 3
    return [q_spec, kv_spec, kpe_spec] + tq + tk


def _attn_fwd(q, kv, z, tabs):
    def body(q_ref, kv_ref, kpe_ref, c1, c2, c3, k1, k2, k3, o_ref):
        def tile(t):
            _, _, _, _, v, p = _attn_probs(t, q_ref, kv_ref, kpe_ref, (c1, c2, c3), (k1, k2, k3))
            o_ref[...] = jnp.dot(p.astype(BF16), v, preferred_element_type=F32)

        _per_q_tile(tile)

    return pl.pallas_call(
        body, name="attn_fwd", grid=(N_HEADS, T // TR), in_specs=_attn_specs(z, tabs),
        out_specs=pl.BlockSpec((TR, V_HEAD), lambda h, i: (i, h)),
        out_shape=jax.ShapeDtypeStruct((T, N_HEADS * V_HEAD), F32),
        compiler_params=_params(2, 48 << 20),
    )(_in_hbm(q), _in_hbm(kv), _in_hbm(z), *tabs, *tabs)


def _attn_bwd(q, kv, z, tabs, dyb):
    def body(q_ref, kv_ref, kpe_ref, c1, c2, c3, k1, k2, k3, do_ref, dq_ref, dkv_ref, dkpe_ref):
        @pl.when(pl.program_id(1) == 0)
        def _():
            dkv_ref[...] = jnp.zeros(dkv_ref.shape, F32)
            dkpe_ref[...] = jnp.zeros(dkpe_ref.shape, F32)

        def tile(t):
            kl = _key_len(t)
            qn, qp, kn, kp, v, p = _attn_probs(t, q_ref, kv_ref, kpe_ref, (c1, c2, c3), (k1, k2, k3))
            do = do_ref[...].astype(BF16)
            dp = lax.dot_general(do, v, _DIMS['nt'], preferred_element_type=F32)
            ds = (p * (dp - jnp.sum(p * dp, axis=-1, keepdims=True)) * SM_SCALE).astype(BF16)
            pb = p.astype(BF16)
            dq_ref[:, :QK_NOPE] = jnp.dot(ds, kn, preferred_element_type=F32).astype(dq_ref.dtype)
            dqp = _derot(jnp.dot(ds, kp, preferred_element_type=F32), c1[...], c2[...], c3[...])
            dq_ref[:, QK_NOPE:] = dqp.astype(dq_ref.dtype)
            dkv_ref[:kl, :QK_NOPE] += lax.dot_general(ds, qn, _DIMS['tn'], preferred_element_type=F32)
            dkv_ref[:kl, QK_NOPE:] += lax.dot_general(pb, do, _DIMS['tn'], preferred_element_type=F32)
            dkpe_ref[:kl, :] += lax.dot_general(ds, qp, _DIMS['tn'], preferred_element_type=F32)

        _per_q_tile(tile)

    return pl.pallas_call(
        body, name="attn_bwd", grid=(N_HEADS, T // TR),
        in_specs=_attn_specs(z, tabs) + [pl.BlockSpec((TR, V_HEAD), lambda h, i: (i, h))],
        out_specs=[pl.BlockSpec((None, TR, HEAD_PAD), lambda h, i: (h, i, 0)),
                   pl.BlockSpec((None, T, HEAD_PAD), lambda h, i: (h, 0, 0)),
                   pl.BlockSpec((None, T, 128), lambda h, i: (h, 0, 0))],
        out_shape=[jax.ShapeDtypeStruct((N_HEADS, T, HEAD_PAD), BF16),
                   jax.ShapeDtypeStruct((N_HEADS, T, HEAD_PAD), F32),
                   jax.ShapeDtypeStruct((N_HEADS, T, 128), F32)],
        compiler_params=_params(2, 56 << 20),
    )(_in_hbm(q), _in_hbm(kv), _in_hbm(z), *tabs, *tabs, _in_hbm(dyb))


def _kpe_bwd(dkpe, tabs):
    def body(d_ref, c1, c2, c3, o_ref):
        d = d_ref[0]
        for h in range(1, N_HEADS):
            d = d + d_ref[h]
        o_ref[...] = _derot(d, c1[...], c2[...], c3[...]).astype(o_ref.dtype)

    tile = pl.BlockSpec((TR, 128), lambda i: (i, 0))
    return pl.pallas_call(
        body, name="kpe_bwd", grid=(T // TR,),
        in_specs=[pl.BlockSpec((N_HEADS, TR, 128), lambda i: (0, i, 0)), tile, tile, tile], out_specs=tile,
        out_shape=jax.ShapeDtypeStruct((T, 128), BF16), compiler_params=_params(1, 16 << 20),
    )(_in_hbm(dkpe), *tabs)


def _s5_prep_math(lam_re, lam_im, log_dt, bt_re, bt_im):
    dt = jnp.exp(log_dt)
    mag = jnp.exp(lam_re * dt)
    a_re, a_im = mag * jnp.cos(lam_im * dt), mag * jnp.sin(lam_im * dt)
    den = lam_re * lam_re + lam_im * lam_im
    w_re = ((a_re - 1.0) * lam_re + a_im * lam_im) / den
    w_im = (a_im * lam_re - (a_re - 1.0) * lam_im) / den
    return a_re, a_im, w_re[None] * bt_re - w_im[None] * bt_im, w_re[None] * bt_im + w_im[None] * bt_re


def _eye_groups():
    return (lax.broadcasted_iota(jnp.int32, (N_GROUPS, N_GROUPS), 0)
            == lax.broadcasted_iota(jnp.int32, (N_GROUPS, N_GROUPS), 1)).astype(F32)


def _row_to_col(row):
    return jnp.sum(_eye_groups() * row, axis=1, keepdims=True)


def _s5_prep(lam_re, lam_im, log_dt, bt_re, bt_im):
    def body(lr, li, ld, br, bi, ar, ai, bbr, bbi):
        ar[...], ai[...], bbr[...], bbi[...] = _s5_prep_math(lr[...], li[...], _row_to_col(ld[...]), br[...], bi[...])

    gp, cgp = jax.ShapeDtypeStruct((N_GROUPS, SSM_STATE), F32), jax.ShapeDtypeStruct(bt_re.shape, F32)
    return pl.pallas_call(body, name="s5_prep", out_shape=[gp, gp, cgp, cgp])(lam_re, lam_im, log_dt, bt_re, bt_im)


def _s5_prep_bwd(lam_re, lam_im, log_dt, bt_re, bt_im, da_re, da_im, dbb_re, dbb_im):
    def body(lr, li, ld, br, bi, dar, dai, dbr, dbi, o1, o2, o3, o4, o5):
        _, vjp = jax.vjp(_s5_prep_math, lr[...], li[...], _row_to_col(ld[...]), br[...], bi[...])
        o1[...], o2[...], dld, o4[...], o5[...] = vjp((dar[...], dai[...], dbr[...], dbi[...]))
        o3[...] = jnp.sum(_eye_groups() * dld, axis=0, keepdims=True)

    ins = (lam_re, lam_im, log_dt, bt_re, bt_im)
    return pl.pallas_call(body, name="s5_prep_bwd", out_shape=[jax.ShapeDtypeStruct(a.shape, F32) for a in ins])(
        *ins, da_re, da_im, dbb_re, dbb_im)


def _cmul(ar, ai, br, bi):
    return ar * br - ai * bi, ar * bi + ai * br


def _seg_rows(i):
    return pl.ds(pl.multiple_of(i * N_SEG, N_SEG), N_SEG)


def _scan(xr, xi, ar, ai, reverse):
    zero = jnp.zeros(ar.shape, F32)

    def local(j, carry):
        rows = _seg_rows(SEG - 1 - j if reverse else j)
        nr, ni = _cmul(ar, ai, *carry)
        nr, ni = nr + xr[rows, :], ni + xi[rows, :]
        xr[rows, :], xi[rows, :] = nr, ni
        return nr, ni

    er, ei = lax.fori_loop(0, SEG, local, (zero, zero))
    pr, pi = ar, ai
    for _ in range(8):
        pr, pi = _cmul(pr, pi, pr, pi)
    pr, pi = _cmul(*_cmul(pr, pi, ar, ai), ar, ai)
    row = lax.broadcasted_iota(jnp.int32, ar.shape, 0)
    edge, shift = (N_SEG - 1, N_SEG - 1) if reverse else (0, 1)
    hr, hi = zero, zero
    for _ in range(N_SEG - 1):
        tr_, ti_ = _cmul(pr, pi, hr, hi)
        hr = jnp.where(row == edge, 0.0, pltpu.roll(tr_ + er, shift, 0))
        hi = jnp.where(row == edge, 0.0, pltpu.roll(ti_ + ei, shift, 0))

    def fix(j, carry):
        rows = _seg_rows(SEG - 1 - j if reverse else j)
        xr[rows, :] += carry[0]
        xi[rows, :] += carry[1]
        return _cmul(ar, ai, *carry)

    lax.fori_loop(0, SEG, fix, _cmul(ar, ai, hr, hi))
    return hr, hi


def _split(x):
    hi = x.astype(BF16)
    return hi, (x - hi.astype(F32)).astype(BF16)


def _dot3(xs, ys, mode='nn'):
    d = lambda p, q: lax.dot_general(p, q, _DIMS[mode], preferred_element_type=F32)
    return d(xs[0], ys[0]) + (d(xs[1], ys[0]) + d(xs[0], ys[1]))


def _s5_hidden(us, a_ref, bre_ref, bim_ref, hre, him):
    hre[...] = _dot3(us, _split(bre_ref[...]))
    him[...] = _dot3(us, _split(bim_ref[...]))
    return _scan(hre, him, a_ref[0], a_ref[1], reverse=False)


def _s5_specs():
    col = pl.BlockSpec((T, 128), lambda q: (0, q))
    a_spec = pl.BlockSpec((None, 2, N_SEG, GQ * SSM_STATE), lambda q: (q, 0, 0, 0))
    wide = pl.BlockSpec((None, 128, GQ * SSM_STATE), lambda q: (q, 0, 0))
    tall = pl.BlockSpec((None, GQ * SSM_STATE, 128), lambda q: (q, 0, 0))
    d_spec = pl.BlockSpec((1, 128), lambda q: (0, q))
    return col, a_spec, wide, tall, d_spec


def _s5_fwd(u, a8, bre, bim, cre, cim, dsk):
    def body(u_ref, a_ref, bre_ref, bim_ref, cre_ref, cim_ref, d_ref, y_ref, hre, him):
        u_ = u_ref[...]
        _s5_hidden(_split(u_), a_ref, bre_ref, bim_ref, hre, him)
        y_ref[...] = (_dot3(_split(hre[...]), _split(cre_ref[...]))
                      - _dot3(_split(him[...]), _split(cim_ref[...])) + d_ref[...] * u_)

    col, a_spec, wide, tall, d_spec = _s5_specs()
    return pl.pallas_call(
        body, name="s5_fwd", grid=(N_GROUPS // GQ,), in_specs=[col, a_spec, wide, wide, tall, tall, d_spec],
        out_specs=col, out_shape=jax.ShapeDtypeStruct((T, D_SSM), F32),
        scratch_shapes=[pltpu.VMEM((T, GQ * SSM_STATE), F32)] * 2,
        compiler_params=_params(1, 40 << 20),
    )(_in_hbm(u), a8, bre, bim, cre, cim, dsk)


def _s5_bwd(u, dy, a8, bre, bim, cre, cim, dsk):
    def body(u_ref, dy_ref, a_ref, bre_ref, bim_ref, cre_ref, cim_ref, d_ref,
             du_ref, dbre_ref, dbim_ref, dcre_ref, dcim_ref, da_ref, dd_ref, hre, him, lre, lim):
        u_, dy_ = u_ref[...], dy_ref[...]
        us, dys = _split(u_), _split(dy_)
        ar, ai = a_ref[0], a_ref[1]
        bres, bims = _split(bre_ref[...]), _split(bim_ref[...])
        hre[...] = _dot3(us, bres)
        him[...] = _dot3(us, bims)
        h0r, h0i = _scan(hre, him, ar, ai, reverse=False)
        lre[...] = _dot3(dys, _split(cre_ref[...]), 'nt')
        lim[...] = -_dot3(dys, _split(cim_ref[...]), 'nt')
        _scan(lre, lim, ar, -ai, reverse=True)

        def acc_da(i, carry):
            lr, li = lre[_seg_rows(i), :], lim[_seg_rows(i), :]
            pr, pi = hre[_seg_rows(i - 1), :], him[_seg_rows(i - 1), :]
            return carry[0] + lr * pr + li * pi, carry[1] + li * pr - lr * pi

        lr, li = lre[_seg_rows(0), :], lim[_seg_rows(0), :]
        dar, dai = lax.fori_loop(1, SEG, acc_da, (lr * h0r + li * h0i, li * h0r - lr * h0i))
        da_ref[0:1, :] = jnp.sum(dar, axis=0, keepdims=True)
        da_ref[1:2, :] = jnp.sum(dai, axis=0, keepdims=True)
        dot = lambda p, q, mode: lax.dot_general(p, q, _DIMS[mode], preferred_element_type=F32)
        lr1, li1 = lre[...].astype(BF16), lim[...].astype(BF16)
        du_ref[...] = dy_ * d_ref[...] + dot(lr1, bres[0], 'nt') + dot(li1, bims[0], 'nt')
        dbre_ref[...] = dot(us[0], lr1, 'tn')
        dbim_ref[...] = dot(us[0], li1, 'tn')
        dcre_ref[...] = dot(hre[...].astype(BF16), dys[0], 'tn')
        dcim_ref[...] = -dot(him[...].astype(BF16), dys[0], 'tn')
        dd_ref[...] = jnp.sum(dy_ * u_, axis=0, keepdims=True)

    col, a_spec, wide, tall, d_spec = _s5_specs()
    nq, ns = N_GROUPS // GQ, GQ * SSM_STATE
    return pl.pallas_call(
        body, name="s5_bwd", grid=(nq,),
        in_specs=[col, col, a_spec, wide, wide, tall, tall, d_spec],
        out_specs=[col, wide, wide, tall, tall, pl.BlockSpec((None, 2, ns), lambda q: (q, 0, 0)), d_spec],
        out_shape=[jax.ShapeDtypeStruct((T, D_SSM), F32),
                   jax.ShapeDtypeStruct((nq, 128, ns), F32), jax.ShapeDtypeStruct((nq, 128, ns), F32),
                   jax.ShapeDtypeStruct((nq, ns, 128), F32), jax.ShapeDtypeStruct((nq, ns, 128), F32),
                   jax.ShapeDtypeStruct((nq, 2, ns), F32), jax.ShapeDtypeStruct((1, D_SSM), F32)],
        scratch_shapes=[pltpu.VMEM((T, ns), F32)] * 4,
        compiler_params=_params(1, 56 << 20),
    )(_in_hbm(u), _in_hbm(dy), a8, bre, bim, cre, cim, dsk)


def _to_segments(a):
    return a.reshape(N_SEG, SEG, a.shape[-1]).transpose(1, 0, 2).reshape(T, a.shape[-1])


def _from_segments(a):
    return a.reshape(SEG, N_SEG, a.shape[-1]).transpose(1, 0, 2).reshape(T, a.shape[-1])


def _block_diag_b(bbt):
    nq = N_GROUPS // GQ
    x = bbt.transpose(1, 0, 2).reshape(nq, GQ, SSM_GROUP, 1, SSM_STATE)
    eye = jnp.eye(GQ, dtype=F32)[None, :, None, :, None]
    return (x * eye).reshape(nq, GQ * SSM_GROUP, GQ * SSM_STATE)


def _block_diag_c(c):
    nq = N_GROUPS // GQ
    x = c.reshape(nq, GQ, SSM_GROUP, SSM_STATE).transpose(0, 1, 3, 2)[:, :, :, None, :]
    eye = jnp.eye(GQ, dtype=F32)[None, :, None, :, None]
    return (x * eye).reshape(nq, GQ * SSM_STATE, GQ * SSM_GROUP)


def _ff_specs():
    up_spec = pl.BlockSpec((2, None, T, 128), lambda p, cb: (0, p, 0, cb))
    piece = pl.BlockSpec((None, T, 128), lambda p, cb: (p, 0, cb))
    cw_spec = pl.BlockSpec((None, 3, 128), lambda p, cb: (p, 0, cb))
    cb_spec = pl.BlockSpec((None, 1, 128), lambda p, cb: (p, 0, cb))
    return up_spec, piece, cw_spec, cb_spec


def _conv_gate(gate, w, cb, pad_ref):
    pad_ref[0:8, :] = jnp.zeros((8, 128), F32)
    pad_ref[8:8 + T, :] = gate
    g1, g2 = pad_ref[pl.ds(7, T), :], pad_ref[pl.ds(6, T), :]
    return w[0:1, :] * g2 + w[1:2, :] * g1 + w[2:3, :] * gate + cb, g1, g2


def _ff_act(up, cw4, cb4):
    def body(up_ref, w_ref, b_ref, o_ref, pad_ref):
        gc, _, _ = _conv_gate(up_ref[0].astype(F32), w_ref[...], b_ref[...], pad_ref)
        o_ref[...] = (jax.nn.silu(gc) * up_ref[1].astype(F32)).astype(o_ref.dtype)

    up_spec, piece, cw_spec, cb_spec = _ff_specs()
    n_cb = pl.cdiv(FF_PIECE, 128)
    return pl.pallas_call(
        body, name="ff_act", grid=(4, n_cb), in_specs=[up_spec, cw_spec, cb_spec], out_specs=piece,
        out_shape=jax.ShapeDtypeStruct((4, T, FF_PIECE), BF16),
        scratch_shapes=[pltpu.VMEM((T + 8, 128), F32)],
        compiler_params=_params(2, 24 << 20),
    )(_in_hbm(up.reshape(2, 4, T, FF_PIECE)), cw4, cb4)


def _ff_act_bwd(up, dact, cw4, cb4):
    def body(up_ref, da_ref, w_ref, b_ref, dup_ref, dw_ref, db_ref, pad_ref, pad2_ref):
        gate, val, w = up_ref[0].astype(F32), up_ref[1].astype(F32), w_ref[...]
        gc, g1, g2 = _conv_gate(gate, w, b_ref[...], pad_ref)
        sg = jax.nn.sigmoid(gc)
        da = da_ref[...].astype(F32)
        dup_ref[1] = (da * gc * sg).astype(dup_ref.dtype)
        dgc = da * val * sg * (1.0 + gc * (1.0 - sg))
        db_ref[...] = jnp.sum(dgc, axis=0, keepdims=True)
        dw_ref[0:1, :] = jnp.sum(dgc * g2, axis=0, keepdims=True)
        dw_ref[1:2, :] = jnp.sum(dgc * g1, axis=0, keepdims=True)
        dw_ref[2:3, :] = jnp.sum(dgc * gate, axis=0, keepdims=True)
        pad2_ref[0:T, :] = dgc
        pad2_ref[T:T + 8, :] = jnp.zeros((8, 128), F32)
        d1, d2 = pad2_ref[pl.ds(1, T), :], pad2_ref[pl.ds(2, T), :]
        dup_ref[0] = (w[2:3, :] * dgc + w[1:2, :] * d1 + w[0:1, :] * d2).astype(dup_ref.dtype)

    up_spec, piece, cw_spec, cb_spec = _ff_specs()
    n_cb = pl.cdiv(FF_PIECE, 128)
    dup, dw, db = pl.pallas_call(
        body, name="ff_act_bwd", grid=(4, n_cb), in_specs=[up_spec, piece, cw_spec, cb_spec],
        out_specs=[up_spec, cw_spec, cb_spec],
        out_shape=[jax.ShapeDtypeStruct((2, 4, T, FF_PIECE), BF16), jax.ShapeDtypeStruct((4, 3, FF_PIECE), F32),
                   jax.ShapeDtypeStruct((4, 1, FF_PIECE), F32)],
        scratch_shapes=[pltpu.VMEM((T + 8, 128), F32)] * 2,
        compiler_params=_params(2, 32 << 20),
    )(_in_hbm(up.reshape(2, 4, T, FF_PIECE)), _in_hbm(dact), cw4, cb4)
    return dup.reshape(8, T, FF_PIECE), dw, db


def _local_step(x, tgt, meta, p, wf, late_weights, send_grads, send_small):
    sm = {}
    h0 = jnp.concatenate([meta, x], axis=0)
    tgt_pad = jnp.concatenate([jnp.zeros((N_META, D), F32), tgt], axis=0)
    tabs = _rope_tables()
    row = lambda a: a.reshape(1, -1)

    xn, r0 = _rowwise("rms_mix", _rms_fwd_fn, [(h0, D, 0)], [row(p['mix_norm'])], [(D, BF16), (1, F32)], [])
    z = _mm("in_proj", xn, wf['w_in'], 'nn', TR, 640, D, F32)
    lam_re, lam_im = p['lam_re'].reshape(N_GROUPS, SSM_STATE), p['lam_im'].reshape(N_GROUPS, SSM_STATE)
    log_dt = p['log_dt'].reshape(1, N_GROUPS)
    bt_re = p['b_re'].reshape(N_GROUPS, SSM_STATE, SSM_GROUP).transpose(2, 0, 1)
    bt_im = p['b_im'].reshape(N_GROUPS, SSM_STATE, SSM_GROUP).transpose(2, 0, 1)
    c_re = p['c_re'].reshape(N_GROUPS, SSM_GROUP, SSM_STATE)
    c_im = p['c_im'].reshape(N_GROUPS, SSM_GROUP, SSM_STATE)
    a_re, a_im, bbt_re, bbt_im = _s5_prep(lam_re, lam_im, log_dt, bt_re, bt_im)
    nq, ns = N_GROUPS // GQ, GQ * SSM_STATE
    a8 = jnp.broadcast_to(jnp.stack([a_re.reshape(nq, ns), a_im.reshape(nq, ns)], 1)[:, :, None, :],
                          (nq, 2, N_SEG, ns))
    bre, bim = _block_diag_b(bbt_re), _block_diag_b(bbt_im)
    cre, cim = _block_diag_c(c_re), _block_diag_c(c_im)
    dsk = row(p['d_skip'])
    u_seg = _to_segments(z[:, :D_SSM])
    y_ssm = _from_segments(_s5_fwd(u_seg, a8, bre, bim, cre, cim, dsk))
    g, = _rowwise("gelu", _gelu_fn, [(y_ssm, D_SSM, 0)], [], [(D_SSM, BF16)], [])
    gl = _mm("glu_proj", g, wf['w_glu'], 'nn', TR, D_SSM, D_SSM, F32)
    yan, ra = _rowwise("glu_norm", _glu_fwd_fn, [(y_ssm, D_SSM, 0), (gl, D_SSM, 0)],
                       [row(p['b_glu']), row(p['out_norm_ssm'])], [(D_SSM, BF16), (1, F32)], [])
    qn, rq = _rowwise("rms_q", _rms_fwd_fn, [(z, Q_LORA, D_SSM // Q_LORA)], [row(p['q_a_norm'])],
                      [(Q_LORA, BF16), (1, F32)], [])
    kvn, rkv = _rowwise("rms_kv", _rms_fwd_fn, [(z, KV_LORA, (D_SSM + Q_LORA) // KV_LORA)], [row(p['kv_a_norm'])],
                        [(KV_LORA, BF16), (1, F32)], [])
    q = _mm("q_proj", qn, wf['w_q_b'], 'nn', T, HEAD_PAD, Q_LORA, F32, stack='b_out')
    kv = _mm("kv_proj", kvn, wf['w_kv_b'], 'nn', T, HEAD_PAD, KV_LORA, F32, stack='b_out')
    yb = _attn_fwd(q, kv, z, tabs)
    ybn, rb = _rowwise("rms_attn", _rms_fwd_fn, [(yb, D_SSM, 0)], [row(p['out_norm_attn'])],
                       [(D_SSM, BF16), (1, F32)], [])
    y = jnp.concatenate([yan, ybn], axis=1)
    wf = dict(wf)
    wf['w_out'], wf['w_up'], wf['w_down'] = late_weights(y)
    h1 = _mm("out_proj", y, wf['w_out'], 'nn', TR, 1024, D, F32, res=h0)
    xn2, r1 = _rowwise("rms_ffn", _rms_fwd_fn, [(h1, D, 0)], [row(p['ffn_norm'])], [(D, BF16), (1, F32)], [])
    up = _mm("up_proj", xn2, wf['w_up'], 'nn', TR, FF_PIECE, D, BF16, stack='b_out')
    cw4 = wf['conv_w'].reshape(3, 4, FF_PIECE).transpose(1, 0, 2)
    cb4 = p['conv_b'].reshape(4, 1, FF_PIECE)
    act = _ff_act(up, cw4, cb4)
    wd4 = wf['w_down'].reshape(4, FF_PIECE, D)
    h2 = _mm("down_proj", act, wd4, 'nn', TR, 1024, FF_PIECE, F32, stack='ab_red', res=h1)
    dh2, sm['final_norm'], loss = _rowwise("loss", _loss_fn, [(h2, D, 0), (tgt_pad, D, 0)], [row(p['final_norm'])],
                                           [(D, F32)], [D, 128])

    big = {}
    dact = _mm("down_bwd_x", dh2, wd4, 'nt', TR, FF_PIECE, D, BF16, stack='b_out')
    g_down = _mm("down_bwd_w", act, dh2, 'tn', FF_PIECE, 512, T, BF16, stack='a_out').reshape(
        N_DEV, D_FF // N_DEV, D)
    dup, dcw4, sm['conv_b'] = _ff_act_bwd(up, dact, cw4, cb4)
    dxn2 = _mm("up_bwd_x", dup, wf['w_up'], 'nt', TR, 1024, FF_PIECE, F32, stack='ab_red')
    g_up = _mm("up_bwd_w", dup, xn2, 'tn', FF_PIECE, 1024, T, BF16, stack='a_out')
    token = send_grads(['w_down', 'w_up'], [g_down, g_up])
    dh1, sm['ffn_norm'] = _rowwise("rms_ffn_bwd", _rms_bwd_res_fn,
                                   [(dxn2, D, 0), (h1, D, 0), (r1, 1, 0), (dh2, D, 0)],
                                   [row(p['ffn_norm']) + token[0:1, 0:1]], [(D, F32)], [D])
    dy = _mm("out_bwd_x", dh1, wf['w_out'], 'nt', TR, 1024, D, F32)
    g_out = _mm("out_bwd_w", y, dh1, 'tn', 1024, 1024, T, BF16).reshape(N_DEV, D // N_DEV, D)
    dgl, dgd, sm['out_norm_ssm'], sm['b_glu'] = _rowwise(
        "glu_bwd", _glu_bwd_fn, [(dy, D_SSM, 0), (y_ssm, D_SSM, 0), (gl, D_SSM, 0), (ra, 1, 0)],
        [row(p['b_glu']), row(p['out_norm_ssm'])], [(D_SSM, BF16), (D_SSM, F32)], [D_SSM, D_SSM])
    dg = _mm("glu_bwd_x", dgl, wf['w_glu'], 'nt', TR, D_SSM, D_SSM, F32, res=dgd)
    g_glu = _mm("glu_bwd_w", g, dgl, 'tn', D_SSM, D_SSM, T, BF16).reshape(N_DEV, D_SSM // N_DEV, D_SSM)
    dy_ssm, = _rowwise("gelu_bwd", _gelu_bwd_fn, [(dg, D_SSM, 0), (y_ssm, D_SSM, 0)], [], [(D_SSM, F32)], [])
    dyb, sm['out_norm_attn'] = _rowwise("rms_attn_bwd", _rms_bwd_fn, [(dy, D_SSM, 1), (yb, D_SSM, 0), (rb, 1, 0)],
                                        [row(p['out_norm_attn'])], [(D_SSM, F32)], [D_SSM])
    dq, dkv, dkpe = _attn_bwd(q, kv, z, tabs, dyb)
    g_q = _mm("q_bwd_w", qn, dq, 'tn', Q_LORA, HEAD_PAD, T, BF16, stack='b_out')
    dqn = _mm("q_bwd_x", dq, wf['w_q_b'], 'nt', TR, Q_LORA, HEAD_PAD, F32, stack='ab_red')
    g_kv = _mm("kv_bwd_w", kvn, dkv, 'tn', KV_LORA, HEAD_PAD, T, BF16, stack='b_out')
    dkvn = _mm("kv_bwd_x", dkv, wf['w_kv_b'], 'nt', TR, KV_LORA, HEAD_PAD, F32, stack='ab_red')
    token = send_grads(['w_out', 'w_glu', 'w_q_b', 'w_kv_b'], [g_out, g_glu, g_q, g_kv])
    dq_a, sm['q_a_norm'] = _rowwise("rms_q_bwd", _rms_bwd_fn,
                                    [(dqn, Q_LORA, 0), (z, Q_LORA, D_SSM // Q_LORA), (rq, 1, 0)],
                                    [row(p['q_a_norm'])], [(Q_LORA, BF16)], [Q_LORA])
    dkv_a, sm['kv_a_norm'] = _rowwise("rms_kv_bwd", _rms_bwd_fn,
                                      [(dkvn, KV_LORA, 0), (z, KV_LORA, (D_SSM + Q_LORA) // KV_LORA), (rkv, 1, 0)],
                                      [row(p['kv_a_norm'])], [(KV_LORA, BF16)], [KV_LORA])
    dk_pe = _kpe_bwd(dkpe, tabs)
    du_seg, dbre, dbim, dcre, dcim, da, sm['d_skip'] = _s5_bwd(
        u_seg, _to_segments(dy_ssm), a8, bre, bim, cre, cim, dsk + token[0:1, 0:1])
    du = _from_segments(du_seg).astype(BF16)
    diag_b = lambda d: jnp.einsum('qgcgp->qgcp', d.reshape(nq, GQ, SSM_GROUP, GQ, SSM_STATE)).transpose(
        2, 0, 1, 3).reshape(SSM_GROUP, N_GROUPS, SSM_STATE)
    diag_c = lambda d: jnp.einsum('qgpgc->qgcp', d.reshape(nq, GQ, SSM_STATE, GQ, SSM_GROUP)).reshape(
        N_GROUPS * SSM_GROUP, SSM_STATE)
    dlam_re, dlam_im, dlog_dt, dbt_re, dbt_im = _s5_prep_bwd(
        lam_re, lam_im, log_dt, bt_re, bt_im, da[:, 0, :].reshape(N_GROUPS, SSM_STATE),
        da[:, 1, :].reshape(N_GROUPS, SSM_STATE), diag_b(dbre), diag_b(dbim))
    sm['lam_re'], sm['lam_im'], sm['log_dt'] = dlam_re, dlam_im, dlog_dt
    cgp = lambda d: d.reshape(SSM_GROUP * N_GROUPS, SSM_STATE)
    sm['b_re'], sm['b_im'] = cgp(dbt_re), cgp(dbt_im)
    sm['c_re'], sm['c_im'] = diag_c(dcre), diag_c(dcim)
    token = send_small(sm, dcw4, loss)
    dz = jnp.concatenate([du, dq_a, dkv_a, dk_pe], axis=1)
    dxn = _mm("in_bwd_x", dz, wf['w_in'], 'nt', TR, 1024, D_IN_PAD, F32)
    big['w_in'] = _mm("in_bwd_w", xn, dz, 'tn', 1024, 640, T, BF16).reshape(N_DEV, D // N_DEV, D_IN_PAD)
    dh0, d_mix_norm = _rowwise("rms_mix_bwd", _rms_bwd_res_fn,
                               [(dxn, D, 0), (h0, D, 0), (r0, 1, 0), (dh1, D, 0)],
                               [row(p['mix_norm']) + token[0:1, 0:1]], [(D, F32)], [D])
    return dh0[N_META:], dh0[:N_META], d_mix_norm, big


def _place():
    x, y, c = lax.axis_index("x"), lax.axis_index("y"), lax.axis_index("c")
    return x, y, c, [(1 - x, y), (x, 1 - y), (1 - x, 1 - y)]


_HBM = pl.BlockSpec(memory_space=pltpu.HBM)


def _all_gather(name, arrs, chip_blocks=()):
    n, nc = len(arrs), len(chip_blocks)

    def body(*refs):
        ins, cins, outs, couts = refs[:n], refs[n:n + nc], refs[n + nc:2 * n + nc], refs[2 * n + nc:2 * (n + nc)]
        send, recv, loc, csend, crecv = refs[2 * (n + nc):]
        x, y, c, chips = _place()
        me, sib = (x, y, c), (x, y, 1 - c)

        def cp(a, k, blk, to, src=None):
            dst = outs[a].at[4 * blk[0] + 2 * blk[1] + blk[2]]
            return pltpu.make_async_remote_copy(src_ref=dst if src is None else src, dst_ref=dst,
                                                send_sem=send.at[a, k], recv_sem=recv.at[a, k],
                                                device_id=to, device_id_type=MESH)

        between_chips = [pltpu.make_async_remote_copy(src_ref=cins[a].at[2 * ch[0] + ch[1]], dst_ref=couts[a].at[j],
                                                      send_sem=csend.at[a, j], recv_sem=crecv.at[a, j],
                                                      device_id=(*ch, c), device_id_type=MESH)
                         for a in range(nc) for j, ch in enumerate(chips)]
        mine = [pltpu.make_async_copy(ins[a], outs[a].at[4 * x + 2 * y + c], loc.at[a]) for a in range(n)]
        first = []
        for a in range(n):
            first.append(cp(a, 0, me, sib, src=ins[a]))
            first += [cp(a, 1 + j, me, (*ch, c), src=ins[a]) for j, ch in enumerate(chips)]
        for f in mine + first + between_chips:
            f.start()
        passed = []
        for a in range(n):
            for j, ch in enumerate(chips):
                cp(a, 1 + j, (*ch, c), me).wait_recv()
                passed.append(cp(a, 4 + j, (*ch, c), sib))
                passed[-1].start()
        for a in range(n):
            cp(a, 0, sib, me).wait_recv()
            for j, ch in enumerate(chips):
                cp(a, 4 + j, (*ch, 1 - c), me).wait_recv()
        for f in first + passed:
            f.wait_send()
        for m in mine + between_chips:
            m.wait()

    out = pl.pallas_call(
        body, name=name, in_specs=[_HBM] * (n + nc), out_specs=[_HBM] * (n + nc),
        out_shape=[jax.ShapeDtypeStruct((N_DEV,) + a.shape, a.dtype) for a in arrs]
        + [jax.ShapeDtypeStruct((3,) + a.shape[1:], a.dtype) for a in chip_blocks],
        scratch_shapes=[pltpu.SemaphoreType.DMA((n, 7)), pltpu.SemaphoreType.DMA((n, 7)),
                        pltpu.SemaphoreType.DMA((n,)), pltpu.SemaphoreType.DMA((max(nc, 1), 3)),
                        pltpu.SemaphoreType.DMA((max(nc, 1), 3))],
    )(*arrs, *chip_blocks)
    return (out[:n], out[n:]) if nc else out


def _exchange_cores(name, arrs):
    n = len(arrs)

    def body(*refs):
        ins, outs = refs[:n], refs[n:2 * n]
        send, recv = refs[2 * n:]
        x, y, c, _ = _place()
        remote = [pltpu.make_async_remote_copy(src_ref=ins[a].at[2 * k + 1 - c], dst_ref=outs[a].at[k],
                                               send_sem=send.at[a, k], recv_sem=recv.at[a, k],
                                               device_id=(x, y, 1 - c), device_id_type=MESH)
                  for a in range(n) for k in range(4)]
        for d in remote:
            d.start()
        for d in remote:
            d.wait()

    return pl.pallas_call(
        body, name=name, in_specs=[_HBM] * n, out_specs=[_HBM] * n,
        out_shape=[jax.ShapeDtypeStruct((4,) + a.shape[1:], a.dtype) for a in arrs],
        scratch_shapes=[pltpu.SemaphoreType.DMA((n, 4))] * 2,
    )(*arrs)


_SEM = pl.BlockSpec(memory_space=pltpu.SEMAPHORE)
_EFFECT = pltpu.SideEffectType.DATAFLOW_SIDE_EFFECTING


def _split_exchange(name, srcs, land_shapes, copies_of, n_cp, after, all_of=None):
    n = len(srcs)
    lands = [pltpu.with_memory_space_constraint(lax.empty(s, a.dtype), pltpu.HBM) for s, a in zip(land_shapes, srcs)]

    per = 1 if all_of else n_cp
    n_sem = n * per

    def descriptors(src_refs, land_refs, send, recv, mirror):
        x, y, c, chips = _place()
        out = []
        for a in range(n):
            if mirror and all_of:
                whole = all_of(land_refs[a])
                out.append(pltpu.make_async_remote_copy(src_ref=whole, dst_ref=whole, send_sem=send[a],
                                                        recv_sem=recv[a], device_id=(x, y, 1 - c),
                                                        device_id_type=MESH))
                continue
            for k, (src, dst, to, back) in enumerate(copies_of(a, src_refs[a], land_refs[a], x, y, c, chips)):
                out.append(pltpu.make_async_remote_copy(src_ref=src, dst_ref=back if mirror else dst,
                                                        send_sem=send[a * per + k % per],
                                                        recv_sem=recv[a * per + k % per],
                                                        device_id=to, device_id_type=MESH))
        return out

    def start_body(*refs):
        sems = refs[2 * n + 1:2 * n + 1 + 2 * n_sem]
        for d in descriptors(refs[:n], refs[n:2 * n], sems[:n_sem], sems[n_sem:], False):
            d.start()
        refs[-1][...] = jnp.zeros((8, 128), F32)

    hbm_like = lambda arrs: [pltpu.HBM(a.shape, a.dtype) for a in arrs]
    res = pl.pallas_call(
        start_body, name=name + "_start", in_specs=[_HBM] * (2 * n) + [pl.BlockSpec(memory_space=pl.ANY)],
        out_specs=[_SEM] * (2 * n_sem) + [_HBM] * (2 * n) + [pl.BlockSpec(memory_space=pltpu.VMEM)],
        out_shape=[pltpu.SemaphoreType.DMA(())] * (2 * n_sem)
        + hbm_like(srcs) + hbm_like(lands) + [jax.ShapeDtypeStruct((8, 128), F32)],
        input_output_aliases={i: 2 * n_sem + i for i in range(2 * n)},
        compiler_params=pltpu.CompilerParams(has_side_effects=_EFFECT),
    )(*[pltpu.with_memory_space_constraint(a, pltpu.HBM) for a in srcs], *lands, after)
    sems, thru, token = res[:2 * n_sem], res[2 * n_sem:2 * n_sem + 2 * n], res[-1]

    def wait(after):
        def wait_body(*refs):
            s = refs[2 * n:2 * n + 2 * n_sem]
            for d in descriptors(refs[:n], refs[n:2 * n], s[:n_sem], s[n_sem:], True):
                d.wait_send()
                d.wait_recv()

        out = pl.pallas_call(
            wait_body, name=name + "_wait",
            in_specs=[_HBM] * (2 * n) + [_SEM] * (2 * n_sem) + [pl.BlockSpec(memory_space=pl.ANY)],
            out_specs=[_HBM] * (2 * n), out_shape=hbm_like(srcs) + hbm_like(lands),
            input_output_aliases={i: i for i in range(2 * n)},
            compiler_params=pltpu.CompilerParams(has_side_effects=_EFFECT),
        )(*thru, *sems, after)
        return out[:n], out[n:]

    return token, wait


def _gather_copies(a, src, land, x, y, c, chips):
    peers = [(x, y, 1 - c)] + [(*ch, c) for ch in chips]
    me = 4 * x + 2 * y + c
    return [(src, land.at[me], to, land.at[4 * to[0] + 2 * to[1] + to[2]]) for to in peers]


def _to_all_copies(a, src, land, x, y, c, chips):
    peers = [(x, y, 1 - c)] + [(*ch, cc) for ch in chips for cc in (c, 1 - c)]
    me = 4 * x + 2 * y + c
    return [(src, land.at[me], to, land.at[4 * to[0] + 2 * to[1] + to[2]]) for to in peers]


def _chips_copies(a, src, land, x, y, c, chips):
    return [(src.at[2 * ch[0] + ch[1]], land.at[j], (*ch, c), land.at[j]) for j, ch in enumerate(chips)]


def _gather_forward(name, arrs):
    n = len(arrs)

    def body(*refs):
        ins, outs = refs[:n], refs[n:2 * n]
        send, recv = refs[2 * n:]
        x, y, c, chips = _place()
        sends, recvs = [], []
        for a in range(n):
            for j, ch in enumerate(chips):
                mine, theirs = 4 * ch[0] + 2 * ch[1] + c, 4 * ch[0] + 2 * ch[1] + 1 - c
                sends.append(pltpu.make_async_remote_copy(src_ref=ins[a].at[mine], dst_ref=outs[a].at[mine],
                                                          send_sem=send.at[a, j], recv_sem=recv.at[a, j],
                                                          device_id=(x, y, 1 - c), device_id_type=MESH))
                recvs.append(pltpu.make_async_remote_copy(src_ref=ins[a].at[theirs], dst_ref=outs[a].at[theirs],
                                                          send_sem=send.at[a, j], recv_sem=recv.at[a, j],
                                                          device_id=(x, y, 1 - c), device_id_type=MESH))
        for d in sends:
            d.start()
        for s, r in zip(sends, recvs):
            s.wait_send()
            r.wait_recv()

    return pl.pallas_call(
        body, name=name, in_specs=[_HBM] * n, out_specs=[_HBM] * n,
        out_shape=[jax.ShapeDtypeStruct(a.shape, a.dtype) for a in arrs],
        input_output_aliases={i: i for i in range(n)},
        scratch_shapes=[pltpu.SemaphoreType.DMA((n, 3))] * 2,
    )(*arrs)


def _blocks_of(r, c):
    if r * c * 4 <= (2 << 20):
        return r, c
    if r % 128 == 0:
        return 128, c
    return r, 256


def _pair_sum(name, own, got, core):
    _, r, c = got.shape
    rb, cb = _blocks_of(r, c)

    def body(s_ref, a_ref, b_ref, o_ref):
        o_ref[...] = (a_ref[...].astype(F32) + b_ref[...].astype(F32)).astype(o_ref.dtype)

    return pl.pallas_call(
        body, name=name, out_shape=jax.ShapeDtypeStruct((4, r, c), got.dtype),
        grid_spec=pltpu.PrefetchScalarGridSpec(
            num_scalar_prefetch=1, grid=(4, r // rb, c // cb),
            in_specs=[pl.BlockSpec((None, rb, cb), lambda k, i, j, s: (2 * k + s[0], i, j)),
                      pl.BlockSpec((None, rb, cb), lambda k, i, j, s: (k, i, j))],
            out_specs=pl.BlockSpec((None, rb, cb), lambda k, i, j, s: (k, i, j))),
        compiler_params=_params(3, 24 << 20),
    )(core, _in_hbm(own), _in_hbm(got))


def _adamw_math(w, g, m, v):
    m = B1 * m + (1.0 - B1) * g
    v = B2 * v + (1.0 - B2) * (g * g)
    m_hat = m / (1.0 - B1 ** STEP)
    v_hat = v / (1.0 - B2 ** STEP)
    return -LR * (m_hat / (jnp.sqrt(v_hat) + ADAM_EPS) + WD * w), m, v


def _adamw_big(name, own, got, chip, w, m, v):
    r, c = w.shape
    rb, cb = _blocks_of(r, c)

    def body(s_ref, o_ref, p_ref, w_ref, m_ref, v_ref, g_ref, d_ref, nm_ref, nv_ref):
        g = o_ref[...].astype(F32)
        for k in range(3):
            g = g + p_ref[k].astype(F32)
        g_ref[...] = g
        d_ref[...], nm_ref[...], nv_ref[...] = _adamw_math(w_ref[...], g, m_ref[...], v_ref[...])

    blk = pl.BlockSpec((rb, cb), lambda i, j, s: (i, j))
    return pl.pallas_call(
        body, name=name, out_shape=[jax.ShapeDtypeStruct((r, c), F32)] * 4,
        grid_spec=pltpu.PrefetchScalarGridSpec(
            num_scalar_prefetch=1, grid=(r // rb, c // cb),
            in_specs=[pl.BlockSpec((None, rb, cb), lambda i, j, s: (s[0], i, j)),
                      pl.BlockSpec((3, rb, cb), lambda i, j, s: (0, i, j)), blk, blk, blk],
            out_specs=[blk] * 4),
        compiler_params=_params(2, 40 << 20),
    )(chip, *map(_in_hbm, (own, got, w, m, v)))


def _adamw_multi(name, items, sums=(), row_blocks=1, own=(), dev=None):
    n, ns = len(items), len(sums)
    n_own = len(own)
    assert n_own in (0, n + ns)

    def body(*refs):
        dev_ref = refs[0] if n_own else None
        refs = refs[1:] if n_own else refs
        owns, refs = refs[:n_own], refs[n_own:]
        ins, outs = refs[:4 * n + ns], refs[4 * n + ns:]

        def total(ref, t):
            g = None
            for d in range(ref.shape[0]):
                part = jnp.where(dev_ref[0] == d, owns[t][...], ref[d]) if n_own else ref[d]
                g = part.astype(F32) if g is None else g + part.astype(F32)
            return g

        for t in range(n):
            p_ref, w_ref, m_ref, v_ref = ins[4 * t:4 * t + 4]
            g = total(p_ref, t)
            outs[4 * t][...] = g
            outs[4 * t + 1][...], outs[4 * t + 2][...], outs[4 * t + 3][...] = _adamw_math(
                w_ref[...], g, m_ref[...], v_ref[...])
        for t in range(ns):
            outs[4 * n + t][...] = total(ins[4 * n + t], n + t)

    def spec(shape, lead):
        blk = (shape[0] // row_blocks,) + tuple(shape[1:])
        nd = len(shape)
        if lead:
            return pl.BlockSpec((lead,) + blk, lambda i: (0, i) + (0,) * (nd - 1))
        return pl.BlockSpec(blk, lambda i: (i,) + (0,) * (nd - 1))

    operands, in_specs, out_specs, out_shape = [], [], [], []
    if n_own:
        operands += [dev] + list(own)
        in_specs += [pl.BlockSpec(memory_space=pltpu.SMEM)] + [spec(o.shape, 0) for o in own]
    for parts, w, m, v in items:
        assert parts.shape[1:] == w.shape == m.shape == v.shape, (name, parts.shape, w.shape)
        operands += [parts, w, m, v]
        in_specs += [spec(w.shape, parts.shape[0])] + [spec(w.shape, 0)] * 3
        out_specs += [spec(w.shape, 0)] * 4
        out_shape += [jax.ShapeDtypeStruct(w.shape, F32)] * 4
    for parts in sums:
        operands.append(parts)
        in_specs.append(spec(parts.shape[1:], parts.shape[0]))
        out_specs.append(spec(parts.shape[1:], False))
        out_shape.append(jax.ShapeDtypeStruct(parts.shape[1:], F32))
    return pl.pallas_call(
        body, name=name, grid=(row_blocks,), in_specs=in_specs, out_specs=out_specs, out_shape=out_shape,
        compiler_params=_params(1, 56 << 20),
    )(*operands)


def kernel(x, meta_tokens, mix_norm, w_in, lam_re, lam_im, log_dt, b_re, b_im, c_re, c_im, d_skip, w_glu, b_glu, q_a_norm, w_q_b, kv_a_norm, w_kv_b, out_norm_ssm, out_norm_attn, w_out, ffn_norm, w_up, conv_w, conv_b, w_down, final_norm, loss_target, m_meta_tokens, m_mix_norm, m_w_in, m_lam_re, m_lam_im, m_log_dt, m_b_re, m_b_im, m_c_re, m_c_im, m_d_skip, m_w_glu, m_b_glu, m_q_a_norm, m_w_q_b, m_kv_a_norm, m_w_kv_b, m_out_norm_ssm, m_out_norm_attn, m_w_out, m_ffn_norm, m_w_up, m_conv_w, m_conv_b, m_w_down, m_final_norm, v_meta_tokens, v_mix_norm, v_w_in, v_lam_re, v_lam_im, v_log_dt, v_b_re, v_b_im, v_c_re, v_c_im, v_d_skip, v_w_glu, v_b_glu, v_q_a_norm, v_w_q_b, v_kv_a_norm, v_w_kv_b, v_out_norm_ssm, v_out_norm_attn, v_w_out, v_ffn_norm, v_w_up, v_conv_w, v_conv_b, v_w_down, v_final_norm):
    given = dict(locals())
    w = {n: given[n] for n in WEIGHTS}
    m = {n: given['m_' + n] for n in WEIGHTS}
    v = {n: given['v_' + n] for n in WEIGHTS}
    dev = 4 * lax.axis_index("x") + 2 * lax.axis_index("y") + lax.axis_index("c")

    shard = {
        'w_in': jnp.pad(w_in[0], ((0, 0), (0, D_IN_PAD - D_IN))),
        'w_glu': w_glu[0],
        'w_q_b': jnp.pad(w_q_b[0], ((0, 0), (0, HEAD_PAD - QK_NOPE - QK_ROPE))),
        'w_kv_b': w_kv_b[0],
        'w_out': w_out[0],
        'w_up': w_up[0],
        'w_down': w_down[0],
    }
    core = lax.axis_index("c").astype(jnp.int32).reshape(1)
    chip = (2 * lax.axis_index("x") + lax.axis_index("y")).astype(jnp.int32).reshape(1)
    early, late = ['w_in', 'w_glu', 'w_q_b', 'w_kv_b'], ['w_out', 'w_up', 'w_down']
    shard = {n: a.astype(BF16) for n, a in shard.items()}

    gathered = _all_gather("gather_early", [shard[n] for n in early]
                           + [meta_tokens, jnp.pad(conv_w[0], ((0, 5), (0, 0)))])
    wf = dict(zip(early, gathered[:len(early)]))
    for n in ('w_in', 'w_glu'):
        wf[n] = wf[n].reshape(-1, wf[n].shape[-1])
    meta = gathered[-2].transpose(1, 0, 2).reshape(N_META, D)
    wf['conv_w'] = gathered[-1][:, :3].transpose(1, 0, 2).reshape(3, D_FF)
    gather_token, wait_late = _split_exchange("gather_late", [shard[n] for n in late],
                                              [(N_DEV,) + shard[n].shape for n in late], _gather_copies, 4,
                                              after=gathered[0])

    def late_weights(after):
        mine, landed = wait_late(after)
        full = _gather_forward("gather_late_forward",
                               [lax.dynamic_update_slice(a, s[None], (dev, 0, 0)) for s, a in zip(mine, landed)])
        return full[0].reshape(-1, D), full[1], full[2].reshape(-1, D)

    pending = []

    def send_grads(names, arrs):
        got = _exchange_cores("reduce_cores_" + names[0], arrs)
        sums = [_pair_sum("sum_cores_" + n, a, g, core) for n, a, g in zip(names, arrs, got)]
        token, wait = _split_exchange("reduce_chips_" + names[0], sums, [(3,) + s.shape[1:] for s in sums],
                                      _chips_copies, 3, after=sums[-1])
        pending.append((names, wait))
        return token

    small_sent = []
    s5_bc = ['b_re', 'b_im', 'c_re', 'c_im']

    def send_small(sm, dcw4, loss_part):
        names = [n for n in SMALL if n in sm]
        arrs = [sm[n].reshape(-1, 128).astype(BF16) if n in s5_bc else sm[n] for n in names] + [dcw4, loss_part]
        token, wait = _split_exchange("gather_small", arrs, [(N_DEV,) + a.shape for a in arrs], _to_all_copies, 7,
                                      after=arrs[-1], all_of=lambda land: land.at[pl.ds(0, N_DEV - 1)])
        small_sent.append((names, wait))
        return token

    small = {n: w[n] for n in SMALL}
    small['mix_norm'] = mix_norm + gather_token[0:1, 0:1]
    grad_x, dmeta, d_mix_norm, big = _local_step(x[0], loss_target[0], meta, small, wf, late_weights, send_grads,
                                                  send_small)
    grads, deltas, new_m, new_v = {}, {}, {}, {}

    def keep(n, outs):
        grads[n], deltas[n], new_m[n], new_v[n] = (o.reshape(w[n].shape) for o in outs)

    rest = list(big)
    from_core = _exchange_cores("reduce_cores", [big[n] for n in rest])
    chip_sums = [_pair_sum("sum_cores_" + n, big[n], got, core) for n, got in zip(rest, from_core)]
    (mix_parts, meta_parts), from_chips = _all_gather("reduce_chips_gather_last", [d_mix_norm, dmeta],
                                                      chip_blocks=chip_sums)
    reduced = list(zip(rest, chip_sums, from_chips))
    for names, wait in pending:
        reduced += zip(names, *wait(from_chips[0]))
    for n, own, got in reduced:
        if n == 'w_up':
            outs = _adamw_big("adamw_" + n, own, got, chip, w[n][0].T, m[n][0].T, v[n][0].T)
            keep(n, [o.T for o in outs])
            continue
        cols = w[n].shape[2]
        keep(n, _adamw_big("adamw_" + n, own[:, :, :cols], got[:, :, :cols], chip, w[n][0], m[n][0], v[n][0]))

    outs = _adamw_multi("adamw_last", [(mix_parts, mix_norm, m['mix_norm'], v['mix_norm'])], sums=[meta_parts])
    keep('mix_norm', outs[:4])
    g_meta = outs[4]
    names, wait = small_sent[0]
    mine, landed = wait(grads['w_up'])
    parts, own = dict(zip(names + ['conv_w', 'loss'], landed)), dict(zip(names + ['conv_w', 'loss'], mine))
    wide = [n for n in names if n not in s5_bc]
    summed = ['conv_w', 'loss'] + s5_bc
    outs = _adamw_multi("adamw_small",
                        [(parts[n],) + tuple(t[n].reshape(own[n].shape) for t in (w, m, v)) for n in wide],
                        sums=[parts[n] for n in summed], own=[own[n] for n in wide + summed],
                        dev=dev.astype(jnp.int32).reshape(1))
    for i, n in enumerate(wide):
        keep(n, outs[4 * i:4 * i + 4])
    g_cw4, loss = outs[-6:-4]
    g_bc = dict(zip(s5_bc, outs[-4:]))
    for n in ('b_re', 'b_im'):
        g_bc[n] = g_bc[n].reshape(SSM_GROUP, N_GROUPS, SSM_STATE).transpose(1, 2, 0).reshape(-1, SSM_GROUP)
    outs = _adamw_multi("adamw_s5_bc", [(g_bc[n].reshape((1, -1, w[n].shape[-1])),)
                                        + tuple(t[n].reshape(-1, w[n].shape[-1]) for t in (w, m, v)) for n in s5_bc],
                        row_blocks=8)
    for i, n in enumerate(s5_bc):
        keep(n, outs[4 * i:4 * i + 4])

    g_meta = lax.dynamic_slice(g_meta, (0, dev * (D // N_DEV)), (N_META, D // N_DEV))
    g_conv = lax.dynamic_slice(g_cw4.transpose(1, 0, 2).reshape(3, D_FF), (0, dev * (D_FF // N_DEV)),
                               (3, D_FF // N_DEV))
    rows8 = lambda a: jnp.pad(a.reshape(3, D_FF // N_DEV), ((0, 5), (0, 0)))
    outs = _adamw_multi("adamw_cols", [(g_meta[None], meta_tokens, m['meta_tokens'], v['meta_tokens']),
                                       (rows8(g_conv)[None], rows8(conv_w), rows8(m['conv_w']), rows8(v['conv_w']))])
    keep('meta_tokens', outs[:4])
    keep('conv_w', [o[:3] for o in outs[4:]])

    return (loss[0, 0], grad_x[None], *[grads[n] for n in WEIGHTS], *[deltas[n] for n in WEIGHTS],
            *[new_m[n] for n in WEIGHTS], *[new_v[n] for n in WEIGHTS])
```

```python
import functools
import math

import jax
import jax.numpy as jnp
from jax import lax
from jax.experimental import pallas as pl
from jax.experimental.pallas import tpu as pltpu

F32, BF16 = jnp.float32, jnp.bfloat16

N_DEV = 8
N_META, SEQ, D = 16, 2048, 2048
T = N_META + SEQ
TR = 688
EPS = 1e-6
D_SSM, Q_LORA, KV_LORA, QK_ROPE = 1024, 512, 256, 64
D_IN = D_SSM + Q_LORA + KV_LORA + QK_ROPE
D_IN_PAD = 1920
N_HEADS, QK_NOPE, V_HEAD = 8, 128, 128
HEAD_PAD = 256
SM_SCALE = 1.0 / math.sqrt(QK_NOPE + QK_ROPE)
CHUNK = 64
N_GROUPS, SSM_GROUP, SSM_STATE = 64, 16, 64
N_SEG = 8
SEG = T // N_SEG
GQ = 8
D_FF = 5504
FF_PIECE = 1376
ROPE_BASE = 10000.0
LR, B1, B2, ADAM_EPS, WD, STEP = 0.001, 0.9, 0.999, 1e-08, 0.01, 10
VMEM_CAP = 60 * 1024 * 1024
MESH = pl.DeviceIdType.MESH

WEIGHTS = ['meta_tokens', 'mix_norm', 'w_in', 'lam_re', 'lam_im', 'log_dt', 'b_re', 'b_im', 'c_re', 'c_im',
           'd_skip', 'w_glu', 'b_glu', 'q_a_norm', 'w_q_b', 'kv_a_norm', 'w_kv_b', 'out_norm_ssm',
           'out_norm_attn', 'w_out', 'ffn_norm', 'w_up', 'conv_w', 'conv_b', 'w_down', 'final_norm']
BIG = ['w_in', 'w_glu', 'w_q_b', 'w_kv_b', 'w_out', 'w_up', 'w_down']
SMALL = [n for n in WEIGHTS if n not in BIG and n not in ('meta_tokens', 'conv_w')]


def _nbytes(shape, dtype):
    return math.prod(shape) * jnp.dtype(dtype).itemsize


def _in_hbm(a):
    return pltpu.with_memory_space_constraint(a, pltpu.HBM) if a.size * a.dtype.itemsize >= (1 << 20) else a


def _params(sem, need):
    return pltpu.CompilerParams(dimension_semantics=("arbitrary",) * sem,
                                vmem_limit_bytes=int(min(VMEM_CAP, max(need, 16 * 1024 * 1024))))


_DIMS = {'nn': (((1,), (0,)), ((), ())), 'nt': (((1,), (1,)), ((), ())), 'tn': (((0,), (0,)), ((), ()))}


def _mm(name, a, b, mode, tm, tn, tk, out_dtype, stack=None, res=None):
    sa, sb, so = stack in ('a_out', 'ab_red'), stack in ('b_out', 'ab_red'), stack in ('a_out', 'b_out')
    nj = a.shape[0] if sa else (b.shape[0] if sb else 1)
    a2, b2 = a.shape[-2:], b.shape[-2:]
    if mode == 'nn':
        (m, k), (k2, n) = a2, b2
    elif mode == 'nt':
        (m, k), (n, k2) = a2, b2
    else:
        (k, m), (k2, n) = a2, b2
    assert k == k2 and m % tm == 0 and n % tn == 0 and k % tk == 0, (name, a.shape, b.shape)
    n_jo, n_jr, n_k = (nj if so else 1), (nj if stack == 'ab_red' else 1), k // tk
    grid = (n_jo, m // tm, n // tn, n_jr, n_k)
    a_blk, a_idx = ((tk, tm), lambda i, j, kk: (kk, i)) if mode == 'tn' else ((tm, tk), lambda i, j, kk: (i, kk))
    b_blk, b_idx = ((tn, tk), lambda i, j, kk: (j, kk)) if mode == 'nt' else ((tk, tn), lambda i, j, kk: (kk, j))

    def spec(blk, idx, stacked, on_out):
        if not stacked:
            return pl.BlockSpec(blk, lambda jo, i, j, jr, kk: idx(i, j, kk))
        return pl.BlockSpec((None,) + blk, lambda jo, i, j, jr, kk: ((jo if on_out else jr),) + idx(i, j, kk))

    o_idx = lambda i, j, kk: (i, j)
    in_specs = [spec(a_blk, a_idx, sa, stack == 'a_out'), spec(b_blk, b_idx, sb, stack == 'b_out')]
    operands = [a, b]
    if res is not None:
        in_specs.append(spec((tm, tn), o_idx, False, False))
        operands.append(res)
    n_red = n_jr * n_k
    dims = _DIMS[mode]

    def body(*refs):
        a_ref, b_ref = refs[0], refs[1]
        res_ref = refs[2] if res is not None else None
        o_ref = refs[3] if res is not None else refs[2]
        part = lax.dot_general(a_ref[...].astype(BF16), b_ref[...].astype(BF16), dims,
                               preferred_element_type=F32)

        def finish(total):
            if res_ref is not None:
                total = total + res_ref[...]
            o_ref[...] = total.astype(o_ref.dtype)

        if n_red == 1:
            finish(part)
        else:
            acc_ref = refs[-1]
            step = pl.program_id(3) * n_k + pl.program_id(4)

            @pl.when(step == 0)
            def _():
                acc_ref[...] = part

            @pl.when(step > 0)
            def _():
                acc_ref[...] += part

            @pl.when(step == n_red - 1)
            def _():
                finish(acc_ref[...])

    out_shape = ((nj,) if so else ()) + (m, n)
    need = 2 * (_nbytes(a_blk, a.dtype) + _nbytes(b_blk, b.dtype) + _nbytes((tm, tn), out_dtype)
                + (_nbytes((tm, tn), F32) if res is not None else 0)) + 3 * _nbytes((tm, tn), F32) + (4 << 20)
    return pl.pallas_call(
        body, name=name, grid=grid, in_specs=in_specs, out_specs=spec((tm, tn), o_idx, so, True),
        out_shape=jax.ShapeDtypeStruct(out_shape, out_dtype),
        scratch_shapes=[pltpu.VMEM((tm, tn), F32)] if n_red > 1 else [],
        compiler_params=_params(5, need),
    )(*map(_in_hbm, operands))


def _rowwise(name, fn, row_ins, const_ins, row_outs, acc_outs, tr=TR):
    n_ri, n_ci, n_ro, n_ao = len(row_ins), len(const_ins), len(row_outs), len(acc_outs)

    def body(*refs):
        ri, ci = refs[:n_ri], refs[n_ri:n_ri + n_ci]
        ro, ao = refs[n_ri + n_ci:n_ri + n_ci + n_ro], refs[n_ri + n_ci + n_ro:]
        i = pl.program_id(0)
        outs = fn(i * tr, *[r[...] for r in ri], *[r[...] for r in ci])
        for r, v in zip(ro, outs[:n_ro]):
            r[...] = v.astype(r.dtype)
        if n_ao:
            @pl.when(i == 0)
            def _():
                for r, v in zip(ao, outs[n_ro:]):
                    r[...] = v

            @pl.when(i > 0)
            def _():
                for r, v in zip(ao, outs[n_ro:]):
                    r[...] += v

    in_specs = [pl.BlockSpec((tr, w), functools.partial(lambda cb, i: (i, cb), cb)) for _, w, cb in row_ins]
    in_specs += [pl.BlockSpec(c.shape, functools.partial(lambda nd, i: (0,) * nd, c.ndim)) for c in const_ins]
    out_specs = [pl.BlockSpec((tr, w), lambda i: (i, 0)) for w, _ in row_outs]
    out_specs += [pl.BlockSpec((1, w), lambda i: (0, 0)) for w in acc_outs]
    out_shape = [jax.ShapeDtypeStruct((T, w), dt) for w, dt in row_outs]
    out_shape += [jax.ShapeDtypeStruct((1, w), F32) for w in acc_outs]
    need = 2 * (sum(_nbytes((tr, w), a.dtype) for a, w, _ in row_ins) + sum(_nbytes((tr, w), dt) for w, dt in row_outs))
    need = 3 * need + (8 << 20)
    return pl.pallas_call(
        body, name=name, grid=(T // tr,), in_specs=in_specs, out_specs=out_specs, out_shape=out_shape,
        compiler_params=_params(1, need),
    )(*[_in_hbm(a) for a, _, _ in row_ins], *const_ins)


def _rms_fwd_fn(row0, h, g):
    h = h.astype(F32)
    r = lax.rsqrt(jnp.mean(h * h, axis=-1, keepdims=True) + EPS)
    return (h * r) * g, r


def _rms_bwd(dxn, h, r, g):
    xhat = h * r
    dxg = dxn * g
    dh = r * (dxg - xhat * jnp.mean(dxg * xhat, axis=-1, keepdims=True))
    return dh, jnp.sum(dxn * xhat, axis=0, keepdims=True)


def _rms_bwd_fn(row0, dxn, h, r, g):
    return _rms_bwd(dxn.astype(F32), h.astype(F32), r, g)


def _rms_bwd_res_fn(row0, dxn, h, r, res, g):
    dh, dg = _rms_bwd(dxn.astype(F32), h.astype(F32), r, g)
    return dh + res, dg


def _gelu_fn(row0, y):
    return (jax.nn.gelu(y),)


def _glu_fwd_fn(row0, y, gl, b, gain):
    ya = jax.nn.gelu(y) * jax.nn.sigmoid(gl + b)
    return _rms_fwd_fn(row0, ya, gain)


def _glu_bwd_fn(row0, dyan, y, gl, ra, b, gain):
    g = jax.nn.gelu(y)
    s = jax.nn.sigmoid(gl + b)
    dya, dgain = _rms_bwd(dyan, g * s, ra, gain)
    dgl = dya * g * s * (1.0 - s)
    return dgl, dya * s, dgain, jnp.sum(dgl, axis=0, keepdims=True)


def _gelu_bwd_fn(row0, dg, y):
    return (jax.vjp(jax.nn.gelu, y)[1](dg)[0],)


def _loss_fn(row0, h, tgt, gain):
    r = lax.rsqrt(jnp.mean(h * h, axis=-1, keepdims=True) + EPS)
    xhat = h * r
    rows = row0 + lax.broadcasted_iota(jnp.int32, h.shape, 0)
    err = jnp.where(rows >= N_META, xhat * gain - tgt, 0.0)
    loss = jnp.full((1, 128), jnp.sum(err * err) * (0.5 / D), F32)
    dh, dgain = _rms_bwd(err * (1.0 / D), h, r, gain)
    return dh, dgain, loss


def _rope_tables():
    pos = jnp.arange(T, dtype=jnp.int32)
    inv_freq = 1.0 / (ROPE_BASE ** (jnp.arange(0, QK_ROPE, 2, dtype=F32) / QK_ROPE))
    ang = pos.astype(F32)[:, None] * inv_freq[None, :]
    cos, sin, z32, z64 = jnp.cos(ang), jnp.sin(ang), jnp.zeros((T, 32), F32), jnp.zeros((T, 64), F32)
    return (jnp.concatenate([cos, cos, z64], 1), jnp.concatenate([-sin, z32, z64], 1),
            jnp.concatenate([z32, sin, z64], 1))


def _rot(x, cc, s1, s2):
    return x * cc + pltpu.roll(x, 96, 1) * s1 + pltpu.roll(x, 32, 1) * s2


def _derot(d, cc, s1, s2):
    return d * cc + pltpu.roll(d * s1, 32, 1) + pltpu.roll(d * s2, 96, 1)


def _chunk_of(pos):
    return jnp.where(pos < N_META, 0, 1 + ((pos - N_META) >> 6))


def _key_len(t):
    last_chunk = 1 + ((t + 1) * TR - 1 - N_META) // CHUNK
    return min(T, pl.cdiv(N_META + last_chunk * CHUNK, 128) * 128)


def _attn_probs(t, q_ref, kv_ref, kpe_ref, tq, tk):
    kl = _key_len(t)
    q = q_ref[...]
    qn = q[:, :QK_NOPE].astype(BF16)
    qp = _rot(q[:, QK_NOPE:], tq[0][...], tq[1][...], tq[2][...]).astype(BF16)
    kn = kv_ref[:kl, :QK_NOPE].astype(BF16)
    v = kv_ref[:kl, QK_NOPE:].astype(BF16)
    kp = _rot(kpe_ref[:kl, :], tk[0][:kl, :], tk[1][:kl, :], tk[2][:kl, :]).astype(BF16)
    s = lax.dot_general(qn, kn, _DIMS['nt'], preferred_element_type=F32)
    s = s + lax.dot_general(qp, kp, _DIMS['nt'], preferred_element_type=F32)
    qc = _chunk_of(t * TR + lax.broadcasted_iota(jnp.int32, (TR, 1), 0))
    kc = _chunk_of(lax.broadcasted_iota(jnp.int32, (1, kl), 1))
    s = jnp.where(kc <= qc, s * SM_SCALE, jnp.finfo(F32).min)
    p = jnp.exp(s - jnp.max(s, axis=-1, keepdims=True))
    p = p * (1.0 / jnp.sum(p, axis=-1, keepdims=True))
    return qn, qp, kn, kp, v, p


def _per_q_tile(fn):
    for t in range(T // TR):
        pl.when(pl.program_id(1) == t)(functools.partial(fn, t))


def _attn_specs(z, tabs):
    q_spec = pl.BlockSpec((None, TR, HEAD_PAD), lambda h, i: (h, i, 0))
    kv_spec = pl.BlockSpec((None, T, HEAD_PAD), lambda h, i: (h, 0, 0))
    kpe_spec = pl.BlockSpec((T, 128), lambda h, i: (0, (D_IN - QK_ROPE) // 128))
    tq = [pl.BlockSpec((TR, 128), lambda h, i: (i, 0))] * 3
    tk = [pl.BlockSpec((T, 128), lambda h, i: (0, 0))] * 3
    return [q_spec, kv_spec, kpe_spec] + tq + tk


def _attn_fwd(q, kv, z, tabs):
    def body(q_ref, kv_ref, kpe_ref, c1, c2, c3, k1, k2, k3, o_ref):
        def tile(t):
            _, _, _, _, v, p = _attn_probs(t, q_ref, kv_ref, kpe_ref, (c1, c2, c3), (k1, k2, k3))
            o_ref[...] = jnp.dot(p.astype(BF16), v, preferred_element_type=F32)

        _per_q_tile(tile)

    return pl.pallas_call(
        body, name="attn_fwd", grid=(N_HEADS, T // TR), in_specs=_attn_specs(z, tabs),
        out_specs=pl.BlockSpec((TR, V_HEAD), lambda h, i: (i, h)),
        out_shape=jax.ShapeDtypeStruct((T, N_HEADS * V_HEAD), F32),
        compiler_params=_params(2, 48 << 20),
    )(_in_hbm(q), _in_hbm(kv), _in_hbm(z), *tabs, *tabs)


def _attn_bwd(q, kv, z, tabs, dyb):
    def body(q_ref, kv_ref, kpe_ref, c1, c2, c3, k1, k2, k3, do_ref, dq_ref, dkv_ref, dkpe_ref):
        @pl.when(pl.program_id(1) == 0)
        def _():
            dkv_ref[...] = jnp.zeros(dkv_ref.shape, F32)
            dkpe_ref[...] = jnp.zeros(dkpe_ref.shape, F32)

        def tile(t):
            kl = _key_len(t)
            qn, qp, kn, kp, v, p = _attn_probs(t, q_ref, kv_ref, kpe_ref, (c1, c2, c3), (k1, k2, k3))
            do = do_ref[...].astype(BF16)
            dp = lax.dot_general(do, v, _DIMS['nt'], preferred_element_type=F32)
            ds = (p * (dp - jnp.sum(p * dp, axis=-1, keepdims=True)) * SM_SCALE).astype(BF16)
            pb = p.astype(BF16)
            dq_ref[:, :QK_NOPE] = jnp.dot(ds, kn, preferred_element_type=F32).astype(dq_ref.dtype)
            dqp = _derot(jnp.dot(ds, kp, preferred_element_type=F32), c1[...], c2[...], c3[...])
            dq_ref[:, QK_NOPE:] = dqp.astype(dq_ref.dtype)
            dkv_ref[:kl, :QK_NOPE] += lax.dot_general(ds, qn, _DIMS['tn'], preferred_element_type=F32)
            dkv_ref[:kl, QK_NOPE:] += lax.dot_general(pb, do, _DIMS['tn'], preferred_element_type=F32)
            dkpe_ref[:kl, :] += lax.dot_general(ds, qp, _DIMS['tn'], preferred_element_type=F32)

        _per_q_tile(tile)

    return pl.pallas_call(
        body, name="attn_bwd", grid=(N_HEADS, T // TR),
        in_specs=_attn_specs(z, tabs) + [pl.BlockSpec((TR, V_HEAD), lambda h, i: (i, h))],
        out_specs=[pl.BlockSpec((None, TR, HEAD_PAD), lambda h, i: (h, i, 0)),
                   pl.BlockSpec((None, T, HEAD_PAD), lambda h, i: (h, 0, 0)),
                   pl.BlockSpec((None, T, 128), lambda h, i: (h, 0, 0))],
        out_shape=[jax.ShapeDtypeStruct((N_HEADS, T, HEAD_PAD), BF16),
                   jax.ShapeDtypeStruct((N_HEADS, T, HEAD_PAD), F32),
                   jax.ShapeDtypeStruct((N_HEADS, T, 128), F32)],
        compiler_params=_params(2, 56 << 20),
    )(_in_hbm(q), _in_hbm(kv), _in_hbm(z), *tabs, *tabs, _in_hbm(dyb))


def _kpe_bwd(dkpe, tabs):
    def body(d_ref, c1, c2, c3, o_ref):
        d = d_ref[0]
        for h in range(1, N_HEADS):
            d = d + d_ref[h]
        o_ref[...] = _derot(d, c1[...], c2[...], c3[...]).astype(o_ref.dtype)

    tile = pl.BlockSpec((TR, 128), lambda i: (i, 0))
    return pl.pallas_call(
        body, name="kpe_bwd", grid=(T // TR,),
        in_specs=[pl.BlockSpec((N_HEADS, TR, 128), lambda i: (0, i, 0)), tile, tile, tile], out_specs=tile,
        out_shape=jax.ShapeDtypeStruct((T, 128), BF16), compiler_params=_params(1, 16 << 20),
    )(_in_hbm(dkpe), *tabs)


def _s5_prep_math(lam_re, lam_im, log_dt, bt_re, bt_im):
    dt = jnp.exp(log_dt)
    mag = jnp.exp(lam_re * dt)
    a_re, a_im = mag * jnp.cos(lam_im * dt), mag * jnp.sin(lam_im * dt)
    den = lam_re * lam_re + lam_im * lam_im
    w_re = ((a_re - 1.0) * lam_re + a_im * lam_im) / den
    w_im = (a_im * lam_re - (a_re - 1.0) * lam_im) / den
    return a_re, a_im, w_re[None] * bt_re - w_im[None] * bt_im, w_re[None] * bt_im + w_im[None] * bt_re


def _eye_groups():
    return (lax.broadcasted_iota(jnp.int32, (N_GROUPS, N_GROUPS), 0)
            == lax.broadcasted_iota(jnp.int32, (N_GROUPS, N_GROUPS), 1)).astype(F32)


def _row_to_col(row):
    return jnp.sum(_eye_groups() * row, axis=1, keepdims=True)


def _s5_prep(lam_re, lam_im, log_dt, bt_re, bt_im):
    def body(lr, li, ld, br, bi, ar, ai, bbr, bbi):
        ar[...], ai[...], bbr[...], bbi[...] = _s5_prep_math(lr[...], li[...], _row_to_col(ld[...]), br[...], bi[...])

    gp, cgp = jax.ShapeDtypeStruct((N_GROUPS, SSM_STATE), F32), jax.ShapeDtypeStruct(bt_re.shape, F32)
    return pl.pallas_call(body, name="s5_prep", out_shape=[gp, gp, cgp, cgp])(lam_re, lam_im, log_dt, bt_re, bt_im)


def _s5_prep_bwd(lam_re, lam_im, log_dt, bt_re, bt_im, da_re, da_im, dbb_re, dbb_im):
    def body(lr, li, ld, br, bi, dar, dai, dbr, dbi, o1, o2, o3, o4, o5):
        _, vjp = jax.vjp(_s5_prep_math, lr[...], li[...], _row_to_col(ld[...]), br[...], bi[...])
        o1[...], o2[...], dld, o4[...], o5[...] = vjp((dar[...], dai[...], dbr[...], dbi[...]))
        o3[...] = jnp.sum(_eye_groups() * dld, axis=0, keepdims=True)

    ins = (lam_re, lam_im, log_dt, bt_re, bt_im)
    return pl.pallas_call(body, name="s5_prep_bwd", out_shape=[jax.ShapeDtypeStruct(a.shape, F32) for a in ins])(
        *ins, da_re, da_im, dbb_re, dbb_im)


def _cmul(ar, ai, br, bi):
    return ar * br - ai * bi, ar * bi + ai * br


def _seg_rows(i):
    return pl.ds(pl.multiple_of(i * N_SEG, N_SEG), N_SEG)


def _scan(xr, xi, ar, ai, reverse):
    zero = jnp.zeros(ar.shape, F32)

    def local(j, carry):
        rows = _seg_rows(SEG - 1 - j if reverse else j)
        nr, ni = _cmul(ar, ai, *carry)
        nr, ni = nr + xr[rows, :], ni + xi[rows, :]
        xr[rows, :], xi[rows, :] = nr, ni
        return nr, ni

    er, ei = lax.fori_loop(0, SEG, local, (zero, zero))
    pr, pi = ar, ai
    for _ in range(8):
        pr, pi = _cmul(pr, pi, pr, pi)
    pr, pi = _cmul(*_cmul(pr, pi, ar, ai), ar, ai)
    row = lax.broadcasted_iota(jnp.int32, ar.shape, 0)
    edge, shift = (N_SEG - 1, N_SEG - 1) if reverse else (0, 1)
    hr, hi = zero, zero
    for _ in range(N_SEG - 1):
        tr_, ti_ = _cmul(pr, pi, hr, hi)
        hr = jnp.where(row == edge, 0.0, pltpu.roll(tr_ + er, shift, 0))
        hi = jnp.where(row == edge, 0.0, pltpu.roll(ti_ + ei, shift, 0))

    def fix(j, carry):
        rows = _seg_rows(SEG - 1 - j if reverse else j)
        xr[rows, :] += carry[0]
        xi[rows, :] += carry[1]
        return _cmul(ar, ai, *carry)

    lax.fori_loop(0, SEG, fix, _cmul(ar, ai, hr, hi))
    return hr, hi


def _split(x):
    hi = x.astype(BF16)
    return hi, (x - hi.astype(F32)).astype(BF16)


def _dot3(xs, ys, mode='nn'):
    d = lambda p, q: lax.dot_general(p, q, _DIMS[mode], preferred_element_type=F32)
    return d(xs[0], ys[0]) + (d(xs[1], ys[0]) + d(xs[0], ys[1]))


def _group_masks():
    ns = GQ * SSM_STATE
    rep = (lax.broadcasted_iota(jnp.int32, (SSM_STATE, ns), 1) % SSM_STATE
           == lax.broadcasted_iota(jnp.int32, (SSM_STATE, ns), 0)).astype(BF16)
    own = (lax.broadcasted_iota(jnp.int32, (128, ns), 0) // SSM_GROUP
           == lax.broadcasted_iota(jnp.int32, (128, ns), 1) // SSM_STATE)
    return rep, own


def _block_diag(x, rep, own):
    dot = lambda p: jnp.dot(p, rep, preferred_element_type=F32)
    return tuple(jnp.where(own, dot(part), 0.0).astype(BF16) for part in _split(x))


def _block_rows(full, rep, own):
    masked = _split(jnp.where(own, full, 0.0))
    dot = lambda p: lax.dot_general(p, rep, _DIMS['nt'], preferred_element_type=F32)
    return dot(masked[0]) + dot(masked[1])


def _s5_specs():
    col = pl.BlockSpec((T, 128), lambda q: (0, q))
    a_spec = pl.BlockSpec((None, 2, N_SEG, GQ * SSM_STATE), lambda q: (q, 0, 0, 0))
    rows = pl.BlockSpec((128, SSM_STATE), lambda q: (q, 0))
    d_spec = pl.BlockSpec((1, 128), lambda q: (0, q))
    return col, a_spec, rows, d_spec


def _s5_fwd(u, a8, bb_re, bb_im, c_re, c_im, dsk):
    def body(u_ref, a_ref, bre_ref, bim_ref, cre_ref, cim_ref, d_ref, y_ref, hre, him):
        u_ = u_ref[...]
        us = _split(u_)
        rep, own = _group_masks()
        hre[...] = _dot3(us, _block_diag(bre_ref[...], rep, own))
        him[...] = _dot3(us, _block_diag(bim_ref[...], rep, own))
        _scan(hre, him, a_ref[0], a_ref[1], reverse=False)
        y_ref[...] = (_dot3(_split(hre[...]), _block_diag(cre_ref[...], rep, own), 'nt')
                      - _dot3(_split(him[...]), _block_diag(cim_ref[...], rep, own), 'nt') + d_ref[...] * u_)

    col, a_spec, rows, d_spec = _s5_specs()
    return pl.pallas_call(
        body, name="s5_fwd", grid=(N_GROUPS // GQ,), in_specs=[col, a_spec, rows, rows, rows, rows, d_spec],
        out_specs=col, out_shape=jax.ShapeDtypeStruct((T, D_SSM), F32),
        scratch_shapes=[pltpu.VMEM((T, GQ * SSM_STATE), F32)] * 2,
        compiler_params=_params(1, 40 << 20),
    )(_in_hbm(u), a8, bb_re, bb_im, c_re, c_im, dsk)


def _s5_bwd(u, dy, a8, bb_re, bb_im, c_re, c_im, dsk):
    def body(u_ref, dy_ref, a_ref, bre_ref, bim_ref, cre_ref, cim_ref, d_ref,
             du_ref, dbre_ref, dbim_ref, dcre_ref, dcim_ref, da_ref, dd_ref, hre, him, lre, lim):
        u_, dy_ = u_ref[...], dy_ref[...]
        us, dys = _split(u_), _split(dy_)
        ar, ai = a_ref[0], a_ref[1]
        rep, own = _group_masks()
        bres, bims = _block_diag(bre_ref[...], rep, own), _block_diag(bim_ref[...], rep, own)
        hre[...] = _dot3(us, bres)
        him[...] = _dot3(us, bims)
        h0r, h0i = _scan(hre, him, ar, ai, reverse=False)
        lre[...] = _dot3(dys, _block_diag(cre_ref[...], rep, own))
        lim[...] = -_dot3(dys, _block_diag(cim_ref[...], rep, own))
        _scan(lre, lim, ar, -ai, reverse=True)

        def acc_da(i, carry):
            lr, li = lre[_seg_rows(i), :], lim[_seg_rows(i), :]
            pr, pi = hre[_seg_rows(i - 1), :], him[_seg_rows(i - 1), :]
            return carry[0] + lr * pr + li * pi, carry[1] + li * pr - lr * pi

        lr, li = lre[_seg_rows(0), :], lim[_seg_rows(0), :]
        dar, dai = lax.fori_loop(1, SEG, acc_da, (lr * h0r + li * h0i, li * h0r - lr * h0i))
        da_ref[0:1, :] = jnp.sum(dar, axis=0, keepdims=True)
        da_ref[1:2, :] = jnp.sum(dai, axis=0, keepdims=True)
        dot = lambda p, q, mode: lax.dot_general(p, q, _DIMS[mode], preferred_element_type=F32)
        lr1, li1 = lre[...].astype(BF16), lim[...].astype(BF16)
        du_ref[...] = dy_ * d_ref[...] + dot(lr1, bres[0], 'nt') + dot(li1, bims[0], 'nt')
        dbre_ref[...] = _block_rows(dot(us[0], lr1, 'tn'), rep, own)
        dbim_ref[...] = _block_rows(dot(us[0], li1, 'tn'), rep, own)
        dcre_ref[...] = _block_rows(dot(dys[0], hre[...].astype(BF16), 'tn'), rep, own)
        dcim_ref[...] = -_block_rows(dot(dys[0], him[...].astype(BF16), 'tn'), rep, own)
        dd_ref[...] = jnp.sum(dy_ * u_, axis=0, keepdims=True)

    col, a_spec, rows, d_spec = _s5_specs()
    nq, ns = N_GROUPS // GQ, GQ * SSM_STATE
    per_row = jax.ShapeDtypeStruct((N_GROUPS * SSM_GROUP, SSM_STATE), F32)
    return pl.pallas_call(
        body, name="s5_bwd", grid=(nq,),
        in_specs=[col, col, a_spec, rows, rows, rows, rows, d_spec],
        out_specs=[col, rows, rows, rows, rows, pl.BlockSpec((None, 2, ns), lambda q: (q, 0, 0)), d_spec],
        out_shape=[jax.ShapeDtypeStruct((T, D_SSM), F32), per_row, per_row, per_row, per_row,
                   jax.ShapeDtypeStruct((nq, 2, ns), F32), jax.ShapeDtypeStruct((1, D_SSM), F32)],
        scratch_shapes=[pltpu.VMEM((T, ns), F32)] * 4,
        compiler_params=_params(1, 56 << 20),
    )(_in_hbm(u), _in_hbm(dy), a8, bb_re, bb_im, c_re, c_im, dsk)


def _to_segments(a):
    return a.reshape(N_SEG, SEG, a.shape[-1]).transpose(1, 0, 2).reshape(T, a.shape[-1])


def _from_segments(a):
    return a.reshape(SEG, N_SEG, a.shape[-1]).transpose(1, 0, 2).reshape(T, a.shape[-1])


def _ff_specs():
    up_spec = pl.BlockSpec((2, None, T, 128), lambda p, cb: (0, p, 0, cb))
    piece = pl.BlockSpec((None, T, 128), lambda p, cb: (p, 0, cb))
    cw_spec = pl.BlockSpec((None, 3, 128), lambda p, cb: (p, 0, cb))
    cb_spec = pl.BlockSpec((None, 1, 128), lambda p, cb: (p, 0, cb))
    return up_spec, piece, cw_spec, cb_spec


def _conv_gate(gate, w, cb, pad_ref):
    pad_ref[0:8, :] = jnp.zeros((8, 128), F32)
    pad_ref[8:8 + T, :] = gate
    g1, g2 = pad_ref[pl.ds(7, T), :], pad_ref[pl.ds(6, T), :]
    return w[0:1, :] * g2 + w[1:2, :] * g1 + w[2:3, :] * gate + cb, g1, g2


def _ff_act(up, cw4, cb4):
    def body(up_ref, w_ref, b_ref, o_ref, pad_ref):
        gc, _, _ = _conv_gate(up_ref[0].astype(F32), w_ref[...], b_ref[...], pad_ref)
        o_ref[...] = (jax.nn.silu(gc) * up_ref[1].astype(F32)).astype(o_ref.dtype)

    up_spec, piece, cw_spec, cb_spec = _ff_specs()
    n_cb = pl.cdiv(FF_PIECE, 128)
    return pl.pallas_call(
        body, name="ff_act", grid=(4, n_cb), in_specs=[up_spec, cw_spec, cb_spec], out_specs=piece,
        out_shape=jax.ShapeDtypeStruct((4, T, FF_PIECE), BF16),
        scratch_shapes=[pltpu.VMEM((T + 8, 128), F32)],
        compiler_params=_params(2, 24 << 20),
    )(_in_hbm(up.reshape(2, 4, T, FF_PIECE)), cw4, cb4)


def _ff_act_bwd(up, dact, cw4, cb4):
    def body(up_ref, da_ref, w_ref, b_ref, dup_ref, dw_ref, db_ref, pad_ref, pad2_ref):
        gate, val, w = up_ref[0].astype(F32), up_ref[1].astype(F32), w_ref[...]
        gc, g1, g2 = _conv_gate(gate, w, b_ref[...], pad_ref)
        sg = jax.nn.sigmoid(gc)
        da = da_ref[...].astype(F32)
        dup_ref[1] = (da * gc * sg).astype(dup_ref.dtype)
        dgc = da * val * sg * (1.0 + gc * (1.0 - sg))
        db_ref[...] = jnp.sum(dgc, axis=0, keepdims=True)
        dw_ref[0:1, :] = jnp.sum(dgc * g2, axis=0, keepdims=True)
        dw_ref[1:2, :] = jnp.sum(dgc * g1, axis=0, keepdims=True)
        dw_ref[2:3, :] = jnp.sum(dgc * gate, axis=0, keepdims=True)
        pad2_ref[0:T, :] = dgc
        pad2_ref[T:T + 8, :] = jnp.zeros((8, 128), F32)
        d1, d2 = pad2_ref[pl.ds(1, T), :], pad2_ref[pl.ds(2, T), :]
        dup_ref[0] = (w[2:3, :] * dgc + w[1:2, :] * d1 + w[0:1, :] * d2).astype(dup_ref.dtype)

    up_spec, piece, cw_spec, cb_spec = _ff_specs()
    n_cb = pl.cdiv(FF_PIECE, 128)
    dup, dw, db = pl.pallas_call(
        body, name="ff_act_bwd", grid=(4, n_cb), in_specs=[up_spec, piece, cw_spec, cb_spec],
        out_specs=[up_spec, cw_spec, cb_spec],
        out_shape=[jax.ShapeDtypeStruct((2, 4, T, FF_PIECE), BF16), jax.ShapeDtypeStruct((4, 3, FF_PIECE), F32),
                   jax.ShapeDtypeStruct((4, 1, FF_PIECE), F32)],
        scratch_shapes=[pltpu.VMEM((T + 8, 128), F32)] * 2,
        compiler_params=_params(2, 32 << 20),
    )(_in_hbm(up.reshape(2, 4, T, FF_PIECE)), _in_hbm(dact), cw4, cb4)
    return dup.reshape(8, T, FF_PIECE), dw, db


def _local_step(x, tgt, meta, p, wf, late_weights, send_grads, send_small):
    sm = {}
    h0 = jnp.concatenate([meta, x], axis=0)
    tgt_pad = jnp.concatenate([jnp.zeros((N_META, D), F32), tgt], axis=0)
    tabs = _rope_tables()
    row = lambda a: a.reshape(1, -1)

    xn, r0 = _rowwise("rms_mix", _rms_fwd_fn, [(h0, D, 0)], [row(p['mix_norm'])], [(D, BF16), (1, F32)], [])
    z = _mm("in_proj", xn, wf['w_in'], 'nn', TR, 640, D, F32)
    lam_re, lam_im = p['lam_re'].reshape(N_GROUPS, SSM_STATE), p['lam_im'].reshape(N_GROUPS, SSM_STATE)
    log_dt = p['log_dt'].reshape(1, N_GROUPS)
    bt_re = p['b_re'].reshape(N_GROUPS, SSM_STATE, SSM_GROUP).transpose(2, 0, 1)
    bt_im = p['b_im'].reshape(N_GROUPS, SSM_STATE, SSM_GROUP).transpose(2, 0, 1)
    c_re = p['c_re'].reshape(N_GROUPS * SSM_GROUP, SSM_STATE)
    c_im = p['c_im'].reshape(N_GROUPS * SSM_GROUP, SSM_STATE)
    a_re, a_im, bbt_re, bbt_im = _s5_prep(lam_re, lam_im, log_dt, bt_re, bt_im)
    nq, ns = N_GROUPS // GQ, GQ * SSM_STATE
    a8 = jnp.broadcast_to(jnp.stack([a_re.reshape(nq, ns), a_im.reshape(nq, ns)], 1)[:, :, None, :],
                          (nq, 2, N_SEG, ns))
    gcp = lambda t: t.transpose(1, 0, 2).reshape(N_GROUPS * SSM_GROUP, SSM_STATE)
    bb_re, bb_im = gcp(bbt_re), gcp(bbt_im)
    dsk = row(p['d_skip'])
    u_seg = _to_segments(z[:, :D_SSM])
    y_ssm = _from_segments(_s5_fwd(u_seg, a8, bb_re, bb_im, c_re, c_im, dsk))
    g, = _rowwise("gelu", _gelu_fn, [(y_ssm, D_SSM, 0)], [], [(D_SSM, BF16)], [])
    gl = _mm("glu_proj", g, wf['w_glu'], 'nn', TR, D_SSM, D_SSM, F32)
    yan, ra = _rowwise("glu_norm", _glu_fwd_fn, [(y_ssm, D_SSM, 0), (gl, D_SSM, 0)],
                       [row(p['b_glu']), row(p['out_norm_ssm'])], [(D_SSM, BF16), (1, F32)], [])
    qn, rq = _rowwise("rms_q", _rms_fwd_fn, [(z, Q_LORA, D_SSM // Q_LORA)], [row(p['q_a_norm'])],
                      [(Q_LORA, BF16), (1, F32)], [])
    kvn, rkv = _rowwise("rms_kv", _rms_fwd_fn, [(z, KV_LORA, (D_SSM + Q_LORA) // KV_LORA)], [row(p['kv_a_norm'])],
                        [(KV_LORA, BF16), (1, F32)], [])
    q = _mm("q_proj", qn, wf['w_q_b'], 'nn', T, HEAD_PAD, Q_LORA, F32, stack='b_out')
    kv = _mm("kv_proj", kvn, wf['w_kv_b'], 'nn', T, HEAD_PAD, KV_LORA, F32, stack='b_out')
    yb = _attn_fwd(q, kv, z, tabs)
    ybn, rb = _rowwise("rms_attn", _rms_fwd_fn, [(yb, D_SSM, 0)], [row(p['out_norm_attn'])],
                       [(D_SSM, BF16), (1, F32)], [])
    y = jnp.concatenate([yan, ybn], axis=1)
    wf = dict(wf)
    wf['w_out'], wf['w_up'], wf['w_down'] = late_weights(y)
    h1 = _mm("out_proj", y, wf['w_out'], 'nn', TR, 1024, D, F32, res=h0)
    xn2, r1 = _rowwise("rms_ffn", _rms_fwd_fn, [(h1, D, 0)], [row(p['ffn_norm'])], [(D, BF16), (1, F32)], [])
    up = _mm("up_proj", xn2, wf['w_up'], 'nn', TR, FF_PIECE, D, BF16, stack='b_out')
    cw4 = wf['conv_w'].reshape(3, 4, FF_PIECE).transpose(1, 0, 2)
    cb4 = p['conv_b'].reshape(4, 1, FF_PIECE)
    act = _ff_act(up, cw4, cb4)
    wd4 = wf['w_down'].reshape(4, FF_PIECE, D)
    h2 = _mm("down_proj", act, wd4, 'nn', TR, 1024, FF_PIECE, F32, stack='ab_red', res=h1)
    dh2, sm['final_norm'], loss = _rowwise("loss", _loss_fn, [(h2, D, 0), (tgt_pad, D, 0)], [row(p['final_norm'])],
                                           [(D, F32)], [D, 128])

    big = {}
    dact = _mm("down_bwd_x", dh2, wd4, 'nt', TR, FF_PIECE, D, BF16, stack='b_out')
    g_down = _mm("down_bwd_w", act, dh2, 'tn', FF_PIECE, 512, T, BF16, stack='a_out').reshape(
        N_DEV, D_FF // N_DEV, D)
    dup, dcw4, sm['conv_b'] = _ff_act_bwd(up, dact, cw4, cb4)
    dxn2 = _mm("up_bwd_x", dup, wf['w_up'], 'nt', TR, 1024, FF_PIECE, F32, stack='ab_red')
    g_up = _mm("up_bwd_w", dup, xn2, 'tn', FF_PIECE, 1024, T, BF16, stack='a_out')
    token = send_grads(['w_down', 'w_up'], [g_down, g_up])
    dh1, sm['ffn_norm'] = _rowwise("rms_ffn_bwd", _rms_bwd_res_fn,
                                   [(dxn2, D, 0), (h1, D, 0), (r1, 1, 0), (dh2, D, 0)],
                                   [row(p['ffn_norm']) + token[0:1, 0:1]], [(D, F32)], [D])
    dy = _mm("out_bwd_x", dh1, wf['w_out'], 'nt', TR, 1024, D, F32)
    g_out = _mm("out_bwd_w", y, dh1, 'tn', 1024, 1024, T, BF16).reshape(N_DEV, D // N_DEV, D)
    dgl, dgd, sm['out_norm_ssm'], sm['b_glu'] = _rowwise(
        "glu_bwd", _glu_bwd_fn, [(dy, D_SSM, 0), (y_ssm, D_SSM, 0), (gl, D_SSM, 0), (ra, 1, 0)],
        [row(p['b_glu']), row(p['out_norm_ssm'])], [(D_SSM, BF16), (D_SSM, F32)], [D_SSM, D_SSM])
    dg = _mm("glu_bwd_x", dgl, wf['w_glu'], 'nt', TR, D_SSM, D_SSM, F32, res=dgd)
    g_glu = _mm("glu_bwd_w", g, dgl, 'tn', D_SSM, D_SSM, T, BF16).reshape(N_DEV, D_SSM // N_DEV, D_SSM)
    dy_ssm, = _rowwise("gelu_bwd", _gelu_bwd_fn, [(dg, D_SSM, 0), (y_ssm, D_SSM, 0)], [], [(D_SSM, F32)], [])
    dyb, sm['out_norm_attn'] = _rowwise("rms_attn_bwd", _rms_bwd_fn, [(dy, D_SSM, 1), (yb, D_SSM, 0), (rb, 1, 0)],
                                        [row(p['out_norm_attn'])], [(D_SSM, F32)], [D_SSM])
    dq, dkv, dkpe = _attn_bwd(q, kv, z, tabs, dyb)
    g_q = _mm("q_bwd_w", qn, dq, 'tn', Q_LORA, HEAD_PAD, T, BF16, stack='b_out')
    dqn = _mm("q_bwd_x", dq, wf['w_q_b'], 'nt', TR, Q_LORA, HEAD_PAD, F32, stack='ab_red')
    g_kv = _mm("kv_bwd_w", kvn, dkv, 'tn', KV_LORA, HEAD_PAD, T, BF16, stack='b_out')
    dkvn = _mm("kv_bwd_x", dkv, wf['w_kv_b'], 'nt', TR, KV_LORA, HEAD_PAD, F32, stack='ab_red')
    token = send_grads(['w_out', 'w_glu', 'w_q_b', 'w_kv_b'], [g_out, g_glu, g_q, g_kv])
    dq_a, sm['q_a_norm'] = _rowwise("rms_q_bwd", _rms_bwd_fn,
                                    [(dqn, Q_LORA, 0), (z, Q_LORA, D_SSM // Q_LORA), (rq, 1, 0)],
                                    [row(p['q_a_norm'])], [(Q_LORA, BF16)], [Q_LORA])
    dkv_a, sm['kv_a_norm'] = _rowwise("rms_kv_bwd", _rms_bwd_fn,
                                      [(dkvn, KV_LORA, 0), (z, KV_LORA, (D_SSM + Q_LORA) // KV_LORA), (rkv, 1, 0)],
                                      [row(p['kv_a_norm'])], [(KV_LORA, BF16)], [KV_LORA])
    dk_pe = _kpe_bwd(dkpe, tabs)
    du_seg, dbre, dbim, dcre, dcim, da, sm['d_skip'] = _s5_bwd(
        u_seg, _to_segments(dy_ssm), a8, bb_re, bb_im, c_re, c_im, dsk + token[0:1, 0:1])
    du = _from_segments(du_seg).astype(BF16)
    cgp = lambda d: d.reshape(N_GROUPS, SSM_GROUP, SSM_STATE).transpose(1, 0, 2)
    dlam_re, dlam_im, dlog_dt, dbt_re, dbt_im = _s5_prep_bwd(
        lam_re, lam_im, log_dt, bt_re, bt_im, da[:, 0, :].reshape(N_GROUPS, SSM_STATE),
        da[:, 1, :].reshape(N_GROUPS, SSM_STATE), cgp(dbre), cgp(dbim))
    sm['lam_re'], sm['lam_im'], sm['log_dt'] = dlam_re, dlam_im, dlog_dt
    sm['b_re'], sm['b_im'] = gcp(dbt_re), gcp(dbt_im)
    sm['c_re'], sm['c_im'] = dcre, dcim
    token = send_small(sm, dcw4, loss)
    dz = jnp.concatenate([du, dq_a, dkv_a, dk_pe], axis=1)
    dxn = _mm("in_bwd_x", dz, wf['w_in'], 'nt', TR, 1024, D_IN_PAD, F32)
    big['w_in'] = _mm("in_bwd_w", xn, dz, 'tn', 1024, 640, T, BF16).reshape(N_DEV, D // N_DEV, D_IN_PAD)
    dh0, d_mix_norm = _rowwise("rms_mix_bwd", _rms_bwd_res_fn,
                               [(dxn, D, 0), (h0, D, 0), (r0, 1, 0), (dh1, D, 0)],
                               [row(p['mix_norm']) + token[0:1, 0:1]], [(D, F32)], [D])
    return dh0[N_META:], dh0[:N_META], d_mix_norm, big


def _place():
    x, y, c = lax.axis_index("x"), lax.axis_index("y"), lax.axis_index("c")
    return x, y, c, [(1 - x, y), (x, 1 - y), (1 - x, 1 - y)]


_HBM = pl.BlockSpec(memory_space=pltpu.HBM)


def _all_gather(name, arrs, chip_blocks=()):
    n, nc = len(arrs), len(chip_blocks)

    def body(*refs):
        ins, cins, outs, couts = refs[:n], refs[n:n + nc], refs[n + nc:2 * n + nc], refs[2 * n + nc:2 * (n + nc)]
        send, recv, loc, csend, crecv = refs[2 * (n + nc):]
        x, y, c, chips = _place()
        me, sib = (x, y, c), (x, y, 1 - c)

        def cp(a, k, blk, to, src=None):
            dst = outs[a].at[4 * blk[0] + 2 * blk[1] + blk[2]]
            return pltpu.make_async_remote_copy(src_ref=dst if src is None else src, dst_ref=dst,
                                                send_sem=send.at[a, k], recv_sem=recv.at[a, k],
                                                device_id=to, device_id_type=MESH)

        between_chips = [pltpu.make_async_remote_copy(src_ref=cins[a].at[2 * ch[0] + ch[1]], dst_ref=couts[a].at[j],
                                                      send_sem=csend.at[a, j], recv_sem=crecv.at[a, j],
                                                      device_id=(*ch, c), device_id_type=MESH)
                         for a in range(nc) for j, ch in enumerate(chips)]
        mine = [pltpu.make_async_copy(ins[a], outs[a].at[4 * x + 2 * y + c], loc.at[a]) for a in range(n)]
        first = []
        for a in range(n):
            first.append(cp(a, 0, me, sib, src=ins[a]))
            first += [cp(a, 1 + j, me, (*ch, c), src=ins[a]) for j, ch in enumerate(chips)]
        for f in mine + first + between_chips:
            f.start()
        passed = []
        for a in range(n):
            for j, ch in enumerate(chips):
                cp(a, 1 + j, (*ch, c), me).wait_recv()
                passed.append(cp(a, 4 + j, (*ch, c), sib))
                passed[-1].start()
        for a in range(n):
            cp(a, 0, sib, me).wait_recv()
            for j, ch in enumerate(chips):
                cp(a, 4 + j, (*ch, 1 - c), me).wait_recv()
        for f in first + passed:
            f.wait_send()
        for m in mine + between_chips:
            m.wait()

    out = pl.pallas_call(
        body, name=name, in_specs=[_HBM] * (n + nc), out_specs=[_HBM] * (n + nc),
        out_shape=[jax.ShapeDtypeStruct((N_DEV,) + a.shape, a.dtype) for a in arrs]
        + [jax.ShapeDtypeStruct((3,) + a.shape[1:], a.dtype) for a in chip_blocks],
        scratch_shapes=[pltpu.SemaphoreType.DMA((n, 7)), pltpu.SemaphoreType.DMA((n, 7)),
                        pltpu.SemaphoreType.DMA((n,)), pltpu.SemaphoreType.DMA((max(nc, 1), 3)),
                        pltpu.SemaphoreType.DMA((max(nc, 1), 3))],
    )(*arrs, *chip_blocks)
    return (out[:n], out[n:]) if nc else out


def _exchange_cores(name, arrs):
    n = len(arrs)

    def body(*refs):
        ins, outs = refs[:n], refs[n:2 * n]
        send, recv = refs[2 * n:]
        x, y, c, _ = _place()
        remote = [pltpu.make_async_remote_copy(src_ref=ins[a].at[2 * k + 1 - c], dst_ref=outs[a].at[k],
                                               send_sem=send.at[a, k], recv_sem=recv.at[a, k],
                                               device_id=(x, y, 1 - c), device_id_type=MESH)
                  for a in range(n) for k in range(4)]
        for d in remote:
            d.start()
        for d in remote:
            d.wait()

    return pl.pallas_call(
        body, name=name, in_specs=[_HBM] * n, out_specs=[_HBM] * n,
        out_shape=[jax.ShapeDtypeStruct((4,) + a.shape[1:], a.dtype) for a in arrs],
        scratch_shapes=[pltpu.SemaphoreType.DMA((n, 4))] * 2,
    )(*arrs)


_SEM = pl.BlockSpec(memory_space=pltpu.SEMAPHORE)
_EFFECT = pltpu.SideEffectType.DATAFLOW_SIDE_EFFECTING


def _split_exchange(name, srcs, land_shapes, copies_of, n_cp, after, all_of=None):
    n = len(srcs)
    lands = [pltpu.with_memory_space_constraint(lax.empty(s, a.dtype), pltpu.HBM) for s, a in zip(land_shapes, srcs)]

    per = 1 if all_of else n_cp
    n_sem = n * per

    def descriptors(src_refs, land_refs, send, recv, mirror):
        x, y, c, chips = _place()
        out = []
        for a in range(n):
            if mirror and all_of:
                whole = all_of(land_refs[a])
                out.append(pltpu.make_async_remote_copy(src_ref=whole, dst_ref=whole, send_sem=send[a],
                                                        recv_sem=recv[a], device_id=(x, y, 1 - c),
                                                        device_id_type=MESH))
                continue
            for k, (src, dst, to, back) in enumerate(copies_of(a, src_refs[a], land_refs[a], x, y, c, chips)):
                out.append(pltpu.make_async_remote_copy(src_ref=src, dst_ref=back if mirror else dst,
                                                        send_sem=send[a * per + k % per],
                                                        recv_sem=recv[a * per + k % per],
                                                        device_id=to, device_id_type=MESH))
        return out

    def start_body(*refs):
        sems = refs[2 * n + 1:2 * n + 1 + 2 * n_sem]
        for d in descriptors(refs[:n], refs[n:2 * n], sems[:n_sem], sems[n_sem:], False):
            d.start()
        refs[-1][...] = jnp.zeros((8, 128), F32)

    hbm_like = lambda arrs: [pltpu.HBM(a.shape, a.dtype) for a in arrs]
    res = pl.pallas_call(
        start_body, name=name + "_start", in_specs=[_HBM] * (2 * n) + [pl.BlockSpec(memory_space=pl.ANY)],
        out_specs=[_SEM] * (2 * n_sem) + [_HBM] * (2 * n) + [pl.BlockSpec(memory_space=pltpu.VMEM)],
        out_shape=[pltpu.SemaphoreType.DMA(())] * (2 * n_sem)
        + hbm_like(srcs) + hbm_like(lands) + [jax.ShapeDtypeStruct((8, 128), F32)],
        input_output_aliases={i: 2 * n_sem + i for i in range(2 * n)},
        compiler_params=pltpu.CompilerParams(has_side_effects=_EFFECT),
    )(*[pltpu.with_memory_space_constraint(a, pltpu.HBM) for a in srcs], *lands, after)
    sems, thru, token = res[:2 * n_sem], res[2 * n_sem:2 * n_sem + 2 * n], res[-1]

    def wait(after):
        def wait_body(*refs):
            s = refs[2 * n:2 * n + 2 * n_sem]
            for d in descriptors(refs[:n], refs[n:2 * n], s[:n_sem], s[n_sem:], True):
                d.wait_send()
                d.wait_recv()

        out = pl.pallas_call(
            wait_body, name=name + "_wait",
            in_specs=[_HBM] * (2 * n) + [_SEM] * (2 * n_sem) + [pl.BlockSpec(memory_space=pl.ANY)],
            out_specs=[_HBM] * (2 * n), out_shape=hbm_like(srcs) + hbm_like(lands),
            input_output_aliases={i: i for i in range(2 * n)},
            compiler_params=pltpu.CompilerParams(has_side_effects=_EFFECT),
        )(*thru, *sems, after)
        return out[:n], out[n:]

    return token, wait


def _gather_copies(a, src, land, x, y, c, chips):
    peers = [(x, y, 1 - c)] + [(*ch, c) for ch in chips]
    me = 4 * x + 2 * y + c
    return [(src, land.at[me], to, land.at[4 * to[0] + 2 * to[1] + to[2]]) for to in peers]


def _to_all_copies(a, src, land, x, y, c, chips):
    peers = [(x, y, 1 - c)] + [(*ch, cc) for ch in chips for cc in (c, 1 - c)]
    me = 4 * x + 2 * y + c
    return [(src, land.at[me], to, land.at[4 * to[0] + 2 * to[1] + to[2]]) for to in peers]


def _chips_copies(a, src, land, x, y, c, chips):
    return [(src.at[2 * ch[0] + ch[1]], land.at[j], (*ch, c), land.at[j]) for j, ch in enumerate(chips)]


def _gather_forward(name, arrs):
    n = len(arrs)

    def body(*refs):
        ins, outs = refs[:n], refs[n:2 * n]
        send, recv = refs[2 * n:]
        x, y, c, chips = _place()
        sends, recvs = [], []
        for a in range(n):
            for j, ch in enumerate(chips):
                mine, theirs = 4 * ch[0] + 2 * ch[1] + c, 4 * ch[0] + 2 * ch[1] + 1 - c
                sends.append(pltpu.make_async_remote_copy(src_ref=ins[a].at[mine], dst_ref=outs[a].at[mine],
                                                          send_sem=send.at[a, j], recv_sem=recv.at[a, j],
                                                          device_id=(x, y, 1 - c), device_id_type=MESH))
                recvs.append(pltpu.make_async_remote_copy(src_ref=ins[a].at[theirs], dst_ref=outs[a].at[theirs],
                                                          send_sem=send.at[a, j], recv_sem=recv.at[a, j],
                                                          device_id=(x, y, 1 - c), device_id_type=MESH))
        for d in sends:
            d.start()
        for s, r in zip(sends, recvs):
            s.wait_send()
            r.wait_recv()

    return pl.pallas_call(
        body, name=name, in_specs=[_HBM] * n, out_specs=[_HBM] * n,
        out_shape=[jax.ShapeDtypeStruct(a.shape, a.dtype) for a in arrs],
        input_output_aliases={i: i for i in range(n)},
        scratch_shapes=[pltpu.SemaphoreType.DMA((n, 3))] * 2,
    )(*arrs)


def _blocks_of(r, c):
    if r * c * 4 <= (2 << 20):
        return r, c
    if r % 128 == 0:
        return 128, c
    return r, 256


def _pair_sum(name, own, got, core):
    _, r, c = got.shape
    rb, cb = _blocks_of(r, c)

    def body(s_ref, a_ref, b_ref, o_ref):
        o_ref[...] = (a_ref[...].astype(F32) + b_ref[...].astype(F32)).astype(o_ref.dtype)

    return pl.pallas_call(
        body, name=name, out_shape=jax.ShapeDtypeStruct((4, r, c), got.dtype),
        grid_spec=pltpu.PrefetchScalarGridSpec(
            num_scalar_prefetch=1, grid=(4, r // rb, c // cb),
            in_specs=[pl.BlockSpec((None, rb, cb), lambda k, i, j, s: (2 * k + s[0], i, j)),
                      pl.BlockSpec((None, rb, cb), lambda k, i, j, s: (k, i, j))],
            out_specs=pl.BlockSpec((None, rb, cb), lambda k, i, j, s: (k, i, j))),
        compiler_params=_params(3, 24 << 20),
    )(core, _in_hbm(own), _in_hbm(got))


def _adamw_math(w, g, m, v):
    m = B1 * m + (1.0 - B1) * g
    v = B2 * v + (1.0 - B2) * (g * g)
    m_hat = m / (1.0 - B1 ** STEP)
    v_hat = v / (1.0 - B2 ** STEP)
    return -LR * (m_hat / (jnp.sqrt(v_hat) + ADAM_EPS) + WD * w), m, v


def _adamw_big(name, own, got, chip, w, m, v):
    r, c = w.shape
    rb, cb = _blocks_of(r, c)

    def body(s_ref, o_ref, p_ref, w_ref, m_ref, v_ref, g_ref, d_ref, nm_ref, nv_ref):
        g = o_ref[...].astype(F32)
        for k in range(3):
            g = g + p_ref[k].astype(F32)
        g_ref[...] = g
        d_ref[...], nm_ref[...], nv_ref[...] = _adamw_math(w_ref[...], g, m_ref[...], v_ref[...])

    blk = pl.BlockSpec((rb, cb), lambda i, j, s: (i, j))
    return pl.pallas_call(
        body, name=name, out_shape=[jax.ShapeDtypeStruct((r, c), F32)] * 4,
        grid_spec=pltpu.PrefetchScalarGridSpec(
            num_scalar_prefetch=1, grid=(r // rb, c // cb),
            in_specs=[pl.BlockSpec((None, rb, cb), lambda i, j, s: (s[0], i, j)),
                      pl.BlockSpec((3, rb, cb), lambda i, j, s: (0, i, j)), blk, blk, blk],
            out_specs=[blk] * 4),
        compiler_params=_params(2, 40 << 20),
    )(chip, *map(_in_hbm, (own, got, w, m, v)))


def _adamw_multi(name, items, sums=(), row_blocks=1, own=(), dev=None):
    n, ns = len(items), len(sums)
    n_own = len(own)
    assert n_own in (0, n + ns)

    def body(*refs):
        dev_ref = refs[0] if n_own else None
        refs = refs[1:] if n_own else refs
        owns, refs = refs[:n_own], refs[n_own:]
        ins, outs = refs[:4 * n + ns], refs[4 * n + ns:]

        def total(ref, t):
            g = None
            for d in range(ref.shape[0]):
                part = jnp.where(dev_ref[0] == d, owns[t][...], ref[d]) if n_own else ref[d]
                g = part.astype(F32) if g is None else g + part.astype(F32)
            return g

        for t in range(n):
            p_ref, w_ref, m_ref, v_ref = ins[4 * t:4 * t + 4]
            g = total(p_ref, t)
            outs[4 * t][...] = g
            outs[4 * t + 1][...], outs[4 * t + 2][...], outs[4 * t + 3][...] = _adamw_math(
                w_ref[...], g, m_ref[...], v_ref[...])
        for t in range(ns):
            outs[4 * n + t][...] = total(ins[4 * n + t], n + t)

    def spec(shape, lead):
        blk = (shape[0] // row_blocks,) + tuple(shape[1:])
        nd = len(shape)
        if lead:
            return pl.BlockSpec((lead,) + blk, lambda i: (0, i) + (0,) * (nd - 1))
        return pl.BlockSpec(blk, lambda i: (i,) + (0,) * (nd - 1))

    operands, in_specs, out_specs, out_shape = [], [], [], []
    if n_own:
        operands += [dev] + list(own)
        in_specs += [pl.BlockSpec(memory_space=pltpu.SMEM)] + [spec(o.shape, 0) for o in own]
    for parts, w, m, v in items:
        assert parts.shape[1:] == w.shape == m.shape == v.shape, (name, parts.shape, w.shape)
        operands += [parts, w, m, v]
        in_specs += [spec(w.shape, parts.shape[0])] + [spec(w.shape, 0)] * 3
        out_specs += [spec(w.shape, 0)] * 4
        out_shape += [jax.ShapeDtypeStruct(w.shape, F32)] * 4
    for parts in sums:
        operands.append(parts)
        in_specs.append(spec(parts.shape[1:], parts.shape[0]))
        out_specs.append(spec(parts.shape[1:], False))
        out_shape.append(jax.ShapeDtypeStruct(parts.shape[1:], F32))
    return pl.pallas_call(
        body, name=name, grid=(row_blocks,), in_specs=in_specs, out_specs=out_specs, out_shape=out_shape,
        compiler_params=_params(1, 56 << 20),
    )(*operands)


def kernel(x, meta_tokens, mix_norm, w_in, lam_re, lam_im, log_dt, b_re, b_im, c_re, c_im, d_skip, w_glu, b_glu, q_a_norm, w_q_b, kv_a_norm, w_kv_b, out_norm_ssm, out_norm_attn, w_out, ffn_norm, w_up, conv_w, conv_b, w_down, final_norm, loss_target, m_meta_tokens, m_mix_norm, m_w_in, m_lam_re, m_lam_im, m_log_dt, m_b_re, m_b_im, m_c_re, m_c_im, m_d_skip, m_w_glu, m_b_glu, m_q_a_norm, m_w_q_b, m_kv_a_norm, m_w_kv_b, m_out_norm_ssm, m_out_norm_attn, m_w_out, m_ffn_norm, m_w_up, m_conv_w, m_conv_b, m_w_down, m_final_norm, v_meta_tokens, v_mix_norm, v_w_in, v_lam_re, v_lam_im, v_log_dt, v_b_re, v_b_im, v_c_re, v_c_im, v_d_skip, v_w_glu, v_b_glu, v_q_a_norm, v_w_q_b, v_kv_a_norm, v_w_kv_b, v_out_norm_ssm, v_out_norm_attn, v_w_out, v_ffn_norm, v_w_up, v_conv_w, v_conv_b, v_w_down, v_final_norm):
    given = dict(locals())
    w = {n: given[n] for n in WEIGHTS}
    m = {n: given['m_' + n] for n in WEIGHTS}
    v = {n: given['v_' + n] for n in WEIGHTS}
    dev = 4 * lax.axis_index("x") + 2 * lax.axis_index("y") + lax.axis_index("c")

    shard = {
        'w_in': jnp.pad(w_in[0], ((0, 0), (0, D_IN_PAD - D_IN))),
        'w_glu': w_glu[0],
        'w_q_b': jnp.pad(w_q_b[0], ((0, 0), (0, HEAD_PAD - QK_NOPE - QK_ROPE))),
        'w_kv_b': w_kv_b[0],
        'w_out': w_out[0],
        'w_up': w_up[0],
        'w_down': w_down[0],
    }
    core = lax.axis_index("c").astype(jnp.int32).reshape(1)
    chip = (2 * lax.axis_index("x") + lax.axis_index("y")).astype(jnp.int32).reshape(1)
    early, late = ['w_in', 'w_glu', 'w_q_b', 'w_kv_b'], ['w_out', 'w_up', 'w_down']
    shard = {n: a.astype(BF16) for n, a in shard.items()}

    gathered = _all_gather("gather_early", [shard[n] for n in early]
                           + [meta_tokens, jnp.pad(conv_w[0], ((0, 5), (0, 0)))])
    wf = dict(zip(early, gathered[:len(early)]))
    for n in ('w_in', 'w_glu'):
        wf[n] = wf[n].reshape(-1, wf[n].shape[-1])
    meta = gathered[-2].transpose(1, 0, 2).reshape(N_META, D)
    wf['conv_w'] = gathered[-1][:, :3].transpose(1, 0, 2).reshape(3, D_FF)
    gather_token, wait_late = _split_exchange("gather_late", [shard[n] for n in late],
                                              [(N_DEV,) + shard[n].shape for n in late], _gather_copies, 4,
                                              after=gathered[0])

    def late_weights(after):
        mine, landed = wait_late(after)
        full = _gather_forward("gather_late_forward",
                               [lax.dynamic_update_slice(a, s[None], (dev, 0, 0)) for s, a in zip(mine, landed)])
        return full[0].reshape(-1, D), full[1], full[2].reshape(-1, D)

    pending = []

    def send_grads(names, arrs):
        got = _exchange_cores("reduce_cores_" + names[0], arrs)
        sums = [_pair_sum("sum_cores_" + n, a, g, core) for n, a, g in zip(names, arrs, got)]
        token, wait = _split_exchange("reduce_chips_" + names[0], sums, [(3,) + s.shape[1:] for s in sums],
                                      _chips_copies, 3, after=sums[-1])
        pending.append((names, wait))
        return token

    small_sent = []
    s5_bc = ['b_re', 'b_im', 'c_re', 'c_im']

    def send_small(sm, dcw4, loss_part):
        names = [n for n in SMALL if n in sm]
        arrs = [sm[n].reshape(-1, 128).astype(BF16) if n in s5_bc else sm[n] for n in names] + [dcw4, loss_part]
        token, wait = _split_exchange("gather_small", arrs, [(N_DEV,) + a.shape for a in arrs], _to_all_copies, 7,
                                      after=arrs[-1], all_of=lambda land: land.at[pl.ds(0, N_DEV - 1)])
        small_sent.append((names, wait))
        return token

    small = {n: w[n] for n in SMALL}
    small['mix_norm'] = mix_norm + gather_token[0:1, 0:1]
    grad_x, dmeta, d_mix_norm, big = _local_step(x[0], loss_target[0], meta, small, wf, late_weights, send_grads,
                                                  send_small)
    grads, deltas, new_m, new_v = {}, {}, {}, {}

    def keep(n, outs):
        grads[n], deltas[n], new_m[n], new_v[n] = (o.reshape(w[n].shape) for o in outs)

    rest = list(big)
    from_core = _exchange_cores("reduce_cores", [big[n] for n in rest])
    chip_sums = [_pair_sum("sum_cores_" + n, big[n], got, core) for n, got in zip(rest, from_core)]
    (mix_parts, meta_parts), from_chips = _all_gather("reduce_chips_gather_last", [d_mix_norm, dmeta],
                                                      chip_blocks=chip_sums)
    reduced = list(zip(rest, chip_sums, from_chips))
    for names, wait in pending:
        reduced += zip(names, *wait(from_chips[0]))
    for n, own, got in reduced:
        if n == 'w_up':
            outs = _adamw_big("adamw_" + n, own, got, chip, w[n][0].T, m[n][0].T, v[n][0].T)
            keep(n, [o.T for o in outs])
            continue
        cols = w[n].shape[2]
        keep(n, _adamw_big("adamw_" + n, own[:, :, :cols], got[:, :, :cols], chip, w[n][0], m[n][0], v[n][0]))

    outs = _adamw_multi("adamw_last", [(mix_parts, mix_norm, m['mix_norm'], v['mix_norm'])], sums=[meta_parts])
    keep('mix_norm', outs[:4])
    g_meta = outs[4]
    names, wait = small_sent[0]
    mine, landed = wait(grads['w_up'])
    parts, own = dict(zip(names + ['conv_w', 'loss'], landed)), dict(zip(names + ['conv_w', 'loss'], mine))
    wide = [n for n in names if n not in s5_bc]
    summed = ['conv_w', 'loss'] + s5_bc
    outs = _adamw_multi("adamw_small",
                        [(parts[n],) + tuple(t[n].reshape(own[n].shape) for t in (w, m, v)) for n in wide],
                        sums=[parts[n] for n in summed], own=[own[n] for n in wide + summed],
                        dev=dev.astype(jnp.int32).reshape(1))
    for i, n in enumerate(wide):
        keep(n, outs[4 * i:4 * i + 4])
    g_cw4, loss = outs[-6:-4]
    rows_gc = lambda a, n: (a.reshape(N_GROUPS, SSM_STATE, SSM_GROUP).transpose(0, 2, 1) if n[0] == 'b' else a
                            ).reshape(N_GROUPS * SSM_GROUP, SSM_STATE)
    outs = _adamw_multi("adamw_s5_bc", [(g.reshape(1, N_GROUPS * SSM_GROUP, SSM_STATE),)
                                        + tuple(rows_gc(t[n], n) for t in (w, m, v))
                                        for n, g in zip(s5_bc, outs[-4:])])
    for i, n in enumerate(s5_bc):
        back = outs[4 * i:4 * i + 4]
        if n[0] == 'b':
            back = [o.reshape(N_GROUPS, SSM_GROUP, SSM_STATE).transpose(0, 2, 1) for o in back]
        keep(n, back)

    g_meta = lax.dynamic_slice(g_meta, (0, dev * (D // N_DEV)), (N_META, D // N_DEV))
    g_conv = lax.dynamic_slice(g_cw4.transpose(1, 0, 2).reshape(3, D_FF), (0, dev * (D_FF // N_DEV)),
                               (3, D_FF // N_DEV))
    rows8 = lambda a: jnp.pad(a.reshape(3, D_FF // N_DEV), ((0, 5), (0, 0)))
    outs = _adamw_multi("adamw_cols", [(g_meta[None], meta_tokens, m['meta_tokens'], v['meta_tokens']),
                                       (rows8(g_conv)[None], rows8(conv_w), rows8(m['conv_w']), rows8(v['conv_w']))])
    keep('meta_tokens', outs[:4])
    keep('conv_w', [o[:3] for o in outs[4:]])

    return (loss[0, 0], grad_x[None], *[grads[n] for n in WEIGHTS], *[deltas[n] for n in WEIGHTS],
            *[new_m[n] for n in WEIGHTS], *[new_v[n] for n in WEIGHTS])
```

```python
import functools
import math

import jax
import jax.numpy as jnp
from jax import lax
from jax.experimental import pallas as pl
from jax.experimental.pallas import tpu as pltpu

F32, BF16 = jnp.float32, jnp.bfloat16

N_DEV = 8
N_META, SEQ, D = 16, 2048, 2048
T = N_META + SEQ
TR = 688
EPS = 1e-6
D_SSM, Q_LORA, KV_LORA, QK_ROPE = 1024, 512, 256, 64
D_IN = D_SSM + Q_LORA + KV_LORA + QK_ROPE
D_IN_PAD = 1920
N_HEADS, QK_NOPE, V_HEAD = 8, 128, 128
HEAD_PAD = 256
SM_SCALE = 1.0 / math.sqrt(QK_NOPE + QK_ROPE)
CHUNK = 64
N_GROUPS, SSM_GROUP, SSM_STATE = 64, 16, 64
N_SEG = 8
SEG = T // N_SEG
GQ = 8
D_FF = 5504
FF_PIECE = 1376
ROPE_BASE = 10000.0
LR, B1, B2, ADAM_EPS, WD, STEP = 0.001, 0.9, 0.999, 1e-08, 0.01, 10
VMEM_CAP = 60 * 1024 * 1024
MESH = pl.DeviceIdType.MESH

WEIGHTS = ['meta_tokens', 'mix_norm', 'w_in', 'lam_re', 'lam_im', 'log_dt', 'b_re', 'b_im', 'c_re', 'c_im',
           'd_skip', 'w_glu', 'b_glu', 'q_a_norm', 'w_q_b', 'kv_a_norm', 'w_kv_b', 'out_norm_ssm',
           'out_norm_attn', 'w_out', 'ffn_norm', 'w_up', 'conv_w', 'conv_b', 'w_down', 'final_norm']
BIG = ['w_in', 'w_glu', 'w_q_b', 'w_kv_b', 'w_out', 'w_up', 'w_down']
SMALL = [n for n in WEIGHTS if n not in BIG and n not in ('meta_tokens', 'conv_w')]


def _nbytes(shape, dtype):
    return math.prod(shape) * jnp.dtype(dtype).itemsize


def _in_hbm(a):
    return pltpu.with_memory_space_constraint(a, pltpu.HBM) if a.size * a.dtype.itemsize >= (1 << 20) else a


def _params(sem, need):
    return pltpu.CompilerParams(dimension_semantics=("arbitrary",) * sem,
                                vmem_limit_bytes=int(min(VMEM_CAP, max(need, 16 * 1024 * 1024))))


_DIMS = {'nn': (((1,), (0,)), ((), ())), 'nt': (((1,), (1,)), ((), ())), 'tn': (((0,), (0,)), ((), ()))}


def _mm(name, a, b, mode, tm, tn, tk, out_dtype, stack=None, res=None):
    sa, sb, so = stack in ('a_out', 'ab_red'), stack in ('b_out', 'ab_red'), stack in ('a_out', 'b_out')
    nj = a.shape[0] if sa else (b.shape[0] if sb else 1)
    a2, b2 = a.shape[-2:], b.shape[-2:]
    if mode == 'nn':
        (m, k), (k2, n) = a2, b2
    elif mode == 'nt':
        (m, k), (n, k2) = a2, b2
    else:
        (k, m), (k2, n) = a2, b2
    assert k == k2 and m % tm == 0 and n % tn == 0 and k % tk == 0, (name, a.shape, b.shape)
    n_jo, n_jr, n_k = (nj if so else 1), (nj if stack == 'ab_red' else 1), k // tk
    grid = (n_jo, m // tm, n // tn, n_jr, n_k)
    a_blk, a_idx = ((tk, tm), lambda i, j, kk: (kk, i)) if mode == 'tn' else ((tm, tk), lambda i, j, kk: (i, kk))
    b_blk, b_idx = ((tn, tk), lambda i, j, kk: (j, kk)) if mode == 'nt' else ((tk, tn), lambda i, j, kk: (kk, j))

    def spec(blk, idx, stacked, on_out):
        if not stacked:
            return pl.BlockSpec(blk, lambda jo, i, j, jr, kk: idx(i, j, kk))
        return pl.BlockSpec((None,) + blk, lambda jo, i, j, jr, kk: ((jo if on_out else jr),) + idx(i, j, kk))

    o_idx = lambda i, j, kk: (i, j)
    in_specs = [spec(a_blk, a_idx, sa, stack == 'a_out'), spec(b_blk, b_idx, sb, stack == 'b_out')]
    operands = [a, b]
    if res is not None:
        in_specs.append(spec((tm, tn), o_idx, False, False))
        operands.append(res)
    n_red = n_jr * n_k
    dims = _DIMS[mode]

    def body(*refs):
        a_ref, b_ref = refs[0], refs[1]
        res_ref = refs[2] if res is not None else None
        o_ref = refs[3] if res is not None else refs[2]
        part = lax.dot_general(a_ref[...].astype(BF16), b_ref[...].astype(BF16), dims,
                               preferred_element_type=F32)

        def finish(total):
            if res_ref is not None:
                total = total + res_ref[...]
            o_ref[...] = total.astype(o_ref.dtype)

        if n_red == 1:
            finish(part)
        else:
            acc_ref = refs[-1]
            step = pl.program_id(3) * n_k + pl.program_id(4)

            @pl.when(step == 0)
            def _():
                acc_ref[...] = part

            @pl.when(step > 0)
            def _():
                acc_ref[...] += part

            @pl.when(step == n_red - 1)
            def _():
                finish(acc_ref[...])

    out_shape = ((nj,) if so else ()) + (m, n)
    need = 2 * (_nbytes(a_blk, a.dtype) + _nbytes(b_blk, b.dtype) + _nbytes((tm, tn), out_dtype)
                + (_nbytes((tm, tn), F32) if res is not None else 0)) + 3 * _nbytes((tm, tn), F32) + (4 << 20)
    return pl.pallas_call(
        body, name=name, grid=grid, in_specs=in_specs, out_specs=spec((tm, tn), o_idx, so, True),
        out_shape=jax.ShapeDtypeStruct(out_shape, out_dtype),
        scratch_shapes=[pltpu.VMEM((tm, tn), F32)] if n_red > 1 else [],
        compiler_params=_params(5, need),
    )(*map(_in_hbm, operands))


def _rowwise(name, fn, row_ins, const_ins, row_outs, acc_outs, tr=TR):
    n_ri, n_ci, n_ro, n_ao = len(row_ins), len(const_ins), len(row_outs), len(acc_outs)

    def body(*refs):
        ri, ci = refs[:n_ri], refs[n_ri:n_ri + n_ci]
        ro, ao = refs[n_ri + n_ci:n_ri + n_ci + n_ro], refs[n_ri + n_ci + n_ro:]
        i = pl.program_id(0)
        outs = fn(i * tr, *[r[...] for r in ri], *[r[...] for r in ci])
        for r, v in zip(ro, outs[:n_ro]):
            r[...] = v.astype(r.dtype)
        if n_ao:
            @pl.when(i == 0)
            def _():
                for r, v in zip(ao, outs[n_ro:]):
                    r[...] = v

            @pl.when(i > 0)
            def _():
                for r, v in zip(ao, outs[n_ro:]):
                    r[...] += v

    in_specs = [pl.BlockSpec((tr, w), functools.partial(lambda cb, i: (i, cb), cb)) for _, w, cb in row_ins]
    in_specs += [pl.BlockSpec(c.shape, functools.partial(lambda nd, i: (0,) * nd, c.ndim)) for c in const_ins]
    out_specs = [pl.BlockSpec((tr, w), lambda i: (i, 0)) for w, _ in row_outs]
    out_specs += [pl.BlockSpec((1, w), lambda i: (0, 0)) for w in acc_outs]
    out_shape = [jax.ShapeDtypeStruct((T, w), dt) for w, dt in row_outs]
    out_shape += [jax.ShapeDtypeStruct((1, w), F32) for w in acc_outs]
    need = 2 * (sum(_nbytes((tr, w), a.dtype) for a, w, _ in row_ins) + sum(_nbytes((tr, w), dt) for w, dt in row_outs))
    need = 3 * need + (8 << 20)
    return pl.pallas_call(
        body, name=name, grid=(T // tr,), in_specs=in_specs, out_specs=out_specs, out_shape=out_shape,
        compiler_params=_params(1, need),
    )(*[_in_hbm(a) for a, _, _ in row_ins], *const_ins)


def _rms_fwd_fn(row0, h, g):
    h = h.astype(F32)
    r = lax.rsqrt(jnp.mean(h * h, axis=-1, keepdims=True) + EPS)
    return (h * r) * g, r


def _rms_bwd(dxn, h, r, g):
    xhat = h * r
    dxg = dxn * g
    dh = r * (dxg - xhat * jnp.mean(dxg * xhat, axis=-1, keepdims=True))
    return dh, jnp.sum(dxn * xhat, axis=0, keepdims=True)


def _rms_bwd_fn(row0, dxn, h, r, g):
    return _rms_bwd(dxn.astype(F32), h.astype(F32), r, g)


def _rms_bwd_res_fn(row0, dxn, h, r, res, g):
    dh, dg = _rms_bwd(dxn.astype(F32), h.astype(F32), r, g)
    return dh + res, dg


def _gelu_fn(row0, y):
    return (jax.nn.gelu(y),)


def _glu_fwd_fn(row0, y, gl, b, gain):
    ya = jax.nn.gelu(y) * jax.nn.sigmoid(gl + b)
    return _rms_fwd_fn(row0, ya, gain)


def _glu_bwd_fn(row0, dyan, y, gl, ra, b, gain):
    g = jax.nn.gelu(y)
    s = jax.nn.sigmoid(gl + b)
    dya, dgain = _rms_bwd(dyan, g * s, ra, gain)
    dgl = dya * g * s * (1.0 - s)
    return dgl, dya * s, dgain, jnp.sum(dgl, axis=0, keepdims=True)


def _gelu_bwd_fn(row0, dg, y):
    return (jax.vjp(jax.nn.gelu, y)[1](dg)[0],)


def _loss_fn(row0, h, tgt, gain):
    r = lax.rsqrt(jnp.mean(h * h, axis=-1, keepdims=True) + EPS)
    xhat = h * r
    rows = row0 + lax.broadcasted_iota(jnp.int32, h.shape, 0)
    err = jnp.where(rows >= N_META, xhat * gain - tgt, 0.0)
    loss = jnp.full((1, 128), jnp.sum(err * err) * (0.5 / D), F32)
    dh, dgain = _rms_bwd(err * (1.0 / D), h, r, gain)
    return dh, dgain, loss


def _rope_tables():
    pos = jnp.arange(T, dtype=jnp.int32)
    inv_freq = 1.0 / (ROPE_BASE ** (jnp.arange(0, QK_ROPE, 2, dtype=F32) / QK_ROPE))
    ang = pos.astype(F32)[:, None] * inv_freq[None, :]
    cos, sin, z32, z64 = jnp.cos(ang), jnp.sin(ang), jnp.zeros((T, 32), F32), jnp.zeros((T, 64), F32)
    return (jnp.concatenate([cos, cos, z64], 1), jnp.concatenate([-sin, z32, z64], 1),
            jnp.concatenate([z32, sin, z64], 1))


def _rot(x, cc, s1, s2):
    return x * cc + pltpu.roll(x, 96, 1) * s1 + pltpu.roll(x, 32, 1) * s2


def _derot(d, cc, s1, s2):
    return d * cc + pltpu.roll(d * s1, 32, 1) + pltpu.roll(d * s2, 96, 1)


def _chunk_of(pos):
    return jnp.where(pos < N_META, 0, 1 + ((pos - N_META) >> 6))


def _key_len(t):
    last_chunk = 1 + ((t + 1) * TR - 1 - N_META) // CHUNK
    return min(T, pl.cdiv(N_META + last_chunk * CHUNK, 128) * 128)


def _attn_probs(t, q_ref, kv_ref, kpe_ref, tq, tk):
    kl = _key_len(t)
    q = q_ref[...]
    qn = q[:, :QK_NOPE].astype(BF16)
    qp = _rot(q[:, QK_NOPE:], tq[0][...], tq[1][...], tq[2][...]).astype(BF16)
    kn = kv_ref[:kl, :QK_NOPE].astype(BF16)
    v = kv_ref[:kl, QK_NOPE:].astype(BF16)
    kp = _rot(kpe_ref[:kl, :], tk[0][:kl, :], tk[1][:kl, :], tk[2][:kl, :]).astype(BF16)
    s = lax.dot_general(qn, kn, _DIMS['nt'], preferred_element_type=F32)
    s = s + lax.dot_general(qp, kp, _DIMS['nt'], preferred_element_type=F32)
    qc = _chunk_of(t * TR + lax.broadcasted_iota(jnp.int32, (TR, 1), 0))
    kc = _chunk_of(lax.broadcasted_iota(jnp.int32, (1, kl), 1))
    s = jnp.where(kc <= qc, s * SM_SCALE, jnp.finfo(F32).min)
    p = jnp.exp(s - jnp.max(s, axis=-1, keepdims=True))
    p = p * (1.0 / jnp.sum(p, axis=-1, keepdims=True))
    return qn, qp, kn, kp, v, p


def _per_q_tile(fn):
    for t in range(T // TR):
        pl.when(pl.program_id(1) == t)(functools.partial(fn, t))


def _attn_specs(z, tabs):
    q_spec = pl.BlockSpec((None, TR, HEAD_PAD), lambda h, i: (h, i, 0))
    kv_spec = pl.BlockSpec((None, T, HEAD_PAD), lambda h, i: (h, 0, 0))
    kpe_spec = pl.BlockSpec((T, 128), lambda h, i: (0, (D_IN - QK_ROPE) // 128))
    tq = [pl.BlockSpec((TR, 128), lambda h, i: (i, 0))] * 3
    tk = [pl.BlockSpec((T, 128), lambda h, i: (0, 0))] * 3
    return [q_spec, kv_spec, kpe_spec] + tq + tk


def _attn_fwd(q, kv, z, tabs):
    def body(q_ref, kv_ref, kpe_ref, c1, c2, c3, k1, k2, k3, o_ref):
        def tile(t):
            _, _, _, _, v, p = _attn_probs(t, q_ref, kv_ref, kpe_ref, (c1, c2, c3), (k1, k2, k3))
            o_ref[...] = jnp.dot(p.astype(BF16), v, preferred_element_type=F32)

        _per_q_tile(tile)

    return pl.pallas_call(
        body, name="attn_fwd", grid=(N_HEADS, T // TR), in_specs=_attn_specs(z, tabs),
        out_specs=pl.BlockSpec((TR, V_HEAD), lambda h, i: (i, h)),
        out_shape=jax.ShapeDtypeStruct((T, N_HEADS * V_HEAD), F32),
        compiler_params=_params(2, 48 << 20),
    )(_in_hbm(q), _in_hbm(kv), _in_hbm(z), *tabs, *tabs)


def _attn_bwd(q, kv, z, tabs, dyb):
    def body(q_ref, kv_ref, kpe_ref, c1, c2, c3, k1, k2, k3, do_ref, dq_ref, dkv_ref, dkpe_ref):
        @pl.when(pl.program_id(1) == 0)
        def _():
            dkv_ref[...] = jnp.zeros(dkv_ref.shape, F32)
            dkpe_ref[...] = jnp.zeros(dkpe_ref.shape, F32)

        def tile(t):
            kl = _key_len(t)
            qn, qp, kn, kp, v, p = _attn_probs(t, q_ref, kv_ref, kpe_ref, (c1, c2, c3), (k1, k2, k3))
            do = do_ref[...].astype(BF16)
            dp = lax.dot_general(do, v, _DIMS['nt'], preferred_element_type=F32)
            ds = (p * (dp - jnp.sum(p * dp, axis=-1, keepdims=True)) * SM_SCALE).astype(BF16)
            pb = p.astype(BF16)
            dq_ref[:, :QK_NOPE] = jnp.dot(ds, kn, preferred_element_type=F32).astype(dq_ref.dtype)
            dqp = _derot(jnp.dot(ds, kp, preferred_element_type=F32), c1[...], c2[...], c3[...])
            dq_ref[:, QK_NOPE:] = dqp.astype(dq_ref.dtype)
            dkv_ref[:kl, :QK_NOPE] += lax.dot_general(ds, qn, _DIMS['tn'], preferred_element_type=F32)
            dkv_ref[:kl, QK_NOPE:] += lax.dot_general(pb, do, _DIMS['tn'], preferred_element_type=F32)
            dkpe_ref[:kl, :] += lax.dot_general(ds, qp, _DIMS['tn'], preferred_element_type=F32)

        _per_q_tile(tile)

    return pl.pallas_call(
        body, name="attn_bwd", grid=(N_HEADS, T // TR),
        in_specs=_attn_specs(z, tabs) + [pl.BlockSpec((TR, V_HEAD), lambda h, i: (i, h))],
        out_specs=[pl.BlockSpec((None, TR, HEAD_PAD), lambda h, i: (h, i, 0)),
                   pl.BlockSpec((None, T, HEAD_PAD), lambda h, i: (h, 0, 0)),
                   pl.BlockSpec((None, T, 128), lambda h, i: (h, 0, 0))],
        out_shape=[jax.ShapeDtypeStruct((N_HEADS, T, HEAD_PAD), BF16),
                   jax.ShapeDtypeStruct((N_HEADS, T, HEAD_PAD), F32),
                   jax.ShapeDtypeStruct((N_HEADS, T, 128), F32)],
        compiler_params=_params(2, 56 << 20),
    )(_in_hbm(q), _in_hbm(kv), _in_hbm(z), *tabs, *tabs, _in_hbm(dyb))


def _kpe_bwd(dkpe, tabs):
    def body(d_ref, c1, c2, c3, o_ref):
        d = d_ref[0]
        for h in range(1, N_HEADS):
            d = d + d_ref[h]
        o_ref[...] = _derot(d, c1[...], c2[...], c3[...]).astype(o_ref.dtype)

    tile = pl.BlockSpec((TR, 128), lambda i: (i, 0))
    return pl.pallas_call(
        body, name="kpe_bwd", grid=(T // TR,),
        in_specs=[pl.BlockSpec((N_HEADS, TR, 128), lambda i: (0, i, 0)), tile, tile, tile], out_specs=tile,
        out_shape=jax.ShapeDtypeStruct((T, 128), BF16), compiler_params=_params(1, 16 << 20),
    )(_in_hbm(dkpe), *tabs)


def _s5_prep_math(lam_re, lam_im, log_dt, bt_re, bt_im):
    dt = jnp.exp(log_dt)
    mag = jnp.exp(lam_re * dt)
    a_re, a_im = mag * jnp.cos(lam_im * dt), mag * jnp.sin(lam_im * dt)
    den = lam_re * lam_re + lam_im * lam_im
    w_re = ((a_re - 1.0) * lam_re + a_im * lam_im) / den
    w_im = (a_im * lam_re - (a_re - 1.0) * lam_im) / den
    return a_re, a_im, w_re[None] * bt_re - w_im[None] * bt_im, w_re[None] * bt_im + w_im[None] * bt_re


def _eye_groups():
    return (lax.broadcasted_iota(jnp.int32, (N_GROUPS, N_GROUPS), 0)
            == lax.broadcasted_iota(jnp.int32, (N_GROUPS, N_GROUPS), 1)).astype(F32)


def _row_to_col(row):
    return jnp.sum(_eye_groups() * row, axis=1, keepdims=True)


def _s5_prep(lam_re, lam_im, log_dt, bt_re, bt_im):
    def body(lr, li, ld, br, bi, ar, ai, bbr, bbi):
        ar[...], ai[...], bbr[...], bbi[...] = _s5_prep_math(lr[...], li[...], _row_to_col(ld[...]), br[...], bi[...])

    gp, cgp = jax.ShapeDtypeStruct((N_GROUPS, SSM_STATE), F32), jax.ShapeDtypeStruct(bt_re.shape, F32)
    return pl.pallas_call(body, name="s5_prep", out_shape=[gp, gp, cgp, cgp])(lam_re, lam_im, log_dt, bt_re, bt_im)


def _s5_prep_bwd(lam_re, lam_im, log_dt, bt_re, bt_im, da_re, da_im, dbb_re, dbb_im):
    def body(lr, li, ld, br, bi, dar, dai, dbr, dbi, o1, o2, o3, o4, o5):
        _, vjp = jax.vjp(_s5_prep_math, lr[...], li[...], _row_to_col(ld[...]), br[...], bi[...])
        o1[...], o2[...], dld, o4[...], o5[...] = vjp((dar[...], dai[...], dbr[...], dbi[...]))
        o3[...] = jnp.sum(_eye_groups() * dld, axis=0, keepdims=True)

    ins = (lam_re, lam_im, log_dt, bt_re, bt_im)
    return pl.pallas_call(body, name="s5_prep_bwd", out_shape=[jax.ShapeDtypeStruct(a.shape, F32) for a in ins])(
        *ins, da_re, da_im, dbb_re, dbb_im)


def _cmul(ar, ai, br, bi):
    return ar * br - ai * bi, ar * bi + ai * br


def _seg_rows(i):
    return pl.ds(pl.multiple_of(i * N_SEG, N_SEG), N_SEG)


def _scan(xr, xi, ar, ai, reverse):
    zero = jnp.zeros(ar.shape, F32)

    def local(j, carry):
        rows = _seg_rows(SEG - 1 - j if reverse else j)
        nr, ni = _cmul(ar, ai, *carry)
        nr, ni = nr + xr[rows, :], ni + xi[rows, :]
        xr[rows, :], xi[rows, :] = nr, ni
        return nr, ni

    er, ei = lax.fori_loop(0, SEG, local, (zero, zero))
    pr, pi = ar, ai
    for _ in range(8):
        pr, pi = _cmul(pr, pi, pr, pi)
    pr, pi = _cmul(*_cmul(pr, pi, ar, ai), ar, ai)
    row = lax.broadcasted_iota(jnp.int32, ar.shape, 0)
    edge, shift = (N_SEG - 1, N_SEG - 1) if reverse else (0, 1)
    hr, hi = zero, zero
    for _ in range(N_SEG - 1):
        tr_, ti_ = _cmul(pr, pi, hr, hi)
        hr = jnp.where(row == edge, 0.0, pltpu.roll(tr_ + er, shift, 0))
        hi = jnp.where(row == edge, 0.0, pltpu.roll(ti_ + ei, shift, 0))

    def fix(j, carry):
        rows = _seg_rows(SEG - 1 - j if reverse else j)
        xr[rows, :] += carry[0]
        xi[rows, :] += carry[1]
        return _cmul(ar, ai, *carry)

    lax.fori_loop(0, SEG, fix, _cmul(ar, ai, hr, hi))
    return hr, hi


def _split(x):
    hi = x.astype(BF16)
    return hi, (x - hi.astype(F32)).astype(BF16)


def _dot3(xs, ys, mode='nn'):
    d = lambda p, q: lax.dot_general(p, q, _DIMS[mode], preferred_element_type=F32)
    return d(xs[0], ys[0]) + (d(xs[1], ys[0]) + d(xs[0], ys[1]))


def _group_masks():
    ns = GQ * SSM_STATE
    rep = (lax.broadcasted_iota(jnp.int32, (SSM_STATE, ns), 1) % SSM_STATE
           == lax.broadcasted_iota(jnp.int32, (SSM_STATE, ns), 0)).astype(BF16)
    own = (lax.broadcasted_iota(jnp.int32, (128, ns), 0) // SSM_GROUP
           == lax.broadcasted_iota(jnp.int32, (128, ns), 1) // SSM_STATE)
    return rep, own


def _block_diag(x, rep, own):
    dot = lambda p: jnp.dot(p, rep, preferred_element_type=F32)
    return tuple(jnp.where(own, dot(part), 0.0).astype(BF16) for part in _split(x))


def _block_rows(full, rep, own):
    masked = _split(jnp.where(own, full, 0.0))
    dot = lambda p: lax.dot_general(p, rep, _DIMS['nt'], preferred_element_type=F32)
    return dot(masked[0]) + dot(masked[1])


def _s5_specs():
    col = pl.BlockSpec((T, 128), lambda q: (0, q))
    a_spec = pl.BlockSpec((None, 2, N_SEG, GQ * SSM_STATE), lambda q: (q, 0, 0, 0))
    rows = pl.BlockSpec((128, SSM_STATE), lambda q: (q, 0))
    d_spec = pl.BlockSpec((1, 128), lambda q: (0, q))
    return col, a_spec, rows, d_spec


def _s5_fwd(u, a8, bb_re, bb_im, c_re, c_im, dsk):
    def body(u_ref, a_ref, bre_ref, bim_ref, cre_ref, cim_ref, d_ref, y_ref, hre, him):
        u_ = u_ref[...]
        us = _split(u_)
        rep, own = _group_masks()
        hre[...] = _dot3(us, _block_diag(bre_ref[...], rep, own))
        him[...] = _dot3(us, _block_diag(bim_ref[...], rep, own))
        _scan(hre, him, a_ref[0], a_ref[1], reverse=False)
        y_ref[...] = (_dot3(_split(hre[...]), _block_diag(cre_ref[...], rep, own), 'nt')
                      - _dot3(_split(him[...]), _block_diag(cim_ref[...], rep, own), 'nt') + d_ref[...] * u_)

    col, a_spec, rows, d_spec = _s5_specs()
    return pl.pallas_call(
        body, name="s5_fwd", grid=(N_GROUPS // GQ,), in_specs=[col, a_spec, rows, rows, rows, rows, d_spec],
        out_specs=col, out_shape=jax.ShapeDtypeStruct((T, D_SSM), F32),
        scratch_shapes=[pltpu.VMEM((T, GQ * SSM_STATE), F32)] * 2,
        compiler_params=_params(1, 40 << 20),
    )(_in_hbm(u), a8, bb_re, bb_im, c_re, c_im, dsk)


def _s5_bwd(u, dy, a8, bb_re, bb_im, c_re, c_im, dsk):
    def body(u_ref, dy_ref, a_ref, bre_ref, bim_ref, cre_ref, cim_ref, d_ref,
             du_ref, dbre_ref, dbim_ref, dcre_ref, dcim_ref, da_ref, dd_ref, hre, him, lre, lim):
        u_, dy_ = u_ref[...], dy_ref[...]
        us, dys = _split(u_), _split(dy_)
        ar, ai = a_ref[0], a_ref[1]
        rep, own = _group_masks()
        bres, bims = _block_diag(bre_ref[...], rep, own), _block_diag(bim_ref[...], rep, own)
        hre[...] = _dot3(us, bres)
        him[...] = _dot3(us, bims)
        h0r, h0i = _scan(hre, him, ar, ai, reverse=False)
        lre[...] = _dot3(dys, _block_diag(cre_ref[...], rep, own))
        lim[...] = -_dot3(dys, _block_diag(cim_ref[...], rep, own))
        _scan(lre, lim, ar, -ai, reverse=True)

        def acc_da(i, carry):
            lr, li = lre[_seg_rows(i), :], lim[_seg_rows(i), :]
            pr, pi = hre[_seg_rows(i - 1), :], him[_seg_rows(i - 1), :]
            return carry[0] + lr * pr + li * pi, carry[1] + li * pr - lr * pi

        lr, li = lre[_seg_rows(0), :], lim[_seg_rows(0), :]
        dar, dai = lax.fori_loop(1, SEG, acc_da, (lr * h0r + li * h0i, li * h0r - lr * h0i))
        da_ref[0:1, :] = jnp.sum(dar, axis=0, keepdims=True)
        da_ref[1:2, :] = jnp.sum(dai, axis=0, keepdims=True)
        dot = lambda p, q, mode: lax.dot_general(p, q, _DIMS[mode], preferred_element_type=F32)
        lr1, li1 = lre[...].astype(BF16), lim[...].astype(BF16)
        du_ref[...] = dy_ * d_ref[...] + dot(lr1, bres[0], 'nt') + dot(li1, bims[0], 'nt')
        dbre_ref[...] = _block_rows(dot(us[0], lr1, 'tn'), rep, own)
        dbim_ref[...] = _block_rows(dot(us[0], li1, 'tn'), rep, own)
        dcre_ref[...] = _block_rows(dot(dys[0], hre[...].astype(BF16), 'tn'), rep, own)
        dcim_ref[...] = -_block_rows(dot(dys[0], him[...].astype(BF16), 'tn'), rep, own)
        dd_ref[...] = jnp.sum(dy_ * u_, axis=0, keepdims=True)

    col, a_spec, rows, d_spec = _s5_specs()
    nq, ns = N_GROUPS // GQ, GQ * SSM_STATE
    per_row = jax.ShapeDtypeStruct((N_GROUPS * SSM_GROUP, SSM_STATE), F32)
    return pl.pallas_call(
        body, name="s5_bwd", grid=(nq,),
        in_specs=[col, col, a_spec, rows, rows, rows, rows, d_spec],
        out_specs=[col, rows, rows, rows, rows, pl.BlockSpec((None, 2, ns), lambda q: (q, 0, 0)), d_spec],
        out_shape=[jax.ShapeDtypeStruct((T, D_SSM), F32), per_row, per_row, per_row, per_row,
                   jax.ShapeDtypeStruct((nq, 2, ns), F32), jax.ShapeDtypeStruct((1, D_SSM), F32)],
        scratch_shapes=[pltpu.VMEM((T, ns), F32)] * 4,
        compiler_params=_params(1, 56 << 20),
    )(_in_hbm(u), _in_hbm(dy), a8, bb_re, bb_im, c_re, c_im, dsk)


def _to_segments(a):
    return a.reshape(N_SEG, SEG, a.shape[-1]).transpose(1, 0, 2).reshape(T, a.shape[-1])


def _from_segments(a):
    return a.reshape(SEG, N_SEG, a.shape[-1]).transpose(1, 0, 2).reshape(T, a.shape[-1])


def _ff_specs():
    up_spec = pl.BlockSpec((2, None, T, 128), lambda p, cb: (0, p, 0, cb))
    piece = pl.BlockSpec((None, T, 128), lambda p, cb: (p, 0, cb))
    cw_spec = pl.BlockSpec((None, 3, 128), lambda p, cb: (p, 0, cb))
    cb_spec = pl.BlockSpec((None, 1, 128), lambda p, cb: (p, 0, cb))
    return up_spec, piece, cw_spec, cb_spec


def _conv_gate(gate, w, cb, pad_ref):
    pad_ref[0:8, :] = jnp.zeros((8, 128), F32)
    pad_ref[8:8 + T, :] = gate
    g1, g2 = pad_ref[pl.ds(7, T), :], pad_ref[pl.ds(6, T), :]
    return w[0:1, :] * g2 + w[1:2, :] * g1 + w[2:3, :] * gate + cb, g1, g2


def _ff_act(up, cw4, cb4):
    def body(up_ref, w_ref, b_ref, o_ref, pad_ref):
        gc, _, _ = _conv_gate(up_ref[0].astype(F32), w_ref[...], b_ref[...], pad_ref)
        o_ref[...] = (jax.nn.silu(gc) * up_ref[1].astype(F32)).astype(o_ref.dtype)

    up_spec, piece, cw_spec, cb_spec = _ff_specs()
    n_cb = pl.cdiv(FF_PIECE, 128)
    return pl.pallas_call(
        body, name="ff_act", grid=(4, n_cb), in_specs=[up_spec, cw_spec, cb_spec], out_specs=piece,
        out_shape=jax.ShapeDtypeStruct((4, T, FF_PIECE), BF16),
        scratch_shapes=[pltpu.VMEM((T + 8, 128), F32)],
        compiler_params=_params(2, 24 << 20),
    )(_in_hbm(up.reshape(2, 4, T, FF_PIECE)), cw4, cb4)


def _ff_act_bwd(up, dact, cw4, cb4):
    def body(up_ref, da_ref, w_ref, b_ref, dup_ref, dw_ref, db_ref, pad_ref, pad2_ref):
        gate, val, w = up_ref[0].astype(F32), up_ref[1].astype(F32), w_ref[...]
        gc, g1, g2 = _conv_gate(gate, w, b_ref[...], pad_ref)
        sg = jax.nn.sigmoid(gc)
        da = da_ref[...].astype(F32)
        dup_ref[1] = (da * gc * sg).astype(dup_ref.dtype)
        dgc = da * val * sg * (1.0 + gc * (1.0 - sg))
        db_ref[...] = jnp.sum(dgc, axis=0, keepdims=True)
        dw_ref[0:1, :] = jnp.sum(dgc * g2, axis=0, keepdims=True)
        dw_ref[1:2, :] = jnp.sum(dgc * g1, axis=0, keepdims=True)
        dw_ref[2:3, :] = jnp.sum(dgc * gate, axis=0, keepdims=True)
        pad2_ref[0:T, :] = dgc
        pad2_ref[T:T + 8, :] = jnp.zeros((8, 128), F32)
        d1, d2 = pad2_ref[pl.ds(1, T), :], pad2_ref[pl.ds(2, T), :]
        dup_ref[0] = (w[2:3, :] * dgc + w[1:2, :] * d1 + w[0:1, :] * d2).astype(dup_ref.dtype)

    up_spec, piece, cw_spec, cb_spec = _ff_specs()
    n_cb = pl.cdiv(FF_PIECE, 128)
    dup, dw, db = pl.pallas_call(
        body, name="ff_act_bwd", grid=(4, n_cb), in_specs=[up_spec, piece, cw_spec, cb_spec],
        out_specs=[up_spec, cw_spec, cb_spec],
        out_shape=[jax.ShapeDtypeStruct((2, 4, T, FF_PIECE), BF16), jax.ShapeDtypeStruct((4, 3, FF_PIECE), F32),
                   jax.ShapeDtypeStruct((4, 1, FF_PIECE), F32)],
        scratch_shapes=[pltpu.VMEM((T + 8, 128), F32)] * 2,
        compiler_params=_params(2, 32 << 20),
    )(_in_hbm(up.reshape(2, 4, T, FF_PIECE)), _in_hbm(dact), cw4, cb4)
    return dup.reshape(8, T, FF_PIECE), dw, db


def _local_step(x, tgt, meta, p, wf, late_weights, send_grads, send_small):
    sm = {}
    h0 = jnp.concatenate([meta, x], axis=0)
    tgt_pad = jnp.concatenate([jnp.zeros((N_META, D), F32), tgt], axis=0)
    tabs = _rope_tables()
    row = lambda a: a.reshape(1, -1)

    xn, r0 = _rowwise("rms_mix", _rms_fwd_fn, [(h0, D, 0)], [row(p['mix_norm'])], [(D, BF16), (1, F32)], [])
    z = _mm("in_proj", xn, wf['w_in'], 'nn', TR, 640, D, F32)
    lam_re, lam_im = p['lam_re'].reshape(N_GROUPS, SSM_STATE), p['lam_im'].reshape(N_GROUPS, SSM_STATE)
    log_dt = p['log_dt'].reshape(1, N_GROUPS)
    bt_re = p['b_re'].reshape(N_GROUPS, SSM_STATE, SSM_GROUP).transpose(2, 0, 1)
    bt_im = p['b_im'].reshape(N_GROUPS, SSM_STATE, SSM_GROUP).transpose(2, 0, 1)
    c_re = p['c_re'].reshape(N_GROUPS * SSM_GROUP, SSM_STATE)
    c_im = p['c_im'].reshape(N_GROUPS * SSM_GROUP, SSM_STATE)
    a_re, a_im, bbt_re, bbt_im = _s5_prep(lam_re, lam_im, log_dt, bt_re, bt_im)
    nq, ns = N_GROUPS // GQ, GQ * SSM_STATE
    a8 = jnp.broadcast_to(jnp.stack([a_re.reshape(nq, ns), a_im.reshape(nq, ns)], 1)[:, :, None, :],
                          (nq, 2, N_SEG, ns))
    gcp = lambda t: t.transpose(1, 0, 2).reshape(N_GROUPS * SSM_GROUP, SSM_STATE)
    bb_re, bb_im = gcp(bbt_re), gcp(bbt_im)
    dsk = row(p['d_skip'])
    u_seg = _to_segments(z[:, :D_SSM])
    y_ssm = _from_segments(_s5_fwd(u_seg, a8, bb_re, bb_im, c_re, c_im, dsk))
    g, = _rowwise("gelu", _gelu_fn, [(y_ssm, D_SSM, 0)], [], [(D_SSM, BF16)], [])
    gl = _mm("glu_proj", g, wf['w_glu'], 'nn', TR, D_SSM, D_SSM, F32)
    yan, ra = _rowwise("glu_norm", _glu_fwd_fn, [(y_ssm, D_SSM, 0), (gl, D_SSM, 0)],
                       [row(p['b_glu']), row(p['out_norm_ssm'])], [(D_SSM, BF16), (1, F32)], [])
    qn, rq = _rowwise("rms_q", _rms_fwd_fn, [(z, Q_LORA, D_SSM // Q_LORA)], [row(p['q_a_norm'])],
                      [(Q_LORA, BF16), (1, F32)], [])
    kvn, rkv = _rowwise("rms_kv", _rms_fwd_fn, [(z, KV_LORA, (D_SSM + Q_LORA) // KV_LORA)], [row(p['kv_a_norm'])],
                        [(KV_LORA, BF16), (1, F32)], [])
    q = _mm("q_proj", qn, wf['w_q_b'], 'nn', T, HEAD_PAD, Q_LORA, F32, stack='b_out')
    kv = _mm("kv_proj", kvn, wf['w_kv_b'], 'nn', T, HEAD_PAD, KV_LORA, F32, stack='b_out')
    yb = _attn_fwd(q, kv, z, tabs)
    ybn, rb = _rowwise("rms_attn", _rms_fwd_fn, [(yb, D_SSM, 0)], [row(p['out_norm_attn'])],
                       [(D_SSM, BF16), (1, F32)], [])
    y = jnp.concatenate([yan, ybn], axis=1)
    wf = dict(wf)
    wf['w_out'], token, ffn_weights = late_weights(y)
    h1 = _mm("out_proj", y, wf['w_out'], 'nn', TR, 1024, D, F32, res=h0)
    xn2, r1 = _rowwise("rms_ffn", _rms_fwd_fn, [(h1, D, 0)], [row(p['ffn_norm']) + token[0:1, 0:1]],
                       [(D, BF16), (1, F32)], [])
    wf['w_up'], wf['w_down'] = ffn_weights(xn2)
    up = _mm("up_proj", xn2, wf['w_up'], 'nn', TR, FF_PIECE, D, BF16, stack='b_out')
    cw4 = wf['conv_w'].reshape(3, 4, FF_PIECE).transpose(1, 0, 2)
    cb4 = p['conv_b'].reshape(4, 1, FF_PIECE)
    act = _ff_act(up, cw4, cb4)
    wd4 = wf['w_down'].reshape(4, FF_PIECE, D)
    h2 = _mm("down_proj", act, wd4, 'nn', TR, 1024, FF_PIECE, F32, stack='ab_red', res=h1)
    dh2, sm['final_norm'], loss = _rowwise("loss", _loss_fn, [(h2, D, 0), (tgt_pad, D, 0)], [row(p['final_norm'])],
                                           [(D, F32)], [D, 128])

    big = {}
    dact = _mm("down_bwd_x", dh2, wd4, 'nt', TR, FF_PIECE, D, BF16, stack='b_out')
    g_down = _mm("down_bwd_w", act, dh2, 'tn', FF_PIECE, 512, T, BF16, stack='a_out').reshape(
        N_DEV, D_FF // N_DEV, D)
    token = send_grads(['w_down'], [g_down], 'cores')
    dup, dcw4, sm['conv_b'] = _ff_act_bwd(up, dact, cw4, cb4 + token[0:1, 0:1])
    dxn2 = _mm("up_bwd_x", dup, wf['w_up'], 'nt', TR, 1024, FF_PIECE, F32, stack='ab_red')
    g_up = _mm("up_bwd_w", dup, xn2, 'tn', FF_PIECE, 1024, T, BF16, stack='a_out')
    token = send_grads(['w_up'], [g_up], 'cores') + send_grads(['w_down'], g_up, 'chips')
    dh1, sm['ffn_norm'] = _rowwise("rms_ffn_bwd", _rms_bwd_res_fn,
                                   [(dxn2, D, 0), (h1, D, 0), (r1, 1, 0), (dh2, D, 0)],
                                   [row(p['ffn_norm']) + token[0:1, 0:1]], [(D, F32)], [D])
    dy = _mm("out_bwd_x", dh1, wf['w_out'], 'nt', TR, 1024, D, F32)
    g_out = _mm("out_bwd_w", y, dh1, 'tn', 1024, 1024, T, BF16).reshape(N_DEV, D // N_DEV, D)
    dgl, dgd, sm['out_norm_ssm'], sm['b_glu'] = _rowwise(
        "glu_bwd", _glu_bwd_fn, [(dy, D_SSM, 0), (y_ssm, D_SSM, 0), (gl, D_SSM, 0), (ra, 1, 0)],
        [row(p['b_glu']), row(p['out_norm_ssm'])], [(D_SSM, BF16), (D_SSM, F32)], [D_SSM, D_SSM])
    dg = _mm("glu_bwd_x", dgl, wf['w_glu'], 'nt', TR, D_SSM, D_SSM, F32, res=dgd)
    g_glu = _mm("glu_bwd_w", g, dgl, 'tn', D_SSM, D_SSM, T, BF16).reshape(N_DEV, D_SSM // N_DEV, D_SSM)
    dy_ssm, = _rowwise("gelu_bwd", _gelu_bwd_fn, [(dg, D_SSM, 0), (y_ssm, D_SSM, 0)], [], [(D_SSM, F32)], [])
    token = send_grads(['w_up'], g_glu, 'chips')
    dyb, sm['out_norm_attn'] = _rowwise("rms_attn_bwd", _rms_bwd_fn, [(dy, D_SSM, 1), (yb, D_SSM, 0), (rb, 1, 0)],
                                        [row(p['out_norm_attn']) + token[0:1, 0:1]], [(D_SSM, F32)], [D_SSM])
    dq, dkv, dkpe = _attn_bwd(q, kv, z, tabs, dyb)
    g_q = _mm("q_bwd_w", qn, dq, 'tn', Q_LORA, HEAD_PAD, T, BF16, stack='b_out')
    dqn = _mm("q_bwd_x", dq, wf['w_q_b'], 'nt', TR, Q_LORA, HEAD_PAD, F32, stack='ab_red')
    g_kv = _mm("kv_bwd_w", kvn, dkv, 'tn', KV_LORA, HEAD_PAD, T, BF16, stack='b_out')
    dkvn = _mm("kv_bwd_x", dkv, wf['w_kv_b'], 'nt', TR, KV_LORA, HEAD_PAD, F32, stack='ab_red')
    token = send_grads(['w_out', 'w_glu', 'w_q_b', 'w_kv_b'], [g_out, g_glu, g_q, g_kv], 'both')
    dq_a, sm['q_a_norm'] = _rowwise("rms_q_bwd", _rms_bwd_fn,
                                    [(dqn, Q_LORA, 0), (z, Q_LORA, D_SSM // Q_LORA), (rq, 1, 0)],
                                    [row(p['q_a_norm'])], [(Q_LORA, BF16)], [Q_LORA])
    dkv_a, sm['kv_a_norm'] = _rowwise("rms_kv_bwd", _rms_bwd_fn,
                                      [(dkvn, KV_LORA, 0), (z, KV_LORA, (D_SSM + Q_LORA) // KV_LORA), (rkv, 1, 0)],
                                      [row(p['kv_a_norm'])], [(KV_LORA, BF16)], [KV_LORA])
    dk_pe = _kpe_bwd(dkpe, tabs)
    du_seg, dbre, dbim, dcre, dcim, da, sm['d_skip'] = _s5_bwd(
        u_seg, _to_segments(dy_ssm), a8, bb_re, bb_im, c_re, c_im, dsk + token[0:1, 0:1])
    du = _from_segments(du_seg).astype(BF16)
    cgp = lambda d: d.reshape(N_GROUPS, SSM_GROUP, SSM_STATE).transpose(1, 0, 2)
    dlam_re, dlam_im, dlog_dt, dbt_re, dbt_im = _s5_prep_bwd(
        lam_re, lam_im, log_dt, bt_re, bt_im, da[:, 0, :].reshape(N_GROUPS, SSM_STATE),
        da[:, 1, :].reshape(N_GROUPS, SSM_STATE), cgp(dbre), cgp(dbim))
    sm['lam_re'], sm['lam_im'], sm['log_dt'] = dlam_re, dlam_im, dlog_dt
    sm['b_re'], sm['b_im'] = gcp(dbt_re), gcp(dbt_im)
    sm['c_re'], sm['c_im'] = dcre, dcim
    token = send_small(sm, dcw4, loss)
    dz = jnp.concatenate([du, dq_a, dkv_a, dk_pe], axis=1)
    dxn = _mm("in_bwd_x", dz, wf['w_in'], 'nt', TR, 1024, D_IN_PAD, F32)
    big['w_in'] = _mm("in_bwd_w", xn, dz, 'tn', 1024, 640, T, BF16).reshape(N_DEV, D // N_DEV, D_IN_PAD)
    dh0, d_mix_norm = _rowwise("rms_mix_bwd", _rms_bwd_res_fn,
                               [(dxn, D, 0), (h0, D, 0), (r0, 1, 0), (dh1, D, 0)],
                               [row(p['mix_norm']) + token[0:1, 0:1]], [(D, F32)], [D])
    return dh0[N_META:], dh0[:N_META], d_mix_norm, big


def _place():
    x, y, c = lax.axis_index("x"), lax.axis_index("y"), lax.axis_index("c")
    return x, y, c, [(1 - x, y), (x, 1 - y), (1 - x, 1 - y)]


_HBM = pl.BlockSpec(memory_space=pltpu.HBM)


def _all_gather(name, arrs, chip_blocks=()):
    n, nc = len(arrs), len(chip_blocks)

    def body(*refs):
        ins, cins, outs, couts = refs[:n], refs[n:n + nc], refs[n + nc:2 * n + nc], refs[2 * n + nc:2 * (n + nc)]
        send, recv, loc, csend, crecv = refs[2 * (n + nc):]
        x, y, c, chips = _place()
        me, sib = (x, y, c), (x, y, 1 - c)

        def cp(a, k, blk, to, src=None):
            dst = outs[a].at[4 * blk[0] + 2 * blk[1] + blk[2]]
            return pltpu.make_async_remote_copy(src_ref=dst if src is None else src, dst_ref=dst,
                                                send_sem=send.at[a, k], recv_sem=recv.at[a, k],
                                                device_id=to, device_id_type=MESH)

        between_chips = [pltpu.make_async_remote_copy(src_ref=cins[a].at[2 * ch[0] + ch[1]], dst_ref=couts[a].at[j],
                                                      send_sem=csend.at[a, j], recv_sem=crecv.at[a, j],
                                                      device_id=(*ch, c), device_id_type=MESH)
                         for a in range(nc) for j, ch in enumerate(chips)]
        mine = [pltpu.make_async_copy(ins[a], outs[a].at[4 * x + 2 * y + c], loc.at[a]) for a in range(n)]
        first = []
        for a in range(n):
            first.append(cp(a, 0, me, sib, src=ins[a]))
            first += [cp(a, 1 + j, me, (*ch, c), src=ins[a]) for j, ch in enumerate(chips)]
        for f in mine + first + between_chips:
            f.start()
        passed = []
        for a in range(n):
            for j, ch in enumerate(chips):
                cp(a, 1 + j, (*ch, c), me).wait_recv()
                passed.append(cp(a, 4 + j, (*ch, c), sib))
                passed[-1].start()
        for a in range(n):
            cp(a, 0, sib, me).wait_recv()
            for j, ch in enumerate(chips):
                cp(a, 4 + j, (*ch, 1 - c), me).wait_recv()
        for f in first + passed:
            f.wait_send()
        for m in mine + between_chips:
            m.wait()

    out = pl.pallas_call(
        body, name=name, in_specs=[_HBM] * (n + nc), out_specs=[_HBM] * (n + nc),
        out_shape=[jax.ShapeDtypeStruct((N_DEV,) + a.shape, a.dtype) for a in arrs]
        + [jax.ShapeDtypeStruct((3,) + a.shape[1:], a.dtype) for a in chip_blocks],
        scratch_shapes=[pltpu.SemaphoreType.DMA((n, 7)), pltpu.SemaphoreType.DMA((n, 7)),
                        pltpu.SemaphoreType.DMA((n,)), pltpu.SemaphoreType.DMA((max(nc, 1), 3)),
                        pltpu.SemaphoreType.DMA((max(nc, 1), 3))],
    )(*arrs, *chip_blocks)
    return (out[:n], out[n:]) if nc else out


def _exchange_cores(name, arrs):
    n = len(arrs)

    def body(*refs):
        ins, outs = refs[:n], refs[n:2 * n]
        send, recv = refs[2 * n:]
        x, y, c, _ = _place()
        remote = [pltpu.make_async_remote_copy(src_ref=ins[a].at[2 * k + 1 - c], dst_ref=outs[a].at[k],
                                               send_sem=send.at[a, k], recv_sem=recv.at[a, k],
                                               device_id=(x, y, 1 - c), device_id_type=MESH)
                  for a in range(n) for k in range(4)]
        for d in remote:
            d.start()
        for d in remote:
            d.wait()

    return pl.pallas_call(
        body, name=name, in_specs=[_HBM] * n, out_specs=[_HBM] * n,
        out_shape=[jax.ShapeDtypeStruct((4,) + a.shape[1:], a.dtype) for a in arrs],
        scratch_shapes=[pltpu.SemaphoreType.DMA((n, 4))] * 2,
    )(*arrs)


_SEM = pl.BlockSpec(memory_space=pltpu.SEMAPHORE)
_EFFECT = pltpu.SideEffectType.DATAFLOW_SIDE_EFFECTING


def _split_exchange(name, srcs, land_shapes, copies_of, n_cp, after, all_of=None):
    n = len(srcs)
    in_place = land_shapes is None
    lands = [] if in_place else [pltpu.with_memory_space_constraint(lax.empty(s, a.dtype), pltpu.HBM)
                                 for s, a in zip(land_shapes, srcs)]
    nb = n + len(lands)

    per = 1 if all_of else n_cp
    n_sem = n * per

    def descriptors(refs, send, recv, mirror):
        src_refs, land_refs = refs[:n], refs[nb - n:nb]
        x, y, c, chips = _place()
        out = []
        for a in range(n):
            if mirror and all_of:
                whole = all_of(a, land_refs[a])
                out.append(pltpu.make_async_remote_copy(src_ref=whole, dst_ref=whole, send_sem=send[a],
                                                        recv_sem=recv[a], device_id=(x, y, 1 - c),
                                                        device_id_type=MESH))
                continue
            for k, (src, dst, to, back) in enumerate(copies_of(a, src_refs[a], land_refs[a], x, y, c, chips)):
                out.append(pltpu.make_async_remote_copy(src_ref=src, dst_ref=back if mirror else dst,
                                                        send_sem=send[a * per + k % per],
                                                        recv_sem=recv[a * per + k % per],
                                                        device_id=to, device_id_type=MESH))
        return out

    def start_body(*refs):
        sems = refs[nb + 1:nb + 1 + 2 * n_sem]
        for d in descriptors(refs, sems[:n_sem], sems[n_sem:], False):
            d.start()
        refs[-1][...] = jnp.zeros((8, 128), F32)

    hbm_like = lambda arrs: [pltpu.HBM(a.shape, a.dtype) for a in arrs]
    res = pl.pallas_call(
        start_body, name=name + "_start", in_specs=[_HBM] * nb + [pl.BlockSpec(memory_space=pl.ANY)],
        out_specs=[_SEM] * (2 * n_sem) + [_HBM] * nb + [pl.BlockSpec(memory_space=pltpu.VMEM)],
        out_shape=[pltpu.SemaphoreType.DMA(())] * (2 * n_sem)
        + hbm_like(srcs) + hbm_like(lands) + [jax.ShapeDtypeStruct((8, 128), F32)],
        input_output_aliases={i: 2 * n_sem + i for i in range(nb)},
        compiler_params=pltpu.CompilerParams(has_side_effects=_EFFECT),
    )(*[pltpu.with_memory_space_constraint(a, pltpu.HBM) for a in srcs], *lands, after)
    sems, thru, token = res[:2 * n_sem], res[2 * n_sem:2 * n_sem + nb], res[-1]

    def wait(after):
        def wait_body(*refs):
            s = refs[nb:nb + 2 * n_sem]
            for d in descriptors(refs, s[:n_sem], s[n_sem:], True):
                d.wait_send()
                d.wait_recv()

        out = pl.pallas_call(
            wait_body, name=name + "_wait",
            in_specs=[_HBM] * nb + [_SEM] * (2 * n_sem) + [pl.BlockSpec(memory_space=pl.ANY)],
            out_specs=[_HBM] * nb, out_shape=hbm_like(srcs) + hbm_like(lands),
            input_output_aliases={i: i for i in range(nb)},
            compiler_params=pltpu.CompilerParams(has_side_effects=_EFFECT),
        )(*thru, *sems, after)
        return out[:n], out[nb - n:]

    return token, wait


def _gather_copies(a, src, land, x, y, c, chips):
    peers = [(x, y, 1 - c)] + [(*ch, c) for ch in chips]
    me = 4 * x + 2 * y + c
    return [(src, land.at[me], to, land.at[4 * to[0] + 2 * to[1] + to[2]]) for to in peers]


def _to_all_copies(a, src, land, x, y, c, chips):
    peers = [(x, y, 1 - c)] + [(*ch, cc) for ch in chips for cc in (c, 1 - c)]
    me = 4 * x + 2 * y + c
    return [(src, land.at[me], to, land.at[4 * to[0] + 2 * to[1] + to[2]]) for to in peers]


def _cores_copies(a, src, land, x, y, c, chips):
    return [(src.at[2 * k + 1 - c], land.at[k], (x, y, 1 - c), land.at[k]) for k in range(4)]


def _forward_copies(a, src, land, x, y, c, chips):
    slot = lambda ch, core: 4 * ch[0] + 2 * ch[1] + core
    return [(land.at[slot(ch, c)], land.at[slot(ch, c)], (x, y, 1 - c), land.at[slot(ch, 1 - c)]) for ch in chips]


def _chips_copies(a, src, land, x, y, c, chips):
    return [(src.at[2 * ch[0] + ch[1]], land.at[j], (*ch, c), land.at[j]) for j, ch in enumerate(chips)]


def _gather_forward(name, arrs):
    n = len(arrs)

    def body(*refs):
        ins, outs = refs[:n], refs[n:2 * n]
        send, recv = refs[2 * n:]
        x, y, c, chips = _place()
        sends, recvs = [], []
        for a in range(n):
            for j, ch in enumerate(chips):
                mine, theirs = 4 * ch[0] + 2 * ch[1] + c, 4 * ch[0] + 2 * ch[1] + 1 - c
                sends.append(pltpu.make_async_remote_copy(src_ref=ins[a].at[mine], dst_ref=outs[a].at[mine],
                                                          send_sem=send.at[a, j], recv_sem=recv.at[a, j],
                                                          device_id=(x, y, 1 - c), device_id_type=MESH))
                recvs.append(pltpu.make_async_remote_copy(src_ref=ins[a].at[theirs], dst_ref=outs[a].at[theirs],
                                                          send_sem=send.at[a, j], recv_sem=recv.at[a, j],
                                                          device_id=(x, y, 1 - c), device_id_type=MESH))
        for d in sends:
            d.start()
        for s, r in zip(sends, recvs):
            s.wait_send()
            r.wait_recv()

    return pl.pallas_call(
        body, name=name, in_specs=[_HBM] * n, out_specs=[_HBM] * n,
        out_shape=[jax.ShapeDtypeStruct(a.shape, a.dtype) for a in arrs],
        input_output_aliases={i: i for i in range(n)},
        scratch_shapes=[pltpu.SemaphoreType.DMA((n, 3))] * 2,
    )(*arrs)


def _blocks_of(r, c):
    if r * c * 4 <= (2 << 20):
        return r, c
    if r % 128 == 0:
        return 128, c
    return r, 256


def _pair_sum(name, own, got, core):
    _, r, c = got.shape
    rb, cb = _blocks_of(r, c)

    def body(s_ref, a_ref, b_ref, o_ref):
        o_ref[...] = (a_ref[...].astype(F32) + b_ref[...].astype(F32)).astype(o_ref.dtype)

    return pl.pallas_call(
        body, name=name, out_shape=jax.ShapeDtypeStruct((4, r, c), got.dtype),
        grid_spec=pltpu.PrefetchScalarGridSpec(
            num_scalar_prefetch=1, grid=(4, r // rb, c // cb),
            in_specs=[pl.BlockSpec((None, rb, cb), lambda k, i, j, s: (2 * k + s[0], i, j)),
                      pl.BlockSpec((None, rb, cb), lambda k, i, j, s: (k, i, j))],
            out_specs=pl.BlockSpec((None, rb, cb), lambda k, i, j, s: (k, i, j))),
        compiler_params=_params(3, 24 << 20),
    )(core, _in_hbm(own), _in_hbm(got))


def _adamw_math(w, g, m, v):
    m = B1 * m + (1.0 - B1) * g
    v = B2 * v + (1.0 - B2) * (g * g)
    m_hat = m / (1.0 - B1 ** STEP)
    v_hat = v / (1.0 - B2 ** STEP)
    return -LR * (m_hat / (jnp.sqrt(v_hat) + ADAM_EPS) + WD * w), m, v


def _adamw_big(name, own, got, chip, w, m, v):
    r, c = w.shape
    rb, cb = _blocks_of(r, c)

    def body(s_ref, o_ref, p_ref, w_ref, m_ref, v_ref, g_ref, d_ref, nm_ref, nv_ref):
        g = o_ref[...].astype(F32)
        for k in range(3):
            g = g + p_ref[k].astype(F32)
        g_ref[...] = g
        d_ref[...], nm_ref[...], nv_ref[...] = _adamw_math(w_ref[...], g, m_ref[...], v_ref[...])

    blk = pl.BlockSpec((rb, cb), lambda i, j, s: (i, j))
    return pl.pallas_call(
        body, name=name, out_shape=[jax.ShapeDtypeStruct((r, c), F32)] * 4,
        grid_spec=pltpu.PrefetchScalarGridSpec(
            num_scalar_prefetch=1, grid=(r // rb, c // cb),
            in_specs=[pl.BlockSpec((None, rb, cb), lambda i, j, s: (s[0], i, j)),
                      pl.BlockSpec((3, rb, cb), lambda i, j, s: (0, i, j)), blk, blk, blk],
            out_specs=[blk] * 4),
        compiler_params=_params(2, 40 << 20),
    )(chip, *map(_in_hbm, (own, got, w, m, v)))


def _adamw_multi(name, items, sums=(), row_blocks=1, own=(), dev=None):
    n, ns = len(items), len(sums)
    n_own = len(own)
    assert n_own in (0, n + ns)

    def body(*refs):
        dev_ref = refs[0] if n_own else None
        refs = refs[1:] if n_own else refs
        owns, refs = refs[:n_own], refs[n_own:]
        ins, outs = refs[:4 * n + ns], refs[4 * n + ns:]

        def total(ref, t):
            g = None
            for d in range(ref.shape[0]):
                part = jnp.where(dev_ref[0] == d, owns[t][...], ref[d]) if n_own else ref[d]
                g = part.astype(F32) if g is None else g + part.astype(F32)
            return g

        for t in range(n):
            p_ref, w_ref, m_ref, v_ref = ins[4 * t:4 * t + 4]
            g = total(p_ref, t)
            outs[4 * t][...] = g
            outs[4 * t + 1][...], outs[4 * t + 2][...], outs[4 * t + 3][...] = _adamw_math(
                w_ref[...], g, m_ref[...], v_ref[...])
        for t in range(ns):
            outs[4 * n + t][...] = total(ins[4 * n + t], n + t)

    def spec(shape, lead):
        blk = (shape[0] // row_blocks,) + tuple(shape[1:])
        nd = len(shape)
        if lead:
            return pl.BlockSpec((lead,) + blk, lambda i: (0, i) + (0,) * (nd - 1))
        return pl.BlockSpec(blk, lambda i: (i,) + (0,) * (nd - 1))

    operands, in_specs, out_specs, out_shape = [], [], [], []
    if n_own:
        operands += [dev] + list(own)
        in_specs += [pl.BlockSpec(memory_space=pltpu.SMEM)] + [spec(o.shape, 0) for o in own]
    for parts, w, m, v in items:
        assert parts.shape[1:] == w.shape == m.shape == v.shape, (name, parts.shape, w.shape)
        operands += [parts, w, m, v]
        in_specs += [spec(w.shape, parts.shape[0])] + [spec(w.shape, 0)] * 3
        out_specs += [spec(w.shape, 0)] * 4
        out_shape += [jax.ShapeDtypeStruct(w.shape, F32)] * 4
    for parts in sums:
        operands.append(parts)
        in_specs.append(spec(parts.shape[1:], parts.shape[0]))
        out_specs.append(spec(parts.shape[1:], False))
        out_shape.append(jax.ShapeDtypeStruct(parts.shape[1:], F32))
    return pl.pallas_call(
        body, name=name, grid=(row_blocks,), in_specs=in_specs, out_specs=out_specs, out_shape=out_shape,
        compiler_params=_params(1, 56 << 20),
    )(*operands)


def kernel(x, meta_tokens, mix_norm, w_in, lam_re, lam_im, log_dt, b_re, b_im, c_re, c_im, d_skip, w_glu, b_glu, q_a_norm, w_q_b, kv_a_norm, w_kv_b, out_norm_ssm, out_norm_attn, w_out, ffn_norm, w_up, conv_w, conv_b, w_down, final_norm, loss_target, m_meta_tokens, m_mix_norm, m_w_in, m_lam_re, m_lam_im, m_log_dt, m_b_re, m_b_im, m_c_re, m_c_im, m_d_skip, m_w_glu, m_b_glu, m_q_a_norm, m_w_q_b, m_kv_a_norm, m_w_kv_b, m_out_norm_ssm, m_out_norm_attn, m_w_out, m_ffn_norm, m_w_up, m_conv_w, m_conv_b, m_w_down, m_final_norm, v_meta_tokens, v_mix_norm, v_w_in, v_lam_re, v_lam_im, v_log_dt, v_b_re, v_b_im, v_c_re, v_c_im, v_d_skip, v_w_glu, v_b_glu, v_q_a_norm, v_w_q_b, v_kv_a_norm, v_w_kv_b, v_out_norm_ssm, v_out_norm_attn, v_w_out, v_ffn_norm, v_w_up, v_conv_w, v_conv_b, v_w_down, v_final_norm):
    given = dict(locals())
    w = {n: given[n] for n in WEIGHTS}
    m = {n: given['m_' + n] for n in WEIGHTS}
    v = {n: given['v_' + n] for n in WEIGHTS}
    dev = 4 * lax.axis_index("x") + 2 * lax.axis_index("y") + lax.axis_index("c")

    shard = {
        'w_in': jnp.pad(w_in[0], ((0, 0), (0, D_IN_PAD - D_IN))),
        'w_glu': w_glu[0],
        'w_q_b': jnp.pad(w_q_b[0], ((0, 0), (0, HEAD_PAD - QK_NOPE - QK_ROPE))),
        'w_kv_b': w_kv_b[0],
        'w_out': w_out[0],
        'w_up': w_up[0],
        'w_down': w_down[0],
    }
    core = lax.axis_index("c").astype(jnp.int32).reshape(1)
    chip = (2 * lax.axis_index("x") + lax.axis_index("y")).astype(jnp.int32).reshape(1)
    early, late = ['w_in', 'w_glu', 'w_q_b', 'w_kv_b'], ['w_out', 'w_up', 'w_down']
    shard = {n: a.astype(BF16) for n, a in shard.items()}

    gathered = _all_gather("gather_early", [shard[n] for n in early]
                           + [meta_tokens, jnp.pad(conv_w[0], ((0, 5), (0, 0)))])
    wf = dict(zip(early, gathered[:len(early)]))
    for n in ('w_in', 'w_glu'):
        wf[n] = wf[n].reshape(-1, wf[n].shape[-1])
    meta = gathered[-2].transpose(1, 0, 2).reshape(N_META, D)
    wf['conv_w'] = gathered[-1][:, :3].transpose(1, 0, 2).reshape(3, D_FF)
    gather_token, wait_late = _split_exchange("gather_late", [shard[n] for n in late],
                                              [(N_DEV,) + shard[n].shape for n in late], _gather_copies, 4,
                                              after=gathered[0])

    def late_weights(after):
        mine, landed = wait_late(after)
        full = [lax.dynamic_update_slice(a, s[None], (dev, 0, 0)) for s, a in zip(mine, landed)]
        full_out, = _gather_forward("gather_forward_w_out", full[:1])
        token, wait_ffn = _split_exchange("gather_forward_ffn", full[1:], None, _forward_copies, 3, after=full_out)

        def ffn_weights(after):
            full_up, full_down = wait_ffn(after)[0]
            return full_up, full_down.reshape(-1, D)

        return full_out.reshape(-1, D), token, ffn_weights

    pending, between_cores = [], {}

    def send_grads(names, arg, stage):
        if stage == 'cores':
            token, between_cores[names[0]] = _split_exchange(
                "reduce_cores_" + names[0], arg, [(4,) + a.shape[1:] for a in arg], _cores_copies, 4, after=arg[-1])
            return token
        mine, got = between_cores.pop(names[0])(arg) if stage == 'chips' else (
            arg, _exchange_cores("reduce_cores_" + names[0], arg))
        sums = [_pair_sum("sum_cores_" + n, a, g, core) for n, a, g in zip(names, mine, got)]
        token, wait = _split_exchange("reduce_chips_" + names[0], sums, [(3,) + s.shape[1:] for s in sums],
                                      _chips_copies, 3, after=sums[-1])
        pending.append((names, wait))
        return token

    small_sent = []
    s5_bc = ['b_re', 'b_im', 'c_re', 'c_im']

    def send_small(sm, dcw4, loss_part):
        names = [n for n in SMALL if n in sm]
        arrs = [sm[n].reshape(-1, 128).astype(BF16) if n in s5_bc else sm[n] for n in names] + [dcw4, loss_part]
        token, wait = _split_exchange("gather_small", arrs, [(N_DEV,) + a.shape for a in arrs], _to_all_copies, 7,
                                      after=arrs[-1], all_of=lambda a, land: land.at[pl.ds(0, N_DEV - 1)])
        small_sent.append((names, wait))
        return token

    small = {n: w[n] for n in SMALL}
    small['mix_norm'] = mix_norm + gather_token[0:1, 0:1]
    grad_x, dmeta, d_mix_norm, big = _local_step(x[0], loss_target[0], meta, small, wf, late_weights, send_grads,
                                                  send_small)
    grads, deltas, new_m, new_v = {}, {}, {}, {}

    def keep(n, outs):
        grads[n], deltas[n], new_m[n], new_v[n] = (o.reshape(w[n].shape) for o in outs)

    rest = list(big)
    from_core = _exchange_cores("reduce_cores", [big[n] for n in rest])
    chip_sums = [_pair_sum("sum_cores_" + n, big[n], got, core) for n, got in zip(rest, from_core)]
    last = chip_sums + [d_mix_norm, dmeta]
    n_big = len(chip_sums)
    _, wait_last = _split_exchange(
        "reduce_last", last, [(3,) + s.shape[1:] for s in chip_sums] + [(N_DEV,) + a.shape for a in last[n_big:]],
        lambda a, *at: _chips_copies(a, *at) if a < n_big else _to_all_copies(a, *at), 7, after=dmeta,
        all_of=lambda a, land: land if a < n_big else land.at[pl.ds(0, N_DEV - 1)])

    def update(n, own, got):
        if n == 'w_up':
            outs = _adamw_big("adamw_" + n, own, got, chip, w[n][0].T, m[n][0].T, v[n][0].T)
            return keep(n, [o.T for o in outs])
        cols = w[n].shape[2]
        keep(n, _adamw_big("adamw_" + n, own[:, :, :cols], got[:, :, :cols], chip, w[n][0], m[n][0], v[n][0]))

    for names, wait in pending:
        for n, own, got in zip(names, *wait(chip_sums[0])):
            update(n, own, got)
    sent, landed = wait_last(grads['w_up'])
    for n, own, got in zip(rest, sent, landed):
        update(n, own, got)

    device = dev.astype(jnp.int32).reshape(1)
    outs = _adamw_multi("adamw_last", [(landed[n_big], mix_norm, m['mix_norm'], v['mix_norm'])],
                        sums=[landed[n_big + 1]], own=sent[n_big:], dev=device)
    keep('mix_norm', outs[:4])
    g_meta = outs[4]
    names, wait = small_sent[0]
    mine, landed = wait(grads['mix_norm'])
    parts, own = dict(zip(names + ['conv_w', 'loss'], landed)), dict(zip(names + ['conv_w', 'loss'], mine))
    wide = [n for n in names if n not in s5_bc]
    summed = ['conv_w', 'loss'] + s5_bc
    outs = _adamw_multi("adamw_small",
                        [(parts[n],) + tuple(t[n].reshape(own[n].shape) for t in (w, m, v)) for n in wide],
                        sums=[parts[n] for n in summed], own=[own[n] for n in wide + summed], dev=device)
    for i, n in enumerate(wide):
        keep(n, outs[4 * i:4 * i + 4])
    g_cw4, loss = outs[-6:-4]
    rows_gc = lambda a, n: (a.reshape(N_GROUPS, SSM_STATE, SSM_GROUP).transpose(0, 2, 1) if n[0] == 'b' else a
                            ).reshape(N_GROUPS * SSM_GROUP, SSM_STATE)
    outs = _adamw_multi("adamw_s5_bc", [(g.reshape(1, N_GROUPS * SSM_GROUP, SSM_STATE),)
                                        + tuple(rows_gc(t[n], n) for t in (w, m, v))
                                        for n, g in zip(s5_bc, outs[-4:])])
    for i, n in enumerate(s5_bc):
        back = outs[4 * i:4 * i + 4]
        if n[0] == 'b':
            back = [o.reshape(N_GROUPS, SSM_GROUP, SSM_STATE).transpose(0, 2, 1) for o in back]
        keep(n, back)

    g_meta = lax.dynamic_slice(g_meta, (0, dev * (D // N_DEV)), (N_META, D // N_DEV))
    g_conv = lax.dynamic_slice(g_cw4.transpose(1, 0, 2).reshape(3, D_FF), (0, dev * (D_FF // N_DEV)),
                               (3, D_FF // N_DEV))
    rows8 = lambda a: jnp.pad(a.reshape(3, D_FF // N_DEV), ((0, 5), (0, 0)))
    outs = _adamw_multi("adamw_cols", [(g_meta[None], meta_tokens, m['meta_tokens'], v['meta_tokens']),
                                       (rows8(g_conv)[None], rows8(conv_w), rows8(m['conv_w']), rows8(v['conv_w']))])
    keep('meta_tokens', outs[:4])
    keep('conv_w', [o[:3] for o in outs[4:]])

    return (loss[0, 0], grad_x[None], *[grads[n] for n in WEIGHTS], *[deltas[n] for n in WEIGHTS],
            *[new_m[n] for n in WEIGHTS], *[new_v[n] for n in WEIGHTS])
```

```python
import functools
import math

import jax
import jax.numpy as jnp
from jax import lax
from jax.experimental import pallas as pl
from jax.experimental.pallas import tpu as pltpu

F32, BF16 = jnp.float32, jnp.bfloat16

N_DEV = 8
N_META, SEQ, D = 16, 2048, 2048
T = N_META + SEQ
TR = 688
EPS = 1e-6
D_SSM, Q_LORA, KV_LORA, QK_ROPE = 1024, 512, 256, 64
D_IN = D_SSM + Q_LORA + KV_LORA + QK_ROPE
D_IN_PAD = 1920
N_HEADS, QK_NOPE, V_HEAD = 8, 128, 128
HEAD_PAD = 256
SM_SCALE = 1.0 / math.sqrt(QK_NOPE + QK_ROPE)
CHUNK = 64
N_GROUPS, SSM_GROUP, SSM_STATE = 64, 16, 64
N_SEG = 8
SEG = T // N_SEG
GQ = 8
D_FF = 5504
FF_PIECE = 1376
ROPE_BASE = 10000.0
LR, B1, B2, ADAM_EPS, WD, STEP = 0.001, 0.9, 0.999, 1e-08, 0.01, 10
VMEM_CAP = 60 * 1024 * 1024
MESH = pl.DeviceIdType.MESH

WEIGHTS = ['meta_tokens', 'mix_norm', 'w_in', 'lam_re', 'lam_im', 'log_dt', 'b_re', 'b_im', 'c_re', 'c_im',
           'd_skip', 'w_glu', 'b_glu', 'q_a_norm', 'w_q_b', 'kv_a_norm', 'w_kv_b', 'out_norm_ssm',
           'out_norm_attn', 'w_out', 'ffn_norm', 'w_up', 'conv_w', 'conv_b', 'w_down', 'final_norm']
BIG = ['w_in', 'w_glu', 'w_q_b', 'w_kv_b', 'w_out', 'w_up', 'w_down']
SMALL = [n for n in WEIGHTS if n not in BIG and n not in ('meta_tokens', 'conv_w')]


def _nbytes(shape, dtype):
    return math.prod(shape) * jnp.dtype(dtype).itemsize


def _in_hbm(a):
    return pltpu.with_memory_space_constraint(a, pltpu.HBM) if a.size * a.dtype.itemsize >= (1 << 20) else a


def _params(sem, need):
    return pltpu.CompilerParams(dimension_semantics=("arbitrary",) * sem,
                                vmem_limit_bytes=int(min(VMEM_CAP, max(need, 16 * 1024 * 1024))))


_DIMS = {'nn': (((1,), (0,)), ((), ())), 'nt': (((1,), (1,)), ((), ())), 'tn': (((0,), (0,)), ((), ()))}


def _mm(name, a, b, mode, tm, tn, tk, out_dtype, stack=None, res=None):
    sa, sb, so = stack in ('a_out', 'ab_red'), stack in ('b_out', 'ab_red'), stack in ('a_out', 'b_out')
    nj = a.shape[0] if sa else (b.shape[0] if sb else 1)
    a2, b2 = a.shape[-2:], b.shape[-2:]
    if mode == 'nn':
        (m, k), (k2, n) = a2, b2
    elif mode == 'nt':
        (m, k), (n, k2) = a2, b2
    else:
        (k, m), (k2, n) = a2, b2
    assert k == k2 and m % tm == 0 and n % tn == 0 and k % tk == 0, (name, a.shape, b.shape)
    n_jo, n_jr, n_k = (nj if so else 1), (nj if stack == 'ab_red' else 1), k // tk
    grid = (n_jo, m // tm, n // tn, n_jr, n_k)
    a_blk, a_idx = ((tk, tm), lambda i, j, kk: (kk, i)) if mode == 'tn' else ((tm, tk), lambda i, j, kk: (i, kk))
    b_blk, b_idx = ((tn, tk), lambda i, j, kk: (j, kk)) if mode == 'nt' else ((tk, tn), lambda i, j, kk: (kk, j))

    def spec(blk, idx, stacked, on_out):
        if not stacked:
            return pl.BlockSpec(blk, lambda jo, i, j, jr, kk: idx(i, j, kk))
        return pl.BlockSpec((None,) + blk, lambda jo, i, j, jr, kk: ((jo if on_out else jr),) + idx(i, j, kk))

    o_idx = lambda i, j, kk: (i, j)
    in_specs = [spec(a_blk, a_idx, sa, stack == 'a_out'), spec(b_blk, b_idx, sb, stack == 'b_out')]
    operands = [a, b]
    if res is not None:
        in_specs.append(spec((tm, tn), o_idx, False, False))
        operands.append(res)
    n_red = n_jr * n_k
    dims = _DIMS[mode]

    def body(*refs):
        a_ref, b_ref = refs[0], refs[1]
        res_ref = refs[2] if res is not None else None
        o_ref = refs[3] if res is not None else refs[2]
        part = lax.dot_general(a_ref[...].astype(BF16), b_ref[...].astype(BF16), dims,
                               preferred_element_type=F32)

        def finish(total):
            if res_ref is not None:
                total = total + res_ref[...]
            o_ref[...] = total.astype(o_ref.dtype)

        if n_red == 1:
            finish(part)
        else:
            acc_ref = refs[-1]
            step = pl.program_id(3) * n_k + pl.program_id(4)

            @pl.when(step == 0)
            def _():
                acc_ref[...] = part

            @pl.when(step > 0)
            def _():
                acc_ref[...] += part

            @pl.when(step == n_red - 1)
            def _():
                finish(acc_ref[...])

    out_shape = ((nj,) if so else ()) + (m, n)
    need = 2 * (_nbytes(a_blk, a.dtype) + _nbytes(b_blk, b.dtype) + _nbytes((tm, tn), out_dtype)
                + (_nbytes((tm, tn), F32) if res is not None else 0)) + 3 * _nbytes((tm, tn), F32) + (4 << 20)
    return pl.pallas_call(
        body, name=name, grid=grid, in_specs=in_specs, out_specs=spec((tm, tn), o_idx, so, True),
        out_shape=jax.ShapeDtypeStruct(out_shape, out_dtype),
        scratch_shapes=[pltpu.VMEM((tm, tn), F32)] if n_red > 1 else [],
        compiler_params=_params(5, need),
    )(*map(_in_hbm, operands))


def _rowwise(name, fn, row_ins, const_ins, row_outs, acc_outs, tr=TR):
    n_ri, n_ci, n_ro, n_ao = len(row_ins), len(const_ins), len(row_outs), len(acc_outs)

    def body(*refs):
        ri, ci = refs[:n_ri], refs[n_ri:n_ri + n_ci]
        ro, ao = refs[n_ri + n_ci:n_ri + n_ci + n_ro], refs[n_ri + n_ci + n_ro:]
        i = pl.program_id(0)
        outs = fn(i * tr, *[r[...] for r in ri], *[r[...] for r in ci])
        for r, v in zip(ro, outs[:n_ro]):
            r[...] = v.astype(r.dtype)
        if n_ao:
            @pl.when(i == 0)
            def _():
                for r, v in zip(ao, outs[n_ro:]):
                    r[...] = v

            @pl.when(i > 0)
            def _():
                for r, v in zip(ao, outs[n_ro:]):
                    r[...] += v

    in_specs = [pl.BlockSpec((tr, w), functools.partial(lambda cb, i: (i, cb), cb)) for _, w, cb in row_ins]
    in_specs += [pl.BlockSpec(c.shape, functools.partial(lambda nd, i: (0,) * nd, c.ndim)) for c in const_ins]
    out_specs = [pl.BlockSpec((tr, w), lambda i: (i, 0)) for w, _ in row_outs]
    out_specs += [pl.BlockSpec((1, w), lambda i: (0, 0)) for w in acc_outs]
    out_shape = [jax.ShapeDtypeStruct((T, w), dt) for w, dt in row_outs]
    out_shape += [jax.ShapeDtypeStruct((1, w), F32) for w in acc_outs]
    need = 2 * (sum(_nbytes((tr, w), a.dtype) for a, w, _ in row_ins) + sum(_nbytes((tr, w), dt) for w, dt in row_outs))
    need = 3 * need + (8 << 20)
    return pl.pallas_call(
        body, name=name, grid=(T // tr,), in_specs=in_specs, out_specs=out_specs, out_shape=out_shape,
        compiler_params=_params(1, need),
    )(*[_in_hbm(a) for a, _, _ in row_ins], *const_ins)


def _rms_fwd_fn(row0, h, g):
    h = h.astype(F32)
    r = lax.rsqrt(jnp.mean(h * h, axis=-1, keepdims=True) + EPS)
    return (h * r) * g, r


def _rms_bwd(dxn, h, r, g):
    xhat = h * r
    dxg = dxn * g
    dh = r * (dxg - xhat * jnp.mean(dxg * xhat, axis=-1, keepdims=True))
    return dh, jnp.sum(dxn * xhat, axis=0, keepdims=True)


def _rms_bwd_fn(row0, dxn, h, r, g):
    return _rms_bwd(dxn.astype(F32), h.astype(F32), r, g)


def _rms_bwd_res_fn(row0, dxn, h, r, res, g):
    dh, dg = _rms_bwd(dxn.astype(F32), h.astype(F32), r, g)
    return dh + res, dg


def _gelu_fn(row0, y):
    return (jax.nn.gelu(y),)


def _glu_fwd_fn(row0, y, gl, b, gain):
    ya = jax.nn.gelu(y) * jax.nn.sigmoid(gl + b)
    return _rms_fwd_fn(row0, ya, gain)


def _glu_bwd_fn(row0, dyan, y, gl, ra, b, gain):
    g = jax.nn.gelu(y)
    s = jax.nn.sigmoid(gl + b)
    dya, dgain = _rms_bwd(dyan, g * s, ra, gain)
    dgl = dya * g * s * (1.0 - s)
    return dgl, dya * s, dgain, jnp.sum(dgl, axis=0, keepdims=True)


def _gelu_bwd_fn(row0, dg, y):
    return (jax.vjp(jax.nn.gelu, y)[1](dg)[0],)


def _loss_fn(row0, h, tgt, gain):
    r = lax.rsqrt(jnp.mean(h * h, axis=-1, keepdims=True) + EPS)
    xhat = h * r
    rows = row0 + lax.broadcasted_iota(jnp.int32, h.shape, 0)
    err = jnp.where(rows >= N_META, xhat * gain - tgt, 0.0)
    loss = jnp.full((1, 128), jnp.sum(err * err) * (0.5 / D), F32)
    dh, dgain = _rms_bwd(err * (1.0 / D), h, r, gain)
    return dh, dgain, loss


def _rope_tables():
    pos = jnp.arange(T, dtype=jnp.int32)
    inv_freq = 1.0 / (ROPE_BASE ** (jnp.arange(0, QK_ROPE, 2, dtype=F32) / QK_ROPE))
    ang = pos.astype(F32)[:, None] * inv_freq[None, :]
    cos, sin, z32, z64 = jnp.cos(ang), jnp.sin(ang), jnp.zeros((T, 32), F32), jnp.zeros((T, 64), F32)
    return (jnp.concatenate([cos, cos, z64], 1), jnp.concatenate([-sin, z32, z64], 1),
            jnp.concatenate([z32, sin, z64], 1))


def _rot(x, cc, s1, s2):
    return x * cc + pltpu.roll(x, 96, 1) * s1 + pltpu.roll(x, 32, 1) * s2


def _derot(d, cc, s1, s2):
    return d * cc + pltpu.roll(d * s1, 32, 1) + pltpu.roll(d * s2, 96, 1)


def _chunk_of(pos):
    return jnp.where(pos < N_META, 0, 1 + ((pos - N_META) >> 6))


def _key_len(t):
    last_chunk = 1 + ((t + 1) * TR - 1 - N_META) // CHUNK
    return min(T, pl.cdiv(N_META + last_chunk * CHUNK, 128) * 128)


def _attn_probs(t, q_ref, kv_ref, kpe_ref, tq, tk):
    kl = _key_len(t)
    q = q_ref[...]
    qn = q[:, :QK_NOPE].astype(BF16)
    qp = _rot(q[:, QK_NOPE:], tq[0][...], tq[1][...], tq[2][...]).astype(BF16)
    kn = kv_ref[:kl, :QK_NOPE].astype(BF16)
    v = kv_ref[:kl, QK_NOPE:].astype(BF16)
    kp = _rot(kpe_ref[:kl, :], tk[0][:kl, :], tk[1][:kl, :], tk[2][:kl, :]).astype(BF16)
    s = lax.dot_general(qn, kn, _DIMS['nt'], preferred_element_type=F32)
    s = s + lax.dot_general(qp, kp, _DIMS['nt'], preferred_element_type=F32)
    qc = _chunk_of(t * TR + lax.broadcasted_iota(jnp.int32, (TR, 1), 0))
    kc = _chunk_of(lax.broadcasted_iota(jnp.int32, (1, kl), 1))
    s = jnp.where(kc <= qc, s * SM_SCALE, jnp.finfo(F32).min)
    p = jnp.exp(s - jnp.max(s, axis=-1, keepdims=True))
    p = p * (1.0 / jnp.sum(p, axis=-1, keepdims=True))
    return qn, qp, kn, kp, v, p


def _per_q_tile(fn):
    for t in range(T // TR):
        pl.when(pl.program_id(1) == t)(functools.partial(fn, t))


def _attn_specs(z, tabs):
    q_spec = pl.BlockSpec((None, TR, HEAD_PAD), lambda h, i: (h, i, 0))
    kv_spec = pl.BlockSpec((None, T, HEAD_PAD), lambda h, i: (h, 0, 0))
    kpe_spec = pl.BlockSpec((T, 128), lambda h, i: (0, (D_IN - QK_ROPE) // 128))
    tq = [pl.BlockSpec((TR, 128), lambda h, i: (i, 0))] * 3
    tk = [pl.BlockSpec((T, 128), lambda h, i: (0, 0))] * 3
    return [q_spec, kv_spec, kpe_spec] + tq + tk


def _attn_fwd(q, kv, z, tabs):
    def body(q_ref, kv_ref, kpe_ref, c1, c2, c3, k1, k2, k3, o_ref):
        def tile(t):
            _, _, _, _, v, p = _attn_probs(t, q_ref, kv_ref, kpe_ref, (c1, c2, c3), (k1, k2, k3))
            o_ref[...] = jnp.dot(p.astype(BF16), v, preferred_element_type=F32)

        _per_q_tile(tile)

    return pl.pallas_call(
        body, name="attn_fwd", grid=(N_HEADS, T // TR), in_specs=_attn_specs(z, tabs),
        out_specs=pl.BlockSpec((TR, V_HEAD), lambda h, i: (i, h)),
        out_shape=jax.ShapeDtypeStruct((T, N_HEADS * V_HEAD), F32),
        compiler_params=_params(2, 48 << 20),
    )(_in_hbm(q), _in_hbm(kv), _in_hbm(z), *tabs, *tabs)


def _attn_bwd(q, kv, z, tabs, dyb):
    def body(q_ref, kv_ref, kpe_ref, c1, c2, c3, k1, k2, k3, do_ref, dq_ref, dkv_ref, dkpe_ref):
        @pl.when(pl.program_id(1) == 0)
        def _():
            dkv_ref[...] = jnp.zeros(dkv_ref.shape, F32)
            dkpe_ref[...] = jnp.zeros(dkpe_ref.shape, F32)

        def tile(t):
            kl = _key_len(t)
            qn, qp, kn, kp, v, p = _attn_probs(t, q_ref, kv_ref, kpe_ref, (c1, c2, c3), (k1, k2, k3))
            do = do_ref[...].astype(BF16)
            dp = lax.dot_general(do, v, _DIMS['nt'], preferred_element_type=F32)
            ds = (p * (dp - jnp.sum(p * dp, axis=-1, keepdims=True)) * SM_SCALE).astype(BF16)
            pb = p.astype(BF16)
            dq_ref[:, :QK_NOPE] = jnp.dot(ds, kn, preferred_element_type=F32).astype(dq_ref.dtype)
            dqp = _derot(jnp.dot(ds, kp, preferred_element_type=F32), c1[...], c2[...], c3[...])
            dq_ref[:, QK_NOPE:] = dqp.astype(dq_ref.dtype)
            dkv_ref[:kl, :QK_NOPE] += lax.dot_general(ds, qn, _DIMS['tn'], preferred_element_type=F32)
            dkv_ref[:kl, QK_NOPE:] += lax.dot_general(pb, do, _DIMS['tn'], preferred_element_type=F32)
            dkpe_ref[:kl, :] += lax.dot_general(ds, qp, _DIMS['tn'], preferred_element_type=F32)

        _per_q_tile(tile)

    return pl.pallas_call(
        body, name="attn_bwd", grid=(N_HEADS, T // TR),
        in_specs=_attn_specs(z, tabs) + [pl.BlockSpec((TR, V_HEAD), lambda h, i: (i, h))],
        out_specs=[pl.BlockSpec((None, TR, HEAD_PAD), lambda h, i: (h, i, 0)),
                   pl.BlockSpec((None, T, HEAD_PAD), lambda h, i: (h, 0, 0)),
                   pl.BlockSpec((None, T, 128), lambda h, i: (h, 0, 0))],
        out_shape=[jax.ShapeDtypeStruct((N_HEADS, T, HEAD_PAD), BF16),
                   jax.ShapeDtypeStruct((N_HEADS, T, HEAD_PAD), F32),
                   jax.ShapeDtypeStruct((N_HEADS, T, 128), F32)],
        compiler_params=_params(2, 56 << 20),
    )(_in_hbm(q), _in_hbm(kv), _in_hbm(z), *tabs, *tabs, _in_hbm(dyb))


def _kpe_bwd(dkpe, tabs):
    def body(d_ref, c1, c2, c3, o_ref):
        d = d_ref[0]
        for h in range(1, N_HEADS):
            d = d + d_ref[h]
        o_ref[...] = _derot(d, c1[...], c2[...], c3[...]).astype(o_ref.dtype)

    tile = pl.BlockSpec((TR, 128), lambda i: (i, 0))
    return pl.pallas_call(
        body, name="kpe_bwd", grid=(T // TR,),
        in_specs=[pl.BlockSpec((N_HEADS, TR, 128), lambda i: (0, i, 0)), tile, tile, tile], out_specs=tile,
        out_shape=jax.ShapeDtypeStruct((T, 128), BF16), compiler_params=_params(1, 16 << 20),
    )(_in_hbm(dkpe), *tabs)


def _s5_prep_math(lam_re, lam_im, log_dt, bt_re, bt_im):
    dt = jnp.exp(log_dt)
    mag = jnp.exp(lam_re * dt)
    a_re, a_im = mag * jnp.cos(lam_im * dt), mag * jnp.sin(lam_im * dt)
    den = lam_re * lam_re + lam_im * lam_im
    w_re = ((a_re - 1.0) * lam_re + a_im * lam_im) / den
    w_im = (a_im * lam_re - (a_re - 1.0) * lam_im) / den
    return a_re, a_im, w_re[None] * bt_re - w_im[None] * bt_im, w_re[None] * bt_im + w_im[None] * bt_re


def _eye_groups():
    return (lax.broadcasted_iota(jnp.int32, (N_GROUPS, N_GROUPS), 0)
            == lax.broadcasted_iota(jnp.int32, (N_GROUPS, N_GROUPS), 1)).astype(F32)


def _row_to_col(row):
    return jnp.sum(_eye_groups() * row, axis=1, keepdims=True)


def _s5_prep(lam_re, lam_im, log_dt, bt_re, bt_im):
    def body(lr, li, ld, br, bi, ar, ai, bbr, bbi):
        ar[...], ai[...], bbr[...], bbi[...] = _s5_prep_math(lr[...], li[...], _row_to_col(ld[...]), br[...], bi[...])

    gp, cgp = jax.ShapeDtypeStruct((N_GROUPS, SSM_STATE), F32), jax.ShapeDtypeStruct(bt_re.shape, F32)
    return pl.pallas_call(body, name="s5_prep", out_shape=[gp, gp, cgp, cgp])(lam_re, lam_im, log_dt, bt_re, bt_im)


def _s5_prep_bwd(lam_re, lam_im, log_dt, bt_re, bt_im, da_re, da_im, dbb_re, dbb_im):
    def body(lr, li, ld, br, bi, dar, dai, dbr, dbi, o1, o2, o3, o4, o5):
        _, vjp = jax.vjp(_s5_prep_math, lr[...], li[...], _row_to_col(ld[...]), br[...], bi[...])
        o1[...], o2[...], dld, o4[...], o5[...] = vjp((dar[...], dai[...], dbr[...], dbi[...]))
        o3[...] = jnp.sum(_eye_groups() * dld, axis=0, keepdims=True)

    ins = (lam_re, lam_im, log_dt, bt_re, bt_im)
    return pl.pallas_call(body, name="s5_prep_bwd", out_shape=[jax.ShapeDtypeStruct(a.shape, F32) for a in ins])(
        *ins, da_re, da_im, dbb_re, dbb_im)


def _cmul(ar, ai, br, bi):
    return ar * br - ai * bi, ar * bi + ai * br


def _seg_rows(i):
    return pl.ds(pl.multiple_of(i * N_SEG, N_SEG), N_SEG)


def _scan(xr, xi, ar, ai, reverse):
    zero = jnp.zeros(ar.shape, F32)

    def local(j, carry):
        rows = _seg_rows(SEG - 1 - j if reverse else j)
        nr, ni = _cmul(ar, ai, *carry)
        nr, ni = nr + xr[rows, :], ni + xi[rows, :]
        xr[rows, :], xi[rows, :] = nr, ni
        return nr, ni

    er, ei = lax.fori_loop(0, SEG, local, (zero, zero))
    pr, pi = ar, ai
    for _ in range(8):
        pr, pi = _cmul(pr, pi, pr, pi)
    pr, pi = _cmul(*_cmul(pr, pi, ar, ai), ar, ai)
    row = lax.broadcasted_iota(jnp.int32, ar.shape, 0)
    edge, shift = (N_SEG - 1, N_SEG - 1) if reverse else (0, 1)
    hr, hi = zero, zero
    for _ in range(N_SEG - 1):
        tr_, ti_ = _cmul(pr, pi, hr, hi)
        hr = jnp.where(row == edge, 0.0, pltpu.roll(tr_ + er, shift, 0))
        hi = jnp.where(row == edge, 0.0, pltpu.roll(ti_ + ei, shift, 0))

    def fix(j, carry):
        rows = _seg_rows(SEG - 1 - j if reverse else j)
        xr[rows, :] += carry[0]
        xi[rows, :] += carry[1]
        return _cmul(ar, ai, *carry)

    lax.fori_loop(0, SEG, fix, _cmul(ar, ai, hr, hi))
    return hr, hi


def _split(x):
    hi = x.astype(BF16)
    return hi, (x - hi.astype(F32)).astype(BF16)


def _dot3(xs, ys, mode='nn'):
    d = lambda p, q: lax.dot_general(p, q, _DIMS[mode], preferred_element_type=F32)
    return d(xs[0], ys[0]) + (d(xs[1], ys[0]) + d(xs[0], ys[1]))


def _group_masks():
    ns = GQ * SSM_STATE
    rep = (lax.broadcasted_iota(jnp.int32, (SSM_STATE, ns), 1) % SSM_STATE
           == lax.broadcasted_iota(jnp.int32, (SSM_STATE, ns), 0)).astype(BF16)
    own = (lax.broadcasted_iota(jnp.int32, (128, ns), 0) // SSM_GROUP
           == lax.broadcasted_iota(jnp.int32, (128, ns), 1) // SSM_STATE)
    return rep, own


def _block_diag(x, rep, own):
    dot = lambda p: jnp.dot(p, rep, preferred_element_type=F32)
    return tuple(jnp.where(own, dot(part), 0.0).astype(BF16) for part in _split(x))


def _block_rows(full, rep, own):
    masked = _split(jnp.where(own, full, 0.0))
    dot = lambda p: lax.dot_general(p, rep, _DIMS['nt'], preferred_element_type=F32)
    return dot(masked[0]) + dot(masked[1])


def _s5_specs():
    col = pl.BlockSpec((T, 128), lambda q: (0, q))
    a_spec = pl.BlockSpec((None, 2, N_SEG, GQ * SSM_STATE), lambda q: (q, 0, 0, 0))
    rows = pl.BlockSpec((128, SSM_STATE), lambda q: (q, 0))
    d_spec = pl.BlockSpec((1, 128), lambda q: (0, q))
    return col, a_spec, rows, d_spec


def _s5_fwd(u, a8, bb_re, bb_im, c_re, c_im, dsk):
    def body(u_ref, a_ref, bre_ref, bim_ref, cre_ref, cim_ref, d_ref, y_ref, hre, him):
        u_ = u_ref[...]
        us = _split(u_)
        rep, own = _group_masks()
        hre[...] = _dot3(us, _block_diag(bre_ref[...], rep, own))
        him[...] = _dot3(us, _block_diag(bim_ref[...], rep, own))
        _scan(hre, him, a_ref[0], a_ref[1], reverse=False)
        y_ref[...] = (_dot3(_split(hre[...]), _block_diag(cre_ref[...], rep, own), 'nt')
                      - _dot3(_split(him[...]), _block_diag(cim_ref[...], rep, own), 'nt') + d_ref[...] * u_)

    col, a_spec, rows, d_spec = _s5_specs()
    return pl.pallas_call(
        body, name="s5_fwd", grid=(N_GROUPS // GQ,), in_specs=[col, a_spec, rows, rows, rows, rows, d_spec],
        out_specs=col, out_shape=jax.ShapeDtypeStruct((T, D_SSM), F32),
        scratch_shapes=[pltpu.VMEM((T, GQ * SSM_STATE), F32)] * 2,
        compiler_params=_params(1, 40 << 20),
    )(_in_hbm(u), a8, bb_re, bb_im, c_re, c_im, dsk)


def _s5_bwd(u, dy, a8, bb_re, bb_im, c_re, c_im, dsk):
    def body(u_ref, dy_ref, a_ref, bre_ref, bim_ref, cre_ref, cim_ref, d_ref,
             du_ref, dbre_ref, dbim_ref, dcre_ref, dcim_ref, da_ref, dd_ref, hre, him, lre, lim):
        u_, dy_ = u_ref[...], dy_ref[...]
        us, dys = _split(u_), _split(dy_)
        ar, ai = a_ref[0], a_ref[1]
        rep, own = _group_masks()
        bres, bims = _block_diag(bre_ref[...], rep, own), _block_diag(bim_ref[...], rep, own)
        hre[...] = _dot3(us, bres)
        him[...] = _dot3(us, bims)
        h0r, h0i = _scan(hre, him, ar, ai, reverse=False)
        lre[...] = _dot3(dys, _block_diag(cre_ref[...], rep, own))
        lim[...] = -_dot3(dys, _block_diag(cim_ref[...], rep, own))
        _scan(lre, lim, ar, -ai, reverse=True)

        def acc_da(i, carry):
            lr, li = lre[_seg_rows(i), :], lim[_seg_rows(i), :]
            pr, pi = hre[_seg_rows(i - 1), :], him[_seg_rows(i - 1), :]
            return carry[0] + lr * pr + li * pi, carry[1] + li * pr - lr * pi

        lr, li = lre[_seg_rows(0), :], lim[_seg_rows(0), :]
        dar, dai = lax.fori_loop(1, SEG, acc_da, (lr * h0r + li * h0i, li * h0r - lr * h0i))
        da_ref[0:1, :] = jnp.sum(dar, axis=0, keepdims=True)
        da_ref[1:2, :] = jnp.sum(dai, axis=0, keepdims=True)
        dot = lambda p, q, mode: lax.dot_general(p, q, _DIMS[mode], preferred_element_type=F32)
        lr1, li1 = lre[...].astype(BF16), lim[...].astype(BF16)
        du_ref[...] = dy_ * d_ref[...] + dot(lr1, bres[0], 'nt') + dot(li1, bims[0], 'nt')
        dbre_ref[...] = _block_rows(dot(us[0], lr1, 'tn'), rep, own)
        dbim_ref[...] = _block_rows(dot(us[0], li1, 'tn'), rep, own)
        dcre_ref[...] = _block_rows(dot(dys[0], hre[...].astype(BF16), 'tn'), rep, own)
        dcim_ref[...] = -_block_rows(dot(dys[0], him[...].astype(BF16), 'tn'), rep, own)
        dd_ref[...] = jnp.sum(dy_ * u_, axis=0, keepdims=True)

    col, a_spec, rows, d_spec = _s5_specs()
    nq, ns = N_GROUPS // GQ, GQ * SSM_STATE
    per_row = jax.ShapeDtypeStruct((N_GROUPS * SSM_GROUP, SSM_STATE), F32)
    return pl.pallas_call(
        body, name="s5_bwd", grid=(nq,),
        in_specs=[col, col, a_spec, rows, rows, rows, rows, d_spec],
        out_specs=[col, rows, rows, rows, rows, pl.BlockSpec((None, 2, ns), lambda q: (q, 0, 0)), d_spec],
        out_shape=[jax.ShapeDtypeStruct((T, D_SSM), F32), per_row, per_row, per_row, per_row,
                   jax.ShapeDtypeStruct((nq, 2, ns), F32), jax.ShapeDtypeStruct((1, D_SSM), F32)],
        scratch_shapes=[pltpu.VMEM((T, ns), F32)] * 4,
        compiler_params=_params(1, 56 << 20),
    )(_in_hbm(u), _in_hbm(dy), a8, bb_re, bb_im, c_re, c_im, dsk)


def _to_segments(a):
    return a.reshape(N_SEG, SEG, a.shape[-1]).transpose(1, 0, 2).reshape(T, a.shape[-1])


def _from_segments(a):
    return a.reshape(SEG, N_SEG, a.shape[-1]).transpose(1, 0, 2).reshape(T, a.shape[-1])


def _ff_specs():
    up_spec = pl.BlockSpec((2, None, T, 128), lambda p, cb: (0, p, 0, cb))
    piece = pl.BlockSpec((None, T, 128), lambda p, cb: (p, 0, cb))
    cw_spec = pl.BlockSpec((None, 3, 128), lambda p, cb: (p, 0, cb))
    cb_spec = pl.BlockSpec((None, 1, 128), lambda p, cb: (p, 0, cb))
    return up_spec, piece, cw_spec, cb_spec


def _conv_gate(gate, w, cb, pad_ref):
    pad_ref[0:8, :] = jnp.zeros((8, 128), F32)
    pad_ref[8:8 + T, :] = gate
    g1, g2 = pad_ref[pl.ds(7, T), :], pad_ref[pl.ds(6, T), :]
    return w[0:1, :] * g2 + w[1:2, :] * g1 + w[2:3, :] * gate + cb, g1, g2


def _ff_act(up, cw4, cb4):
    def body(up_ref, w_ref, b_ref, o_ref, pad_ref):
        gc, _, _ = _conv_gate(up_ref[0].astype(F32), w_ref[...], b_ref[...], pad_ref)
        o_ref[...] = (jax.nn.silu(gc) * up_ref[1].astype(F32)).astype(o_ref.dtype)

    up_spec, piece, cw_spec, cb_spec = _ff_specs()
    n_cb = pl.cdiv(FF_PIECE, 128)
    return pl.pallas_call(
        body, name="ff_act", grid=(4, n_cb), in_specs=[up_spec, cw_spec, cb_spec], out_specs=piece,
        out_shape=jax.ShapeDtypeStruct((4, T, FF_PIECE), BF16),
        scratch_shapes=[pltpu.VMEM((T + 8, 128), F32)],
        compiler_params=_params(2, 24 << 20),
    )(_in_hbm(up.reshape(2, 4, T, FF_PIECE)), cw4, cb4)


def _ff_act_bwd(up, dact, cw4, cb4):
    def body(up_ref, da_ref, w_ref, b_ref, dup_ref, dw_ref, db_ref, pad_ref, pad2_ref):
        gate, val, w = up_ref[0].astype(F32), up_ref[1].astype(F32), w_ref[...]
        gc, g1, g2 = _conv_gate(gate, w, b_ref[...], pad_ref)
        sg = jax.nn.sigmoid(gc)
        da = da_ref[...].astype(F32)
        dup_ref[1] = (da * gc * sg).astype(dup_ref.dtype)
        dgc = da * val * sg * (1.0 + gc * (1.0 - sg))
        db_ref[...] = jnp.sum(dgc, axis=0, keepdims=True)
        dw_ref[0:1, :] = jnp.sum(dgc * g2, axis=0, keepdims=True)
        dw_ref[1:2, :] = jnp.sum(dgc * g1, axis=0, keepdims=True)
        dw_ref[2:3, :] = jnp.sum(dgc * gate, axis=0, keepdims=True)
        pad2_ref[0:T, :] = dgc
        pad2_ref[T:T + 8, :] = jnp.zeros((8, 128), F32)
        d1, d2 = pad2_ref[pl.ds(1, T), :], pad2_ref[pl.ds(2, T), :]
        dup_ref[0] = (w[2:3, :] * dgc + w[1:2, :] * d1 + w[0:1, :] * d2).astype(dup_ref.dtype)

    up_spec, piece, cw_spec, cb_spec = _ff_specs()
    n_cb = pl.cdiv(FF_PIECE, 128)
    dup, dw, db = pl.pallas_call(
        body, name="ff_act_bwd", grid=(4, n_cb), in_specs=[up_spec, piece, cw_spec, cb_spec],
        out_specs=[up_spec, cw_spec, cb_spec],
        out_shape=[jax.ShapeDtypeStruct((2, 4, T, FF_PIECE), BF16), jax.ShapeDtypeStruct((4, 3, FF_PIECE), F32),
                   jax.ShapeDtypeStruct((4, 1, FF_PIECE), F32)],
        scratch_shapes=[pltpu.VMEM((T + 8, 128), F32)] * 2,
        compiler_params=_params(2, 32 << 20),
    )(_in_hbm(up.reshape(2, 4, T, FF_PIECE)), _in_hbm(dact), cw4, cb4)
    return dup.reshape(8, T, FF_PIECE), dw, db


def _local_step(x, tgt, meta, p, wf, late_weights, send_grads, send_small):
    sm = {}
    h0 = jnp.concatenate([meta, x], axis=0)
    tgt_pad = jnp.concatenate([jnp.zeros((N_META, D), F32), tgt], axis=0)
    tabs = _rope_tables()
    row = lambda a: a.reshape(1, -1)

    xn, r0 = _rowwise("rms_mix", _rms_fwd_fn, [(h0, D, 0)], [row(p['mix_norm'])], [(D, BF16), (1, F32)], [])
    z = _mm("in_proj", xn, wf['w_in'], 'nn', TR, 640, D, F32)
    lam_re, lam_im = p['lam_re'].reshape(N_GROUPS, SSM_STATE), p['lam_im'].reshape(N_GROUPS, SSM_STATE)
    log_dt = p['log_dt'].reshape(1, N_GROUPS)
    bt_re = p['b_re'].reshape(N_GROUPS, SSM_STATE, SSM_GROUP).transpose(2, 0, 1)
    bt_im = p['b_im'].reshape(N_GROUPS, SSM_STATE, SSM_GROUP).transpose(2, 0, 1)
    c_re = p['c_re'].reshape(N_GROUPS * SSM_GROUP, SSM_STATE)
    c_im = p['c_im'].reshape(N_GROUPS * SSM_GROUP, SSM_STATE)
    a_re, a_im, bbt_re, bbt_im = _s5_prep(lam_re, lam_im, log_dt, bt_re, bt_im)
    nq, ns = N_GROUPS // GQ, GQ * SSM_STATE
    a8 = jnp.broadcast_to(jnp.stack([a_re.reshape(nq, ns), a_im.reshape(nq, ns)], 1)[:, :, None, :],
                          (nq, 2, N_SEG, ns))
    gcp = lambda t: t.transpose(1, 0, 2).reshape(N_GROUPS * SSM_GROUP, SSM_STATE)
    bb_re, bb_im = gcp(bbt_re), gcp(bbt_im)
    dsk = row(p['d_skip'])
    u_seg = _to_segments(z[:, :D_SSM])
    y_ssm = _from_segments(_s5_fwd(u_seg, a8, bb_re, bb_im, c_re, c_im, dsk))
    g, = _rowwise("gelu", _gelu_fn, [(y_ssm, D_SSM, 0)], [], [(D_SSM, BF16)], [])
    gl = _mm("glu_proj", g, wf['w_glu'], 'nn', TR, D_SSM, D_SSM, F32)
    yan, ra = _rowwise("glu_norm", _glu_fwd_fn, [(y_ssm, D_SSM, 0), (gl, D_SSM, 0)],
                       [row(p['b_glu']), row(p['out_norm_ssm'])], [(D_SSM, BF16), (1, F32)], [])
    qn, rq = _rowwise("rms_q", _rms_fwd_fn, [(z, Q_LORA, D_SSM // Q_LORA)], [row(p['q_a_norm'])],
                      [(Q_LORA, BF16), (1, F32)], [])
    kvn, rkv = _rowwise("rms_kv", _rms_fwd_fn, [(z, KV_LORA, (D_SSM + Q_LORA) // KV_LORA)], [row(p['kv_a_norm'])],
                        [(KV_LORA, BF16), (1, F32)], [])
    q = _mm("q_proj", qn, wf['w_q_b'], 'nn', T, HEAD_PAD, Q_LORA, F32, stack='b_out')
    kv = _mm("kv_proj", kvn, wf['w_kv_b'], 'nn', T, HEAD_PAD, KV_LORA, F32, stack='b_out')
    yb = _attn_fwd(q, kv, z, tabs)
    ybn, rb = _rowwise("rms_attn", _rms_fwd_fn, [(yb, D_SSM, 0)], [row(p['out_norm_attn'])],
                       [(D_SSM, BF16), (1, F32)], [])
    y = jnp.concatenate([yan, ybn], axis=1)
    wf = dict(wf)
    wf['w_out'], token, ffn_weights = late_weights(y)
    h1 = _mm("out_proj", y, wf['w_out'], 'nn', TR, 1024, D, F32, res=h0)
    xn2, r1 = _rowwise("rms_ffn", _rms_fwd_fn, [(h1, D, 0)], [row(p['ffn_norm']) + token[0:1, 0:1]],
                       [(D, BF16), (1, F32)], [])
    wf['w_up'], wf['w_down'] = ffn_weights(xn2)
    up = _mm("up_proj", xn2, wf['w_up'], 'nn', TR, FF_PIECE, D, BF16, stack='b_out')
    cw4 = wf['conv_w'].reshape(3, 4, FF_PIECE).transpose(1, 0, 2)
    cb4 = p['conv_b'].reshape(4, 1, FF_PIECE)
    act = _ff_act(up, cw4, cb4)
    wd4 = wf['w_down'].reshape(4, FF_PIECE, D)
    h2 = _mm("down_proj", act, wd4, 'nn', TR, 1024, FF_PIECE, F32, stack='ab_red', res=h1)
    dh2, sm['final_norm'], loss = _rowwise("loss", _loss_fn, [(h2, D, 0), (tgt_pad, D, 0)], [row(p['final_norm'])],
                                           [(D, F32)], [D, 128])

    big = {}
    dact = _mm("down_bwd_x", dh2, wd4, 'nt', TR, FF_PIECE, D, BF16, stack='b_out')
    g_down = _mm("down_bwd_w", act, dh2, 'tn', FF_PIECE, 512, T, BF16, stack='a_out').reshape(
        N_DEV, D_FF // N_DEV, D)
    token = send_grads(['w_down'], [g_down], 'cores')
    dup, dcw4, sm['conv_b'] = _ff_act_bwd(up, dact, cw4, cb4 + token[0:1, 0:1])
    dxn2 = _mm("up_bwd_x", dup, wf['w_up'], 'nt', TR, 1024, FF_PIECE, F32, stack='ab_red')
    g_up = _mm("up_bwd_w", dup, xn2, 'tn', FF_PIECE, 1024, T, BF16, stack='a_out')
    token = send_grads(['w_up'], [g_up], 'cores') + send_grads(['w_down'], dxn2, 'chips')
    dh1, sm['ffn_norm'] = _rowwise("rms_ffn_bwd", _rms_bwd_res_fn,
                                   [(dxn2, D, 0), (h1, D, 0), (r1, 1, 0), (dh2, D, 0)],
                                   [row(p['ffn_norm']) + token[0:1, 0:1]], [(D, F32)], [D])
    dy = _mm("out_bwd_x", dh1, wf['w_out'], 'nt', TR, 1024, D, F32)
    g_out = _mm("out_bwd_w", y, dh1, 'tn', 1024, 1024, T, BF16).reshape(N_DEV, D // N_DEV, D)
    dgl, dgd, sm['out_norm_ssm'], sm['b_glu'] = _rowwise(
        "glu_bwd", _glu_bwd_fn, [(dy, D_SSM, 0), (y_ssm, D_SSM, 0), (gl, D_SSM, 0), (ra, 1, 0)],
        [row(p['b_glu']), row(p['out_norm_ssm'])], [(D_SSM, BF16), (D_SSM, F32)], [D_SSM, D_SSM])
    dg = _mm("glu_bwd_x", dgl, wf['w_glu'], 'nt', TR, D_SSM, D_SSM, F32, res=dgd)
    g_glu = _mm("glu_bwd_w", g, dgl, 'tn', D_SSM, D_SSM, T, BF16).reshape(N_DEV, D_SSM // N_DEV, D_SSM)
    dy_ssm, = _rowwise("gelu_bwd", _gelu_bwd_fn, [(dg, D_SSM, 0), (y_ssm, D_SSM, 0)], [], [(D_SSM, F32)], [])
    token = send_grads(['w_up'], g_glu, 'chips')
    dyb, sm['out_norm_attn'] = _rowwise("rms_attn_bwd", _rms_bwd_fn, [(dy, D_SSM, 1), (yb, D_SSM, 0), (rb, 1, 0)],
                                        [row(p['out_norm_attn']) + token[0:1, 0:1]], [(D_SSM, F32)], [D_SSM])
    dq, dkv, dkpe = _attn_bwd(q, kv, z, tabs, dyb)
    g_q = _mm("q_bwd_w", qn, dq, 'tn', Q_LORA, HEAD_PAD, T, BF16, stack='b_out')
    dqn = _mm("q_bwd_x", dq, wf['w_q_b'], 'nt', TR, Q_LORA, HEAD_PAD, F32, stack='ab_red')
    g_kv = _mm("kv_bwd_w", kvn, dkv, 'tn', KV_LORA, HEAD_PAD, T, BF16, stack='b_out')
    dkvn = _mm("kv_bwd_x", dkv, wf['w_kv_b'], 'nt', TR, KV_LORA, HEAD_PAD, F32, stack='ab_red')
    token = send_grads(['w_out', 'w_glu', 'w_q_b', 'w_kv_b'], [g_out, g_glu, g_q, g_kv], 'both')
    dq_a, sm['q_a_norm'] = _rowwise("rms_q_bwd", _rms_bwd_fn,
                                    [(dqn, Q_LORA, 0), (z, Q_LORA, D_SSM // Q_LORA), (rq, 1, 0)],
                                    [row(p['q_a_norm'])], [(Q_LORA, BF16)], [Q_LORA])
    dkv_a, sm['kv_a_norm'] = _rowwise("rms_kv_bwd", _rms_bwd_fn,
                                      [(dkvn, KV_LORA, 0), (z, KV_LORA, (D_SSM + Q_LORA) // KV_LORA), (rkv, 1, 0)],
                                      [row(p['kv_a_norm'])], [(KV_LORA, BF16)], [KV_LORA])
    dk_pe = _kpe_bwd(dkpe, tabs)
    du_seg, dbre, dbim, dcre, dcim, da, sm['d_skip'] = _s5_bwd(
        u_seg, _to_segments(dy_ssm), a8, bb_re, bb_im, c_re, c_im, dsk + token[0:1, 0:1])
    du = _from_segments(du_seg).astype(BF16)
    cgp = lambda d: d.reshape(N_GROUPS, SSM_GROUP, SSM_STATE).transpose(1, 0, 2)
    dlam_re, dlam_im, dlog_dt, dbt_re, dbt_im = _s5_prep_bwd(
        lam_re, lam_im, log_dt, bt_re, bt_im, da[:, 0, :].reshape(N_GROUPS, SSM_STATE),
        da[:, 1, :].reshape(N_GROUPS, SSM_STATE), cgp(dbre), cgp(dbim))
    sm['lam_re'], sm['lam_im'], sm['log_dt'] = dlam_re, dlam_im, dlog_dt
    sm['b_re'], sm['b_im'] = gcp(dbt_re), gcp(dbt_im)
    sm['c_re'], sm['c_im'] = dcre, dcim
    token = send_small(sm, dcw4, loss)
    dz = jnp.concatenate([du, dq_a, dkv_a, dk_pe], axis=1)
    dxn = _mm("in_bwd_x", dz, wf['w_in'], 'nt', TR, 1024, D_IN_PAD, F32)
    big['w_in'] = _mm("in_bwd_w", xn, dz, 'tn', 1024, 640, T, BF16).reshape(N_DEV, D // N_DEV, D_IN_PAD)
    dh0, d_mix_norm = _rowwise("rms_mix_bwd", _rms_bwd_res_fn,
                               [(dxn, D, 0), (h0, D, 0), (r0, 1, 0), (dh1, D, 0)],
                               [row(p['mix_norm']) + token[0:1, 0:1]], [(D, F32)], [D])
    return dh0[N_META:], dh0[:N_META], d_mix_norm, big


def _place():
    x, y, c = lax.axis_index("x"), lax.axis_index("y"), lax.axis_index("c")
    return x, y, c, [(1 - x, y), (x, 1 - y), (1 - x, 1 - y)]


_HBM = pl.BlockSpec(memory_space=pltpu.HBM)


def _all_gather(name, arrs, chip_blocks=()):
    n, nc = len(arrs), len(chip_blocks)

    def body(*refs):
        ins, cins, outs, couts = refs[:n], refs[n:n + nc], refs[n + nc:2 * n + nc], refs[2 * n + nc:2 * (n + nc)]
        send, recv, loc, csend, crecv = refs[2 * (n + nc):]
        x, y, c, chips = _place()
        me, sib = (x, y, c), (x, y, 1 - c)

        def cp(a, k, blk, to, src=None):
            dst = outs[a].at[4 * blk[0] + 2 * blk[1] + blk[2]]
            return pltpu.make_async_remote_copy(src_ref=dst if src is None else src, dst_ref=dst,
                                                send_sem=send.at[a, k], recv_sem=recv.at[a, k],
                                                device_id=to, device_id_type=MESH)

        between_chips = [pltpu.make_async_remote_copy(src_ref=cins[a].at[2 * ch[0] + ch[1]], dst_ref=couts[a].at[j],
                                                      send_sem=csend.at[a, j], recv_sem=crecv.at[a, j],
                                                      device_id=(*ch, c), device_id_type=MESH)
                         for a in range(nc) for j, ch in enumerate(chips)]
        mine = [pltpu.make_async_copy(ins[a], outs[a].at[4 * x + 2 * y + c], loc.at[a]) for a in range(n)]
        first = []
        for a in range(n):
            first.append(cp(a, 0, me, sib, src=ins[a]))
            first += [cp(a, 1 + j, me, (*ch, c), src=ins[a]) for j, ch in enumerate(chips)]
        for f in mine + first + between_chips:
            f.start()
        passed = []
        for a in range(n):
            for j, ch in enumerate(chips):
                cp(a, 1 + j, (*ch, c), me).wait_recv()
                passed.append(cp(a, 4 + j, (*ch, c), sib))
                passed[-1].start()
        for a in range(n):
            cp(a, 0, sib, me).wait_recv()
            for j, ch in enumerate(chips):
                cp(a, 4 + j, (*ch, 1 - c), me).wait_recv()
        for f in first + passed:
            f.wait_send()
        for m in mine + between_chips:
            m.wait()

    out = pl.pallas_call(
        body, name=name, in_specs=[_HBM] * (n + nc), out_specs=[_HBM] * (n + nc),
        out_shape=[jax.ShapeDtypeStruct((N_DEV,) + a.shape, a.dtype) for a in arrs]
        + [jax.ShapeDtypeStruct((3,) + a.shape[1:], a.dtype) for a in chip_blocks],
        scratch_shapes=[pltpu.SemaphoreType.DMA((n, 7)), pltpu.SemaphoreType.DMA((n, 7)),
                        pltpu.SemaphoreType.DMA((n,)), pltpu.SemaphoreType.DMA((max(nc, 1), 3)),
                        pltpu.SemaphoreType.DMA((max(nc, 1), 3))],
    )(*arrs, *chip_blocks)
    return (out[:n], out[n:]) if nc else out


def _exchange_cores(name, arrs):
    n = len(arrs)

    def body(*refs):
        ins, outs = refs[:n], refs[n:2 * n]
        send, recv = refs[2 * n:]
        x, y, c, _ = _place()
        remote = [pltpu.make_async_remote_copy(src_ref=ins[a].at[2 * k + 1 - c], dst_ref=outs[a].at[k],
                                               send_sem=send.at[a, k], recv_sem=recv.at[a, k],
                                               device_id=(x, y, 1 - c), device_id_type=MESH)
                  for a in range(n) for k in range(4)]
        for d in remote:
            d.start()
        for d in remote:
            d.wait()

    return pl.pallas_call(
        body, name=name, in_specs=[_HBM] * n, out_specs=[_HBM] * n,
        out_shape=[jax.ShapeDtypeStruct((4,) + a.shape[1:], a.dtype) for a in arrs],
        scratch_shapes=[pltpu.SemaphoreType.DMA((n, 4))] * 2,
    )(*arrs)


_SEM = pl.BlockSpec(memory_space=pltpu.SEMAPHORE)
_EFFECT = pltpu.SideEffectType.DATAFLOW_SIDE_EFFECTING


def _split_exchange(name, srcs, land_shapes, copies_of, n_cp, after=None, all_of=None):
    n = len(srcs)
    in_place = land_shapes is None
    lands = [] if in_place else [pltpu.with_memory_space_constraint(lax.empty(s, a.dtype), pltpu.HBM)
                                 for s, a in zip(land_shapes, srcs)]
    nb = n + len(lands)

    per = 1 if all_of else n_cp
    n_sem = n * per

    def descriptors(refs, send, recv, mirror):
        src_refs, land_refs = refs[:n], refs[nb - n:nb]
        x, y, c, chips = _place()
        out = []
        for a in range(n):
            if mirror and all_of:
                whole = all_of(a, land_refs[a])
                out.append(pltpu.make_async_remote_copy(src_ref=whole, dst_ref=whole, send_sem=send[a],
                                                        recv_sem=recv[a], device_id=(x, y, 1 - c),
                                                        device_id_type=MESH))
                continue
            for k, (src, dst, to, back) in enumerate(copies_of(a, src_refs[a], land_refs[a], x, y, c, chips)):
                out.append(pltpu.make_async_remote_copy(src_ref=src, dst_ref=back if mirror else dst,
                                                        send_sem=send[a * per + k % per],
                                                        recv_sem=recv[a * per + k % per],
                                                        device_id=to, device_id_type=MESH))
        return out

    def start_body(*refs):
        first = nb + (after is not None)
        sems = refs[first:first + 2 * n_sem]
        for d in descriptors(refs, sems[:n_sem], sems[n_sem:], False):
            d.start()
        refs[-1][...] = jnp.zeros((8, 128), F32)

    hbm_like = lambda arrs: [pltpu.HBM(a.shape, a.dtype) for a in arrs]
    order = [] if after is None else [after]
    res = pl.pallas_call(
        start_body, name=name + "_start", in_specs=[_HBM] * nb + [pl.BlockSpec(memory_space=pl.ANY)] * len(order),
        out_specs=[_SEM] * (2 * n_sem) + [_HBM] * nb + [pl.BlockSpec(memory_space=pltpu.VMEM)],
        out_shape=[pltpu.SemaphoreType.DMA(())] * (2 * n_sem)
        + hbm_like(srcs) + hbm_like(lands) + [jax.ShapeDtypeStruct((8, 128), F32)],
        input_output_aliases={i: 2 * n_sem + i for i in range(nb)},
        compiler_params=pltpu.CompilerParams(has_side_effects=_EFFECT),
    )(*srcs, *lands, *order)
    sems, thru, token = res[:2 * n_sem], res[2 * n_sem:2 * n_sem + nb], res[-1]

    def wait(after):
        def wait_body(*refs):
            s = refs[nb:nb + 2 * n_sem]
            for d in descriptors(refs, s[:n_sem], s[n_sem:], True):
                d.wait_send()
                d.wait_recv()

        out = pl.pallas_call(
            wait_body, name=name + "_wait",
            in_specs=[_HBM] * nb + [_SEM] * (2 * n_sem) + [pl.BlockSpec(memory_space=pl.ANY)],
            out_specs=[_HBM] * nb, out_shape=hbm_like(srcs) + hbm_like(lands),
            input_output_aliases={i: i for i in range(nb)},
            compiler_params=pltpu.CompilerParams(has_side_effects=_EFFECT),
        )(*thru, *sems, after)
        return out[:n], out[nb - n:]

    return token, wait


def _gather_copies(a, src, land, x, y, c, chips):
    peers = [(x, y, 1 - c)] + [(*ch, c) for ch in chips]
    me = 4 * x + 2 * y + c
    return [(src, land.at[me], to, land.at[4 * to[0] + 2 * to[1] + to[2]]) for to in peers]


def _to_all_copies(a, src, land, x, y, c, chips):
    peers = [(x, y, 1 - c)] + [(*ch, cc) for ch in chips for cc in (c, 1 - c)]
    me = 4 * x + 2 * y + c
    return [(src, land.at[me], to, land.at[4 * to[0] + 2 * to[1] + to[2]]) for to in peers]


def _cores_copies(a, src, land, x, y, c, chips):
    return [(src.at[2 * k + 1 - c], land.at[k], (x, y, 1 - c), land.at[k]) for k in range(4)]


def _forward_copies(a, src, land, x, y, c, chips):
    slot = lambda ch, core: 4 * ch[0] + 2 * ch[1] + core
    return [(land.at[slot(ch, c)], land.at[slot(ch, c)], (x, y, 1 - c), land.at[slot(ch, 1 - c)]) for ch in chips]


def _chips_copies(a, src, land, x, y, c, chips):
    return [(src.at[2 * ch[0] + ch[1]], land.at[j], (*ch, c), land.at[j]) for j, ch in enumerate(chips)]


def _gather_forward(name, arrs):
    n = len(arrs)

    def body(*refs):
        ins, outs = refs[:n], refs[n:2 * n]
        send, recv = refs[2 * n:]
        x, y, c, chips = _place()
        sends, recvs = [], []
        for a in range(n):
            for j, ch in enumerate(chips):
                mine, theirs = 4 * ch[0] + 2 * ch[1] + c, 4 * ch[0] + 2 * ch[1] + 1 - c
                sends.append(pltpu.make_async_remote_copy(src_ref=ins[a].at[mine], dst_ref=outs[a].at[mine],
                                                          send_sem=send.at[a, j], recv_sem=recv.at[a, j],
                                                          device_id=(x, y, 1 - c), device_id_type=MESH))
                recvs.append(pltpu.make_async_remote_copy(src_ref=ins[a].at[theirs], dst_ref=outs[a].at[theirs],
                                                          send_sem=send.at[a, j], recv_sem=recv.at[a, j],
                                                          device_id=(x, y, 1 - c), device_id_type=MESH))
        for d in sends:
            d.start()
        for s, r in zip(sends, recvs):
            s.wait_send()
            r.wait_recv()

    return pl.pallas_call(
        body, name=name, in_specs=[_HBM] * n, out_specs=[_HBM] * n,
        out_shape=[jax.ShapeDtypeStruct(a.shape, a.dtype) for a in arrs],
        input_output_aliases={i: i for i in range(n)},
        scratch_shapes=[pltpu.SemaphoreType.DMA((n, 3))] * 2,
    )(*arrs)


def _blocks_of(r, c):
    if r * c * 4 <= (2 << 20):
        return r, c
    if r % 128 == 0:
        return 128, c
    return r, 256


def _pair_sum(name, own, got, core):
    _, r, c = got.shape
    rb, cb = _blocks_of(r, c)

    def body(s_ref, a_ref, b_ref, o_ref):
        o_ref[...] = (a_ref[...].astype(F32) + b_ref[...].astype(F32)).astype(o_ref.dtype)

    return pl.pallas_call(
        body, name=name, out_shape=jax.ShapeDtypeStruct((4, r, c), got.dtype),
        grid_spec=pltpu.PrefetchScalarGridSpec(
            num_scalar_prefetch=1, grid=(4, r // rb, c // cb),
            in_specs=[pl.BlockSpec((None, rb, cb), lambda k, i, j, s: (2 * k + s[0], i, j)),
                      pl.BlockSpec((None, rb, cb), lambda k, i, j, s: (k, i, j))],
            out_specs=pl.BlockSpec((None, rb, cb), lambda k, i, j, s: (k, i, j))),
        compiler_params=_params(3, 24 << 20),
    )(core, _in_hbm(own), _in_hbm(got))


def _adamw_math(w, g, m, v):
    m = B1 * m + (1.0 - B1) * g
    v = B2 * v + (1.0 - B2) * (g * g)
    m_hat = m / (1.0 - B1 ** STEP)
    v_hat = v / (1.0 - B2 ** STEP)
    return -LR * (m_hat / (jnp.sqrt(v_hat) + ADAM_EPS) + WD * w), m, v


def _adamw_big(name, own, got, chip, w, m, v):
    r, c = w.shape
    rb, cb = _blocks_of(r, c)

    def body(s_ref, o_ref, p_ref, w_ref, m_ref, v_ref, g_ref, d_ref, nm_ref, nv_ref):
        g = o_ref[...].astype(F32)
        for k in range(3):
            g = g + p_ref[k].astype(F32)
        g_ref[...] = g
        d_ref[...], nm_ref[...], nv_ref[...] = _adamw_math(w_ref[...], g, m_ref[...], v_ref[...])

    blk = pl.BlockSpec((rb, cb), lambda i, j, s: (i, j))
    return pl.pallas_call(
        body, name=name, out_shape=[jax.ShapeDtypeStruct((r, c), F32)] * 4,
        grid_spec=pltpu.PrefetchScalarGridSpec(
            num_scalar_prefetch=1, grid=(r // rb, c // cb),
            in_specs=[pl.BlockSpec((None, rb, cb), lambda i, j, s: (s[0], i, j)),
                      pl.BlockSpec((3, rb, cb), lambda i, j, s: (0, i, j)), blk, blk, blk],
            out_specs=[blk] * 4),
        compiler_params=_params(2, 40 << 20),
    )(chip, *map(_in_hbm, (own, got, w, m, v)))


def _adamw_multi(name, items, sums=(), row_blocks=1, own=(), dev=None):
    n, ns = len(items), len(sums)
    n_own = len(own)
    assert n_own in (0, n + ns)

    def body(*refs):
        dev_ref = refs[0] if n_own else None
        refs = refs[1:] if n_own else refs
        owns, refs = refs[:n_own], refs[n_own:]
        ins, outs = refs[:4 * n + ns], refs[4 * n + ns:]

        def total(ref, t):
            g = None
            for d in range(ref.shape[0]):
                part = jnp.where(dev_ref[0] == d, owns[t][...], ref[d]) if n_own else ref[d]
                g = part.astype(F32) if g is None else g + part.astype(F32)
            return g

        for t in range(n):
            p_ref, w_ref, m_ref, v_ref = ins[4 * t:4 * t + 4]
            g = total(p_ref, t)
            outs[4 * t][...] = g
            outs[4 * t + 1][...], outs[4 * t + 2][...], outs[4 * t + 3][...] = _adamw_math(
                w_ref[...], g, m_ref[...], v_ref[...])
        for t in range(ns):
            outs[4 * n + t][...] = total(ins[4 * n + t], n + t)

    def spec(shape, lead):
        blk = (shape[0] // row_blocks,) + tuple(shape[1:])
        nd = len(shape)
        if lead:
            return pl.BlockSpec((lead,) + blk, lambda i: (0, i) + (0,) * (nd - 1))
        return pl.BlockSpec(blk, lambda i: (i,) + (0,) * (nd - 1))

    operands, in_specs, out_specs, out_shape = [], [], [], []
    if n_own:
        operands += [dev] + list(own)
        in_specs += [pl.BlockSpec(memory_space=pltpu.SMEM)] + [spec(o.shape, 0) for o in own]
    for parts, w, m, v in items:
        assert parts.shape[1:] == w.shape == m.shape == v.shape, (name, parts.shape, w.shape)
        operands += [parts, w, m, v]
        in_specs += [spec(w.shape, parts.shape[0])] + [spec(w.shape, 0)] * 3
        out_specs += [spec(w.shape, 0)] * 4
        out_shape += [jax.ShapeDtypeStruct(w.shape, F32)] * 4
    for parts in sums:
        operands.append(parts)
        in_specs.append(spec(parts.shape[1:], parts.shape[0]))
        out_specs.append(spec(parts.shape[1:], False))
        out_shape.append(jax.ShapeDtypeStruct(parts.shape[1:], F32))
    return pl.pallas_call(
        body, name=name, grid=(row_blocks,), in_specs=in_specs, out_specs=out_specs, out_shape=out_shape,
        compiler_params=_params(1, 56 << 20),
    )(*operands)


def kernel(x, meta_tokens, mix_norm, w_in, lam_re, lam_im, log_dt, b_re, b_im, c_re, c_im, d_skip, w_glu, b_glu, q_a_norm, w_q_b, kv_a_norm, w_kv_b, out_norm_ssm, out_norm_attn, w_out, ffn_norm, w_up, conv_w, conv_b, w_down, final_norm, loss_target, m_meta_tokens, m_mix_norm, m_w_in, m_lam_re, m_lam_im, m_log_dt, m_b_re, m_b_im, m_c_re, m_c_im, m_d_skip, m_w_glu, m_b_glu, m_q_a_norm, m_w_q_b, m_kv_a_norm, m_w_kv_b, m_out_norm_ssm, m_out_norm_attn, m_w_out, m_ffn_norm, m_w_up, m_conv_w, m_conv_b, m_w_down, m_final_norm, v_meta_tokens, v_mix_norm, v_w_in, v_lam_re, v_lam_im, v_log_dt, v_b_re, v_b_im, v_c_re, v_c_im, v_d_skip, v_w_glu, v_b_glu, v_q_a_norm, v_w_q_b, v_kv_a_norm, v_w_kv_b, v_out_norm_ssm, v_out_norm_attn, v_w_out, v_ffn_norm, v_w_up, v_conv_w, v_conv_b, v_w_down, v_final_norm):
    given = dict(locals())
    w = {n: given[n] for n in WEIGHTS}
    m = {n: given['m_' + n] for n in WEIGHTS}
    v = {n: given['v_' + n] for n in WEIGHTS}
    dev = 4 * lax.axis_index("x") + 2 * lax.axis_index("y") + lax.axis_index("c")

    shard = {
        'w_in': jnp.pad(w_in[0], ((0, 0), (0, D_IN_PAD - D_IN))),
        'w_glu': w_glu[0],
        'w_q_b': jnp.pad(w_q_b[0], ((0, 0), (0, HEAD_PAD - QK_NOPE - QK_ROPE))),
        'w_kv_b': w_kv_b[0],
        'w_out': w_out[0],
        'w_up': w_up[0],
        'w_down': w_down[0],
    }
    core = lax.axis_index("c").astype(jnp.int32).reshape(1)
    chip = (2 * lax.axis_index("x") + lax.axis_index("y")).astype(jnp.int32).reshape(1)
    early, late = ['w_in', 'w_glu', 'w_q_b', 'w_kv_b'], ['w_out', 'w_up', 'w_down']
    shard = {n: a.astype(BF16) for n, a in shard.items()}

    gathered = _all_gather("gather_early", [shard[n] for n in early]
                           + [meta_tokens, jnp.pad(conv_w[0], ((0, 5), (0, 0)))])
    wf = dict(zip(early, gathered[:len(early)]))
    for n in ('w_in', 'w_glu'):
        wf[n] = wf[n].reshape(-1, wf[n].shape[-1])
    meta = gathered[-2].transpose(1, 0, 2).reshape(N_META, D)
    wf['conv_w'] = gathered[-1][:, :3].transpose(1, 0, 2).reshape(3, D_FF)
    gather_token, wait_late = _split_exchange("gather_late", [shard[n] for n in late],
                                              [(N_DEV,) + shard[n].shape for n in late], _gather_copies, 4,
                                              after=gathered[0])

    def late_weights(after):
        mine, landed = wait_late(after)
        full = [lax.dynamic_update_slice(a, s[None], (dev, 0, 0)) for s, a in zip(mine, landed)]
        full_out, = _gather_forward("gather_forward_w_out", full[:1])
        token, wait_ffn = _split_exchange("gather_forward_ffn", full[1:], None, _forward_copies, 3, after=full_out)

        def ffn_weights(after):
            full_up, full_down = wait_ffn(after)[0]
            return full_up, full_down.reshape(-1, D)

        return full_out.reshape(-1, D), token, ffn_weights

    pending, between_cores = [], {}

    def send_grads(names, arg, stage):
        if stage == 'cores':
            token, between_cores[names[0]] = _split_exchange(
                "reduce_cores_" + names[0], arg, [(4,) + a.shape[1:] for a in arg], _cores_copies, 4)
            return token
        mine, got = between_cores.pop(names[0])(arg) if stage == 'chips' else (
            arg, _exchange_cores("reduce_cores_" + names[0], arg))
        sums = [_pair_sum("sum_cores_" + n, a, g, core) for n, a, g in zip(names, mine, got)]
        token, wait = _split_exchange("reduce_chips_" + names[0], sums, [(3,) + s.shape[1:] for s in sums],
                                      _chips_copies, 3)
        pending.append((names, wait))
        return token

    small_sent = []
    s5_bc = ['b_re', 'b_im', 'c_re', 'c_im']

    def send_small(sm, dcw4, loss_part):
        names = [n for n in SMALL if n in sm]
        arrs = [sm[n].reshape(-1, 128).astype(BF16) if n in s5_bc else sm[n] for n in names] + [dcw4, loss_part]
        token, wait = _split_exchange("gather_small", arrs, [(N_DEV,) + a.shape for a in arrs], _to_all_copies, 7,
                                      all_of=lambda a, land: land.at[pl.ds(0, N_DEV - 1)])
        small_sent.append((names, wait))
        return token

    small = {n: w[n] for n in SMALL}
    small['mix_norm'] = mix_norm + gather_token[0:1, 0:1]
    grad_x, dmeta, d_mix_norm, big = _local_step(x[0], loss_target[0], meta, small, wf, late_weights, send_grads,
                                                  send_small)
    grads, deltas, new_m, new_v = {}, {}, {}, {}

    def keep(n, outs):
        grads[n], deltas[n], new_m[n], new_v[n] = (o.reshape(w[n].shape) for o in outs)

    rest = list(big)
    from_core = _exchange_cores("reduce_cores", [big[n] for n in rest])
    chip_sums = [_pair_sum("sum_cores_" + n, big[n], got, core) for n, got in zip(rest, from_core)]
    last = chip_sums + [d_mix_norm, dmeta]
    n_big = len(chip_sums)
    _, wait_last = _split_exchange(
        "reduce_last", last, [(3,) + s.shape[1:] for s in chip_sums] + [(N_DEV,) + a.shape for a in last[n_big:]],
        lambda a, *at: _chips_copies(a, *at) if a < n_big else _to_all_copies(a, *at), 7,
        all_of=lambda a, land: land if a < n_big else land.at[pl.ds(0, N_DEV - 1)])

    def update(n, own, got):
        if n == 'w_up':
            outs = _adamw_big("adamw_" + n, own, got, chip, w[n][0].T, m[n][0].T, v[n][0].T)
            return keep(n, [o.T for o in outs])
        cols = w[n].shape[2]
        keep(n, _adamw_big("adamw_" + n, own[:, :, :cols], got[:, :, :cols], chip, w[n][0], m[n][0], v[n][0]))

    for names, wait in pending:
        for n, own, got in zip(names, *wait(chip_sums[0])):
            update(n, own, got)
    updated = [new_v[n].reshape(-1)[:1] for names, _ in pending for n in names]
    sent, landed = wait_last(functools.reduce(jnp.add, updated))
    for n, own, got in zip(rest, sent, landed):
        update(n, own, got)

    device = dev.astype(jnp.int32).reshape(1)
    outs = _adamw_multi("adamw_last", [(landed[n_big], mix_norm, m['mix_norm'], v['mix_norm'])],
                        sums=[landed[n_big + 1]], own=sent[n_big:], dev=device)
    keep('mix_norm', outs[:4])
    g_meta = outs[4]
    names, wait = small_sent[0]
    mine, landed = wait(grads['mix_norm'])
    parts, own = dict(zip(names + ['conv_w', 'loss'], landed)), dict(zip(names + ['conv_w', 'loss'], mine))
    wide = [n for n in names if n not in s5_bc]
    summed = ['conv_w', 'loss'] + s5_bc
    outs = _adamw_multi("adamw_small",
                        [(parts[n],) + tuple(t[n].reshape(own[n].shape) for t in (w, m, v)) for n in wide],
                        sums=[parts[n] for n in summed], own=[own[n] for n in wide + summed], dev=device)
    for i, n in enumerate(wide):
        keep(n, outs[4 * i:4 * i + 4])
    g_cw4, loss = outs[-6:-4]
    rows_gc = lambda a, n: (a.reshape(N_GROUPS, SSM_STATE, SSM_GROUP).transpose(0, 2, 1) if n[0] == 'b' else a
                            ).reshape(N_GROUPS * SSM_GROUP, SSM_STATE)
    outs = _adamw_multi("adamw_s5_bc", [(g.reshape(1, N_GROUPS * SSM_GROUP, SSM_STATE),)
                                        + tuple(rows_gc(t[n], n) for t in (w, m, v))
                                        for n, g in zip(s5_bc, outs[-4:])])
    for i, n in enumerate(s5_bc):
        back = outs[4 * i:4 * i + 4]
        if n[0] == 'b':
            back = [o.reshape(N_GROUPS, SSM_GROUP, SSM_STATE).transpose(0, 2, 1) for o in back]
        keep(n, back)

    g_meta = lax.dynamic_slice(g_meta, (0, dev * (D // N_DEV)), (N_META, D // N_DEV))
    g_conv = lax.dynamic_slice(g_cw4.transpose(1, 0, 2).reshape(3, D_FF), (0, dev * (D_FF // N_DEV)),
                               (3, D_FF // N_DEV))
    rows8 = lambda a: jnp.pad(a.reshape(3, D_FF // N_DEV), ((0, 5), (0, 0)))
    outs = _adamw_multi("adamw_cols", [(g_meta[None], meta_tokens, m['meta_tokens'], v['meta_tokens']),
                                       (rows8(g_conv)[None], rows8(conv_w), rows8(m['conv_w']), rows8(v['conv_w']))])
    keep('meta_tokens', outs[:4])
    keep('conv_w', [o[:3] for o in outs[4:]])

    return (loss[0, 0], grad_x[None], *[grads[n] for n in WEIGHTS], *[deltas[n] for n in WEIGHTS],
            *[new_m[n] for n in WEIGHTS], *[new_v[n] for n in WEIGHTS])
```

```python
import functools
import math

import jax
import jax.numpy as jnp
from jax import lax
from jax.experimental import pallas as pl
from jax.experimental.pallas import tpu as pltpu

F32, BF16 = jnp.float32, jnp.bfloat16

N_DEV = 8
N_META, SEQ, D = 16, 2048, 2048
T = N_META + SEQ
TR = 688
EPS = 1e-6
D_SSM, Q_LORA, KV_LORA, QK_ROPE = 1024, 512, 256, 64
D_IN = D_SSM + Q_LORA + KV_LORA + QK_ROPE
D_IN_PAD = 1920
N_HEADS, QK_NOPE, V_HEAD = 8, 128, 128
HEAD_PAD = 256
SM_SCALE = 1.0 / math.sqrt(QK_NOPE + QK_ROPE)
CHUNK = 64
N_GROUPS, SSM_GROUP, SSM_STATE = 64, 16, 64
N_SEG = 8
SEG = T // N_SEG
GQ = 8
D_FF = 5504
FF_PIECE = 1376
ROPE_BASE = 10000.0
LR, B1, B2, ADAM_EPS, WD, STEP = 0.001, 0.9, 0.999, 1e-08, 0.01, 10
VMEM_CAP = 60 * 1024 * 1024
MESH = pl.DeviceIdType.MESH

WEIGHTS = ['meta_tokens', 'mix_norm', 'w_in', 'lam_re', 'lam_im', 'log_dt', 'b_re', 'b_im', 'c_re', 'c_im',
           'd_skip', 'w_glu', 'b_glu', 'q_a_norm', 'w_q_b', 'kv_a_norm', 'w_kv_b', 'out_norm_ssm',
           'out_norm_attn', 'w_out', 'ffn_norm', 'w_up', 'conv_w', 'conv_b', 'w_down', 'final_norm']
BIG = ['w_in', 'w_glu', 'w_q_b', 'w_kv_b', 'w_out', 'w_up', 'w_down']
SMALL = [n for n in WEIGHTS if n not in BIG and n not in ('meta_tokens', 'conv_w')]


def _nbytes(shape, dtype):
    return math.prod(shape) * jnp.dtype(dtype).itemsize


def _in_hbm(a):
    return pltpu.with_memory_space_constraint(a, pltpu.HBM) if a.size * a.dtype.itemsize >= (1 << 20) else a


def _out_hbm(shape, dtype):
    big = math.prod(shape) * jnp.dtype(dtype).itemsize >= (1 << 20)
    return pltpu.HBM(shape, dtype) if big else jax.ShapeDtypeStruct(shape, dtype)


def _params(sem, need):
    return pltpu.CompilerParams(dimension_semantics=("arbitrary",) * sem,
                                vmem_limit_bytes=int(min(VMEM_CAP, max(need, 16 * 1024 * 1024))))


_DIMS = {'nn': (((1,), (0,)), ((), ())), 'nt': (((1,), (1,)), ((), ())), 'tn': (((0,), (0,)), ((), ()))}


def _mm(name, a, b, mode, tm, tn, tk, out_dtype, stack=None, res=None):
    sa, sb, so = stack in ('a_out', 'ab_red'), stack in ('b_out', 'ab_red'), stack in ('a_out', 'b_out')
    nj = a.shape[0] if sa else (b.shape[0] if sb else 1)
    a2, b2 = a.shape[-2:], b.shape[-2:]
    if mode == 'nn':
        (m, k), (k2, n) = a2, b2
    elif mode == 'nt':
        (m, k), (n, k2) = a2, b2
    else:
        (k, m), (k2, n) = a2, b2
    assert k == k2 and m % tm == 0 and n % tn == 0 and k % tk == 0, (name, a.shape, b.shape)
    n_jo, n_jr, n_k = (nj if so else 1), (nj if stack == 'ab_red' else 1), k // tk
    grid = (n_jo, m // tm, n // tn, n_jr, n_k)
    a_blk, a_idx = ((tk, tm), lambda i, j, kk: (kk, i)) if mode == 'tn' else ((tm, tk), lambda i, j, kk: (i, kk))
    b_blk, b_idx = ((tn, tk), lambda i, j, kk: (j, kk)) if mode == 'nt' else ((tk, tn), lambda i, j, kk: (kk, j))

    def spec(blk, idx, stacked, on_out):
        if not stacked:
            return pl.BlockSpec(blk, lambda jo, i, j, jr, kk: idx(i, j, kk))
        return pl.BlockSpec((None,) + blk, lambda jo, i, j, jr, kk: ((jo if on_out else jr),) + idx(i, j, kk))

    o_idx = lambda i, j, kk: (i, j)
    in_specs = [spec(a_blk, a_idx, sa, stack == 'a_out'), spec(b_blk, b_idx, sb, stack == 'b_out')]
    operands = [a, b]
    if res is not None:
        in_specs.append(spec((tm, tn), o_idx, False, False))
        operands.append(res)
    n_red = n_jr * n_k
    dims = _DIMS[mode]

    def body(*refs):
        a_ref, b_ref = refs[0], refs[1]
        res_ref = refs[2] if res is not None else None
        o_ref = refs[3] if res is not None else refs[2]
        part = lax.dot_general(a_ref[...].astype(BF16), b_ref[...].astype(BF16), dims,
                               preferred_element_type=F32)

        def finish(total):
            if res_ref is not None:
                total = total + res_ref[...]
            o_ref[...] = total.astype(o_ref.dtype)

        if n_red == 1:
            finish(part)
        else:
            acc_ref = refs[-1]
            step = pl.program_id(3) * n_k + pl.program_id(4)

            @pl.when(step == 0)
            def _():
                acc_ref[...] = part

            @pl.when(step > 0)
            def _():
                acc_ref[...] += part

            @pl.when(step == n_red - 1)
            def _():
                finish(acc_ref[...])

    out_shape = ((nj,) if so else ()) + (m, n)
    need = 2 * (_nbytes(a_blk, a.dtype) + _nbytes(b_blk, b.dtype) + _nbytes((tm, tn), out_dtype)
                + (_nbytes((tm, tn), F32) if res is not None else 0)) + 3 * _nbytes((tm, tn), F32) + (4 << 20)
    return pl.pallas_call(
        body, name=name, grid=grid, in_specs=in_specs, out_specs=spec((tm, tn), o_idx, so, True),
        out_shape=_out_hbm(out_shape, out_dtype),
        scratch_shapes=[pltpu.VMEM((tm, tn), F32)] if n_red > 1 else [],
        compiler_params=_params(5, need),
    )(*map(_in_hbm, operands))


def _rowwise(name, fn, row_ins, const_ins, row_outs, acc_outs, tr=TR):
    n_ri, n_ci, n_ro, n_ao = len(row_ins), len(const_ins), len(row_outs), len(acc_outs)

    def body(*refs):
        ri, ci = refs[:n_ri], refs[n_ri:n_ri + n_ci]
        ro, ao = refs[n_ri + n_ci:n_ri + n_ci + n_ro], refs[n_ri + n_ci + n_ro:]
        i = pl.program_id(0)
        outs = fn(i * tr, *[r[...] for r in ri], *[r[...] for r in ci])
        for r, v in zip(ro, outs[:n_ro]):
            r[...] = v.astype(r.dtype)
        if n_ao:
            @pl.when(i == 0)
            def _():
                for r, v in zip(ao, outs[n_ro:]):
                    r[...] = v

            @pl.when(i > 0)
            def _():
                for r, v in zip(ao, outs[n_ro:]):
                    r[...] += v

    in_specs = [pl.BlockSpec((tr, w), functools.partial(lambda cb, i: (i, cb), cb)) for _, w, cb in row_ins]
    in_specs += [pl.BlockSpec(c.shape, functools.partial(lambda nd, i: (0,) * nd, c.ndim)) for c in const_ins]
    out_specs = [pl.BlockSpec((tr, w), lambda i: (i, 0)) for w, _ in row_outs]
    out_specs += [pl.BlockSpec((1, w), lambda i: (0, 0)) for w in acc_outs]
    out_shape = [_out_hbm((T, w), dt) for w, dt in row_outs]
    out_shape += [jax.ShapeDtypeStruct((1, w), F32) for w in acc_outs]
    need = 2 * (sum(_nbytes((tr, w), a.dtype) for a, w, _ in row_ins) + sum(_nbytes((tr, w), dt) for w, dt in row_outs))
    need = 3 * need + (8 << 20)
    return pl.pallas_call(
        body, name=name, grid=(T // tr,), in_specs=in_specs, out_specs=out_specs, out_shape=out_shape,
        compiler_params=_params(1, need),
    )(*[_in_hbm(a) for a, _, _ in row_ins], *const_ins)


def _rms_fwd_fn(row0, h, g):
    h = h.astype(F32)
    r = lax.rsqrt(jnp.mean(h * h, axis=-1, keepdims=True) + EPS)
    return (h * r) * g, r


def _rms_bwd(dxn, h, r, g):
    xhat = h * r
    dxg = dxn * g
    dh = r * (dxg - xhat * jnp.mean(dxg * xhat, axis=-1, keepdims=True))
    return dh, jnp.sum(dxn * xhat, axis=0, keepdims=True)


def _rms_bwd_fn(row0, dxn, h, r, g):
    return _rms_bwd(dxn.astype(F32), h.astype(F32), r, g)


def _rms_bwd_res_fn(row0, dxn, h, r, res, g):
    dh, dg = _rms_bwd(dxn.astype(F32), h.astype(F32), r, g)
    return dh + res, dg


def _gelu_fn(row0, y):
    return (jax.nn.gelu(y),)


def _glu_fwd_fn(row0, y, gl, b, gain):
    ya = jax.nn.gelu(y) * jax.nn.sigmoid(gl + b)
    return _rms_fwd_fn(row0, ya, gain)


def _glu_bwd_fn(row0, dyan, y, gl, ra, b, gain):
    g = jax.nn.gelu(y)
    s = jax.nn.sigmoid(gl + b)
    dya, dgain = _rms_bwd(dyan, g * s, ra, gain)
    dgl = dya * g * s * (1.0 - s)
    return dgl, dya * s, dgain, jnp.sum(dgl, axis=0, keepdims=True)


def _gelu_bwd_fn(row0, dg, y):
    return (jax.vjp(jax.nn.gelu, y)[1](dg)[0],)


def _loss_fn(row0, h, tgt, gain):
    r = lax.rsqrt(jnp.mean(h * h, axis=-1, keepdims=True) + EPS)
    xhat = h * r
    rows = row0 + lax.broadcasted_iota(jnp.int32, h.shape, 0)
    err = jnp.where(rows >= N_META, xhat * gain - tgt, 0.0)
    loss = jnp.full((1, 128), jnp.sum(err * err) * (0.5 / D), F32)
    dh, dgain = _rms_bwd(err * (1.0 / D), h, r, gain)
    return dh, dgain, loss


def _rope_tables():
    pos = jnp.arange(T, dtype=jnp.int32)
    inv_freq = 1.0 / (ROPE_BASE ** (jnp.arange(0, QK_ROPE, 2, dtype=F32) / QK_ROPE))
    ang = pos.astype(F32)[:, None] * inv_freq[None, :]
    cos, sin, z32, z64 = jnp.cos(ang), jnp.sin(ang), jnp.zeros((T, 32), F32), jnp.zeros((T, 64), F32)
    return (jnp.concatenate([cos, cos, z64], 1), jnp.concatenate([-sin, z32, z64], 1),
            jnp.concatenate([z32, sin, z64], 1))


def _rot(x, cc, s1, s2):
    return x * cc + pltpu.roll(x, 96, 1) * s1 + pltpu.roll(x, 32, 1) * s2


def _derot(d, cc, s1, s2):
    return d * cc + pltpu.roll(d * s1, 32, 1) + pltpu.roll(d * s2, 96, 1)


def _chunk_of(pos):
    return jnp.where(pos < N_META, 0, 1 + ((pos - N_META) >> 6))


def _key_len(t):
    last_chunk = 1 + ((t + 1) * TR - 1 - N_META) // CHUNK
    return min(T, pl.cdiv(N_META + last_chunk * CHUNK, 128) * 128)


def _attn_probs(t, q_ref, kv_ref, kpe_ref, tq, tk):
    kl = _key_len(t)
    q = q_ref[...]
    qn = q[:, :QK_NOPE].astype(BF16)
    qp = _rot(q[:, QK_NOPE:], tq[0][...], tq[1][...], tq[2][...]).astype(BF16)
    kn = kv_ref[:kl, :QK_NOPE].astype(BF16)
    v = kv_ref[:kl, QK_NOPE:].astype(BF16)
    kp = _rot(kpe_ref[:kl, :], tk[0][:kl, :], tk[1][:kl, :], tk[2][:kl, :]).astype(BF16)
    s = lax.dot_general(qn, kn, _DIMS['nt'], preferred_element_type=F32)
    s = s + lax.dot_general(qp, kp, _DIMS['nt'], preferred_element_type=F32)
    qc = _chunk_of(t * TR + lax.broadcasted_iota(jnp.int32, (TR, 1), 0))
    kc = _chunk_of(lax.broadcasted_iota(jnp.int32, (1, kl), 1))
    s = jnp.where(kc <= qc, s * SM_SCALE, jnp.finfo(F32).min)
    p = jnp.exp(s - jnp.max(s, axis=-1, keepdims=True))
    p = p * (1.0 / jnp.sum(p, axis=-1, keepdims=True))
    return qn, qp, kn, kp, v, p


def _per_q_tile(fn):
    for t in range(T // TR):
        pl.when(pl.program_id(1) == t)(functools.partial(fn, t))


def _attn_specs(z, tabs):
    q_spec = pl.BlockSpec((None, TR, HEAD_PAD), lambda h, i: (h, i, 0))
    kv_spec = pl.BlockSpec((None, T, HEAD_PAD), lambda h, i: (h, 0, 0))
    kpe_spec = pl.BlockSpec((T, 128), lambda h, i: (0, (D_IN - QK_ROPE) // 128))
    tq = [pl.BlockSpec((TR, 128), lambda h, i: (i, 0))] * 3
    tk = [pl.BlockSpec((T, 128), lambda h, i: (0, 0))] * 3
    return [q_spec, kv_spec, kpe_spec] + tq + tk


def _attn_fwd(q, kv, z, tabs):
    def body(q_ref, kv_ref, kpe_ref, c1, c2, c3, k1, k2, k3, o_ref):
        def tile(t):
            _, _, _, _, v, p = _attn_probs(t, q_ref, kv_ref, kpe_ref, (c1, c2, c3), (k1, k2, k3))
            o_ref[...] = jnp.dot(p.astype(BF16), v, preferred_element_type=F32)

        _per_q_tile(tile)

    return pl.pallas_call(
        body, name="attn_fwd", grid=(N_HEADS, T // TR), in_specs=_attn_specs(z, tabs),
        out_specs=pl.BlockSpec((TR, V_HEAD), lambda h, i: (i, h)),
        out_shape=_out_hbm((T, N_HEADS * V_HEAD), F32),
        compiler_params=_params(2, 48 << 20),
    )(_in_hbm(q), _in_hbm(kv), _in_hbm(z), *tabs, *tabs)


def _attn_bwd(q, kv, z, tabs, dyb):
    def body(q_ref, kv_ref, kpe_ref, c1, c2, c3, k1, k2, k3, do_ref, dq_ref, dkv_ref, dkpe_ref):
        @pl.when(pl.program_id(1) == 0)
        def _():
            dkv_ref[...] = jnp.zeros(dkv_ref.shape, F32)
            dkpe_ref[...] = jnp.zeros(dkpe_ref.shape, F32)

        def tile(t):
            kl = _key_len(t)
            qn, qp, kn, kp, v, p = _attn_probs(t, q_ref, kv_ref, kpe_ref, (c1, c2, c3), (k1, k2, k3))
            do = do_ref[...].astype(BF16)
            dp = lax.dot_general(do, v, _DIMS['nt'], preferred_element_type=F32)
            ds = (p * (dp - jnp.sum(p * dp, axis=-1, keepdims=True)) * SM_SCALE).astype(BF16)
            pb = p.astype(BF16)
            dq_ref[:, :QK_NOPE] = jnp.dot(ds, kn, preferred_element_type=F32).astype(dq_ref.dtype)
            dqp = _derot(jnp.dot(ds, kp, preferred_element_type=F32), c1[...], c2[...], c3[...])
            dq_ref[:, QK_NOPE:] = dqp.astype(dq_ref.dtype)
            dkv_ref[:kl, :QK_NOPE] += lax.dot_general(ds, qn, _DIMS['tn'], preferred_element_type=F32)
            dkv_ref[:kl, QK_NOPE:] += lax.dot_general(pb, do, _DIMS['tn'], preferred_element_type=F32)
            dkpe_ref[:kl, :] += lax.dot_general(ds, qp, _DIMS['tn'], preferred_element_type=F32)

        _per_q_tile(tile)

    return pl.pallas_call(
        body, name="attn_bwd", grid=(N_HEADS, T // TR),
        in_specs=_attn_specs(z, tabs) + [pl.BlockSpec((TR, V_HEAD), lambda h, i: (i, h))],
        out_specs=[pl.BlockSpec((None, TR, HEAD_PAD), lambda h, i: (h, i, 0)),
                   pl.BlockSpec((None, T, HEAD_PAD), lambda h, i: (h, 0, 0)),
                   pl.BlockSpec((None, T, 128), lambda h, i: (h, 0, 0))],
        out_shape=[_out_hbm((N_HEADS, T, HEAD_PAD), BF16), _out_hbm((N_HEADS, T, HEAD_PAD), F32),
                   _out_hbm((N_HEADS, T, 128), F32)],
        compiler_params=_params(2, 56 << 20),
    )(_in_hbm(q), _in_hbm(kv), _in_hbm(z), *tabs, *tabs, _in_hbm(dyb))


def _kpe_bwd(dkpe, tabs):
    def body(d_ref, c1, c2, c3, o_ref):
        d = d_ref[0]
        for h in range(1, N_HEADS):
            d = d + d_ref[h]
        o_ref[...] = _derot(d, c1[...], c2[...], c3[...]).astype(o_ref.dtype)

    tile = pl.BlockSpec((TR, 128), lambda i: (i, 0))
    return pl.pallas_call(
        body, name="kpe_bwd", grid=(T // TR,),
        in_specs=[pl.BlockSpec((N_HEADS, TR, 128), lambda i: (0, i, 0)), tile, tile, tile], out_specs=tile,
        out_shape=jax.ShapeDtypeStruct((T, 128), BF16), compiler_params=_params(1, 16 << 20),
    )(_in_hbm(dkpe), *tabs)


def _s5_prep_math(lam_re, lam_im, log_dt, bt_re, bt_im):
    dt = jnp.exp(log_dt)
    mag = jnp.exp(lam_re * dt)
    a_re, a_im = mag * jnp.cos(lam_im * dt), mag * jnp.sin(lam_im * dt)
    den = lam_re * lam_re + lam_im * lam_im
    w_re = ((a_re - 1.0) * lam_re + a_im * lam_im) / den
    w_im = (a_im * lam_re - (a_re - 1.0) * lam_im) / den
    return a_re, a_im, w_re[None] * bt_re - w_im[None] * bt_im, w_re[None] * bt_im + w_im[None] * bt_re


def _eye_groups():
    return (lax.broadcasted_iota(jnp.int32, (N_GROUPS, N_GROUPS), 0)
            == lax.broadcasted_iota(jnp.int32, (N_GROUPS, N_GROUPS), 1)).astype(F32)


def _row_to_col(row):
    return jnp.sum(_eye_groups() * row, axis=1, keepdims=True)


def _s5_prep(lam_re, lam_im, log_dt, bt_re, bt_im):
    def body(lr, li, ld, br, bi, ar, ai, bbr, bbi):
        ar[...], ai[...], bbr[...], bbi[...] = _s5_prep_math(lr[...], li[...], _row_to_col(ld[...]), br[...], bi[...])

    gp, cgp = jax.ShapeDtypeStruct((N_GROUPS, SSM_STATE), F32), jax.ShapeDtypeStruct(bt_re.shape, F32)
    return pl.pallas_call(body, name="s5_prep", out_shape=[gp, gp, cgp, cgp])(lam_re, lam_im, log_dt, bt_re, bt_im)


def _s5_prep_bwd(lam_re, lam_im, log_dt, bt_re, bt_im, da_re, da_im, dbb_re, dbb_im):
    def body(lr, li, ld, br, bi, dar, dai, dbr, dbi, o1, o2, o3, o4, o5):
        _, vjp = jax.vjp(_s5_prep_math, lr[...], li[...], _row_to_col(ld[...]), br[...], bi[...])
        o1[...], o2[...], dld, o4[...], o5[...] = vjp((dar[...], dai[...], dbr[...], dbi[...]))
        o3[...] = jnp.sum(_eye_groups() * dld, axis=0, keepdims=True)

    ins = (lam_re, lam_im, log_dt, bt_re, bt_im)
    return pl.pallas_call(body, name="s5_prep_bwd", out_shape=[jax.ShapeDtypeStruct(a.shape, F32) for a in ins])(
        *ins, da_re, da_im, dbb_re, dbb_im)


def _cmul(ar, ai, br, bi):
    return ar * br - ai * bi, ar * bi + ai * br


def _seg_rows(i):
    return pl.ds(pl.multiple_of(i * N_SEG, N_SEG), N_SEG)


def _scan(xr, xi, ar, ai, reverse):
    zero = jnp.zeros(ar.shape, F32)

    def local(j, carry):
        rows = _seg_rows(SEG - 1 - j if reverse else j)
        nr, ni = _cmul(ar, ai, *carry)
        nr, ni = nr + xr[rows, :], ni + xi[rows, :]
        xr[rows, :], xi[rows, :] = nr, ni
        return nr, ni

    er, ei = lax.fori_loop(0, SEG, local, (zero, zero))
    pr, pi = ar, ai
    for _ in range(8):
        pr, pi = _cmul(pr, pi, pr, pi)
    pr, pi = _cmul(*_cmul(pr, pi, ar, ai), ar, ai)
    row = lax.broadcasted_iota(jnp.int32, ar.shape, 0)
    edge, shift = (N_SEG - 1, N_SEG - 1) if reverse else (0, 1)
    hr, hi = zero, zero
    for _ in range(N_SEG - 1):
        tr_, ti_ = _cmul(pr, pi, hr, hi)
        hr = jnp.where(row == edge, 0.0, pltpu.roll(tr_ + er, shift, 0))
        hi = jnp.where(row == edge, 0.0, pltpu.roll(ti_ + ei, shift, 0))

    def fix(j, carry):
        rows = _seg_rows(SEG - 1 - j if reverse else j)
        xr[rows, :] += carry[0]
        xi[rows, :] += carry[1]
        return _cmul(ar, ai, *carry)

    lax.fori_loop(0, SEG, fix, _cmul(ar, ai, hr, hi))
    return hr, hi


def _split(x):
    hi = x.astype(BF16)
    return hi, (x - hi.astype(F32)).astype(BF16)


def _dot3(xs, ys, mode='nn'):
    d = lambda p, q: lax.dot_general(p, q, _DIMS[mode], preferred_element_type=F32)
    return d(xs[0], ys[0]) + (d(xs[1], ys[0]) + d(xs[0], ys[1]))


def _group_masks():
    ns = GQ * SSM_STATE
    rep = (lax.broadcasted_iota(jnp.int32, (SSM_STATE, ns), 1) % SSM_STATE
           == lax.broadcasted_iota(jnp.int32, (SSM_STATE, ns), 0)).astype(BF16)
    own = (lax.broadcasted_iota(jnp.int32, (128, ns), 0) // SSM_GROUP
           == lax.broadcasted_iota(jnp.int32, (128, ns), 1) // SSM_STATE)
    return rep, own


def _block_diag(x, rep, own):
    dot = lambda p: jnp.dot(p, rep, preferred_element_type=F32)
    return tuple(jnp.where(own, dot(part), 0.0).astype(BF16) for part in _split(x))


def _block_rows(full, rep, own):
    masked = _split(jnp.where(own, full, 0.0))
    dot = lambda p: lax.dot_general(p, rep, _DIMS['nt'], preferred_element_type=F32)
    return dot(masked[0]) + dot(masked[1])


def _s5_specs():
    col = pl.BlockSpec((T, 128), lambda q: (0, q))
    a_spec = pl.BlockSpec((None, 2, N_SEG, GQ * SSM_STATE), lambda q: (q, 0, 0, 0))
    rows = pl.BlockSpec((128, SSM_STATE), lambda q: (q, 0))
    d_spec = pl.BlockSpec((1, 128), lambda q: (0, q))
    return col, a_spec, rows, d_spec


def _s5_fwd(u, a8, bb_re, bb_im, c_re, c_im, dsk):
    def body(u_ref, a_ref, bre_ref, bim_ref, cre_ref, cim_ref, d_ref, y_ref, hre, him):
        u_ = u_ref[...]
        us = _split(u_)
        rep, own = _group_masks()
        hre[...] = _dot3(us, _block_diag(bre_ref[...], rep, own))
        him[...] = _dot3(us, _block_diag(bim_ref[...], rep, own))
        _scan(hre, him, a_ref[0], a_ref[1], reverse=False)
        y_ref[...] = (_dot3(_split(hre[...]), _block_diag(cre_ref[...], rep, own), 'nt')
                      - _dot3(_split(him[...]), _block_diag(cim_ref[...], rep, own), 'nt') + d_ref[...] * u_)

    col, a_spec, rows, d_spec = _s5_specs()
    return pl.pallas_call(
        body, name="s5_fwd", grid=(N_GROUPS // GQ,), in_specs=[col, a_spec, rows, rows, rows, rows, d_spec],
        out_specs=col, out_shape=_out_hbm((T, D_SSM), F32),
        scratch_shapes=[pltpu.VMEM((T, GQ * SSM_STATE), F32)] * 2,
        compiler_params=_params(1, 40 << 20),
    )(_in_hbm(u), a8, bb_re, bb_im, c_re, c_im, dsk)


def _s5_bwd(u, dy, a8, bb_re, bb_im, c_re, c_im, dsk):
    def body(u_ref, dy_ref, a_ref, bre_ref, bim_ref, cre_ref, cim_ref, d_ref,
             du_ref, dbre_ref, dbim_ref, dcre_ref, dcim_ref, da_ref, dd_ref, hre, him, lre, lim):
        u_, dy_ = u_ref[...], dy_ref[...]
        us, dys = _split(u_), _split(dy_)
        ar, ai = a_ref[0], a_ref[1]
        rep, own = _group_masks()
        bres, bims = _block_diag(bre_ref[...], rep, own), _block_diag(bim_ref[...], rep, own)
        hre[...] = _dot3(us, bres)
        him[...] = _dot3(us, bims)
        h0r, h0i = _scan(hre, him, ar, ai, reverse=False)
        lre[...] = _dot3(dys, _block_diag(cre_ref[...], rep, own))
        lim[...] = -_dot3(dys, _block_diag(cim_ref[...], rep, own))
        _scan(lre, lim, ar, -ai, reverse=True)

        def acc_da(i, carry):
            lr, li = lre[_seg_rows(i), :], lim[_seg_rows(i), :]
            pr, pi = hre[_seg_rows(i - 1), :], him[_seg_rows(i - 1), :]
            return carry[0] + lr * pr + li * pi, carry[1] + li * pr - lr * pi

        lr, li = lre[_seg_rows(0), :], lim[_seg_rows(0), :]
        dar, dai = lax.fori_loop(1, SEG, acc_da, (lr * h0r + li * h0i, li * h0r - lr * h0i))
        da_ref[0:1, :] = jnp.sum(dar, axis=0, keepdims=True)
        da_ref[1:2, :] = jnp.sum(dai, axis=0, keepdims=True)
        dot = lambda p, q, mode: lax.dot_general(p, q, _DIMS[mode], preferred_element_type=F32)
        lr1, li1 = lre[...].astype(BF16), lim[...].astype(BF16)
        du_ref[...] = dy_ * d_ref[...] + dot(lr1, bres[0], 'nt') + dot(li1, bims[0], 'nt')
        dbre_ref[...] = _block_rows(dot(us[0], lr1, 'tn'), rep, own)
        dbim_ref[...] = _block_rows(dot(us[0], li1, 'tn'), rep, own)
        dcre_ref[...] = _block_rows(dot(dys[0], hre[...].astype(BF16), 'tn'), rep, own)
        dcim_ref[...] = -_block_rows(dot(dys[0], him[...].astype(BF16), 'tn'), rep, own)
        dd_ref[...] = jnp.sum(dy_ * u_, axis=0, keepdims=True)

    col, a_spec, rows, d_spec = _s5_specs()
    nq, ns = N_GROUPS // GQ, GQ * SSM_STATE
    per_row = jax.ShapeDtypeStruct((N_GROUPS * SSM_GROUP, SSM_STATE), F32)
    return pl.pallas_call(
        body, name="s5_bwd", grid=(nq,),
        in_specs=[col, col, a_spec, rows, rows, rows, rows, d_spec],
        out_specs=[col, rows, rows, rows, rows, pl.BlockSpec((None, 2, ns), lambda q: (q, 0, 0)), d_spec],
        out_shape=[_out_hbm((T, D_SSM), F32), per_row, per_row, per_row, per_row,
                   jax.ShapeDtypeStruct((nq, 2, ns), F32), jax.ShapeDtypeStruct((1, D_SSM), F32)],
        scratch_shapes=[pltpu.VMEM((T, ns), F32)] * 4,
        compiler_params=_params(1, 56 << 20),
    )(_in_hbm(u), _in_hbm(dy), a8, bb_re, bb_im, c_re, c_im, dsk)


def _to_segments(a):
    return a.reshape(N_SEG, SEG, a.shape[-1]).transpose(1, 0, 2).reshape(T, a.shape[-1])


def _from_segments(a):
    return a.reshape(SEG, N_SEG, a.shape[-1]).transpose(1, 0, 2).reshape(T, a.shape[-1])


def _ff_specs():
    up_spec = pl.BlockSpec((2, None, T, 128), lambda p, cb: (0, p, 0, cb))
    piece = pl.BlockSpec((None, T, 128), lambda p, cb: (p, 0, cb))
    cw_spec = pl.BlockSpec((None, 3, 128), lambda p, cb: (p, 0, cb))
    cb_spec = pl.BlockSpec((None, 1, 128), lambda p, cb: (p, 0, cb))
    return up_spec, piece, cw_spec, cb_spec


def _conv_gate(gate, w, cb, pad_ref):
    pad_ref[0:8, :] = jnp.zeros((8, 128), F32)
    pad_ref[8:8 + T, :] = gate
    g1, g2 = pad_ref[pl.ds(7, T), :], pad_ref[pl.ds(6, T), :]
    return w[0:1, :] * g2 + w[1:2, :] * g1 + w[2:3, :] * gate + cb, g1, g2


def _ff_act(up, cw4, cb4):
    def body(up_ref, w_ref, b_ref, o_ref, pad_ref):
        gc, _, _ = _conv_gate(up_ref[0].astype(F32), w_ref[...], b_ref[...], pad_ref)
        o_ref[...] = (jax.nn.silu(gc) * up_ref[1].astype(F32)).astype(o_ref.dtype)

    up_spec, piece, cw_spec, cb_spec = _ff_specs()
    n_cb = pl.cdiv(FF_PIECE, 128)
    return pl.pallas_call(
        body, name="ff_act", grid=(4, n_cb), in_specs=[up_spec, cw_spec, cb_spec], out_specs=piece,
        out_shape=_out_hbm((4, T, FF_PIECE), BF16),
        scratch_shapes=[pltpu.VMEM((T + 8, 128), F32)],
        compiler_params=_params(2, 24 << 20),
    )(_in_hbm(up.reshape(2, 4, T, FF_PIECE)), cw4, cb4)


def _ff_act_bwd(up, dact, cw4, cb4):
    def body(up_ref, da_ref, w_ref, b_ref, dup_ref, dw_ref, db_ref, pad_ref, pad2_ref):
        gate, val, w = up_ref[0].astype(F32), up_ref[1].astype(F32), w_ref[...]
        gc, g1, g2 = _conv_gate(gate, w, b_ref[...], pad_ref)
        sg = jax.nn.sigmoid(gc)
        da = da_ref[...].astype(F32)
        dup_ref[1] = (da * gc * sg).astype(dup_ref.dtype)
        dgc = da * val * sg * (1.0 + gc * (1.0 - sg))
        db_ref[...] = jnp.sum(dgc, axis=0, keepdims=True)
        dw_ref[0:1, :] = jnp.sum(dgc * g2, axis=0, keepdims=True)
        dw_ref[1:2, :] = jnp.sum(dgc * g1, axis=0, keepdims=True)
        dw_ref[2:3, :] = jnp.sum(dgc * gate, axis=0, keepdims=True)
        pad2_ref[0:T, :] = dgc
        pad2_ref[T:T + 8, :] = jnp.zeros((8, 128), F32)
        d1, d2 = pad2_ref[pl.ds(1, T), :], pad2_ref[pl.ds(2, T), :]
        dup_ref[0] = (w[2:3, :] * dgc + w[1:2, :] * d1 + w[0:1, :] * d2).astype(dup_ref.dtype)

    up_spec, piece, cw_spec, cb_spec = _ff_specs()
    n_cb = pl.cdiv(FF_PIECE, 128)
    dup, dw, db = pl.pallas_call(
        body, name="ff_act_bwd", grid=(4, n_cb), in_specs=[up_spec, piece, cw_spec, cb_spec],
        out_specs=[up_spec, cw_spec, cb_spec],
        out_shape=[_out_hbm((2, 4, T, FF_PIECE), BF16), jax.ShapeDtypeStruct((4, 3, FF_PIECE), F32),
                   jax.ShapeDtypeStruct((4, 1, FF_PIECE), F32)],
        scratch_shapes=[pltpu.VMEM((T + 8, 128), F32)] * 2,
        compiler_params=_params(2, 32 << 20),
    )(_in_hbm(up.reshape(2, 4, T, FF_PIECE)), _in_hbm(dact), cw4, cb4)
    return dup.reshape(8, T, FF_PIECE), dw, db


def _local_step(x, tgt, meta, p, wf, late_weights, send_grads, send_small):
    sm = {}
    h0 = jnp.concatenate([meta, x], axis=0)
    tgt_pad = jnp.concatenate([jnp.zeros((N_META, D), F32), tgt], axis=0)
    tabs = _rope_tables()
    row = lambda a: a.reshape(1, -1)

    xn, r0 = _rowwise("rms_mix", _rms_fwd_fn, [(h0, D, 0)], [row(p['mix_norm'])], [(D, BF16), (1, F32)], [])
    z = _mm("in_proj", xn, wf['w_in'], 'nn', TR, 640, D, F32)
    lam_re, lam_im = p['lam_re'].reshape(N_GROUPS, SSM_STATE), p['lam_im'].reshape(N_GROUPS, SSM_STATE)
    log_dt = p['log_dt'].reshape(1, N_GROUPS)
    bt_re = p['b_re'].reshape(N_GROUPS, SSM_STATE, SSM_GROUP).transpose(2, 0, 1)
    bt_im = p['b_im'].reshape(N_GROUPS, SSM_STATE, SSM_GROUP).transpose(2, 0, 1)
    c_re = p['c_re'].reshape(N_GROUPS * SSM_GROUP, SSM_STATE)
    c_im = p['c_im'].reshape(N_GROUPS * SSM_GROUP, SSM_STATE)
    a_re, a_im, bbt_re, bbt_im = _s5_prep(lam_re, lam_im, log_dt, bt_re, bt_im)
    nq, ns = N_GROUPS // GQ, GQ * SSM_STATE
    a8 = jnp.broadcast_to(jnp.stack([a_re.reshape(nq, ns), a_im.reshape(nq, ns)], 1)[:, :, None, :],
                          (nq, 2, N_SEG, ns))
    gcp = lambda t: t.transpose(1, 0, 2).reshape(N_GROUPS * SSM_GROUP, SSM_STATE)
    bb_re, bb_im = gcp(bbt_re), gcp(bbt_im)
    dsk = row(p['d_skip'])
    u_seg = _to_segments(z[:, :D_SSM])
    y_ssm = _from_segments(_s5_fwd(u_seg, a8, bb_re, bb_im, c_re, c_im, dsk))
    g, = _rowwise("gelu", _gelu_fn, [(y_ssm, D_SSM, 0)], [], [(D_SSM, BF16)], [])
    gl = _mm("glu_proj", g, wf['w_glu'], 'nn', TR, D_SSM, D_SSM, F32)
    yan, ra = _rowwise("glu_norm", _glu_fwd_fn, [(y_ssm, D_SSM, 0), (gl, D_SSM, 0)],
                       [row(p['b_glu']), row(p['out_norm_ssm'])], [(D_SSM, BF16), (1, F32)], [])
    qn, rq = _rowwise("rms_q", _rms_fwd_fn, [(z, Q_LORA, D_SSM // Q_LORA)], [row(p['q_a_norm'])],
                      [(Q_LORA, BF16), (1, F32)], [])
    kvn, rkv = _rowwise("rms_kv", _rms_fwd_fn, [(z, KV_LORA, (D_SSM + Q_LORA) // KV_LORA)], [row(p['kv_a_norm'])],
                        [(KV_LORA, BF16), (1, F32)], [])
    q = _mm("q_proj", qn, wf['w_q_b'], 'nn', T, HEAD_PAD, Q_LORA, F32, stack='b_out')
    kv = _mm("kv_proj", kvn, wf['w_kv_b'], 'nn', T, HEAD_PAD, KV_LORA, F32, stack='b_out')
    yb = _attn_fwd(q, kv, z, tabs)
    ybn, rb = _rowwise("rms_attn", _rms_fwd_fn, [(yb, D_SSM, 0)], [row(p['out_norm_attn'])],
                       [(D_SSM, BF16), (1, F32)], [])
    y = jnp.concatenate([yan, ybn], axis=1)
    wf = dict(wf)
    wf['w_out'], token, ffn_weights = late_weights(y)
    h1 = _mm("out_proj", y, wf['w_out'], 'nn', TR, 1024, D, F32, res=h0)
    xn2, r1 = _rowwise("rms_ffn", _rms_fwd_fn, [(h1, D, 0)], [row(p['ffn_norm']) + token[0:1, 0:1]],
                       [(D, BF16), (1, F32)], [])
    wf['w_up'], wf['w_down'] = ffn_weights(xn2)
    up = _mm("up_proj", xn2, wf['w_up'], 'nn', TR, FF_PIECE, D, BF16, stack='b_out')
    cw4 = wf['conv_w'].reshape(3, 4, FF_PIECE).transpose(1, 0, 2)
    cb4 = p['conv_b'].reshape(4, 1, FF_PIECE)
    act = _ff_act(up, cw4, cb4)
    wd4 = wf['w_down'].reshape(4, FF_PIECE, D)
    h2 = _mm("down_proj", act, wd4, 'nn', TR, 1024, FF_PIECE, F32, stack='ab_red', res=h1)
    dh2, sm['final_norm'], loss = _rowwise("loss", _loss_fn, [(h2, D, 0), (tgt_pad, D, 0)], [row(p['final_norm'])],
                                           [(D, F32)], [D, 128])

    big = {}
    dact = _mm("down_bwd_x", dh2, wd4, 'nt', TR, FF_PIECE, D, BF16, stack='b_out')
    g_down = _mm("down_bwd_w", act, dh2, 'tn', FF_PIECE, 512, T, BF16, stack='a_out').reshape(
        N_DEV, D_FF // N_DEV, D)
    token = send_grads(['w_down'], [g_down], 'cores')
    dup, dcw4, sm['conv_b'] = _ff_act_bwd(up, dact, cw4, cb4 + token[0:1, 0:1])
    dxn2 = _mm("up_bwd_x", dup, wf['w_up'], 'nt', TR, 1024, FF_PIECE, F32, stack='ab_red')
    g_up = _mm("up_bwd_w", dup, xn2, 'tn', FF_PIECE, 1024, T, BF16, stack='a_out')
    token = send_grads(['w_up'], [g_up], 'cores') + send_grads(['w_down'], dxn2, 'chips')
    dh1, sm['ffn_norm'] = _rowwise("rms_ffn_bwd", _rms_bwd_res_fn,
                                   [(dxn2, D, 0), (h1, D, 0), (r1, 1, 0), (dh2, D, 0)],
                                   [row(p['ffn_norm']) + token[0:1, 0:1]], [(D, F32)], [D])
    dy = _mm("out_bwd_x", dh1, wf['w_out'], 'nt', TR, 1024, D, F32)
    g_out = _mm("out_bwd_w", y, dh1, 'tn', 1024, 1024, T, BF16).reshape(N_DEV, D // N_DEV, D)
    dgl, dgd, sm['out_norm_ssm'], sm['b_glu'] = _rowwise(
        "glu_bwd", _glu_bwd_fn, [(dy, D_SSM, 0), (y_ssm, D_SSM, 0), (gl, D_SSM, 0), (ra, 1, 0)],
        [row(p['b_glu']), row(p['out_norm_ssm'])], [(D_SSM, BF16), (D_SSM, F32)], [D_SSM, D_SSM])
    dg = _mm("glu_bwd_x", dgl, wf['w_glu'], 'nt', TR, D_SSM, D_SSM, F32, res=dgd)
    g_glu = _mm("glu_bwd_w", g, dgl, 'tn', D_SSM, D_SSM, T, BF16).reshape(N_DEV, D_SSM // N_DEV, D_SSM)
    dy_ssm, = _rowwise("gelu_bwd", _gelu_bwd_fn, [(dg, D_SSM, 0), (y_ssm, D_SSM, 0)], [], [(D_SSM, F32)], [])
    token = send_grads(['w_up'], g_glu, 'chips')
    dyb, sm['out_norm_attn'] = _rowwise("rms_attn_bwd", _rms_bwd_fn, [(dy, D_SSM, 1), (yb, D_SSM, 0), (rb, 1, 0)],
                                        [row(p['out_norm_attn']) + token[0:1, 0:1]], [(D_SSM, F32)], [D_SSM])
    dq, dkv, dkpe = _attn_bwd(q, kv, z, tabs, dyb)
    g_q = _mm("q_bwd_w", qn, dq, 'tn', Q_LORA, HEAD_PAD, T, BF16, stack='b_out')
    dqn = _mm("q_bwd_x", dq, wf['w_q_b'], 'nt', T, Q_LORA, HEAD_PAD, F32, stack='ab_red')
    g_kv = _mm("kv_bwd_w", kvn, dkv, 'tn', KV_LORA, HEAD_PAD, T, BF16, stack='b_out')
    dkvn = _mm("kv_bwd_x", dkv, wf['w_kv_b'], 'nt', T, KV_LORA, HEAD_PAD, F32, stack='ab_red')
    token = send_grads(['w_out', 'w_glu', 'w_q_b', 'w_kv_b'], [g_out, g_glu, g_q, g_kv], 'both')
    dq_a, sm['q_a_norm'] = _rowwise("rms_q_bwd", _rms_bwd_fn,
                                    [(dqn, Q_LORA, 0), (z, Q_LORA, D_SSM // Q_LORA), (rq, 1, 0)],
                                    [row(p['q_a_norm'])], [(Q_LORA, BF16)], [Q_LORA])
    dkv_a, sm['kv_a_norm'] = _rowwise("rms_kv_bwd", _rms_bwd_fn,
                                      [(dkvn, KV_LORA, 0), (z, KV_LORA, (D_SSM + Q_LORA) // KV_LORA), (rkv, 1, 0)],
                                      [row(p['kv_a_norm'])], [(KV_LORA, BF16)], [KV_LORA])
    dk_pe = _kpe_bwd(dkpe, tabs)
    du_seg, dbre, dbim, dcre, dcim, da, sm['d_skip'] = _s5_bwd(
        u_seg, _to_segments(dy_ssm), a8, bb_re, bb_im, c_re, c_im, dsk + token[0:1, 0:1])
    du = _from_segments(du_seg).astype(BF16)
    cgp = lambda d: d.reshape(N_GROUPS, SSM_GROUP, SSM_STATE).transpose(1, 0, 2)
    dlam_re, dlam_im, dlog_dt, dbt_re, dbt_im = _s5_prep_bwd(
        lam_re, lam_im, log_dt, bt_re, bt_im, da[:, 0, :].reshape(N_GROUPS, SSM_STATE),
        da[:, 1, :].reshape(N_GROUPS, SSM_STATE), cgp(dbre), cgp(dbim))
    sm['lam_re'], sm['lam_im'], sm['log_dt'] = dlam_re, dlam_im, dlog_dt
    sm['b_re'], sm['b_im'] = gcp(dbt_re), gcp(dbt_im)
    sm['c_re'], sm['c_im'] = dcre, dcim
    token = send_small(sm, dcw4, loss)
    dz = jnp.concatenate([du, dq_a, dkv_a, dk_pe], axis=1)
    dxn = _mm("in_bwd_x", dz, wf['w_in'], 'nt', TR, 1024, D_IN_PAD, F32)
    big['w_in'] = _mm("in_bwd_w", xn, dz, 'tn', 1024, 640, T, BF16).reshape(N_DEV, D // N_DEV, D_IN_PAD)
    dh0, d_mix_norm = _rowwise("rms_mix_bwd", _rms_bwd_res_fn,
                               [(dxn, D, 0), (h0, D, 0), (r0, 1, 0), (dh1, D, 0)],
                               [row(p['mix_norm']) + token[0:1, 0:1]], [(D, F32)], [D])
    return dh0[N_META:], dh0[:N_META], d_mix_norm, big


def _place():
    x, y, c = lax.axis_index("x"), lax.axis_index("y"), lax.axis_index("c")
    return x, y, c, [(1 - x, y), (x, 1 - y), (1 - x, 1 - y)]


_HBM = pl.BlockSpec(memory_space=pltpu.HBM)


def _all_gather(name, arrs, chip_blocks=()):
    n, nc = len(arrs), len(chip_blocks)

    def body(*refs):
        ins, cins, outs, couts = refs[:n], refs[n:n + nc], refs[n + nc:2 * n + nc], refs[2 * n + nc:2 * (n + nc)]
        send, recv, loc, csend, crecv = refs[2 * (n + nc):]
        x, y, c, chips = _place()
        me, sib = (x, y, c), (x, y, 1 - c)

        def cp(a, k, blk, to, src=None):
            dst = outs[a].at[4 * blk[0] + 2 * blk[1] + blk[2]]
            return pltpu.make_async_remote_copy(src_ref=dst if src is None else src, dst_ref=dst,
                                                send_sem=send.at[a, k], recv_sem=recv.at[a, k],
                                                device_id=to, device_id_type=MESH)

        between_chips = [pltpu.make_async_remote_copy(src_ref=cins[a].at[2 * ch[0] + ch[1]], dst_ref=couts[a].at[j],
                                                      send_sem=csend.at[a, j], recv_sem=crecv.at[a, j],
                                                      device_id=(*ch, c), device_id_type=MESH)
                         for a in range(nc) for j, ch in enumerate(chips)]
        mine = [pltpu.make_async_copy(ins[a], outs[a].at[4 * x + 2 * y + c], loc.at[a]) for a in range(n)]
        first = []
        for a in range(n):
            first.append(cp(a, 0, me, sib, src=ins[a]))
            first += [cp(a, 1 + j, me, (*ch, c), src=ins[a]) for j, ch in enumerate(chips)]
        for f in mine + first + between_chips:
            f.start()
        passed = []
        for a in range(n):
            for j, ch in enumerate(chips):
                cp(a, 1 + j, (*ch, c), me).wait_recv()
                passed.append(cp(a, 4 + j, (*ch, c), sib))
                passed[-1].start()
        for a in range(n):
            cp(a, 0, sib, me).wait_recv()
            for j, ch in enumerate(chips):
                cp(a, 4 + j, (*ch, 1 - c), me).wait_recv()
        for f in first + passed:
            f.wait_send()
        for m in mine + between_chips:
            m.wait()

    out = pl.pallas_call(
        body, name=name, in_specs=[_HBM] * (n + nc), out_specs=[_HBM] * (n + nc),
        out_shape=[jax.ShapeDtypeStruct((N_DEV,) + a.shape, a.dtype) for a in arrs]
        + [jax.ShapeDtypeStruct((3,) + a.shape[1:], a.dtype) for a in chip_blocks],
        scratch_shapes=[pltpu.SemaphoreType.DMA((n, 7)), pltpu.SemaphoreType.DMA((n, 7)),
                        pltpu.SemaphoreType.DMA((n,)), pltpu.SemaphoreType.DMA((max(nc, 1), 3)),
                        pltpu.SemaphoreType.DMA((max(nc, 1), 3))],
    )(*arrs, *chip_blocks)
    return (out[:n], out[n:]) if nc else out


def _exchange_cores(name, arrs):
    n = len(arrs)

    def body(*refs):
        ins, outs = refs[:n], refs[n:2 * n]
        send, recv = refs[2 * n:]
        x, y, c, _ = _place()
        remote = [pltpu.make_async_remote_copy(src_ref=ins[a].at[2 * k + 1 - c], dst_ref=outs[a].at[k],
                                               send_sem=send.at[a, k], recv_sem=recv.at[a, k],
                                               device_id=(x, y, 1 - c), device_id_type=MESH)
                  for a in range(n) for k in range(4)]
        for d in remote:
            d.start()
        for d in remote:
            d.wait()

    return pl.pallas_call(
        body, name=name, in_specs=[_HBM] * n, out_specs=[_HBM] * n,
        out_shape=[jax.ShapeDtypeStruct((4,) + a.shape[1:], a.dtype) for a in arrs],
        scratch_shapes=[pltpu.SemaphoreType.DMA((n, 4))] * 2,
    )(*arrs)


_SEM = pl.BlockSpec(memory_space=pltpu.SEMAPHORE)
_EFFECT = pltpu.SideEffectType.DATAFLOW_SIDE_EFFECTING


def _split_exchange(name, srcs, land_shapes, copies_of, n_cp, after=None, all_of=None):
    n = len(srcs)
    in_place = land_shapes is None
    lands = [] if in_place else [pltpu.with_memory_space_constraint(lax.empty(s, a.dtype), pltpu.HBM)
                                 for s, a in zip(land_shapes, srcs)]
    nb = n + len(lands)

    per = 1 if all_of else n_cp
    n_sem = n * per

    def descriptors(refs, send, recv, mirror):
        src_refs, land_refs = refs[:n], refs[nb - n:nb]
        x, y, c, chips = _place()
        out = []
        for a in range(n):
            if mirror and all_of:
                whole = all_of(a, land_refs[a])
                out.append(pltpu.make_async_remote_copy(src_ref=whole, dst_ref=whole, send_sem=send[a],
                                                        recv_sem=recv[a], device_id=(x, y, 1 - c),
                                                        device_id_type=MESH))
                continue
            for k, (src, dst, to, back) in enumerate(copies_of(a, src_refs[a], land_refs[a], x, y, c, chips)):
                out.append(pltpu.make_async_remote_copy(src_ref=src, dst_ref=back if mirror else dst,
                                                        send_sem=send[a * per + k % per],
                                                        recv_sem=recv[a * per + k % per],
                                                        device_id=to, device_id_type=MESH))
        return out

    def start_body(*refs):
        first = nb + (after is not None)
        sems = refs[first:first + 2 * n_sem]
        for d in descriptors(refs, sems[:n_sem], sems[n_sem:], False):
            d.start()
        refs[-1][...] = jnp.zeros((8, 128), F32)

    hbm_like = lambda arrs: [pltpu.HBM(a.shape, a.dtype) for a in arrs]
    order = [] if after is None else [after]
    res = pl.pallas_call(
        start_body, name=name + "_start", in_specs=[_HBM] * nb + [pl.BlockSpec(memory_space=pl.ANY)] * len(order),
        out_specs=[_SEM] * (2 * n_sem) + [_HBM] * nb + [pl.BlockSpec(memory_space=pltpu.VMEM)],
        out_shape=[pltpu.SemaphoreType.DMA(())] * (2 * n_sem)
        + hbm_like(srcs) + hbm_like(lands) + [jax.ShapeDtypeStruct((8, 128), F32)],
        input_output_aliases={i: 2 * n_sem + i for i in range(nb)},
        compiler_params=pltpu.CompilerParams(has_side_effects=_EFFECT),
    )(*srcs, *lands, *order)
    sems, thru, token = res[:2 * n_sem], res[2 * n_sem:2 * n_sem + nb], res[-1]

    def wait(after):
        def wait_body(*refs):
            s = refs[nb:nb + 2 * n_sem]
            for d in descriptors(refs, s[:n_sem], s[n_sem:], True):
                d.wait_send()
                d.wait_recv()

        out = pl.pallas_call(
            wait_body, name=name + "_wait",
            in_specs=[_HBM] * nb + [_SEM] * (2 * n_sem) + [pl.BlockSpec(memory_space=pl.ANY)],
            out_specs=[_HBM] * nb, out_shape=hbm_like(srcs) + hbm_like(lands),
            input_output_aliases={i: i for i in range(nb)},
            compiler_params=pltpu.CompilerParams(has_side_effects=_EFFECT),
        )(*thru, *sems, after)
        return out[:n], out[nb - n:]

    return token, wait


def _gather_copies(a, src, land, x, y, c, chips):
    peers = [(x, y, 1 - c)] + [(*ch, c) for ch in chips]
    me = 4 * x + 2 * y + c
    return [(src, land.at[me], to, land.at[4 * to[0] + 2 * to[1] + to[2]]) for to in peers]


def _to_all_copies(a, src, land, x, y, c, chips):
    peers = [(x, y, 1 - c)] + [(*ch, cc) for ch in chips for cc in (c, 1 - c)]
    me = 4 * x + 2 * y + c
    return [(src, land.at[me], to, land.at[4 * to[0] + 2 * to[1] + to[2]]) for to in peers]


def _cores_copies(a, src, land, x, y, c, chips):
    return [(src.at[2 * k + 1 - c], land.at[k], (x, y, 1 - c), land.at[k]) for k in range(4)]


def _forward_copies(a, src, land, x, y, c, chips):
    slot = lambda ch, core: 4 * ch[0] + 2 * ch[1] + core
    return [(land.at[slot(ch, c)], land.at[slot(ch, c)], (x, y, 1 - c), land.at[slot(ch, 1 - c)]) for ch in chips]


def _chips_copies(a, src, land, x, y, c, chips):
    return [(src.at[2 * ch[0] + ch[1]], land.at[j], (*ch, c), land.at[j]) for j, ch in enumerate(chips)]


def _gather_forward(name, arrs):
    n = len(arrs)

    def body(*refs):
        ins, outs = refs[:n], refs[n:2 * n]
        send, recv = refs[2 * n:]
        x, y, c, chips = _place()
        sends, recvs = [], []
        for a in range(n):
            for j, ch in enumerate(chips):
                mine, theirs = 4 * ch[0] + 2 * ch[1] + c, 4 * ch[0] + 2 * ch[1] + 1 - c
                sends.append(pltpu.make_async_remote_copy(src_ref=ins[a].at[mine], dst_ref=outs[a].at[mine],
                                                          send_sem=send.at[a, j], recv_sem=recv.at[a, j],
                                                          device_id=(x, y, 1 - c), device_id_type=MESH))
                recvs.append(pltpu.make_async_remote_copy(src_ref=ins[a].at[theirs], dst_ref=outs[a].at[theirs],
                                                          send_sem=send.at[a, j], recv_sem=recv.at[a, j],
                                                          device_id=(x, y, 1 - c), device_id_type=MESH))
        for d in sends:
            d.start()
        for s, r in zip(sends, recvs):
            s.wait_send()
            r.wait_recv()

    return pl.pallas_call(
        body, name=name, in_specs=[_HBM] * n, out_specs=[_HBM] * n,
        out_shape=[jax.ShapeDtypeStruct(a.shape, a.dtype) for a in arrs],
        input_output_aliases={i: i for i in range(n)},
        scratch_shapes=[pltpu.SemaphoreType.DMA((n, 3))] * 2,
    )(*arrs)


def _blocks_of(r, c):
    if r * c * 4 <= (2 << 20):
        return r, c
    if r % 128 == 0:
        return 128, c
    return r, 256


def _pair_sum(name, own, got, core):
    _, r, c = got.shape
    rb, cb = _blocks_of(r, c)

    def body(s_ref, a_ref, b_ref, o_ref):
        o_ref[...] = (a_ref[...].astype(F32) + b_ref[...].astype(F32)).astype(o_ref.dtype)

    return pl.pallas_call(
        body, name=name, out_shape=_out_hbm((4, r, c), got.dtype),
        grid_spec=pltpu.PrefetchScalarGridSpec(
            num_scalar_prefetch=1, grid=(4, r // rb, c // cb),
            in_specs=[pl.BlockSpec((None, rb, cb), lambda k, i, j, s: (2 * k + s[0], i, j)),
                      pl.BlockSpec((None, rb, cb), lambda k, i, j, s: (k, i, j))],
            out_specs=pl.BlockSpec((None, rb, cb), lambda k, i, j, s: (k, i, j))),
        compiler_params=_params(3, 24 << 20),
    )(core, _in_hbm(own), _in_hbm(got))


def _adamw_math(w, g, m, v):
    m = B1 * m + (1.0 - B1) * g
    v = B2 * v + (1.0 - B2) * (g * g)
    m_hat = m / (1.0 - B1 ** STEP)
    v_hat = v / (1.0 - B2 ** STEP)
    return -LR * (m_hat / (jnp.sqrt(v_hat) + ADAM_EPS) + WD * w), m, v


def _adamw_big(name, own, got, chip, w, m, v):
    r, c = w.shape
    rb, cb = _blocks_of(r, c)

    def body(s_ref, o_ref, p_ref, w_ref, m_ref, v_ref, g_ref, d_ref, nm_ref, nv_ref):
        g = o_ref[...].astype(F32)
        for k in range(3):
            g = g + p_ref[k].astype(F32)
        g_ref[...] = g
        d_ref[...], nm_ref[...], nv_ref[...] = _adamw_math(w_ref[...], g, m_ref[...], v_ref[...])

    blk = pl.BlockSpec((rb, cb), lambda i, j, s: (i, j))
    return pl.pallas_call(
        body, name=name, out_shape=[jax.ShapeDtypeStruct((r, c), F32)] * 4,
        grid_spec=pltpu.PrefetchScalarGridSpec(
            num_scalar_prefetch=1, grid=(r // rb, c // cb),
            in_specs=[pl.BlockSpec((None, rb, cb), lambda i, j, s: (s[0], i, j)),
                      pl.BlockSpec((3, rb, cb), lambda i, j, s: (0, i, j)), blk, blk, blk],
            out_specs=[blk] * 4),
        compiler_params=_params(2, 40 << 20),
    )(chip, *map(_in_hbm, (own, got, w, m, v)))


def _adamw_multi(name, items, sums=(), row_blocks=1, own=(), dev=None):
    n, ns = len(items), len(sums)
    n_own = len(own)
    assert n_own in (0, n + ns)

    def body(*refs):
        dev_ref = refs[0] if n_own else None
        refs = refs[1:] if n_own else refs
        owns, refs = refs[:n_own], refs[n_own:]
        ins, outs = refs[:4 * n + ns], refs[4 * n + ns:]

        def total(ref, t):
            g = None
            for d in range(ref.shape[0]):
                part = jnp.where(dev_ref[0] == d, owns[t][...], ref[d]) if n_own else ref[d]
                g = part.astype(F32) if g is None else g + part.astype(F32)
            return g

        for t in range(n):
            p_ref, w_ref, m_ref, v_ref = ins[4 * t:4 * t + 4]
            g = total(p_ref, t)
            outs[4 * t][...] = g
            outs[4 * t + 1][...], outs[4 * t + 2][...], outs[4 * t + 3][...] = _adamw_math(
                w_ref[...], g, m_ref[...], v_ref[...])
        for t in range(ns):
            outs[4 * n + t][...] = total(ins[4 * n + t], n + t)

    def spec(shape, lead):
        blk = (shape[0] // row_blocks,) + tuple(shape[1:])
        nd = len(shape)
        if lead:
            return pl.BlockSpec((lead,) + blk, lambda i: (0, i) + (0,) * (nd - 1))
        return pl.BlockSpec(blk, lambda i: (i,) + (0,) * (nd - 1))

    operands, in_specs, out_specs, out_shape = [], [], [], []
    if n_own:
        operands += [dev] + list(own)
        in_specs += [pl.BlockSpec(memory_space=pltpu.SMEM)] + [spec(o.shape, 0) for o in own]
    for parts, w, m, v in items:
        assert parts.shape[1:] == w.shape == m.shape == v.shape, (name, parts.shape, w.shape)
        operands += [parts, w, m, v]
        in_specs += [spec(w.shape, parts.shape[0])] + [spec(w.shape, 0)] * 3
        out_specs += [spec(w.shape, 0)] * 4
        out_shape += [jax.ShapeDtypeStruct(w.shape, F32)] * 4
    for parts in sums:
        operands.append(parts)
        in_specs.append(spec(parts.shape[1:], parts.shape[0]))
        out_specs.append(spec(parts.shape[1:], False))
        out_shape.append(jax.ShapeDtypeStruct(parts.shape[1:], F32))
    return pl.pallas_call(
        body, name=name, grid=(row_blocks,), in_specs=in_specs, out_specs=out_specs, out_shape=out_shape,
        compiler_params=_params(1, 56 << 20),
    )(*operands)


def kernel(x, meta_tokens, mix_norm, w_in, lam_re, lam_im, log_dt, b_re, b_im, c_re, c_im, d_skip, w_glu, b_glu, q_a_norm, w_q_b, kv_a_norm, w_kv_b, out_norm_ssm, out_norm_attn, w_out, ffn_norm, w_up, conv_w, conv_b, w_down, final_norm, loss_target, m_meta_tokens, m_mix_norm, m_w_in, m_lam_re, m_lam_im, m_log_dt, m_b_re, m_b_im, m_c_re, m_c_im, m_d_skip, m_w_glu, m_b_glu, m_q_a_norm, m_w_q_b, m_kv_a_norm, m_w_kv_b, m_out_norm_ssm, m_out_norm_attn, m_w_out, m_ffn_norm, m_w_up, m_conv_w, m_conv_b, m_w_down, m_final_norm, v_meta_tokens, v_mix_norm, v_w_in, v_lam_re, v_lam_im, v_log_dt, v_b_re, v_b_im, v_c_re, v_c_im, v_d_skip, v_w_glu, v_b_glu, v_q_a_norm, v_w_q_b, v_kv_a_norm, v_w_kv_b, v_out_norm_ssm, v_out_norm_attn, v_w_out, v_ffn_norm, v_w_up, v_conv_w, v_conv_b, v_w_down, v_final_norm):
    given = dict(locals())
    w = {n: given[n] for n in WEIGHTS}
    m = {n: given['m_' + n] for n in WEIGHTS}
    v = {n: given['v_' + n] for n in WEIGHTS}
    dev = 4 * lax.axis_index("x") + 2 * lax.axis_index("y") + lax.axis_index("c")

    shard = {
        'w_in': jnp.pad(w_in[0], ((0, 0), (0, D_IN_PAD - D_IN))),
        'w_glu': w_glu[0],
        'w_q_b': jnp.pad(w_q_b[0], ((0, 0), (0, HEAD_PAD - QK_NOPE - QK_ROPE))),
        'w_kv_b': w_kv_b[0],
        'w_out': w_out[0],
        'w_up': w_up[0],
        'w_down': w_down[0],
    }
    core = lax.axis_index("c").astype(jnp.int32).reshape(1)
    chip = (2 * lax.axis_index("x") + lax.axis_index("y")).astype(jnp.int32).reshape(1)
    early, late = ['w_in', 'w_glu', 'w_q_b', 'w_kv_b'], ['w_out', 'w_up', 'w_down']
    shard = {n: a.astype(BF16) for n, a in shard.items()}

    gathered = _all_gather("gather_early", [shard[n] for n in early]
                           + [meta_tokens, jnp.pad(conv_w[0], ((0, 5), (0, 0)))])
    wf = dict(zip(early, gathered[:len(early)]))
    for n in ('w_in', 'w_glu'):
        wf[n] = wf[n].reshape(-1, wf[n].shape[-1])
    meta = gathered[-2].transpose(1, 0, 2).reshape(N_META, D)
    wf['conv_w'] = gathered[-1][:, :3].transpose(1, 0, 2).reshape(3, D_FF)
    gather_token, wait_late = _split_exchange("gather_late", [shard[n] for n in late],
                                              [(N_DEV,) + shard[n].shape for n in late], _gather_copies, 4,
                                              after=gathered[0])

    def late_weights(after):
        mine, landed = wait_late(after)
        full = [lax.dynamic_update_slice(a, s[None], (dev, 0, 0)) for s, a in zip(mine, landed)]
        full_out, = _gather_forward("gather_forward_w_out", full[:1])
        token, wait_ffn = _split_exchange("gather_forward_ffn", full[1:], None, _forward_copies, 3, after=full_out)

        def ffn_weights(after):
            full_up, full_down = wait_ffn(after)[0]
            return full_up, full_down.reshape(-1, D)

        return full_out.reshape(-1, D), token, ffn_weights

    pending, between_cores = [], {}

    def send_grads(names, arg, stage):
        if stage == 'cores':
            token, between_cores[names[0]] = _split_exchange(
                "reduce_cores_" + names[0], arg, [(4,) + a.shape[1:] for a in arg], _cores_copies, 4)
            return token
        mine, got = between_cores.pop(names[0])(arg) if stage == 'chips' else (
            arg, _exchange_cores("reduce_cores_" + names[0], arg))
        sums = [_pair_sum("sum_cores_" + n, a, g, core) for n, a, g in zip(names, mine, got)]
        token, wait = _split_exchange("reduce_chips_" + names[0], sums, [(3,) + s.shape[1:] for s in sums],
                                      _chips_copies, 3)
        pending.append((names, wait))
        return token

    small_sent = []
    s5_bc = ['b_re', 'b_im', 'c_re', 'c_im']

    def send_small(sm, dcw4, loss_part):
        names = [n for n in SMALL if n in sm]
        arrs = [sm[n].reshape(-1, 128).astype(BF16) if n in s5_bc else sm[n] for n in names] + [dcw4, loss_part]
        token, wait = _split_exchange("gather_small", arrs, [(N_DEV,) + a.shape for a in arrs], _to_all_copies, 7,
                                      all_of=lambda a, land: land.at[pl.ds(0, N_DEV - 1)])
        small_sent.append((names, wait))
        return token

    small = {n: w[n] for n in SMALL}
    small['mix_norm'] = mix_norm + gather_token[0:1, 0:1]
    grad_x, dmeta, d_mix_norm, big = _local_step(x[0], loss_target[0], meta, small, wf, late_weights, send_grads,
                                                  send_small)
    grads, deltas, new_m, new_v = {}, {}, {}, {}

    def keep(n, outs):
        grads[n], deltas[n], new_m[n], new_v[n] = (o.reshape(w[n].shape) for o in outs)

    rest = list(big)
    from_core = _exchange_cores("reduce_cores", [big[n] for n in rest])
    chip_sums = [_pair_sum("sum_cores_" + n, big[n], got, core) for n, got in zip(rest, from_core)]
    last = chip_sums + [d_mix_norm, dmeta]
    n_big = len(chip_sums)
    _, wait_last = _split_exchange(
        "reduce_last", last, [(3,) + s.shape[1:] for s in chip_sums] + [(N_DEV,) + a.shape for a in last[n_big:]],
        lambda a, *at: _chips_copies(a, *at) if a < n_big else _to_all_copies(a, *at), 7,
        all_of=lambda a, land: land if a < n_big else land.at[pl.ds(0, N_DEV - 1)])

    def update(n, own, got):
        if n == 'w_up':
            outs = _adamw_big("adamw_" + n, own, got, chip, w[n][0].T, m[n][0].T, v[n][0].T)
            return keep(n, [o.T for o in outs])
        cols = w[n].shape[2]
        keep(n, _adamw_big("adamw_" + n, own[:, :, :cols], got[:, :, :cols], chip, w[n][0], m[n][0], v[n][0]))

    for names, wait in pending:
        for n, own, got in zip(names, *wait(chip_sums[0])):
            update(n, own, got)
    updated = [new_v[n].reshape(-1)[:1] for names, _ in pending for n in names]
    sent, landed = wait_last(functools.reduce(jnp.add, updated))
    for n, own, got in zip(rest, sent, landed):
        update(n, own, got)

    device = dev.astype(jnp.int32).reshape(1)
    outs = _adamw_multi("adamw_last", [(landed[n_big], mix_norm, m['mix_norm'], v['mix_norm'])],
                        sums=[landed[n_big + 1]], own=sent[n_big:], dev=device)
    keep('mix_norm', outs[:4])
    g_meta = outs[4]
    names, wait = small_sent[0]
    mine, landed = wait(grads['mix_norm'])
    parts, own = dict(zip(names + ['conv_w', 'loss'], landed)), dict(zip(names + ['conv_w', 'loss'], mine))
    wide = [n for n in names if n not in s5_bc]
    summed = ['conv_w', 'loss'] + s5_bc
    outs = _adamw_multi("adamw_small",
                        [(parts[n],) + tuple(t[n].reshape(own[n].shape) for t in (w, m, v)) for n in wide],
                        sums=[parts[n] for n in summed], own=[own[n] for n in wide + summed], dev=device)
    for i, n in enumerate(wide):
        keep(n, outs[4 * i:4 * i + 4])
    g_cw4, loss = outs[-6:-4]
    rows_gc = lambda a, n: (a.reshape(N_GROUPS, SSM_STATE, SSM_GROUP).transpose(0, 2, 1) if n[0] == 'b' else a
                            ).reshape(N_GROUPS * SSM_GROUP, SSM_STATE)
    outs = _adamw_multi("adamw_s5_bc", [(g.reshape(1, N_GROUPS * SSM_GROUP, SSM_STATE),)
                                        + tuple(rows_gc(t[n], n) for t in (w, m, v))
                                        for n, g in zip(s5_bc, outs[-4:])])
    for i, n in enumerate(s5_bc):
        back = outs[4 * i:4 * i + 4]
        if n[0] == 'b':
            back = [o.reshape(N_GROUPS, SSM_GROUP, SSM_STATE).transpose(0, 2, 1) for o in back]
        keep(n, back)

    g_meta = lax.dynamic_slice(g_meta, (0, dev * (D // N_DEV)), (N_META, D // N_DEV))
    g_conv = lax.dynamic_slice(g_cw4.transpose(1, 0, 2).reshape(3, D_FF), (0, dev * (D_FF // N_DEV)),
                               (3, D_FF // N_DEV))
    rows8 = lambda a: jnp.pad(a.reshape(3, D_FF // N_DEV), ((0, 5), (0, 0)))
    outs = _adamw_multi("adamw_cols", [(g_meta[None], meta_tokens, m['meta_tokens'], v['meta_tokens']),
                                       (rows8(g_conv)[None], rows8(conv_w), rows8(m['conv_w']), rows8(v['conv_w']))])
    keep('meta_tokens', outs[:4])
    keep('conv_w', [o[:3] for o in outs[4:]])

    return (loss[0, 0], grad_x[None], *[grads[n] for n in WEIGHTS], *[deltas[n] for n in WEIGHTS],
            *[new_m[n] for n in WEIGHTS], *[new_v[n] for n in WEIGHTS])
```

```python
import functools
import math

import jax
import jax.numpy as jnp
from jax import lax
from jax.experimental import pallas as pl
from jax.experimental.pallas import tpu as pltpu

F32, BF16 = jnp.float32, jnp.bfloat16

N_DEV = 8
N_META, SEQ, D = 16, 2048, 2048
T = N_META + SEQ
TR = 688
EPS = 1e-6
D_SSM, Q_LORA, KV_LORA, QK_ROPE = 1024, 512, 256, 64
D_IN = D_SSM + Q_LORA + KV_LORA + QK_ROPE
D_IN_PAD = 1920
N_HEADS, QK_NOPE, V_HEAD = 8, 128, 128
HEAD_PAD = 256
SM_SCALE = 1.0 / math.sqrt(QK_NOPE + QK_ROPE)
CHUNK = 64
N_GROUPS, SSM_GROUP, SSM_STATE = 64, 16, 64
N_SEG = 8
SEG = T // N_SEG
GQ = 8
D_FF = 5504
FF_PIECE = 1376
ROPE_BASE = 10000.0
LR, B1, B2, ADAM_EPS, WD, STEP = 0.001, 0.9, 0.999, 1e-08, 0.01, 10
VMEM_CAP = 60 * 1024 * 1024
MESH = pl.DeviceIdType.MESH

WEIGHTS = ['meta_tokens', 'mix_norm', 'w_in', 'lam_re', 'lam_im', 'log_dt', 'b_re', 'b_im', 'c_re', 'c_im',
           'd_skip', 'w_glu', 'b_glu', 'q_a_norm', 'w_q_b', 'kv_a_norm', 'w_kv_b', 'out_norm_ssm',
           'out_norm_attn', 'w_out', 'ffn_norm', 'w_up', 'conv_w', 'conv_b', 'w_down', 'final_norm']
BIG = ['w_in', 'w_glu', 'w_q_b', 'w_kv_b', 'w_out', 'w_up', 'w_down']
SMALL = [n for n in WEIGHTS if n not in BIG and n not in ('meta_tokens', 'conv_w')]


def _nbytes(shape, dtype):
    return math.prod(shape) * jnp.dtype(dtype).itemsize


def _in_hbm(a):
    return pltpu.with_memory_space_constraint(a, pltpu.HBM) if a.size * a.dtype.itemsize >= (1 << 20) else a


def _out_hbm(shape, dtype):
    big = math.prod(shape) * jnp.dtype(dtype).itemsize >= (1 << 20)
    return pltpu.HBM(shape, dtype) if big else jax.ShapeDtypeStruct(shape, dtype)


def _params(sem, need):
    return pltpu.CompilerParams(dimension_semantics=("arbitrary",) * sem,
                                vmem_limit_bytes=int(min(VMEM_CAP, max(need, 16 * 1024 * 1024))))


_DIMS = {'nn': (((1,), (0,)), ((), ())), 'nt': (((1,), (1,)), ((), ())), 'tn': (((0,), (0,)), ((), ()))}


def _mm(name, a, b, mode, tm, tn, tk, out_dtype, stack=None, res=None):
    sa, sb, so = stack in ('a_out', 'ab_red'), stack in ('b_out', 'ab_red'), stack in ('a_out', 'b_out')
    nj = a.shape[0] if sa else (b.shape[0] if sb else 1)
    a2, b2 = a.shape[-2:], b.shape[-2:]
    if mode == 'nn':
        (m, k), (k2, n) = a2, b2
    elif mode == 'nt':
        (m, k), (n, k2) = a2, b2
    else:
        (k, m), (k2, n) = a2, b2
    assert k == k2 and m % tm == 0 and n % tn == 0 and k % tk == 0, (name, a.shape, b.shape)
    n_jo, n_jr, n_k = (nj if so else 1), (nj if stack == 'ab_red' else 1), k // tk
    grid = (n_jo, m // tm, n // tn, n_jr, n_k)
    a_blk, a_idx = ((tk, tm), lambda i, j, kk: (kk, i)) if mode == 'tn' else ((tm, tk), lambda i, j, kk: (i, kk))
    b_blk, b_idx = ((tn, tk), lambda i, j, kk: (j, kk)) if mode == 'nt' else ((tk, tn), lambda i, j, kk: (kk, j))

    def spec(blk, idx, stacked, on_out):
        if not stacked:
            return pl.BlockSpec(blk, lambda jo, i, j, jr, kk: idx(i, j, kk))
        return pl.BlockSpec((None,) + blk, lambda jo, i, j, jr, kk: ((jo if on_out else jr),) + idx(i, j, kk))

    o_idx = lambda i, j, kk: (i, j)
    in_specs = [spec(a_blk, a_idx, sa, stack == 'a_out'), spec(b_blk, b_idx, sb, stack == 'b_out')]
    operands = [a, b]
    if res is not None:
        in_specs.append(spec((tm, tn), o_idx, False, False))
        operands.append(res)
    n_red = n_jr * n_k
    dims = _DIMS[mode]

    def body(*refs):
        a_ref, b_ref = refs[0], refs[1]
        res_ref = refs[2] if res is not None else None
        o_ref = refs[3] if res is not None else refs[2]
        part = lax.dot_general(a_ref[...].astype(BF16), b_ref[...].astype(BF16), dims,
                               preferred_element_type=F32)

        def finish(total):
            if res_ref is not None:
                total = total + res_ref[...]
            o_ref[...] = total.astype(o_ref.dtype)

        if n_red == 1:
            finish(part)
        else:
            acc_ref = refs[-1]
            step = pl.program_id(3) * n_k + pl.program_id(4)

            @pl.when(step == 0)
            def _():
                acc_ref[...] = part

            @pl.when(step > 0)
            def _():
                acc_ref[...] += part

            @pl.when(step == n_red - 1)
            def _():
                finish(acc_ref[...])

    out_shape = ((nj,) if so else ()) + (m, n)
    need = 2 * (_nbytes(a_blk, a.dtype) + _nbytes(b_blk, b.dtype) + _nbytes((tm, tn), out_dtype)
                + (_nbytes((tm, tn), F32) if res is not None else 0)) + 3 * _nbytes((tm, tn), F32) + (4 << 20)
    return pl.pallas_call(
        body, name=name, grid=grid, in_specs=in_specs, out_specs=spec((tm, tn), o_idx, so, True),
        out_shape=_out_hbm(out_shape, out_dtype),
        scratch_shapes=[pltpu.VMEM((tm, tn), F32)] if n_red > 1 else [],
        compiler_params=_params(5, need),
    )(*map(_in_hbm, operands))


def _rowwise(name, fn, row_ins, const_ins, row_outs, acc_outs, tr=TR):
    n_ri, n_ci, n_ro, n_ao = len(row_ins), len(const_ins), len(row_outs), len(acc_outs)

    def body(*refs):
        ri, ci = refs[:n_ri], refs[n_ri:n_ri + n_ci]
        ro, ao = refs[n_ri + n_ci:n_ri + n_ci + n_ro], refs[n_ri + n_ci + n_ro:]
        i = pl.program_id(0)
        outs = fn(i * tr, *[r[...] for r in ri], *[r[...] for r in ci])
        for r, v in zip(ro, outs[:n_ro]):
            r[...] = v.astype(r.dtype)
        if n_ao:
            @pl.when(i == 0)
            def _():
                for r, v in zip(ao, outs[n_ro:]):
                    r[...] = v

            @pl.when(i > 0)
            def _():
                for r, v in zip(ao, outs[n_ro:]):
                    r[...] += v

    in_specs = [pl.BlockSpec((tr, w), functools.partial(lambda cb, i: (i, cb), cb)) for _, w, cb in row_ins]
    in_specs += [pl.BlockSpec(c.shape, functools.partial(lambda nd, i: (0,) * nd, c.ndim)) for c in const_ins]
    out_specs = [pl.BlockSpec((tr, w), lambda i: (i, 0)) for w, _ in row_outs]
    out_specs += [pl.BlockSpec((1, w), lambda i: (0, 0)) for w in acc_outs]
    out_shape = [_out_hbm((T, w), dt) for w, dt in row_outs]
    out_shape += [jax.ShapeDtypeStruct((1, w), F32) for w in acc_outs]
    need = 2 * (sum(_nbytes((tr, w), a.dtype) for a, w, _ in row_ins) + sum(_nbytes((tr, w), dt) for w, dt in row_outs))
    need = 3 * need + (8 << 20)
    return pl.pallas_call(
        body, name=name, grid=(T // tr,), in_specs=in_specs, out_specs=out_specs, out_shape=out_shape,
        compiler_params=_params(1, need),
    )(*[_in_hbm(a) for a, _, _ in row_ins], *const_ins)


def _rms_fwd_fn(row0, h, g):
    h = h.astype(F32)
    r = lax.rsqrt(jnp.mean(h * h, axis=-1, keepdims=True) + EPS)
    return (h * r) * g, r


def _rms_bwd(dxn, h, r, g):
    xhat = h * r
    dxg = dxn * g
    dh = r * (dxg - xhat * jnp.mean(dxg * xhat, axis=-1, keepdims=True))
    return dh, jnp.sum(dxn * xhat, axis=0, keepdims=True)


def _rms_bwd_fn(row0, dxn, h, r, g):
    return _rms_bwd(dxn.astype(F32), h.astype(F32), r, g)


def _rms_bwd_res_fn(row0, dxn, h, r, res, g):
    dh, dg = _rms_bwd(dxn.astype(F32), h.astype(F32), r, g)
    return dh + res, dg


def _gelu_fn(row0, y):
    return (jax.nn.gelu(y),)


def _glu_fwd_fn(row0, y, gl, b, gain):
    ya = jax.nn.gelu(y) * jax.nn.sigmoid(gl + b)
    return _rms_fwd_fn(row0, ya, gain)


def _glu_bwd_fn(row0, dyan, y, gl, ra, b, gain):
    g = jax.nn.gelu(y)
    s = jax.nn.sigmoid(gl + b)
    dya, dgain = _rms_bwd(dyan, g * s, ra, gain)
    dgl = dya * g * s * (1.0 - s)
    return dgl, dya * s, dgain, jnp.sum(dgl, axis=0, keepdims=True)


def _gelu_bwd_fn(row0, dg, y):
    return (jax.vjp(jax.nn.gelu, y)[1](dg)[0],)


def _loss_fn(row0, h, tgt, gain):
    r = lax.rsqrt(jnp.mean(h * h, axis=-1, keepdims=True) + EPS)
    xhat = h * r
    rows = row0 + lax.broadcasted_iota(jnp.int32, h.shape, 0)
    err = jnp.where(rows >= N_META, xhat * gain - tgt, 0.0)
    loss = jnp.full((1, 128), jnp.sum(err * err) * (0.5 / D), F32)
    dh, dgain = _rms_bwd(err * (1.0 / D), h, r, gain)
    return dh, dgain, loss


def _rope_tables():
    pos = jnp.arange(T, dtype=jnp.int32)
    inv_freq = 1.0 / (ROPE_BASE ** (jnp.arange(0, QK_ROPE, 2, dtype=F32) / QK_ROPE))
    ang = pos.astype(F32)[:, None] * inv_freq[None, :]
    cos, sin, z32, z64 = jnp.cos(ang), jnp.sin(ang), jnp.zeros((T, 32), F32), jnp.zeros((T, 64), F32)
    return (jnp.concatenate([cos, cos, z64], 1), jnp.concatenate([-sin, z32, z64], 1),
            jnp.concatenate([z32, sin, z64], 1))


def _rot(x, cc, s1, s2):
    return x * cc + pltpu.roll(x, 96, 1) * s1 + pltpu.roll(x, 32, 1) * s2


def _derot(d, cc, s1, s2):
    return d * cc + pltpu.roll(d * s1, 32, 1) + pltpu.roll(d * s2, 96, 1)


def _chunk_of(pos):
    return jnp.where(pos < N_META, 0, 1 + ((pos - N_META) >> 6))


def _key_len(t):
    last_chunk = 1 + ((t + 1) * TR - 1 - N_META) // CHUNK
    return min(T, pl.cdiv(N_META + last_chunk * CHUNK, 128) * 128)


def _attn_probs(t, q_ref, kv_ref, kpe_ref, tq, tk):
    kl = _key_len(t)
    q = q_ref[...]
    qn = q[:, :QK_NOPE].astype(BF16)
    qp = _rot(q[:, QK_NOPE:], tq[0][...], tq[1][...], tq[2][...]).astype(BF16)
    kn = kv_ref[:kl, :QK_NOPE].astype(BF16)
    v = kv_ref[:kl, QK_NOPE:].astype(BF16)
    kp = _rot(kpe_ref[:kl, :], tk[0][:kl, :], tk[1][:kl, :], tk[2][:kl, :]).astype(BF16)
    s = lax.dot_general(qn, kn, _DIMS['nt'], preferred_element_type=F32)
    s = s + lax.dot_general(qp, kp, _DIMS['nt'], preferred_element_type=F32)
    qc = _chunk_of(t * TR + lax.broadcasted_iota(jnp.int32, (TR, 1), 0))
    kc = _chunk_of(lax.broadcasted_iota(jnp.int32, (1, kl), 1))
    s = jnp.where(kc <= qc, s * SM_SCALE, jnp.finfo(F32).min)
    p = jnp.exp(s - jnp.max(s, axis=-1, keepdims=True))
    p = p * (1.0 / jnp.sum(p, axis=-1, keepdims=True))
    return qn, qp, kn, kp, v, p


def _per_q_tile(fn):
    for t in range(T // TR):
        pl.when(pl.program_id(1) == t)(functools.partial(fn, t))


def _attn_specs(z, tabs):
    q_spec = pl.BlockSpec((None, TR, HEAD_PAD), lambda h, i: (h, i, 0))
    kv_spec = pl.BlockSpec((None, T, HEAD_PAD), lambda h, i: (h, 0, 0))
    kpe_spec = pl.BlockSpec((T, 128), lambda h, i: (0, (D_IN - QK_ROPE) // 128))
    tq = [pl.BlockSpec((TR, 128), lambda h, i: (i, 0))] * 3
    tk = [pl.BlockSpec((T, 128), lambda h, i: (0, 0))] * 3
    return [q_spec, kv_spec, kpe_spec] + tq + tk


def _attn_fwd(q, kv, z, tabs):
    def body(q_ref, kv_ref, kpe_ref, c1, c2, c3, k1, k2, k3, o_ref):
        def tile(t):
            _, _, _, _, v, p = _attn_probs(t, q_ref, kv_ref, kpe_ref, (c1, c2, c3), (k1, k2, k3))
            o_ref[...] = jnp.dot(p.astype(BF16), v, preferred_element_type=F32)

        _per_q_tile(tile)

    return pl.pallas_call(
        body, name="attn_fwd", grid=(N_HEADS, T // TR), in_specs=_attn_specs(z, tabs),
        out_specs=pl.BlockSpec((TR, V_HEAD), lambda h, i: (i, h)),
        out_shape=_out_hbm((T, N_HEADS * V_HEAD), F32),
        compiler_params=_params(2, 48 << 20),
    )(_in_hbm(q), _in_hbm(kv), _in_hbm(z), *tabs, *tabs)


def _attn_bwd(q, kv, z, tabs, dyb):
    def body(q_ref, kv_ref, kpe_ref, c1, c2, c3, k1, k2, k3, do_ref, dq_ref, dkv_ref, dkpe_ref):
        @pl.when(pl.program_id(1) == 0)
        def _():
            dkv_ref[...] = jnp.zeros(dkv_ref.shape, F32)
            dkpe_ref[...] = jnp.zeros(dkpe_ref.shape, F32)

        def tile(t):
            kl = _key_len(t)
            qn, qp, kn, kp, v, p = _attn_probs(t, q_ref, kv_ref, kpe_ref, (c1, c2, c3), (k1, k2, k3))
            do = do_ref[...].astype(BF16)
            dp = lax.dot_general(do, v, _DIMS['nt'], preferred_element_type=F32)
            ds = (p * (dp - jnp.sum(p * dp, axis=-1, keepdims=True)) * SM_SCALE).astype(BF16)
            pb = p.astype(BF16)
            dq_ref[:, :QK_NOPE] = jnp.dot(ds, kn, preferred_element_type=F32).astype(dq_ref.dtype)
            dqp = _derot(jnp.dot(ds, kp, preferred_element_type=F32), c1[...], c2[...], c3[...])
            dq_ref[:, QK_NOPE:] = dqp.astype(dq_ref.dtype)
            dkv_ref[:kl, :QK_NOPE] += lax.dot_general(ds, qn, _DIMS['tn'], preferred_element_type=F32)
            dkv_ref[:kl, QK_NOPE:] += lax.dot_general(pb, do, _DIMS['tn'], preferred_element_type=F32)
            dkpe_ref[:kl, :] += lax.dot_general(ds, qp, _DIMS['tn'], preferred_element_type=F32)

        _per_q_tile(tile)

    return pl.pallas_call(
        body, name="attn_bwd", grid=(N_HEADS, T // TR),
        in_specs=_attn_specs(z, tabs) + [pl.BlockSpec((TR, V_HEAD), lambda h, i: (i, h))],
        out_specs=[pl.BlockSpec((None, TR, HEAD_PAD), lambda h, i: (h, i, 0)),
                   pl.BlockSpec((None, T, HEAD_PAD), lambda h, i: (h, 0, 0)),
                   pl.BlockSpec((None, T, 128), lambda h, i: (h, 0, 0))],
        out_shape=[_out_hbm((N_HEADS, T, HEAD_PAD), BF16), _out_hbm((N_HEADS, T, HEAD_PAD), F32),
                   _out_hbm((N_HEADS, T, 128), F32)],
        compiler_params=_params(2, 56 << 20),
    )(_in_hbm(q), _in_hbm(kv), _in_hbm(z), *tabs, *tabs, _in_hbm(dyb))


def _kpe_bwd(dkpe, tabs):
    def body(d_ref, c1, c2, c3, o_ref):
        d = d_ref[0]
        for h in range(1, N_HEADS):
            d = d + d_ref[h]
        o_ref[...] = _derot(d, c1[...], c2[...], c3[...]).astype(o_ref.dtype)

    tile = pl.BlockSpec((TR, 128), lambda i: (i, 0))
    return pl.pallas_call(
        body, name="kpe_bwd", grid=(T // TR,),
        in_specs=[pl.BlockSpec((N_HEADS, TR, 128), lambda i: (0, i, 0)), tile, tile, tile], out_specs=tile,
        out_shape=jax.ShapeDtypeStruct((T, 128), BF16), compiler_params=_params(1, 16 << 20),
    )(_in_hbm(dkpe), *tabs)


def _s5_prep_math(lam_re, lam_im, log_dt, bt_re, bt_im):
    dt = jnp.exp(log_dt)
    mag = jnp.exp(lam_re * dt)
    a_re, a_im = mag * jnp.cos(lam_im * dt), mag * jnp.sin(lam_im * dt)
    den = lam_re * lam_re + lam_im * lam_im
    w_re = ((a_re - 1.0) * lam_re + a_im * lam_im) / den
    w_im = (a_im * lam_re - (a_re - 1.0) * lam_im) / den
    return a_re, a_im, w_re[None] * bt_re - w_im[None] * bt_im, w_re[None] * bt_im + w_im[None] * bt_re


def _eye_groups():
    return (lax.broadcasted_iota(jnp.int32, (N_GROUPS, N_GROUPS), 0)
            == lax.broadcasted_iota(jnp.int32, (N_GROUPS, N_GROUPS), 1)).astype(F32)


def _row_to_col(row):
    return jnp.sum(_eye_groups() * row, axis=1, keepdims=True)


def _s5_prep(lam_re, lam_im, log_dt, bt_re, bt_im):
    def body(lr, li, ld, br, bi, ar, ai, bbr, bbi):
        ar[...], ai[...], bbr[...], bbi[...] = _s5_prep_math(lr[...], li[...], _row_to_col(ld[...]), br[...], bi[...])

    gp, cgp = jax.ShapeDtypeStruct((N_GROUPS, SSM_STATE), F32), jax.ShapeDtypeStruct(bt_re.shape, F32)
    return pl.pallas_call(body, name="s5_prep", out_shape=[gp, gp, cgp, cgp])(lam_re, lam_im, log_dt, bt_re, bt_im)


def _s5_prep_bwd(lam_re, lam_im, log_dt, bt_re, bt_im, da_re, da_im, dbb_re, dbb_im):
    def body(lr, li, ld, br, bi, dar, dai, dbr, dbi, o1, o2, o3, o4, o5):
        _, vjp = jax.vjp(_s5_prep_math, lr[...], li[...], _row_to_col(ld[...]), br[...], bi[...])
        o1[...], o2[...], dld, o4[...], o5[...] = vjp((dar[...], dai[...], dbr[...], dbi[...]))
        o3[...] = jnp.sum(_eye_groups() * dld, axis=0, keepdims=True)

    ins = (lam_re, lam_im, log_dt, bt_re, bt_im)
    return pl.pallas_call(body, name="s5_prep_bwd", out_shape=[jax.ShapeDtypeStruct(a.shape, F32) for a in ins])(
        *ins, da_re, da_im, dbb_re, dbb_im)


def _cmul(ar, ai, br, bi):
    return ar * br - ai * bi, ar * bi + ai * br


def _seg_rows(i):
    return pl.ds(pl.multiple_of(i * N_SEG, N_SEG), N_SEG)


def _scan(xr, xi, ar, ai, reverse):
    zero = jnp.zeros(ar.shape, F32)

    def local(j, carry):
        rows = _seg_rows(SEG - 1 - j if reverse else j)
        nr, ni = _cmul(ar, ai, *carry)
        nr, ni = nr + xr[rows, :], ni + xi[rows, :]
        xr[rows, :], xi[rows, :] = nr, ni
        return nr, ni

    er, ei = lax.fori_loop(0, SEG, local, (zero, zero))
    pr, pi = ar, ai
    for _ in range(8):
        pr, pi = _cmul(pr, pi, pr, pi)
    pr, pi = _cmul(*_cmul(pr, pi, ar, ai), ar, ai)
    row = lax.broadcasted_iota(jnp.int32, ar.shape, 0)
    edge, shift = (N_SEG - 1, N_SEG - 1) if reverse else (0, 1)
    hr, hi = zero, zero
    for _ in range(N_SEG - 1):
        tr_, ti_ = _cmul(pr, pi, hr, hi)
        hr = jnp.where(row == edge, 0.0, pltpu.roll(tr_ + er, shift, 0))
        hi = jnp.where(row == edge, 0.0, pltpu.roll(ti_ + ei, shift, 0))

    def fix(j, carry):
        rows = _seg_rows(SEG - 1 - j if reverse else j)
        xr[rows, :] += carry[0]
        xi[rows, :] += carry[1]
        return _cmul(ar, ai, *carry)

    lax.fori_loop(0, SEG, fix, _cmul(ar, ai, hr, hi))
    return hr, hi


def _split(x):
    hi = x.astype(BF16)
    return hi, (x - hi.astype(F32)).astype(BF16)


def _dot2(xs, ys, mode='nn'):
    d = lambda p, q: lax.dot_general(p, q, _DIMS[mode], preferred_element_type=F32)
    return d(xs[0], ys[0]) + d(xs[1], ys[0])


def _group_masks():
    ns = GQ * SSM_STATE
    rep = (lax.broadcasted_iota(jnp.int32, (SSM_STATE, ns), 1) % SSM_STATE
           == lax.broadcasted_iota(jnp.int32, (SSM_STATE, ns), 0)).astype(BF16)
    own = (lax.broadcasted_iota(jnp.int32, (128, ns), 0) // SSM_GROUP
           == lax.broadcasted_iota(jnp.int32, (128, ns), 1) // SSM_STATE)
    return rep, own


def _block_diag(x, rep, own):
    dot = lambda p: jnp.dot(p, rep, preferred_element_type=F32)
    return tuple(jnp.where(own, dot(part), 0.0).astype(BF16) for part in _split(x))


def _block_rows(full, rep, own):
    masked = _split(jnp.where(own, full, 0.0))
    dot = lambda p: lax.dot_general(p, rep, _DIMS['nt'], preferred_element_type=F32)
    return dot(masked[0]) + dot(masked[1])


def _s5_specs():
    col = pl.BlockSpec((T, 128), lambda q: (0, q))
    a_spec = pl.BlockSpec((None, 2, N_SEG, GQ * SSM_STATE), lambda q: (q, 0, 0, 0))
    rows = pl.BlockSpec((128, SSM_STATE), lambda q: (q, 0))
    d_spec = pl.BlockSpec((1, 128), lambda q: (0, q))
    return col, a_spec, rows, d_spec


def _s5_fwd(u, a8, bb_re, bb_im, c_re, c_im, dsk):
    def body(u_ref, a_ref, bre_ref, bim_ref, cre_ref, cim_ref, d_ref, y_ref, hre, him):
        u_ = u_ref[...]
        us = _split(u_)
        rep, own = _group_masks()
        hre[...] = _dot2(us, _block_diag(bre_ref[...], rep, own))
        him[...] = _dot2(us, _block_diag(bim_ref[...], rep, own))
        _scan(hre, him, a_ref[0], a_ref[1], reverse=False)
        dot = lambda h, c: lax.dot_general(h[...].astype(BF16), _block_diag(c[...], rep, own)[0], _DIMS['nt'],
                                           preferred_element_type=F32)
        y_ref[...] = dot(hre, cre_ref) - dot(him, cim_ref) + d_ref[...] * u_

    col, a_spec, rows, d_spec = _s5_specs()
    return pl.pallas_call(
        body, name="s5_fwd", grid=(N_GROUPS // GQ,), in_specs=[col, a_spec, rows, rows, rows, rows, d_spec],
        out_specs=col, out_shape=_out_hbm((T, D_SSM), F32),
        scratch_shapes=[pltpu.VMEM((T, GQ * SSM_STATE), F32)] * 2,
        compiler_params=_params(1, 40 << 20),
    )(_in_hbm(u), a8, bb_re, bb_im, c_re, c_im, dsk)


def _s5_bwd(u, dy, a8, bb_re, bb_im, c_re, c_im, dsk):
    def body(u_ref, dy_ref, a_ref, bre_ref, bim_ref, cre_ref, cim_ref, d_ref,
             du_ref, dbre_ref, dbim_ref, dcre_ref, dcim_ref, da_ref, dd_ref, hre, him, lre, lim):
        u_, dy_ = u_ref[...], dy_ref[...]
        us, dys = _split(u_), _split(dy_)
        ar, ai = a_ref[0], a_ref[1]
        rep, own = _group_masks()
        bres, bims = _block_diag(bre_ref[...], rep, own), _block_diag(bim_ref[...], rep, own)
        hre[...] = _dot2(us, bres)
        him[...] = _dot2(us, bims)
        h0r, h0i = _scan(hre, him, ar, ai, reverse=False)
        lre[...] = _dot2(dys, _block_diag(cre_ref[...], rep, own))
        lim[...] = -_dot2(dys, _block_diag(cim_ref[...], rep, own))
        _scan(lre, lim, ar, -ai, reverse=True)

        def acc_da(i, carry):
            lr, li = lre[_seg_rows(i), :], lim[_seg_rows(i), :]
            pr, pi = hre[_seg_rows(i - 1), :], him[_seg_rows(i - 1), :]
            return carry[0] + lr * pr + li * pi, carry[1] + li * pr - lr * pi

        lr, li = lre[_seg_rows(0), :], lim[_seg_rows(0), :]
        dar, dai = lax.fori_loop(1, SEG, acc_da, (lr * h0r + li * h0i, li * h0r - lr * h0i))
        da_ref[0:1, :] = jnp.sum(dar, axis=0, keepdims=True)
        da_ref[1:2, :] = jnp.sum(dai, axis=0, keepdims=True)
        dot = lambda p, q, mode: lax.dot_general(p, q, _DIMS[mode], preferred_element_type=F32)
        lr1, li1 = lre[...].astype(BF16), lim[...].astype(BF16)
        du_ref[...] = dy_ * d_ref[...] + dot(lr1, bres[0], 'nt') + dot(li1, bims[0], 'nt')
        dbre_ref[...] = _block_rows(dot(us[0], lr1, 'tn'), rep, own)
        dbim_ref[...] = _block_rows(dot(us[0], li1, 'tn'), rep, own)
        dcre_ref[...] = _block_rows(dot(dys[0], hre[...].astype(BF16), 'tn'), rep, own)
        dcim_ref[...] = -_block_rows(dot(dys[0], him[...].astype(BF16), 'tn'), rep, own)
        dd_ref[...] = jnp.sum(dy_ * u_, axis=0, keepdims=True)

    col, a_spec, rows, d_spec = _s5_specs()
    nq, ns = N_GROUPS // GQ, GQ * SSM_STATE
    per_row = jax.ShapeDtypeStruct((N_GROUPS * SSM_GROUP, SSM_STATE), F32)
    return pl.pallas_call(
        body, name="s5_bwd", grid=(nq,),
        in_specs=[col, col, a_spec, rows, rows, rows, rows, d_spec],
        out_specs=[col, rows, rows, rows, rows, pl.BlockSpec((None, 2, ns), lambda q: (q, 0, 0)), d_spec],
        out_shape=[_out_hbm((T, D_SSM), F32), per_row, per_row, per_row, per_row,
                   jax.ShapeDtypeStruct((nq, 2, ns), F32), jax.ShapeDtypeStruct((1, D_SSM), F32)],
        scratch_shapes=[pltpu.VMEM((T, ns), F32)] * 4,
        compiler_params=_params(1, 56 << 20),
    )(_in_hbm(u), _in_hbm(dy), a8, bb_re, bb_im, c_re, c_im, dsk)


def _to_segments(a):
    return a.reshape(N_SEG, SEG, a.shape[-1]).transpose(1, 0, 2).reshape(T, a.shape[-1])


def _from_segments(a):
    return a.reshape(SEG, N_SEG, a.shape[-1]).transpose(1, 0, 2).reshape(T, a.shape[-1])


def _ff_specs():
    up_spec = pl.BlockSpec((2, None, T, 128), lambda p, cb: (0, p, 0, cb))
    piece = pl.BlockSpec((None, T, 128), lambda p, cb: (p, 0, cb))
    cw_spec = pl.BlockSpec((None, 3, 128), lambda p, cb: (p, 0, cb))
    cb_spec = pl.BlockSpec((None, 1, 128), lambda p, cb: (p, 0, cb))
    return up_spec, piece, cw_spec, cb_spec


def _conv_gate(gate, w, cb, pad_ref):
    pad_ref[0:8, :] = jnp.zeros((8, 128), F32)
    pad_ref[8:8 + T, :] = gate
    g1, g2 = pad_ref[pl.ds(7, T), :], pad_ref[pl.ds(6, T), :]
    return w[0:1, :] * g2 + w[1:2, :] * g1 + w[2:3, :] * gate + cb, g1, g2


def _ff_act(up, cw4, cb4):
    def body(up_ref, w_ref, b_ref, o_ref, pad_ref):
        gc, _, _ = _conv_gate(up_ref[0].astype(F32), w_ref[...], b_ref[...], pad_ref)
        o_ref[...] = (jax.nn.silu(gc) * up_ref[1].astype(F32)).astype(o_ref.dtype)

    up_spec, piece, cw_spec, cb_spec = _ff_specs()
    n_cb = pl.cdiv(FF_PIECE, 128)
    return pl.pallas_call(
        body, name="ff_act", grid=(4, n_cb), in_specs=[up_spec, cw_spec, cb_spec], out_specs=piece,
        out_shape=_out_hbm((4, T, FF_PIECE), BF16),
        scratch_shapes=[pltpu.VMEM((T + 8, 128), F32)],
        compiler_params=_params(2, 24 << 20),
    )(_in_hbm(up.reshape(2, 4, T, FF_PIECE)), cw4, cb4)


def _ff_act_bwd(up, dact, cw4, cb4):
    def body(up_ref, da_ref, w_ref, b_ref, dup_ref, dw_ref, db_ref, pad_ref, pad2_ref):
        gate, val, w = up_ref[0].astype(F32), up_ref[1].astype(F32), w_ref[...]
        gc, g1, g2 = _conv_gate(gate, w, b_ref[...], pad_ref)
        sg = jax.nn.sigmoid(gc)
        da = da_ref[...].astype(F32)
        dup_ref[1] = (da * gc * sg).astype(dup_ref.dtype)
        dgc = da * val * sg * (1.0 + gc * (1.0 - sg))
        db_ref[...] = jnp.sum(dgc, axis=0, keepdims=True)
        dw_ref[0:1, :] = jnp.sum(dgc * g2, axis=0, keepdims=True)
        dw_ref[1:2, :] = jnp.sum(dgc * g1, axis=0, keepdims=True)
        dw_ref[2:3, :] = jnp.sum(dgc * gate, axis=0, keepdims=True)
        pad2_ref[0:T, :] = dgc
        pad2_ref[T:T + 8, :] = jnp.zeros((8, 128), F32)
        d1, d2 = pad2_ref[pl.ds(1, T), :], pad2_ref[pl.ds(2, T), :]
        dup_ref[0] = (w[2:3, :] * dgc + w[1:2, :] * d1 + w[0:1, :] * d2).astype(dup_ref.dtype)

    up_spec, piece, cw_spec, cb_spec = _ff_specs()
    n_cb = pl.cdiv(FF_PIECE, 128)
    dup, dw, db = pl.pallas_call(
        body, name="ff_act_bwd", grid=(4, n_cb), in_specs=[up_spec, piece, cw_spec, cb_spec],
        out_specs=[up_spec, cw_spec, cb_spec],
        out_shape=[_out_hbm((2, 4, T, FF_PIECE), BF16), jax.ShapeDtypeStruct((4, 3, FF_PIECE), F32),
                   jax.ShapeDtypeStruct((4, 1, FF_PIECE), F32)],
        scratch_shapes=[pltpu.VMEM((T + 8, 128), F32)] * 2,
        compiler_params=_params(2, 32 << 20),
    )(_in_hbm(up.reshape(2, 4, T, FF_PIECE)), _in_hbm(dact), cw4, cb4)
    return dup.reshape(8, T, FF_PIECE), dw, db


def _local_step(x, tgt, meta, p, wf, late_weights, send_grads, send_small):
    sm = {}
    h0 = jnp.concatenate([meta, x], axis=0)
    tgt_pad = jnp.concatenate([jnp.zeros((N_META, D), F32), tgt], axis=0)
    tabs = _rope_tables()
    row = lambda a: a.reshape(1, -1)

    xn, r0 = _rowwise("rms_mix", _rms_fwd_fn, [(h0, D, 0)], [row(p['mix_norm'])], [(D, BF16), (1, F32)], [])
    z = _mm("in_proj", xn, wf['w_in'], 'nn', TR, 640, D, F32)
    lam_re, lam_im = p['lam_re'].reshape(N_GROUPS, SSM_STATE), p['lam_im'].reshape(N_GROUPS, SSM_STATE)
    log_dt = p['log_dt'].reshape(1, N_GROUPS)
    bt_re = p['b_re'].reshape(N_GROUPS, SSM_STATE, SSM_GROUP).transpose(2, 0, 1)
    bt_im = p['b_im'].reshape(N_GROUPS, SSM_STATE, SSM_GROUP).transpose(2, 0, 1)
    c_re = p['c_re'].reshape(N_GROUPS * SSM_GROUP, SSM_STATE)
    c_im = p['c_im'].reshape(N_GROUPS * SSM_GROUP, SSM_STATE)
    a_re, a_im, bbt_re, bbt_im = _s5_prep(lam_re, lam_im, log_dt, bt_re, bt_im)
    nq, ns = N_GROUPS // GQ, GQ * SSM_STATE
    a8 = jnp.broadcast_to(jnp.stack([a_re.reshape(nq, ns), a_im.reshape(nq, ns)], 1)[:, :, None, :],
                          (nq, 2, N_SEG, ns))
    gcp = lambda t: t.transpose(1, 0, 2).reshape(N_GROUPS * SSM_GROUP, SSM_STATE)
    bb_re, bb_im = gcp(bbt_re), gcp(bbt_im)
    dsk = row(p['d_skip'])
    u_seg = _to_segments(z[:, :D_SSM])
    y_ssm = _from_segments(_s5_fwd(u_seg, a8, bb_re, bb_im, c_re, c_im, dsk))
    g, = _rowwise("gelu", _gelu_fn, [(y_ssm, D_SSM, 0)], [], [(D_SSM, BF16)], [])
    gl = _mm("glu_proj", g, wf['w_glu'], 'nn', TR, D_SSM, D_SSM, F32)
    yan, ra = _rowwise("glu_norm", _glu_fwd_fn, [(y_ssm, D_SSM, 0), (gl, D_SSM, 0)],
                       [row(p['b_glu']), row(p['out_norm_ssm'])], [(D_SSM, BF16), (1, F32)], [])
    qn, rq = _rowwise("rms_q", _rms_fwd_fn, [(z, Q_LORA, D_SSM // Q_LORA)], [row(p['q_a_norm'])],
                      [(Q_LORA, BF16), (1, F32)], [])
    kvn, rkv = _rowwise("rms_kv", _rms_fwd_fn, [(z, KV_LORA, (D_SSM + Q_LORA) // KV_LORA)], [row(p['kv_a_norm'])],
                        [(KV_LORA, BF16), (1, F32)], [])
    q = _mm("q_proj", qn, wf['w_q_b'], 'nn', T, HEAD_PAD, Q_LORA, F32, stack='b_out')
    kv = _mm("kv_proj", kvn, wf['w_kv_b'], 'nn', T, HEAD_PAD, KV_LORA, F32, stack='b_out')
    yb = _attn_fwd(q, kv, z, tabs)
    ybn, rb = _rowwise("rms_attn", _rms_fwd_fn, [(yb, D_SSM, 0)], [row(p['out_norm_attn'])],
                       [(D_SSM, BF16), (1, F32)], [])
    y = jnp.concatenate([yan, ybn], axis=1)
    wf = dict(wf)
    wf['w_out'], token, ffn_weights = late_weights(y)
    h1 = _mm("out_proj", y, wf['w_out'], 'nn', TR, 1024, D, F32, res=h0)
    xn2, r1 = _rowwise("rms_ffn", _rms_fwd_fn, [(h1, D, 0)], [row(p['ffn_norm']) + token[0:1, 0:1]],
                       [(D, BF16), (1, F32)], [])
    wf['w_up'], wf['w_down'] = ffn_weights(xn2)
    up = _mm("up_proj", xn2, wf['w_up'], 'nn', TR, FF_PIECE, D, BF16, stack='b_out')
    cw4 = wf['conv_w'].reshape(3, 4, FF_PIECE).transpose(1, 0, 2)
    cb4 = p['conv_b'].reshape(4, 1, FF_PIECE)
    act = _ff_act(up, cw4, cb4)
    wd4 = wf['w_down'].reshape(4, FF_PIECE, D)
    h2 = _mm("down_proj", act, wd4, 'nn', TR, 1024, FF_PIECE, F32, stack='ab_red', res=h1)
    dh2, sm['final_norm'], loss = _rowwise("loss", _loss_fn, [(h2, D, 0), (tgt_pad, D, 0)], [row(p['final_norm'])],
                                           [(D, F32)], [D, 128])

    big = {}
    dact = _mm("down_bwd_x", dh2, wd4, 'nt', TR, FF_PIECE, D, BF16, stack='b_out')
    g_down = _mm("down_bwd_w", act, dh2, 'tn', FF_PIECE, 512, T, BF16, stack='a_out').reshape(
        N_DEV, D_FF // N_DEV, D)
    token = send_grads(['w_down'], [g_down], 'cores')
    dup, dcw4, sm['conv_b'] = _ff_act_bwd(up, dact, cw4, cb4 + token[0:1, 0:1])
    dxn2 = _mm("up_bwd_x", dup, wf['w_up'], 'nt', TR, 1024, FF_PIECE, F32, stack='ab_red')
    g_up = _mm("up_bwd_w", dup, xn2, 'tn', FF_PIECE, 1024, T, BF16, stack='a_out')
    token = send_grads(['w_up'], [g_up], 'cores') + send_grads(['w_down'], dxn2, 'chips')
    dh1, sm['ffn_norm'] = _rowwise("rms_ffn_bwd", _rms_bwd_res_fn,
                                   [(dxn2, D, 0), (h1, D, 0), (r1, 1, 0), (dh2, D, 0)],
                                   [row(p['ffn_norm']) + token[0:1, 0:1]], [(D, F32)], [D])
    dy = _mm("out_bwd_x", dh1, wf['w_out'], 'nt', TR, 1024, D, F32)
    g_out = _mm("out_bwd_w", y, dh1, 'tn', 1024, 1024, T, BF16).reshape(N_DEV, D // N_DEV, D)
    dgl, dgd, sm['out_norm_ssm'], sm['b_glu'] = _rowwise(
        "glu_bwd", _glu_bwd_fn, [(dy, D_SSM, 0), (y_ssm, D_SSM, 0), (gl, D_SSM, 0), (ra, 1, 0)],
        [row(p['b_glu']), row(p['out_norm_ssm'])], [(D_SSM, BF16), (D_SSM, F32)], [D_SSM, D_SSM])
    dg = _mm("glu_bwd_x", dgl, wf['w_glu'], 'nt', TR, D_SSM, D_SSM, F32, res=dgd)
    g_glu = _mm("glu_bwd_w", g, dgl, 'tn', D_SSM, D_SSM, T, BF16).reshape(N_DEV, D_SSM // N_DEV, D_SSM)
    dy_ssm, = _rowwise("gelu_bwd", _gelu_bwd_fn, [(dg, D_SSM, 0), (y_ssm, D_SSM, 0)], [], [(D_SSM, F32)], [])
    token = send_grads(['w_up'], g_glu, 'chips')
    dyb, sm['out_norm_attn'] = _rowwise("rms_attn_bwd", _rms_bwd_fn, [(dy, D_SSM, 1), (yb, D_SSM, 0), (rb, 1, 0)],
                                        [row(p['out_norm_attn']) + token[0:1, 0:1]], [(D_SSM, F32)], [D_SSM])
    dq, dkv, dkpe = _attn_bwd(q, kv, z, tabs, dyb)
    g_q = _mm("q_bwd_w", qn, dq, 'tn', Q_LORA, HEAD_PAD, T, BF16, stack='b_out')
    dqn = _mm("q_bwd_x", dq, wf['w_q_b'], 'nt', T, Q_LORA, HEAD_PAD, F32, stack='ab_red')
    g_kv = _mm("kv_bwd_w", kvn, dkv, 'tn', KV_LORA, HEAD_PAD, T, BF16, stack='b_out')
    dkvn = _mm("kv_bwd_x", dkv, wf['w_kv_b'], 'nt', T, KV_LORA, HEAD_PAD, F32, stack='ab_red')
    token = send_grads(['w_out', 'w_glu', 'w_q_b', 'w_kv_b'], [g_out, g_glu, g_q, g_kv], 'both')
    dq_a, sm['q_a_norm'] = _rowwise("rms_q_bwd", _rms_bwd_fn,
                                    [(dqn, Q_LORA, 0), (z, Q_LORA, D_SSM // Q_LORA), (rq, 1, 0)],
                                    [row(p['q_a_norm'])], [(Q_LORA, BF16)], [Q_LORA])
    dkv_a, sm['kv_a_norm'] = _rowwise("rms_kv_bwd", _rms_bwd_fn,
                                      [(dkvn, KV_LORA, 0), (z, KV_LORA, (D_SSM + Q_LORA) // KV_LORA), (rkv, 1, 0)],
                                      [row(p['kv_a_norm'])], [(KV_LORA, BF16)], [KV_LORA])
    dk_pe = _kpe_bwd(dkpe, tabs)
    du_seg, dbre, dbim, dcre, dcim, da, sm['d_skip'] = _s5_bwd(
        u_seg, _to_segments(dy_ssm), a8, bb_re, bb_im, c_re, c_im, dsk + token[0:1, 0:1])
    du = _from_segments(du_seg).astype(BF16)
    cgp = lambda d: d.reshape(N_GROUPS, SSM_GROUP, SSM_STATE).transpose(1, 0, 2)
    dlam_re, dlam_im, dlog_dt, dbt_re, dbt_im = _s5_prep_bwd(
        lam_re, lam_im, log_dt, bt_re, bt_im, da[:, 0, :].reshape(N_GROUPS, SSM_STATE),
        da[:, 1, :].reshape(N_GROUPS, SSM_STATE), cgp(dbre), cgp(dbim))
    sm['lam_re'], sm['lam_im'], sm['log_dt'] = dlam_re, dlam_im, dlog_dt
    sm['b_re'], sm['b_im'] = gcp(dbt_re), gcp(dbt_im)
    sm['c_re'], sm['c_im'] = dcre, dcim
    token = send_small(sm, dcw4, loss)
    dz = jnp.concatenate([du, dq_a, dkv_a, dk_pe], axis=1)
    dxn = _mm("in_bwd_x", dz, wf['w_in'], 'nt', TR, 1024, D_IN_PAD, F32)
    big['w_in'] = _mm("in_bwd_w", xn, dz, 'tn', 1024, 640, T, BF16).reshape(N_DEV, D // N_DEV, D_IN_PAD)
    dh0, d_mix_norm = _rowwise("rms_mix_bwd", _rms_bwd_res_fn,
                               [(dxn, D, 0), (h0, D, 0), (r0, 1, 0), (dh1, D, 0)],
                               [row(p['mix_norm']) + token[0:1, 0:1]], [(D, F32)], [D])
    return dh0[N_META:], dh0[:N_META], d_mix_norm, big


def _place():
    x, y, c = lax.axis_index("x"), lax.axis_index("y"), lax.axis_index("c")
    return x, y, c, [(1 - x, y), (x, 1 - y), (1 - x, 1 - y)]


_HBM = pl.BlockSpec(memory_space=pltpu.HBM)


def _all_gather(name, arrs, chip_blocks=()):
    n, nc = len(arrs), len(chip_blocks)

    def body(*refs):
        ins, cins, outs, couts = refs[:n], refs[n:n + nc], refs[n + nc:2 * n + nc], refs[2 * n + nc:2 * (n + nc)]
        send, recv, loc, csend, crecv = refs[2 * (n + nc):]
        x, y, c, chips = _place()
        me, sib = (x, y, c), (x, y, 1 - c)

        def cp(a, k, blk, to, src=None):
            dst = outs[a].at[4 * blk[0] + 2 * blk[1] + blk[2]]
            return pltpu.make_async_remote_copy(src_ref=dst if src is None else src, dst_ref=dst,
                                                send_sem=send.at[a, k], recv_sem=recv.at[a, k],
                                                device_id=to, device_id_type=MESH)

        between_chips = [pltpu.make_async_remote_copy(src_ref=cins[a].at[2 * ch[0] + ch[1]], dst_ref=couts[a].at[j],
                                                      send_sem=csend.at[a, j], recv_sem=crecv.at[a, j],
                                                      device_id=(*ch, c), device_id_type=MESH)
                         for a in range(nc) for j, ch in enumerate(chips)]
        mine = [pltpu.make_async_copy(ins[a], outs[a].at[4 * x + 2 * y + c], loc.at[a]) for a in range(n)]
        first = []
        for a in range(n):
            first.append(cp(a, 0, me, sib, src=ins[a]))
            first += [cp(a, 1 + j, me, (*ch, c), src=ins[a]) for j, ch in enumerate(chips)]
        for f in mine + first + between_chips:
            f.start()
        passed = []
        for a in range(n):
            for j, ch in enumerate(chips):
                cp(a, 1 + j, (*ch, c), me).wait_recv()
                passed.append(cp(a, 4 + j, (*ch, c), sib))
                passed[-1].start()
        for a in range(n):
            cp(a, 0, sib, me).wait_recv()
            for j, ch in enumerate(chips):
                cp(a, 4 + j, (*ch, 1 - c), me).wait_recv()
        for f in first + passed:
            f.wait_send()
        for m in mine + between_chips:
            m.wait()

    out = pl.pallas_call(
        body, name=name, in_specs=[_HBM] * (n + nc), out_specs=[_HBM] * (n + nc),
        out_shape=[jax.ShapeDtypeStruct((N_DEV,) + a.shape, a.dtype) for a in arrs]
        + [jax.ShapeDtypeStruct((3,) + a.shape[1:], a.dtype) for a in chip_blocks],
        scratch_shapes=[pltpu.SemaphoreType.DMA((n, 7)), pltpu.SemaphoreType.DMA((n, 7)),
                        pltpu.SemaphoreType.DMA((n,)), pltpu.SemaphoreType.DMA((max(nc, 1), 3)),
                        pltpu.SemaphoreType.DMA((max(nc, 1), 3))],
    )(*arrs, *chip_blocks)
    return (out[:n], out[n:]) if nc else out


def _exchange_cores(name, arrs):
    n = len(arrs)

    def body(*refs):
        ins, outs = refs[:n], refs[n:2 * n]
        send, recv = refs[2 * n:]
        x, y, c, _ = _place()
        remote = [pltpu.make_async_remote_copy(src_ref=ins[a].at[2 * k + 1 - c], dst_ref=outs[a].at[k],
                                               send_sem=send.at[a, k], recv_sem=recv.at[a, k],
                                               device_id=(x, y, 1 - c), device_id_type=MESH)
                  for a in range(n) for k in range(4)]
        for d in remote:
            d.start()
        for d in remote:
            d.wait()

    return pl.pallas_call(
        body, name=name, in_specs=[_HBM] * n, out_specs=[_HBM] * n,
        out_shape=[jax.ShapeDtypeStruct((4,) + a.shape[1:], a.dtype) for a in arrs],
        scratch_shapes=[pltpu.SemaphoreType.DMA((n, 4))] * 2,
    )(*arrs)


_SEM = pl.BlockSpec(memory_space=pltpu.SEMAPHORE)
_EFFECT = pltpu.SideEffectType.DATAFLOW_SIDE_EFFECTING


def _split_exchange(name, srcs, land_shapes, copies_of, n_cp, after=None, all_of=None):
    n = len(srcs)
    in_place = land_shapes is None
    lands = [] if in_place else [pltpu.with_memory_space_constraint(lax.empty(s, a.dtype), pltpu.HBM)
                                 for s, a in zip(land_shapes, srcs)]
    nb = n + len(lands)

    per = 1 if all_of else n_cp
    n_sem = n * per

    def descriptors(refs, send, recv, mirror):
        src_refs, land_refs = refs[:n], refs[nb - n:nb]
        x, y, c, chips = _place()
        out = []
        for a in range(n):
            if mirror and all_of:
                whole = all_of(a, land_refs[a])
                out.append(pltpu.make_async_remote_copy(src_ref=whole, dst_ref=whole, send_sem=send[a],
                                                        recv_sem=recv[a], device_id=(x, y, 1 - c),
                                                        device_id_type=MESH))
                continue
            for k, (src, dst, to, back) in enumerate(copies_of(a, src_refs[a], land_refs[a], x, y, c, chips)):
                out.append(pltpu.make_async_remote_copy(src_ref=src, dst_ref=back if mirror else dst,
                                                        send_sem=send[a * per + k % per],
                                                        recv_sem=recv[a * per + k % per],
                                                        device_id=to, device_id_type=MESH))
        return out

    def start_body(*refs):
        first = nb + (after is not None)
        sems = refs[first:first + 2 * n_sem]
        for d in descriptors(refs, sems[:n_sem], sems[n_sem:], False):
            d.start()
        refs[-1][...] = jnp.zeros((8, 128), F32)

    hbm_like = lambda arrs: [pltpu.HBM(a.shape, a.dtype) for a in arrs]
    order = [] if after is None else [after]
    res = pl.pallas_call(
        start_body, name=name + "_start", in_specs=[_HBM] * nb + [pl.BlockSpec(memory_space=pl.ANY)] * len(order),
        out_specs=[_SEM] * (2 * n_sem) + [_HBM] * nb + [pl.BlockSpec(memory_space=pltpu.VMEM)],
        out_shape=[pltpu.SemaphoreType.DMA(())] * (2 * n_sem)
        + hbm_like(srcs) + hbm_like(lands) + [jax.ShapeDtypeStruct((8, 128), F32)],
        input_output_aliases={i: 2 * n_sem + i for i in range(nb)},
        compiler_params=pltpu.CompilerParams(has_side_effects=_EFFECT),
    )(*srcs, *lands, *order)
    sems, thru, token = res[:2 * n_sem], res[2 * n_sem:2 * n_sem + nb], res[-1]

    def wait(after):
        def wait_body(*refs):
            s = refs[nb:nb + 2 * n_sem]
            for d in descriptors(refs, s[:n_sem], s[n_sem:], True):
                d.wait_send()
                d.wait_recv()

        out = pl.pallas_call(
            wait_body, name=name + "_wait",
            in_specs=[_HBM] * nb + [_SEM] * (2 * n_sem) + [pl.BlockSpec(memory_space=pl.ANY)],
            out_specs=[_HBM] * nb, out_shape=hbm_like(srcs) + hbm_like(lands),
            input_output_aliases={i: i for i in range(nb)},
            compiler_params=pltpu.CompilerParams(has_side_effects=_EFFECT),
        )(*thru, *sems, after)
        return out[:n], out[nb - n:]

    return token, wait


def _gather_copies(a, src, land, x, y, c, chips):
    peers = [(x, y, 1 - c)] + [(*ch, c) for ch in chips]
    me = 4 * x + 2 * y + c
    return [(src, land.at[me], to, land.at[4 * to[0] + 2 * to[1] + to[2]]) for to in peers]


def _to_all_copies(a, src, land, x, y, c, chips):
    peers = [(x, y, 1 - c)] + [(*ch, cc) for ch in chips for cc in (c, 1 - c)]
    me = 4 * x + 2 * y + c
    return [(src, land.at[me], to, land.at[4 * to[0] + 2 * to[1] + to[2]]) for to in peers]


def _cores_copies(a, src, land, x, y, c, chips):
    return [(src.at[2 * k + 1 - c], land.at[k], (x, y, 1 - c), land.at[k]) for k in range(4)]


def _forward_copies(a, src, land, x, y, c, chips):
    slot = lambda ch, core: 4 * ch[0] + 2 * ch[1] + core
    return [(land.at[slot(ch, c)], land.at[slot(ch, c)], (x, y, 1 - c), land.at[slot(ch, 1 - c)]) for ch in chips]


def _chips_copies(a, src, land, x, y, c, chips):
    return [(src.at[2 * ch[0] + ch[1]], land.at[j], (*ch, c), land.at[j]) for j, ch in enumerate(chips)]


def _gather_forward(name, arrs):
    n = len(arrs)

    def body(*refs):
        ins, outs = refs[:n], refs[n:2 * n]
        send, recv = refs[2 * n:]
        x, y, c, chips = _place()
        sends, recvs = [], []
        for a in range(n):
            for j, ch in enumerate(chips):
                mine, theirs = 4 * ch[0] + 2 * ch[1] + c, 4 * ch[0] + 2 * ch[1] + 1 - c
                sends.append(pltpu.make_async_remote_copy(src_ref=ins[a].at[mine], dst_ref=outs[a].at[mine],
                                                          send_sem=send.at[a, j], recv_sem=recv.at[a, j],
                                                          device_id=(x, y, 1 - c), device_id_type=MESH))
                recvs.append(pltpu.make_async_remote_copy(src_ref=ins[a].at[theirs], dst_ref=outs[a].at[theirs],
                                                          send_sem=send.at[a, j], recv_sem=recv.at[a, j],
                                                          device_id=(x, y, 1 - c), device_id_type=MESH))
        for d in sends:
            d.start()
        for s, r in zip(sends, recvs):
            s.wait_send()
            r.wait_recv()

    return pl.pallas_call(
        body, name=name, in_specs=[_HBM] * n, out_specs=[_HBM] * n,
        out_shape=[jax.ShapeDtypeStruct(a.shape, a.dtype) for a in arrs],
        input_output_aliases={i: i for i in range(n)},
        scratch_shapes=[pltpu.SemaphoreType.DMA((n, 3))] * 2,
    )(*arrs)


def _blocks_of(r, c):
    if r * c * 4 <= (2 << 20):
        return r, c
    if r % 128 == 0:
        return 128, c
    return r, 256


def _pair_sum(name, own, got, core):
    _, r, c = got.shape
    rb, cb = _blocks_of(r, c)

    def body(s_ref, a_ref, b_ref, o_ref):
        o_ref[...] = (a_ref[...].astype(F32) + b_ref[...].astype(F32)).astype(o_ref.dtype)

    return pl.pallas_call(
        body, name=name, out_shape=_out_hbm((4, r, c), got.dtype),
        grid_spec=pltpu.PrefetchScalarGridSpec(
            num_scalar_prefetch=1, grid=(4, r // rb, c // cb),
            in_specs=[pl.BlockSpec((None, rb, cb), lambda k, i, j, s: (2 * k + s[0], i, j)),
                      pl.BlockSpec((None, rb, cb), lambda k, i, j, s: (k, i, j))],
            out_specs=pl.BlockSpec((None, rb, cb), lambda k, i, j, s: (k, i, j))),
        compiler_params=_params(3, 24 << 20),
    )(core, _in_hbm(own), _in_hbm(got))


def _adamw_math(w, g, m, v):
    m = B1 * m + (1.0 - B1) * g
    v = B2 * v + (1.0 - B2) * (g * g)
    m_hat = m / (1.0 - B1 ** STEP)
    v_hat = v / (1.0 - B2 ** STEP)
    return -LR * (m_hat / (jnp.sqrt(v_hat) + ADAM_EPS) + WD * w), m, v


def _adamw_big(name, own, got, chip, w, m, v, behind):
    r, c = w.shape
    rb, cb = _blocks_of(r, c)

    def body(s_ref, o_ref, p_ref, w_ref, m_ref, v_ref, behind_ref, g_ref, d_ref, nm_ref, nv_ref):
        g = o_ref[...].astype(F32)
        for k in range(3):
            g = g + p_ref[k].astype(F32)
        g_ref[...] = g
        d_ref[...], nm_ref[...], nv_ref[...] = _adamw_math(w_ref[...], g, m_ref[...], v_ref[...])

    blk = pl.BlockSpec((rb, cb), lambda i, j, s: (i, j))
    return pl.pallas_call(
        body, name=name, out_shape=[jax.ShapeDtypeStruct((r, c), F32)] * 4,
        grid_spec=pltpu.PrefetchScalarGridSpec(
            num_scalar_prefetch=1, grid=(r // rb, c // cb),
            in_specs=[pl.BlockSpec((None, rb, cb), lambda i, j, s: (s[0], i, j)),
                      pl.BlockSpec((3, rb, cb), lambda i, j, s: (0, i, j)), blk, blk, blk,
                      pl.BlockSpec(memory_space=pl.ANY)],
            out_specs=[blk] * 4),
        compiler_params=_params(2, 40 << 20),
    )(chip, *map(_in_hbm, (own, got, w, m, v)), behind)


def _adamw_multi(name, items, sums=(), row_blocks=1, own=(), dev=None):
    n, ns = len(items), len(sums)
    n_own = len(own)
    assert n_own in (0, n + ns)

    def body(*refs):
        dev_ref = refs[0] if n_own else None
        refs = refs[1:] if n_own else refs
        owns, refs = refs[:n_own], refs[n_own:]
        ins, outs = refs[:4 * n + ns], refs[4 * n + ns:]

        def total(ref, t):
            g = None
            for d in range(ref.shape[0]):
                part = jnp.where(dev_ref[0] == d, owns[t][...], ref[d]) if n_own else ref[d]
                g = part.astype(F32) if g is None else g + part.astype(F32)
            return g

        for t in range(n):
            p_ref, w_ref, m_ref, v_ref = ins[4 * t:4 * t + 4]
            g = total(p_ref, t)
            outs[4 * t][...] = g
            outs[4 * t + 1][...], outs[4 * t + 2][...], outs[4 * t + 3][...] = _adamw_math(
                w_ref[...], g, m_ref[...], v_ref[...])
        for t in range(ns):
            outs[4 * n + t][...] = total(ins[4 * n + t], n + t)

    def spec(shape, lead):
        blk = (shape[0] // row_blocks,) + tuple(shape[1:])
        nd = len(shape)
        if lead:
            return pl.BlockSpec((lead,) + blk, lambda i: (0, i) + (0,) * (nd - 1))
        return pl.BlockSpec(blk, lambda i: (i,) + (0,) * (nd - 1))

    operands, in_specs, out_specs, out_shape = [], [], [], []
    if n_own:
        operands += [dev] + list(own)
        in_specs += [pl.BlockSpec(memory_space=pltpu.SMEM)] + [spec(o.shape, 0) for o in own]
    for parts, w, m, v in items:
        assert parts.shape[1:] == w.shape == m.shape == v.shape, (name, parts.shape, w.shape)
        operands += [parts, w, m, v]
        in_specs += [spec(w.shape, parts.shape[0])] + [spec(w.shape, 0)] * 3
        out_specs += [spec(w.shape, 0)] * 4
        out_shape += [jax.ShapeDtypeStruct(w.shape, F32)] * 4
    for parts in sums:
        operands.append(parts)
        in_specs.append(spec(parts.shape[1:], parts.shape[0]))
        out_specs.append(spec(parts.shape[1:], False))
        out_shape.append(jax.ShapeDtypeStruct(parts.shape[1:], F32))
    return pl.pallas_call(
        body, name=name, grid=(row_blocks,), in_specs=in_specs, out_specs=out_specs, out_shape=out_shape,
        compiler_params=_params(1, 56 << 20),
    )(*operands)


def kernel(x, meta_tokens, mix_norm, w_in, lam_re, lam_im, log_dt, b_re, b_im, c_re, c_im, d_skip, w_glu, b_glu, q_a_norm, w_q_b, kv_a_norm, w_kv_b, out_norm_ssm, out_norm_attn, w_out, ffn_norm, w_up, conv_w, conv_b, w_down, final_norm, loss_target, m_meta_tokens, m_mix_norm, m_w_in, m_lam_re, m_lam_im, m_log_dt, m_b_re, m_b_im, m_c_re, m_c_im, m_d_skip, m_w_glu, m_b_glu, m_q_a_norm, m_w_q_b, m_kv_a_norm, m_w_kv_b, m_out_norm_ssm, m_out_norm_attn, m_w_out, m_ffn_norm, m_w_up, m_conv_w, m_conv_b, m_w_down, m_final_norm, v_meta_tokens, v_mix_norm, v_w_in, v_lam_re, v_lam_im, v_log_dt, v_b_re, v_b_im, v_c_re, v_c_im, v_d_skip, v_w_glu, v_b_glu, v_q_a_norm, v_w_q_b, v_kv_a_norm, v_w_kv_b, v_out_norm_ssm, v_out_norm_attn, v_w_out, v_ffn_norm, v_w_up, v_conv_w, v_conv_b, v_w_down, v_final_norm):
    given = dict(locals())
    w = {n: given[n] for n in WEIGHTS}
    m = {n: given['m_' + n] for n in WEIGHTS}
    v = {n: given['v_' + n] for n in WEIGHTS}
    dev = 4 * lax.axis_index("x") + 2 * lax.axis_index("y") + lax.axis_index("c")

    shard = {
        'w_in': jnp.pad(w_in[0], ((0, 0), (0, D_IN_PAD - D_IN))),
        'w_glu': w_glu[0],
        'w_q_b': jnp.pad(w_q_b[0], ((0, 0), (0, HEAD_PAD - QK_NOPE - QK_ROPE))),
        'w_kv_b': w_kv_b[0],
        'w_out': w_out[0],
        'w_up': w_up[0],
        'w_down': w_down[0],
    }
    core = lax.axis_index("c").astype(jnp.int32).reshape(1)
    chip = (2 * lax.axis_index("x") + lax.axis_index("y")).astype(jnp.int32).reshape(1)
    early, late = ['w_in', 'w_glu', 'w_q_b', 'w_kv_b'], ['w_out', 'w_up', 'w_down']
    shard = {n: a.astype(BF16) for n, a in shard.items()}

    gathered = _all_gather("gather_early", [shard[n] for n in early]
                           + [meta_tokens, jnp.pad(conv_w[0], ((0, 5), (0, 0)))])
    wf = dict(zip(early, gathered[:len(early)]))
    for n in ('w_in', 'w_glu'):
        wf[n] = wf[n].reshape(-1, wf[n].shape[-1])
    meta = gathered[-2].transpose(1, 0, 2).reshape(N_META, D)
    wf['conv_w'] = gathered[-1][:, :3].transpose(1, 0, 2).reshape(3, D_FF)
    gather_token, wait_late = _split_exchange("gather_late", [shard[n] for n in late],
                                              [(N_DEV,) + shard[n].shape for n in late], _gather_copies, 4,
                                              after=gathered[0])

    def late_weights(after):
        mine, landed = wait_late(after)
        full = [lax.dynamic_update_slice(a, s[None], (dev, 0, 0)) for s, a in zip(mine, landed)]
        full_out, = _gather_forward("gather_forward_w_out", full[:1])
        token, wait_ffn = _split_exchange("gather_forward_ffn", full[1:], None, _forward_copies, 3, after=full_out)

        def ffn_weights(after):
            full_up, full_down = wait_ffn(after)[0]
            return full_up, full_down.reshape(-1, D)

        return full_out.reshape(-1, D), token, ffn_weights

    pending, between_cores = [], {}

    def send_grads(names, arg, stage):
        if stage == 'cores':
            token, between_cores[names[0]] = _split_exchange(
                "reduce_cores_" + names[0], arg, [(4,) + a.shape[1:] for a in arg], _cores_copies, 4)
            return token
        mine, got = between_cores.pop(names[0])(arg) if stage == 'chips' else (
            arg, _exchange_cores("reduce_cores_" + names[0], arg))
        sums = [_pair_sum("sum_cores_" + n, a, g, core) for n, a, g in zip(names, mine, got)]
        token, wait = _split_exchange("reduce_chips_" + names[0], sums, [(3,) + s.shape[1:] for s in sums],
                                      _chips_copies, 3)
        pending.append((names, wait))
        return token

    small_sent = []
    s5_bc = ['b_re', 'b_im', 'c_re', 'c_im']

    def send_small(sm, dcw4, loss_part):
        names = [n for n in SMALL if n in sm]
        arrs = [sm[n].reshape(-1, 128).astype(BF16) if n in s5_bc else sm[n] for n in names] + [dcw4, loss_part]
        token, wait = _split_exchange("gather_small", arrs, [(N_DEV,) + a.shape for a in arrs], _to_all_copies, 7,
                                      all_of=lambda a, land: land.at[pl.ds(0, N_DEV - 1)])
        small_sent.append((names, wait))
        return token

    small = {n: w[n] for n in SMALL}
    small['mix_norm'] = mix_norm + gather_token[0:1, 0:1]
    grad_x, dmeta, d_mix_norm, big = _local_step(x[0], loss_target[0], meta, small, wf, late_weights, send_grads,
                                                  send_small)
    grads, deltas, new_m, new_v = {}, {}, {}, {}

    def keep(n, outs):
        grads[n], deltas[n], new_m[n], new_v[n] = (o.reshape(w[n].shape) for o in outs)

    rest = list(big)
    from_core = _exchange_cores("reduce_cores", [big[n] for n in rest])
    chip_sums = [_pair_sum("sum_cores_" + n, big[n], got, core) for n, got in zip(rest, from_core)]
    last = chip_sums + [d_mix_norm, dmeta]
    n_big = len(chip_sums)
    last_token, wait_last = _split_exchange(
        "reduce_last", last, [(3,) + s.shape[1:] for s in chip_sums] + [(N_DEV,) + a.shape for a in last[n_big:]],
        lambda a, *at: _chips_copies(a, *at) if a < n_big else _to_all_copies(a, *at), 7,
        all_of=lambda a, land: land if a < n_big else land.at[pl.ds(0, N_DEV - 1)])

    def update(n, own, got):
        if n == 'w_up':
            outs = _adamw_big("adamw_" + n, own, got, chip, w[n][0].T, m[n][0].T, v[n][0].T, last_token)
            return keep(n, [o.T for o in outs])
        cols = w[n].shape[2]
        keep(n, _adamw_big("adamw_" + n, own[:, :, :cols], got[:, :, :cols], chip, w[n][0], m[n][0], v[n][0],
                           last_token))

    for names, wait in pending:
        for n, own, got in zip(names, *wait(chip_sums[0])):
            update(n, own, got)
    updated = [new_v[n].reshape(-1)[:1] for names, _ in pending for n in names]
    sent, landed = wait_last(functools.reduce(jnp.add, updated))
    for n, own, got in zip(rest, sent, landed):
        update(n, own, got)

    device = dev.astype(jnp.int32).reshape(1)
    outs = _adamw_multi("adamw_last", [(landed[n_big], mix_norm, m['mix_norm'], v['mix_norm'])],
                        sums=[landed[n_big + 1]], own=sent[n_big:], dev=device)
    keep('mix_norm', outs[:4])
    g_meta = outs[4]
    names, wait = small_sent[0]
    mine, landed = wait(grads['mix_norm'])
    parts, own = dict(zip(names + ['conv_w', 'loss'], landed)), dict(zip(names + ['conv_w', 'loss'], mine))
    wide = [n for n in names if n not in s5_bc]
    summed = ['conv_w', 'loss'] + s5_bc
    outs = _adamw_multi("adamw_small",
                        [(parts[n],) + tuple(t[n].reshape(own[n].shape) for t in (w, m, v)) for n in wide],
                        sums=[parts[n] for n in summed], own=[own[n] for n in wide + summed], dev=device)
    for i, n in enumerate(wide):
        keep(n, outs[4 * i:4 * i + 4])
    g_cw4, loss = outs[-6:-4]
    rows_gc = lambda a, n: (a.reshape(N_GROUPS, SSM_STATE, SSM_GROUP).transpose(0, 2, 1) if n[0] == 'b' else a
                            ).reshape(N_GROUPS * SSM_GROUP, SSM_STATE)
    outs = _adamw_multi("adamw_s5_bc", [(g.reshape(1, N_GROUPS * SSM_GROUP, SSM_STATE),)
                                        + tuple(rows_gc(t[n], n) for t in (w, m, v))
                                        for n, g in zip(s5_bc, outs[-4:])])
    for i, n in enumerate(s5_bc):
        back = outs[4 * i:4 * i + 4]
        if n[0] == 'b':
            back = [o.reshape(N_GROUPS, SSM_GROUP, SSM_STATE).transpose(0, 2, 1) for o in back]
        keep(n, back)

    g_meta = lax.dynamic_slice(g_meta, (0, dev * (D // N_DEV)), (N_META, D // N_DEV))
    g_conv = lax.dynamic_slice(g_cw4.transpose(1, 0, 2).reshape(3, D_FF), (0, dev * (D_FF // N_DEV)),
                               (3, D_FF // N_DEV))
    rows8 = lambda a: jnp.pad(a.reshape(3, D_FF // N_DEV), ((0, 5), (0, 0)))
    outs = _adamw_multi("adamw_cols", [(g_meta[None], meta_tokens, m['meta_tokens'], v['meta_tokens']),
                                       (rows8(g_conv)[None], rows8(conv_w), rows8(m['conv_w']), rows8(v['conv_w']))])
    keep('meta_tokens', outs[:4])
    keep('conv_w', [o[:3] for o in outs[4:]])

    return (loss[0, 0], grad_x[None], *[grads[n] for n in WEIGHTS], *[deltas[n] for n in WEIGHTS],
            *[new_m[n] for n in WEIGHTS], *[new_v[n] for n in WEIGHTS])
```

```python
import functools
import math

import jax
import jax.numpy as jnp
from jax import lax
from jax.experimental import pallas as pl
from jax.experimental.pallas import tpu as pltpu

F32, BF16 = jnp.float32, jnp.bfloat16

N_DEV = 8
N_META, SEQ, D = 16, 2048, 2048
T = N_META + SEQ
TR = 688
EPS = 1e-6
D_SSM, Q_LORA, KV_LORA, QK_ROPE = 1024, 512, 256, 64
D_IN = D_SSM + Q_LORA + KV_LORA + QK_ROPE
D_IN_PAD = 1920
N_HEADS, QK_NOPE, V_HEAD = 8, 128, 128
HEAD_PAD = 256
SM_SCALE = 1.0 / math.sqrt(QK_NOPE + QK_ROPE)
CHUNK = 64
N_GROUPS, SSM_GROUP, SSM_STATE = 64, 16, 64
N_SEG = 8
SEG = T // N_SEG
GQ = 8
D_FF = 5504
FF_PIECE = 1376
ROPE_BASE = 10000.0
LR, B1, B2, ADAM_EPS, WD, STEP = 0.001, 0.9, 0.999, 1e-08, 0.01, 10
VMEM_CAP = 60 * 1024 * 1024
MESH = pl.DeviceIdType.MESH

WEIGHTS = ['meta_tokens', 'mix_norm', 'w_in', 'lam_re', 'lam_im', 'log_dt', 'b_re', 'b_im', 'c_re', 'c_im',
           'd_skip', 'w_glu', 'b_glu', 'q_a_norm', 'w_q_b', 'kv_a_norm', 'w_kv_b', 'out_norm_ssm',
           'out_norm_attn', 'w_out', 'ffn_norm', 'w_up', 'conv_w', 'conv_b', 'w_down', 'final_norm']
BIG = ['w_in', 'w_glu', 'w_q_b', 'w_kv_b', 'w_out', 'w_up', 'w_down']
SMALL = [n for n in WEIGHTS if n not in BIG and n not in ('meta_tokens', 'conv_w')]


def _nbytes(shape, dtype):
    return math.prod(shape) * jnp.dtype(dtype).itemsize


def _in_hbm(a):
    return pltpu.with_memory_space_constraint(a, pltpu.HBM) if a.size * a.dtype.itemsize >= (1 << 20) else a


def _out_hbm(shape, dtype):
    big = math.prod(shape) * jnp.dtype(dtype).itemsize >= (1 << 20)
    return pltpu.HBM(shape, dtype) if big else jax.ShapeDtypeStruct(shape, dtype)


def _params(sem, need):
    return pltpu.CompilerParams(dimension_semantics=("arbitrary",) * sem,
                                vmem_limit_bytes=int(min(VMEM_CAP, max(need, 16 * 1024 * 1024))))


_DIMS = {'nn': (((1,), (0,)), ((), ())), 'nt': (((1,), (1,)), ((), ())), 'tn': (((0,), (0,)), ((), ()))}


def _mm(name, a, b, mode, tm, tn, tk, out_dtype, stack=None, res=None):
    sa, sb, so = stack in ('a_out', 'ab_red'), stack in ('b_out', 'ab_red'), stack in ('a_out', 'b_out')
    nj = a.shape[0] if sa else (b.shape[0] if sb else 1)
    a2, b2 = a.shape[-2:], b.shape[-2:]
    if mode == 'nn':
        (m, k), (k2, n) = a2, b2
    elif mode == 'nt':
        (m, k), (n, k2) = a2, b2
    else:
        (k, m), (k2, n) = a2, b2
    assert k == k2 and m % tm == 0 and n % tn == 0 and k % tk == 0, (name, a.shape, b.shape)
    n_jo, n_jr, n_k = (nj if so else 1), (nj if stack == 'ab_red' else 1), k // tk
    grid = (n_jo, m // tm, n // tn, n_jr, n_k)
    a_blk, a_idx = ((tk, tm), lambda i, j, kk: (kk, i)) if mode == 'tn' else ((tm, tk), lambda i, j, kk: (i, kk))
    b_blk, b_idx = ((tn, tk), lambda i, j, kk: (j, kk)) if mode == 'nt' else ((tk, tn), lambda i, j, kk: (kk, j))

    def spec(blk, idx, stacked, on_out):
        if not stacked:
            return pl.BlockSpec(blk, lambda jo, i, j, jr, kk: idx(i, j, kk))
        return pl.BlockSpec((None,) + blk, lambda jo, i, j, jr, kk: ((jo if on_out else jr),) + idx(i, j, kk))

    o_idx = lambda i, j, kk: (i, j)
    in_specs = [spec(a_blk, a_idx, sa, stack == 'a_out'), spec(b_blk, b_idx, sb, stack == 'b_out')]
    operands = [a, b]
    if res is not None:
        in_specs.append(spec((tm, tn), o_idx, False, False))
        operands.append(res)
    n_red = n_jr * n_k
    dims = _DIMS[mode]

    def body(*refs):
        a_ref, b_ref = refs[0], refs[1]
        res_ref = refs[2] if res is not None else None
        o_ref = refs[3] if res is not None else refs[2]
        part = lax.dot_general(a_ref[...].astype(BF16), b_ref[...].astype(BF16), dims,
                               preferred_element_type=F32)

        def finish(total):
            if res_ref is not None:
                total = total + res_ref[...]
            o_ref[...] = total.astype(o_ref.dtype)

        if n_red == 1:
            finish(part)
        else:
            acc_ref = refs[-1]
            step = pl.program_id(3) * n_k + pl.program_id(4)

            @pl.when(step == 0)
            def _():
                acc_ref[...] = part

            @pl.when(step > 0)
            def _():
                acc_ref[...] += part

            @pl.when(step == n_red - 1)
            def _():
                finish(acc_ref[...])

    out_shape = ((nj,) if so else ()) + (m, n)
    need = 2 * (_nbytes(a_blk, a.dtype) + _nbytes(b_blk, b.dtype) + _nbytes((tm, tn), out_dtype)
                + (_nbytes((tm, tn), F32) if res is not None else 0)) + 3 * _nbytes((tm, tn), F32) + (4 << 20)
    return pl.pallas_call(
        body, name=name, grid=grid, in_specs=in_specs, out_specs=spec((tm, tn), o_idx, so, True),
        out_shape=_out_hbm(out_shape, out_dtype),
        scratch_shapes=[pltpu.VMEM((tm, tn), F32)] if n_red > 1 else [],
        compiler_params=_params(5, need),
    )(*map(_in_hbm, operands))


def _rowwise(name, fn, row_ins, const_ins, row_outs, acc_outs, tr=TR):
    n_ri, n_ci, n_ro, n_ao = len(row_ins), len(const_ins), len(row_outs), len(acc_outs)

    def body(*refs):
        ri, ci = refs[:n_ri], refs[n_ri:n_ri + n_ci]
        ro, ao = refs[n_ri + n_ci:n_ri + n_ci + n_ro], refs[n_ri + n_ci + n_ro:]
        i = pl.program_id(0)
        outs = fn(i * tr, *[r[...] for r in ri], *[r[...] for r in ci])
        for r, v in zip(ro, outs[:n_ro]):
            r[...] = v.astype(r.dtype)
        if n_ao:
            @pl.when(i == 0)
            def _():
                for r, v in zip(ao, outs[n_ro:]):
                    r[...] = v

            @pl.when(i > 0)
            def _():
                for r, v in zip(ao, outs[n_ro:]):
                    r[...] += v

    in_specs = [pl.BlockSpec((tr, w), functools.partial(lambda cb, i: (i, cb), cb)) for _, w, cb in row_ins]
    in_specs += [pl.BlockSpec(c.shape, functools.partial(lambda nd, i: (0,) * nd, c.ndim)) for c in const_ins]
    out_specs = [pl.BlockSpec((tr, w), lambda i: (i, 0)) for w, _ in row_outs]
    out_specs += [pl.BlockSpec((1, w), lambda i: (0, 0)) for w in acc_outs]
    out_shape = [_out_hbm((T, w), dt) for w, dt in row_outs]
    out_shape += [jax.ShapeDtypeStruct((1, w), F32) for w in acc_outs]
    need = 2 * (sum(_nbytes((tr, w), a.dtype) for a, w, _ in row_ins) + sum(_nbytes((tr, w), dt) for w, dt in row_outs))
    need = 3 * need + (8 << 20)
    return pl.pallas_call(
        body, name=name, grid=(T // tr,), in_specs=in_specs, out_specs=out_specs, out_shape=out_shape,
        compiler_params=_params(1, need),
    )(*[_in_hbm(a) for a, _, _ in row_ins], *const_ins)


def _rms_fwd_fn(row0, h, g):
    h = h.astype(F32)
    r = lax.rsqrt(jnp.mean(h * h, axis=-1, keepdims=True) + EPS)
    return (h * r) * g, r


def _rms_bwd(dxn, h, r, g):
    xhat = h * r
    dxg = dxn * g
    dh = r * (dxg - xhat * jnp.mean(dxg * xhat, axis=-1, keepdims=True))
    return dh, jnp.sum(dxn * xhat, axis=0, keepdims=True)


def _rms_bwd_fn(row0, dxn, h, r, g):
    return _rms_bwd(dxn.astype(F32), h.astype(F32), r, g)


def _rms_bwd_res_fn(row0, dxn, h, r, res, g):
    dh, dg = _rms_bwd(dxn.astype(F32), h.astype(F32), r, g)
    return dh + res, dg


def _gelu_fn(row0, y):
    return (jax.nn.gelu(y),)


def _glu_fwd_fn(row0, y, gl, b, gain):
    ya = jax.nn.gelu(y) * jax.nn.sigmoid(gl + b)
    return _rms_fwd_fn(row0, ya, gain)


def _glu_bwd_fn(row0, dyan, y, gl, ra, b, gain):
    g = jax.nn.gelu(y)
    s = jax.nn.sigmoid(gl + b)
    dya, dgain = _rms_bwd(dyan, g * s, ra, gain)
    dgl = dya * g * s * (1.0 - s)
    return dgl, dya * s, dgain, jnp.sum(dgl, axis=0, keepdims=True)


def _gelu_bwd_fn(row0, dg, y):
    return (jax.vjp(jax.nn.gelu, y)[1](dg)[0],)


def _loss_fn(row0, h, tgt, gain):
    r = lax.rsqrt(jnp.mean(h * h, axis=-1, keepdims=True) + EPS)
    xhat = h * r
    rows = row0 + lax.broadcasted_iota(jnp.int32, h.shape, 0)
    err = jnp.where(rows >= N_META, xhat * gain - tgt, 0.0)
    loss = jnp.full((1, 128), jnp.sum(err * err) * (0.5 / D), F32)
    dh, dgain = _rms_bwd(err * (1.0 / D), h, r, gain)
    return dh, dgain, loss


def _rope_tables():
    pos = jnp.arange(T, dtype=jnp.int32)
    inv_freq = 1.0 / (ROPE_BASE ** (jnp.arange(0, QK_ROPE, 2, dtype=F32) / QK_ROPE))
    ang = pos.astype(F32)[:, None] * inv_freq[None, :]
    cos, sin, z32, z64 = jnp.cos(ang), jnp.sin(ang), jnp.zeros((T, 32), F32), jnp.zeros((T, 64), F32)
    return (jnp.concatenate([cos, cos, z64], 1), jnp.concatenate([-sin, z32, z64], 1),
            jnp.concatenate([z32, sin, z64], 1))


def _rot(x, cc, s1, s2):
    return x * cc + pltpu.roll(x, 96, 1) * s1 + pltpu.roll(x, 32, 1) * s2


def _derot(d, cc, s1, s2):
    return d * cc + pltpu.roll(d * s1, 32, 1) + pltpu.roll(d * s2, 96, 1)


def _chunk_of(pos):
    return jnp.where(pos < N_META, 0, 1 + ((pos - N_META) >> 6))


def _key_len(t):
    last_chunk = 1 + ((t + 1) * TR - 1 - N_META) // CHUNK
    return min(T, pl.cdiv(N_META + last_chunk * CHUNK, 128) * 128)


def _attn_probs(t, q_ref, kv_ref, kpe_ref, tq, tk):
    kl = _key_len(t)
    q = q_ref[...]
    qn = q[:, :QK_NOPE].astype(BF16)
    qp = _rot(q[:, QK_NOPE:], tq[0][...], tq[1][...], tq[2][...]).astype(BF16)
    kn = kv_ref[:kl, :QK_NOPE].astype(BF16)
    v = kv_ref[:kl, QK_NOPE:].astype(BF16)
    kp = _rot(kpe_ref[:kl, :], tk[0][:kl, :], tk[1][:kl, :], tk[2][:kl, :]).astype(BF16)
    s = lax.dot_general(qn, kn, _DIMS['nt'], preferred_element_type=F32)
    s = s + lax.dot_general(qp, kp, _DIMS['nt'], preferred_element_type=F32)
    qc = _chunk_of(t * TR + lax.broadcasted_iota(jnp.int32, (TR, 1), 0))
    kc = _chunk_of(lax.broadcasted_iota(jnp.int32, (1, kl), 1))
    s = jnp.where(kc <= qc, s * SM_SCALE, jnp.finfo(F32).min)
    e = jnp.exp(s - jnp.max(s, axis=-1, keepdims=True))
    return qn, qp, kn, kp, v, e, 1.0 / jnp.sum(e, axis=-1, keepdims=True)


def _per_q_tile(fn):
    for t in range(T // TR):
        pl.when(pl.program_id(1) == t)(functools.partial(fn, t))


def _attn_specs(z, tabs):
    q_spec = pl.BlockSpec((None, TR, HEAD_PAD), lambda h, i: (h, i, 0))
    kv_spec = pl.BlockSpec((None, T, HEAD_PAD), lambda h, i: (h, 0, 0))
    kpe_spec = pl.BlockSpec((T, 128), lambda h, i: (0, (D_IN - QK_ROPE) // 128))
    tq = [pl.BlockSpec((TR, 128), lambda h, i: (i, 0))] * 3
    tk = [pl.BlockSpec((T, 128), lambda h, i: (0, 0))] * 3
    return [q_spec, kv_spec, kpe_spec] + tq + tk


def _attn_fwd(q, kv, z, tabs):
    def body(q_ref, kv_ref, kpe_ref, c1, c2, c3, k1, k2, k3, o_ref):
        def tile(t):
            _, _, _, _, v, e, inv = _attn_probs(t, q_ref, kv_ref, kpe_ref, (c1, c2, c3), (k1, k2, k3))
            o_ref[...] = jnp.dot(e.astype(BF16), v, preferred_element_type=F32) * inv

        _per_q_tile(tile)

    return pl.pallas_call(
        body, name="attn_fwd", grid=(N_HEADS, T // TR), in_specs=_attn_specs(z, tabs),
        out_specs=pl.BlockSpec((TR, V_HEAD), lambda h, i: (i, h)),
        out_shape=_out_hbm((T, N_HEADS * V_HEAD), F32),
        compiler_params=_params(2, 48 << 20),
    )(_in_hbm(q), _in_hbm(kv), _in_hbm(z), *tabs, *tabs)


def _attn_bwd(q, kv, z, tabs, dyb):
    def body(q_ref, kv_ref, kpe_ref, c1, c2, c3, k1, k2, k3, do_ref, dq_ref, dkv_ref, dkpe_ref):
        @pl.when(pl.program_id(1) == 0)
        def _():
            dkv_ref[...] = jnp.zeros(dkv_ref.shape, F32)
            dkpe_ref[...] = jnp.zeros(dkpe_ref.shape, F32)

        def tile(t):
            kl = _key_len(t)
            qn, qp, kn, kp, v, e, inv = _attn_probs(t, q_ref, kv_ref, kpe_ref, (c1, c2, c3), (k1, k2, k3))
            do = do_ref[...]
            dp = lax.dot_general(do.astype(BF16), v, _DIMS['nt'], preferred_element_type=F32)
            delta = jnp.sum(e * dp, axis=-1, keepdims=True) * inv
            ds = (e * ((dp - delta) * (inv * SM_SCALE))).astype(BF16)
            pb, do = e.astype(BF16), (do * inv).astype(BF16)
            dq_ref[:, :QK_NOPE] = jnp.dot(ds, kn, preferred_element_type=F32).astype(dq_ref.dtype)
            dqp = _derot(jnp.dot(ds, kp, preferred_element_type=F32), c1[...], c2[...], c3[...])
            dq_ref[:, QK_NOPE:] = dqp.astype(dq_ref.dtype)
            dkv_ref[:kl, :QK_NOPE] += lax.dot_general(ds, qn, _DIMS['tn'], preferred_element_type=F32)
            dkv_ref[:kl, QK_NOPE:] += lax.dot_general(pb, do, _DIMS['tn'], preferred_element_type=F32)
            dkpe_ref[:kl, :] += lax.dot_general(ds, qp, _DIMS['tn'], preferred_element_type=F32)

        _per_q_tile(tile)

    return pl.pallas_call(
        body, name="attn_bwd", grid=(N_HEADS, T // TR),
        in_specs=_attn_specs(z, tabs) + [pl.BlockSpec((TR, V_HEAD), lambda h, i: (i, h))],
        out_specs=[pl.BlockSpec((None, TR, HEAD_PAD), lambda h, i: (h, i, 0)),
                   pl.BlockSpec((None, T, HEAD_PAD), lambda h, i: (h, 0, 0)),
                   pl.BlockSpec((None, T, 128), lambda h, i: (h, 0, 0))],
        out_shape=[_out_hbm((N_HEADS, T, HEAD_PAD), BF16), _out_hbm((N_HEADS, T, HEAD_PAD), F32),
                   _out_hbm((N_HEADS, T, 128), F32)],
        compiler_params=_params(2, 56 << 20),
    )(_in_hbm(q), _in_hbm(kv), _in_hbm(z), *tabs, *tabs, _in_hbm(dyb))


def _kpe_bwd(dkpe, tabs):
    def body(d_ref, c1, c2, c3, o_ref):
        d = d_ref[0]
        for h in range(1, N_HEADS):
            d = d + d_ref[h]
        o_ref[...] = _derot(d, c1[...], c2[...], c3[...]).astype(o_ref.dtype)

    tile = pl.BlockSpec((TR, 128), lambda i: (i, 0))
    return pl.pallas_call(
        body, name="kpe_bwd", grid=(T // TR,),
        in_specs=[pl.BlockSpec((N_HEADS, TR, 128), lambda i: (0, i, 0)), tile, tile, tile], out_specs=tile,
        out_shape=jax.ShapeDtypeStruct((T, 128), BF16), compiler_params=_params(1, 16 << 20),
    )(_in_hbm(dkpe), *tabs)


def _s5_prep_math(lam_re, lam_im, log_dt, bt_re, bt_im):
    dt = jnp.exp(log_dt)
    mag = jnp.exp(lam_re * dt)
    a_re, a_im = mag * jnp.cos(lam_im * dt), mag * jnp.sin(lam_im * dt)
    den = lam_re * lam_re + lam_im * lam_im
    w_re = ((a_re - 1.0) * lam_re + a_im * lam_im) / den
    w_im = (a_im * lam_re - (a_re - 1.0) * lam_im) / den
    return a_re, a_im, w_re[None] * bt_re - w_im[None] * bt_im, w_re[None] * bt_im + w_im[None] * bt_re


def _eye_groups():
    return (lax.broadcasted_iota(jnp.int32, (N_GROUPS, N_GROUPS), 0)
            == lax.broadcasted_iota(jnp.int32, (N_GROUPS, N_GROUPS), 1)).astype(F32)


def _row_to_col(row):
    return jnp.sum(_eye_groups() * row, axis=1, keepdims=True)


def _s5_prep(lam_re, lam_im, log_dt, bt_re, bt_im):
    def body(lr, li, ld, br, bi, ar, ai, bbr, bbi):
        ar[...], ai[...], bbr[...], bbi[...] = _s5_prep_math(lr[...], li[...], _row_to_col(ld[...]), br[...], bi[...])

    gp, cgp = jax.ShapeDtypeStruct((N_GROUPS, SSM_STATE), F32), jax.ShapeDtypeStruct(bt_re.shape, F32)
    return pl.pallas_call(body, name="s5_prep", out_shape=[gp, gp, cgp, cgp])(lam_re, lam_im, log_dt, bt_re, bt_im)


def _s5_prep_bwd(lam_re, lam_im, log_dt, bt_re, bt_im, da_re, da_im, dbb_re, dbb_im):
    def body(lr, li, ld, br, bi, dar, dai, dbr, dbi, o1, o2, o3, o4, o5):
        _, vjp = jax.vjp(_s5_prep_math, lr[...], li[...], _row_to_col(ld[...]), br[...], bi[...])
        o1[...], o2[...], dld, o4[...], o5[...] = vjp((dar[...], dai[...], dbr[...], dbi[...]))
        o3[...] = jnp.sum(_eye_groups() * dld, axis=0, keepdims=True)

    ins = (lam_re, lam_im, log_dt, bt_re, bt_im)
    return pl.pallas_call(body, name="s5_prep_bwd", out_shape=[jax.ShapeDtypeStruct(a.shape, F32) for a in ins])(
        *ins, da_re, da_im, dbb_re, dbb_im)


def _cmul(ar, ai, br, bi):
    return ar * br - ai * bi, ar * bi + ai * br


def _seg_rows(i):
    return pl.ds(pl.multiple_of(i * N_SEG, N_SEG), N_SEG)


def _scan(xr, xi, ar, ai, reverse):
    zero = jnp.zeros(ar.shape, F32)

    def local(j, carry):
        rows = _seg_rows(SEG - 1 - j if reverse else j)
        nr, ni = _cmul(ar, ai, *carry)
        nr, ni = nr + xr[rows, :], ni + xi[rows, :]
        xr[rows, :], xi[rows, :] = nr, ni
        return nr, ni

    er, ei = lax.fori_loop(0, SEG, local, (zero, zero))
    pr, pi = ar, ai
    for _ in range(8):
        pr, pi = _cmul(pr, pi, pr, pi)
    pr, pi = _cmul(*_cmul(pr, pi, ar, ai), ar, ai)
    row = lax.broadcasted_iota(jnp.int32, ar.shape, 0)
    edge, shift = (N_SEG - 1, N_SEG - 1) if reverse else (0, 1)
    hr, hi = zero, zero
    for _ in range(N_SEG - 1):
        tr_, ti_ = _cmul(pr, pi, hr, hi)
        hr = jnp.where(row == edge, 0.0, pltpu.roll(tr_ + er, shift, 0))
        hi = jnp.where(row == edge, 0.0, pltpu.roll(ti_ + ei, shift, 0))

    def fix(j, carry):
        rows = _seg_rows(SEG - 1 - j if reverse else j)
        xr[rows, :] += carry[0]
        xi[rows, :] += carry[1]
        return _cmul(ar, ai, *carry)

    lax.fori_loop(0, SEG, fix, _cmul(ar, ai, hr, hi))
    return hr, hi


def _split(x):
    hi = x.astype(BF16)
    return hi, (x - hi.astype(F32)).astype(BF16)


def _dot2(xs, ys, mode='nn'):
    d = lambda p, q: lax.dot_general(p, q, _DIMS[mode], preferred_element_type=F32)
    return d(xs[0], ys[0]) + d(xs[1], ys[0])


def _group_masks():
    ns = GQ * SSM_STATE
    rep = (lax.broadcasted_iota(jnp.int32, (SSM_STATE, ns), 1) % SSM_STATE
           == lax.broadcasted_iota(jnp.int32, (SSM_STATE, ns), 0)).astype(BF16)
    own = (lax.broadcasted_iota(jnp.int32, (128, ns), 0) // SSM_GROUP
           == lax.broadcasted_iota(jnp.int32, (128, ns), 1) // SSM_STATE)
    return rep, own


def _block_diag(x, rep, own):
    dot = lambda p: jnp.dot(p, rep, preferred_element_type=F32)
    return tuple(jnp.where(own, dot(part), 0.0).astype(BF16) for part in _split(x))


def _block_rows(full, rep, own):
    masked = _split(jnp.where(own, full, 0.0))
    dot = lambda p: lax.dot_general(p, rep, _DIMS['nt'], preferred_element_type=F32)
    return dot(masked[0]) + dot(masked[1])


def _s5_specs():
    col = pl.BlockSpec((T, 128), lambda q: (0, q))
    a_spec = pl.BlockSpec((None, 2, N_SEG, GQ * SSM_STATE), lambda q: (q, 0, 0, 0))
    rows = pl.BlockSpec((128, SSM_STATE), lambda q: (q, 0))
    d_spec = pl.BlockSpec((1, 128), lambda q: (0, q))
    return col, a_spec, rows, d_spec


def _s5_fwd(u, a8, bb_re, bb_im, c_re, c_im, dsk):
    def body(u_ref, a_ref, bre_ref, bim_ref, cre_ref, cim_ref, d_ref, y_ref, hre, him):
        u_ = u_ref[...]
        us = _split(u_)
        rep, own = _group_masks()
        hre[...] = _dot2(us, _block_diag(bre_ref[...], rep, own))
        him[...] = _dot2(us, _block_diag(bim_ref[...], rep, own))
        _scan(hre, him, a_ref[0], a_ref[1], reverse=False)
        dot = lambda h, c: lax.dot_general(h[...].astype(BF16), _block_diag(c[...], rep, own)[0], _DIMS['nt'],
                                           preferred_element_type=F32)
        y_ref[...] = dot(hre, cre_ref) - dot(him, cim_ref) + d_ref[...] * u_

    col, a_spec, rows, d_spec = _s5_specs()
    return pl.pallas_call(
        body, name="s5_fwd", grid=(N_GROUPS // GQ,), in_specs=[col, a_spec, rows, rows, rows, rows, d_spec],
        out_specs=col, out_shape=_out_hbm((T, D_SSM), F32),
        scratch_shapes=[pltpu.VMEM((T, GQ * SSM_STATE), F32)] * 2,
        compiler_params=_params(1, 40 << 20),
    )(_in_hbm(u), a8, bb_re, bb_im, c_re, c_im, dsk)


def _s5_bwd(u, dy, a8, bb_re, bb_im, c_re, c_im, dsk):
    def body(u_ref, dy_ref, a_ref, bre_ref, bim_ref, cre_ref, cim_ref, d_ref,
             du_ref, dbre_ref, dbim_ref, dcre_ref, dcim_ref, da_ref, dd_ref, hre, him, lre, lim):
        u_, dy_ = u_ref[...], dy_ref[...]
        us, dys = _split(u_), _split(dy_)
        ar, ai = a_ref[0], a_ref[1]
        rep, own = _group_masks()
        bres, bims = _block_diag(bre_ref[...], rep, own), _block_diag(bim_ref[...], rep, own)
        hre[...] = _dot2(us, bres)
        him[...] = _dot2(us, bims)
        h0r, h0i = _scan(hre, him, ar, ai, reverse=False)
        lre[...] = _dot2(dys, _block_diag(cre_ref[...], rep, own))
        lim[...] = -_dot2(dys, _block_diag(cim_ref[...], rep, own))
        _scan(lre, lim, ar, -ai, reverse=True)

        def acc_da(i, carry):
            lr, li = lre[_seg_rows(i), :], lim[_seg_rows(i), :]
            pr, pi = hre[_seg_rows(i - 1), :], him[_seg_rows(i - 1), :]
            return carry[0] + lr * pr + li * pi, carry[1] + li * pr - lr * pi

        lr, li = lre[_seg_rows(0), :], lim[_seg_rows(0), :]
        dar, dai = lax.fori_loop(1, SEG, acc_da, (lr * h0r + li * h0i, li * h0r - lr * h0i))
        da_ref[0:1, :] = jnp.sum(dar, axis=0, keepdims=True)
        da_ref[1:2, :] = jnp.sum(dai, axis=0, keepdims=True)
        dot = lambda p, q, mode: lax.dot_general(p, q, _DIMS[mode], preferred_element_type=F32)
        lr1, li1 = lre[...].astype(BF16), lim[...].astype(BF16)
        du_ref[...] = dy_ * d_ref[...] + dot(lr1, bres[0], 'nt') + dot(li1, bims[0], 'nt')
        dbre_ref[...] = _block_rows(dot(us[0], lr1, 'tn'), rep, own)
        dbim_ref[...] = _block_rows(dot(us[0], li1, 'tn'), rep, own)
        dcre_ref[...] = _block_rows(dot(dys[0], hre[...].astype(BF16), 'tn'), rep, own)
        dcim_ref[...] = -_block_rows(dot(dys[0], him[...].astype(BF16), 'tn'), rep, own)
        dd_ref[...] = jnp.sum(dy_ * u_, axis=0, keepdims=True)

    col, a_spec, rows, d_spec = _s5_specs()
    nq, ns = N_GROUPS // GQ, GQ * SSM_STATE
    per_row = jax.ShapeDtypeStruct((N_GROUPS * SSM_GROUP, SSM_STATE), F32)
    return pl.pallas_call(
        body, name="s5_bwd", grid=(nq,),
        in_specs=[col, col, a_spec, rows, rows, rows, rows, d_spec],
        out_specs=[col, rows, rows, rows, rows, pl.BlockSpec((None, 2, ns), lambda q: (q, 0, 0)), d_spec],
        out_shape=[_out_hbm((T, D_SSM), F32), per_row, per_row, per_row, per_row,
                   jax.ShapeDtypeStruct((nq, 2, ns), F32), jax.ShapeDtypeStruct((1, D_SSM), F32)],
        scratch_shapes=[pltpu.VMEM((T, ns), F32)] * 4,
        compiler_params=_params(1, 56 << 20),
    )(_in_hbm(u), _in_hbm(dy), a8, bb_re, bb_im, c_re, c_im, dsk)


def _to_segments(a, cols=None):
    a = a.reshape(N_SEG, SEG, a.shape[-1]).transpose(1, 0, 2)[:, :, :cols]
    return a.reshape(T, a.shape[-1])


def _from_segments(a):
    return a.reshape(SEG, N_SEG, a.shape[-1]).transpose(1, 0, 2).reshape(T, a.shape[-1])


def _ff_specs():
    up_spec = pl.BlockSpec((2, None, T, 128), lambda p, cb: (0, p, 0, cb))
    piece = pl.BlockSpec((None, T, 128), lambda p, cb: (p, 0, cb))
    cw_spec = pl.BlockSpec((None, 3, 128), lambda p, cb: (p, 0, cb))
    cb_spec = pl.BlockSpec((None, 1, 128), lambda p, cb: (p, 0, cb))
    return up_spec, piece, cw_spec, cb_spec


def _conv_gate(gate, w, cb, pad_ref):
    pad_ref[0:8, :] = jnp.zeros((8, 128), F32)
    pad_ref[8:8 + T, :] = gate
    g1, g2 = pad_ref[pl.ds(7, T), :], pad_ref[pl.ds(6, T), :]
    return w[0:1, :] * g2 + w[1:2, :] * g1 + w[2:3, :] * gate + cb, g1, g2


def _ff_act(up, cw4, cb4):
    def body(up_ref, w_ref, b_ref, o_ref, pad_ref):
        gc, _, _ = _conv_gate(up_ref[0].astype(F32), w_ref[...], b_ref[...], pad_ref)
        o_ref[...] = (jax.nn.silu(gc) * up_ref[1].astype(F32)).astype(o_ref.dtype)

    up_spec, piece, cw_spec, cb_spec = _ff_specs()
    n_cb = pl.cdiv(FF_PIECE, 128)
    return pl.pallas_call(
        body, name="ff_act", grid=(4, n_cb), in_specs=[up_spec, cw_spec, cb_spec], out_specs=piece,
        out_shape=_out_hbm((4, T, FF_PIECE), BF16),
        scratch_shapes=[pltpu.VMEM((T + 8, 128), F32)],
        compiler_params=_params(2, 24 << 20),
    )(_in_hbm(up.reshape(2, 4, T, FF_PIECE)), cw4, cb4)


def _ff_act_bwd(up, dact, cw4, cb4):
    def body(up_ref, da_ref, w_ref, b_ref, dup_ref, dw_ref, db_ref, pad_ref, pad2_ref):
        gate, val, w = up_ref[0].astype(F32), up_ref[1].astype(F32), w_ref[...]
        gc, g1, g2 = _conv_gate(gate, w, b_ref[...], pad_ref)
        sg = jax.nn.sigmoid(gc)
        da = da_ref[...].astype(F32)
        dup_ref[1] = (da * gc * sg).astype(dup_ref.dtype)
        dgc = da * val * sg * (1.0 + gc * (1.0 - sg))
        db_ref[...] = jnp.sum(dgc, axis=0, keepdims=True)
        dw_ref[0:1, :] = jnp.sum(dgc * g2, axis=0, keepdims=True)
        dw_ref[1:2, :] = jnp.sum(dgc * g1, axis=0, keepdims=True)
        dw_ref[2:3, :] = jnp.sum(dgc * gate, axis=0, keepdims=True)
        pad2_ref[0:T, :] = dgc
        pad2_ref[T:T + 8, :] = jnp.zeros((8, 128), F32)
        d1, d2 = pad2_ref[pl.ds(1, T), :], pad2_ref[pl.ds(2, T), :]
        dup_ref[0] = (w[2:3, :] * dgc + w[1:2, :] * d1 + w[0:1, :] * d2).astype(dup_ref.dtype)

    up_spec, piece, cw_spec, cb_spec = _ff_specs()
    n_cb = pl.cdiv(FF_PIECE, 128)
    dup, dw, db = pl.pallas_call(
        body, name="ff_act_bwd", grid=(4, n_cb), in_specs=[up_spec, piece, cw_spec, cb_spec],
        out_specs=[up_spec, cw_spec, cb_spec],
        out_shape=[_out_hbm((2, 4, T, FF_PIECE), BF16), jax.ShapeDtypeStruct((4, 3, FF_PIECE), F32),
                   jax.ShapeDtypeStruct((4, 1, FF_PIECE), F32)],
        scratch_shapes=[pltpu.VMEM((T + 8, 128), F32)] * 2,
        compiler_params=_params(2, 32 << 20),
    )(_in_hbm(up.reshape(2, 4, T, FF_PIECE)), _in_hbm(dact), cw4, cb4)
    return dup.reshape(8, T, FF_PIECE), dw, db


def _local_step(x, tgt, meta, p, wf, late_weights, send_grads, send_small):
    sm = {}
    h0 = jnp.concatenate([meta, x], axis=0)
    tgt_pad = jnp.concatenate([jnp.zeros((N_META, D), F32), tgt], axis=0)
    tabs = _rope_tables()
    row = lambda a: a.reshape(1, -1)

    xn, r0 = _rowwise("rms_mix", _rms_fwd_fn, [(h0, D, 0)], [row(p['mix_norm'])], [(D, BF16), (1, F32)], [])
    z = _mm("in_proj", xn, wf['w_in'], 'nn', TR, 640, D, F32)
    lam_re, lam_im = p['lam_re'].reshape(N_GROUPS, SSM_STATE), p['lam_im'].reshape(N_GROUPS, SSM_STATE)
    log_dt = p['log_dt'].reshape(1, N_GROUPS)
    bt_re = p['b_re'].reshape(N_GROUPS, SSM_STATE, SSM_GROUP).transpose(2, 0, 1)
    bt_im = p['b_im'].reshape(N_GROUPS, SSM_STATE, SSM_GROUP).transpose(2, 0, 1)
    c_re = p['c_re'].reshape(N_GROUPS * SSM_GROUP, SSM_STATE)
    c_im = p['c_im'].reshape(N_GROUPS * SSM_GROUP, SSM_STATE)
    a_re, a_im, bbt_re, bbt_im = _s5_prep(lam_re, lam_im, log_dt, bt_re, bt_im)
    nq, ns = N_GROUPS // GQ, GQ * SSM_STATE
    a8 = jnp.broadcast_to(jnp.stack([a_re.reshape(nq, ns), a_im.reshape(nq, ns)], 1)[:, :, None, :],
                          (nq, 2, N_SEG, ns))
    gcp = lambda t: t.transpose(1, 0, 2).reshape(N_GROUPS * SSM_GROUP, SSM_STATE)
    bb_re, bb_im = gcp(bbt_re), gcp(bbt_im)
    dsk = row(p['d_skip'])
    u_seg = _to_segments(z, D_SSM)
    y_ssm = _from_segments(_s5_fwd(u_seg, a8, bb_re, bb_im, c_re, c_im, dsk))
    g, = _rowwise("gelu", _gelu_fn, [(y_ssm, D_SSM, 0)], [], [(D_SSM, BF16)], [])
    gl = _mm("glu_proj", g, wf['w_glu'], 'nn', TR, D_SSM, D_SSM, F32)
    yan, ra = _rowwise("glu_norm", _glu_fwd_fn, [(y_ssm, D_SSM, 0), (gl, D_SSM, 0)],
                       [row(p['b_glu']), row(p['out_norm_ssm'])], [(D_SSM, BF16), (1, F32)], [])
    qn, rq, kvn, rkv = _rowwise(
        "rms_q_kv", lambda row0, zq, zkv, gq, gkv: _rms_fwd_fn(row0, zq, gq) + _rms_fwd_fn(row0, zkv, gkv),
        [(z, Q_LORA, D_SSM // Q_LORA), (z, KV_LORA, (D_SSM + Q_LORA) // KV_LORA)],
        [row(p['q_a_norm']), row(p['kv_a_norm'])], [(Q_LORA, BF16), (1, F32), (KV_LORA, BF16), (1, F32)], [])
    q = _mm("q_proj", qn, wf['w_q_b'], 'nn', T, HEAD_PAD, Q_LORA, F32, stack='b_out')
    kv = _mm("kv_proj", kvn, wf['w_kv_b'], 'nn', T, HEAD_PAD, KV_LORA, F32, stack='b_out')
    yb = _attn_fwd(q, kv, z, tabs)
    ybn, rb = _rowwise("rms_attn", _rms_fwd_fn, [(yb, D_SSM, 0)], [row(p['out_norm_attn'])],
                       [(D_SSM, BF16), (1, F32)], [])
    y = jnp.concatenate([yan, ybn], axis=1)
    wf = dict(wf)
    wf['w_out'], token, ffn_weights = late_weights(y)
    h1 = _mm("out_proj", y, wf['w_out'], 'nn', TR, 1024, D, F32, res=h0)
    xn2, r1 = _rowwise("rms_ffn", _rms_fwd_fn, [(h1, D, 0)], [row(p['ffn_norm']) + token[0:1, 0:1]],
                       [(D, BF16), (1, F32)], [])
    wf['w_up'], wf['w_down'] = ffn_weights(xn2)
    up = _mm("up_proj", xn2, wf['w_up'], 'nn', TR, FF_PIECE, D, BF16, stack='b_out')
    cw4 = wf['conv_w'].reshape(3, 4, FF_PIECE).transpose(1, 0, 2)
    cb4 = p['conv_b'].reshape(4, 1, FF_PIECE)
    act = _ff_act(up, cw4, cb4)
    wd4 = wf['w_down'].reshape(4, FF_PIECE, D)
    h2 = _mm("down_proj", act, wd4, 'nn', TR, 1024, FF_PIECE, F32, stack='ab_red', res=h1)
    dh2, sm['final_norm'], loss = _rowwise("loss", _loss_fn, [(h2, D, 0), (tgt_pad, D, 0)], [row(p['final_norm'])],
                                           [(D, F32)], [D, 128])

    big = {}
    dact = _mm("down_bwd_x", dh2, wd4, 'nt', TR, FF_PIECE, D, BF16, stack='b_out')
    g_down = _mm("down_bwd_w", act, dh2, 'tn', FF_PIECE, 512, T, BF16, stack='a_out').reshape(
        N_DEV, D_FF // N_DEV, D)
    token = send_grads(['w_down'], [g_down], 'cores')
    dup, dcw4, sm['conv_b'] = _ff_act_bwd(up, dact, cw4, cb4 + token[0:1, 0:1])
    dxn2 = _mm("up_bwd_x", dup, wf['w_up'], 'nt', TR, D, FF_PIECE, F32, stack='ab_red')
    g_up = _mm("up_bwd_w", dup, xn2, 'tn', FF_PIECE, 1024, T, BF16, stack='a_out')
    token = send_grads(['w_up'], [g_up], 'cores') + send_grads(['w_down'], dxn2, 'chips')
    dh1, sm['ffn_norm'] = _rowwise("rms_ffn_bwd", _rms_bwd_res_fn,
                                   [(dxn2, D, 0), (h1, D, 0), (r1, 1, 0), (dh2, D, 0)],
                                   [row(p['ffn_norm']) + token[0:1, 0:1]], [(D, F32)], [D])
    dy = _mm("out_bwd_x", dh1, wf['w_out'], 'nt', TR, 1024, D, F32)
    g_out = _mm("out_bwd_w", y, dh1, 'tn', 1024, 1024, T, BF16).reshape(N_DEV, D // N_DEV, D)
    dgl, dgd, sm['out_norm_ssm'], sm['b_glu'] = _rowwise(
        "glu_bwd", _glu_bwd_fn, [(dy, D_SSM, 0), (y_ssm, D_SSM, 0), (gl, D_SSM, 0), (ra, 1, 0)],
        [row(p['b_glu']), row(p['out_norm_ssm'])], [(D_SSM, BF16), (D_SSM, F32)], [D_SSM, D_SSM])
    dg = _mm("glu_bwd_x", dgl, wf['w_glu'], 'nt', TR, D_SSM, D_SSM, F32, res=dgd)
    g_glu = _mm("glu_bwd_w", g, dgl, 'tn', D_SSM, D_SSM, T, BF16).reshape(N_DEV, D_SSM // N_DEV, D_SSM)
    dy_ssm, = _rowwise("gelu_bwd", _gelu_bwd_fn, [(dg, D_SSM, 0), (y_ssm, D_SSM, 0)], [], [(D_SSM, F32)], [])
    token = send_grads(['w_up'], g_glu, 'chips')
    dyb, sm['out_norm_attn'] = _rowwise("rms_attn_bwd", _rms_bwd_fn, [(dy, D_SSM, 1), (yb, D_SSM, 0), (rb, 1, 0)],
                                        [row(p['out_norm_attn']) + token[0:1, 0:1]], [(D_SSM, F32)], [D_SSM])
    dq, dkv, dkpe = _attn_bwd(q, kv, z, tabs, dyb)
    g_q = _mm("q_bwd_w", qn, dq, 'tn', Q_LORA, HEAD_PAD, T, BF16, stack='b_out')
    dqn = _mm("q_bwd_x", dq, wf['w_q_b'], 'nt', T, Q_LORA, HEAD_PAD, F32, stack='ab_red')
    g_kv = _mm("kv_bwd_w", kvn, dkv, 'tn', KV_LORA, HEAD_PAD, T, BF16, stack='b_out')
    dkvn = _mm("kv_bwd_x", dkv, wf['w_kv_b'], 'nt', T, KV_LORA, HEAD_PAD, F32, stack='ab_red')
    token = send_grads(['w_out', 'w_glu', 'w_q_b', 'w_kv_b'], [g_out, g_glu, g_q, g_kv], 'both')
    def both_bwd(row0, dq_, zq, rq_, dkv_, zkv, rkv_, gq, gkv):
        (dzq, dgq), (dzkv, dgkv) = _rms_bwd(dq_, zq, rq_, gq), _rms_bwd(dkv_, zkv, rkv_, gkv)
        return dzq, dzkv, dgq, dgkv

    dq_a, dkv_a, sm['q_a_norm'], sm['kv_a_norm'] = _rowwise(
        "rms_q_kv_bwd", both_bwd,
        [(dqn, Q_LORA, 0), (z, Q_LORA, D_SSM // Q_LORA), (rq, 1, 0),
         (dkvn, KV_LORA, 0), (z, KV_LORA, (D_SSM + Q_LORA) // KV_LORA), (rkv, 1, 0)],
        [row(p['q_a_norm']), row(p['kv_a_norm'])], [(Q_LORA, BF16), (KV_LORA, BF16)], [Q_LORA, KV_LORA])
    dk_pe = _kpe_bwd(dkpe, tabs)
    du_seg, dbre, dbim, dcre, dcim, da, sm['d_skip'] = _s5_bwd(
        u_seg, _to_segments(dy_ssm), a8, bb_re, bb_im, c_re, c_im, dsk + token[0:1, 0:1])
    du = _from_segments(du_seg).astype(BF16)
    cgp = lambda d: d.reshape(N_GROUPS, SSM_GROUP, SSM_STATE).transpose(1, 0, 2)
    dlam_re, dlam_im, dlog_dt, dbt_re, dbt_im = _s5_prep_bwd(
        lam_re, lam_im, log_dt, bt_re, bt_im, da[:, 0, :].reshape(N_GROUPS, SSM_STATE),
        da[:, 1, :].reshape(N_GROUPS, SSM_STATE), cgp(dbre), cgp(dbim))
    sm['lam_re'], sm['lam_im'], sm['log_dt'] = dlam_re, dlam_im, dlog_dt
    sm['b_re'], sm['b_im'] = gcp(dbt_re), gcp(dbt_im)
    sm['c_re'], sm['c_im'] = dcre, dcim
    token = send_small(sm, dcw4, loss)
    dz = jnp.concatenate([du, dq_a, dkv_a, dk_pe], axis=1)
    dxn = _mm("in_bwd_x", dz, wf['w_in'], 'nt', TR, 1024, D_IN_PAD, F32)
    big['w_in'] = _mm("in_bwd_w", xn, dz, 'tn', 1024, 640, T, BF16).reshape(N_DEV, D // N_DEV, D_IN_PAD)
    dh0, d_mix_norm = _rowwise("rms_mix_bwd", _rms_bwd_res_fn,
                               [(dxn, D, 0), (h0, D, 0), (r0, 1, 0), (dh1, D, 0)],
                               [row(p['mix_norm']) + token[0:1, 0:1]], [(D, F32)], [D])
    return dh0[N_META:], dh0[:N_META], d_mix_norm, big


def _place():
    x, y, c = lax.axis_index("x"), lax.axis_index("y"), lax.axis_index("c")
    return x, y, c, [(1 - x, y), (x, 1 - y), (1 - x, 1 - y)]


_HBM = pl.BlockSpec(memory_space=pltpu.HBM)


def _all_gather(name, arrs, chip_blocks=()):
    n, nc = len(arrs), len(chip_blocks)

    def body(*refs):
        ins, cins, outs, couts = refs[:n], refs[n:n + nc], refs[n + nc:2 * n + nc], refs[2 * n + nc:2 * (n + nc)]
        send, recv, loc, csend, crecv = refs[2 * (n + nc):]
        x, y, c, chips = _place()
        me, sib = (x, y, c), (x, y, 1 - c)

        def cp(a, k, blk, to, src=None):
            dst = outs[a].at[4 * blk[0] + 2 * blk[1] + blk[2]]
            return pltpu.make_async_remote_copy(src_ref=dst if src is None else src, dst_ref=dst,
                                                send_sem=send.at[a, k], recv_sem=recv.at[a, k],
                                                device_id=to, device_id_type=MESH)

        between_chips = [pltpu.make_async_remote_copy(src_ref=cins[a].at[2 * ch[0] + ch[1]], dst_ref=couts[a].at[j],
                                                      send_sem=csend.at[a, j], recv_sem=crecv.at[a, j],
                                                      device_id=(*ch, c), device_id_type=MESH)
                         for a in range(nc) for j, ch in enumerate(chips)]
        mine = [pltpu.make_async_copy(ins[a], outs[a].at[4 * x + 2 * y + c], loc.at[a]) for a in range(n)]
        first = []
        for a in range(n):
            first.append(cp(a, 0, me, sib, src=ins[a]))
            first += [cp(a, 1 + j, me, (*ch, c), src=ins[a]) for j, ch in enumerate(chips)]
        for f in mine + first + between_chips:
            f.start()
        passed = []
        for a in range(n):
            for j, ch in enumerate(chips):
                cp(a, 1 + j, (*ch, c), me).wait_recv()
                passed.append(cp(a, 4 + j, (*ch, c), sib))
                passed[-1].start()
        for a in range(n):
            cp(a, 0, sib, me).wait_recv()
            for j, ch in enumerate(chips):
                cp(a, 4 + j, (*ch, 1 - c), me).wait_recv()
        for f in first + passed:
            f.wait_send()
        for m in mine + between_chips:
            m.wait()

    out = pl.pallas_call(
        body, name=name, in_specs=[_HBM] * (n + nc), out_specs=[_HBM] * (n + nc),
        out_shape=[jax.ShapeDtypeStruct((N_DEV,) + a.shape, a.dtype) for a in arrs]
        + [jax.ShapeDtypeStruct((3,) + a.shape[1:], a.dtype) for a in chip_blocks],
        scratch_shapes=[pltpu.SemaphoreType.DMA((n, 7)), pltpu.SemaphoreType.DMA((n, 7)),
                        pltpu.SemaphoreType.DMA((n,)), pltpu.SemaphoreType.DMA((max(nc, 1), 3)),
                        pltpu.SemaphoreType.DMA((max(nc, 1), 3))],
    )(*arrs, *chip_blocks)
    return (out[:n], out[n:]) if nc else out


def _exchange_cores(name, arrs):
    n = len(arrs)

    def body(*refs):
        ins, outs = refs[:n], refs[n:2 * n]
        send, recv = refs[2 * n:]
        x, y, c, _ = _place()
        remote = [pltpu.make_async_remote_copy(src_ref=ins[a].at[2 * k + 1 - c], dst_ref=outs[a].at[k],
                                               send_sem=send.at[a, k], recv_sem=recv.at[a, k],
                                               device_id=(x, y, 1 - c), device_id_type=MESH)
                  for a in range(n) for k in range(4)]
        for d in remote:
            d.start()
        for d in remote:
            d.wait()

    return pl.pallas_call(
        body, name=name, in_specs=[_HBM] * n, out_specs=[_HBM] * n,
        out_shape=[jax.ShapeDtypeStruct((4,) + a.shape[1:], a.dtype) for a in arrs],
        scratch_shapes=[pltpu.SemaphoreType.DMA((n, 4))] * 2,
    )(*arrs)


_SEM = pl.BlockSpec(memory_space=pltpu.SEMAPHORE)
_EFFECT = pltpu.SideEffectType.DATAFLOW_SIDE_EFFECTING


def _split_exchange(name, srcs, land_shapes, copies_of, n_cp, after=None, all_of=None):
    n = len(srcs)
    in_place = land_shapes is None
    lands = [] if in_place else [pltpu.with_memory_space_constraint(lax.empty(s, a.dtype), pltpu.HBM)
                                 for s, a in zip(land_shapes, srcs)]
    nb = n + len(lands)

    per = 1 if all_of else n_cp
    n_sem = n * per

    def descriptors(refs, send, recv, mirror):
        src_refs, land_refs = refs[:n], refs[nb - n:nb]
        x, y, c, chips = _place()
        out = []
        for a in range(n):
            if mirror and all_of:
                whole = all_of(a, land_refs[a])
                out.append(pltpu.make_async_remote_copy(src_ref=whole, dst_ref=whole, send_sem=send[a],
                                                        recv_sem=recv[a], device_id=(x, y, 1 - c),
                                                        device_id_type=MESH))
                continue
            for k, (src, dst, to, back) in enumerate(copies_of(a, src_refs[a], land_refs[a], x, y, c, chips)):
                out.append(pltpu.make_async_remote_copy(src_ref=src, dst_ref=back if mirror else dst,
                                                        send_sem=send[a * per + k % per],
                                                        recv_sem=recv[a * per + k % per],
                                                        device_id=to, device_id_type=MESH))
        return out

    def start_body(*refs):
        first = nb + (after is not None)
        sems = refs[first:first + 2 * n_sem]
        for d in descriptors(refs, sems[:n_sem], sems[n_sem:], False):
            d.start()
        refs[-1][...] = jnp.zeros((8, 128), F32)

    hbm_like = lambda arrs: [pltpu.HBM(a.shape, a.dtype) for a in arrs]
    order = [] if after is None else [after]
    res = pl.pallas_call(
        start_body, name=name + "_start", in_specs=[_HBM] * nb + [pl.BlockSpec(memory_space=pl.ANY)] * len(order),
        out_specs=[_SEM] * (2 * n_sem) + [_HBM] * nb + [pl.BlockSpec(memory_space=pltpu.VMEM)],
        out_shape=[pltpu.SemaphoreType.DMA(())] * (2 * n_sem)
        + hbm_like(srcs) + hbm_like(lands) + [jax.ShapeDtypeStruct((8, 128), F32)],
        input_output_aliases={i: 2 * n_sem + i for i in range(nb)},
        compiler_params=pltpu.CompilerParams(has_side_effects=_EFFECT),
    )(*srcs, *lands, *order)
    sems, thru, token = res[:2 * n_sem], res[2 * n_sem:2 * n_sem + nb], res[-1]

    def wait(after):
        def wait_body(*refs):
            s = refs[nb:nb + 2 * n_sem]
            for d in descriptors(refs, s[:n_sem], s[n_sem:], True):
                d.wait_send()
                d.wait_recv()

        out = pl.pallas_call(
            wait_body, name=name + "_wait",
            in_specs=[_HBM] * nb + [_SEM] * (2 * n_sem) + [pl.BlockSpec(memory_space=pl.ANY)],
            out_specs=[_HBM] * nb, out_shape=hbm_like(srcs) + hbm_like(lands),
            input_output_aliases={i: i for i in range(nb)},
            compiler_params=pltpu.CompilerParams(has_side_effects=_EFFECT),
        )(*thru, *sems, after)
        return out[:n], out[nb - n:]

    return token, wait


def _gather_copies(a, src, land, x, y, c, chips):
    peers = [(x, y, 1 - c)] + [(*ch, c) for ch in chips]
    me = 4 * x + 2 * y + c
    return [(src, land.at[me], to, land.at[4 * to[0] + 2 * to[1] + to[2]]) for to in peers]


def _to_all_copies(a, src, land, x, y, c, chips):
    peers = [(x, y, 1 - c)] + [(*ch, cc) for ch in chips for cc in (c, 1 - c)]
    me = 4 * x + 2 * y + c
    return [(src, land.at[me], to, land.at[4 * to[0] + 2 * to[1] + to[2]]) for to in peers]


def _cores_copies(a, src, land, x, y, c, chips):
    return [(src.at[2 * k + 1 - c], land.at[k], (x, y, 1 - c), land.at[k]) for k in range(4)]


def _forward_copies(a, src, land, x, y, c, chips):
    slot = lambda ch, core: 4 * ch[0] + 2 * ch[1] + core
    return [(land.at[slot(ch, c)], land.at[slot(ch, c)], (x, y, 1 - c), land.at[slot(ch, 1 - c)]) for ch in chips]


def _chips_copies(a, src, land, x, y, c, chips):
    return [(src.at[2 * ch[0] + ch[1]], land.at[j], (*ch, c), land.at[j]) for j, ch in enumerate(chips)]


def _gather_forward(name, arrs):
    n = len(arrs)

    def body(*refs):
        ins, outs = refs[:n], refs[n:2 * n]
        send, recv = refs[2 * n:]
        x, y, c, chips = _place()
        sends, recvs = [], []
        for a in range(n):
            for j, ch in enumerate(chips):
                mine, theirs = 4 * ch[0] + 2 * ch[1] + c, 4 * ch[0] + 2 * ch[1] + 1 - c
                sends.append(pltpu.make_async_remote_copy(src_ref=ins[a].at[mine], dst_ref=outs[a].at[mine],
                                                          send_sem=send.at[a, j], recv_sem=recv.at[a, j],
                                                          device_id=(x, y, 1 - c), device_id_type=MESH))
                recvs.append(pltpu.make_async_remote_copy(src_ref=ins[a].at[theirs], dst_ref=outs[a].at[theirs],
                                                          send_sem=send.at[a, j], recv_sem=recv.at[a, j],
                                                          device_id=(x, y, 1 - c), device_id_type=MESH))
        for d in sends:
            d.start()
        for s, r in zip(sends, recvs):
            s.wait_send()
            r.wait_recv()

    return pl.pallas_call(
        body, name=name, in_specs=[_HBM] * n, out_specs=[_HBM] * n,
        out_shape=[jax.ShapeDtypeStruct(a.shape, a.dtype) for a in arrs],
        input_output_aliases={i: i for i in range(n)},
        scratch_shapes=[pltpu.SemaphoreType.DMA((n, 3))] * 2,
    )(*arrs)


def _blocks_of(r, c):
    if r * c * 4 <= (2 << 20):
        return r, c
    if r % 128 == 0:
        return 128, c
    return r, 256


def _pair_sum(name, own, got, core):
    _, r, c = got.shape
    rb, cb = _blocks_of(r, c)

    def body(s_ref, a_ref, b_ref, o_ref):
        o_ref[...] = (a_ref[...].astype(F32) + b_ref[...].astype(F32)).astype(o_ref.dtype)

    return pl.pallas_call(
        body, name=name, out_shape=_out_hbm((4, r, c), got.dtype),
        grid_spec=pltpu.PrefetchScalarGridSpec(
            num_scalar_prefetch=1, grid=(4, r // rb, c // cb),
            in_specs=[pl.BlockSpec((None, rb, cb), lambda k, i, j, s: (2 * k + s[0], i, j)),
                      pl.BlockSpec((None, rb, cb), lambda k, i, j, s: (k, i, j))],
            out_specs=pl.BlockSpec((None, rb, cb), lambda k, i, j, s: (k, i, j))),
        compiler_params=_params(3, 24 << 20),
    )(core, _in_hbm(own), _in_hbm(got))


def _adamw_math(w, g, m, v):
    m = B1 * m + (1.0 - B1) * g
    v = B2 * v + (1.0 - B2) * (g * g)
    m_hat = m / (1.0 - B1 ** STEP)
    v_hat = v / (1.0 - B2 ** STEP)
    return -LR * (m_hat / (jnp.sqrt(v_hat) + ADAM_EPS) + WD * w), m, v


def _adamw_big(name, own, got, chip, w, m, v, behind):
    r, c = w.shape
    rb, cb = _blocks_of(r, c)

    def body(s_ref, o_ref, p_ref, w_ref, m_ref, v_ref, behind_ref, g_ref, d_ref, nm_ref, nv_ref):
        g = o_ref[...].astype(F32)
        for k in range(3):
            g = g + p_ref[k].astype(F32)
        g_ref[...] = g
        d_ref[...], nm_ref[...], nv_ref[...] = _adamw_math(w_ref[...], g, m_ref[...], v_ref[...])

    blk = pl.BlockSpec((rb, cb), lambda i, j, s: (i, j))
    return pl.pallas_call(
        body, name=name, out_shape=[jax.ShapeDtypeStruct((r, c), F32)] * 4,
        grid_spec=pltpu.PrefetchScalarGridSpec(
            num_scalar_prefetch=1, grid=(r // rb, c // cb),
            in_specs=[pl.BlockSpec((None, rb, cb), lambda i, j, s: (s[0], i, j)),
                      pl.BlockSpec((3, rb, cb), lambda i, j, s: (0, i, j)), blk, blk, blk,
                      pl.BlockSpec(memory_space=pl.ANY)],
            out_specs=[blk] * 4),
        compiler_params=_params(2, 40 << 20),
    )(chip, *map(_in_hbm, (own, got, w, m, v)), behind)


def _adamw_multi(name, items, sums=(), row_blocks=1, own=(), dev=None):
    n, ns = len(items), len(sums)
    n_own = len(own)
    assert n_own in (0, n + ns)

    def body(*refs):
        dev_ref = refs[0] if n_own else None
        refs = refs[1:] if n_own else refs
        owns, refs = refs[:n_own], refs[n_own:]
        ins, outs = refs[:4 * n + ns], refs[4 * n + ns:]

        def total(ref, t):
            g = None
            for d in range(ref.shape[0]):
                part = jnp.where(dev_ref[0] == d, owns[t][...], ref[d]) if n_own else ref[d]
                g = part.astype(F32) if g is None else g + part.astype(F32)
            return g

        for t in range(n):
            p_ref, w_ref, m_ref, v_ref = ins[4 * t:4 * t + 4]
            g = total(p_ref, t)
            outs[4 * t][...] = g
            outs[4 * t + 1][...], outs[4 * t + 2][...], outs[4 * t + 3][...] = _adamw_math(
                w_ref[...], g, m_ref[...], v_ref[...])
        for t in range(ns):
            outs[4 * n + t][...] = total(ins[4 * n + t], n + t)

    def spec(shape, lead):
        blk = (shape[0] // row_blocks,) + tuple(shape[1:])
        nd = len(shape)
        if lead:
            return pl.BlockSpec((lead,) + blk, lambda i: (0, i) + (0,) * (nd - 1))
        return pl.BlockSpec(blk, lambda i: (i,) + (0,) * (nd - 1))

    operands, in_specs, out_specs, out_shape = [], [], [], []
    if n_own:
        operands += [dev] + list(own)
        in_specs += [pl.BlockSpec(memory_space=pltpu.SMEM)] + [spec(o.shape, 0) for o in own]
    for parts, w, m, v in items:
        assert parts.shape[1:] == w.shape == m.shape == v.shape, (name, parts.shape, w.shape)
        operands += [parts, w, m, v]
        in_specs += [spec(w.shape, parts.shape[0])] + [spec(w.shape, 0)] * 3
        out_specs += [spec(w.shape, 0)] * 4
        out_shape += [jax.ShapeDtypeStruct(w.shape, F32)] * 4
    for parts in sums:
        operands.append(parts)
        in_specs.append(spec(parts.shape[1:], parts.shape[0]))
        out_specs.append(spec(parts.shape[1:], False))
        out_shape.append(jax.ShapeDtypeStruct(parts.shape[1:], F32))
    return pl.pallas_call(
        body, name=name, grid=(row_blocks,), in_specs=in_specs, out_specs=out_specs, out_shape=out_shape,
        compiler_params=_params(1, 56 << 20),
    )(*operands)


def kernel(x, meta_tokens, mix_norm, w_in, lam_re, lam_im, log_dt, b_re, b_im, c_re, c_im, d_skip, w_glu, b_glu, q_a_norm, w_q_b, kv_a_norm, w_kv_b, out_norm_ssm, out_norm_attn, w_out, ffn_norm, w_up, conv_w, conv_b, w_down, final_norm, loss_target, m_meta_tokens, m_mix_norm, m_w_in, m_lam_re, m_lam_im, m_log_dt, m_b_re, m_b_im, m_c_re, m_c_im, m_d_skip, m_w_glu, m_b_glu, m_q_a_norm, m_w_q_b, m_kv_a_norm, m_w_kv_b, m_out_norm_ssm, m_out_norm_attn, m_w_out, m_ffn_norm, m_w_up, m_conv_w, m_conv_b, m_w_down, m_final_norm, v_meta_tokens, v_mix_norm, v_w_in, v_lam_re, v_lam_im, v_log_dt, v_b_re, v_b_im, v_c_re, v_c_im, v_d_skip, v_w_glu, v_b_glu, v_q_a_norm, v_w_q_b, v_kv_a_norm, v_w_kv_b, v_out_norm_ssm, v_out_norm_attn, v_w_out, v_ffn_norm, v_w_up, v_conv_w, v_conv_b, v_w_down, v_final_norm):
    given = dict(locals())
    w = {n: given[n] for n in WEIGHTS}
    m = {n: given['m_' + n] for n in WEIGHTS}
    v = {n: given['v_' + n] for n in WEIGHTS}
    dev = 4 * lax.axis_index("x") + 2 * lax.axis_index("y") + lax.axis_index("c")

    shard = {
        'w_in': jnp.pad(w_in[0], ((0, 0), (0, D_IN_PAD - D_IN))),
        'w_glu': w_glu[0],
        'w_q_b': jnp.pad(w_q_b[0], ((0, 0), (0, HEAD_PAD - QK_NOPE - QK_ROPE))),
        'w_kv_b': w_kv_b[0],
        'w_out': w_out[0],
        'w_up': w_up[0],
        'w_down': w_down[0],
    }
    core = lax.axis_index("c").astype(jnp.int32).reshape(1)
    chip = (2 * lax.axis_index("x") + lax.axis_index("y")).astype(jnp.int32).reshape(1)
    early, late = ['w_in', 'w_glu', 'w_q_b', 'w_kv_b'], ['w_out', 'w_up', 'w_down']
    shard = {n: a.astype(BF16) for n, a in shard.items()}

    gathered = _all_gather("gather_early", [shard[n] for n in early]
                           + [meta_tokens, jnp.pad(conv_w[0], ((0, 5), (0, 0)))])
    wf = dict(zip(early, gathered[:len(early)]))
    for n in ('w_in', 'w_glu'):
        wf[n] = wf[n].reshape(-1, wf[n].shape[-1])
    meta = gathered[-2].transpose(1, 0, 2).reshape(N_META, D)
    wf['conv_w'] = gathered[-1][:, :3].transpose(1, 0, 2).reshape(3, D_FF)
    gather_token, wait_late = _split_exchange("gather_late", [shard[n] for n in late],
                                              [(N_DEV,) + shard[n].shape for n in late], _gather_copies, 4,
                                              after=gathered[0])

    def late_weights(after):
        mine, landed = wait_late(after)
        full = [lax.dynamic_update_slice(a, s[None], (dev, 0, 0)) for s, a in zip(mine, landed)]
        full_out, = _gather_forward("gather_forward_w_out", full[:1])
        token, wait_ffn = _split_exchange("gather_forward_ffn", full[1:], None, _forward_copies, 3, after=full_out)

        def ffn_weights(after):
            full_up, full_down = wait_ffn(after)[0]
            return full_up, full_down.reshape(-1, D)

        return full_out.reshape(-1, D), token, ffn_weights

    pending, between_cores = [], {}

    def send_grads(names, arg, stage):
        if stage == 'cores':
            token, between_cores[names[0]] = _split_exchange(
                "reduce_cores_" + names[0], arg, [(4,) + a.shape[1:] for a in arg], _cores_copies, 4)
            return token
        mine, got = between_cores.pop(names[0])(arg) if stage == 'chips' else (
            arg, _exchange_cores("reduce_cores_" + names[0], arg))
        sums = [_pair_sum("sum_cores_" + n, a, g, core) for n, a, g in zip(names, mine, got)]
        token, wait = _split_exchange("reduce_chips_" + names[0], sums, [(3,) + s.shape[1:] for s in sums],
                                      _chips_copies, 3)
        pending.append((names, wait))
        return token

    small_sent = []
    s5_bc = ['b_re', 'b_im', 'c_re', 'c_im']

    def send_small(sm, dcw4, loss_part):
        names = [n for n in SMALL if n in sm]
        arrs = [sm[n].reshape(-1, 128).astype(BF16) if n in s5_bc else sm[n] for n in names] + [dcw4, loss_part]
        token, wait = _split_exchange("gather_small", arrs, [(N_DEV,) + a.shape for a in arrs], _to_all_copies, 7,
                                      all_of=lambda a, land: land.at[pl.ds(0, N_DEV - 1)])
        small_sent.append((names, wait))
        return token

    small = {n: w[n] for n in SMALL}
    small['mix_norm'] = mix_norm + gather_token[0:1, 0:1]
    grad_x, dmeta, d_mix_norm, big = _local_step(x[0], loss_target[0], meta, small, wf, late_weights, send_grads,
                                                  send_small)
    grads, deltas, new_m, new_v = {}, {}, {}, {}

    def keep(n, outs):
        grads[n], deltas[n], new_m[n], new_v[n] = (o.reshape(w[n].shape) for o in outs)

    rest = list(big)
    from_core = _exchange_cores("reduce_cores", [big[n] for n in rest])
    chip_sums = [_pair_sum("sum_cores_" + n, big[n], got, core) for n, got in zip(rest, from_core)]
    last = chip_sums + [d_mix_norm, dmeta]
    n_big = len(chip_sums)
    last_token, wait_last = _split_exchange(
        "reduce_last", last, [(3,) + s.shape[1:] for s in chip_sums] + [(N_DEV,) + a.shape for a in last[n_big:]],
        lambda a, *at: _chips_copies(a, *at) if a < n_big else _to_all_copies(a, *at), 7,
        all_of=lambda a, land: land if a < n_big else land.at[pl.ds(0, N_DEV - 1)])

    def update(n, own, got):
        if n == 'w_up':
            outs = _adamw_big("adamw_" + n, own, got, chip, w[n][0].T, m[n][0].T, v[n][0].T, last_token)
            return keep(n, [o.T for o in outs])
        cols = w[n].shape[2]
        keep(n, _adamw_big("adamw_" + n, own[:, :, :cols], got[:, :, :cols], chip, w[n][0], m[n][0], v[n][0],
                           last_token))

    for names, wait in pending:
        for n, own, got in zip(names, *wait(chip_sums[0])):
            update(n, own, got)
    updated = [new_v[n].reshape(-1)[:1] for names, _ in pending for n in names]
    sent, landed = wait_last(functools.reduce(jnp.add, updated))
    for n, own, got in zip(rest, sent, landed):
        update(n, own, got)

    device = dev.astype(jnp.int32).reshape(1)
    outs = _adamw_multi("adamw_last", [(landed[n_big], mix_norm, m['mix_norm'], v['mix_norm'])],
                        sums=[landed[n_big + 1]], own=sent[n_big:], dev=device)
    keep('mix_norm', outs[:4])
    g_meta = outs[4]
    names, wait = small_sent[0]
    mine, landed = wait(grads['mix_norm'])
    parts, own = dict(zip(names + ['conv_w', 'loss'], landed)), dict(zip(names + ['conv_w', 'loss'], mine))
    wide = [n for n in names if n not in s5_bc]
    summed = ['conv_w', 'loss'] + s5_bc
    outs = _adamw_multi("adamw_small",
                        [(parts[n],) + tuple(t[n].reshape(own[n].shape) for t in (w, m, v)) for n in wide],
                        sums=[parts[n] for n in summed], own=[own[n] for n in wide + summed], dev=device)
    for i, n in enumerate(wide):
        keep(n, outs[4 * i:4 * i + 4])
    g_cw4, loss = outs[-6:-4]
    rows_gc = lambda a, n: (a.reshape(N_GROUPS, SSM_STATE, SSM_GROUP).transpose(0, 2, 1) if n[0] == 'b' else a
                            ).reshape(N_GROUPS * SSM_GROUP, SSM_STATE)
    outs = _adamw_multi("adamw_s5_bc", [(g.reshape(1, N_GROUPS * SSM_GROUP, SSM_STATE),)
                                        + tuple(rows_gc(t[n], n) for t in (w, m, v))
                                        for n, g in zip(s5_bc, outs[-4:])])
    for i, n in enumerate(s5_bc):
        back = outs[4 * i:4 * i + 4]
        if n[0] == 'b':
            back = [o.reshape(N_GROUPS, SSM_GROUP, SSM_STATE).transpose(0, 2, 1) for o in back]
        keep(n, back)

    g_meta = lax.dynamic_slice(g_meta, (0, dev * (D // N_DEV)), (N_META, D // N_DEV))
    g_conv = lax.dynamic_slice(g_cw4.transpose(1, 0, 2).reshape(3, D_FF), (0, dev * (D_FF // N_DEV)),
                               (3, D_FF // N_DEV))
    rows8 = lambda a: jnp.pad(a.reshape(3, D_FF // N_DEV), ((0, 5), (0, 0)))
    outs = _adamw_multi("adamw_cols", [(g_meta[None], meta_tokens, m['meta_tokens'], v['meta_tokens']),
                                       (rows8(g_conv)[None], rows8(conv_w), rows8(m['conv_w']), rows8(v['conv_w']))])
    keep('meta_tokens', outs[:4])
    keep('conv_w', [o[:3] for o in outs[4:]])

    return (loss[0, 0], grad_x[None], *[grads[n] for n in WEIGHTS], *[deltas[n] for n in WEIGHTS],
            *[new_m[n] for n in WEIGHTS], *[new_v[n] for n in WEIGHTS])
```

```python
import functools
import math

import jax
import jax.numpy as jnp
from jax import lax
from jax.experimental import pallas as pl
from jax.experimental.pallas import tpu as pltpu

F32, BF16 = jnp.float32, jnp.bfloat16

N_DEV = 8
N_META, SEQ, D = 16, 2048, 2048
T = N_META + SEQ
TR = 688
EPS = 1e-6
D_SSM, Q_LORA, KV_LORA, QK_ROPE = 1024, 512, 256, 64
D_IN = D_SSM + Q_LORA + KV_LORA + QK_ROPE
D_IN_PAD = 1920
N_HEADS, QK_NOPE, V_HEAD = 8, 128, 128
HEAD_PAD = 256
SM_SCALE = 1.0 / math.sqrt(QK_NOPE + QK_ROPE)
CHUNK = 64
N_GROUPS, SSM_GROUP, SSM_STATE = 64, 16, 64
N_SEG = 8
SEG = T // N_SEG
GQ = 8
D_FF = 5504
FF_PIECE = 1376
ROPE_BASE = 10000.0
LR, B1, B2, ADAM_EPS, WD, STEP = 0.001, 0.9, 0.999, 1e-08, 0.01, 10
VMEM_CAP = 60 * 1024 * 1024
MESH = pl.DeviceIdType.MESH

WEIGHTS = ['meta_tokens', 'mix_norm', 'w_in', 'lam_re', 'lam_im', 'log_dt', 'b_re', 'b_im', 'c_re', 'c_im',
           'd_skip', 'w_glu', 'b_glu', 'q_a_norm', 'w_q_b', 'kv_a_norm', 'w_kv_b', 'out_norm_ssm',
           'out_norm_attn', 'w_out', 'ffn_norm', 'w_up', 'conv_w', 'conv_b', 'w_down', 'final_norm']
BIG = ['w_in', 'w_glu', 'w_q_b', 'w_kv_b', 'w_out', 'w_up', 'w_down']
SMALL = [n for n in WEIGHTS if n not in BIG and n not in ('meta_tokens', 'conv_w')]


def _nbytes(shape, dtype):
    return math.prod(shape) * jnp.dtype(dtype).itemsize


def _in_hbm(a):
    return pltpu.with_memory_space_constraint(a, pltpu.HBM) if a.size * a.dtype.itemsize >= (1 << 20) else a


def _out_hbm(shape, dtype):
    big = math.prod(shape) * jnp.dtype(dtype).itemsize >= (1 << 20)
    return pltpu.HBM(shape, dtype) if big else jax.ShapeDtypeStruct(shape, dtype)


def _params(sem, need):
    return pltpu.CompilerParams(dimension_semantics=("arbitrary",) * sem,
                                vmem_limit_bytes=int(min(VMEM_CAP, max(need, 16 * 1024 * 1024))))


_DIMS = {'nn': (((1,), (0,)), ((), ())), 'nt': (((1,), (1,)), ((), ())), 'tn': (((0,), (0,)), ((), ()))}


def _mm(name, a, b, mode, tm, tn, tk, out_dtype, stack=None, res=None):
    sa, sb, so = stack in ('a_out', 'ab_red'), stack in ('b_out', 'ab_red'), stack in ('a_out', 'b_out')
    nj = a.shape[0] if sa else (b.shape[0] if sb else 1)
    a2, b2 = a.shape[-2:], b.shape[-2:]
    if mode == 'nn':
        (m, k), (k2, n) = a2, b2
    elif mode == 'nt':
        (m, k), (n, k2) = a2, b2
    else:
        (k, m), (k2, n) = a2, b2
    assert k == k2 and m % tm == 0 and n % tn == 0 and k % tk == 0, (name, a.shape, b.shape)
    n_jo, n_jr, n_k = (nj if so else 1), (nj if stack == 'ab_red' else 1), k // tk
    grid = (n_jo, m // tm, n // tn, n_jr, n_k)
    a_blk, a_idx = ((tk, tm), lambda i, j, kk: (kk, i)) if mode == 'tn' else ((tm, tk), lambda i, j, kk: (i, kk))
    b_blk, b_idx = ((tn, tk), lambda i, j, kk: (j, kk)) if mode == 'nt' else ((tk, tn), lambda i, j, kk: (kk, j))

    def spec(blk, idx, stacked, on_out):
        if not stacked:
            return pl.BlockSpec(blk, lambda jo, i, j, jr, kk: idx(i, j, kk))
        return pl.BlockSpec((None,) + blk, lambda jo, i, j, jr, kk: ((jo if on_out else jr),) + idx(i, j, kk))

    o_idx = lambda i, j, kk: (i, j)
    in_specs = [spec(a_blk, a_idx, sa, stack == 'a_out'), spec(b_blk, b_idx, sb, stack == 'b_out')]
    operands = [a, b]
    if res is not None:
        in_specs.append(spec((tm, tn), o_idx, False, False))
        operands.append(res)
    n_red = n_jr * n_k
    dims = _DIMS[mode]

    def body(*refs):
        a_ref, b_ref = refs[0], refs[1]
        res_ref = refs[2] if res is not None else None
        o_ref = refs[3] if res is not None else refs[2]
        part = lax.dot_general(a_ref[...].astype(BF16), b_ref[...].astype(BF16), dims,
                               preferred_element_type=F32)

        def finish(total):
            if res_ref is not None:
                total = total + res_ref[...]
            o_ref[...] = total.astype(o_ref.dtype)

        if n_red == 1:
            finish(part)
        else:
            acc_ref = refs[-1]
            step = pl.program_id(3) * n_k + pl.program_id(4)

            @pl.when(step == 0)
            def _():
                acc_ref[...] = part

            @pl.when(step > 0)
            def _():
                acc_ref[...] += part

            @pl.when(step == n_red - 1)
            def _():
                finish(acc_ref[...])

    out_shape = ((nj,) if so else ()) + (m, n)
    need = 2 * (_nbytes(a_blk, a.dtype) + _nbytes(b_blk, b.dtype) + _nbytes((tm, tn), out_dtype)
                + (_nbytes((tm, tn), F32) if res is not None else 0)) + 3 * _nbytes((tm, tn), F32) + (4 << 20)
    return pl.pallas_call(
        body, name=name, grid=grid, in_specs=in_specs, out_specs=spec((tm, tn), o_idx, so, True),
        out_shape=_out_hbm(out_shape, out_dtype),
        scratch_shapes=[pltpu.VMEM((tm, tn), F32)] if n_red > 1 else [],
        compiler_params=_params(5, need),
    )(*map(_in_hbm, operands))


def _rowwise(name, fn, row_ins, const_ins, row_outs, acc_outs, tr=TR):
    n_ri, n_ci, n_ro, n_ao = len(row_ins), len(const_ins), len(row_outs), len(acc_outs)

    def body(*refs):
        ri, ci = refs[:n_ri], refs[n_ri:n_ri + n_ci]
        ro, ao = refs[n_ri + n_ci:n_ri + n_ci + n_ro], refs[n_ri + n_ci + n_ro:]
        i = pl.program_id(0)
        outs = fn(i * tr, *[r[...] for r in ri], *[r[...] for r in ci])
        for r, v in zip(ro, outs[:n_ro]):
            r[...] = v.astype(r.dtype)
        if n_ao:
            @pl.when(i == 0)
            def _():
                for r, v in zip(ao, outs[n_ro:]):
                    r[...] = v

            @pl.when(i > 0)
            def _():
                for r, v in zip(ao, outs[n_ro:]):
                    r[...] += v

    in_specs = [pl.BlockSpec((tr, w), functools.partial(lambda cb, i: (i, cb), cb)) for _, w, cb in row_ins]
    in_specs += [pl.BlockSpec(c.shape, functools.partial(lambda nd, i: (0,) * nd, c.ndim)) for c in const_ins]
    out_specs = [pl.BlockSpec((tr, w), lambda i: (i, 0)) for w, _ in row_outs]
    out_specs += [pl.BlockSpec((1, w), lambda i: (0, 0)) for w in acc_outs]
    out_shape = [_out_hbm((T, w), dt) for w, dt in row_outs]
    out_shape += [jax.ShapeDtypeStruct((1, w), F32) for w in acc_outs]
    need = 2 * (sum(_nbytes((tr, w), a.dtype) for a, w, _ in row_ins) + sum(_nbytes((tr, w), dt) for w, dt in row_outs))
    need = 3 * need + (8 << 20)
    return pl.pallas_call(
        body, name=name, grid=(T // tr,), in_specs=in_specs, out_specs=out_specs, out_shape=out_shape,
        compiler_params=_params(1, need),
    )(*[_in_hbm(a) for a, _, _ in row_ins], *const_ins)


def _rms_fwd_fn(row0, h, g):
    h = h.astype(F32)
    r = lax.rsqrt(jnp.mean(h * h, axis=-1, keepdims=True) + EPS)
    return (h * r) * g, r


def _rms_bwd(dxn, h, r, g):
    xhat = h * r
    dxg = dxn * g
    dh = r * (dxg - xhat * jnp.mean(dxg * xhat, axis=-1, keepdims=True))
    return dh, jnp.sum(dxn * xhat, axis=0, keepdims=True)


def _rms_bwd_fn(row0, dxn, h, r, g):
    return _rms_bwd(dxn.astype(F32), h.astype(F32), r, g)


def _rms_bwd_res_fn(row0, dxn, h, r, res, g):
    dh, dg = _rms_bwd(dxn.astype(F32), h.astype(F32), r, g)
    return dh + res, dg


def _gelu_fn(row0, y):
    return (jax.nn.gelu(y),)


def _glu_fwd_fn(row0, y, gl, b, gain):
    ya = jax.nn.gelu(y) * jax.nn.sigmoid(gl + b)
    return _rms_fwd_fn(row0, ya, gain)


def _glu_bwd_fn(row0, dyan, y, gl, ra, b, gain):
    g = jax.nn.gelu(y)
    s = jax.nn.sigmoid(gl + b)
    dya, dgain = _rms_bwd(dyan, g * s, ra, gain)
    dgl = dya * g * s * (1.0 - s)
    return dgl, dya * s, dgain, jnp.sum(dgl, axis=0, keepdims=True)


def _gelu_bwd_fn(row0, dg, y):
    return (jax.vjp(jax.nn.gelu, y)[1](dg)[0],)


def _loss_fn(row0, h, tgt, gain):
    r = lax.rsqrt(jnp.mean(h * h, axis=-1, keepdims=True) + EPS)
    xhat = h * r
    rows = row0 + lax.broadcasted_iota(jnp.int32, h.shape, 0)
    err = jnp.where(rows >= N_META, xhat * gain - tgt, 0.0)
    loss = jnp.full((1, 128), jnp.sum(err * err) * (0.5 / D), F32)
    dh, dgain = _rms_bwd(err * (1.0 / D), h, r, gain)
    return dh, dgain, loss


def _rope_tables():
    pos = jnp.arange(T, dtype=jnp.int32)
    inv_freq = 1.0 / (ROPE_BASE ** (jnp.arange(0, QK_ROPE, 2, dtype=F32) / QK_ROPE))
    ang = pos.astype(F32)[:, None] * inv_freq[None, :]
    cos, sin, z32, z64 = jnp.cos(ang), jnp.sin(ang), jnp.zeros((T, 32), F32), jnp.zeros((T, 64), F32)
    return (jnp.concatenate([cos, cos, z64], 1), jnp.concatenate([-sin, z32, z64], 1),
            jnp.concatenate([z32, sin, z64], 1))


def _rot(x, cc, s1, s2):
    return x * cc + pltpu.roll(x, 96, 1) * s1 + pltpu.roll(x, 32, 1) * s2


def _derot(d, cc, s1, s2):
    return d * cc + pltpu.roll(d * s1, 32, 1) + pltpu.roll(d * s2, 96, 1)


def _chunk_of(pos):
    return jnp.where(pos < N_META, 0, 1 + ((pos - N_META) >> 6))


def _key_len(t):
    last_chunk = 1 + ((t + 1) * TR - 1 - N_META) // CHUNK
    return min(T, pl.cdiv(N_META + last_chunk * CHUNK, 128) * 128)


def _attn_probs(t, q_ref, kv_ref, kpe_ref, tq, tk):
    kl = _key_len(t)
    q = q_ref[...]
    qn = q[:, :QK_NOPE].astype(BF16)
    qp = _rot(q[:, QK_NOPE:], tq[0][...], tq[1][...], tq[2][...]).astype(BF16)
    kn = kv_ref[:kl, :QK_NOPE].astype(BF16)
    v = kv_ref[:kl, QK_NOPE:].astype(BF16)
    kp = _rot(kpe_ref[:kl, :], tk[0][:kl, :], tk[1][:kl, :], tk[2][:kl, :]).astype(BF16)
    s = lax.dot_general(qn, kn, _DIMS['nt'], preferred_element_type=F32)
    s = s + lax.dot_general(qp, kp, _DIMS['nt'], preferred_element_type=F32)
    qc = _chunk_of(t * TR + lax.broadcasted_iota(jnp.int32, (TR, 1), 0))
    kc = _chunk_of(lax.broadcasted_iota(jnp.int32, (1, kl), 1))
    s = jnp.where(kc <= qc, s * SM_SCALE, jnp.finfo(F32).min)
    e = jnp.exp(s - jnp.max(s, axis=-1, keepdims=True))
    return qn, qp, kn, kp, v, e, 1.0 / jnp.sum(e, axis=-1, keepdims=True)


def _per_q_tile(fn):
    for t in range(T // TR):
        pl.when(pl.program_id(1) == t)(functools.partial(fn, t))


def _attn_specs(z, tabs):
    q_spec = pl.BlockSpec((None, TR, HEAD_PAD), lambda h, i: (h, i, 0))
    kv_spec = pl.BlockSpec((None, T, HEAD_PAD), lambda h, i: (h, 0, 0))
    kpe_spec = pl.BlockSpec((T, 128), lambda h, i: (0, (D_IN - QK_ROPE) // 128))
    tq = [pl.BlockSpec((TR, 128), lambda h, i: (i, 0))] * 3
    tk = [pl.BlockSpec((T, 128), lambda h, i: (0, 0))] * 3
    return [q_spec, kv_spec, kpe_spec] + tq + tk


def _attn_fwd(q, kv, z, tabs):
    def body(q_ref, kv_ref, kpe_ref, c1, c2, c3, k1, k2, k3, o_ref):
        def tile(t):
            _, _, _, _, v, e, inv = _attn_probs(t, q_ref, kv_ref, kpe_ref, (c1, c2, c3), (k1, k2, k3))
            o_ref[...] = jnp.dot(e.astype(BF16), v, preferred_element_type=F32) * inv

        _per_q_tile(tile)

    return pl.pallas_call(
        body, name="attn_fwd", grid=(N_HEADS, T // TR), in_specs=_attn_specs(z, tabs),
        out_specs=pl.BlockSpec((TR, V_HEAD), lambda h, i: (i, h)),
        out_shape=_out_hbm((T, N_HEADS * V_HEAD), F32),
        compiler_params=_params(2, 48 << 20),
    )(_in_hbm(q), _in_hbm(kv), _in_hbm(z), *tabs, *tabs)


def _attn_bwd(q, kv, z, tabs, dyb):
    def body(q_ref, kv_ref, kpe_ref, c1, c2, c3, k1, k2, k3, do_ref, dq_ref, dkv_ref, dkpe_ref):
        @pl.when(pl.program_id(1) == 0)
        def _():
            dkv_ref[...] = jnp.zeros(dkv_ref.shape, F32)
            dkpe_ref[...] = jnp.zeros(dkpe_ref.shape, F32)

        def tile(t):
            kl = _key_len(t)
            qn, qp, kn, kp, v, e, inv = _attn_probs(t, q_ref, kv_ref, kpe_ref, (c1, c2, c3), (k1, k2, k3))
            do = do_ref[...]
            dp = lax.dot_general(do.astype(BF16), v, _DIMS['nt'], preferred_element_type=F32)
            delta = jnp.sum(e * dp, axis=-1, keepdims=True) * inv
            ds = (e * ((dp - delta) * (inv * SM_SCALE))).astype(BF16)
            pb, do = e.astype(BF16), (do * inv).astype(BF16)
            dq_ref[:, :QK_NOPE] = jnp.dot(ds, kn, preferred_element_type=F32).astype(dq_ref.dtype)
            dqp = _derot(jnp.dot(ds, kp, preferred_element_type=F32), c1[...], c2[...], c3[...])
            dq_ref[:, QK_NOPE:] = dqp.astype(dq_ref.dtype)
            dkv_ref[:kl, :QK_NOPE] += lax.dot_general(ds, qn, _DIMS['tn'], preferred_element_type=F32)
            dkv_ref[:kl, QK_NOPE:] += lax.dot_general(pb, do, _DIMS['tn'], preferred_element_type=F32)
            dkpe_ref[:kl, :] += lax.dot_general(ds, qp, _DIMS['tn'], preferred_element_type=F32)

        _per_q_tile(tile)

    return pl.pallas_call(
        body, name="attn_bwd", grid=(N_HEADS, T // TR),
        in_specs=_attn_specs(z, tabs) + [pl.BlockSpec((TR, V_HEAD), lambda h, i: (i, h))],
        out_specs=[pl.BlockSpec((None, TR, HEAD_PAD), lambda h, i: (h, i, 0)),
                   pl.BlockSpec((None, T, HEAD_PAD), lambda h, i: (h, 0, 0)),
                   pl.BlockSpec((None, T, 128), lambda h, i: (h, 0, 0))],
        out_shape=[_out_hbm((N_HEADS, T, HEAD_PAD), BF16), _out_hbm((N_HEADS, T, HEAD_PAD), F32),
                   _out_hbm((N_HEADS, T, 128), F32)],
        compiler_params=_params(2, 56 << 20),
    )(_in_hbm(q), _in_hbm(kv), _in_hbm(z), *tabs, *tabs, _in_hbm(dyb))


def _kpe_bwd(dkpe, tabs):
    def body(d_ref, c1, c2, c3, o_ref):
        d = d_ref[0]
        for h in range(1, N_HEADS):
            d = d + d_ref[h]
        o_ref[...] = _derot(d, c1[...], c2[...], c3[...]).astype(o_ref.dtype)

    tile = pl.BlockSpec((TR, 128), lambda i: (i, 0))
    return pl.pallas_call(
        body, name="kpe_bwd", grid=(T // TR,),
        in_specs=[pl.BlockSpec((N_HEADS, TR, 128), lambda i: (0, i, 0)), tile, tile, tile], out_specs=tile,
        out_shape=jax.ShapeDtypeStruct((T, 128), BF16), compiler_params=_params(1, 16 << 20),
    )(_in_hbm(dkpe), *tabs)


def _s5_prep_math(lam_re, lam_im, log_dt, bt_re, bt_im):
    dt = jnp.exp(log_dt)
    mag = jnp.exp(lam_re * dt)
    a_re, a_im = mag * jnp.cos(lam_im * dt), mag * jnp.sin(lam_im * dt)
    den = lam_re * lam_re + lam_im * lam_im
    w_re = ((a_re - 1.0) * lam_re + a_im * lam_im) / den
    w_im = (a_im * lam_re - (a_re - 1.0) * lam_im) / den
    return a_re, a_im, w_re[None] * bt_re - w_im[None] * bt_im, w_re[None] * bt_im + w_im[None] * bt_re


def _eye_groups():
    return (lax.broadcasted_iota(jnp.int32, (N_GROUPS, N_GROUPS), 0)
            == lax.broadcasted_iota(jnp.int32, (N_GROUPS, N_GROUPS), 1)).astype(F32)


def _row_to_col(row):
    return jnp.sum(_eye_groups() * row, axis=1, keepdims=True)


def _s5_prep(lam_re, lam_im, log_dt, bt_re, bt_im):
    def body(lr, li, ld, br, bi, ar, ai, bbr, bbi):
        ar[...], ai[...], bbr[...], bbi[...] = _s5_prep_math(lr[...], li[...], _row_to_col(ld[...]), br[...], bi[...])

    gp, cgp = jax.ShapeDtypeStruct((N_GROUPS, SSM_STATE), F32), jax.ShapeDtypeStruct(bt_re.shape, F32)
    return pl.pallas_call(body, name="s5_prep", out_shape=[gp, gp, cgp, cgp])(lam_re, lam_im, log_dt, bt_re, bt_im)


def _s5_prep_bwd(lam_re, lam_im, log_dt, bt_re, bt_im, da_re, da_im, dbb_re, dbb_im):
    def body(lr, li, ld, br, bi, dar, dai, dbr, dbi, o1, o2, o3, o4, o5):
        _, vjp = jax.vjp(_s5_prep_math, lr[...], li[...], _row_to_col(ld[...]), br[...], bi[...])
        o1[...], o2[...], dld, o4[...], o5[...] = vjp((dar[...], dai[...], dbr[...], dbi[...]))
        o3[...] = jnp.sum(_eye_groups() * dld, axis=0, keepdims=True)

    ins = (lam_re, lam_im, log_dt, bt_re, bt_im)
    return pl.pallas_call(body, name="s5_prep_bwd", out_shape=[jax.ShapeDtypeStruct(a.shape, F32) for a in ins])(
        *ins, da_re, da_im, dbb_re, dbb_im)


def _cmul(ar, ai, br, bi):
    return ar * br - ai * bi, ar * bi + ai * br


def _seg_rows(i):
    return pl.ds(pl.multiple_of(i * N_SEG, N_SEG), N_SEG)


def _scan(xr, xi, ar, ai, reverse):
    zero = jnp.zeros(ar.shape, F32)

    def local(j, carry):
        rows = _seg_rows(SEG - 1 - j if reverse else j)
        nr, ni = _cmul(ar, ai, *carry)
        nr, ni = nr + xr[rows, :], ni + xi[rows, :]
        xr[rows, :], xi[rows, :] = nr, ni
        return nr, ni

    er, ei = lax.fori_loop(0, SEG, local, (zero, zero))
    pr, pi = ar, ai
    for _ in range(8):
        pr, pi = _cmul(pr, pi, pr, pi)
    pr, pi = _cmul(*_cmul(pr, pi, ar, ai), ar, ai)
    row = lax.broadcasted_iota(jnp.int32, ar.shape, 0)
    edge, shift = (N_SEG - 1, N_SEG - 1) if reverse else (0, 1)
    hr, hi = zero, zero
    for _ in range(N_SEG - 1):
        tr_, ti_ = _cmul(pr, pi, hr, hi)
        hr = jnp.where(row == edge, 0.0, pltpu.roll(tr_ + er, shift, 0))
        hi = jnp.where(row == edge, 0.0, pltpu.roll(ti_ + ei, shift, 0))

    def fix(j, carry):
        rows = _seg_rows(SEG - 1 - j if reverse else j)
        xr[rows, :] += carry[0]
        xi[rows, :] += carry[1]
        return _cmul(ar, ai, *carry)

    lax.fori_loop(0, SEG, fix, _cmul(ar, ai, hr, hi))
    return hr, hi


def _split(x):
    hi = x.astype(BF16)
    return hi, (x - hi.astype(F32)).astype(BF16)


def _dot2(xs, ys, mode='nn'):
    d = lambda p, q: lax.dot_general(p, q, _DIMS[mode], preferred_element_type=F32)
    return d(xs[0], ys[0]) + d(xs[1], ys[0])


def _group_masks():
    ns = GQ * SSM_STATE
    rep = (lax.broadcasted_iota(jnp.int32, (SSM_STATE, ns), 1) % SSM_STATE
           == lax.broadcasted_iota(jnp.int32, (SSM_STATE, ns), 0)).astype(BF16)
    own = (lax.broadcasted_iota(jnp.int32, (128, ns), 0) // SSM_GROUP
           == lax.broadcasted_iota(jnp.int32, (128, ns), 1) // SSM_STATE)
    return rep, own


def _block_diag(x, rep, own):
    dot = lambda p: jnp.dot(p, rep, preferred_element_type=F32)
    return tuple(jnp.where(own, dot(part), 0.0).astype(BF16) for part in _split(x))


def _block_rows(full, rep, own):
    masked = _split(jnp.where(own, full, 0.0))
    dot = lambda p: lax.dot_general(p, rep, _DIMS['nt'], preferred_element_type=F32)
    return dot(masked[0]) + dot(masked[1])


def _s5_specs():
    col = pl.BlockSpec((T, 128), lambda q: (0, q))
    a_spec = pl.BlockSpec((None, 2, N_SEG, GQ * SSM_STATE), lambda q: (q, 0, 0, 0))
    rows = pl.BlockSpec((128, SSM_STATE), lambda q: (q, 0))
    d_spec = pl.BlockSpec((1, 128), lambda q: (0, q))
    return col, a_spec, rows, d_spec


def _s5_fwd(u, a8, bb_re, bb_im, c_re, c_im, dsk):
    def body(u_ref, a_ref, bre_ref, bim_ref, cre_ref, cim_ref, d_ref, y_ref, hre, him):
        u_ = u_ref[...]
        us = _split(u_)
        rep, own = _group_masks()
        hre[...] = _dot2(us, _block_diag(bre_ref[...], rep, own))
        him[...] = _dot2(us, _block_diag(bim_ref[...], rep, own))
        _scan(hre, him, a_ref[0], a_ref[1], reverse=False)
        dot = lambda h, c: lax.dot_general(h[...].astype(BF16), _block_diag(c[...], rep, own)[0], _DIMS['nt'],
                                           preferred_element_type=F32)
        y_ref[...] = dot(hre, cre_ref) - dot(him, cim_ref) + d_ref[...] * u_

    col, a_spec, rows, d_spec = _s5_specs()
    return pl.pallas_call(
        body, name="s5_fwd", grid=(N_GROUPS // GQ,), in_specs=[col, a_spec, rows, rows, rows, rows, d_spec],
        out_specs=col, out_shape=_out_hbm((T, D_SSM), F32),
        scratch_shapes=[pltpu.VMEM((T, GQ * SSM_STATE), F32)] * 2,
        compiler_params=_params(1, 40 << 20),
    )(_in_hbm(u), a8, bb_re, bb_im, c_re, c_im, dsk)


def _s5_bwd(u, dy, a8, bb_re, bb_im, c_re, c_im, dsk):
    def body(u_ref, dy_ref, a_ref, bre_ref, bim_ref, cre_ref, cim_ref, d_ref,
             du_ref, dbre_ref, dbim_ref, dcre_ref, dcim_ref, da_ref, dd_ref, hre, him, lre, lim):
        u_, dy_ = u_ref[...], dy_ref[...]
        us, dys = _split(u_), _split(dy_)
        ar, ai = a_ref[0], a_ref[1]
        rep, own = _group_masks()
        bres, bims = _block_diag(bre_ref[...], rep, own), _block_diag(bim_ref[...], rep, own)
        hre[...] = _dot2(us, bres)
        him[...] = _dot2(us, bims)
        h0r, h0i = _scan(hre, him, ar, ai, reverse=False)
        lre[...] = _dot2(dys, _block_diag(cre_ref[...], rep, own))
        lim[...] = -_dot2(dys, _block_diag(cim_ref[...], rep, own))
        _scan(lre, lim, ar, -ai, reverse=True)

        def acc_da(i, carry):
            lr, li = lre[_seg_rows(i), :], lim[_seg_rows(i), :]
            pr, pi = hre[_seg_rows(i - 1), :], him[_seg_rows(i - 1), :]
            return carry[0] + lr * pr + li * pi, carry[1] + li * pr - lr * pi

        lr, li = lre[_seg_rows(0), :], lim[_seg_rows(0), :]
        dar, dai = lax.fori_loop(1, SEG, acc_da, (lr * h0r + li * h0i, li * h0r - lr * h0i))
        da_ref[0:1, :] = jnp.sum(dar, axis=0, keepdims=True)
        da_ref[1:2, :] = jnp.sum(dai, axis=0, keepdims=True)
        dot = lambda p, q, mode: lax.dot_general(p, q, _DIMS[mode], preferred_element_type=F32)
        lr1, li1 = lre[...].astype(BF16), lim[...].astype(BF16)
        du_ref[...] = dy_ * d_ref[...] + dot(lr1, bres[0], 'nt') + dot(li1, bims[0], 'nt')
        dbre_ref[...] = _block_rows(dot(us[0], lr1, 'tn'), rep, own)
        dbim_ref[...] = _block_rows(dot(us[0], li1, 'tn'), rep, own)
        dcre_ref[...] = _block_rows(dot(dys[0], hre[...].astype(BF16), 'tn'), rep, own)
        dcim_ref[...] = -_block_rows(dot(dys[0], him[...].astype(BF16), 'tn'), rep, own)
        dd_ref[...] = jnp.sum(dy_ * u_, axis=0, keepdims=True)

    col, a_spec, rows, d_spec = _s5_specs()
    nq, ns = N_GROUPS // GQ, GQ * SSM_STATE
    per_row = jax.ShapeDtypeStruct((N_GROUPS * SSM_GROUP, SSM_STATE), F32)
    return pl.pallas_call(
        body, name="s5_bwd", grid=(nq,),
        in_specs=[col, col, a_spec, rows, rows, rows, rows, d_spec],
        out_specs=[col, rows, rows, rows, rows, pl.BlockSpec((None, 2, ns), lambda q: (q, 0, 0)), d_spec],
        out_shape=[_out_hbm((T, D_SSM), F32), per_row, per_row, per_row, per_row,
                   jax.ShapeDtypeStruct((nq, 2, ns), F32), jax.ShapeDtypeStruct((1, D_SSM), F32)],
        scratch_shapes=[pltpu.VMEM((T, ns), F32)] * 4,
        compiler_params=_params(1, 56 << 20),
    )(_in_hbm(u), _in_hbm(dy), a8, bb_re, bb_im, c_re, c_im, dsk)


def _to_segments(a, cols=None):
    a = a.reshape(N_SEG, SEG, a.shape[-1]).transpose(1, 0, 2)[:, :, :cols]
    return a.reshape(T, a.shape[-1])


def _from_segments(a):
    return a.reshape(SEG, N_SEG, a.shape[-1]).transpose(1, 0, 2).reshape(T, a.shape[-1])


def _ff_specs():
    up_spec = pl.BlockSpec((2, None, T, 128), lambda p, cb: (0, p, 0, cb))
    piece = pl.BlockSpec((None, T, 128), lambda p, cb: (p, 0, cb))
    cw_spec = pl.BlockSpec((None, 3, 128), lambda p, cb: (p, 0, cb))
    cb_spec = pl.BlockSpec((None, 1, 128), lambda p, cb: (p, 0, cb))
    return up_spec, piece, cw_spec, cb_spec


def _conv_gate(gate, w, cb, pad_ref):
    pad_ref[0:8, :] = jnp.zeros((8, 128), F32)
    pad_ref[8:8 + T, :] = gate
    g1, g2 = pad_ref[pl.ds(7, T), :], pad_ref[pl.ds(6, T), :]
    return w[0:1, :] * g2 + w[1:2, :] * g1 + w[2:3, :] * gate + cb, g1, g2


def _ff_act(up, cw4, cb4):
    def body(up_ref, w_ref, b_ref, o_ref, pad_ref):
        gc, _, _ = _conv_gate(up_ref[0].astype(F32), w_ref[...], b_ref[...], pad_ref)
        o_ref[...] = (jax.nn.silu(gc) * up_ref[1].astype(F32)).astype(o_ref.dtype)

    up_spec, piece, cw_spec, cb_spec = _ff_specs()
    n_cb = pl.cdiv(FF_PIECE, 128)
    return pl.pallas_call(
        body, name="ff_act", grid=(4, n_cb), in_specs=[up_spec, cw_spec, cb_spec], out_specs=piece,
        out_shape=_out_hbm((4, T, FF_PIECE), BF16),
        scratch_shapes=[pltpu.VMEM((T + 8, 128), F32)],
        compiler_params=_params(2, 24 << 20),
    )(_in_hbm(up.reshape(2, 4, T, FF_PIECE)), cw4, cb4)


def _ff_act_bwd(up, dact, cw4, cb4):
    def body(up_ref, da_ref, w_ref, b_ref, dup_ref, dw_ref, db_ref, pad_ref, pad2_ref):
        gate, val, w = up_ref[0].astype(F32), up_ref[1].astype(F32), w_ref[...]
        gc, g1, g2 = _conv_gate(gate, w, b_ref[...], pad_ref)
        sg = jax.nn.sigmoid(gc)
        da = da_ref[...].astype(F32)
        dup_ref[1] = (da * gc * sg).astype(dup_ref.dtype)
        dgc = da * val * sg * (1.0 + gc * (1.0 - sg))
        db_ref[...] = jnp.sum(dgc, axis=0, keepdims=True)
        dw_ref[0:1, :] = jnp.sum(dgc * g2, axis=0, keepdims=True)
        dw_ref[1:2, :] = jnp.sum(dgc * g1, axis=0, keepdims=True)
        dw_ref[2:3, :] = jnp.sum(dgc * gate, axis=0, keepdims=True)
        pad2_ref[0:T, :] = dgc
        pad2_ref[T:T + 8, :] = jnp.zeros((8, 128), F32)
        d1, d2 = pad2_ref[pl.ds(1, T), :], pad2_ref[pl.ds(2, T), :]
        dup_ref[0] = (w[2:3, :] * dgc + w[1:2, :] * d1 + w[0:1, :] * d2).astype(dup_ref.dtype)

    up_spec, piece, cw_spec, cb_spec = _ff_specs()
    n_cb = pl.cdiv(FF_PIECE, 128)
    dup, dw, db = pl.pallas_call(
        body, name="ff_act_bwd", grid=(4, n_cb), in_specs=[up_spec, piece, cw_spec, cb_spec],
        out_specs=[up_spec, cw_spec, cb_spec],
        out_shape=[_out_hbm((2, 4, T, FF_PIECE), BF16), jax.ShapeDtypeStruct((4, 3, FF_PIECE), F32),
                   jax.ShapeDtypeStruct((4, 1, FF_PIECE), F32)],
        scratch_shapes=[pltpu.VMEM((T + 8, 128), F32)] * 2,
        compiler_params=_params(2, 32 << 20),
    )(_in_hbm(up.reshape(2, 4, T, FF_PIECE)), _in_hbm(dact), cw4, cb4)
    return dup.reshape(8, T, FF_PIECE), dw, db


def _local_step(x, tgt, meta, p, wf, mid_weights, late_weights, send_grads, send_small):
    sm = {}
    h0 = jnp.concatenate([meta, x], axis=0)
    tgt_pad = jnp.concatenate([jnp.zeros((N_META, D), F32), tgt], axis=0)
    tabs = _rope_tables()
    row = lambda a: a.reshape(1, -1)

    xn, r0 = _rowwise("rms_mix", _rms_fwd_fn, [(h0, D, 0)], [row(p['mix_norm'])], [(D, BF16), (1, F32)], [])
    z = _mm("in_proj", xn, wf['w_in'], 'nn', TR, 640, D, F32)
    lam_re, lam_im = p['lam_re'].reshape(N_GROUPS, SSM_STATE), p['lam_im'].reshape(N_GROUPS, SSM_STATE)
    log_dt = p['log_dt'].reshape(1, N_GROUPS)
    bt_re = p['b_re'].reshape(N_GROUPS, SSM_STATE, SSM_GROUP).transpose(2, 0, 1)
    bt_im = p['b_im'].reshape(N_GROUPS, SSM_STATE, SSM_GROUP).transpose(2, 0, 1)
    c_re = p['c_re'].reshape(N_GROUPS * SSM_GROUP, SSM_STATE)
    c_im = p['c_im'].reshape(N_GROUPS * SSM_GROUP, SSM_STATE)
    a_re, a_im, bbt_re, bbt_im = _s5_prep(lam_re, lam_im, log_dt, bt_re, bt_im)
    nq, ns = N_GROUPS // GQ, GQ * SSM_STATE
    a8 = jnp.broadcast_to(jnp.stack([a_re.reshape(nq, ns), a_im.reshape(nq, ns)], 1)[:, :, None, :],
                          (nq, 2, N_SEG, ns))
    gcp = lambda t: t.transpose(1, 0, 2).reshape(N_GROUPS * SSM_GROUP, SSM_STATE)
    bb_re, bb_im = gcp(bbt_re), gcp(bbt_im)
    dsk = row(p['d_skip'])
    u_seg = _to_segments(z, D_SSM)
    y_ssm = _from_segments(_s5_fwd(u_seg, a8, bb_re, bb_im, c_re, c_im, dsk))
    wf = dict(wf, **mid_weights(y_ssm))
    g, = _rowwise("gelu", _gelu_fn, [(y_ssm, D_SSM, 0)], [], [(D_SSM, BF16)], [])
    gl = _mm("glu_proj", g, wf['w_glu'], 'nn', TR, D_SSM, D_SSM, F32)
    yan, ra = _rowwise("glu_norm", _glu_fwd_fn, [(y_ssm, D_SSM, 0), (gl, D_SSM, 0)],
                       [row(p['b_glu']), row(p['out_norm_ssm'])], [(D_SSM, BF16), (1, F32)], [])
    qn, rq, kvn, rkv = _rowwise(
        "rms_q_kv", lambda row0, zq, zkv, gq, gkv: _rms_fwd_fn(row0, zq, gq) + _rms_fwd_fn(row0, zkv, gkv),
        [(z, Q_LORA, D_SSM // Q_LORA), (z, KV_LORA, (D_SSM + Q_LORA) // KV_LORA)],
        [row(p['q_a_norm']), row(p['kv_a_norm'])], [(Q_LORA, BF16), (1, F32), (KV_LORA, BF16), (1, F32)], [])
    q = _mm("q_proj", qn, wf['w_q_b'], 'nn', T, HEAD_PAD, Q_LORA, F32, stack='b_out')
    kv = _mm("kv_proj", kvn, wf['w_kv_b'], 'nn', T, HEAD_PAD, KV_LORA, F32, stack='b_out')
    yb = _attn_fwd(q, kv, z, tabs)
    ybn, rb = _rowwise("rms_attn", _rms_fwd_fn, [(yb, D_SSM, 0)], [row(p['out_norm_attn'])],
                       [(D_SSM, BF16), (1, F32)], [])
    y = jnp.concatenate([yan, ybn], axis=1)
    wf = dict(wf)
    wf['w_out'], token, ffn_weights = late_weights(y)
    h1 = _mm("out_proj", y, wf['w_out'], 'nn', TR, 1024, D, F32, res=h0)
    xn2, r1 = _rowwise("rms_ffn", _rms_fwd_fn, [(h1, D, 0)], [row(p['ffn_norm']) + token[0:1, 0:1]],
                       [(D, BF16), (1, F32)], [])
    wf['w_up'], wf['w_down'] = ffn_weights(xn2)
    up = _mm("up_proj", xn2, wf['w_up'], 'nn', TR, FF_PIECE, D, BF16, stack='b_out')
    cw4 = wf['conv_w'].reshape(3, 4, FF_PIECE).transpose(1, 0, 2)
    cb4 = p['conv_b'].reshape(4, 1, FF_PIECE)
    act = _ff_act(up, cw4, cb4)
    wd4 = wf['w_down'].reshape(4, FF_PIECE, D)
    h2 = _mm("down_proj", act, wd4, 'nn', TR, 1024, FF_PIECE, F32, stack='ab_red', res=h1)
    dh2, sm['final_norm'], loss = _rowwise("loss", _loss_fn, [(h2, D, 0), (tgt_pad, D, 0)], [row(p['final_norm'])],
                                           [(D, F32)], [D, 128])

    big = {}
    dact = _mm("down_bwd_x", dh2, wd4, 'nt', TR, FF_PIECE, D, BF16, stack='b_out')
    g_down = _mm("down_bwd_w", act, dh2, 'tn', FF_PIECE, 512, T, BF16, stack='a_out').reshape(
        N_DEV, D_FF // N_DEV, D)
    token = send_grads(['w_down'], [g_down], 'cores')
    dup, dcw4, sm['conv_b'] = _ff_act_bwd(up, dact, cw4, cb4 + token[0:1, 0:1])
    dxn2 = _mm("up_bwd_x", dup, wf['w_up'], 'nt', TR, D, FF_PIECE, F32, stack='ab_red')
    g_up = _mm("up_bwd_w", dup, xn2, 'tn', FF_PIECE, 1024, T, BF16, stack='a_out')
    token = send_grads(['w_up'], [g_up], 'cores') + send_grads(['w_down'], dxn2, 'chips')
    dh1, sm['ffn_norm'] = _rowwise("rms_ffn_bwd", _rms_bwd_res_fn,
                                   [(dxn2, D, 0), (h1, D, 0), (r1, 1, 0), (dh2, D, 0)],
                                   [row(p['ffn_norm']) + token[0:1, 0:1]], [(D, F32)], [D])
    dy = _mm("out_bwd_x", dh1, wf['w_out'], 'nt', TR, 1024, D, F32)
    g_out = _mm("out_bwd_w", y, dh1, 'tn', 1024, 1024, T, BF16).reshape(N_DEV, D // N_DEV, D)
    dgl, dgd, sm['out_norm_ssm'], sm['b_glu'] = _rowwise(
        "glu_bwd", _glu_bwd_fn, [(dy, D_SSM, 0), (y_ssm, D_SSM, 0), (gl, D_SSM, 0), (ra, 1, 0)],
        [row(p['b_glu']), row(p['out_norm_ssm'])], [(D_SSM, BF16), (D_SSM, F32)], [D_SSM, D_SSM])
    dg = _mm("glu_bwd_x", dgl, wf['w_glu'], 'nt', TR, D_SSM, D_SSM, F32, res=dgd)
    g_glu = _mm("glu_bwd_w", g, dgl, 'tn', D_SSM, D_SSM, T, BF16).reshape(N_DEV, D_SSM // N_DEV, D_SSM)
    dy_ssm, = _rowwise("gelu_bwd", _gelu_bwd_fn, [(dg, D_SSM, 0), (y_ssm, D_SSM, 0)], [], [(D_SSM, F32)], [])
    token = send_grads(['w_up'], g_glu, 'chips')
    dyb, sm['out_norm_attn'] = _rowwise("rms_attn_bwd", _rms_bwd_fn, [(dy, D_SSM, 1), (yb, D_SSM, 0), (rb, 1, 0)],
                                        [row(p['out_norm_attn']) + token[0:1, 0:1]], [(D_SSM, F32)], [D_SSM])
    dq, dkv, dkpe = _attn_bwd(q, kv, z, tabs, dyb)
    g_q = _mm("q_bwd_w", qn, dq, 'tn', Q_LORA, HEAD_PAD, T, BF16, stack='b_out')
    dqn = _mm("q_bwd_x", dq, wf['w_q_b'], 'nt', T, Q_LORA, HEAD_PAD, F32, stack='ab_red')
    g_kv = _mm("kv_bwd_w", kvn, dkv, 'tn', KV_LORA, HEAD_PAD, T, BF16, stack='b_out')
    dkvn = _mm("kv_bwd_x", dkv, wf['w_kv_b'], 'nt', T, KV_LORA, HEAD_PAD, F32, stack='ab_red')
    token = send_grads(['w_out', 'w_glu', 'w_q_b', 'w_kv_b'], [g_out, g_glu, g_q, g_kv], 'both')
    def both_bwd(row0, dq_, zq, rq_, dkv_, zkv, rkv_, gq, gkv):
        (dzq, dgq), (dzkv, dgkv) = _rms_bwd(dq_, zq, rq_, gq), _rms_bwd(dkv_, zkv, rkv_, gkv)
        return dzq, dzkv, dgq, dgkv

    dq_a, dkv_a, sm['q_a_norm'], sm['kv_a_norm'] = _rowwise(
        "rms_q_kv_bwd", both_bwd,
        [(dqn, Q_LORA, 0), (z, Q_LORA, D_SSM // Q_LORA), (rq, 1, 0),
         (dkvn, KV_LORA, 0), (z, KV_LORA, (D_SSM + Q_LORA) // KV_LORA), (rkv, 1, 0)],
        [row(p['q_a_norm']), row(p['kv_a_norm'])], [(Q_LORA, BF16), (KV_LORA, BF16)], [Q_LORA, KV_LORA])
    dk_pe = _kpe_bwd(dkpe, tabs)
    du_seg, dbre, dbim, dcre, dcim, da, sm['d_skip'] = _s5_bwd(
        u_seg, _to_segments(dy_ssm), a8, bb_re, bb_im, c_re, c_im, dsk + token[0:1, 0:1])
    du = _from_segments(du_seg).astype(BF16)
    cgp = lambda d: d.reshape(N_GROUPS, SSM_GROUP, SSM_STATE).transpose(1, 0, 2)
    dlam_re, dlam_im, dlog_dt, dbt_re, dbt_im = _s5_prep_bwd(
        lam_re, lam_im, log_dt, bt_re, bt_im, da[:, 0, :].reshape(N_GROUPS, SSM_STATE),
        da[:, 1, :].reshape(N_GROUPS, SSM_STATE), cgp(dbre), cgp(dbim))
    sm['lam_re'], sm['lam_im'], sm['log_dt'] = dlam_re, dlam_im, dlog_dt
    sm['b_re'], sm['b_im'] = gcp(dbt_re), gcp(dbt_im)
    sm['c_re'], sm['c_im'] = dcre, dcim
    token = send_small(sm, dcw4, loss)
    dz = jnp.concatenate([du, dq_a, dkv_a, dk_pe], axis=1)
    dxn = _mm("in_bwd_x", dz, wf['w_in'], 'nt', TR, 1024, D_IN_PAD, F32)
    big['w_in'] = _mm("in_bwd_w", xn, dz, 'tn', 1024, 640, T, BF16).reshape(N_DEV, D // N_DEV, D_IN_PAD)
    dh0, d_mix_norm = _rowwise("rms_mix_bwd", _rms_bwd_res_fn,
                               [(dxn, D, 0), (h0, D, 0), (r0, 1, 0), (dh1, D, 0)],
                               [row(p['mix_norm']) + token[0:1, 0:1]], [(D, F32)], [D])
    return dh0[N_META:], dh0[:N_META], d_mix_norm, big


def _place():
    x, y, c = lax.axis_index("x"), lax.axis_index("y"), lax.axis_index("c")
    return x, y, c, [(1 - x, y), (x, 1 - y), (1 - x, 1 - y)]


_HBM = pl.BlockSpec(memory_space=pltpu.HBM)


def _all_gather(name, arrs, chip_blocks=()):
    n, nc = len(arrs), len(chip_blocks)

    def body(*refs):
        ins, cins, outs, couts = refs[:n], refs[n:n + nc], refs[n + nc:2 * n + nc], refs[2 * n + nc:2 * (n + nc)]
        send, recv, loc, csend, crecv = refs[2 * (n + nc):]
        x, y, c, chips = _place()
        me, sib = (x, y, c), (x, y, 1 - c)

        def cp(a, k, blk, to, src=None):
            dst = outs[a].at[4 * blk[0] + 2 * blk[1] + blk[2]]
            return pltpu.make_async_remote_copy(src_ref=dst if src is None else src, dst_ref=dst,
                                                send_sem=send.at[a, k], recv_sem=recv.at[a, k],
                                                device_id=to, device_id_type=MESH)

        between_chips = [pltpu.make_async_remote_copy(src_ref=cins[a].at[2 * ch[0] + ch[1]], dst_ref=couts[a].at[j],
                                                      send_sem=csend.at[a, j], recv_sem=crecv.at[a, j],
                                                      device_id=(*ch, c), device_id_type=MESH)
                         for a in range(nc) for j, ch in enumerate(chips)]
        mine = [pltpu.make_async_copy(ins[a], outs[a].at[4 * x + 2 * y + c], loc.at[a]) for a in range(n)]
        first = []
        for a in range(n):
            first.append(cp(a, 0, me, sib, src=ins[a]))
            first += [cp(a, 1 + j, me, (*ch, c), src=ins[a]) for j, ch in enumerate(chips)]
        for f in mine + first + between_chips:
            f.start()
        passed = []
        for a in range(n):
            for j, ch in enumerate(chips):
                cp(a, 1 + j, (*ch, c), me).wait_recv()
                passed.append(cp(a, 4 + j, (*ch, c), sib))
                passed[-1].start()
        for a in range(n):
            cp(a, 0, sib, me).wait_recv()
            for j, ch in enumerate(chips):
                cp(a, 4 + j, (*ch, 1 - c), me).wait_recv()
        for f in first + passed:
            f.wait_send()
        for m in mine + between_chips:
            m.wait()

    out = pl.pallas_call(
        body, name=name, in_specs=[_HBM] * (n + nc), out_specs=[_HBM] * (n + nc),
        out_shape=[jax.ShapeDtypeStruct((N_DEV,) + a.shape, a.dtype) for a in arrs]
        + [jax.ShapeDtypeStruct((3,) + a.shape[1:], a.dtype) for a in chip_blocks],
        scratch_shapes=[pltpu.SemaphoreType.DMA((n, 7)), pltpu.SemaphoreType.DMA((n, 7)),
                        pltpu.SemaphoreType.DMA((n,)), pltpu.SemaphoreType.DMA((max(nc, 1), 3)),
                        pltpu.SemaphoreType.DMA((max(nc, 1), 3))],
    )(*arrs, *chip_blocks)
    return (out[:n], out[n:]) if nc else out


def _exchange_cores(name, arrs):
    n = len(arrs)

    def body(*refs):
        ins, outs = refs[:n], refs[n:2 * n]
        send, recv = refs[2 * n:]
        x, y, c, _ = _place()
        remote = [pltpu.make_async_remote_copy(src_ref=ins[a].at[2 * k + 1 - c], dst_ref=outs[a].at[k],
                                               send_sem=send.at[a, k], recv_sem=recv.at[a, k],
                                               device_id=(x, y, 1 - c), device_id_type=MESH)
                  for a in range(n) for k in range(4)]
        for d in remote:
            d.start()
        for d in remote:
            d.wait()

    return pl.pallas_call(
        body, name=name, in_specs=[_HBM] * n, out_specs=[_HBM] * n,
        out_shape=[jax.ShapeDtypeStruct((4,) + a.shape[1:], a.dtype) for a in arrs],
        scratch_shapes=[pltpu.SemaphoreType.DMA((n, 4))] * 2,
    )(*arrs)


_SEM = pl.BlockSpec(memory_space=pltpu.SEMAPHORE)
_EFFECT = pltpu.SideEffectType.DATAFLOW_SIDE_EFFECTING


def _split_exchange(name, srcs, land_shapes, copies_of, n_cp, after=None, all_of=None):
    n = len(srcs)
    in_place = land_shapes is None
    lands = [] if in_place else [pltpu.with_memory_space_constraint(lax.empty(s, a.dtype), pltpu.HBM)
                                 for s, a in zip(land_shapes, srcs)]
    nb = n + len(lands)

    per = 1 if all_of else n_cp
    n_sem = n * per

    def descriptors(refs, send, recv, mirror):
        src_refs, land_refs = refs[:n], refs[nb - n:nb]
        x, y, c, chips = _place()
        out = []
        for a in range(n):
            if mirror and all_of:
                whole = all_of(a, land_refs[a])
                out.append(pltpu.make_async_remote_copy(src_ref=whole, dst_ref=whole, send_sem=send[a],
                                                        recv_sem=recv[a], device_id=(x, y, 1 - c),
                                                        device_id_type=MESH))
                continue
            for k, (src, dst, to, back) in enumerate(copies_of(a, src_refs[a], land_refs[a], x, y, c, chips)):
                out.append(pltpu.make_async_remote_copy(src_ref=src, dst_ref=back if mirror else dst,
                                                        send_sem=send[a * per + k % per],
                                                        recv_sem=recv[a * per + k % per],
                                                        device_id=to, device_id_type=MESH))
        return out

    def start_body(*refs):
        first = nb + (after is not None)
        sems = refs[first:first + 2 * n_sem]
        for d in descriptors(refs, sems[:n_sem], sems[n_sem:], False):
            d.start()
        refs[-1][...] = jnp.zeros((8, 128), F32)

    hbm_like = lambda arrs: [pltpu.HBM(a.shape, a.dtype) for a in arrs]
    order = [] if after is None else [after]
    res = pl.pallas_call(
        start_body, name=name + "_start", in_specs=[_HBM] * nb + [pl.BlockSpec(memory_space=pl.ANY)] * len(order),
        out_specs=[_SEM] * (2 * n_sem) + [_HBM] * nb + [pl.BlockSpec(memory_space=pltpu.VMEM)],
        out_shape=[pltpu.SemaphoreType.DMA(())] * (2 * n_sem)
        + hbm_like(srcs) + hbm_like(lands) + [jax.ShapeDtypeStruct((8, 128), F32)],
        input_output_aliases={i: 2 * n_sem + i for i in range(nb)},
        compiler_params=pltpu.CompilerParams(has_side_effects=_EFFECT),
    )(*srcs, *lands, *order)
    sems, thru, token = res[:2 * n_sem], res[2 * n_sem:2 * n_sem + nb], res[-1]

    def wait(after):
        def wait_body(*refs):
            s = refs[nb:nb + 2 * n_sem]
            for d in descriptors(refs, s[:n_sem], s[n_sem:], True):
                d.wait_send()
                d.wait_recv()

        out = pl.pallas_call(
            wait_body, name=name + "_wait",
            in_specs=[_HBM] * nb + [_SEM] * (2 * n_sem) + [pl.BlockSpec(memory_space=pl.ANY)],
            out_specs=[_HBM] * nb, out_shape=hbm_like(srcs) + hbm_like(lands),
            input_output_aliases={i: i for i in range(nb)},
            compiler_params=pltpu.CompilerParams(has_side_effects=_EFFECT),
        )(*thru, *sems, after)
        return out[:n], out[nb - n:]

    return token, wait


def _gather_copies(a, src, land, x, y, c, chips):
    peers = [(x, y, 1 - c)] + [(*ch, c) for ch in chips]
    me = 4 * x + 2 * y + c
    return [(src, land.at[me], to, land.at[4 * to[0] + 2 * to[1] + to[2]]) for to in peers]


def _to_all_copies(a, src, land, x, y, c, chips):
    peers = [(x, y, 1 - c)] + [(*ch, cc) for ch in chips for cc in (c, 1 - c)]
    me = 4 * x + 2 * y + c
    return [(src, land.at[me], to, land.at[4 * to[0] + 2 * to[1] + to[2]]) for to in peers]


def _cores_copies(a, src, land, x, y, c, chips):
    return [(src.at[2 * k + 1 - c], land.at[k], (x, y, 1 - c), land.at[k]) for k in range(4)]


def _forward_copies(a, src, land, x, y, c, chips):
    slot = lambda ch, core: 4 * ch[0] + 2 * ch[1] + core
    return [(land.at[slot(ch, c)], land.at[slot(ch, c)], (x, y, 1 - c), land.at[slot(ch, 1 - c)]) for ch in chips]


def _chips_copies(a, src, land, x, y, c, chips):
    return [(src.at[2 * ch[0] + ch[1]], land.at[j], (*ch, c), land.at[j]) for j, ch in enumerate(chips)]


def _gather_forward(name, arrs):
    n = len(arrs)

    def body(*refs):
        ins, outs = refs[:n], refs[n:2 * n]
        send, recv = refs[2 * n:]
        x, y, c, chips = _place()
        sends, recvs = [], []
        for a in range(n):
            for j, ch in enumerate(chips):
                mine, theirs = 4 * ch[0] + 2 * ch[1] + c, 4 * ch[0] + 2 * ch[1] + 1 - c
                sends.append(pltpu.make_async_remote_copy(src_ref=ins[a].at[mine], dst_ref=outs[a].at[mine],
                                                          send_sem=send.at[a, j], recv_sem=recv.at[a, j],
                                                          device_id=(x, y, 1 - c), device_id_type=MESH))
                recvs.append(pltpu.make_async_remote_copy(src_ref=ins[a].at[theirs], dst_ref=outs[a].at[theirs],
                                                          send_sem=send.at[a, j], recv_sem=recv.at[a, j],
                                                          device_id=(x, y, 1 - c), device_id_type=MESH))
        for d in sends:
            d.start()
        for s, r in zip(sends, recvs):
            s.wait_send()
            r.wait_recv()

    return pl.pallas_call(
        body, name=name, in_specs=[_HBM] * n, out_specs=[_HBM] * n,
        out_shape=[jax.ShapeDtypeStruct(a.shape, a.dtype) for a in arrs],
        input_output_aliases={i: i for i in range(n)},
        scratch_shapes=[pltpu.SemaphoreType.DMA((n, 3))] * 2,
    )(*arrs)


def _blocks_of(r, c):
    if r * c * 4 <= (2 << 20):
        return r, c
    if r % 128 == 0:
        return 128, c
    return r, 256


def _pair_sum(name, own, got, core):
    _, r, c = got.shape
    rb, cb = _blocks_of(r, c)

    def body(s_ref, a_ref, b_ref, o_ref):
        o_ref[...] = (a_ref[...].astype(F32) + b_ref[...].astype(F32)).astype(o_ref.dtype)

    return pl.pallas_call(
        body, name=name, out_shape=_out_hbm((4, r, c), got.dtype),
        grid_spec=pltpu.PrefetchScalarGridSpec(
            num_scalar_prefetch=1, grid=(4, r // rb, c // cb),
            in_specs=[pl.BlockSpec((None, rb, cb), lambda k, i, j, s: (2 * k + s[0], i, j)),
                      pl.BlockSpec((None, rb, cb), lambda k, i, j, s: (k, i, j))],
            out_specs=pl.BlockSpec((None, rb, cb), lambda k, i, j, s: (k, i, j))),
        compiler_params=_params(3, 24 << 20),
    )(core, _in_hbm(own), _in_hbm(got))


def _adamw_math(w, g, m, v):
    m = B1 * m + (1.0 - B1) * g
    v = B2 * v + (1.0 - B2) * (g * g)
    m_hat = m / (1.0 - B1 ** STEP)
    v_hat = v / (1.0 - B2 ** STEP)
    return -LR * (m_hat / (jnp.sqrt(v_hat) + ADAM_EPS) + WD * w), m, v


def _adamw_big(name, own, got, chip, w, m, v, behind):
    r, c = w.shape
    rb, cb = _blocks_of(r, c)

    def body(s_ref, o_ref, p_ref, w_ref, m_ref, v_ref, behind_ref, g_ref, d_ref, nm_ref, nv_ref):
        g = o_ref[...].astype(F32)
        for k in range(3):
            g = g + p_ref[k].astype(F32)
        g_ref[...] = g
        d_ref[...], nm_ref[...], nv_ref[...] = _adamw_math(w_ref[...], g, m_ref[...], v_ref[...])

    blk = pl.BlockSpec((rb, cb), lambda i, j, s: (i, j))
    return pl.pallas_call(
        body, name=name, out_shape=[jax.ShapeDtypeStruct((r, c), F32)] * 4,
        grid_spec=pltpu.PrefetchScalarGridSpec(
            num_scalar_prefetch=1, grid=(r // rb, c // cb),
            in_specs=[pl.BlockSpec((None, rb, cb), lambda i, j, s: (s[0], i, j)),
                      pl.BlockSpec((3, rb, cb), lambda i, j, s: (0, i, j)), blk, blk, blk,
                      pl.BlockSpec(memory_space=pl.ANY)],
            out_specs=[blk] * 4),
        compiler_params=_params(2, 40 << 20),
    )(chip, *map(_in_hbm, (own, got, w, m, v)), behind)


def _adamw_multi(name, items, sums=(), row_blocks=1, own=(), dev=None):
    n, ns = len(items), len(sums)
    n_own = len(own)
    assert n_own in (0, n + ns)

    def body(*refs):
        dev_ref = refs[0] if n_own else None
        refs = refs[1:] if n_own else refs
        owns, refs = refs[:n_own], refs[n_own:]
        ins, outs = refs[:4 * n + ns], refs[4 * n + ns:]

        def total(ref, t):
            g = None
            for d in range(ref.shape[0]):
                part = jnp.where(dev_ref[0] == d, owns[t][...], ref[d]) if n_own else ref[d]
                g = part.astype(F32) if g is None else g + part.astype(F32)
            return g

        for t in range(n):
            p_ref, w_ref, m_ref, v_ref = ins[4 * t:4 * t + 4]
            g = total(p_ref, t)
            outs[4 * t][...] = g
            outs[4 * t + 1][...], outs[4 * t + 2][...], outs[4 * t + 3][...] = _adamw_math(
                w_ref[...], g, m_ref[...], v_ref[...])
        for t in range(ns):
            outs[4 * n + t][...] = total(ins[4 * n + t], n + t)

    def spec(shape, lead):
        blk = (shape[0] // row_blocks,) + tuple(shape[1:])
        nd = len(shape)
        if lead:
            return pl.BlockSpec((lead,) + blk, lambda i: (0, i) + (0,) * (nd - 1))
        return pl.BlockSpec(blk, lambda i: (i,) + (0,) * (nd - 1))

    operands, in_specs, out_specs, out_shape = [], [], [], []
    if n_own:
        operands += [dev] + list(own)
        in_specs += [pl.BlockSpec(memory_space=pltpu.SMEM)] + [spec(o.shape, 0) for o in own]
    for parts, w, m, v in items:
        assert parts.shape[1:] == w.shape == m.shape == v.shape, (name, parts.shape, w.shape)
        operands += [parts, w, m, v]
        in_specs += [spec(w.shape, parts.shape[0])] + [spec(w.shape, 0)] * 3
        out_specs += [spec(w.shape, 0)] * 4
        out_shape += [jax.ShapeDtypeStruct(w.shape, F32)] * 4
    for parts in sums:
        operands.append(parts)
        in_specs.append(spec(parts.shape[1:], parts.shape[0]))
        out_specs.append(spec(parts.shape[1:], False))
        out_shape.append(jax.ShapeDtypeStruct(parts.shape[1:], F32))
    return pl.pallas_call(
        body, name=name, grid=(row_blocks,), in_specs=in_specs, out_specs=out_specs, out_shape=out_shape,
        compiler_params=_params(1, 56 << 20),
    )(*operands)


def kernel(x, meta_tokens, mix_norm, w_in, lam_re, lam_im, log_dt, b_re, b_im, c_re, c_im, d_skip, w_glu, b_glu, q_a_norm, w_q_b, kv_a_norm, w_kv_b, out_norm_ssm, out_norm_attn, w_out, ffn_norm, w_up, conv_w, conv_b, w_down, final_norm, loss_target, m_meta_tokens, m_mix_norm, m_w_in, m_lam_re, m_lam_im, m_log_dt, m_b_re, m_b_im, m_c_re, m_c_im, m_d_skip, m_w_glu, m_b_glu, m_q_a_norm, m_w_q_b, m_kv_a_norm, m_w_kv_b, m_out_norm_ssm, m_out_norm_attn, m_w_out, m_ffn_norm, m_w_up, m_conv_w, m_conv_b, m_w_down, m_final_norm, v_meta_tokens, v_mix_norm, v_w_in, v_lam_re, v_lam_im, v_log_dt, v_b_re, v_b_im, v_c_re, v_c_im, v_d_skip, v_w_glu, v_b_glu, v_q_a_norm, v_w_q_b, v_kv_a_norm, v_w_kv_b, v_out_norm_ssm, v_out_norm_attn, v_w_out, v_ffn_norm, v_w_up, v_conv_w, v_conv_b, v_w_down, v_final_norm):
    given = dict(locals())
    w = {n: given[n] for n in WEIGHTS}
    m = {n: given['m_' + n] for n in WEIGHTS}
    v = {n: given['v_' + n] for n in WEIGHTS}
    dev = 4 * lax.axis_index("x") + 2 * lax.axis_index("y") + lax.axis_index("c")

    shard = {
        'w_in': jnp.pad(w_in[0], ((0, 0), (0, D_IN_PAD - D_IN))),
        'w_glu': w_glu[0],
        'w_q_b': jnp.pad(w_q_b[0], ((0, 0), (0, HEAD_PAD - QK_NOPE - QK_ROPE))),
        'w_kv_b': w_kv_b[0],
        'w_out': w_out[0],
        'w_up': w_up[0],
        'w_down': w_down[0],
    }
    core = lax.axis_index("c").astype(jnp.int32).reshape(1)
    chip = (2 * lax.axis_index("x") + lax.axis_index("y")).astype(jnp.int32).reshape(1)
    mid, late = ['w_q_b', 'w_kv_b', 'w_glu'], ['w_out', 'w_up', 'w_down']
    shard = {n: a.astype(BF16) for n, a in shard.items()}

    gathered = _all_gather("gather_early", [shard['w_in'], meta_tokens, jnp.pad(conv_w[0], ((0, 5), (0, 0)))])
    wf = {'w_in': gathered[0].reshape(-1, D_IN_PAD)}
    meta = gathered[1].transpose(1, 0, 2).reshape(N_META, D)
    wf['conv_w'] = gathered[2][:, :3].transpose(1, 0, 2).reshape(3, D_FF)
    mid_token, wait_mid = _split_exchange("gather_mid", [shard[n] for n in mid],
                                          [(N_DEV,) + shard[n].shape for n in mid], _gather_copies, 4,
                                          after=gathered[0])
    gather_token, wait_late = _split_exchange("gather_late", [shard[n] for n in late],
                                              [(N_DEV,) + shard[n].shape for n in late], _gather_copies, 4,
                                              after=mid_token)

    def mid_weights(after):
        mine, landed = wait_mid(after)
        full = _gather_forward("gather_forward_mid",
                               [lax.dynamic_update_slice(a, s[None], (dev, 0, 0)) for s, a in zip(mine, landed)])
        return {'w_q_b': full[0], 'w_kv_b': full[1], 'w_glu': full[2].reshape(-1, D_SSM)}

    def late_weights(after):
        mine, landed = wait_late(after)
        full = [lax.dynamic_update_slice(a, s[None], (dev, 0, 0)) for s, a in zip(mine, landed)]
        full_out, = _gather_forward("gather_forward_w_out", full[:1])
        token, wait_ffn = _split_exchange("gather_forward_ffn", full[1:], None, _forward_copies, 3, after=full_out)

        def ffn_weights(after):
            full_up, full_down = wait_ffn(after)[0]
            return full_up, full_down.reshape(-1, D)

        return full_out.reshape(-1, D), token, ffn_weights

    pending, between_cores = [], {}

    def send_grads(names, arg, stage):
        if stage == 'cores':
            token, between_cores[names[0]] = _split_exchange(
                "reduce_cores_" + names[0], arg, [(4,) + a.shape[1:] for a in arg], _cores_copies, 4)
            return token
        mine, got = between_cores.pop(names[0])(arg) if stage == 'chips' else (
            arg, _exchange_cores("reduce_cores_" + names[0], arg))
        sums = [_pair_sum("sum_cores_" + n, a, g, core) for n, a, g in zip(names, mine, got)]
        token, wait = _split_exchange("reduce_chips_" + names[0], sums, [(3,) + s.shape[1:] for s in sums],
                                      _chips_copies, 3)
        pending.append((names, wait))
        return token

    small_sent = []
    s5_bc = ['b_re', 'b_im', 'c_re', 'c_im']

    def send_small(sm, dcw4, loss_part):
        names = [n for n in SMALL if n in sm]
        arrs = [sm[n].reshape(-1, 128).astype(BF16) if n in s5_bc else sm[n] for n in names] + [dcw4, loss_part]
        token, wait = _split_exchange("gather_small", arrs, [(N_DEV,) + a.shape for a in arrs], _to_all_copies, 7,
                                      all_of=lambda a, land: land.at[pl.ds(0, N_DEV - 1)])
        small_sent.append((names, wait))
        return token

    small = {n: w[n] for n in SMALL}
    small['mix_norm'] = mix_norm + gather_token[0:1, 0:1]
    grad_x, dmeta, d_mix_norm, big = _local_step(x[0], loss_target[0], meta, small, wf, mid_weights, late_weights,
                                                  send_grads, send_small)
    grads, deltas, new_m, new_v = {}, {}, {}, {}

    def keep(n, outs):
        grads[n], deltas[n], new_m[n], new_v[n] = (o.reshape(w[n].shape) for o in outs)

    rest = list(big)
    from_core = _exchange_cores("reduce_cores", [big[n] for n in rest])
    chip_sums = [_pair_sum("sum_cores_" + n, big[n], got, core) for n, got in zip(rest, from_core)]
    last = chip_sums + [d_mix_norm, dmeta]
    n_big = len(chip_sums)
    last_token, wait_last = _split_exchange(
        "reduce_last", last, [(3,) + s.shape[1:] for s in chip_sums] + [(N_DEV,) + a.shape for a in last[n_big:]],
        lambda a, *at: _chips_copies(a, *at) if a < n_big else _to_all_copies(a, *at), 7,
        all_of=lambda a, land: land if a < n_big else land.at[pl.ds(0, N_DEV - 1)])

    def update(n, own, got):
        if n == 'w_up':
            outs = _adamw_big("adamw_" + n, own, got, chip, w[n][0].T, m[n][0].T, v[n][0].T, last_token)
            return keep(n, [o.T for o in outs])
        cols = w[n].shape[2]
        keep(n, _adamw_big("adamw_" + n, own[:, :, :cols], got[:, :, :cols], chip, w[n][0], m[n][0], v[n][0],
                           last_token))

    for names, wait in pending:
        for n, own, got in zip(names, *wait(chip_sums[0])):
            update(n, own, got)
    updated = [new_v[n].reshape(-1)[:1] for names, _ in pending for n in names]
    sent, landed = wait_last(functools.reduce(jnp.add, updated))
    for n, own, got in zip(rest, sent, landed):
        update(n, own, got)

    device = dev.astype(jnp.int32).reshape(1)
    outs = _adamw_multi("adamw_last", [(landed[n_big], mix_norm, m['mix_norm'], v['mix_norm'])],
                        sums=[landed[n_big + 1]], own=sent[n_big:], dev=device)
    keep('mix_norm', outs[:4])
    g_meta = outs[4]
    names, wait = small_sent[0]
    mine, landed = wait(grads['mix_norm'])
    parts, own = dict(zip(names + ['conv_w', 'loss'], landed)), dict(zip(names + ['conv_w', 'loss'], mine))
    wide = [n for n in names if n not in s5_bc]
    summed = ['conv_w', 'loss'] + s5_bc
    outs = _adamw_multi("adamw_small",
                        [(parts[n],) + tuple(t[n].reshape(own[n].shape) for t in (w, m, v)) for n in wide],
                        sums=[parts[n] for n in summed], own=[own[n] for n in wide + summed], dev=device)
    for i, n in enumerate(wide):
        keep(n, outs[4 * i:4 * i + 4])
    g_cw4, loss = outs[-6:-4]
    rows_gc = lambda a, n: (a.reshape(N_GROUPS, SSM_STATE, SSM_GROUP).transpose(0, 2, 1) if n[0] == 'b' else a
                            ).reshape(N_GROUPS * SSM_GROUP, SSM_STATE)
    outs = _adamw_multi("adamw_s5_bc", [(g.reshape(1, N_GROUPS * SSM_GROUP, SSM_STATE),)
                                        + tuple(rows_gc(t[n], n) for t in (w, m, v))
                                        for n, g in zip(s5_bc, outs[-4:])])
    for i, n in enumerate(s5_bc):
        back = outs[4 * i:4 * i + 4]
        if n[0] == 'b':
            back = [o.reshape(N_GROUPS, SSM_GROUP, SSM_STATE).transpose(0, 2, 1) for o in back]
        keep(n, back)

    g_meta = lax.dynamic_slice(g_meta, (0, dev * (D // N_DEV)), (N_META, D // N_DEV))
    g_conv = lax.dynamic_slice(g_cw4.transpose(1, 0, 2).reshape(3, D_FF), (0, dev * (D_FF // N_DEV)),
                               (3, D_FF // N_DEV))
    rows8 = lambda a: jnp.pad(a.reshape(3, D_FF // N_DEV), ((0, 5), (0, 0)))
    outs = _adamw_multi("adamw_cols", [(g_meta[None], meta_tokens, m['meta_tokens'], v['meta_tokens']),
                                       (rows8(g_conv)[None], rows8(conv_w), rows8(m['conv_w']), rows8(v['conv_w']))])
    keep('meta_tokens', outs[:4])
    keep('conv_w', [o[:3] for o in outs[4:]])

    return (loss[0, 0], grad_x[None], *[grads[n] for n in WEIGHTS], *[deltas[n] for n in WEIGHTS],
            *[new_m[n] for n in WEIGHTS], *[new_v[n] for n in WEIGHTS])
```

```python
import functools
import math

import jax
import jax.numpy as jnp
from jax import lax
from jax.experimental import pallas as pl
from jax.experimental.pallas import tpu as pltpu

F32, BF16 = jnp.float32, jnp.bfloat16

N_DEV = 8
N_META, SEQ, D = 16, 2048, 2048
T = N_META + SEQ
TR = 688
EPS = 1e-6
D_SSM, Q_LORA, KV_LORA, QK_ROPE = 1024, 512, 256, 64
D_IN = D_SSM + Q_LORA + KV_LORA + QK_ROPE
D_IN_PAD = 1920
N_HEADS, QK_NOPE, V_HEAD = 8, 128, 128
HEAD_PAD = 256
SM_SCALE = 1.0 / math.sqrt(QK_NOPE + QK_ROPE)
CHUNK = 64
N_GROUPS, SSM_GROUP, SSM_STATE = 64, 16, 64
N_SEG = 8
SEG = T // N_SEG
GQ = 8
D_FF = 5504
FF_PIECE = 1376
ROPE_BASE = 10000.0
LR, B1, B2, ADAM_EPS, WD, STEP = 0.001, 0.9, 0.999, 1e-08, 0.01, 10
VMEM_CAP = 60 * 1024 * 1024
MESH = pl.DeviceIdType.MESH

WEIGHTS = ['meta_tokens', 'mix_norm', 'w_in', 'lam_re', 'lam_im', 'log_dt', 'b_re', 'b_im', 'c_re', 'c_im',
           'd_skip', 'w_glu', 'b_glu', 'q_a_norm', 'w_q_b', 'kv_a_norm', 'w_kv_b', 'out_norm_ssm',
           'out_norm_attn', 'w_out', 'ffn_norm', 'w_up', 'conv_w', 'conv_b', 'w_down', 'final_norm']
BIG = ['w_in', 'w_glu', 'w_q_b', 'w_kv_b', 'w_out', 'w_up', 'w_down']
SMALL = [n for n in WEIGHTS if n not in BIG and n not in ('meta_tokens', 'conv_w')]


def _nbytes(shape, dtype):
    return math.prod(shape) * jnp.dtype(dtype).itemsize


def _in_hbm(a):
    return pltpu.with_memory_space_constraint(a, pltpu.HBM) if a.size * a.dtype.itemsize >= (1 << 20) else a


def _out_hbm(shape, dtype):
    big = math.prod(shape) * jnp.dtype(dtype).itemsize >= (1 << 20)
    return pltpu.HBM(shape, dtype) if big else jax.ShapeDtypeStruct(shape, dtype)


def _params(sem, need):
    return pltpu.CompilerParams(dimension_semantics=("arbitrary",) * sem,
                                vmem_limit_bytes=int(min(VMEM_CAP, max(need, 16 * 1024 * 1024))))


_DIMS = {'nn': (((1,), (0,)), ((), ())), 'nt': (((1,), (1,)), ((), ())), 'tn': (((0,), (0,)), ((), ()))}


def _mm(name, a, b, mode, tm, tn, tk, out_dtype, stack=None, res=None):
    sa, sb, so = stack in ('a_out', 'ab_red'), stack in ('b_out', 'ab_red'), stack in ('a_out', 'b_out')
    nj = a.shape[0] if sa else (b.shape[0] if sb else 1)
    a2, b2 = a.shape[-2:], b.shape[-2:]
    if mode == 'nn':
        (m, k), (k2, n) = a2, b2
    elif mode == 'nt':
        (m, k), (n, k2) = a2, b2
    else:
        (k, m), (k2, n) = a2, b2
    assert k == k2 and m % tm == 0 and n % tn == 0 and k % tk == 0, (name, a.shape, b.shape)
    n_jo, n_jr, n_k = (nj if so else 1), (nj if stack == 'ab_red' else 1), k // tk
    grid = (n_jo, m // tm, n // tn, n_jr, n_k)
    a_blk, a_idx = ((tk, tm), lambda i, j, kk: (kk, i)) if mode == 'tn' else ((tm, tk), lambda i, j, kk: (i, kk))
    b_blk, b_idx = ((tn, tk), lambda i, j, kk: (j, kk)) if mode == 'nt' else ((tk, tn), lambda i, j, kk: (kk, j))

    def spec(blk, idx, stacked, on_out):
        if not stacked:
            return pl.BlockSpec(blk, lambda jo, i, j, jr, kk: idx(i, j, kk))
        return pl.BlockSpec((None,) + blk, lambda jo, i, j, jr, kk: ((jo if on_out else jr),) + idx(i, j, kk))

    o_idx = lambda i, j, kk: (i, j)
    in_specs = [spec(a_blk, a_idx, sa, stack == 'a_out'), spec(b_blk, b_idx, sb, stack == 'b_out')]
    operands = [a, b]
    if res is not None:
        in_specs.append(spec((tm, tn), o_idx, False, False))
        operands.append(res)
    n_red = n_jr * n_k
    dims = _DIMS[mode]

    def body(*refs):
        a_ref, b_ref = refs[0], refs[1]
        res_ref = refs[2] if res is not None else None
        o_ref = refs[3] if res is not None else refs[2]
        part = lax.dot_general(a_ref[...].astype(BF16), b_ref[...].astype(BF16), dims,
                               preferred_element_type=F32)

        def finish(total):
            if res_ref is not None:
                total = total + res_ref[...]
            o_ref[...] = total.astype(o_ref.dtype)

        if n_red == 1:
            finish(part)
        else:
            acc_ref = refs[-1]
            step = pl.program_id(3) * n_k + pl.program_id(4)

            @pl.when(step == 0)
            def _():
                acc_ref[...] = part

            @pl.when(step > 0)
            def _():
                acc_ref[...] += part

            @pl.when(step == n_red - 1)
            def _():
                finish(acc_ref[...])

    out_shape = ((nj,) if so else ()) + (m, n)
    need = 2 * (_nbytes(a_blk, a.dtype) + _nbytes(b_blk, b.dtype) + _nbytes((tm, tn), out_dtype)
                + (_nbytes((tm, tn), F32) if res is not None else 0)) + 3 * _nbytes((tm, tn), F32) + (4 << 20)
    return pl.pallas_call(
        body, name=name, grid=grid, in_specs=in_specs, out_specs=spec((tm, tn), o_idx, so, True),
        out_shape=_out_hbm(out_shape, out_dtype),
        scratch_shapes=[pltpu.VMEM((tm, tn), F32)] if n_red > 1 else [],
        compiler_params=_params(5, need),
    )(*map(_in_hbm, operands))


def _rowwise(name, fn, row_ins, const_ins, row_outs, acc_outs, tr=TR):
    n_ri, n_ci, n_ro, n_ao = len(row_ins), len(const_ins), len(row_outs), len(acc_outs)

    def body(*refs):
        ri, ci = refs[:n_ri], refs[n_ri:n_ri + n_ci]
        ro, ao = refs[n_ri + n_ci:n_ri + n_ci + n_ro], refs[n_ri + n_ci + n_ro:]
        i = pl.program_id(0)
        outs = fn(i * tr, *[r[...] for r in ri], *[r[...] for r in ci])
        for r, v in zip(ro, outs[:n_ro]):
            r[...] = v.astype(r.dtype)
        if n_ao:
            @pl.when(i == 0)
            def _():
                for r, v in zip(ao, outs[n_ro:]):
                    r[...] = v

            @pl.when(i > 0)
            def _():
                for r, v in zip(ao, outs[n_ro:]):
                    r[...] += v

    in_specs = [pl.BlockSpec((tr, w), functools.partial(lambda cb, i: (i, cb), cb)) for _, w, cb in row_ins]
    in_specs += [pl.BlockSpec(c.shape, functools.partial(lambda nd, i: (0,) * nd, c.ndim)) for c in const_ins]
    out_specs = [pl.BlockSpec((tr, w), lambda i: (i, 0)) for w, _ in row_outs]
    out_specs += [pl.BlockSpec((1, w), lambda i: (0, 0)) for w in acc_outs]
    out_shape = [_out_hbm((T, w), dt) for w, dt in row_outs]
    out_shape += [jax.ShapeDtypeStruct((1, w), F32) for w in acc_outs]
    need = 2 * (sum(_nbytes((tr, w), a.dtype) for a, w, _ in row_ins) + sum(_nbytes((tr, w), dt) for w, dt in row_outs))
    need = 3 * need + (8 << 20)
    return pl.pallas_call(
        body, name=name, grid=(T // tr,), in_specs=in_specs, out_specs=out_specs, out_shape=out_shape,
        compiler_params=_params(1, need),
    )(*[_in_hbm(a) for a, _, _ in row_ins], *const_ins)


def _rms_fwd_fn(row0, h, g):
    h = h.astype(F32)
    r = lax.rsqrt(jnp.mean(h * h, axis=-1, keepdims=True) + EPS)
    return (h * r) * g, r


def _rms_bwd(dxn, h, r, g):
    xhat = h * r
    dxg = dxn * g
    dh = r * (dxg - xhat * jnp.mean(dxg * xhat, axis=-1, keepdims=True))
    return dh, jnp.sum(dxn * xhat, axis=0, keepdims=True)


def _rms_bwd_fn(row0, dxn, h, r, g):
    return _rms_bwd(dxn.astype(F32), h.astype(F32), r, g)


def _rms_bwd_res_fn(row0, dxn, h, r, res, g):
    dh, dg = _rms_bwd(dxn.astype(F32), h.astype(F32), r, g)
    return dh + res, dg


def _gelu_fn(row0, y):
    return (jax.nn.gelu(y),)


def _glu_fwd_fn(row0, y, gl, b, gain):
    ya = jax.nn.gelu(y) * jax.nn.sigmoid(gl + b)
    return _rms_fwd_fn(row0, ya, gain)


def _glu_bwd_fn(row0, dyan, y, gl, ra, b, gain):
    g = jax.nn.gelu(y)
    s = jax.nn.sigmoid(gl + b)
    dya, dgain = _rms_bwd(dyan, g * s, ra, gain)
    dgl = dya * g * s * (1.0 - s)
    return dgl, dya * s, dgain, jnp.sum(dgl, axis=0, keepdims=True)


def _gelu_bwd_fn(row0, dg, y):
    return (jax.vjp(jax.nn.gelu, y)[1](dg)[0],)


def _loss_fn(row0, h, tgt, gain):
    r = lax.rsqrt(jnp.mean(h * h, axis=-1, keepdims=True) + EPS)
    xhat = h * r
    rows = row0 + lax.broadcasted_iota(jnp.int32, h.shape, 0)
    err = jnp.where(rows >= N_META, xhat * gain - tgt, 0.0)
    loss = jnp.full((1, 128), jnp.sum(err * err) * (0.5 / D), F32)
    dh, dgain = _rms_bwd(err * (1.0 / D), h, r, gain)
    return dh, dgain, loss


def _rope_tables():
    pos = jnp.arange(T, dtype=jnp.int32)
    inv_freq = 1.0 / (ROPE_BASE ** (jnp.arange(0, QK_ROPE, 2, dtype=F32) / QK_ROPE))
    ang = pos.astype(F32)[:, None] * inv_freq[None, :]
    cos, sin, z32, z64 = jnp.cos(ang), jnp.sin(ang), jnp.zeros((T, 32), F32), jnp.zeros((T, 64), F32)
    return (jnp.concatenate([cos, cos, z64], 1), jnp.concatenate([-sin, z32, z64], 1),
            jnp.concatenate([z32, sin, z64], 1))


def _rot(x, cc, s1, s2):
    return x * cc + pltpu.roll(x, 96, 1) * s1 + pltpu.roll(x, 32, 1) * s2


def _derot(d, cc, s1, s2):
    return d * cc + pltpu.roll(d * s1, 32, 1) + pltpu.roll(d * s2, 96, 1)


def _chunk_of(pos):
    return jnp.where(pos < N_META, 0, 1 + ((pos - N_META) >> 6))


def _key_len(t):
    last_chunk = 1 + ((t + 1) * TR - 1 - N_META) // CHUNK
    return min(T, pl.cdiv(N_META + last_chunk * CHUNK, 128) * 128)


def _attn_probs(t, q_ref, kv_ref, kpe_ref, tq, tk):
    kl = _key_len(t)
    q = q_ref[...]
    qn = q[:, :QK_NOPE].astype(BF16)
    qp = _rot(q[:, QK_NOPE:], tq[0][...], tq[1][...], tq[2][...]).astype(BF16)
    kn = kv_ref[:kl, :QK_NOPE].astype(BF16)
    v = kv_ref[:kl, QK_NOPE:].astype(BF16)
    kp = _rot(kpe_ref[:kl, :], tk[0][:kl, :], tk[1][:kl, :], tk[2][:kl, :]).astype(BF16)
    s = lax.dot_general(qn, kn, _DIMS['nt'], preferred_element_type=F32)
    s = s + lax.dot_general(qp, kp, _DIMS['nt'], preferred_element_type=F32)
    qc = _chunk_of(t * TR + lax.broadcasted_iota(jnp.int32, (TR, 1), 0))
    kc = _chunk_of(lax.broadcasted_iota(jnp.int32, (1, kl), 1))
    s = jnp.where(kc <= qc, s * SM_SCALE, jnp.finfo(F32).min)
    e = jnp.exp(s - jnp.max(s, axis=-1, keepdims=True))
    return qn, qp, kn, kp, v, e, 1.0 / jnp.sum(e, axis=-1, keepdims=True)


def _per_q_tile(fn):
    for t in range(T // TR):
        pl.when(pl.program_id(1) == t)(functools.partial(fn, t))


def _attn_specs(z, tabs):
    q_spec = pl.BlockSpec((None, TR, HEAD_PAD), lambda h, i: (h, i, 0))
    kv_spec = pl.BlockSpec((None, T, HEAD_PAD), lambda h, i: (h, 0, 0))
    kpe_spec = pl.BlockSpec((T, 128), lambda h, i: (0, (D_IN - QK_ROPE) // 128))
    tq = [pl.BlockSpec((TR, 128), lambda h, i: (i, 0))] * 3
    tk = [pl.BlockSpec((T, 128), lambda h, i: (0, 0))] * 3
    return [q_spec, kv_spec, kpe_spec] + tq + tk


def _attn_fwd(q, kv, z, tabs):
    def body(q_ref, kv_ref, kpe_ref, c1, c2, c3, k1, k2, k3, o_ref):
        def tile(t):
            _, _, _, _, v, e, inv = _attn_probs(t, q_ref, kv_ref, kpe_ref, (c1, c2, c3), (k1, k2, k3))
            o_ref[...] = jnp.dot(e.astype(BF16), v, preferred_element_type=F32) * inv

        _per_q_tile(tile)

    return pl.pallas_call(
        body, name="attn_fwd", grid=(N_HEADS, T // TR), in_specs=_attn_specs(z, tabs),
        out_specs=pl.BlockSpec((TR, V_HEAD), lambda h, i: (i, h)),
        out_shape=_out_hbm((T, N_HEADS * V_HEAD), F32),
        compiler_params=_params(2, 48 << 20),
    )(_in_hbm(q), _in_hbm(kv), _in_hbm(z), *tabs, *tabs)


def _attn_bwd(q, kv, z, tabs, dyb):
    def body(q_ref, kv_ref, kpe_ref, c1, c2, c3, k1, k2, k3, do_ref, dq_ref, dkv_ref, dkpe_ref):
        @pl.when(pl.program_id(1) == 0)
        def _():
            dkv_ref[...] = jnp.zeros(dkv_ref.shape, F32)
            dkpe_ref[...] = jnp.zeros(dkpe_ref.shape, F32)

        def tile(t):
            kl = _key_len(t)
            qn, qp, kn, kp, v, e, inv = _attn_probs(t, q_ref, kv_ref, kpe_ref, (c1, c2, c3), (k1, k2, k3))
            do = do_ref[...]
            dp = lax.dot_general(do.astype(BF16), v, _DIMS['nt'], preferred_element_type=F32)
            delta = jnp.sum(e * dp, axis=-1, keepdims=True) * inv
            ds = (e * ((dp - delta) * (inv * SM_SCALE))).astype(BF16)
            pb, do = e.astype(BF16), (do * inv).astype(BF16)
            dq_ref[:, :QK_NOPE] = jnp.dot(ds, kn, preferred_element_type=F32).astype(dq_ref.dtype)
            dqp = _derot(jnp.dot(ds, kp, preferred_element_type=F32), c1[...], c2[...], c3[...])
            dq_ref[:, QK_NOPE:] = dqp.astype(dq_ref.dtype)
            dkv_ref[:kl, :QK_NOPE] += lax.dot_general(ds, qn, _DIMS['tn'], preferred_element_type=F32)
            dkv_ref[:kl, QK_NOPE:] += lax.dot_general(pb, do, _DIMS['tn'], preferred_element_type=F32)
            dkpe_ref[:kl, :] += lax.dot_general(ds, qp, _DIMS['tn'], preferred_element_type=F32)

        _per_q_tile(tile)

    return pl.pallas_call(
        body, name="attn_bwd", grid=(N_HEADS, T // TR),
        in_specs=_attn_specs(z, tabs) + [pl.BlockSpec((TR, V_HEAD), lambda h, i: (i, h))],
        out_specs=[pl.BlockSpec((None, TR, HEAD_PAD), lambda h, i: (h, i, 0)),
                   pl.BlockSpec((None, T, HEAD_PAD), lambda h, i: (h, 0, 0)),
                   pl.BlockSpec((None, T, 128), lambda h, i: (h, 0, 0))],
        out_shape=[_out_hbm((N_HEADS, T, HEAD_PAD), BF16), _out_hbm((N_HEADS, T, HEAD_PAD), F32),
                   _out_hbm((N_HEADS, T, 128), F32)],
        compiler_params=_params(2, 56 << 20),
    )(_in_hbm(q), _in_hbm(kv), _in_hbm(z), *tabs, *tabs, _in_hbm(dyb))


def _kpe_bwd(dkpe, tabs):
    def body(d_ref, c1, c2, c3, o_ref):
        d = d_ref[0]
        for h in range(1, N_HEADS):
            d = d + d_ref[h]
        o_ref[...] = _derot(d, c1[...], c2[...], c3[...]).astype(o_ref.dtype)

    tile = pl.BlockSpec((TR, 128), lambda i: (i, 0))
    return pl.pallas_call(
        body, name="kpe_bwd", grid=(T // TR,),
        in_specs=[pl.BlockSpec((N_HEADS, TR, 128), lambda i: (0, i, 0)), tile, tile, tile], out_specs=tile,
        out_shape=jax.ShapeDtypeStruct((T, 128), BF16), compiler_params=_params(1, 16 << 20),
    )(_in_hbm(dkpe), *tabs)


def _s5_prep_math(lam_re, lam_im, log_dt, bt_re, bt_im):
    dt = jnp.exp(log_dt)
    mag = jnp.exp(lam_re * dt)
    a_re, a_im = mag * jnp.cos(lam_im * dt), mag * jnp.sin(lam_im * dt)
    den = lam_re * lam_re + lam_im * lam_im
    w_re = ((a_re - 1.0) * lam_re + a_im * lam_im) / den
    w_im = (a_im * lam_re - (a_re - 1.0) * lam_im) / den
    return a_re, a_im, w_re[None] * bt_re - w_im[None] * bt_im, w_re[None] * bt_im + w_im[None] * bt_re


def _eye_groups():
    return (lax.broadcasted_iota(jnp.int32, (N_GROUPS, N_GROUPS), 0)
            == lax.broadcasted_iota(jnp.int32, (N_GROUPS, N_GROUPS), 1)).astype(F32)


def _row_to_col(row):
    return jnp.sum(_eye_groups() * row, axis=1, keepdims=True)


def _s5_prep(lam_re, lam_im, log_dt, bt_re, bt_im):
    def body(lr, li, ld, br, bi, ar, ai, bbr, bbi):
        ar[...], ai[...], bbr[...], bbi[...] = _s5_prep_math(lr[...], li[...], _row_to_col(ld[...]), br[...], bi[...])

    gp, cgp = jax.ShapeDtypeStruct((N_GROUPS, SSM_STATE), F32), jax.ShapeDtypeStruct(bt_re.shape, F32)
    return pl.pallas_call(body, name="s5_prep", out_shape=[gp, gp, cgp, cgp])(lam_re, lam_im, log_dt, bt_re, bt_im)


def _s5_prep_bwd(lam_re, lam_im, log_dt, bt_re, bt_im, da_re, da_im, dbb_re, dbb_im):
    def body(lr, li, ld, br, bi, dar, dai, dbr, dbi, o1, o2, o3, o4, o5):
        _, vjp = jax.vjp(_s5_prep_math, lr[...], li[...], _row_to_col(ld[...]), br[...], bi[...])
        o1[...], o2[...], dld, o4[...], o5[...] = vjp((dar[...], dai[...], dbr[...], dbi[...]))
        o3[...] = jnp.sum(_eye_groups() * dld, axis=0, keepdims=True)

    ins = (lam_re, lam_im, log_dt, bt_re, bt_im)
    return pl.pallas_call(body, name="s5_prep_bwd", out_shape=[jax.ShapeDtypeStruct(a.shape, F32) for a in ins])(
        *ins, da_re, da_im, dbb_re, dbb_im)


def _cmul(ar, ai, br, bi):
    return ar * br - ai * bi, ar * bi + ai * br


def _seg_rows(i):
    return pl.ds(pl.multiple_of(i * N_SEG, N_SEG), N_SEG)


def _scan(xr, xi, ar, ai, reverse):
    zero = jnp.zeros(ar.shape, F32)

    def local(j, carry):
        rows = _seg_rows(SEG - 1 - j if reverse else j)
        nr, ni = _cmul(ar, ai, *carry)
        nr, ni = nr + xr[rows, :], ni + xi[rows, :]
        xr[rows, :], xi[rows, :] = nr, ni
        return nr, ni

    er, ei = lax.fori_loop(0, SEG, local, (zero, zero))
    pr, pi = ar, ai
    for _ in range(8):
        pr, pi = _cmul(pr, pi, pr, pi)
    pr, pi = _cmul(*_cmul(pr, pi, ar, ai), ar, ai)
    row = lax.broadcasted_iota(jnp.int32, ar.shape, 0)
    edge, shift = (N_SEG - 1, N_SEG - 1) if reverse else (0, 1)
    hr, hi = zero, zero
    for _ in range(N_SEG - 1):
        tr_, ti_ = _cmul(pr, pi, hr, hi)
        hr = jnp.where(row == edge, 0.0, pltpu.roll(tr_ + er, shift, 0))
        hi = jnp.where(row == edge, 0.0, pltpu.roll(ti_ + ei, shift, 0))

    def fix(j, carry):
        rows = _seg_rows(SEG - 1 - j if reverse else j)
        xr[rows, :] += carry[0]
        xi[rows, :] += carry[1]
        return _cmul(ar, ai, *carry)

    lax.fori_loop(0, SEG, fix, _cmul(ar, ai, hr, hi))
    return hr, hi


def _split(x):
    hi = x.astype(BF16)
    return hi, (x - hi.astype(F32)).astype(BF16)


def _dot2(xs, ys, mode='nn'):
    d = lambda p, q: lax.dot_general(p, q, _DIMS[mode], preferred_element_type=F32)
    return d(xs[0], ys[0]) + d(xs[1], ys[0])


def _group_masks():
    ns = GQ * SSM_STATE
    rep = (lax.broadcasted_iota(jnp.int32, (SSM_STATE, ns), 1) % SSM_STATE
           == lax.broadcasted_iota(jnp.int32, (SSM_STATE, ns), 0)).astype(BF16)
    own = (lax.broadcasted_iota(jnp.int32, (128, ns), 0) // SSM_GROUP
           == lax.broadcasted_iota(jnp.int32, (128, ns), 1) // SSM_STATE)
    return rep, own


def _block_diag(x, rep, own):
    dot = lambda p: jnp.dot(p, rep, preferred_element_type=F32)
    return tuple(jnp.where(own, dot(part), 0.0).astype(BF16) for part in _split(x))


def _block_rows(full, rep, own):
    masked = _split(jnp.where(own, full, 0.0))
    dot = lambda p: lax.dot_general(p, rep, _DIMS['nt'], preferred_element_type=F32)
    return dot(masked[0]) + dot(masked[1])


def _s5_specs():
    col = pl.BlockSpec((T, 128), lambda q: (0, q))
    a_spec = pl.BlockSpec((None, 2, N_SEG, GQ * SSM_STATE), lambda q: (q, 0, 0, 0))
    rows = pl.BlockSpec((128, SSM_STATE), lambda q: (q, 0))
    d_spec = pl.BlockSpec((1, 128), lambda q: (0, q))
    return col, a_spec, rows, d_spec


def _s5_fwd(u, a8, bb_re, bb_im, c_re, c_im, dsk):
    def body(u_ref, a_ref, bre_ref, bim_ref, cre_ref, cim_ref, d_ref, y_ref, hre, him):
        u_ = u_ref[...]
        us = _split(u_)
        rep, own = _group_masks()
        hre[...] = _dot2(us, _block_diag(bre_ref[...], rep, own))
        him[...] = _dot2(us, _block_diag(bim_ref[...], rep, own))
        _scan(hre, him, a_ref[0], a_ref[1], reverse=False)
        dot = lambda h, c: lax.dot_general(h[...].astype(BF16), _block_diag(c[...], rep, own)[0], _DIMS['nt'],
                                           preferred_element_type=F32)
        y_ref[...] = dot(hre, cre_ref) - dot(him, cim_ref) + d_ref[...] * u_

    col, a_spec, rows, d_spec = _s5_specs()
    return pl.pallas_call(
        body, name="s5_fwd", grid=(N_GROUPS // GQ,), in_specs=[col, a_spec, rows, rows, rows, rows, d_spec],
        out_specs=col, out_shape=_out_hbm((T, D_SSM), F32),
        scratch_shapes=[pltpu.VMEM((T, GQ * SSM_STATE), F32)] * 2,
        compiler_params=_params(1, 40 << 20),
    )(_in_hbm(u), a8, bb_re, bb_im, c_re, c_im, dsk)


def _s5_bwd(u, dy, a8, bb_re, bb_im, c_re, c_im, dsk):
    def body(u_ref, dy_ref, a_ref, bre_ref, bim_ref, cre_ref, cim_ref, d_ref,
             du_ref, dbre_ref, dbim_ref, dcre_ref, dcim_ref, da_ref, dd_ref, hre, him, lre, lim):
        u_, dy_ = u_ref[...], dy_ref[...]
        us, dys = _split(u_), _split(dy_)
        ar, ai = a_ref[0], a_ref[1]
        rep, own = _group_masks()
        bres, bims = _block_diag(bre_ref[...], rep, own), _block_diag(bim_ref[...], rep, own)
        hre[...] = _dot2(us, bres)
        him[...] = _dot2(us, bims)
        h0r, h0i = _scan(hre, him, ar, ai, reverse=False)
        lre[...] = _dot2(dys, _block_diag(cre_ref[...], rep, own))
        lim[...] = -_dot2(dys, _block_diag(cim_ref[...], rep, own))
        _scan(lre, lim, ar, -ai, reverse=True)

        def acc_da(i, carry):
            lr, li = lre[_seg_rows(i), :], lim[_seg_rows(i), :]
            pr, pi = hre[_seg_rows(i - 1), :], him[_seg_rows(i - 1), :]
            return carry[0] + lr * pr + li * pi, carry[1] + li * pr - lr * pi

        lr, li = lre[_seg_rows(0), :], lim[_seg_rows(0), :]
        dar, dai = lax.fori_loop(1, SEG, acc_da, (lr * h0r + li * h0i, li * h0r - lr * h0i))
        da_ref[0:1, :] = jnp.sum(dar, axis=0, keepdims=True)
        da_ref[1:2, :] = jnp.sum(dai, axis=0, keepdims=True)
        dot = lambda p, q, mode: lax.dot_general(p, q, _DIMS[mode], preferred_element_type=F32)
        lr1, li1 = lre[...].astype(BF16), lim[...].astype(BF16)
        du_ref[...] = dy_ * d_ref[...] + dot(lr1, bres[0], 'nt') + dot(li1, bims[0], 'nt')
        dbre_ref[...] = _block_rows(dot(us[0], lr1, 'tn'), rep, own)
        dbim_ref[...] = _block_rows(dot(us[0], li1, 'tn'), rep, own)
        dcre_ref[...] = _block_rows(dot(dys[0], hre[...].astype(BF16), 'tn'), rep, own)
        dcim_ref[...] = -_block_rows(dot(dys[0], him[...].astype(BF16), 'tn'), rep, own)
        dd_ref[...] = jnp.sum(dy_ * u_, axis=0, keepdims=True)

    col, a_spec, rows, d_spec = _s5_specs()
    nq, ns = N_GROUPS // GQ, GQ * SSM_STATE
    per_row = jax.ShapeDtypeStruct((N_GROUPS * SSM_GROUP, SSM_STATE), F32)
    return pl.pallas_call(
        body, name="s5_bwd", grid=(nq,),
        in_specs=[col, col, a_spec, rows, rows, rows, rows, d_spec],
        out_specs=[col, rows, rows, rows, rows, pl.BlockSpec((None, 2, ns), lambda q: (q, 0, 0)), d_spec],
        out_shape=[_out_hbm((T, D_SSM), F32), per_row, per_row, per_row, per_row,
                   jax.ShapeDtypeStruct((nq, 2, ns), F32), jax.ShapeDtypeStruct((1, D_SSM), F32)],
        scratch_shapes=[pltpu.VMEM((T, ns), F32)] * 4,
        compiler_params=_params(1, 56 << 20),
    )(_in_hbm(u), _in_hbm(dy), a8, bb_re, bb_im, c_re, c_im, dsk)


def _to_segments(a, cols=None):
    a = a.reshape(N_SEG, SEG, a.shape[-1]).transpose(1, 0, 2)[:, :, :cols]
    return a.reshape(T, a.shape[-1])


def _from_segments(a):
    return a.reshape(SEG, N_SEG, a.shape[-1]).transpose(1, 0, 2).reshape(T, a.shape[-1])


def _ff_specs():
    up_spec = pl.BlockSpec((2, None, T, 128), lambda p, cb: (0, p, 0, cb))
    piece = pl.BlockSpec((None, T, 128), lambda p, cb: (p, 0, cb))
    cw_spec = pl.BlockSpec((None, 3, 128), lambda p, cb: (p, 0, cb))
    cb_spec = pl.BlockSpec((None, 1, 128), lambda p, cb: (p, 0, cb))
    return up_spec, piece, cw_spec, cb_spec


def _conv_gate(gate, w, cb, pad_ref):
    pad_ref[0:8, :] = jnp.zeros((8, 128), F32)
    pad_ref[8:8 + T, :] = gate
    g1, g2 = pad_ref[pl.ds(7, T), :], pad_ref[pl.ds(6, T), :]
    return w[0:1, :] * g2 + w[1:2, :] * g1 + w[2:3, :] * gate + cb, g1, g2


def _ff_act(up, cw4, cb4):
    def body(up_ref, w_ref, b_ref, o_ref, pad_ref):
        gc, _, _ = _conv_gate(up_ref[0].astype(F32), w_ref[...], b_ref[...], pad_ref)
        o_ref[...] = (jax.nn.silu(gc) * up_ref[1].astype(F32)).astype(o_ref.dtype)

    up_spec, piece, cw_spec, cb_spec = _ff_specs()
    n_cb = pl.cdiv(FF_PIECE, 128)
    return pl.pallas_call(
        body, name="ff_act", grid=(4, n_cb), in_specs=[up_spec, cw_spec, cb_spec], out_specs=piece,
        out_shape=_out_hbm((4, T, FF_PIECE), BF16),
        scratch_shapes=[pltpu.VMEM((T + 8, 128), F32)],
        compiler_params=_params(2, 24 << 20),
    )(_in_hbm(up.reshape(2, 4, T, FF_PIECE)), cw4, cb4)


def _ff_act_bwd(up, dact, cw4, cb4):
    def body(up_ref, da_ref, w_ref, b_ref, dup_ref, dw_ref, db_ref, pad_ref, pad2_ref):
        gate, val, w = up_ref[0].astype(F32), up_ref[1].astype(F32), w_ref[...]
        gc, g1, g2 = _conv_gate(gate, w, b_ref[...], pad_ref)
        sg = jax.nn.sigmoid(gc)
        da = da_ref[...].astype(F32)
        dup_ref[1] = (da * gc * sg).astype(dup_ref.dtype)
        dgc = da * val * sg * (1.0 + gc * (1.0 - sg))
        db_ref[...] = jnp.sum(dgc, axis=0, keepdims=True)
        dw_ref[0:1, :] = jnp.sum(dgc * g2, axis=0, keepdims=True)
        dw_ref[1:2, :] = jnp.sum(dgc * g1, axis=0, keepdims=True)
        dw_ref[2:3, :] = jnp.sum(dgc * gate, axis=0, keepdims=True)
        pad2_ref[0:T, :] = dgc
        pad2_ref[T:T + 8, :] = jnp.zeros((8, 128), F32)
        d1, d2 = pad2_ref[pl.ds(1, T), :], pad2_ref[pl.ds(2, T), :]
        dup_ref[0] = (w[2:3, :] * dgc + w[1:2, :] * d1 + w[0:1, :] * d2).astype(dup_ref.dtype)

    up_spec, piece, cw_spec, cb_spec = _ff_specs()
    n_cb = pl.cdiv(FF_PIECE, 128)
    dup, dw, db = pl.pallas_call(
        body, name="ff_act_bwd", grid=(4, n_cb), in_specs=[up_spec, piece, cw_spec, cb_spec],
        out_specs=[up_spec, cw_spec, cb_spec],
        out_shape=[_out_hbm((2, 4, T, FF_PIECE), BF16), jax.ShapeDtypeStruct((4, 3, FF_PIECE), F32),
                   jax.ShapeDtypeStruct((4, 1, FF_PIECE), F32)],
        scratch_shapes=[pltpu.VMEM((T + 8, 128), F32)] * 2,
        compiler_params=_params(2, 32 << 20),
    )(_in_hbm(up.reshape(2, 4, T, FF_PIECE)), _in_hbm(dact), cw4, cb4)
    return dup.reshape(8, T, FF_PIECE), dw, db


def _local_step(x, tgt, meta, p, wf, mid_weights, late_weights, send_grads, send_small):
    sm = {}
    h0 = jnp.concatenate([meta, x], axis=0)
    tgt_pad = jnp.concatenate([jnp.zeros((N_META, D), F32), tgt], axis=0)
    tabs = _rope_tables()
    row = lambda a: a.reshape(1, -1)

    xn, r0 = _rowwise("rms_mix", _rms_fwd_fn, [(h0, D, 0)], [row(p['mix_norm'])], [(D, BF16), (1, F32)], [])
    z = _mm("in_proj", xn, wf['w_in'], 'nn', TR, 640, D, F32)
    lam_re, lam_im = p['lam_re'].reshape(N_GROUPS, SSM_STATE), p['lam_im'].reshape(N_GROUPS, SSM_STATE)
    log_dt = p['log_dt'].reshape(1, N_GROUPS)
    bt_re = p['b_re'].reshape(N_GROUPS, SSM_STATE, SSM_GROUP).transpose(2, 0, 1)
    bt_im = p['b_im'].reshape(N_GROUPS, SSM_STATE, SSM_GROUP).transpose(2, 0, 1)
    c_re = p['c_re'].reshape(N_GROUPS * SSM_GROUP, SSM_STATE)
    c_im = p['c_im'].reshape(N_GROUPS * SSM_GROUP, SSM_STATE)
    a_re, a_im, bbt_re, bbt_im = _s5_prep(lam_re, lam_im, log_dt, bt_re, bt_im)
    nq, ns = N_GROUPS // GQ, GQ * SSM_STATE
    a8 = jnp.broadcast_to(jnp.stack([a_re.reshape(nq, ns), a_im.reshape(nq, ns)], 1)[:, :, None, :],
                          (nq, 2, N_SEG, ns))
    gcp = lambda t: t.transpose(1, 0, 2).reshape(N_GROUPS * SSM_GROUP, SSM_STATE)
    bb_re, bb_im = gcp(bbt_re), gcp(bbt_im)
    dsk = row(p['d_skip'])
    u_seg = _to_segments(z, D_SSM)
    y_ssm = _from_segments(_s5_fwd(u_seg, a8, bb_re, bb_im, c_re, c_im, dsk))
    wf = dict(wf, **mid_weights(y_ssm))
    g, = _rowwise("gelu", _gelu_fn, [(y_ssm, D_SSM, 0)], [], [(D_SSM, BF16)], [])
    gl = _mm("glu_proj", g, wf['w_glu'], 'nn', TR, D_SSM, D_SSM, F32)
    yan, ra = _rowwise("glu_norm", _glu_fwd_fn, [(y_ssm, D_SSM, 0), (gl, D_SSM, 0)],
                       [row(p['b_glu']), row(p['out_norm_ssm'])], [(D_SSM, BF16), (1, F32)], [])
    qn, rq, kvn, rkv = _rowwise(
        "rms_q_kv", lambda row0, zq, zkv, gq, gkv: _rms_fwd_fn(row0, zq, gq) + _rms_fwd_fn(row0, zkv, gkv),
        [(z, Q_LORA, D_SSM // Q_LORA), (z, KV_LORA, (D_SSM + Q_LORA) // KV_LORA)],
        [row(p['q_a_norm']), row(p['kv_a_norm'])], [(Q_LORA, BF16), (1, F32), (KV_LORA, BF16), (1, F32)], [])
    q = _mm("q_proj", qn, wf['w_q_b'], 'nn', T, HEAD_PAD, Q_LORA, F32, stack='b_out')
    kv = _mm("kv_proj", kvn, wf['w_kv_b'], 'nn', T, HEAD_PAD, KV_LORA, F32, stack='b_out')
    yb = _attn_fwd(q, kv, z, tabs)
    ybn, rb = _rowwise("rms_attn", _rms_fwd_fn, [(yb, D_SSM, 0)], [row(p['out_norm_attn'])],
                       [(D_SSM, BF16), (1, F32)], [])
    y = jnp.concatenate([yan, ybn], axis=1)
    wf = dict(wf)
    wf['w_out'], token, up_weights = late_weights(y)
    h1 = _mm("out_proj", y, wf['w_out'], 'nn', TR, 1024, D, F32, res=h0)
    xn2, r1 = _rowwise("rms_ffn", _rms_fwd_fn, [(h1, D, 0)], [row(p['ffn_norm']) + token[0:1, 0:1]],
                       [(D, BF16), (1, F32)], [])
    wf['w_up'], token, down_weights = up_weights(xn2)
    up = _mm("up_proj", xn2, wf['w_up'], 'nn', TR, FF_PIECE, D, BF16, stack='b_out')
    cw4 = wf['conv_w'].reshape(3, 4, FF_PIECE).transpose(1, 0, 2)
    cb4 = p['conv_b'].reshape(4, 1, FF_PIECE)
    act = _ff_act(up, cw4, cb4 + token[0:1, 0:1])
    wf['w_down'] = down_weights(act)
    wd4 = wf['w_down'].reshape(4, FF_PIECE, D)
    h2 = _mm("down_proj", act, wd4, 'nn', TR, 1024, FF_PIECE, F32, stack='ab_red', res=h1)
    dh2, sm['final_norm'], loss = _rowwise("loss", _loss_fn, [(h2, D, 0), (tgt_pad, D, 0)], [row(p['final_norm'])],
                                           [(D, F32)], [D, 128])

    big = {}
    dact = _mm("down_bwd_x", dh2, wd4, 'nt', TR, FF_PIECE, D, BF16, stack='b_out')
    g_down = _mm("down_bwd_w", act, dh2, 'tn', FF_PIECE, 512, T, BF16, stack='a_out').reshape(
        N_DEV, D_FF // N_DEV, D)
    token = send_grads(['w_down'], [g_down], 'cores')
    dup, dcw4, sm['conv_b'] = _ff_act_bwd(up, dact, cw4, cb4 + token[0:1, 0:1])
    dxn2 = _mm("up_bwd_x", dup, wf['w_up'], 'nt', TR, D, FF_PIECE, F32, stack='ab_red')
    g_up = _mm("up_bwd_w", dup, xn2, 'tn', FF_PIECE, 1024, T, BF16, stack='a_out')
    token = send_grads(['w_up'], [g_up], 'cores') + send_grads(['w_down'], dxn2, 'chips')
    dh1, sm['ffn_norm'] = _rowwise("rms_ffn_bwd", _rms_bwd_res_fn,
                                   [(dxn2, D, 0), (h1, D, 0), (r1, 1, 0), (dh2, D, 0)],
                                   [row(p['ffn_norm']) + token[0:1, 0:1]], [(D, F32)], [D])
    dy = _mm("out_bwd_x", dh1, wf['w_out'], 'nt', TR, 1024, D, F32)
    g_out = _mm("out_bwd_w", y, dh1, 'tn', 1024, 1024, T, BF16).reshape(N_DEV, D // N_DEV, D)
    dgl, dgd, sm['out_norm_ssm'], sm['b_glu'] = _rowwise(
        "glu_bwd", _glu_bwd_fn, [(dy, D_SSM, 0), (y_ssm, D_SSM, 0), (gl, D_SSM, 0), (ra, 1, 0)],
        [row(p['b_glu']), row(p['out_norm_ssm'])], [(D_SSM, BF16), (D_SSM, F32)], [D_SSM, D_SSM])
    dg = _mm("glu_bwd_x", dgl, wf['w_glu'], 'nt', TR, D_SSM, D_SSM, F32, res=dgd)
    g_glu = _mm("glu_bwd_w", g, dgl, 'tn', D_SSM, D_SSM, T, BF16).reshape(N_DEV, D_SSM // N_DEV, D_SSM)
    dy_ssm, = _rowwise("gelu_bwd", _gelu_bwd_fn, [(dg, D_SSM, 0), (y_ssm, D_SSM, 0)], [], [(D_SSM, F32)], [])
    token = send_grads(['w_up'], g_glu, 'chips')
    dyb, sm['out_norm_attn'] = _rowwise("rms_attn_bwd", _rms_bwd_fn, [(dy, D_SSM, 1), (yb, D_SSM, 0), (rb, 1, 0)],
                                        [row(p['out_norm_attn']) + token[0:1, 0:1]], [(D_SSM, F32)], [D_SSM])
    dq, dkv, dkpe = _attn_bwd(q, kv, z, tabs, dyb)
    g_q = _mm("q_bwd_w", qn, dq, 'tn', Q_LORA, HEAD_PAD, T, BF16, stack='b_out')
    dqn = _mm("q_bwd_x", dq, wf['w_q_b'], 'nt', T, Q_LORA, HEAD_PAD, F32, stack='ab_red')
    g_kv = _mm("kv_bwd_w", kvn, dkv, 'tn', KV_LORA, HEAD_PAD, T, BF16, stack='b_out')
    dkvn = _mm("kv_bwd_x", dkv, wf['w_kv_b'], 'nt', T, KV_LORA, HEAD_PAD, F32, stack='ab_red')
    token = send_grads(['w_out', 'w_glu', 'w_q_b', 'w_kv_b'], [g_out, g_glu, g_q, g_kv], 'both')
    def both_bwd(row0, dq_, zq, rq_, dkv_, zkv, rkv_, gq, gkv):
        (dzq, dgq), (dzkv, dgkv) = _rms_bwd(dq_, zq, rq_, gq), _rms_bwd(dkv_, zkv, rkv_, gkv)
        return dzq, dzkv, dgq, dgkv

    dq_a, dkv_a, sm['q_a_norm'], sm['kv_a_norm'] = _rowwise(
        "rms_q_kv_bwd", both_bwd,
        [(dqn, Q_LORA, 0), (z, Q_LORA, D_SSM // Q_LORA), (rq, 1, 0),
         (dkvn, KV_LORA, 0), (z, KV_LORA, (D_SSM + Q_LORA) // KV_LORA), (rkv, 1, 0)],
        [row(p['q_a_norm']), row(p['kv_a_norm'])], [(Q_LORA, BF16), (KV_LORA, BF16)], [Q_LORA, KV_LORA])
    dk_pe = _kpe_bwd(dkpe, tabs)
    du_seg, dbre, dbim, dcre, dcim, da, sm['d_skip'] = _s5_bwd(
        u_seg, _to_segments(dy_ssm), a8, bb_re, bb_im, c_re, c_im, dsk + token[0:1, 0:1])
    du = _from_segments(du_seg).astype(BF16)
    cgp = lambda d: d.reshape(N_GROUPS, SSM_GROUP, SSM_STATE).transpose(1, 0, 2)
    dlam_re, dlam_im, dlog_dt, dbt_re, dbt_im = _s5_prep_bwd(
        lam_re, lam_im, log_dt, bt_re, bt_im, da[:, 0, :].reshape(N_GROUPS, SSM_STATE),
        da[:, 1, :].reshape(N_GROUPS, SSM_STATE), cgp(dbre), cgp(dbim))
    sm['lam_re'], sm['lam_im'], sm['log_dt'] = dlam_re, dlam_im, dlog_dt
    sm['b_re'], sm['b_im'] = gcp(dbt_re), gcp(dbt_im)
    sm['c_re'], sm['c_im'] = dcre, dcim
    token = send_small(sm, dcw4, loss)
    dz = jnp.concatenate([du, dq_a, dkv_a, dk_pe], axis=1)
    dxn = _mm("in_bwd_x", dz, wf['w_in'], 'nt', TR, 1024, D_IN_PAD, F32)
    big['w_in'] = _mm("in_bwd_w", xn, dz, 'tn', 1024, 640, T, BF16).reshape(N_DEV, D // N_DEV, D_IN_PAD)
    dh0, d_mix_norm = _rowwise("rms_mix_bwd", _rms_bwd_res_fn,
                               [(dxn, D, 0), (h0, D, 0), (r0, 1, 0), (dh1, D, 0)],
                               [row(p['mix_norm']) + token[0:1, 0:1]], [(D, F32)], [D])
    return dh0[N_META:], dh0[:N_META], d_mix_norm, big


def _place():
    x, y, c = lax.axis_index("x"), lax.axis_index("y"), lax.axis_index("c")
    return x, y, c, [(1 - x, y), (x, 1 - y), (1 - x, 1 - y)]


_HBM = pl.BlockSpec(memory_space=pltpu.HBM)


def _all_gather(name, arrs, chip_blocks=()):
    n, nc = len(arrs), len(chip_blocks)

    def body(*refs):
        ins, cins, outs, couts = refs[:n], refs[n:n + nc], refs[n + nc:2 * n + nc], refs[2 * n + nc:2 * (n + nc)]
        send, recv, loc, csend, crecv = refs[2 * (n + nc):]
        x, y, c, chips = _place()
        me, sib = (x, y, c), (x, y, 1 - c)

        def cp(a, k, blk, to, src=None):
            dst = outs[a].at[4 * blk[0] + 2 * blk[1] + blk[2]]
            return pltpu.make_async_remote_copy(src_ref=dst if src is None else src, dst_ref=dst,
                                                send_sem=send.at[a, k], recv_sem=recv.at[a, k],
                                                device_id=to, device_id_type=MESH)

        between_chips = [pltpu.make_async_remote_copy(src_ref=cins[a].at[2 * ch[0] + ch[1]], dst_ref=couts[a].at[j],
                                                      send_sem=csend.at[a, j], recv_sem=crecv.at[a, j],
                                                      device_id=(*ch, c), device_id_type=MESH)
                         for a in range(nc) for j, ch in enumerate(chips)]
        mine = [pltpu.make_async_copy(ins[a], outs[a].at[4 * x + 2 * y + c], loc.at[a]) for a in range(n)]
        first = []
        for a in range(n):
            first.append(cp(a, 0, me, sib, src=ins[a]))
            first += [cp(a, 1 + j, me, (*ch, c), src=ins[a]) for j, ch in enumerate(chips)]
        for f in mine + first + between_chips:
            f.start()
        passed = []
        for a in range(n):
            for j, ch in enumerate(chips):
                cp(a, 1 + j, (*ch, c), me).wait_recv()
                passed.append(cp(a, 4 + j, (*ch, c), sib))
                passed[-1].start()
        for a in range(n):
            cp(a, 0, sib, me).wait_recv()
            for j, ch in enumerate(chips):
                cp(a, 4 + j, (*ch, 1 - c), me).wait_recv()
        for f in first + passed:
            f.wait_send()
        for m in mine + between_chips:
            m.wait()

    out = pl.pallas_call(
        body, name=name, in_specs=[_HBM] * (n + nc), out_specs=[_HBM] * (n + nc),
        out_shape=[jax.ShapeDtypeStruct((N_DEV,) + a.shape, a.dtype) for a in arrs]
        + [jax.ShapeDtypeStruct((3,) + a.shape[1:], a.dtype) for a in chip_blocks],
        scratch_shapes=[pltpu.SemaphoreType.DMA((n, 7)), pltpu.SemaphoreType.DMA((n, 7)),
                        pltpu.SemaphoreType.DMA((n,)), pltpu.SemaphoreType.DMA((max(nc, 1), 3)),
                        pltpu.SemaphoreType.DMA((max(nc, 1), 3))],
    )(*arrs, *chip_blocks)
    return (out[:n], out[n:]) if nc else out


def _exchange_cores(name, arrs):
    n = len(arrs)

    def body(*refs):
        ins, outs = refs[:n], refs[n:2 * n]
        send, recv = refs[2 * n:]
        x, y, c, _ = _place()
        remote = [pltpu.make_async_remote_copy(src_ref=ins[a].at[2 * k + 1 - c], dst_ref=outs[a].at[k],
                                               send_sem=send.at[a, k], recv_sem=recv.at[a, k],
                                               device_id=(x, y, 1 - c), device_id_type=MESH)
                  for a in range(n) for k in range(4)]
        for d in remote:
            d.start()
        for d in remote:
            d.wait()

    return pl.pallas_call(
        body, name=name, in_specs=[_HBM] * n, out_specs=[_HBM] * n,
        out_shape=[jax.ShapeDtypeStruct((4,) + a.shape[1:], a.dtype) for a in arrs],
        scratch_shapes=[pltpu.SemaphoreType.DMA((n, 4))] * 2,
    )(*arrs)


_SEM = pl.BlockSpec(memory_space=pltpu.SEMAPHORE)
_EFFECT = pltpu.SideEffectType.DATAFLOW_SIDE_EFFECTING


def _split_exchange(name, srcs, land_shapes, copies_of, n_cp, after=None, all_of=None):
    n = len(srcs)
    in_place = land_shapes is None
    lands = [] if in_place else [pltpu.with_memory_space_constraint(lax.empty(s, a.dtype), pltpu.HBM)
                                 for s, a in zip(land_shapes, srcs)]
    nb = n + len(lands)

    per = 1 if all_of else n_cp
    n_sem = n * per

    def descriptors(refs, send, recv, mirror):
        src_refs, land_refs = refs[:n], refs[nb - n:nb]
        x, y, c, chips = _place()
        out = []
        for a in range(n):
            if mirror and all_of:
                whole = all_of(a, land_refs[a])
                out.append(pltpu.make_async_remote_copy(src_ref=whole, dst_ref=whole, send_sem=send[a],
                                                        recv_sem=recv[a], device_id=(x, y, 1 - c),
                                                        device_id_type=MESH))
                continue
            for k, (src, dst, to, back) in enumerate(copies_of(a, src_refs[a], land_refs[a], x, y, c, chips)):
                out.append(pltpu.make_async_remote_copy(src_ref=src, dst_ref=back if mirror else dst,
                                                        send_sem=send[a * per + k % per],
                                                        recv_sem=recv[a * per + k % per],
                                                        device_id=to, device_id_type=MESH))
        return out

    def start_body(*refs):
        first = nb + (after is not None)
        sems = refs[first:first + 2 * n_sem]
        for d in descriptors(refs, sems[:n_sem], sems[n_sem:], False):
            d.start()
        refs[-1][...] = jnp.zeros((8, 128), F32)

    hbm_like = lambda arrs: [pltpu.HBM(a.shape, a.dtype) for a in arrs]
    order = [] if after is None else [after]
    res = pl.pallas_call(
        start_body, name=name + "_start", in_specs=[_HBM] * nb + [pl.BlockSpec(memory_space=pl.ANY)] * len(order),
        out_specs=[_SEM] * (2 * n_sem) + [_HBM] * nb + [pl.BlockSpec(memory_space=pltpu.VMEM)],
        out_shape=[pltpu.SemaphoreType.DMA(())] * (2 * n_sem)
        + hbm_like(srcs) + hbm_like(lands) + [jax.ShapeDtypeStruct((8, 128), F32)],
        input_output_aliases={i: 2 * n_sem + i for i in range(nb)},
        compiler_params=pltpu.CompilerParams(has_side_effects=_EFFECT),
    )(*srcs, *lands, *order)
    sems, thru, token = res[:2 * n_sem], res[2 * n_sem:2 * n_sem + nb], res[-1]

    def wait(after):
        def wait_body(*refs):
            s = refs[nb:nb + 2 * n_sem]
            for d in descriptors(refs, s[:n_sem], s[n_sem:], True):
                d.wait_send()
                d.wait_recv()

        out = pl.pallas_call(
            wait_body, name=name + "_wait",
            in_specs=[_HBM] * nb + [_SEM] * (2 * n_sem) + [pl.BlockSpec(memory_space=pl.ANY)],
            out_specs=[_HBM] * nb, out_shape=hbm_like(srcs) + hbm_like(lands),
            input_output_aliases={i: i for i in range(nb)},
            compiler_params=pltpu.CompilerParams(has_side_effects=_EFFECT),
        )(*thru, *sems, after)
        return out[:n], out[nb - n:]

    return token, wait


def _gather_copies(a, src, land, x, y, c, chips):
    peers = [(x, y, 1 - c)] + [(*ch, c) for ch in chips]
    me = 4 * x + 2 * y + c
    return [(src, land.at[me], to, land.at[4 * to[0] + 2 * to[1] + to[2]]) for to in peers]


def _to_all_copies(a, src, land, x, y, c, chips):
    peers = [(x, y, 1 - c)] + [(*ch, cc) for ch in chips for cc in (c, 1 - c)]
    me = 4 * x + 2 * y + c
    return [(src, land.at[me], to, land.at[4 * to[0] + 2 * to[1] + to[2]]) for to in peers]


def _cores_copies(a, src, land, x, y, c, chips):
    return [(src.at[2 * k + 1 - c], land.at[k], (x, y, 1 - c), land.at[k]) for k in range(4)]


def _forward_copies(a, src, land, x, y, c, chips):
    slot = lambda ch, core: 4 * ch[0] + 2 * ch[1] + core
    return [(land.at[slot(ch, c)], land.at[slot(ch, c)], (x, y, 1 - c), land.at[slot(ch, 1 - c)]) for ch in chips]


def _chips_copies(a, src, land, x, y, c, chips):
    return [(src.at[2 * ch[0] + ch[1]], land.at[j], (*ch, c), land.at[j]) for j, ch in enumerate(chips)]


def _gather_forward(name, arrs):
    n = len(arrs)

    def body(*refs):
        ins, outs = refs[:n], refs[n:2 * n]
        send, recv = refs[2 * n:]
        x, y, c, chips = _place()
        sends, recvs = [], []
        for a in range(n):
            for j, ch in enumerate(chips):
                mine, theirs = 4 * ch[0] + 2 * ch[1] + c, 4 * ch[0] + 2 * ch[1] + 1 - c
                sends.append(pltpu.make_async_remote_copy(src_ref=ins[a].at[mine], dst_ref=outs[a].at[mine],
                                                          send_sem=send.at[a, j], recv_sem=recv.at[a, j],
                                                          device_id=(x, y, 1 - c), device_id_type=MESH))
                recvs.append(pltpu.make_async_remote_copy(src_ref=ins[a].at[theirs], dst_ref=outs[a].at[theirs],
                                                          send_sem=send.at[a, j], recv_sem=recv.at[a, j],
                                                          device_id=(x, y, 1 - c), device_id_type=MESH))
        for d in sends:
            d.start()
        for s, r in zip(sends, recvs):
            s.wait_send()
            r.wait_recv()

    return pl.pallas_call(
        body, name=name, in_specs=[_HBM] * n, out_specs=[_HBM] * n,
        out_shape=[jax.ShapeDtypeStruct(a.shape, a.dtype) for a in arrs],
        input_output_aliases={i: i for i in range(n)},
        scratch_shapes=[pltpu.SemaphoreType.DMA((n, 3))] * 2,
    )(*arrs)


def _blocks_of(r, c):
    if r * c * 4 <= (2 << 20):
        return r, c
    if r % 128 == 0:
        return 128, c
    return r, 256


def _pair_sum(name, own, got, core):
    _, r, c = got.shape
    rb, cb = _blocks_of(r, c)

    def body(s_ref, a_ref, b_ref, o_ref):
        o_ref[...] = (a_ref[...].astype(F32) + b_ref[...].astype(F32)).astype(o_ref.dtype)

    return pl.pallas_call(
        body, name=name, out_shape=_out_hbm((4, r, c), got.dtype),
        grid_spec=pltpu.PrefetchScalarGridSpec(
            num_scalar_prefetch=1, grid=(4, r // rb, c // cb),
            in_specs=[pl.BlockSpec((None, rb, cb), lambda k, i, j, s: (2 * k + s[0], i, j)),
                      pl.BlockSpec((None, rb, cb), lambda k, i, j, s: (k, i, j))],
            out_specs=pl.BlockSpec((None, rb, cb), lambda k, i, j, s: (k, i, j))),
        compiler_params=_params(3, 24 << 20),
    )(core, _in_hbm(own), _in_hbm(got))


def _adamw_math(w, g, m, v):
    m = B1 * m + (1.0 - B1) * g
    v = B2 * v + (1.0 - B2) * (g * g)
    m_hat = m / (1.0 - B1 ** STEP)
    v_hat = v / (1.0 - B2 ** STEP)
    return -LR * (m_hat / (jnp.sqrt(v_hat) + ADAM_EPS) + WD * w), m, v


def _adamw_big(name, own, got, chip, w, m, v, behind):
    r, c = w.shape
    rb, cb = _blocks_of(r, c)

    def body(s_ref, o_ref, p_ref, w_ref, m_ref, v_ref, behind_ref, g_ref, d_ref, nm_ref, nv_ref):
        g = o_ref[...].astype(F32)
        for k in range(3):
            g = g + p_ref[k].astype(F32)
        g_ref[...] = g
        d_ref[...], nm_ref[...], nv_ref[...] = _adamw_math(w_ref[...], g, m_ref[...], v_ref[...])

    blk = pl.BlockSpec((rb, cb), lambda i, j, s: (i, j))
    return pl.pallas_call(
        body, name=name, out_shape=[jax.ShapeDtypeStruct((r, c), F32)] * 4,
        grid_spec=pltpu.PrefetchScalarGridSpec(
            num_scalar_prefetch=1, grid=(r // rb, c // cb),
            in_specs=[pl.BlockSpec((None, rb, cb), lambda i, j, s: (s[0], i, j)),
                      pl.BlockSpec((3, rb, cb), lambda i, j, s: (0, i, j)), blk, blk, blk,
                      pl.BlockSpec(memory_space=pl.ANY)],
            out_specs=[blk] * 4),
        compiler_params=_params(2, 40 << 20),
    )(chip, *map(_in_hbm, (own, got, w, m, v)), behind)


def _adamw_multi(name, items, sums=(), row_blocks=1, own=(), dev=None):
    n, ns = len(items), len(sums)
    n_own = len(own)
    assert n_own in (0, n + ns)

    def body(*refs):
        dev_ref = refs[0] if n_own else None
        refs = refs[1:] if n_own else refs
        owns, refs = refs[:n_own], refs[n_own:]
        ins, outs = refs[:4 * n + ns], refs[4 * n + ns:]

        def total(ref, t):
            g = None
            for d in range(ref.shape[0]):
                part = jnp.where(dev_ref[0] == d, owns[t][...], ref[d]) if n_own else ref[d]
                g = part.astype(F32) if g is None else g + part.astype(F32)
            return g

        for t in range(n):
            p_ref, w_ref, m_ref, v_ref = ins[4 * t:4 * t + 4]
            g = total(p_ref, t)
            outs[4 * t][...] = g
            outs[4 * t + 1][...], outs[4 * t + 2][...], outs[4 * t + 3][...] = _adamw_math(
                w_ref[...], g, m_ref[...], v_ref[...])
        for t in range(ns):
            outs[4 * n + t][...] = total(ins[4 * n + t], n + t)

    def spec(shape, lead):
        blk = (shape[0] // row_blocks,) + tuple(shape[1:])
        nd = len(shape)
        if lead:
            return pl.BlockSpec((lead,) + blk, lambda i: (0, i) + (0,) * (nd - 1))
        return pl.BlockSpec(blk, lambda i: (i,) + (0,) * (nd - 1))

    operands, in_specs, out_specs, out_shape = [], [], [], []
    if n_own:
        operands += [dev] + list(own)
        in_specs += [pl.BlockSpec(memory_space=pltpu.SMEM)] + [spec(o.shape, 0) for o in own]
    for parts, w, m, v in items:
        assert parts.shape[1:] == w.shape == m.shape == v.shape, (name, parts.shape, w.shape)
        operands += [parts, w, m, v]
        in_specs += [spec(w.shape, parts.shape[0])] + [spec(w.shape, 0)] * 3
        out_specs += [spec(w.shape, 0)] * 4
        out_shape += [jax.ShapeDtypeStruct(w.shape, F32)] * 4
    for parts in sums:
        operands.append(parts)
        in_specs.append(spec(parts.shape[1:], parts.shape[0]))
        out_specs.append(spec(parts.shape[1:], False))
        out_shape.append(jax.ShapeDtypeStruct(parts.shape[1:], F32))
    return pl.pallas_call(
        body, name=name, grid=(row_blocks,), in_specs=in_specs, out_specs=out_specs, out_shape=out_shape,
        compiler_params=_params(1, 56 << 20),
    )(*operands)


def kernel(x, meta_tokens, mix_norm, w_in, lam_re, lam_im, log_dt, b_re, b_im, c_re, c_im, d_skip, w_glu, b_glu, q_a_norm, w_q_b, kv_a_norm, w_kv_b, out_norm_ssm, out_norm_attn, w_out, ffn_norm, w_up, conv_w, conv_b, w_down, final_norm, loss_target, m_meta_tokens, m_mix_norm, m_w_in, m_lam_re, m_lam_im, m_log_dt, m_b_re, m_b_im, m_c_re, m_c_im, m_d_skip, m_w_glu, m_b_glu, m_q_a_norm, m_w_q_b, m_kv_a_norm, m_w_kv_b, m_out_norm_ssm, m_out_norm_attn, m_w_out, m_ffn_norm, m_w_up, m_conv_w, m_conv_b, m_w_down, m_final_norm, v_meta_tokens, v_mix_norm, v_w_in, v_lam_re, v_lam_im, v_log_dt, v_b_re, v_b_im, v_c_re, v_c_im, v_d_skip, v_w_glu, v_b_glu, v_q_a_norm, v_w_q_b, v_kv_a_norm, v_w_kv_b, v_out_norm_ssm, v_out_norm_attn, v_w_out, v_ffn_norm, v_w_up, v_conv_w, v_conv_b, v_w_down, v_final_norm):
    given = dict(locals())
    w = {n: given[n] for n in WEIGHTS}
    m = {n: given['m_' + n] for n in WEIGHTS}
    v = {n: given['v_' + n] for n in WEIGHTS}
    dev = 4 * lax.axis_index("x") + 2 * lax.axis_index("y") + lax.axis_index("c")

    shard = {
        'w_in': jnp.pad(w_in[0], ((0, 0), (0, D_IN_PAD - D_IN))),
        'w_glu': w_glu[0],
        'w_q_b': jnp.pad(w_q_b[0], ((0, 0), (0, HEAD_PAD - QK_NOPE - QK_ROPE))),
        'w_kv_b': w_kv_b[0],
        'w_out': w_out[0],
        'w_up': w_up[0],
        'w_down': w_down[0],
    }
    core = lax.axis_index("c").astype(jnp.int32).reshape(1)
    chip = (2 * lax.axis_index("x") + lax.axis_index("y")).astype(jnp.int32).reshape(1)
    mid, late = ['w_q_b', 'w_kv_b', 'w_glu'], ['w_out', 'w_up', 'w_down']
    shard = {n: a.astype(BF16) for n, a in shard.items()}

    gathered = _all_gather("gather_early", [shard['w_in'], meta_tokens, jnp.pad(conv_w[0], ((0, 5), (0, 0)))])
    wf = {'w_in': gathered[0].reshape(-1, D_IN_PAD)}
    meta = gathered[1].transpose(1, 0, 2).reshape(N_META, D)
    wf['conv_w'] = gathered[2][:, :3].transpose(1, 0, 2).reshape(3, D_FF)
    mid_token, wait_mid = _split_exchange("gather_mid", [shard[n] for n in mid],
                                          [(N_DEV,) + shard[n].shape for n in mid], _gather_copies, 4,
                                          after=gathered[0])
    gather_token, late_waits = mid_token, {}
    for n in late:
        gather_token, late_waits[n] = _split_exchange("gather_" + n, [shard[n]], [(N_DEV,) + shard[n].shape],
                                                      _gather_copies, 4, after=gather_token)

    def with_mine(mine, landed):
        return [lax.dynamic_update_slice(a, s[None], (dev, 0, 0)) for s, a in zip(mine, landed)]

    def mid_weights(after):
        full = _gather_forward("gather_forward_mid", with_mine(*wait_mid(after)))
        return {'w_q_b': full[0], 'w_kv_b': full[1], 'w_glu': full[2].reshape(-1, D_SSM)}

    def late_weights(after):
        full_out, = _gather_forward("gather_forward_w_out", with_mine(*late_waits['w_out'](after)))
        token, wait_up = _split_exchange("gather_forward_w_up", with_mine(*late_waits['w_up'](after)), None,
                                         _forward_copies, 3, after=full_out)

        def up_weights(after):
            full_up, = wait_up(after)[0]
            token, wait_down = _split_exchange("gather_forward_w_down", with_mine(*late_waits['w_down'](after)),
                                               None, _forward_copies, 3, after=full_up)
            return full_up, token, lambda after: wait_down(after)[0][0].reshape(-1, D)

        return full_out.reshape(-1, D), token, up_weights

    pending, between_cores = [], {}

    def send_grads(names, arg, stage):
        if stage == 'cores':
            token, between_cores[names[0]] = _split_exchange(
                "reduce_cores_" + names[0], arg, [(4,) + a.shape[1:] for a in arg], _cores_copies, 4)
            return token
        mine, got = between_cores.pop(names[0])(arg) if stage == 'chips' else (
            arg, _exchange_cores("reduce_cores_" + names[0], arg))
        sums = [_pair_sum("sum_cores_" + n, a, g, core) for n, a, g in zip(names, mine, got)]
        token, wait = _split_exchange("reduce_chips_" + names[0], sums, [(3,) + s.shape[1:] for s in sums],
                                      _chips_copies, 3)
        pending.append((names, wait))
        return token

    small_sent = []
    s5_bc = ['b_re', 'b_im', 'c_re', 'c_im']

    def send_small(sm, dcw4, loss_part):
        names = [n for n in SMALL if n in sm]
        arrs = [sm[n].reshape(-1, 128).astype(BF16) if n in s5_bc else sm[n] for n in names] + [dcw4, loss_part]
        token, wait = _split_exchange("gather_small", arrs, [(N_DEV,) + a.shape for a in arrs], _to_all_copies, 7,
                                      all_of=lambda a, land: land.at[pl.ds(0, N_DEV - 1)])
        small_sent.append((names, wait))
        return token

    small = {n: w[n] for n in SMALL}
    small['mix_norm'] = mix_norm + gather_token[0:1, 0:1]
    grad_x, dmeta, d_mix_norm, big = _local_step(x[0], loss_target[0], meta, small, wf, mid_weights, late_weights,
                                                  send_grads, send_small)
    grads, deltas, new_m, new_v = {}, {}, {}, {}

    def keep(n, outs):
        grads[n], deltas[n], new_m[n], new_v[n] = (o.reshape(w[n].shape) for o in outs)

    rest = list(big)
    from_core = _exchange_cores("reduce_cores", [big[n] for n in rest])
    chip_sums = [_pair_sum("sum_cores_" + n, big[n], got, core) for n, got in zip(rest, from_core)]
    last = chip_sums + [d_mix_norm, dmeta]
    n_big = len(chip_sums)
    last_token, wait_last = _split_exchange(
        "reduce_last", last, [(3,) + s.shape[1:] for s in chip_sums] + [(N_DEV,) + a.shape for a in last[n_big:]],
        lambda a, *at: _chips_copies(a, *at) if a < n_big else _to_all_copies(a, *at), 7,
        all_of=lambda a, land: land if a < n_big else land.at[pl.ds(0, N_DEV - 1)])

    def update(n, own, got):
        if n == 'w_up':
            outs = _adamw_big("adamw_" + n, own, got, chip, w[n][0].T, m[n][0].T, v[n][0].T, last_token)
            return keep(n, [o.T for o in outs])
        cols = w[n].shape[2]
        keep(n, _adamw_big("adamw_" + n, own[:, :, :cols], got[:, :, :cols], chip, w[n][0], m[n][0], v[n][0],
                           last_token))

    for names, wait in pending:
        for n, own, got in zip(names, *wait(chip_sums[0])):
            update(n, own, got)
    updated = [new_v[n].reshape(-1)[:1] for names, _ in pending for n in names]
    sent, landed = wait_last(functools.reduce(jnp.add, updated))
    for n, own, got in zip(rest, sent, landed):
        update(n, own, got)

    device = dev.astype(jnp.int32).reshape(1)
    outs = _adamw_multi("adamw_last", [(landed[n_big], mix_norm, m['mix_norm'], v['mix_norm'])],
                        sums=[landed[n_big + 1]], own=sent[n_big:], dev=device)
    keep('mix_norm', outs[:4])
    g_meta = outs[4]
    names, wait = small_sent[0]
    mine, landed = wait(grads['mix_norm'])
    parts, own = dict(zip(names + ['conv_w', 'loss'], landed)), dict(zip(names + ['conv_w', 'loss'], mine))
    wide = [n for n in names if n not in s5_bc]
    summed = ['conv_w', 'loss'] + s5_bc
    outs = _adamw_multi("adamw_small",
                        [(parts[n],) + tuple(t[n].reshape(own[n].shape) for t in (w, m, v)) for n in wide],
                        sums=[parts[n] for n in summed], own=[own[n] for n in wide + summed], dev=device)
    for i, n in enumerate(wide):
        keep(n, outs[4 * i:4 * i + 4])
    g_cw4, loss = outs[-6:-4]
    rows_gc = lambda a, n: (a.reshape(N_GROUPS, SSM_STATE, SSM_GROUP).transpose(0, 2, 1) if n[0] == 'b' else a
                            ).reshape(N_GROUPS * SSM_GROUP, SSM_STATE)
    outs = _adamw_multi("adamw_s5_bc", [(g.reshape(1, N_GROUPS * SSM_GROUP, SSM_STATE),)
                                        + tuple(rows_gc(t[n], n) for t in (w, m, v))
                                        for n, g in zip(s5_bc, outs[-4:])])
    for i, n in enumerate(s5_bc):
        back = outs[4 * i:4 * i + 4]
        if n[0] == 'b':
            back = [o.reshape(N_GROUPS, SSM_GROUP, SSM_STATE).transpose(0, 2, 1) for o in back]
        keep(n, back)

    g_meta = lax.dynamic_slice(g_meta, (0, dev * (D // N_DEV)), (N_META, D // N_DEV))
    g_conv = lax.dynamic_slice(g_cw4.transpose(1, 0, 2).reshape(3, D_FF), (0, dev * (D_FF // N_DEV)),
                               (3, D_FF // N_DEV))
    rows8 = lambda a: jnp.pad(a.reshape(3, D_FF // N_DEV), ((0, 5), (0, 0)))
    outs = _adamw_multi("adamw_cols", [(g_meta[None], meta_tokens, m['meta_tokens'], v['meta_tokens']),
                                       (rows8(g_conv)[None], rows8(conv_w), rows8(m['conv_w']), rows8(v['conv_w']))])
    keep('meta_tokens', outs[:4])
    keep('conv_w', [o[:3] for o in outs[4:]])

    return (loss[0, 0], grad_x[None], *[grads[n] for n in WEIGHTS], *[deltas[n] for n in WEIGHTS],
            *[new_m[n] for n in WEIGHTS], *[new_v[n] for n in WEIGHTS])
```

```python
import functools
import math

import jax
import jax.numpy as jnp
from jax import lax
from jax.experimental import pallas as pl
from jax.experimental.pallas import tpu as pltpu

F32, BF16 = jnp.float32, jnp.bfloat16

N_DEV = 8
N_META, SEQ, D = 16, 2048, 2048
T = N_META + SEQ
TR = 688
EPS = 1e-6
D_SSM, Q_LORA, KV_LORA, QK_ROPE = 1024, 512, 256, 64
D_IN = D_SSM + Q_LORA + KV_LORA + QK_ROPE
D_IN_PAD = 1920
N_HEADS, QK_NOPE, V_HEAD = 8, 128, 128
HEAD_PAD = 256
SM_SCALE = 1.0 / math.sqrt(QK_NOPE + QK_ROPE)
CHUNK = 64
N_GROUPS, SSM_GROUP, SSM_STATE = 64, 16, 64
N_SEG = 8
SEG = T // N_SEG
GQ = 8
D_FF = 5504
FF_PIECE = 1376
ROPE_BASE = 10000.0
LR, B1, B2, ADAM_EPS, WD, STEP = 0.001, 0.9, 0.999, 1e-08, 0.01, 10
VMEM_CAP = 60 * 1024 * 1024
MESH = pl.DeviceIdType.MESH

WEIGHTS = ['meta_tokens', 'mix_norm', 'w_in', 'lam_re', 'lam_im', 'log_dt', 'b_re', 'b_im', 'c_re', 'c_im',
           'd_skip', 'w_glu', 'b_glu', 'q_a_norm', 'w_q_b', 'kv_a_norm', 'w_kv_b', 'out_norm_ssm',
           'out_norm_attn', 'w_out', 'ffn_norm', 'w_up', 'conv_w', 'conv_b', 'w_down', 'final_norm']
BIG = ['w_in', 'w_glu', 'w_q_b', 'w_kv_b', 'w_out', 'w_up', 'w_down']
SMALL = [n for n in WEIGHTS if n not in BIG and n not in ('meta_tokens', 'conv_w')]


def _nbytes(shape, dtype):
    return math.prod(shape) * jnp.dtype(dtype).itemsize


def _in_hbm(a):
    return pltpu.with_memory_space_constraint(a, pltpu.HBM) if a.size * a.dtype.itemsize >= (1 << 20) else a


def _out_hbm(shape, dtype):
    big = math.prod(shape) * jnp.dtype(dtype).itemsize >= (1 << 20)
    return pltpu.HBM(shape, dtype) if big else jax.ShapeDtypeStruct(shape, dtype)


def _params(sem, need):
    return pltpu.CompilerParams(dimension_semantics=("arbitrary",) * sem,
                                vmem_limit_bytes=int(min(VMEM_CAP, max(need, 16 * 1024 * 1024))))


_DIMS = {'nn': (((1,), (0,)), ((), ())), 'nt': (((1,), (1,)), ((), ())), 'tn': (((0,), (0,)), ((), ()))}


def _mm(name, a, b, mode, tm, tn, tk, out_dtype, stack=None, res=None):
    sa, sb, so = stack in ('a_out', 'ab_red'), stack in ('b_out', 'ab_red'), stack in ('a_out', 'b_out')
    nj = a.shape[0] if sa else (b.shape[0] if sb else 1)
    a2, b2 = a.shape[-2:], b.shape[-2:]
    if mode == 'nn':
        (m, k), (k2, n) = a2, b2
    elif mode == 'nt':
        (m, k), (n, k2) = a2, b2
    else:
        (k, m), (k2, n) = a2, b2
    assert k == k2 and m % tm == 0 and n % tn == 0 and k % tk == 0, (name, a.shape, b.shape)
    n_jo, n_jr, n_k = (nj if so else 1), (nj if stack == 'ab_red' else 1), k // tk
    grid = (n_jo, m // tm, n // tn, n_jr, n_k)
    a_blk, a_idx = ((tk, tm), lambda i, j, kk: (kk, i)) if mode == 'tn' else ((tm, tk), lambda i, j, kk: (i, kk))
    b_blk, b_idx = ((tn, tk), lambda i, j, kk: (j, kk)) if mode == 'nt' else ((tk, tn), lambda i, j, kk: (kk, j))

    def spec(blk, idx, stacked, on_out):
        if not stacked:
            return pl.BlockSpec(blk, lambda jo, i, j, jr, kk: idx(i, j, kk))
        return pl.BlockSpec((None,) + blk, lambda jo, i, j, jr, kk: ((jo if on_out else jr),) + idx(i, j, kk))

    o_idx = lambda i, j, kk: (i, j)
    in_specs = [spec(a_blk, a_idx, sa, stack == 'a_out'), spec(b_blk, b_idx, sb, stack == 'b_out')]
    operands = [a, b]
    if res is not None:
        in_specs.append(spec((tm, tn), o_idx, False, False))
        operands.append(res)
    n_red = n_jr * n_k
    dims = _DIMS[mode]

    def body(*refs):
        a_ref, b_ref = refs[0], refs[1]
        res_ref = refs[2] if res is not None else None
        o_ref = refs[3] if res is not None else refs[2]
        part = lax.dot_general(a_ref[...].astype(BF16), b_ref[...].astype(BF16), dims,
                               preferred_element_type=F32)

        def finish(total):
            if res_ref is not None:
                total = total + res_ref[...]
            o_ref[...] = total.astype(o_ref.dtype)

        if n_red == 1:
            finish(part)
        else:
            acc_ref = refs[-1]
            step = pl.program_id(3) * n_k + pl.program_id(4)

            @pl.when(step == 0)
            def _():
                acc_ref[...] = part

            @pl.when(step > 0)
            def _():
                acc_ref[...] += part

            @pl.when(step == n_red - 1)
            def _():
                finish(acc_ref[...])

    out_shape = ((nj,) if so else ()) + (m, n)
    need = 2 * (_nbytes(a_blk, a.dtype) + _nbytes(b_blk, b.dtype) + _nbytes((tm, tn), out_dtype)
                + (_nbytes((tm, tn), F32) if res is not None else 0)) + 3 * _nbytes((tm, tn), F32) + (4 << 20)
    return pl.pallas_call(
        body, name=name, grid=grid, in_specs=in_specs, out_specs=spec((tm, tn), o_idx, so, True),
        out_shape=_out_hbm(out_shape, out_dtype),
        scratch_shapes=[pltpu.VMEM((tm, tn), F32)] if n_red > 1 else [],
        compiler_params=_params(5, need),
    )(*map(_in_hbm, operands))


def _rowwise(name, fn, row_ins, const_ins, row_outs, acc_outs, tr=TR):
    n_ri, n_ci, n_ro, n_ao = len(row_ins), len(const_ins), len(row_outs), len(acc_outs)

    def body(*refs):
        ri, ci = refs[:n_ri], refs[n_ri:n_ri + n_ci]
        ro, ao = refs[n_ri + n_ci:n_ri + n_ci + n_ro], refs[n_ri + n_ci + n_ro:]
        i = pl.program_id(0)
        tiles = [jnp.where(i == 0, pltpu.roll(r[...], lead, 0), r[...]) if lead else r[...]
                 for r, lead in zip(ri, leads)]
        outs = fn(i * tr, *tiles, *[r[...] for r in ci])
        for r, v in zip(ro, outs[:n_ro]):
            r[...] = v.astype(r.dtype)
        if n_ao:
            @pl.when(i == 0)
            def _():
                for r, v in zip(ao, outs[n_ro:]):
                    r[...] = v

            @pl.when(i > 0)
            def _():
                for r, v in zip(ao, outs[n_ro:]):
                    r[...] += v

    def row_spec(a, w, cb, lead=0):
        if not lead:
            return pl.BlockSpec((tr, w), lambda i: (i, cb))
        assert a.shape[0] + lead == T and lead % 8 == 0 and tr % 8 == 0
        return pl.BlockSpec((pl.Element(tr), pl.Element(w)),
                            lambda i: (jnp.maximum(i * (tr // 8) - lead // 8, 0) * 8, cb * w))

    in_specs = [row_spec(*r) for r in row_ins]
    leads = [r[3] if len(r) > 3 else 0 for r in row_ins]
    row_ins = [r[:3] for r in row_ins]
    in_specs += [pl.BlockSpec(c.shape, functools.partial(lambda nd, i: (0,) * nd, c.ndim)) for c in const_ins]
    out_specs = [pl.BlockSpec((tr, w), lambda i: (i, 0)) for w, _ in row_outs]
    out_specs += [pl.BlockSpec((1, w), lambda i: (0, 0)) for w in acc_outs]
    out_shape = [_out_hbm((T, w), dt) for w, dt in row_outs]
    out_shape += [jax.ShapeDtypeStruct((1, w), F32) for w in acc_outs]
    need = 2 * (sum(_nbytes((tr, w), a.dtype) for a, w, _ in row_ins) + sum(_nbytes((tr, w), dt) for w, dt in row_outs))
    need = 3 * need + (8 << 20)
    return pl.pallas_call(
        body, name=name, grid=(T // tr,), in_specs=in_specs, out_specs=out_specs, out_shape=out_shape,
        compiler_params=_params(1, need),
    )(*[_in_hbm(a) for a, _, _ in row_ins], *const_ins)


def _rms_fwd_fn(row0, h, g):
    h = h.astype(F32)
    r = lax.rsqrt(jnp.mean(h * h, axis=-1, keepdims=True) + EPS)
    return (h * r) * g, r


def _rms_mix_fn(row0, x, meta, g):
    h = jnp.where(row0 == 0, jnp.concatenate([meta, x[N_META:]], axis=0), x)
    return (h,) + _rms_fwd_fn(row0, h, g)


def _rms_bwd(dxn, h, r, g):
    xhat = h * r
    dxg = dxn * g
    dh = r * (dxg - xhat * jnp.mean(dxg * xhat, axis=-1, keepdims=True))
    return dh, jnp.sum(dxn * xhat, axis=0, keepdims=True)


def _rms_bwd_fn(row0, dxn, h, r, g):
    return _rms_bwd(dxn.astype(F32), h.astype(F32), r, g)


def _rms_bwd_res_fn(row0, dxn, h, r, res, g):
    dh, dg = _rms_bwd(dxn.astype(F32), h.astype(F32), r, g)
    return dh + res, dg


def _gelu_fn(row0, y):
    return (jax.nn.gelu(y),)


def _glu_fwd_fn(row0, y, gl, b, gain):
    ya = jax.nn.gelu(y) * jax.nn.sigmoid(gl + b)
    return _rms_fwd_fn(row0, ya, gain)


def _glu_bwd_fn(row0, dyan, y, gl, ra, b, gain):
    g = jax.nn.gelu(y)
    s = jax.nn.sigmoid(gl + b)
    dya, dgain = _rms_bwd(dyan, g * s, ra, gain)
    dgl = dya * g * s * (1.0 - s)
    return dgl, dya * s, dgain, jnp.sum(dgl, axis=0, keepdims=True)


def _gelu_bwd_fn(row0, dg, y):
    return (jax.vjp(jax.nn.gelu, y)[1](dg)[0],)


def _loss_fn(row0, h, tgt, gain):
    r = lax.rsqrt(jnp.mean(h * h, axis=-1, keepdims=True) + EPS)
    xhat = h * r
    rows = row0 + lax.broadcasted_iota(jnp.int32, h.shape, 0)
    err = jnp.where(rows >= N_META, xhat * gain - tgt, 0.0)
    loss = jnp.full((1, 128), jnp.sum(err * err) * (0.5 / D), F32)
    dh, dgain = _rms_bwd(err * (1.0 / D), h, r, gain)
    return dh, dgain, loss


def _rope_tables():
    pos = jnp.arange(T, dtype=jnp.int32)
    inv_freq = 1.0 / (ROPE_BASE ** (jnp.arange(0, QK_ROPE, 2, dtype=F32) / QK_ROPE))
    ang = pos.astype(F32)[:, None] * inv_freq[None, :]
    cos, sin, z32, z64 = jnp.cos(ang), jnp.sin(ang), jnp.zeros((T, 32), F32), jnp.zeros((T, 64), F32)
    return (jnp.concatenate([cos, cos, z64], 1), jnp.concatenate([-sin, z32, z64], 1),
            jnp.concatenate([z32, sin, z64], 1))


def _rot(x, cc, s1, s2):
    return x * cc + pltpu.roll(x, 96, 1) * s1 + pltpu.roll(x, 32, 1) * s2


def _derot(d, cc, s1, s2):
    return d * cc + pltpu.roll(d * s1, 32, 1) + pltpu.roll(d * s2, 96, 1)


def _chunk_of(pos):
    return jnp.where(pos < N_META, 0, 1 + ((pos - N_META) >> 6))


def _key_len(t):
    last_chunk = 1 + ((t + 1) * TR - 1 - N_META) // CHUNK
    return min(T, pl.cdiv(N_META + last_chunk * CHUNK, 128) * 128)


def _attn_probs(t, q_ref, kv_ref, kpe_ref, tq, tk):
    kl = _key_len(t)
    q = q_ref[...]
    qn = q[:, :QK_NOPE].astype(BF16)
    qp = _rot(q[:, QK_NOPE:], tq[0][...], tq[1][...], tq[2][...]).astype(BF16)
    kn = kv_ref[:kl, :QK_NOPE].astype(BF16)
    v = kv_ref[:kl, QK_NOPE:].astype(BF16)
    kp = _rot(kpe_ref[:kl, :], tk[0][:kl, :], tk[1][:kl, :], tk[2][:kl, :]).astype(BF16)
    s = lax.dot_general(qn, kn, _DIMS['nt'], preferred_element_type=F32)
    s = s + lax.dot_general(qp, kp, _DIMS['nt'], preferred_element_type=F32)
    qc = _chunk_of(t * TR + lax.broadcasted_iota(jnp.int32, (TR, 1), 0))
    kc = _chunk_of(lax.broadcasted_iota(jnp.int32, (1, kl), 1))
    s = jnp.where(kc <= qc, s * SM_SCALE, jnp.finfo(F32).min)
    e = jnp.exp(s - jnp.max(s, axis=-1, keepdims=True))
    return qn, qp, kn, kp, v, e, 1.0 / jnp.sum(e, axis=-1, keepdims=True)


def _per_q_tile(fn):
    for t in range(T // TR):
        pl.when(pl.program_id(1) == t)(functools.partial(fn, t))


def _attn_specs(z, tabs):
    q_spec = pl.BlockSpec((None, TR, HEAD_PAD), lambda h, i: (h, i, 0))
    kv_spec = pl.BlockSpec((None, T, HEAD_PAD), lambda h, i: (h, 0, 0))
    kpe_spec = pl.BlockSpec((T, 128), lambda h, i: (0, (D_IN - QK_ROPE) // 128))
    tq = [pl.BlockSpec((TR, 128), lambda h, i: (i, 0))] * 3
    tk = [pl.BlockSpec((T, 128), lambda h, i: (0, 0))] * 3
    return [q_spec, kv_spec, kpe_spec] + tq + tk


def _attn_fwd(q, kv, z, tabs):
    def body(q_ref, kv_ref, kpe_ref, c1, c2, c3, k1, k2, k3, o_ref):
        def tile(t):
            _, _, _, _, v, e, inv = _attn_probs(t, q_ref, kv_ref, kpe_ref, (c1, c2, c3), (k1, k2, k3))
            o_ref[...] = jnp.dot(e.astype(BF16), v, preferred_element_type=F32) * inv

        _per_q_tile(tile)

    return pl.pallas_call(
        body, name="attn_fwd", grid=(N_HEADS, T // TR), in_specs=_attn_specs(z, tabs),
        out_specs=pl.BlockSpec((TR, V_HEAD), lambda h, i: (i, h)),
        out_shape=_out_hbm((T, N_HEADS * V_HEAD), F32),
        compiler_params=_params(2, 48 << 20),
    )(_in_hbm(q), _in_hbm(kv), _in_hbm(z), *tabs, *tabs)


def _attn_bwd(q, kv, z, tabs, dyb):
    def body(q_ref, kv_ref, kpe_ref, c1, c2, c3, k1, k2, k3, do_ref, dq_ref, dkv_ref, dkpe_ref):
        @pl.when(pl.program_id(1) == 0)
        def _():
            dkv_ref[...] = jnp.zeros(dkv_ref.shape, F32)
            dkpe_ref[...] = jnp.zeros(dkpe_ref.shape, F32)

        def tile(t):
            kl = _key_len(t)
            qn, qp, kn, kp, v, e, inv = _attn_probs(t, q_ref, kv_ref, kpe_ref, (c1, c2, c3), (k1, k2, k3))
            do = do_ref[...]
            dp = lax.dot_general(do.astype(BF16), v, _DIMS['nt'], preferred_element_type=F32)
            delta = jnp.sum(e * dp, axis=-1, keepdims=True) * inv
            ds = (e * ((dp - delta) * (inv * SM_SCALE))).astype(BF16)
            pb, do = e.astype(BF16), (do * inv).astype(BF16)
            dq_ref[:, :QK_NOPE] = jnp.dot(ds, kn, preferred_element_type=F32).astype(dq_ref.dtype)
            dqp = _derot(jnp.dot(ds, kp, preferred_element_type=F32), c1[...], c2[...], c3[...])
            dq_ref[:, QK_NOPE:] = dqp.astype(dq_ref.dtype)
            dkv_ref[:kl, :QK_NOPE] += lax.dot_general(ds, qn, _DIMS['tn'], preferred_element_type=F32)
            dkv_ref[:kl, QK_NOPE:] += lax.dot_general(pb, do, _DIMS['tn'], preferred_element_type=F32)
            dkpe_ref[:kl, :] += lax.dot_general(ds, qp, _DIMS['tn'], preferred_element_type=F32)

        _per_q_tile(tile)

    return pl.pallas_call(
        body, name="attn_bwd", grid=(N_HEADS, T // TR),
        in_specs=_attn_specs(z, tabs) + [pl.BlockSpec((TR, V_HEAD), lambda h, i: (i, h))],
        out_specs=[pl.BlockSpec((None, TR, HEAD_PAD), lambda h, i: (h, i, 0)),
                   pl.BlockSpec((None, T, HEAD_PAD), lambda h, i: (h, 0, 0)),
                   pl.BlockSpec((None, T, 128), lambda h, i: (h, 0, 0))],
        out_shape=[_out_hbm((N_HEADS, T, HEAD_PAD), BF16), _out_hbm((N_HEADS, T, HEAD_PAD), F32),
                   _out_hbm((N_HEADS, T, 128), F32)],
        compiler_params=_params(2, 56 << 20),
    )(_in_hbm(q), _in_hbm(kv), _in_hbm(z), *tabs, *tabs, _in_hbm(dyb))


def _kpe_bwd(dkpe, tabs):
    def body(d_ref, c1, c2, c3, o_ref):
        d = d_ref[0]
        for h in range(1, N_HEADS):
            d = d + d_ref[h]
        o_ref[...] = _derot(d, c1[...], c2[...], c3[...]).astype(o_ref.dtype)

    tile = pl.BlockSpec((TR, 128), lambda i: (i, 0))
    return pl.pallas_call(
        body, name="kpe_bwd", grid=(T // TR,),
        in_specs=[pl.BlockSpec((N_HEADS, TR, 128), lambda i: (0, i, 0)), tile, tile, tile], out_specs=tile,
        out_shape=jax.ShapeDtypeStruct((T, 128), BF16), compiler_params=_params(1, 16 << 20),
    )(_in_hbm(dkpe), *tabs)


def _s5_prep_math(lam_re, lam_im, log_dt, bt_re, bt_im):
    dt = jnp.exp(log_dt)
    mag = jnp.exp(lam_re * dt)
    a_re, a_im = mag * jnp.cos(lam_im * dt), mag * jnp.sin(lam_im * dt)
    den = lam_re * lam_re + lam_im * lam_im
    w_re = ((a_re - 1.0) * lam_re + a_im * lam_im) / den
    w_im = (a_im * lam_re - (a_re - 1.0) * lam_im) / den
    return a_re, a_im, w_re[None] * bt_re - w_im[None] * bt_im, w_re[None] * bt_im + w_im[None] * bt_re


def _eye_groups():
    return (lax.broadcasted_iota(jnp.int32, (N_GROUPS, N_GROUPS), 0)
            == lax.broadcasted_iota(jnp.int32, (N_GROUPS, N_GROUPS), 1)).astype(F32)


def _row_to_col(row):
    return jnp.sum(_eye_groups() * row, axis=1, keepdims=True)


def _s5_prep(lam_re, lam_im, log_dt, bt_re, bt_im):
    def body(lr, li, ld, br, bi, ar, ai, bbr, bbi):
        ar[...], ai[...], bbr[...], bbi[...] = _s5_prep_math(lr[...], li[...], _row_to_col(ld[...]), br[...], bi[...])

    gp, cgp = jax.ShapeDtypeStruct((N_GROUPS, SSM_STATE), F32), jax.ShapeDtypeStruct(bt_re.shape, F32)
    return pl.pallas_call(body, name="s5_prep", out_shape=[gp, gp, cgp, cgp])(lam_re, lam_im, log_dt, bt_re, bt_im)


def _s5_prep_bwd(lam_re, lam_im, log_dt, bt_re, bt_im, da_re, da_im, dbb_re, dbb_im):
    def body(lr, li, ld, br, bi, dar, dai, dbr, dbi, o1, o2, o3, o4, o5):
        _, vjp = jax.vjp(_s5_prep_math, lr[...], li[...], _row_to_col(ld[...]), br[...], bi[...])
        o1[...], o2[...], dld, o4[...], o5[...] = vjp((dar[...], dai[...], dbr[...], dbi[...]))
        o3[...] = jnp.sum(_eye_groups() * dld, axis=0, keepdims=True)

    ins = (lam_re, lam_im, log_dt, bt_re, bt_im)
    return pl.pallas_call(body, name="s5_prep_bwd", out_shape=[jax.ShapeDtypeStruct(a.shape, F32) for a in ins])(
        *ins, da_re, da_im, dbb_re, dbb_im)


def _cmul(ar, ai, br, bi):
    return ar * br - ai * bi, ar * bi + ai * br


def _seg_rows(i):
    return pl.ds(pl.multiple_of(i * N_SEG, N_SEG), N_SEG)


def _scan(xr, xi, ar, ai, reverse):
    zero = jnp.zeros(ar.shape, F32)

    def local(j, carry):
        rows = _seg_rows(SEG - 1 - j if reverse else j)
        nr, ni = _cmul(ar, ai, *carry)
        nr, ni = nr + xr[rows, :], ni + xi[rows, :]
        xr[rows, :], xi[rows, :] = nr, ni
        return nr, ni

    er, ei = lax.fori_loop(0, SEG, local, (zero, zero))
    pr, pi = ar, ai
    for _ in range(8):
        pr, pi = _cmul(pr, pi, pr, pi)
    pr, pi = _cmul(*_cmul(pr, pi, ar, ai), ar, ai)
    row = lax.broadcasted_iota(jnp.int32, ar.shape, 0)
    edge, shift = (N_SEG - 1, N_SEG - 1) if reverse else (0, 1)
    hr, hi = zero, zero
    for _ in range(N_SEG - 1):
        tr_, ti_ = _cmul(pr, pi, hr, hi)
        hr = jnp.where(row == edge, 0.0, pltpu.roll(tr_ + er, shift, 0))
        hi = jnp.where(row == edge, 0.0, pltpu.roll(ti_ + ei, shift, 0))

    def fix(j, carry):
        rows = _seg_rows(SEG - 1 - j if reverse else j)
        xr[rows, :] += carry[0]
        xi[rows, :] += carry[1]
        return _cmul(ar, ai, *carry)

    lax.fori_loop(0, SEG, fix, _cmul(ar, ai, hr, hi))
    return hr, hi


def _split(x):
    hi = x.astype(BF16)
    return hi, (x - hi.astype(F32)).astype(BF16)


def _dot2(xs, ys, mode='nn'):
    d = lambda p, q: lax.dot_general(p, q, _DIMS[mode], preferred_element_type=F32)
    return d(xs[0], ys[0]) + d(xs[1], ys[0])


def _group_masks():
    ns = GQ * SSM_STATE
    rep = (lax.broadcasted_iota(jnp.int32, (SSM_STATE, ns), 1) % SSM_STATE
           == lax.broadcasted_iota(jnp.int32, (SSM_STATE, ns), 0)).astype(BF16)
    own = (lax.broadcasted_iota(jnp.int32, (128, ns), 0) // SSM_GROUP
           == lax.broadcasted_iota(jnp.int32, (128, ns), 1) // SSM_STATE)
    return rep, own


def _block_diag(x, rep, own):
    dot = lambda p: jnp.dot(p, rep, preferred_element_type=F32)
    return tuple(jnp.where(own, dot(part), 0.0).astype(BF16) for part in _split(x))


def _block_rows(full, rep, own):
    masked = _split(jnp.where(own, full, 0.0))
    dot = lambda p: lax.dot_general(p, rep, _DIMS['nt'], preferred_element_type=F32)
    return dot(masked[0]) + dot(masked[1])


def _s5_specs():
    col = pl.BlockSpec((T, 128), lambda q: (0, q))
    a_spec = pl.BlockSpec((None, 2, N_SEG, GQ * SSM_STATE), lambda q: (q, 0, 0, 0))
    rows = pl.BlockSpec((128, SSM_STATE), lambda q: (q, 0))
    d_spec = pl.BlockSpec((1, 128), lambda q: (0, q))
    return col, a_spec, rows, d_spec


def _s5_fwd(u, a8, bb_re, bb_im, c_re, c_im, dsk):
    def body(u_ref, a_ref, bre_ref, bim_ref, cre_ref, cim_ref, d_ref, y_ref, hre, him):
        u_ = u_ref[...]
        us = _split(u_)
        rep, own = _group_masks()
        hre[...] = _dot2(us, _block_diag(bre_ref[...], rep, own))
        him[...] = _dot2(us, _block_diag(bim_ref[...], rep, own))
        _scan(hre, him, a_ref[0], a_ref[1], reverse=False)
        dot = lambda h, c: lax.dot_general(h[...].astype(BF16), _block_diag(c[...], rep, own)[0], _DIMS['nt'],
                                           preferred_element_type=F32)
        y_ref[...] = dot(hre, cre_ref) - dot(him, cim_ref) + d_ref[...] * u_

    col, a_spec, rows, d_spec = _s5_specs()
    return pl.pallas_call(
        body, name="s5_fwd", grid=(N_GROUPS // GQ,), in_specs=[col, a_spec, rows, rows, rows, rows, d_spec],
        out_specs=col, out_shape=_out_hbm((T, D_SSM), F32),
        scratch_shapes=[pltpu.VMEM((T, GQ * SSM_STATE), F32)] * 2,
        compiler_params=_params(1, 40 << 20),
    )(_in_hbm(u), a8, bb_re, bb_im, c_re, c_im, dsk)


def _s5_bwd(u, dy, a8, bb_re, bb_im, c_re, c_im, dsk):
    def body(u_ref, dy_ref, a_ref, bre_ref, bim_ref, cre_ref, cim_ref, d_ref,
             du_ref, dbre_ref, dbim_ref, dcre_ref, dcim_ref, da_ref, dd_ref, hre, him, lre, lim):
        u_, dy_ = u_ref[...], dy_ref[...]
        us, dys = _split(u_), _split(dy_)
        ar, ai = a_ref[0], a_ref[1]
        rep, own = _group_masks()
        bres, bims = _block_diag(bre_ref[...], rep, own), _block_diag(bim_ref[...], rep, own)
        hre[...] = _dot2(us, bres)
        him[...] = _dot2(us, bims)
        h0r, h0i = _scan(hre, him, ar, ai, reverse=False)
        lre[...] = _dot2(dys, _block_diag(cre_ref[...], rep, own))
        lim[...] = -_dot2(dys, _block_diag(cim_ref[...], rep, own))
        _scan(lre, lim, ar, -ai, reverse=True)

        def acc_da(i, carry):
            lr, li = lre[_seg_rows(i), :], lim[_seg_rows(i), :]
            pr, pi = hre[_seg_rows(i - 1), :], him[_seg_rows(i - 1), :]
            return carry[0] + lr * pr + li * pi, carry[1] + li * pr - lr * pi

        lr, li = lre[_seg_rows(0), :], lim[_seg_rows(0), :]
        dar, dai = lax.fori_loop(1, SEG, acc_da, (lr * h0r + li * h0i, li * h0r - lr * h0i))
        da_ref[0:1, :] = jnp.sum(dar, axis=0, keepdims=True)
        da_ref[1:2, :] = jnp.sum(dai, axis=0, keepdims=True)
        dot = lambda p, q, mode: lax.dot_general(p, q, _DIMS[mode], preferred_element_type=F32)
        lr1, li1 = lre[...].astype(BF16), lim[...].astype(BF16)
        du_ref[...] = dy_ * d_ref[...] + dot(lr1, bres[0], 'nt') + dot(li1, bims[0], 'nt')
        dbre_ref[...] = _block_rows(dot(us[0], lr1, 'tn'), rep, own)
        dbim_ref[...] = _block_rows(dot(us[0], li1, 'tn'), rep, own)
        dcre_ref[...] = _block_rows(dot(dys[0], hre[...].astype(BF16), 'tn'), rep, own)
        dcim_ref[...] = -_block_rows(dot(dys[0], him[...].astype(BF16), 'tn'), rep, own)
        dd_ref[...] = jnp.sum(dy_ * u_, axis=0, keepdims=True)

    col, a_spec, rows, d_spec = _s5_specs()
    nq, ns = N_GROUPS // GQ, GQ * SSM_STATE
    per_row = jax.ShapeDtypeStruct((N_GROUPS * SSM_GROUP, SSM_STATE), F32)
    return pl.pallas_call(
        body, name="s5_bwd", grid=(nq,),
        in_specs=[col, col, a_spec, rows, rows, rows, rows, d_spec],
        out_specs=[col, rows, rows, rows, rows, pl.BlockSpec((None, 2, ns), lambda q: (q, 0, 0)), d_spec],
        out_shape=[_out_hbm((T, D_SSM), F32), per_row, per_row, per_row, per_row,
                   jax.ShapeDtypeStruct((nq, 2, ns), F32), jax.ShapeDtypeStruct((1, D_SSM), F32)],
        scratch_shapes=[pltpu.VMEM((T, ns), F32)] * 4,
        compiler_params=_params(1, 56 << 20),
    )(_in_hbm(u), _in_hbm(dy), a8, bb_re, bb_im, c_re, c_im, dsk)


def _to_segments(a, cols=None):
    a = a.reshape(N_SEG, SEG, a.shape[-1]).transpose(1, 0, 2)[:, :, :cols]
    return a.reshape(T, a.shape[-1])


def _from_segments(a):
    return a.reshape(SEG, N_SEG, a.shape[-1]).transpose(1, 0, 2).reshape(T, a.shape[-1])


def _ff_specs():
    up_spec = pl.BlockSpec((2, None, T, 128), lambda p, cb: (0, p, 0, cb))
    piece = pl.BlockSpec((None, T, 128), lambda p, cb: (p, 0, cb))
    cw_spec = pl.BlockSpec((None, 3, 128), lambda p, cb: (p, 0, cb))
    cb_spec = pl.BlockSpec((None, 1, 128), lambda p, cb: (p, 0, cb))
    return up_spec, piece, cw_spec, cb_spec


def _conv_gate(gate, w, cb, pad_ref):
    pad_ref[0:8, :] = jnp.zeros((8, 128), F32)
    pad_ref[8:8 + T, :] = gate
    g1, g2 = pad_ref[pl.ds(7, T), :], pad_ref[pl.ds(6, T), :]
    return w[0:1, :] * g2 + w[1:2, :] * g1 + w[2:3, :] * gate + cb, g1, g2


def _ff_act(up, cw4, cb4):
    def body(up_ref, w_ref, b_ref, o_ref, pad_ref):
        gc, _, _ = _conv_gate(up_ref[0].astype(F32), w_ref[...], b_ref[...], pad_ref)
        o_ref[...] = (jax.nn.silu(gc) * up_ref[1].astype(F32)).astype(o_ref.dtype)

    up_spec, piece, cw_spec, cb_spec = _ff_specs()
    n_cb = pl.cdiv(FF_PIECE, 128)
    return pl.pallas_call(
        body, name="ff_act", grid=(4, n_cb), in_specs=[up_spec, cw_spec, cb_spec], out_specs=piece,
        out_shape=_out_hbm((4, T, FF_PIECE), BF16),
        scratch_shapes=[pltpu.VMEM((T + 8, 128), F32)],
        compiler_params=_params(2, 24 << 20),
    )(_in_hbm(up.reshape(2, 4, T, FF_PIECE)), cw4, cb4)


def _ff_act_bwd(up, dact, cw4, cb4):
    def body(up_ref, da_ref, w_ref, b_ref, dup_ref, dw_ref, db_ref, pad_ref, pad2_ref):
        gate, val, w = up_ref[0].astype(F32), up_ref[1].astype(F32), w_ref[...]
        gc, g1, g2 = _conv_gate(gate, w, b_ref[...], pad_ref)
        sg = jax.nn.sigmoid(gc)
        da = da_ref[...].astype(F32)
        dup_ref[1] = (da * gc * sg).astype(dup_ref.dtype)
        dgc = da * val * sg * (1.0 + gc * (1.0 - sg))
        db_ref[...] = jnp.sum(dgc, axis=0, keepdims=True)
        dw_ref[0:1, :] = jnp.sum(dgc * g2, axis=0, keepdims=True)
        dw_ref[1:2, :] = jnp.sum(dgc * g1, axis=0, keepdims=True)
        dw_ref[2:3, :] = jnp.sum(dgc * gate, axis=0, keepdims=True)
        pad2_ref[0:T, :] = dgc
        pad2_ref[T:T + 8, :] = jnp.zeros((8, 128), F32)
        d1, d2 = pad2_ref[pl.ds(1, T), :], pad2_ref[pl.ds(2, T), :]
        dup_ref[0] = (w[2:3, :] * dgc + w[1:2, :] * d1 + w[0:1, :] * d2).astype(dup_ref.dtype)

    up_spec, piece, cw_spec, cb_spec = _ff_specs()
    n_cb = pl.cdiv(FF_PIECE, 128)
    dup, dw, db = pl.pallas_call(
        body, name="ff_act_bwd", grid=(4, n_cb), in_specs=[up_spec, piece, cw_spec, cb_spec],
        out_specs=[up_spec, cw_spec, cb_spec],
        out_shape=[_out_hbm((2, 4, T, FF_PIECE), BF16), jax.ShapeDtypeStruct((4, 3, FF_PIECE), F32),
                   jax.ShapeDtypeStruct((4, 1, FF_PIECE), F32)],
        scratch_shapes=[pltpu.VMEM((T + 8, 128), F32)] * 2,
        compiler_params=_params(2, 32 << 20),
    )(_in_hbm(up.reshape(2, 4, T, FF_PIECE)), _in_hbm(dact), cw4, cb4)
    return dup.reshape(8, T, FF_PIECE), dw, db


def _local_step(x, tgt, meta, p, wf, mid_weights, late_weights, send_grads, send_small):
    sm = {}
    tabs = _rope_tables()
    row = lambda a: a.reshape(1, -1)

    h0, xn, r0 = _rowwise("rms_mix", _rms_mix_fn, [(x, D, 0, N_META)], [meta, row(p['mix_norm'])],
                          [(D, F32), (D, BF16), (1, F32)], [])
    z = _mm("in_proj", xn, wf['w_in'], 'nn', TR, 640, D, F32)
    lam_re, lam_im = p['lam_re'].reshape(N_GROUPS, SSM_STATE), p['lam_im'].reshape(N_GROUPS, SSM_STATE)
    log_dt = p['log_dt'].reshape(1, N_GROUPS)
    bt_re = p['b_re'].reshape(N_GROUPS, SSM_STATE, SSM_GROUP).transpose(2, 0, 1)
    bt_im = p['b_im'].reshape(N_GROUPS, SSM_STATE, SSM_GROUP).transpose(2, 0, 1)
    c_re = p['c_re'].reshape(N_GROUPS * SSM_GROUP, SSM_STATE)
    c_im = p['c_im'].reshape(N_GROUPS * SSM_GROUP, SSM_STATE)
    a_re, a_im, bbt_re, bbt_im = _s5_prep(lam_re, lam_im, log_dt, bt_re, bt_im)
    nq, ns = N_GROUPS // GQ, GQ * SSM_STATE
    a8 = jnp.broadcast_to(jnp.stack([a_re.reshape(nq, ns), a_im.reshape(nq, ns)], 1)[:, :, None, :],
                          (nq, 2, N_SEG, ns))
    gcp = lambda t: t.transpose(1, 0, 2).reshape(N_GROUPS * SSM_GROUP, SSM_STATE)
    bb_re, bb_im = gcp(bbt_re), gcp(bbt_im)
    dsk = row(p['d_skip'])
    u_seg = _to_segments(z, D_SSM)
    y_ssm = _from_segments(_s5_fwd(u_seg, a8, bb_re, bb_im, c_re, c_im, dsk))
    wf = dict(wf, **mid_weights(y_ssm))
    g, = _rowwise("gelu", _gelu_fn, [(y_ssm, D_SSM, 0)], [], [(D_SSM, BF16)], [])
    gl = _mm("glu_proj", g, wf['w_glu'], 'nn', TR, D_SSM, D_SSM, F32)
    yan, ra = _rowwise("glu_norm", _glu_fwd_fn, [(y_ssm, D_SSM, 0), (gl, D_SSM, 0)],
                       [row(p['b_glu']), row(p['out_norm_ssm'])], [(D_SSM, BF16), (1, F32)], [])
    qn, rq, kvn, rkv = _rowwise(
        "rms_q_kv", lambda row0, zq, zkv, gq, gkv: _rms_fwd_fn(row0, zq, gq) + _rms_fwd_fn(row0, zkv, gkv),
        [(z, Q_LORA, D_SSM // Q_LORA), (z, KV_LORA, (D_SSM + Q_LORA) // KV_LORA)],
        [row(p['q_a_norm']), row(p['kv_a_norm'])], [(Q_LORA, BF16), (1, F32), (KV_LORA, BF16), (1, F32)], [])
    q = _mm("q_proj", qn, wf['w_q_b'], 'nn', T, HEAD_PAD, Q_LORA, F32, stack='b_out')
    kv = _mm("kv_proj", kvn, wf['w_kv_b'], 'nn', T, HEAD_PAD, KV_LORA, F32, stack='b_out')
    yb = _attn_fwd(q, kv, z, tabs)
    ybn, rb = _rowwise("rms_attn", _rms_fwd_fn, [(yb, D_SSM, 0)], [row(p['out_norm_attn'])],
                       [(D_SSM, BF16), (1, F32)], [])
    y = jnp.concatenate([yan, ybn], axis=1)
    wf = dict(wf)
    wf['w_out'], token, up_weights = late_weights(y)
    h1 = _mm("out_proj", y, wf['w_out'], 'nn', TR, 1024, D, F32, res=h0)
    xn2, r1 = _rowwise("rms_ffn", _rms_fwd_fn, [(h1, D, 0)], [row(p['ffn_norm']) + token[0:1, 0:1]],
                       [(D, BF16), (1, F32)], [])
    wf['w_up'], token, down_weights = up_weights(xn2)
    up = _mm("up_proj", xn2, wf['w_up'], 'nn', TR, FF_PIECE, D, BF16, stack='b_out')
    cw4 = wf['conv_w'].reshape(3, 4, FF_PIECE).transpose(1, 0, 2)
    cb4 = p['conv_b'].reshape(4, 1, FF_PIECE)
    act = _ff_act(up, cw4, cb4 + token[0:1, 0:1])
    wf['w_down'] = down_weights(act)
    wd4 = wf['w_down'].reshape(4, FF_PIECE, D)
    h2 = _mm("down_proj", act, wd4, 'nn', TR, 1024, FF_PIECE, F32, stack='ab_red', res=h1)
    dh2, sm['final_norm'], loss = _rowwise("loss", _loss_fn, [(h2, D, 0), (tgt, D, 0, N_META)], [row(p['final_norm'])],
                                           [(D, F32)], [D, 128])

    big = {}
    dact = _mm("down_bwd_x", dh2, wd4, 'nt', TR, FF_PIECE, D, BF16, stack='b_out')
    g_down = _mm("down_bwd_w", act, dh2, 'tn', FF_PIECE, 512, T, BF16, stack='a_out').reshape(
        N_DEV, D_FF // N_DEV, D)
    token = send_grads(['w_down'], [g_down], 'cores')
    dup, dcw4, sm['conv_b'] = _ff_act_bwd(up, dact, cw4, cb4 + token[0:1, 0:1])
    dxn2 = _mm("up_bwd_x", dup, wf['w_up'], 'nt', TR, D, FF_PIECE, F32, stack='ab_red')
    g_up = _mm("up_bwd_w", dup, xn2, 'tn', FF_PIECE, 1024, T, BF16, stack='a_out')
    token = send_grads(['w_up'], [g_up], 'cores') + send_grads(['w_down'], dxn2, 'chips')
    dh1, sm['ffn_norm'] = _rowwise("rms_ffn_bwd", _rms_bwd_res_fn,
                                   [(dxn2, D, 0), (h1, D, 0), (r1, 1, 0), (dh2, D, 0)],
                                   [row(p['ffn_norm']) + token[0:1, 0:1]], [(D, F32)], [D])
    dy = _mm("out_bwd_x", dh1, wf['w_out'], 'nt', TR, 1024, D, F32)
    g_out = _mm("out_bwd_w", y, dh1, 'tn', 1024, 1024, T, BF16).reshape(N_DEV, D // N_DEV, D)
    dgl, dgd, sm['out_norm_ssm'], sm['b_glu'] = _rowwise(
        "glu_bwd", _glu_bwd_fn, [(dy, D_SSM, 0), (y_ssm, D_SSM, 0), (gl, D_SSM, 0), (ra, 1, 0)],
        [row(p['b_glu']), row(p['out_norm_ssm'])], [(D_SSM, BF16), (D_SSM, F32)], [D_SSM, D_SSM])
    dg = _mm("glu_bwd_x", dgl, wf['w_glu'], 'nt', TR, D_SSM, D_SSM, F32, res=dgd)
    g_glu = _mm("glu_bwd_w", g, dgl, 'tn', D_SSM, D_SSM, T, BF16).reshape(N_DEV, D_SSM // N_DEV, D_SSM)
    dy_ssm, = _rowwise("gelu_bwd", _gelu_bwd_fn, [(dg, D_SSM, 0), (y_ssm, D_SSM, 0)], [], [(D_SSM, F32)], [])
    token = send_grads(['w_up'], g_glu, 'chips')
    dyb, sm['out_norm_attn'] = _rowwise("rms_attn_bwd", _rms_bwd_fn, [(dy, D_SSM, 1), (yb, D_SSM, 0), (rb, 1, 0)],
                                        [row(p['out_norm_attn']) + token[0:1, 0:1]], [(D_SSM, F32)], [D_SSM])
    dq, dkv, dkpe = _attn_bwd(q, kv, z, tabs, dyb)
    g_q = _mm("q_bwd_w", qn, dq, 'tn', Q_LORA, HEAD_PAD, T, BF16, stack='b_out')
    dqn = _mm("q_bwd_x", dq, wf['w_q_b'], 'nt', T, Q_LORA, HEAD_PAD, F32, stack='ab_red')
    g_kv = _mm("kv_bwd_w", kvn, dkv, 'tn', KV_LORA, HEAD_PAD, T, BF16, stack='b_out')
    dkvn = _mm("kv_bwd_x", dkv, wf['w_kv_b'], 'nt', T, KV_LORA, HEAD_PAD, F32, stack='ab_red')
    token = send_grads(['w_out', 'w_glu', 'w_q_b', 'w_kv_b'], [g_out, g_glu, g_q, g_kv], 'both')
    def both_bwd(row0, dq_, zq, rq_, dkv_, zkv, rkv_, gq, gkv):
        (dzq, dgq), (dzkv, dgkv) = _rms_bwd(dq_, zq, rq_, gq), _rms_bwd(dkv_, zkv, rkv_, gkv)
        return dzq, dzkv, dgq, dgkv

    dq_a, dkv_a, sm['q_a_norm'], sm['kv_a_norm'] = _rowwise(
        "rms_q_kv_bwd", both_bwd,
        [(dqn, Q_LORA, 0), (z, Q_LORA, D_SSM // Q_LORA), (rq, 1, 0),
         (dkvn, KV_LORA, 0), (z, KV_LORA, (D_SSM + Q_LORA) // KV_LORA), (rkv, 1, 0)],
        [row(p['q_a_norm']), row(p['kv_a_norm'])], [(Q_LORA, BF16), (KV_LORA, BF16)], [Q_LORA, KV_LORA])
    dk_pe = _kpe_bwd(dkpe, tabs)
    du_seg, dbre, dbim, dcre, dcim, da, sm['d_skip'] = _s5_bwd(
        u_seg, _to_segments(dy_ssm), a8, bb_re, bb_im, c_re, c_im, dsk + token[0:1, 0:1])
    du = _from_segments(du_seg).astype(BF16)
    cgp = lambda d: d.reshape(N_GROUPS, SSM_GROUP, SSM_STATE).transpose(1, 0, 2)
    dlam_re, dlam_im, dlog_dt, dbt_re, dbt_im = _s5_prep_bwd(
        lam_re, lam_im, log_dt, bt_re, bt_im, da[:, 0, :].reshape(N_GROUPS, SSM_STATE),
        da[:, 1, :].reshape(N_GROUPS, SSM_STATE), cgp(dbre), cgp(dbim))
    sm['lam_re'], sm['lam_im'], sm['log_dt'] = dlam_re, dlam_im, dlog_dt
    sm['b_re'], sm['b_im'] = gcp(dbt_re), gcp(dbt_im)
    sm['c_re'], sm['c_im'] = dcre, dcim
    token = send_small(sm, dcw4, loss)
    dz = jnp.concatenate([du, dq_a, dkv_a, dk_pe], axis=1)
    dxn = _mm("in_bwd_x", dz, wf['w_in'], 'nt', TR, 1024, D_IN_PAD, F32)
    big['w_in'] = _mm("in_bwd_w", xn, dz, 'tn', 1024, 640, T, BF16).reshape(N_DEV, D // N_DEV, D_IN_PAD)
    dh0, d_mix_norm = _rowwise("rms_mix_bwd", _rms_bwd_res_fn,
                               [(dxn, D, 0), (h0, D, 0), (r0, 1, 0), (dh1, D, 0)],
                               [row(p['mix_norm']) + token[0:1, 0:1]], [(D, F32)], [D])
    return dh0[N_META:], dh0[:N_META], d_mix_norm, big


def _place():
    x, y, c = lax.axis_index("x"), lax.axis_index("y"), lax.axis_index("c")
    return x, y, c, [(1 - x, y), (x, 1 - y), (1 - x, 1 - y)]


_HBM = pl.BlockSpec(memory_space=pltpu.HBM)


def _all_gather(name, arrs, chip_blocks=()):
    n, nc = len(arrs), len(chip_blocks)

    def body(*refs):
        ins, cins, outs, couts = refs[:n], refs[n:n + nc], refs[n + nc:2 * n + nc], refs[2 * n + nc:2 * (n + nc)]
        send, recv, loc, csend, crecv = refs[2 * (n + nc):]
        x, y, c, chips = _place()
        me, sib = (x, y, c), (x, y, 1 - c)

        def cp(a, k, blk, to, src=None):
            dst = outs[a].at[4 * blk[0] + 2 * blk[1] + blk[2]]
            return pltpu.make_async_remote_copy(src_ref=dst if src is None else src, dst_ref=dst,
                                                send_sem=send.at[a, k], recv_sem=recv.at[a, k],
                                                device_id=to, device_id_type=MESH)

        between_chips = [pltpu.make_async_remote_copy(src_ref=cins[a].at[2 * ch[0] + ch[1]], dst_ref=couts[a].at[j],
                                                      send_sem=csend.at[a, j], recv_sem=crecv.at[a, j],
                                                      device_id=(*ch, c), device_id_type=MESH)
                         for a in range(nc) for j, ch in enumerate(chips)]
        mine = [pltpu.make_async_copy(ins[a], outs[a].at[4 * x + 2 * y + c], loc.at[a]) for a in range(n)]
        first = []
        for a in range(n):
            first.append(cp(a, 0, me, sib, src=ins[a]))
            first += [cp(a, 1 + j, me, (*ch, c), src=ins[a]) for j, ch in enumerate(chips)]
        for f in mine + first + between_chips:
            f.start()
        passed = []
        for a in range(n):
            for j, ch in enumerate(chips):
                cp(a, 1 + j, (*ch, c), me).wait_recv()
                passed.append(cp(a, 4 + j, (*ch, c), sib))
                passed[-1].start()
        for a in range(n):
            cp(a, 0, sib, me).wait_recv()
            for j, ch in enumerate(chips):
                cp(a, 4 + j, (*ch, 1 - c), me).wait_recv()
        for f in first + passed:
            f.wait_send()
        for m in mine + between_chips:
            m.wait()

    out = pl.pallas_call(
        body, name=name, in_specs=[_HBM] * (n + nc), out_specs=[_HBM] * (n + nc),
        out_shape=[jax.ShapeDtypeStruct((N_DEV,) + a.shape, a.dtype) for a in arrs]
        + [jax.ShapeDtypeStruct((3,) + a.shape[1:], a.dtype) for a in chip_blocks],
        scratch_shapes=[pltpu.SemaphoreType.DMA((n, 7)), pltpu.SemaphoreType.DMA((n, 7)),
                        pltpu.SemaphoreType.DMA((n,)), pltpu.SemaphoreType.DMA((max(nc, 1), 3)),
                        pltpu.SemaphoreType.DMA((max(nc, 1), 3))],
    )(*arrs, *chip_blocks)
    return (out[:n], out[n:]) if nc else out


def _exchange_cores(name, arrs):
    n = len(arrs)

    def body(*refs):
        ins, outs = refs[:n], refs[n:2 * n]
        send, recv = refs[2 * n:]
        x, y, c, _ = _place()
        remote = [pltpu.make_async_remote_copy(src_ref=ins[a].at[2 * k + 1 - c], dst_ref=outs[a].at[k],
                                               send_sem=send.at[a, k], recv_sem=recv.at[a, k],
                                               device_id=(x, y, 1 - c), device_id_type=MESH)
                  for a in range(n) for k in range(4)]
        for d in remote:
            d.start()
        for d in remote:
            d.wait()

    return pl.pallas_call(
        body, name=name, in_specs=[_HBM] * n, out_specs=[_HBM] * n,
        out_shape=[jax.ShapeDtypeStruct((4,) + a.shape[1:], a.dtype) for a in arrs],
        scratch_shapes=[pltpu.SemaphoreType.DMA((n, 4))] * 2,
    )(*arrs)


_SEM = pl.BlockSpec(memory_space=pltpu.SEMAPHORE)
_EFFECT = pltpu.SideEffectType.DATAFLOW_SIDE_EFFECTING


def _split_exchange(name, srcs, land_shapes, copies_of, n_cp, after=None, all_of=None):
    n = len(srcs)
    in_place = land_shapes is None
    lands = [] if in_place else [pltpu.with_memory_space_constraint(lax.empty(s, a.dtype), pltpu.HBM)
                                 for s, a in zip(land_shapes, srcs)]
    nb = n + len(lands)

    per = 1 if all_of else n_cp
    n_sem = n * per

    def descriptors(refs, send, recv, mirror):
        src_refs, land_refs = refs[:n], refs[nb - n:nb]
        x, y, c, chips = _place()
        out = []
        for a in range(n):
            if mirror and all_of:
                whole = all_of(a, land_refs[a])
                out.append(pltpu.make_async_remote_copy(src_ref=whole, dst_ref=whole, send_sem=send[a],
                                                        recv_sem=recv[a], device_id=(x, y, 1 - c),
                                                        device_id_type=MESH))
                continue
            for k, (src, dst, to, back) in enumerate(copies_of(a, src_refs[a], land_refs[a], x, y, c, chips)):
                out.append(pltpu.make_async_remote_copy(src_ref=src, dst_ref=back if mirror else dst,
                                                        send_sem=send[a * per + k % per],
                                                        recv_sem=recv[a * per + k % per],
                                                        device_id=to, device_id_type=MESH))
        return out

    def start_body(*refs):
        first = nb + (after is not None)
        sems = refs[first:first + 2 * n_sem]
        for d in descriptors(refs, sems[:n_sem], sems[n_sem:], False):
            d.start()
        refs[-1][...] = jnp.zeros((8, 128), F32)

    hbm_like = lambda arrs: [pltpu.HBM(a.shape, a.dtype) for a in arrs]
    order = [] if after is None else [after]
    res = pl.pallas_call(
        start_body, name=name + "_start", in_specs=[_HBM] * nb + [pl.BlockSpec(memory_space=pl.ANY)] * len(order),
        out_specs=[_SEM] * (2 * n_sem) + [_HBM] * nb + [pl.BlockSpec(memory_space=pltpu.VMEM)],
        out_shape=[pltpu.SemaphoreType.DMA(())] * (2 * n_sem)
        + hbm_like(srcs) + hbm_like(lands) + [jax.ShapeDtypeStruct((8, 128), F32)],
        input_output_aliases={i: 2 * n_sem + i for i in range(nb)},
        compiler_params=pltpu.CompilerParams(has_side_effects=_EFFECT),
    )(*srcs, *lands, *order)
    sems, thru, token = res[:2 * n_sem], res[2 * n_sem:2 * n_sem + nb], res[-1]

    def wait(after):
        def wait_body(*refs):
            s = refs[nb:nb + 2 * n_sem]
            for d in descriptors(refs, s[:n_sem], s[n_sem:], True):
                d.wait_send()
                d.wait_recv()

        out = pl.pallas_call(
            wait_body, name=name + "_wait",
            in_specs=[_HBM] * nb + [_SEM] * (2 * n_sem) + [pl.BlockSpec(memory_space=pl.ANY)],
            out_specs=[_HBM] * nb, out_shape=hbm_like(srcs) + hbm_like(lands),
            input_output_aliases={i: i for i in range(nb)},
            compiler_params=pltpu.CompilerParams(has_side_effects=_EFFECT),
        )(*thru, *sems, after)
        return out[:n], out[nb - n:]

    return token, wait


def _gather_copies(a, src, land, x, y, c, chips):
    peers = [(x, y, 1 - c)] + [(*ch, c) for ch in chips]
    me = 4 * x + 2 * y + c
    return [(src, land.at[me], to, land.at[4 * to[0] + 2 * to[1] + to[2]]) for to in peers]


def _to_all_copies(a, src, land, x, y, c, chips):
    peers = [(x, y, 1 - c)] + [(*ch, cc) for ch in chips for cc in (c, 1 - c)]
    me = 4 * x + 2 * y + c
    return [(src, land.at[me], to, land.at[4 * to[0] + 2 * to[1] + to[2]]) for to in peers]


def _cores_copies(a, src, land, x, y, c, chips):
    return [(src.at[2 * k + 1 - c], land.at[k], (x, y, 1 - c), land.at[k]) for k in range(4)]


def _forward_copies(a, src, land, x, y, c, chips):
    slot = lambda ch, core: 4 * ch[0] + 2 * ch[1] + core
    return [(land.at[slot(ch, c)], land.at[slot(ch, c)], (x, y, 1 - c), land.at[slot(ch, 1 - c)]) for ch in chips]


def _chips_copies(a, src, land, x, y, c, chips):
    return [(src.at[2 * ch[0] + ch[1]], land.at[j], (*ch, c), land.at[j]) for j, ch in enumerate(chips)]


def _gather_forward(name, arrs):
    n = len(arrs)

    def body(*refs):
        ins, outs = refs[:n], refs[n:2 * n]
        send, recv = refs[2 * n:]
        x, y, c, chips = _place()
        sends, recvs = [], []
        for a in range(n):
            for j, ch in enumerate(chips):
                mine, theirs = 4 * ch[0] + 2 * ch[1] + c, 4 * ch[0] + 2 * ch[1] + 1 - c
                sends.append(pltpu.make_async_remote_copy(src_ref=ins[a].at[mine], dst_ref=outs[a].at[mine],
                                                          send_sem=send.at[a, j], recv_sem=recv.at[a, j],
                                                          device_id=(x, y, 1 - c), device_id_type=MESH))
                recvs.append(pltpu.make_async_remote_copy(src_ref=ins[a].at[theirs], dst_ref=outs[a].at[theirs],
                                                          send_sem=send.at[a, j], recv_sem=recv.at[a, j],
                                                          device_id=(x, y, 1 - c), device_id_type=MESH))
        for d in sends:
            d.start()
        for s, r in zip(sends, recvs):
            s.wait_send()
            r.wait_recv()

    return pl.pallas_call(
        body, name=name, in_specs=[_HBM] * n, out_specs=[_HBM] * n,
        out_shape=[jax.ShapeDtypeStruct(a.shape, a.dtype) for a in arrs],
        input_output_aliases={i: i for i in range(n)},
        scratch_shapes=[pltpu.SemaphoreType.DMA((n, 3))] * 2,
    )(*arrs)


def _blocks_of(r, c):
    if r * c * 4 <= (2 << 20):
        return r, c
    if r % 128 == 0:
        return 128, c
    return r, 256


def _pair_sum(name, own, got, core):
    _, r, c = got.shape
    rb, cb = _blocks_of(r, c)

    def body(s_ref, a_ref, b_ref, o_ref):
        o_ref[...] = (a_ref[...].astype(F32) + b_ref[...].astype(F32)).astype(o_ref.dtype)

    return pl.pallas_call(
        body, name=name, out_shape=_out_hbm((4, r, c), got.dtype),
        grid_spec=pltpu.PrefetchScalarGridSpec(
            num_scalar_prefetch=1, grid=(4, r // rb, c // cb),
            in_specs=[pl.BlockSpec((None, rb, cb), lambda k, i, j, s: (2 * k + s[0], i, j)),
                      pl.BlockSpec((None, rb, cb), lambda k, i, j, s: (k, i, j))],
            out_specs=pl.BlockSpec((None, rb, cb), lambda k, i, j, s: (k, i, j))),
        compiler_params=_params(3, 24 << 20),
    )(core, _in_hbm(own), _in_hbm(got))


def _adamw_math(w, g, m, v):
    m = B1 * m + (1.0 - B1) * g
    v = B2 * v + (1.0 - B2) * (g * g)
    m_hat = m / (1.0 - B1 ** STEP)
    v_hat = v / (1.0 - B2 ** STEP)
    return -LR * (m_hat / (jnp.sqrt(v_hat) + ADAM_EPS) + WD * w), m, v


def _adamw_big(name, own, got, chip, w, m, v, behind):
    r, c = w.shape
    rb, cb = _blocks_of(r, c)

    def body(s_ref, o_ref, p_ref, w_ref, m_ref, v_ref, behind_ref, g_ref, d_ref, nm_ref, nv_ref):
        g = o_ref[...].astype(F32)
        for k in range(3):
            g = g + p_ref[k].astype(F32)
        g_ref[...] = g
        d_ref[...], nm_ref[...], nv_ref[...] = _adamw_math(w_ref[...], g, m_ref[...], v_ref[...])

    blk = pl.BlockSpec((rb, cb), lambda i, j, s: (i, j))
    return pl.pallas_call(
        body, name=name, out_shape=[jax.ShapeDtypeStruct((r, c), F32)] * 4,
        grid_spec=pltpu.PrefetchScalarGridSpec(
            num_scalar_prefetch=1, grid=(r // rb, c // cb),
            in_specs=[pl.BlockSpec((None, rb, cb), lambda i, j, s: (s[0], i, j)),
                      pl.BlockSpec((3, rb, cb), lambda i, j, s: (0, i, j)), blk, blk, blk,
                      pl.BlockSpec(memory_space=pl.ANY)],
            out_specs=[blk] * 4),
        compiler_params=_params(2, 40 << 20),
    )(chip, *map(_in_hbm, (own, got, w, m, v)), behind)


def _adamw_multi(name, items, sums=(), row_blocks=1, own=(), dev=None):
    n, ns = len(items), len(sums)
    n_own = len(own)
    assert n_own in (0, n + ns)

    def body(*refs):
        dev_ref = refs[0] if n_own else None
        refs = refs[1:] if n_own else refs
        owns, refs = refs[:n_own], refs[n_own:]
        ins, outs = refs[:4 * n + ns], refs[4 * n + ns:]

        def total(ref, t):
            g = None
            for d in range(ref.shape[0]):
                part = jnp.where(dev_ref[0] == d, owns[t][...], ref[d]) if n_own else ref[d]
                g = part.astype(F32) if g is None else g + part.astype(F32)
            return g

        for t in range(n):
            p_ref, w_ref, m_ref, v_ref = ins[4 * t:4 * t + 4]
            g = total(p_ref, t)
            outs[4 * t][...] = g
            outs[4 * t + 1][...], outs[4 * t + 2][...], outs[4 * t + 3][...] = _adamw_math(
                w_ref[...], g, m_ref[...], v_ref[...])
        for t in range(ns):
            outs[4 * n + t][...] = total(ins[4 * n + t], n + t)

    def spec(shape, lead):
        blk = (shape[0] // row_blocks,) + tuple(shape[1:])
        nd = len(shape)
        if lead:
            return pl.BlockSpec((lead,) + blk, lambda i: (0, i) + (0,) * (nd - 1))
        return pl.BlockSpec(blk, lambda i: (i,) + (0,) * (nd - 1))

    operands, in_specs, out_specs, out_shape = [], [], [], []
    if n_own:
        operands += [dev] + list(own)
        in_specs += [pl.BlockSpec(memory_space=pltpu.SMEM)] + [spec(o.shape, 0) for o in own]
    for parts, w, m, v in items:
        assert parts.shape[1:] == w.shape == m.shape == v.shape, (name, parts.shape, w.shape)
        operands += [parts, w, m, v]
        in_specs += [spec(w.shape, parts.shape[0])] + [spec(w.shape, 0)] * 3
        out_specs += [spec(w.shape, 0)] * 4
        out_shape += [jax.ShapeDtypeStruct(w.shape, F32)] * 4
    for parts in sums:
        operands.append(parts)
        in_specs.append(spec(parts.shape[1:], parts.shape[0]))
        out_specs.append(spec(parts.shape[1:], False))
        out_shape.append(jax.ShapeDtypeStruct(parts.shape[1:], F32))
    return pl.pallas_call(
        body, name=name, grid=(row_blocks,), in_specs=in_specs, out_specs=out_specs, out_shape=out_shape,
        compiler_params=_params(1, 56 << 20),
    )(*operands)


def kernel(x, meta_tokens, mix_norm, w_in, lam_re, lam_im, log_dt, b_re, b_im, c_re, c_im, d_skip, w_glu, b_glu, q_a_norm, w_q_b, kv_a_norm, w_kv_b, out_norm_ssm, out_norm_attn, w_out, ffn_norm, w_up, conv_w, conv_b, w_down, final_norm, loss_target, m_meta_tokens, m_mix_norm, m_w_in, m_lam_re, m_lam_im, m_log_dt, m_b_re, m_b_im, m_c_re, m_c_im, m_d_skip, m_w_glu, m_b_glu, m_q_a_norm, m_w_q_b, m_kv_a_norm, m_w_kv_b, m_out_norm_ssm, m_out_norm_attn, m_w_out, m_ffn_norm, m_w_up, m_conv_w, m_conv_b, m_w_down, m_final_norm, v_meta_tokens, v_mix_norm, v_w_in, v_lam_re, v_lam_im, v_log_dt, v_b_re, v_b_im, v_c_re, v_c_im, v_d_skip, v_w_glu, v_b_glu, v_q_a_norm, v_w_q_b, v_kv_a_norm, v_w_kv_b, v_out_norm_ssm, v_out_norm_attn, v_w_out, v_ffn_norm, v_w_up, v_conv_w, v_conv_b, v_w_down, v_final_norm):
    given = dict(locals())
    w = {n: given[n] for n in WEIGHTS}
    m = {n: given['m_' + n] for n in WEIGHTS}
    v = {n: given['v_' + n] for n in WEIGHTS}
    dev = 4 * lax.axis_index("x") + 2 * lax.axis_index("y") + lax.axis_index("c")

    shard = {
        'w_in': jnp.pad(w_in[0], ((0, 0), (0, D_IN_PAD - D_IN))),
        'w_glu': w_glu[0],
        'w_q_b': jnp.pad(w_q_b[0], ((0, 0), (0, HEAD_PAD - QK_NOPE - QK_ROPE))),
        'w_kv_b': w_kv_b[0],
        'w_out': w_out[0],
        'w_up': w_up[0],
        'w_down': w_down[0],
    }
    core = lax.axis_index("c").astype(jnp.int32).reshape(1)
    chip = (2 * lax.axis_index("x") + lax.axis_index("y")).astype(jnp.int32).reshape(1)
    mid, late = ['w_q_b', 'w_kv_b', 'w_glu'], ['w_out', 'w_up', 'w_down']
    shard = {n: a.astype(BF16) for n, a in shard.items()}

    gathered = _all_gather("gather_early", [shard['w_in'], meta_tokens, jnp.pad(conv_w[0], ((0, 5), (0, 0)))])
    wf = {'w_in': gathered[0].reshape(-1, D_IN_PAD)}
    meta = gathered[1].transpose(1, 0, 2).reshape(N_META, D)
    wf['conv_w'] = gathered[2][:, :3].transpose(1, 0, 2).reshape(3, D_FF)
    mid_token, wait_mid = _split_exchange("gather_mid", [shard[n] for n in mid],
                                          [(N_DEV,) + shard[n].shape for n in mid], _gather_copies, 4,
                                          after=gathered[0])
    gather_token, late_waits = mid_token, {}
    for n in late:
        gather_token, late_waits[n] = _split_exchange("gather_" + n, [shard[n]], [(N_DEV,) + shard[n].shape],
                                                      _gather_copies, 4, after=gather_token)

    def with_mine(mine, landed):
        return [lax.dynamic_update_slice(a, s[None], (dev, 0, 0)) for s, a in zip(mine, landed)]

    def mid_weights(after):
        full = _gather_forward("gather_forward_mid", with_mine(*wait_mid(after)))
        return {'w_q_b': full[0], 'w_kv_b': full[1], 'w_glu': full[2].reshape(-1, D_SSM)}

    def late_weights(after):
        full_out, = _gather_forward("gather_forward_w_out", with_mine(*late_waits['w_out'](after)))
        token, wait_up = _split_exchange("gather_forward_w_up", with_mine(*late_waits['w_up'](after)), None,
                                         _forward_copies, 3, after=full_out)

        def up_weights(after):
            full_up, = wait_up(after)[0]
            token, wait_down = _split_exchange("gather_forward_w_down", with_mine(*late_waits['w_down'](after)),
                                               None, _forward_copies, 3, after=full_up)
            return full_up, token, lambda after: wait_down(after)[0][0].reshape(-1, D)

        return full_out.reshape(-1, D), token, up_weights

    pending, between_cores = [], {}

    def send_grads(names, arg, stage):
        if stage == 'cores':
            token, between_cores[names[0]] = _split_exchange(
                "reduce_cores_" + names[0], arg, [(4,) + a.shape[1:] for a in arg], _cores_copies, 4)
            return token
        mine, got = between_cores.pop(names[0])(arg) if stage == 'chips' else (
            arg, _exchange_cores("reduce_cores_" + names[0], arg))
        sums = [_pair_sum("sum_cores_" + n, a, g, core) for n, a, g in zip(names, mine, got)]
        token, wait = _split_exchange("reduce_chips_" + names[0], sums, [(3,) + s.shape[1:] for s in sums],
                                      _chips_copies, 3)
        pending.append((names, wait))
        return token

    small_sent = []
    s5_bc = ['b_re', 'b_im', 'c_re', 'c_im']

    def send_small(sm, dcw4, loss_part):
        names = [n for n in SMALL if n in sm]
        arrs = [sm[n].reshape(-1, 128).astype(BF16) if n in s5_bc else sm[n] for n in names] + [dcw4, loss_part]
        token, wait = _split_exchange("gather_small", arrs, [(N_DEV,) + a.shape for a in arrs], _to_all_copies, 7,
                                      all_of=lambda a, land: land.at[pl.ds(0, N_DEV - 1)])
        small_sent.append((names, wait))
        return token

    small = {n: w[n] for n in SMALL}
    small['mix_norm'] = mix_norm + gather_token[0:1, 0:1]
    grad_x, dmeta, d_mix_norm, big = _local_step(x[0], loss_target[0], meta, small, wf, mid_weights, late_weights,
                                                  send_grads, send_small)
    grads, deltas, new_m, new_v = {}, {}, {}, {}

    def keep(n, outs):
        grads[n], deltas[n], new_m[n], new_v[n] = (o.reshape(w[n].shape) for o in outs)

    rest = list(big)
    from_core = _exchange_cores("reduce_cores", [big[n] for n in rest])
    chip_sums = [_pair_sum("sum_cores_" + n, big[n], got, core) for n, got in zip(rest, from_core)]
    last = chip_sums + [d_mix_norm, dmeta]
    n_big = len(chip_sums)
    last_token, wait_last = _split_exchange(
        "reduce_last", last, [(3,) + s.shape[1:] for s in chip_sums] + [(N_DEV,) + a.shape for a in last[n_big:]],
        lambda a, *at: _chips_copies(a, *at) if a < n_big else _to_all_copies(a, *at), 7,
        all_of=lambda a, land: land if a < n_big else land.at[pl.ds(0, N_DEV - 1)])

    def update(n, own, got):
        if n == 'w_up':
            outs = _adamw_big("adamw_" + n, own, got, chip, w[n][0].T, m[n][0].T, v[n][0].T, last_token)
            return keep(n, [o.T for o in outs])
        cols = w[n].shape[2]
        keep(n, _adamw_big("adamw_" + n, own[:, :, :cols], got[:, :, :cols], chip, w[n][0], m[n][0], v[n][0],
                           last_token))

    for names, wait in pending:
        for n, own, got in zip(names, *wait(chip_sums[0])):
            update(n, own, got)
    updated = [new_v[n].reshape(-1)[:1] for names, _ in pending for n in names]
    sent, landed = wait_last(functools.reduce(jnp.add, updated))
    for n, own, got in zip(rest, sent, landed):
        update(n, own, got)

    device = dev.astype(jnp.int32).reshape(1)
    outs = _adamw_multi("adamw_last", [(landed[n_big], mix_norm, m['mix_norm'], v['mix_norm'])],
                        sums=[landed[n_big + 1]], own=sent[n_big:], dev=device)
    keep('mix_norm', outs[:4])
    g_meta = outs[4]
    names, wait = small_sent[0]
    mine, landed = wait(grads['mix_norm'])
    parts, own = dict(zip(names + ['conv_w', 'loss'], landed)), dict(zip(names + ['conv_w', 'loss'], mine))
    wide = [n for n in names if n not in s5_bc]
    summed = ['conv_w', 'loss'] + s5_bc
    outs = _adamw_multi("adamw_small",
                        [(parts[n],) + tuple(t[n].reshape(own[n].shape) for t in (w, m, v)) for n in wide],
                        sums=[parts[n] for n in summed], own=[own[n] for n in wide + summed], dev=device)
    for i, n in enumerate(wide):
        keep(n, outs[4 * i:4 * i + 4])
    g_cw4, loss = outs[-6:-4]
    rows_gc = lambda a, n: (a.reshape(N_GROUPS, SSM_STATE, SSM_GROUP).transpose(0, 2, 1) if n[0] == 'b' else a
                            ).reshape(N_GROUPS * SSM_GROUP, SSM_STATE)
    outs = _adamw_multi("adamw_s5_bc", [(g.reshape(1, N_GROUPS * SSM_GROUP, SSM_STATE),)
                                        + tuple(rows_gc(t[n], n) for t in (w, m, v))
                                        for n, g in zip(s5_bc, outs[-4:])])
    for i, n in enumerate(s5_bc):
        back = outs[4 * i:4 * i + 4]
        if n[0] == 'b':
            back = [o.reshape(N_GROUPS, SSM_GROUP, SSM_STATE).transpose(0, 2, 1) for o in back]
        keep(n, back)

    g_meta = lax.dynamic_slice(g_meta, (0, dev * (D // N_DEV)), (N_META, D // N_DEV))
    g_conv = lax.dynamic_slice(g_cw4.transpose(1, 0, 2).reshape(3, D_FF), (0, dev * (D_FF // N_DEV)),
                               (3, D_FF // N_DEV))
    rows8 = lambda a: jnp.pad(a.reshape(3, D_FF // N_DEV), ((0, 5), (0, 0)))
    outs = _adamw_multi("adamw_cols", [(g_meta[None], meta_tokens, m['meta_tokens'], v['meta_tokens']),
                                       (rows8(g_conv)[None], rows8(conv_w), rows8(m['conv_w']), rows8(v['conv_w']))])
    keep('meta_tokens', outs[:4])
    keep('conv_w', [o[:3] for o in outs[4:]])

    return (loss[0, 0], grad_x[None], *[grads[n] for n in WEIGHTS], *[deltas[n] for n in WEIGHTS],
            *[new_m[n] for n in WEIGHTS], *[new_v[n] for n in WEIGHTS])
```

```python
import functools
import math

import jax
import jax.numpy as jnp
from jax import lax
from jax.experimental import pallas as pl
from jax.experimental.pallas import tpu as pltpu

F32, BF16 = jnp.float32, jnp.bfloat16

N_DEV = 8
N_META, SEQ, D = 16, 2048, 2048
T = N_META + SEQ
TR = 688
EPS = 1e-6
D_SSM, Q_LORA, KV_LORA, QK_ROPE = 1024, 512, 256, 64
D_IN = D_SSM + Q_LORA + KV_LORA + QK_ROPE
D_IN_PAD = 1920
N_HEADS, QK_NOPE, V_HEAD = 8, 128, 128
HEAD_PAD = 256
SM_SCALE = 1.0 / math.sqrt(QK_NOPE + QK_ROPE)
CHUNK = 64
N_GROUPS, SSM_GROUP, SSM_STATE = 64, 16, 64
N_SEG = 8
SEG = 264
TP = N_SEG * SEG
GQ = 8
D_FF = 5504
FF_PIECE = 1376
ROPE_BASE = 10000.0
LR, B1, B2, ADAM_EPS, WD, STEP = 0.001, 0.9, 0.999, 1e-08, 0.01, 10
VMEM_CAP = 60 * 1024 * 1024
MESH = pl.DeviceIdType.MESH

WEIGHTS = ['meta_tokens', 'mix_norm', 'w_in', 'lam_re', 'lam_im', 'log_dt', 'b_re', 'b_im', 'c_re', 'c_im',
           'd_skip', 'w_glu', 'b_glu', 'q_a_norm', 'w_q_b', 'kv_a_norm', 'w_kv_b', 'out_norm_ssm',
           'out_norm_attn', 'w_out', 'ffn_norm', 'w_up', 'conv_w', 'conv_b', 'w_down', 'final_norm']
BIG = ['w_in', 'w_glu', 'w_q_b', 'w_kv_b', 'w_out', 'w_up', 'w_down']
SMALL = [n for n in WEIGHTS if n not in BIG and n not in ('meta_tokens', 'conv_w')]


def _nbytes(shape, dtype):
    return math.prod(shape) * jnp.dtype(dtype).itemsize


def _in_hbm(a):
    return pltpu.with_memory_space_constraint(a, pltpu.HBM) if a.size * a.dtype.itemsize >= (1 << 20) else a


def _out_hbm(shape, dtype):
    big = math.prod(shape) * jnp.dtype(dtype).itemsize >= (1 << 20)
    return pltpu.HBM(shape, dtype) if big else jax.ShapeDtypeStruct(shape, dtype)


def _params(sem, need):
    return pltpu.CompilerParams(dimension_semantics=("arbitrary",) * sem,
                                vmem_limit_bytes=int(min(VMEM_CAP, max(need, 16 * 1024 * 1024))))


_DIMS = {'nn': (((1,), (0,)), ((), ())), 'nt': (((1,), (1,)), ((), ())), 'tn': (((0,), (0,)), ((), ()))}


def _mm(name, a, b, mode, tm, tn, tk, out_dtype, stack=None, res=None):
    sa, sb, so = stack in ('a_out', 'ab_red'), stack in ('b_out', 'ab_red'), stack in ('a_out', 'b_out')
    nj = a.shape[0] if sa else (b.shape[0] if sb else 1)
    a2, b2 = a.shape[-2:], b.shape[-2:]
    if mode == 'nn':
        (m, k), (k2, n) = a2, b2
    elif mode == 'nt':
        (m, k), (n, k2) = a2, b2
    else:
        (k, m), (k2, n) = a2, b2
    assert k == k2 and m % tm == 0 and n % tn == 0 and k % tk == 0, (name, a.shape, b.shape)
    n_jo, n_jr, n_k = (nj if so else 1), (nj if stack == 'ab_red' else 1), k // tk
    grid = (n_jo, m // tm, n // tn, n_jr, n_k)
    a_blk, a_idx = ((tk, tm), lambda i, j, kk: (kk, i)) if mode == 'tn' else ((tm, tk), lambda i, j, kk: (i, kk))
    b_blk, b_idx = ((tn, tk), lambda i, j, kk: (j, kk)) if mode == 'nt' else ((tk, tn), lambda i, j, kk: (kk, j))

    def spec(blk, idx, stacked, on_out):
        if not stacked:
            return pl.BlockSpec(blk, lambda jo, i, j, jr, kk: idx(i, j, kk))
        return pl.BlockSpec((None,) + blk, lambda jo, i, j, jr, kk: ((jo if on_out else jr),) + idx(i, j, kk))

    o_idx = lambda i, j, kk: (i, j)
    in_specs = [spec(a_blk, a_idx, sa, stack == 'a_out'), spec(b_blk, b_idx, sb, stack == 'b_out')]
    operands = [a, b]
    if res is not None:
        in_specs.append(spec((tm, tn), o_idx, False, False))
        operands.append(res)
    n_red = n_jr * n_k
    dims = _DIMS[mode]

    def body(*refs):
        a_ref, b_ref = refs[0], refs[1]
        res_ref = refs[2] if res is not None else None
        o_ref = refs[3] if res is not None else refs[2]
        part = lax.dot_general(a_ref[...].astype(BF16), b_ref[...].astype(BF16), dims,
                               preferred_element_type=F32)

        def finish(total):
            if res_ref is not None:
                total = total + res_ref[...]
            o_ref[...] = total.astype(o_ref.dtype)

        if n_red == 1:
            finish(part)
        else:
            acc_ref = refs[-1]
            step = pl.program_id(3) * n_k + pl.program_id(4)

            @pl.when(step == 0)
            def _():
                acc_ref[...] = part

            @pl.when(step > 0)
            def _():
                acc_ref[...] += part

            @pl.when(step == n_red - 1)
            def _():
                finish(acc_ref[...])

    out_shape = ((nj,) if so else ()) + (m, n)
    need = 2 * (_nbytes(a_blk, a.dtype) + _nbytes(b_blk, b.dtype) + _nbytes((tm, tn), out_dtype)
                + (_nbytes((tm, tn), F32) if res is not None else 0)) + 3 * _nbytes((tm, tn), F32) + (4 << 20)
    return pl.pallas_call(
        body, name=name, grid=grid, in_specs=in_specs, out_specs=spec((tm, tn), o_idx, so, True),
        out_shape=_out_hbm(out_shape, out_dtype),
        scratch_shapes=[pltpu.VMEM((tm, tn), F32)] if n_red > 1 else [],
        compiler_params=_params(5, need),
    )(*map(_in_hbm, operands))


def _rowwise(name, fn, row_ins, const_ins, row_outs, acc_outs, tr=TR):
    n_ri, n_ci, n_ro, n_ao = len(row_ins), len(const_ins), len(row_outs), len(acc_outs)

    def body(*refs):
        ri, ci = refs[:n_ri], refs[n_ri:n_ri + n_ci]
        ro, ao = refs[n_ri + n_ci:n_ri + n_ci + n_ro], refs[n_ri + n_ci + n_ro:]
        i = pl.program_id(0)
        tiles = [jnp.where(i == 0, pltpu.roll(r[...], lead, 0), r[...]) if lead else r[...]
                 for r, lead in zip(ri, leads)]
        outs = fn(i * tr, *tiles, *[r[...] for r in ci])
        for r, v in zip(ro, outs[:n_ro]):
            r[...] = v.astype(r.dtype)
        if n_ao:
            @pl.when(i == 0)
            def _():
                for r, v in zip(ao, outs[n_ro:]):
                    r[...] = v

            @pl.when(i > 0)
            def _():
                for r, v in zip(ao, outs[n_ro:]):
                    r[...] += v

    def row_spec(a, w, cb, lead=0):
        if not lead:
            return pl.BlockSpec((tr, w), lambda i: (i, cb))
        assert a.shape[0] + lead == T and lead % 8 == 0 and tr % 8 == 0
        return pl.BlockSpec((pl.Element(tr), pl.Element(w)),
                            lambda i: (jnp.maximum(i * (tr // 8) - lead // 8, 0) * 8, cb * w))

    in_specs = [row_spec(*r) for r in row_ins]
    leads = [r[3] if len(r) > 3 else 0 for r in row_ins]
    row_ins = [r[:3] for r in row_ins]
    in_specs += [pl.BlockSpec(c.shape, functools.partial(lambda nd, i: (0,) * nd, c.ndim)) for c in const_ins]
    out_specs = [pl.BlockSpec((tr, w), lambda i: (i, 0)) for w, _ in row_outs]
    out_specs += [pl.BlockSpec((1, w), lambda i: (0, 0)) for w in acc_outs]
    out_shape = [_out_hbm((T, w), dt) for w, dt in row_outs]
    out_shape += [jax.ShapeDtypeStruct((1, w), F32) for w in acc_outs]
    need = 2 * (sum(_nbytes((tr, w), a.dtype) for a, w, _ in row_ins) + sum(_nbytes((tr, w), dt) for w, dt in row_outs))
    need = 3 * need + (8 << 20)
    return pl.pallas_call(
        body, name=name, grid=(T // tr,), in_specs=in_specs, out_specs=out_specs, out_shape=out_shape,
        compiler_params=_params(1, need),
    )(*[_in_hbm(a) for a, _, _ in row_ins], *const_ins)


def _rms_fwd_fn(row0, h, g):
    h = h.astype(F32)
    r = lax.rsqrt(jnp.mean(h * h, axis=-1, keepdims=True) + EPS)
    return (h * r) * g, r


def _rms_mix_fn(row0, x, meta, g):
    h = jnp.where(row0 == 0, jnp.concatenate([meta, x[N_META:]], axis=0), x)
    return (h,) + _rms_fwd_fn(row0, h, g)


def _rms_bwd(dxn, h, r, g):
    xhat = h * r
    dxg = dxn * g
    dh = r * (dxg - xhat * jnp.mean(dxg * xhat, axis=-1, keepdims=True))
    return dh, jnp.sum(dxn * xhat, axis=0, keepdims=True)


def _rms_bwd_fn(row0, dxn, h, r, g):
    return _rms_bwd(dxn.astype(F32), h.astype(F32), r, g)


def _rms_bwd_res_fn(row0, dxn, h, r, res, g):
    dh, dg = _rms_bwd(dxn.astype(F32), h.astype(F32), r, g)
    return dh + res, dg


def _gelu_fn(row0, y):
    return (jax.nn.gelu(y),)


def _glu_fwd_fn(row0, y, gl, b, gain):
    ya = jax.nn.gelu(y) * jax.nn.sigmoid(gl + b)
    return _rms_fwd_fn(row0, ya, gain)


def _glu_bwd_fn(row0, dyan, y, gl, ra, b, gain):
    g = jax.nn.gelu(y)
    s = jax.nn.sigmoid(gl + b)
    dya, dgain = _rms_bwd(dyan, g * s, ra, gain)
    dgl = dya * g * s * (1.0 - s)
    return dgl, dya * s, dgain, jnp.sum(dgl, axis=0, keepdims=True)


def _gelu_bwd_fn(row0, dg, y):
    return (jax.vjp(jax.nn.gelu, y)[1](dg)[0],)


def _loss_fn(row0, h, tgt, gain):
    r = lax.rsqrt(jnp.mean(h * h, axis=-1, keepdims=True) + EPS)
    xhat = h * r
    rows = row0 + lax.broadcasted_iota(jnp.int32, h.shape, 0)
    err = jnp.where(rows >= N_META, xhat * gain - tgt, 0.0)
    loss = jnp.full((1, 128), jnp.sum(err * err) * (0.5 / D), F32)
    dh, dgain = _rms_bwd(err * (1.0 / D), h, r, gain)
    return dh, dgain, loss


def _rope_tables():
    pos = jnp.arange(T, dtype=jnp.int32)
    inv_freq = 1.0 / (ROPE_BASE ** (jnp.arange(0, QK_ROPE, 2, dtype=F32) / QK_ROPE))
    ang = pos.astype(F32)[:, None] * inv_freq[None, :]
    cos, sin, z32, z64 = jnp.cos(ang), jnp.sin(ang), jnp.zeros((T, 32), F32), jnp.zeros((T, 64), F32)
    return (jnp.concatenate([cos, cos, z64], 1), jnp.concatenate([-sin, z32, z64], 1),
            jnp.concatenate([z32, sin, z64], 1))


def _rot(x, cc, s1, s2):
    return x * cc + pltpu.roll(x, 96, 1) * s1 + pltpu.roll(x, 32, 1) * s2


def _derot(d, cc, s1, s2):
    return d * cc + pltpu.roll(d * s1, 32, 1) + pltpu.roll(d * s2, 96, 1)


def _chunk_of(pos):
    return jnp.where(pos < N_META, 0, 1 + ((pos - N_META) >> 6))


def _key_len(t):
    last_chunk = 1 + ((t + 1) * TR - 1 - N_META) // CHUNK
    return min(T, pl.cdiv(N_META + last_chunk * CHUNK, 128) * 128)


def _attn_probs(t, q_ref, kv_ref, kpe_ref, tq, tk):
    kl = _key_len(t)
    q = q_ref[...]
    qn = q[:, :QK_NOPE].astype(BF16)
    qp = _rot(q[:, QK_NOPE:], tq[0][...], tq[1][...], tq[2][...]).astype(BF16)
    kn = kv_ref[:kl, :QK_NOPE].astype(BF16)
    v = kv_ref[:kl, QK_NOPE:].astype(BF16)
    kp = _rot(kpe_ref[:kl, :], tk[0][:kl, :], tk[1][:kl, :], tk[2][:kl, :]).astype(BF16)
    s = lax.dot_general(qn, kn, _DIMS['nt'], preferred_element_type=F32)
    s = s + lax.dot_general(qp, kp, _DIMS['nt'], preferred_element_type=F32)
    qc = _chunk_of(t * TR + lax.broadcasted_iota(jnp.int32, (TR, 1), 0))
    kc = _chunk_of(lax.broadcasted_iota(jnp.int32, (1, kl), 1))
    s = jnp.where(kc <= qc, s * SM_SCALE, jnp.finfo(F32).min)
    e = jnp.exp(s - jnp.max(s, axis=-1, keepdims=True))
    return qn, qp, kn, kp, v, e, 1.0 / jnp.sum(e, axis=-1, keepdims=True)


def _per_q_tile(fn):
    for t in range(T // TR):
        pl.when(pl.program_id(1) == t)(functools.partial(fn, t))


def _attn_specs(z, tabs):
    q_spec = pl.BlockSpec((None, TR, HEAD_PAD), lambda h, i: (h, i, 0))
    kv_spec = pl.BlockSpec((None, T, HEAD_PAD), lambda h, i: (h, 0, 0))
    kpe_spec = pl.BlockSpec((T, 128), lambda h, i: (0, (D_IN - QK_ROPE) // 128))
    tq = [pl.BlockSpec((TR, 128), lambda h, i: (i, 0))] * 3
    tk = [pl.BlockSpec((T, 128), lambda h, i: (0, 0))] * 3
    return [q_spec, kv_spec, kpe_spec] + tq + tk


def _attn_fwd(q, kv, z, tabs):
    def body(q_ref, kv_ref, kpe_ref, c1, c2, c3, k1, k2, k3, o_ref):
        def tile(t):
            _, _, _, _, v, e, inv = _attn_probs(t, q_ref, kv_ref, kpe_ref, (c1, c2, c3), (k1, k2, k3))
            o_ref[...] = jnp.dot(e.astype(BF16), v, preferred_element_type=F32) * inv

        _per_q_tile(tile)

    return pl.pallas_call(
        body, name="attn_fwd", grid=(N_HEADS, T // TR), in_specs=_attn_specs(z, tabs),
        out_specs=pl.BlockSpec((TR, V_HEAD), lambda h, i: (i, h)),
        out_shape=_out_hbm((T, N_HEADS * V_HEAD), F32),
        compiler_params=_params(2, 48 << 20),
    )(_in_hbm(q), _in_hbm(kv), _in_hbm(z), *tabs, *tabs)


def _attn_bwd(q, kv, z, tabs, dyb):
    def body(q_ref, kv_ref, kpe_ref, c1, c2, c3, k1, k2, k3, do_ref, dq_ref, dkv_ref, dkpe_ref):
        @pl.when(pl.program_id(1) == 0)
        def _():
            dkv_ref[...] = jnp.zeros(dkv_ref.shape, F32)
            dkpe_ref[...] = jnp.zeros(dkpe_ref.shape, F32)

        def tile(t):
            kl = _key_len(t)
            qn, qp, kn, kp, v, e, inv = _attn_probs(t, q_ref, kv_ref, kpe_ref, (c1, c2, c3), (k1, k2, k3))
            do = do_ref[...]
            dp = lax.dot_general(do.astype(BF16), v, _DIMS['nt'], preferred_element_type=F32)
            delta = jnp.sum(e * dp, axis=-1, keepdims=True) * inv
            ds = (e * ((dp - delta) * (inv * SM_SCALE))).astype(BF16)
            pb, do = e.astype(BF16), (do * inv).astype(BF16)
            dq_ref[:, :QK_NOPE] = jnp.dot(ds, kn, preferred_element_type=F32).astype(dq_ref.dtype)
            dqp = _derot(jnp.dot(ds, kp, preferred_element_type=F32), c1[...], c2[...], c3[...])
            dq_ref[:, QK_NOPE:] = dqp.astype(dq_ref.dtype)
            dkv_ref[:kl, :QK_NOPE] += lax.dot_general(ds, qn, _DIMS['tn'], preferred_element_type=F32)
            dkv_ref[:kl, QK_NOPE:] += lax.dot_general(pb, do, _DIMS['tn'], preferred_element_type=F32)
            dkpe_ref[:kl, :] += lax.dot_general(ds, qp, _DIMS['tn'], preferred_element_type=F32)

        _per_q_tile(tile)

    return pl.pallas_call(
        body, name="attn_bwd", grid=(N_HEADS, T // TR),
        in_specs=_attn_specs(z, tabs) + [pl.BlockSpec((TR, V_HEAD), lambda h, i: (i, h))],
        out_specs=[pl.BlockSpec((None, TR, HEAD_PAD), lambda h, i: (h, i, 0)),
                   pl.BlockSpec((None, T, HEAD_PAD), lambda h, i: (h, 0, 0)),
                   pl.BlockSpec((None, T, 128), lambda h, i: (h, 0, 0))],
        out_shape=[_out_hbm((N_HEADS, T, HEAD_PAD), BF16), _out_hbm((N_HEADS, T, HEAD_PAD), F32),
                   _out_hbm((N_HEADS, T, 128), F32)],
        compiler_params=_params(2, 56 << 20),
    )(_in_hbm(q), _in_hbm(kv), _in_hbm(z), *tabs, *tabs, _in_hbm(dyb))


def _kpe_bwd(dkpe, tabs):
    def body(d_ref, c1, c2, c3, o_ref):
        d = d_ref[0]
        for h in range(1, N_HEADS):
            d = d + d_ref[h]
        o_ref[...] = _derot(d, c1[...], c2[...], c3[...]).astype(o_ref.dtype)

    tile = pl.BlockSpec((TR, 128), lambda i: (i, 0))
    return pl.pallas_call(
        body, name="kpe_bwd", grid=(T // TR,),
        in_specs=[pl.BlockSpec((N_HEADS, TR, 128), lambda i: (0, i, 0)), tile, tile, tile], out_specs=tile,
        out_shape=jax.ShapeDtypeStruct((T, 128), BF16), compiler_params=_params(1, 16 << 20),
    )(_in_hbm(dkpe), *tabs)


def _s5_prep_math(lam_re, lam_im, log_dt, bt_re, bt_im):
    dt = jnp.exp(log_dt)
    mag = jnp.exp(lam_re * dt)
    a_re, a_im = mag * jnp.cos(lam_im * dt), mag * jnp.sin(lam_im * dt)
    den = lam_re * lam_re + lam_im * lam_im
    w_re = ((a_re - 1.0) * lam_re + a_im * lam_im) / den
    w_im = (a_im * lam_re - (a_re - 1.0) * lam_im) / den
    return a_re, a_im, w_re[None] * bt_re - w_im[None] * bt_im, w_re[None] * bt_im + w_im[None] * bt_re


def _eye_groups():
    return (lax.broadcasted_iota(jnp.int32, (N_GROUPS, N_GROUPS), 0)
            == lax.broadcasted_iota(jnp.int32, (N_GROUPS, N_GROUPS), 1)).astype(F32)


def _row_to_col(row):
    return jnp.sum(_eye_groups() * row, axis=1, keepdims=True)


def _s5_prep(lam_re, lam_im, log_dt, bt_re, bt_im):
    def body(lr, li, ld, br, bi, ar, ai, bbr, bbi):
        ar[...], ai[...], bbr[...], bbi[...] = _s5_prep_math(lr[...], li[...], _row_to_col(ld[...]), br[...], bi[...])

    gp, cgp = jax.ShapeDtypeStruct((N_GROUPS, SSM_STATE), F32), jax.ShapeDtypeStruct(bt_re.shape, F32)
    return pl.pallas_call(body, name="s5_prep", out_shape=[gp, gp, cgp, cgp])(lam_re, lam_im, log_dt, bt_re, bt_im)


def _s5_prep_bwd(lam_re, lam_im, log_dt, bt_re, bt_im, da_re, da_im, dbb_re, dbb_im):
    def body(lr, li, ld, br, bi, dar, dai, dbr, dbi, o1, o2, o3, o4, o5):
        _, vjp = jax.vjp(_s5_prep_math, lr[...], li[...], _row_to_col(ld[...]), br[...], bi[...])
        o1[...], o2[...], dld, o4[...], o5[...] = vjp((dar[...], dai[...], dbr[...], dbi[...]))
        o3[...] = jnp.sum(_eye_groups() * dld, axis=0, keepdims=True)

    ins = (lam_re, lam_im, log_dt, bt_re, bt_im)
    return pl.pallas_call(body, name="s5_prep_bwd", out_shape=[jax.ShapeDtypeStruct(a.shape, F32) for a in ins])(
        *ins, da_re, da_im, dbb_re, dbb_im)


def _cmul(ar, ai, br, bi):
    return ar * br - ai * bi, ar * bi + ai * br


def _seg_rows(i):
    return pl.ds(pl.multiple_of(i * N_SEG, N_SEG), N_SEG)


def _scan(xr, xi, ar, ai, reverse):
    zero = jnp.zeros(ar.shape, F32)

    def local(j, carry):
        rows = _seg_rows(SEG - 1 - j if reverse else j)
        nr, ni = _cmul(ar, ai, *carry)
        nr, ni = nr + xr[rows, :], ni + xi[rows, :]
        xr[rows, :], xi[rows, :] = nr, ni
        return nr, ni

    er, ei = lax.fori_loop(0, SEG, local, (zero, zero))
    power, square, bits = None, (ar, ai), SEG
    while bits:
        if bits & 1:
            power = square if power is None else _cmul(*power, *square)
        square, bits = _cmul(*square, *square), bits >> 1
    pr, pi = power
    row = lax.broadcasted_iota(jnp.int32, ar.shape, 0)
    edge, shift = (N_SEG - 1, N_SEG - 1) if reverse else (0, 1)
    hr, hi = zero, zero
    for _ in range(N_SEG - 1):
        tr_, ti_ = _cmul(pr, pi, hr, hi)
        hr = jnp.where(row == edge, 0.0, pltpu.roll(tr_ + er, shift, 0))
        hi = jnp.where(row == edge, 0.0, pltpu.roll(ti_ + ei, shift, 0))

    def fix(j, carry):
        rows = _seg_rows(SEG - 1 - j if reverse else j)
        xr[rows, :] += carry[0]
        xi[rows, :] += carry[1]
        return _cmul(ar, ai, *carry)

    lax.fori_loop(0, SEG, fix, _cmul(ar, ai, hr, hi))
    return hr, hi


def _split(x):
    hi = x.astype(BF16)
    return hi, (x - hi.astype(F32)).astype(BF16)


def _dot2(xs, ys, mode='nn'):
    d = lambda p, q: lax.dot_general(p, q, _DIMS[mode], preferred_element_type=F32)
    return d(xs[0], ys[0]) + d(xs[1], ys[0])


def _group_masks():
    ns = GQ * SSM_STATE
    rep = (lax.broadcasted_iota(jnp.int32, (SSM_STATE, ns), 1) % SSM_STATE
           == lax.broadcasted_iota(jnp.int32, (SSM_STATE, ns), 0)).astype(BF16)
    own = (lax.broadcasted_iota(jnp.int32, (128, ns), 0) // SSM_GROUP
           == lax.broadcasted_iota(jnp.int32, (128, ns), 1) // SSM_STATE)
    return rep, own


def _block_diag(x, rep, own):
    dot = lambda p: jnp.dot(p, rep, preferred_element_type=F32)
    return tuple(jnp.where(own, dot(part), 0.0).astype(BF16) for part in _split(x))


def _block_rows(full, rep, own):
    masked = _split(jnp.where(own, full, 0.0))
    dot = lambda p: lax.dot_general(p, rep, _DIMS['nt'], preferred_element_type=F32)
    return dot(masked[0]) + dot(masked[1])


def _s5_specs():
    col = pl.BlockSpec((TP, 128), lambda q: (0, q))
    a_spec = pl.BlockSpec((None, 2, N_SEG, GQ * SSM_STATE), lambda q: (q, 0, 0, 0))
    rows = pl.BlockSpec((128, SSM_STATE), lambda q: (q, 0))
    d_spec = pl.BlockSpec((1, 128), lambda q: (0, q))
    return col, a_spec, rows, d_spec


def _s5_fwd(u, a8, bb_re, bb_im, c_re, c_im, dsk):
    def body(u_ref, a_ref, bre_ref, bim_ref, cre_ref, cim_ref, d_ref, y_ref, hre, him):
        u_ = u_ref[...]
        us = _split(u_)
        rep, own = _group_masks()
        hre[...] = _dot2(us, _block_diag(bre_ref[...], rep, own))
        him[...] = _dot2(us, _block_diag(bim_ref[...], rep, own))
        _scan(hre, him, a_ref[0], a_ref[1], reverse=False)
        dot = lambda h, c: lax.dot_general(h[...].astype(BF16), _block_diag(c[...], rep, own)[0], _DIMS['nt'],
                                           preferred_element_type=F32)
        y_ref[...] = dot(hre, cre_ref) - dot(him, cim_ref) + d_ref[...] * u_

    col, a_spec, rows, d_spec = _s5_specs()
    return pl.pallas_call(
        body, name="s5_fwd", grid=(N_GROUPS // GQ,), in_specs=[col, a_spec, rows, rows, rows, rows, d_spec],
        out_specs=col, out_shape=_out_hbm((TP, D_SSM), F32),
        scratch_shapes=[pltpu.VMEM((TP, GQ * SSM_STATE), F32)] * 2,
        compiler_params=_params(1, 40 << 20),
    )(_in_hbm(u), a8, bb_re, bb_im, c_re, c_im, dsk)


def _s5_bwd(u, dy, a8, bb_re, bb_im, c_re, c_im, dsk):
    def body(u_ref, dy_ref, a_ref, bre_ref, bim_ref, cre_ref, cim_ref, d_ref,
             du_ref, dbre_ref, dbim_ref, dcre_ref, dcim_ref, da_ref, dd_ref, hre, him, lre, lim):
        u_, dy_ = u_ref[...], dy_ref[...]
        us, dys = _split(u_), _split(dy_)
        ar, ai = a_ref[0], a_ref[1]
        rep, own = _group_masks()
        bres, bims = _block_diag(bre_ref[...], rep, own), _block_diag(bim_ref[...], rep, own)
        hre[...] = _dot2(us, bres)
        him[...] = _dot2(us, bims)
        h0r, h0i = _scan(hre, him, ar, ai, reverse=False)
        lre[...] = _dot2(dys, _block_diag(cre_ref[...], rep, own))
        lim[...] = -_dot2(dys, _block_diag(cim_ref[...], rep, own))
        _scan(lre, lim, ar, -ai, reverse=True)

        def acc_da(i, carry):
            lr, li = lre[_seg_rows(i), :], lim[_seg_rows(i), :]
            pr, pi = hre[_seg_rows(i - 1), :], him[_seg_rows(i - 1), :]
            return carry[0] + lr * pr + li * pi, carry[1] + li * pr - lr * pi

        lr, li = lre[_seg_rows(0), :], lim[_seg_rows(0), :]
        dar, dai = lax.fori_loop(1, SEG, acc_da, (lr * h0r + li * h0i, li * h0r - lr * h0i))
        da_ref[0:1, :] = jnp.sum(dar, axis=0, keepdims=True)
        da_ref[1:2, :] = jnp.sum(dai, axis=0, keepdims=True)
        dot = lambda p, q, mode: lax.dot_general(p, q, _DIMS[mode], preferred_element_type=F32)
        lr1, li1 = lre[...].astype(BF16), lim[...].astype(BF16)
        du_ref[...] = dy_ * d_ref[...] + dot(lr1, bres[0], 'nt') + dot(li1, bims[0], 'nt')
        dbre_ref[...] = _block_rows(dot(us[0], lr1, 'tn'), rep, own)
        dbim_ref[...] = _block_rows(dot(us[0], li1, 'tn'), rep, own)
        dcre_ref[...] = _block_rows(dot(dys[0], hre[...].astype(BF16), 'tn'), rep, own)
        dcim_ref[...] = -_block_rows(dot(dys[0], him[...].astype(BF16), 'tn'), rep, own)
        dd_ref[...] = jnp.sum(dy_ * u_, axis=0, keepdims=True)

    col, a_spec, rows, d_spec = _s5_specs()
    nq, ns = N_GROUPS // GQ, GQ * SSM_STATE
    per_row = jax.ShapeDtypeStruct((N_GROUPS * SSM_GROUP, SSM_STATE), F32)
    return pl.pallas_call(
        body, name="s5_bwd", grid=(nq,),
        in_specs=[col, col, a_spec, rows, rows, rows, rows, d_spec],
        out_specs=[col, rows, rows, rows, rows, pl.BlockSpec((None, 2, ns), lambda q: (q, 0, 0)), d_spec],
        out_shape=[_out_hbm((TP, D_SSM), F32), per_row, per_row, per_row, per_row,
                   jax.ShapeDtypeStruct((nq, 2, ns), F32), jax.ShapeDtypeStruct((1, D_SSM), F32)],
        scratch_shapes=[pltpu.VMEM((TP, ns), F32)] * 4,
        compiler_params=_params(1, 56 << 20),
    )(_in_hbm(u), _in_hbm(dy), a8, bb_re, bb_im, c_re, c_im, dsk)


def _to_segments(a, cols=None):
    a = jnp.pad(a[:, :cols], ((0, TP - T), (0, 0)))
    return a.reshape(N_SEG, SEG, a.shape[-1]).transpose(1, 0, 2).reshape(TP, a.shape[-1])


def _from_segments(a):
    return a.reshape(SEG, N_SEG, a.shape[-1]).transpose(1, 0, 2).reshape(TP, a.shape[-1])[:T]


def _ff_specs():
    up_spec = pl.BlockSpec((2, None, T, 128), lambda p, cb: (0, p, 0, cb))
    piece = pl.BlockSpec((None, T, 128), lambda p, cb: (p, 0, cb))
    cw_spec = pl.BlockSpec((None, 3, 128), lambda p, cb: (p, 0, cb))
    cb_spec = pl.BlockSpec((None, 1, 128), lambda p, cb: (p, 0, cb))
    return up_spec, piece, cw_spec, cb_spec


def _conv_gate(gate, w, cb, pad_ref):
    pad_ref[0:8, :] = jnp.zeros((8, 128), F32)
    pad_ref[8:8 + T, :] = gate
    g1, g2 = pad_ref[pl.ds(7, T), :], pad_ref[pl.ds(6, T), :]
    return w[0:1, :] * g2 + w[1:2, :] * g1 + w[2:3, :] * gate + cb, g1, g2


def _ff_act(up, cw4, cb4):
    def body(up_ref, w_ref, b_ref, o_ref, pad_ref):
        gc, _, _ = _conv_gate(up_ref[0].astype(F32), w_ref[...], b_ref[...], pad_ref)
        o_ref[...] = (jax.nn.silu(gc) * up_ref[1].astype(F32)).astype(o_ref.dtype)

    up_spec, piece, cw_spec, cb_spec = _ff_specs()
    n_cb = pl.cdiv(FF_PIECE, 128)
    return pl.pallas_call(
        body, name="ff_act", grid=(4, n_cb), in_specs=[up_spec, cw_spec, cb_spec], out_specs=piece,
        out_shape=_out_hbm((4, T, FF_PIECE), BF16),
        scratch_shapes=[pltpu.VMEM((T + 8, 128), F32)],
        compiler_params=_params(2, 24 << 20),
    )(_in_hbm(up.reshape(2, 4, T, FF_PIECE)), cw4, cb4)


def _ff_act_bwd(up, dact, cw4, cb4):
    def body(up_ref, da_ref, w_ref, b_ref, dup_ref, dw_ref, db_ref, pad_ref, pad2_ref):
        gate, val, w = up_ref[0].astype(F32), up_ref[1].astype(F32), w_ref[...]
        gc, g1, g2 = _conv_gate(gate, w, b_ref[...], pad_ref)
        sg = jax.nn.sigmoid(gc)
        da = da_ref[...].astype(F32)
        dup_ref[1] = (da * gc * sg).astype(dup_ref.dtype)
        dgc = da * val * sg * (1.0 + gc * (1.0 - sg))
        db_ref[...] = jnp.sum(dgc, axis=0, keepdims=True)
        dw_ref[0:1, :] = jnp.sum(dgc * g2, axis=0, keepdims=True)
        dw_ref[1:2, :] = jnp.sum(dgc * g1, axis=0, keepdims=True)
        dw_ref[2:3, :] = jnp.sum(dgc * gate, axis=0, keepdims=True)
        pad2_ref[0:T, :] = dgc
        pad2_ref[T:T + 8, :] = jnp.zeros((8, 128), F32)
        d1, d2 = pad2_ref[pl.ds(1, T), :], pad2_ref[pl.ds(2, T), :]
        dup_ref[0] = (w[2:3, :] * dgc + w[1:2, :] * d1 + w[0:1, :] * d2).astype(dup_ref.dtype)

    up_spec, piece, cw_spec, cb_spec = _ff_specs()
    n_cb = pl.cdiv(FF_PIECE, 128)
    dup, dw, db = pl.pallas_call(
        body, name="ff_act_bwd", grid=(4, n_cb), in_specs=[up_spec, piece, cw_spec, cb_spec],
        out_specs=[up_spec, cw_spec, cb_spec],
        out_shape=[_out_hbm((2, 4, T, FF_PIECE), BF16), jax.ShapeDtypeStruct((4, 3, FF_PIECE), F32),
                   jax.ShapeDtypeStruct((4, 1, FF_PIECE), F32)],
        scratch_shapes=[pltpu.VMEM((T + 8, 128), F32)] * 2,
        compiler_params=_params(2, 32 << 20),
    )(_in_hbm(up.reshape(2, 4, T, FF_PIECE)), _in_hbm(dact), cw4, cb4)
    return dup.reshape(8, T, FF_PIECE), dw, db


def _local_step(x, tgt, meta, p, wf, mid_weights, late_weights, send_grads, send_small):
    sm = {}
    tabs = _rope_tables()
    row = lambda a: a.reshape(1, -1)

    h0, xn, r0 = _rowwise("rms_mix", _rms_mix_fn, [(x, D, 0, N_META)], [meta, row(p['mix_norm'])],
                          [(D, F32), (D, BF16), (1, F32)], [])
    z = _mm("in_proj", xn, wf['w_in'], 'nn', TR, 640, D, F32)
    lam_re, lam_im = p['lam_re'].reshape(N_GROUPS, SSM_STATE), p['lam_im'].reshape(N_GROUPS, SSM_STATE)
    log_dt = p['log_dt'].reshape(1, N_GROUPS)
    bt_re = p['b_re'].reshape(N_GROUPS, SSM_STATE, SSM_GROUP).transpose(2, 0, 1)
    bt_im = p['b_im'].reshape(N_GROUPS, SSM_STATE, SSM_GROUP).transpose(2, 0, 1)
    c_re = p['c_re'].reshape(N_GROUPS * SSM_GROUP, SSM_STATE)
    c_im = p['c_im'].reshape(N_GROUPS * SSM_GROUP, SSM_STATE)
    a_re, a_im, bbt_re, bbt_im = _s5_prep(lam_re, lam_im, log_dt, bt_re, bt_im)
    nq, ns = N_GROUPS // GQ, GQ * SSM_STATE
    a8 = jnp.broadcast_to(jnp.stack([a_re.reshape(nq, ns), a_im.reshape(nq, ns)], 1)[:, :, None, :],
                          (nq, 2, N_SEG, ns))
    gcp = lambda t: t.transpose(1, 0, 2).reshape(N_GROUPS * SSM_GROUP, SSM_STATE)
    bb_re, bb_im = gcp(bbt_re), gcp(bbt_im)
    dsk = row(p['d_skip'])
    u_seg = _to_segments(z, D_SSM)
    y_ssm = _from_segments(_s5_fwd(u_seg, a8, bb_re, bb_im, c_re, c_im, dsk))
    wf = dict(wf, **mid_weights(y_ssm))
    g, = _rowwise("gelu", _gelu_fn, [(y_ssm, D_SSM, 0)], [], [(D_SSM, BF16)], [])
    gl = _mm("glu_proj", g, wf['w_glu'], 'nn', TR, D_SSM, D_SSM, F32)
    yan, ra = _rowwise("glu_norm", _glu_fwd_fn, [(y_ssm, D_SSM, 0), (gl, D_SSM, 0)],
                       [row(p['b_glu']), row(p['out_norm_ssm'])], [(D_SSM, BF16), (1, F32)], [])
    qn, rq, kvn, rkv = _rowwise(
        "rms_q_kv", lambda row0, zq, zkv, gq, gkv: _rms_fwd_fn(row0, zq, gq) + _rms_fwd_fn(row0, zkv, gkv),
        [(z, Q_LORA, D_SSM // Q_LORA), (z, KV_LORA, (D_SSM + Q_LORA) // KV_LORA)],
        [row(p['q_a_norm']), row(p['kv_a_norm'])], [(Q_LORA, BF16), (1, F32), (KV_LORA, BF16), (1, F32)], [])
    q = _mm("q_proj", qn, wf['w_q_b'], 'nn', T, HEAD_PAD, Q_LORA, F32, stack='b_out')
    kv = _mm("kv_proj", kvn, wf['w_kv_b'], 'nn', T, HEAD_PAD, KV_LORA, F32, stack='b_out')
    yb = _attn_fwd(q, kv, z, tabs)
    ybn, rb = _rowwise("rms_attn", _rms_fwd_fn, [(yb, D_SSM, 0)], [row(p['out_norm_attn'])],
                       [(D_SSM, BF16), (1, F32)], [])
    y = jnp.concatenate([yan, ybn], axis=1)
    wf = dict(wf)
    wf['w_out'], token, up_weights = late_weights(y)
    h1 = _mm("out_proj", y, wf['w_out'], 'nn', TR, 1024, D, F32, res=h0)
    xn2, r1 = _rowwise("rms_ffn", _rms_fwd_fn, [(h1, D, 0)], [row(p['ffn_norm']) + token[0:1, 0:1]],
                       [(D, BF16), (1, F32)], [])
    wf['w_up'], token, down_weights = up_weights(xn2)
    up = _mm("up_proj", xn2, wf['w_up'], 'nn', TR, FF_PIECE, D, BF16, stack='b_out')
    cw4 = wf['conv_w'].reshape(3, 4, FF_PIECE).transpose(1, 0, 2)
    cb4 = p['conv_b'].reshape(4, 1, FF_PIECE)
    act = _ff_act(up, cw4, cb4 + token[0:1, 0:1])
    wf['w_down'] = down_weights(act)
    wd4 = wf['w_down'].reshape(4, FF_PIECE, D)
    h2 = _mm("down_proj", act, wd4, 'nn', TR, 1024, FF_PIECE, F32, stack='ab_red', res=h1)
    dh2, sm['final_norm'], loss = _rowwise("loss", _loss_fn, [(h2, D, 0), (tgt, D, 0, N_META)], [row(p['final_norm'])],
                                           [(D, F32)], [D, 128])

    big = {}
    dact = _mm("down_bwd_x", dh2, wd4, 'nt', TR, FF_PIECE, D, BF16, stack='b_out')
    g_down = _mm("down_bwd_w", act, dh2, 'tn', FF_PIECE, 512, T, BF16, stack='a_out').reshape(
        N_DEV, D_FF // N_DEV, D)
    token = send_grads(['w_down'], [g_down], 'cores')
    dup, dcw4, sm['conv_b'] = _ff_act_bwd(up, dact, cw4, cb4 + token[0:1, 0:1])
    dxn2 = _mm("up_bwd_x", dup, wf['w_up'], 'nt', TR, D, FF_PIECE, F32, stack='ab_red')
    g_up = _mm("up_bwd_w", dup, xn2, 'tn', FF_PIECE, 1024, T, BF16, stack='a_out')
    token = send_grads(['w_up'], [g_up], 'cores') + send_grads(['w_down'], dxn2, 'chips')
    dh1, sm['ffn_norm'] = _rowwise("rms_ffn_bwd", _rms_bwd_res_fn,
                                   [(dxn2, D, 0), (h1, D, 0), (r1, 1, 0), (dh2, D, 0)],
                                   [row(p['ffn_norm']) + token[0:1, 0:1]], [(D, F32)], [D])
    dy = _mm("out_bwd_x", dh1, wf['w_out'], 'nt', TR, 1024, D, F32)
    g_out = _mm("out_bwd_w", y, dh1, 'tn', 1024, 1024, T, BF16).reshape(N_DEV, D // N_DEV, D)
    dgl, dgd, sm['out_norm_ssm'], sm['b_glu'] = _rowwise(
        "glu_bwd", _glu_bwd_fn, [(dy, D_SSM, 0), (y_ssm, D_SSM, 0), (gl, D_SSM, 0), (ra, 1, 0)],
        [row(p['b_glu']), row(p['out_norm_ssm'])], [(D_SSM, BF16), (D_SSM, F32)], [D_SSM, D_SSM])
    dg = _mm("glu_bwd_x", dgl, wf['w_glu'], 'nt', TR, D_SSM, D_SSM, F32, res=dgd)
    g_glu = _mm("glu_bwd_w", g, dgl, 'tn', D_SSM, D_SSM, T, BF16).reshape(N_DEV, D_SSM // N_DEV, D_SSM)
    dy_ssm, = _rowwise("gelu_bwd", _gelu_bwd_fn, [(dg, D_SSM, 0), (y_ssm, D_SSM, 0)], [], [(D_SSM, F32)], [])
    token = send_grads(['w_up'], g_glu, 'chips')
    dyb, sm['out_norm_attn'] = _rowwise("rms_attn_bwd", _rms_bwd_fn, [(dy, D_SSM, 1), (yb, D_SSM, 0), (rb, 1, 0)],
                                        [row(p['out_norm_attn']) + token[0:1, 0:1]], [(D_SSM, F32)], [D_SSM])
    dq, dkv, dkpe = _attn_bwd(q, kv, z, tabs, dyb)
    g_q = _mm("q_bwd_w", qn, dq, 'tn', Q_LORA, HEAD_PAD, T, BF16, stack='b_out')
    dqn = _mm("q_bwd_x", dq, wf['w_q_b'], 'nt', T, Q_LORA, HEAD_PAD, F32, stack='ab_red')
    g_kv = _mm("kv_bwd_w", kvn, dkv, 'tn', KV_LORA, HEAD_PAD, T, BF16, stack='b_out')
    dkvn = _mm("kv_bwd_x", dkv, wf['w_kv_b'], 'nt', T, KV_LORA, HEAD_PAD, F32, stack='ab_red')
    token = send_grads(['w_out', 'w_glu', 'w_q_b', 'w_kv_b'], [g_out, g_glu, g_q, g_kv], 'both')
    def both_bwd(row0, dq_, zq, rq_, dkv_, zkv, rkv_, gq, gkv):
        (dzq, dgq), (dzkv, dgkv) = _rms_bwd(dq_, zq, rq_, gq), _rms_bwd(dkv_, zkv, rkv_, gkv)
        return dzq, dzkv, dgq, dgkv

    dq_a, dkv_a, sm['q_a_norm'], sm['kv_a_norm'] = _rowwise(
        "rms_q_kv_bwd", both_bwd,
        [(dqn, Q_LORA, 0), (z, Q_LORA, D_SSM // Q_LORA), (rq, 1, 0),
         (dkvn, KV_LORA, 0), (z, KV_LORA, (D_SSM + Q_LORA) // KV_LORA), (rkv, 1, 0)],
        [row(p['q_a_norm']), row(p['kv_a_norm'])], [(Q_LORA, BF16), (KV_LORA, BF16)], [Q_LORA, KV_LORA])
    dk_pe = _kpe_bwd(dkpe, tabs)
    du_seg, dbre, dbim, dcre, dcim, da, sm['d_skip'] = _s5_bwd(
        u_seg, _to_segments(dy_ssm), a8, bb_re, bb_im, c_re, c_im, dsk + token[0:1, 0:1])
    du = _from_segments(du_seg).astype(BF16)
    cgp = lambda d: d.reshape(N_GROUPS, SSM_GROUP, SSM_STATE).transpose(1, 0, 2)
    dlam_re, dlam_im, dlog_dt, dbt_re, dbt_im = _s5_prep_bwd(
        lam_re, lam_im, log_dt, bt_re, bt_im, da[:, 0, :].reshape(N_GROUPS, SSM_STATE),
        da[:, 1, :].reshape(N_GROUPS, SSM_STATE), cgp(dbre), cgp(dbim))
    sm['lam_re'], sm['lam_im'], sm['log_dt'] = dlam_re, dlam_im, dlog_dt
    sm['b_re'], sm['b_im'] = gcp(dbt_re), gcp(dbt_im)
    sm['c_re'], sm['c_im'] = dcre, dcim
    token = send_small(sm, dcw4, loss)
    dz = jnp.concatenate([du, dq_a, dkv_a, dk_pe], axis=1)
    dxn = _mm("in_bwd_x", dz, wf['w_in'], 'nt', TR, 1024, D_IN_PAD, F32)
    big['w_in'] = _mm("in_bwd_w", xn, dz, 'tn', 1024, 640, T, BF16).reshape(N_DEV, D // N_DEV, D_IN_PAD)
    dh0, d_mix_norm = _rowwise("rms_mix_bwd", _rms_bwd_res_fn,
                               [(dxn, D, 0), (h0, D, 0), (r0, 1, 0), (dh1, D, 0)],
                               [row(p['mix_norm']) + token[0:1, 0:1]], [(D, F32)], [D])
    return dh0[N_META:], dh0[:N_META], d_mix_norm, big


def _place():
    x, y, c = lax.axis_index("x"), lax.axis_index("y"), lax.axis_index("c")
    return x, y, c, [(1 - x, y), (x, 1 - y), (1 - x, 1 - y)]


_HBM = pl.BlockSpec(memory_space=pltpu.HBM)


def _all_gather(name, arrs, chip_blocks=()):
    n, nc = len(arrs), len(chip_blocks)

    def body(*refs):
        ins, cins, outs, couts = refs[:n], refs[n:n + nc], refs[n + nc:2 * n + nc], refs[2 * n + nc:2 * (n + nc)]
        send, recv, loc, csend, crecv = refs[2 * (n + nc):]
        x, y, c, chips = _place()
        me, sib = (x, y, c), (x, y, 1 - c)

        def cp(a, k, blk, to, src=None):
            dst = outs[a].at[4 * blk[0] + 2 * blk[1] + blk[2]]
            return pltpu.make_async_remote_copy(src_ref=dst if src is None else src, dst_ref=dst,
                                                send_sem=send.at[a, k], recv_sem=recv.at[a, k],
                                                device_id=to, device_id_type=MESH)

        between_chips = [pltpu.make_async_remote_copy(src_ref=cins[a].at[2 * ch[0] + ch[1]], dst_ref=couts[a].at[j],
                                                      send_sem=csend.at[a, j], recv_sem=crecv.at[a, j],
                                                      device_id=(*ch, c), device_id_type=MESH)
                         for a in range(nc) for j, ch in enumerate(chips)]
        mine = [pltpu.make_async_copy(ins[a], outs[a].at[4 * x + 2 * y + c], loc.at[a]) for a in range(n)]
        first = []
        for a in range(n):
            first.append(cp(a, 0, me, sib, src=ins[a]))
            first += [cp(a, 1 + j, me, (*ch, c), src=ins[a]) for j, ch in enumerate(chips)]
        for f in mine + first + between_chips:
            f.start()
        passed = []
        for a in range(n):
            for j, ch in enumerate(chips):
                cp(a, 1 + j, (*ch, c), me).wait_recv()
                passed.append(cp(a, 4 + j, (*ch, c), sib))
                passed[-1].start()
        for a in range(n):
            cp(a, 0, sib, me).wait_recv()
            for j, ch in enumerate(chips):
                cp(a, 4 + j, (*ch, 1 - c), me).wait_recv()
        for f in first + passed:
            f.wait_send()
        for m in mine + between_chips:
            m.wait()

    out = pl.pallas_call(
        body, name=name, in_specs=[_HBM] * (n + nc), out_specs=[_HBM] * (n + nc),
        out_shape=[jax.ShapeDtypeStruct((N_DEV,) + a.shape, a.dtype) for a in arrs]
        + [jax.ShapeDtypeStruct((3,) + a.shape[1:], a.dtype) for a in chip_blocks],
        scratch_shapes=[pltpu.SemaphoreType.DMA((n, 7)), pltpu.SemaphoreType.DMA((n, 7)),
                        pltpu.SemaphoreType.DMA((n,)), pltpu.SemaphoreType.DMA((max(nc, 1), 3)),
                        pltpu.SemaphoreType.DMA((max(nc, 1), 3))],
    )(*arrs, *chip_blocks)
    return (out[:n], out[n:]) if nc else out


def _exchange_cores(name, arrs):
    n = len(arrs)

    def body(*refs):
        ins, outs = refs[:n], refs[n:2 * n]
        send, recv = refs[2 * n:]
        x, y, c, _ = _place()
        remote = [pltpu.make_async_remote_copy(src_ref=ins[a].at[2 * k + 1 - c], dst_ref=outs[a].at[k],
                                               send_sem=send.at[a, k], recv_sem=recv.at[a, k],
                                               device_id=(x, y, 1 - c), device_id_type=MESH)
                  for a in range(n) for k in range(4)]
        for d in remote:
            d.start()
        for d in remote:
            d.wait()

    return pl.pallas_call(
        body, name=name, in_specs=[_HBM] * n, out_specs=[_HBM] * n,
        out_shape=[jax.ShapeDtypeStruct((4,) + a.shape[1:], a.dtype) for a in arrs],
        scratch_shapes=[pltpu.SemaphoreType.DMA((n, 4))] * 2,
    )(*arrs)


_SEM = pl.BlockSpec(memory_space=pltpu.SEMAPHORE)
_EFFECT = pltpu.SideEffectType.DATAFLOW_SIDE_EFFECTING


def _split_exchange(name, srcs, land_shapes, copies_of, n_cp, after=None, all_of=None):
    n = len(srcs)
    in_place = land_shapes is None
    lands = [] if in_place else [pltpu.with_memory_space_constraint(lax.empty(s, a.dtype), pltpu.HBM)
                                 for s, a in zip(land_shapes, srcs)]
    nb = n + len(lands)

    per = 1 if all_of else n_cp
    n_sem = n * per

    def descriptors(refs, send, recv, mirror):
        src_refs, land_refs = refs[:n], refs[nb - n:nb]
        x, y, c, chips = _place()
        out = []
        for a in range(n):
            if mirror and all_of:
                whole = all_of(a, land_refs[a])
                out.append(pltpu.make_async_remote_copy(src_ref=whole, dst_ref=whole, send_sem=send[a],
                                                        recv_sem=recv[a], device_id=(x, y, 1 - c),
                                                        device_id_type=MESH))
                continue
            for k, (src, dst, to, back) in enumerate(copies_of(a, src_refs[a], land_refs[a], x, y, c, chips)):
                out.append(pltpu.make_async_remote_copy(src_ref=src, dst_ref=back if mirror else dst,
                                                        send_sem=send[a * per + k % per],
                                                        recv_sem=recv[a * per + k % per],
                                                        device_id=to, device_id_type=MESH))
        return out

    def start_body(*refs):
        first = nb + (after is not None)
        sems = refs[first:first + 2 * n_sem]
        for d in descriptors(refs, sems[:n_sem], sems[n_sem:], False):
            d.start()
        refs[-1][...] = jnp.zeros((8, 128), F32)

    hbm_like = lambda arrs: [pltpu.HBM(a.shape, a.dtype) for a in arrs]
    order = [] if after is None else [after]
    res = pl.pallas_call(
        start_body, name=name + "_start", in_specs=[_HBM] * nb + [pl.BlockSpec(memory_space=pl.ANY)] * len(order),
        out_specs=[_SEM] * (2 * n_sem) + [_HBM] * nb + [pl.BlockSpec(memory_space=pltpu.VMEM)],
        out_shape=[pltpu.SemaphoreType.DMA(())] * (2 * n_sem)
        + hbm_like(srcs) + hbm_like(lands) + [jax.ShapeDtypeStruct((8, 128), F32)],
        input_output_aliases={i: 2 * n_sem + i for i in range(nb)},
        compiler_params=pltpu.CompilerParams(has_side_effects=_EFFECT),
    )(*srcs, *lands, *order)
    sems, thru, token = res[:2 * n_sem], res[2 * n_sem:2 * n_sem + nb], res[-1]

    def wait(after):
        def wait_body(*refs):
            s = refs[nb:nb + 2 * n_sem]
            for d in descriptors(refs, s[:n_sem], s[n_sem:], True):
                d.wait_send()
                d.wait_recv()

        out = pl.pallas_call(
            wait_body, name=name + "_wait",
            in_specs=[_HBM] * nb + [_SEM] * (2 * n_sem) + [pl.BlockSpec(memory_space=pl.ANY)],
            out_specs=[_HBM] * nb, out_shape=hbm_like(srcs) + hbm_like(lands),
            input_output_aliases={i: i for i in range(nb)},
            compiler_params=pltpu.CompilerParams(has_side_effects=_EFFECT),
        )(*thru, *sems, after)
        return out[:n], out[nb - n:]

    return token, wait


def _gather_copies(a, src, land, x, y, c, chips):
    peers = [(x, y, 1 - c)] + [(*ch, c) for ch in chips]
    me = 4 * x + 2 * y + c
    return [(src, land.at[me], to, land.at[4 * to[0] + 2 * to[1] + to[2]]) for to in peers]


def _to_all_copies(a, src, land, x, y, c, chips):
    peers = [(x, y, 1 - c)] + [(*ch, cc) for ch in chips for cc in (c, 1 - c)]
    me = 4 * x + 2 * y + c
    return [(src, land.at[me], to, land.at[4 * to[0] + 2 * to[1] + to[2]]) for to in peers]


def _cores_copies(a, src, land, x, y, c, chips):
    return [(src.at[2 * k + 1 - c], land.at[k], (x, y, 1 - c), land.at[k]) for k in range(4)]


def _forward_copies(a, src, land, x, y, c, chips):
    slot = lambda ch, core: 4 * ch[0] + 2 * ch[1] + core
    return [(land.at[slot(ch, c)], land.at[slot(ch, c)], (x, y, 1 - c), land.at[slot(ch, 1 - c)]) for ch in chips]


def _chips_copies(a, src, land, x, y, c, chips):
    return [(src.at[2 * ch[0] + ch[1]], land.at[j], (*ch, c), land.at[j]) for j, ch in enumerate(chips)]


def _gather_forward(name, arrs):
    n = len(arrs)

    def body(*refs):
        ins, outs = refs[:n], refs[n:2 * n]
        send, recv = refs[2 * n:]
        x, y, c, chips = _place()
        sends, recvs = [], []
        for a in range(n):
            for j, ch in enumerate(chips):
                mine, theirs = 4 * ch[0] + 2 * ch[1] + c, 4 * ch[0] + 2 * ch[1] + 1 - c
                sends.append(pltpu.make_async_remote_copy(src_ref=ins[a].at[mine], dst_ref=outs[a].at[mine],
                                                          send_sem=send.at[a, j], recv_sem=recv.at[a, j],
                                                          device_id=(x, y, 1 - c), device_id_type=MESH))
                recvs.append(pltpu.make_async_remote_copy(src_ref=ins[a].at[theirs], dst_ref=outs[a].at[theirs],
                                                          send_sem=send.at[a, j], recv_sem=recv.at[a, j],
                                                          device_id=(x, y, 1 - c), device_id_type=MESH))
        for d in sends:
            d.start()
        for s, r in zip(sends, recvs):
            s.wait_send()
            r.wait_recv()

    return pl.pallas_call(
        body, name=name, in_specs=[_HBM] * n, out_specs=[_HBM] * n,
        out_shape=[jax.ShapeDtypeStruct(a.shape, a.dtype) for a in arrs],
        input_output_aliases={i: i for i in range(n)},
        scratch_shapes=[pltpu.SemaphoreType.DMA((n, 3))] * 2,
    )(*arrs)


def _blocks_of(r, c):
    if r * c * 4 <= (2 << 20):
        return r, c
    if r % 128 == 0:
        return 128, c
    return r, 256


def _pair_sum(name, own, got, core):
    _, r, c = got.shape
    rb, cb = _blocks_of(r, c)

    def body(s_ref, a_ref, b_ref, o_ref):
        o_ref[...] = (a_ref[...].astype(F32) + b_ref[...].astype(F32)).astype(o_ref.dtype)

    return pl.pallas_call(
        body, name=name, out_shape=_out_hbm((4, r, c), got.dtype),
        grid_spec=pltpu.PrefetchScalarGridSpec(
            num_scalar_prefetch=1, grid=(4, r // rb, c // cb),
            in_specs=[pl.BlockSpec((None, rb, cb), lambda k, i, j, s: (2 * k + s[0], i, j)),
                      pl.BlockSpec((None, rb, cb), lambda k, i, j, s: (k, i, j))],
            out_specs=pl.BlockSpec((None, rb, cb), lambda k, i, j, s: (k, i, j))),
        compiler_params=_params(3, 24 << 20),
    )(core, _in_hbm(own), _in_hbm(got))


def _adamw_math(w, g, m, v):
    m = B1 * m + (1.0 - B1) * g
    v = B2 * v + (1.0 - B2) * (g * g)
    m_hat = m / (1.0 - B1 ** STEP)
    v_hat = v / (1.0 - B2 ** STEP)
    return -LR * (m_hat / (jnp.sqrt(v_hat) + ADAM_EPS) + WD * w), m, v


def _adamw_big(name, own, got, chip, w, m, v, behind):
    r, c = w.shape
    rb, cb = _blocks_of(r, c)

    def body(s_ref, o_ref, p_ref, w_ref, m_ref, v_ref, behind_ref, g_ref, d_ref, nm_ref, nv_ref):
        g = o_ref[...].astype(F32)
        for k in range(3):
            g = g + p_ref[k].astype(F32)
        g_ref[...] = g
        d_ref[...], nm_ref[...], nv_ref[...] = _adamw_math(w_ref[...], g, m_ref[...], v_ref[...])

    blk = pl.BlockSpec((rb, cb), lambda i, j, s: (i, j))
    return pl.pallas_call(
        body, name=name, out_shape=[jax.ShapeDtypeStruct((r, c), F32)] * 4,
        grid_spec=pltpu.PrefetchScalarGridSpec(
            num_scalar_prefetch=1, grid=(r // rb, c // cb),
            in_specs=[pl.BlockSpec((None, rb, cb), lambda i, j, s: (s[0], i, j)),
                      pl.BlockSpec((3, rb, cb), lambda i, j, s: (0, i, j)), blk, blk, blk,
                      pl.BlockSpec(memory_space=pl.ANY)],
            out_specs=[blk] * 4),
        compiler_params=_params(2, 40 << 20),
    )(chip, *map(_in_hbm, (own, got, w, m, v)), behind)


def _adamw_multi(name, items, sums=(), row_blocks=1, own=(), dev=None):
    n, ns = len(items), len(sums)
    n_own = len(own)
    assert n_own in (0, n + ns)

    def body(*refs):
        dev_ref = refs[0] if n_own else None
        refs = refs[1:] if n_own else refs
        owns, refs = refs[:n_own], refs[n_own:]
        ins, outs = refs[:4 * n + ns], refs[4 * n + ns:]

        def total(ref, t):
            g = None
            for d in range(ref.shape[0]):
                part = jnp.where(dev_ref[0] == d, owns[t][...], ref[d]) if n_own else ref[d]
                g = part.astype(F32) if g is None else g + part.astype(F32)
            return g

        for t in range(n):
            p_ref, w_ref, m_ref, v_ref = ins[4 * t:4 * t + 4]
            g = total(p_ref, t)
            outs[4 * t][...] = g
            outs[4 * t + 1][...], outs[4 * t + 2][...], outs[4 * t + 3][...] = _adamw_math(
                w_ref[...], g, m_ref[...], v_ref[...])
        for t in range(ns):
            outs[4 * n + t][...] = total(ins[4 * n + t], n + t)

    def spec(shape, lead):
        blk = (shape[0] // row_blocks,) + tuple(shape[1:])
        nd = len(shape)
        if lead:
            return pl.BlockSpec((lead,) + blk, lambda i: (0, i) + (0,) * (nd - 1))
        return pl.BlockSpec(blk, lambda i: (i,) + (0,) * (nd - 1))

    operands, in_specs, out_specs, out_shape = [], [], [], []
    if n_own:
        operands += [dev] + list(own)
        in_specs += [pl.BlockSpec(memory_space=pltpu.SMEM)] + [spec(o.shape, 0) for o in own]
    for parts, w, m, v in items:
        assert parts.shape[1:] == w.shape == m.shape == v.shape, (name, parts.shape, w.shape)
        operands += [parts, w, m, v]
        in_specs += [spec(w.shape, parts.shape[0])] + [spec(w.shape, 0)] * 3
        out_specs += [spec(w.shape, 0)] * 4
        out_shape += [jax.ShapeDtypeStruct(w.shape, F32)] * 4
    for parts in sums:
        operands.append(parts)
        in_specs.append(spec(parts.shape[1:], parts.shape[0]))
        out_specs.append(spec(parts.shape[1:], False))
        out_shape.append(jax.ShapeDtypeStruct(parts.shape[1:], F32))
    return pl.pallas_call(
        body, name=name, grid=(row_blocks,), in_specs=in_specs, out_specs=out_specs, out_shape=out_shape,
        compiler_params=_params(1, 56 << 20),
    )(*operands)


def kernel(x, meta_tokens, mix_norm, w_in, lam_re, lam_im, log_dt, b_re, b_im, c_re, c_im, d_skip, w_glu, b_glu, q_a_norm, w_q_b, kv_a_norm, w_kv_b, out_norm_ssm, out_norm_attn, w_out, ffn_norm, w_up, conv_w, conv_b, w_down, final_norm, loss_target, m_meta_tokens, m_mix_norm, m_w_in, m_lam_re, m_lam_im, m_log_dt, m_b_re, m_b_im, m_c_re, m_c_im, m_d_skip, m_w_glu, m_b_glu, m_q_a_norm, m_w_q_b, m_kv_a_norm, m_w_kv_b, m_out_norm_ssm, m_out_norm_attn, m_w_out, m_ffn_norm, m_w_up, m_conv_w, m_conv_b, m_w_down, m_final_norm, v_meta_tokens, v_mix_norm, v_w_in, v_lam_re, v_lam_im, v_log_dt, v_b_re, v_b_im, v_c_re, v_c_im, v_d_skip, v_w_glu, v_b_glu, v_q_a_norm, v_w_q_b, v_kv_a_norm, v_w_kv_b, v_out_norm_ssm, v_out_norm_attn, v_w_out, v_ffn_norm, v_w_up, v_conv_w, v_conv_b, v_w_down, v_final_norm):
    given = dict(locals())
    w = {n: given[n] for n in WEIGHTS}
    m = {n: given['m_' + n] for n in WEIGHTS}
    v = {n: given['v_' + n] for n in WEIGHTS}
    dev = 4 * lax.axis_index("x") + 2 * lax.axis_index("y") + lax.axis_index("c")

    shard = {
        'w_in': jnp.pad(w_in[0], ((0, 0), (0, D_IN_PAD - D_IN))),
        'w_glu': w_glu[0],
        'w_q_b': jnp.pad(w_q_b[0], ((0, 0), (0, HEAD_PAD - QK_NOPE - QK_ROPE))),
        'w_kv_b': w_kv_b[0],
        'w_out': w_out[0],
        'w_up': w_up[0],
        'w_down': w_down[0],
    }
    core = lax.axis_index("c").astype(jnp.int32).reshape(1)
    chip = (2 * lax.axis_index("x") + lax.axis_index("y")).astype(jnp.int32).reshape(1)
    mid, late = ['w_q_b', 'w_kv_b', 'w_glu'], ['w_out', 'w_up', 'w_down']
    shard = {n: a.astype(BF16) for n, a in shard.items()}

    gathered = _all_gather("gather_early", [shard['w_in'], meta_tokens, jnp.pad(conv_w[0], ((0, 5), (0, 0)))])
    wf = {'w_in': gathered[0].reshape(-1, D_IN_PAD)}
    meta = gathered[1].transpose(1, 0, 2).reshape(N_META, D)
    wf['conv_w'] = gathered[2][:, :3].transpose(1, 0, 2).reshape(3, D_FF)
    mid_token, wait_mid = _split_exchange("gather_mid", [shard[n] for n in mid],
                                          [(N_DEV,) + shard[n].shape for n in mid], _gather_copies, 4,
                                          after=gathered[0])
    gather_token, late_waits = mid_token, {}
    for n in late:
        gather_token, late_waits[n] = _split_exchange("gather_" + n, [shard[n]], [(N_DEV,) + shard[n].shape],
                                                      _gather_copies, 4, after=gather_token)

    def with_mine(mine, landed):
        return [lax.dynamic_update_slice(a, s[None], (dev, 0, 0)) for s, a in zip(mine, landed)]

    def mid_weights(after):
        full = _gather_forward("gather_forward_mid", with_mine(*wait_mid(after)))
        return {'w_q_b': full[0], 'w_kv_b': full[1], 'w_glu': full[2].reshape(-1, D_SSM)}

    def late_weights(after):
        full_out, = _gather_forward("gather_forward_w_out", with_mine(*late_waits['w_out'](after)))
        token, wait_up = _split_exchange("gather_forward_w_up", with_mine(*late_waits['w_up'](after)), None,
                                         _forward_copies, 3, after=full_out)

        def up_weights(after):
            full_up, = wait_up(after)[0]
            token, wait_down = _split_exchange("gather_forward_w_down", with_mine(*late_waits['w_down'](after)),
                                               None, _forward_copies, 3, after=full_up)
            return full_up, token, lambda after: wait_down(after)[0][0].reshape(-1, D)

        return full_out.reshape(-1, D), token, up_weights

    pending, between_cores = [], {}

    def send_grads(names, arg, stage):
        if stage == 'cores':
            token, between_cores[names[0]] = _split_exchange(
                "reduce_cores_" + names[0], arg, [(4,) + a.shape[1:] for a in arg], _cores_copies, 4)
            return token
        mine, got = between_cores.pop(names[0])(arg) if stage == 'chips' else (
            arg, _exchange_cores("reduce_cores_" + names[0], arg))
        sums = [_pair_sum("sum_cores_" + n, a, g, core) for n, a, g in zip(names, mine, got)]
        token, wait = _split_exchange("reduce_chips_" + names[0], sums, [(3,) + s.shape[1:] for s in sums],
                                      _chips_copies, 3)
        pending.append((names, wait))
        return token

    small_sent = []
    s5_bc = ['b_re', 'b_im', 'c_re', 'c_im']

    def send_small(sm, dcw4, loss_part):
        names = [n for n in SMALL if n in sm]
        arrs = [sm[n].reshape(-1, 128).astype(BF16) if n in s5_bc else sm[n] for n in names] + [dcw4, loss_part]
        token, wait = _split_exchange("gather_small", arrs, [(N_DEV,) + a.shape for a in arrs], _to_all_copies, 7,
                                      all_of=lambda a, land: land.at[pl.ds(0, N_DEV - 1)])
        small_sent.append((names, wait))
        return token

    small = {n: w[n] for n in SMALL}
    small['mix_norm'] = mix_norm + gather_token[0:1, 0:1]
    grad_x, dmeta, d_mix_norm, big = _local_step(x[0], loss_target[0], meta, small, wf, mid_weights, late_weights,
                                                  send_grads, send_small)
    grads, deltas, new_m, new_v = {}, {}, {}, {}

    def keep(n, outs):
        grads[n], deltas[n], new_m[n], new_v[n] = (o.reshape(w[n].shape) for o in outs)

    rest = list(big)
    from_core = _exchange_cores("reduce_cores", [big[n] for n in rest])
    chip_sums = [_pair_sum("sum_cores_" + n, big[n], got, core) for n, got in zip(rest, from_core)]
    last = chip_sums + [d_mix_norm, dmeta]
    n_big = len(chip_sums)
    last_token, wait_last = _split_exchange(
        "reduce_last", last, [(3,) + s.shape[1:] for s in chip_sums] + [(N_DEV,) + a.shape for a in last[n_big:]],
        lambda a, *at: _chips_copies(a, *at) if a < n_big else _to_all_copies(a, *at), 7,
        all_of=lambda a, land: land if a < n_big else land.at[pl.ds(0, N_DEV - 1)])

    def update(n, own, got):
        if n == 'w_up':
            outs = _adamw_big("adamw_" + n, own, got, chip, w[n][0].T, m[n][0].T, v[n][0].T, last_token)
            return keep(n, [o.T for o in outs])
        cols = w[n].shape[2]
        keep(n, _adamw_big("adamw_" + n, own[:, :, :cols], got[:, :, :cols], chip, w[n][0], m[n][0], v[n][0],
                           last_token))

    for names, wait in pending:
        for n, own, got in zip(names, *wait(chip_sums[0])):
            update(n, own, got)
    updated = [new_v[n].reshape(-1)[:1] for names, _ in pending for n in names]
    sent, landed = wait_last(functools.reduce(jnp.add, updated))
    for n, own, got in zip(rest, sent, landed):
        update(n, own, got)

    device = dev.astype(jnp.int32).reshape(1)
    outs = _adamw_multi("adamw_last", [(landed[n_big], mix_norm, m['mix_norm'], v['mix_norm'])],
                        sums=[landed[n_big + 1]], own=sent[n_big:], dev=device)
    keep('mix_norm', outs[:4])
    g_meta = outs[4]
    names, wait = small_sent[0]
    mine, landed = wait(grads['mix_norm'])
    parts, own = dict(zip(names + ['conv_w', 'loss'], landed)), dict(zip(names + ['conv_w', 'loss'], mine))
    wide = [n for n in names if n not in s5_bc]
    summed = ['conv_w', 'loss'] + s5_bc
    outs = _adamw_multi("adamw_small",
                        [(parts[n],) + tuple(t[n].reshape(own[n].shape) for t in (w, m, v)) for n in wide],
                        sums=[parts[n] for n in summed], own=[own[n] for n in wide + summed], dev=device)
    for i, n in enumerate(wide):
        keep(n, outs[4 * i:4 * i + 4])
    g_cw4, loss = outs[-6:-4]
    rows_gc = lambda a, n: (a.reshape(N_GROUPS, SSM_STATE, SSM_GROUP).transpose(0, 2, 1) if n[0] == 'b' else a
                            ).reshape(N_GROUPS * SSM_GROUP, SSM_STATE)
    outs = _adamw_multi("adamw_s5_bc", [(g.reshape(1, N_GROUPS * SSM_GROUP, SSM_STATE),)
                                        + tuple(rows_gc(t[n], n) for t in (w, m, v))
                                        for n, g in zip(s5_bc, outs[-4:])])
    for i, n in enumerate(s5_bc):
        back = outs[4 * i:4 * i + 4]
        if n[0] == 'b':
            back = [o.reshape(N_GROUPS, SSM_GROUP, SSM_STATE).transpose(0, 2, 1) for o in back]
        keep(n, back)

    g_meta = lax.dynamic_slice(g_meta, (0, dev * (D // N_DEV)), (N_META, D // N_DEV))
    g_conv = lax.dynamic_slice(g_cw4.transpose(1, 0, 2).reshape(3, D_FF), (0, dev * (D_FF // N_DEV)),
                               (3, D_FF // N_DEV))
    rows8 = lambda a: jnp.pad(a.reshape(3, D_FF // N_DEV), ((0, 5), (0, 0)))
    outs = _adamw_multi("adamw_cols", [(g_meta[None], meta_tokens, m['meta_tokens'], v['meta_tokens']),
                                       (rows8(g_conv)[None], rows8(conv_w), rows8(m['conv_w']), rows8(v['conv_w']))])
    keep('meta_tokens', outs[:4])
    keep('conv_w', [o[:3] for o in outs[4:]])

    return (loss[0, 0], grad_x[None], *[grads[n] for n in WEIGHTS], *[deltas[n] for n in WEIGHTS],
            *[new_m[n] for n in WEIGHTS], *[new_v[n] for n in WEIGHTS])
```

```python
import functools
import math

import jax
import jax.numpy as jnp
from jax import lax
from jax.experimental import pallas as pl
from jax.experimental.pallas import tpu as pltpu

F32, BF16 = jnp.float32, jnp.bfloat16

N_DEV = 8
N_META, SEQ, D = 16, 2048, 2048
T = N_META + SEQ
TR = 688
EPS = 1e-6
D_SSM, Q_LORA, KV_LORA, QK_ROPE = 1024, 512, 256, 64
D_IN = D_SSM + Q_LORA + KV_LORA + QK_ROPE
D_IN_PAD = 1920
N_HEADS, QK_NOPE, V_HEAD = 8, 128, 128
HEAD_PAD = 256
SM_SCALE = 1.0 / math.sqrt(QK_NOPE + QK_ROPE)
CHUNK = 64
N_GROUPS, SSM_GROUP, SSM_STATE = 64, 16, 64
N_SEG = 8
SEG = 264
TP = N_SEG * SEG
GQ = 8
D_FF = 5504
FF_PIECE = 1376
ROPE_BASE = 10000.0
LR, B1, B2, ADAM_EPS, WD, STEP = 0.001, 0.9, 0.999, 1e-08, 0.01, 10
VMEM_CAP = 60 * 1024 * 1024
MESH = pl.DeviceIdType.MESH

WEIGHTS = ['meta_tokens', 'mix_norm', 'w_in', 'lam_re', 'lam_im', 'log_dt', 'b_re', 'b_im', 'c_re', 'c_im',
           'd_skip', 'w_glu', 'b_glu', 'q_a_norm', 'w_q_b', 'kv_a_norm', 'w_kv_b', 'out_norm_ssm',
           'out_norm_attn', 'w_out', 'ffn_norm', 'w_up', 'conv_w', 'conv_b', 'w_down', 'final_norm']
BIG = ['w_in', 'w_glu', 'w_q_b', 'w_kv_b', 'w_out', 'w_up', 'w_down']
SMALL = [n for n in WEIGHTS if n not in BIG and n not in ('meta_tokens', 'conv_w')]


def _nbytes(shape, dtype):
    return math.prod(shape) * jnp.dtype(dtype).itemsize


def _in_hbm(a):
    return pltpu.with_memory_space_constraint(a, pltpu.HBM) if a.size * a.dtype.itemsize >= (1 << 20) else a


def _out_hbm(shape, dtype):
    big = math.prod(shape) * jnp.dtype(dtype).itemsize >= (1 << 20)
    return pltpu.HBM(shape, dtype) if big else jax.ShapeDtypeStruct(shape, dtype)


def _params(sem, need):
    return pltpu.CompilerParams(dimension_semantics=("arbitrary",) * sem,
                                vmem_limit_bytes=int(min(VMEM_CAP, max(need, 16 * 1024 * 1024))))


_DIMS = {'nn': (((1,), (0,)), ((), ())), 'nt': (((1,), (1,)), ((), ())), 'tn': (((0,), (0,)), ((), ()))}


def _mm(name, a, b, mode, tm, tn, tk, out_dtype, stack=None, res=None):
    sa, sb, so = stack in ('a_out', 'ab_red'), stack in ('b_out', 'ab_red'), stack in ('a_out', 'b_out')
    nj = a.shape[0] if sa else (b.shape[0] if sb else 1)
    a2, b2 = a.shape[-2:], b.shape[-2:]
    if mode == 'nn':
        (m, k), (k2, n) = a2, b2
    elif mode == 'nt':
        (m, k), (n, k2) = a2, b2
    else:
        (k, m), (k2, n) = a2, b2
    assert k == k2 and m % tm == 0 and n % tn == 0 and k % tk == 0, (name, a.shape, b.shape)
    n_jo, n_jr, n_k = (nj if so else 1), (nj if stack == 'ab_red' else 1), k // tk
    grid = (n_jo, m // tm, n // tn, n_jr, n_k)
    a_blk, a_idx = ((tk, tm), lambda i, j, kk: (kk, i)) if mode == 'tn' else ((tm, tk), lambda i, j, kk: (i, kk))
    b_blk, b_idx = ((tn, tk), lambda i, j, kk: (j, kk)) if mode == 'nt' else ((tk, tn), lambda i, j, kk: (kk, j))

    def spec(blk, idx, stacked, on_out):
        if not stacked:
            return pl.BlockSpec(blk, lambda jo, i, j, jr, kk: idx(i, j, kk))
        return pl.BlockSpec((None,) + blk, lambda jo, i, j, jr, kk: ((jo if on_out else jr),) + idx(i, j, kk))

    o_idx = lambda i, j, kk: (i, j)
    in_specs = [spec(a_blk, a_idx, sa, stack == 'a_out'), spec(b_blk, b_idx, sb, stack == 'b_out')]
    operands = [a, b]
    if res is not None:
        in_specs.append(spec((tm, tn), o_idx, False, False))
        operands.append(res)
    n_red = n_jr * n_k
    dims = _DIMS[mode]

    def body(*refs):
        a_ref, b_ref = refs[0], refs[1]
        res_ref = refs[2] if res is not None else None
        o_ref = refs[3] if res is not None else refs[2]
        part = lax.dot_general(a_ref[...].astype(BF16), b_ref[...].astype(BF16), dims,
                               preferred_element_type=F32)

        def finish(total):
            if res_ref is not None:
                total = total + res_ref[...]
            o_ref[...] = total.astype(o_ref.dtype)

        if n_red == 1:
            finish(part)
        else:
            acc_ref = refs[-1]
            step = pl.program_id(3) * n_k + pl.program_id(4)

            @pl.when(step == 0)
            def _():
                acc_ref[...] = part

            @pl.when(step > 0)
            def _():
                acc_ref[...] += part

            @pl.when(step == n_red - 1)
            def _():
                finish(acc_ref[...])

    out_shape = ((nj,) if so else ()) + (m, n)
    need = 2 * (_nbytes(a_blk, a.dtype) + _nbytes(b_blk, b.dtype) + _nbytes((tm, tn), out_dtype)
                + (_nbytes((tm, tn), F32) if res is not None else 0)) + 3 * _nbytes((tm, tn), F32) + (4 << 20)
    return pl.pallas_call(
        body, name=name, grid=grid, in_specs=in_specs, out_specs=spec((tm, tn), o_idx, so, True),
        out_shape=_out_hbm(out_shape, out_dtype),
        scratch_shapes=[pltpu.VMEM((tm, tn), F32)] if n_red > 1 else [],
        compiler_params=_params(5, need),
    )(*map(_in_hbm, operands))


def _rowwise(name, fn, row_ins, const_ins, row_outs, acc_outs, tr=TR):
    n_ri, n_ci, n_ro, n_ao = len(row_ins), len(const_ins), len(row_outs), len(acc_outs)

    def body(*refs):
        ri, ci = refs[:n_ri], refs[n_ri:n_ri + n_ci]
        ro, ao = refs[n_ri + n_ci:n_ri + n_ci + n_ro], refs[n_ri + n_ci + n_ro:]
        i = pl.program_id(0)
        tiles = [jnp.where(i == 0, pltpu.roll(r[...], lead, 0), r[...]) if lead else r[...]
                 for r, lead in zip(ri, leads)]
        outs = fn(i * tr, *tiles, *[r[...] for r in ci])
        for r, v in zip(ro, outs[:n_ro]):
            r[...] = v.astype(r.dtype)
        if n_ao:
            @pl.when(i == 0)
            def _():
                for r, v in zip(ao, outs[n_ro:]):
                    r[...] = v

            @pl.when(i > 0)
            def _():
                for r, v in zip(ao, outs[n_ro:]):
                    r[...] += v

    def row_spec(a, w, cb, lead=0):
        if not lead:
            return pl.BlockSpec((tr, w), lambda i: (i, cb))
        assert a.shape[0] + lead == T and lead % 8 == 0 and tr % 8 == 0
        return pl.BlockSpec((pl.Element(tr), pl.Element(w)),
                            lambda i: (jnp.maximum(i * (tr // 8) - lead // 8, 0) * 8, cb * w))

    in_specs = [row_spec(*r) for r in row_ins]
    leads = [r[3] if len(r) > 3 else 0 for r in row_ins]
    row_ins = [r[:3] for r in row_ins]
    in_specs += [pl.BlockSpec(c.shape, functools.partial(lambda nd, i: (0,) * nd, c.ndim)) for c in const_ins]
    out_specs = [pl.BlockSpec((tr, w), lambda i: (i, 0)) for w, _ in row_outs]
    out_specs += [pl.BlockSpec((1, w), lambda i: (0, 0)) for w in acc_outs]
    out_shape = [_out_hbm((T, w), dt) for w, dt in row_outs]
    out_shape += [jax.ShapeDtypeStruct((1, w), F32) for w in acc_outs]
    need = 2 * (sum(_nbytes((tr, w), a.dtype) for a, w, _ in row_ins) + sum(_nbytes((tr, w), dt) for w, dt in row_outs))
    need = 3 * need + (8 << 20)
    return pl.pallas_call(
        body, name=name, grid=(T // tr,), in_specs=in_specs, out_specs=out_specs, out_shape=out_shape,
        compiler_params=_params(1, need),
    )(*[_in_hbm(a) for a, _, _ in row_ins], *const_ins)


def _rms_fwd_fn(row0, h, g):
    h = h.astype(F32)
    r = lax.rsqrt(jnp.mean(h * h, axis=-1, keepdims=True) + EPS)
    return (h * r) * g, r


def _rms_mix_fn(row0, x, meta, g):
    h = jnp.where(row0 == 0, jnp.concatenate([meta, x[N_META:]], axis=0), x)
    return (h,) + _rms_fwd_fn(row0, h, g)


def _rms_bwd(dxn, h, r, g):
    xhat = h * r
    dxg = dxn * g
    dh = r * (dxg - xhat * jnp.mean(dxg * xhat, axis=-1, keepdims=True))
    return dh, jnp.sum(dxn * xhat, axis=0, keepdims=True)


def _rms_bwd_fn(row0, dxn, h, r, g):
    return _rms_bwd(dxn.astype(F32), h.astype(F32), r, g)


def _rms_bwd_res_fn(row0, dxn, h, r, res, g):
    dh, dg = _rms_bwd(dxn.astype(F32), h.astype(F32), r, g)
    return dh + res, dg


def _rms_bwd_res_twice_fn(*args):
    dh, dg = _rms_bwd_res_fn(*args)
    return dh, dh, dg


def _gelu_fn(row0, y):
    return (jax.nn.gelu(y),)


def _glu_fwd_fn(row0, y, gl, b, gain):
    ya = jax.nn.gelu(y) * jax.nn.sigmoid(gl + b)
    return _rms_fwd_fn(row0, ya, gain)


def _glu_bwd_fn(row0, dyan, y, gl, ra, b, gain):
    g = jax.nn.gelu(y)
    s = jax.nn.sigmoid(gl + b)
    dya, dgain = _rms_bwd(dyan, g * s, ra, gain)
    dgl = dya * g * s * (1.0 - s)
    return dgl, dya * s, dgain, jnp.sum(dgl, axis=0, keepdims=True)


def _gelu_bwd_fn(row0, dg, y):
    return (jax.vjp(jax.nn.gelu, y)[1](dg)[0],)


def _loss_fn(row0, h, tgt, gain):
    r = lax.rsqrt(jnp.mean(h * h, axis=-1, keepdims=True) + EPS)
    xhat = h * r
    rows = row0 + lax.broadcasted_iota(jnp.int32, h.shape, 0)
    err = jnp.where(rows >= N_META, xhat * gain - tgt, 0.0)
    loss = jnp.full((1, 128), jnp.sum(err * err) * (0.5 / D), F32)
    dh, dgain = _rms_bwd(err * (1.0 / D), h, r, gain)
    return dh, dh, dgain, loss


def _rope_tables():
    pos = jnp.arange(T, dtype=jnp.int32)
    inv_freq = 1.0 / (ROPE_BASE ** (jnp.arange(0, QK_ROPE, 2, dtype=F32) / QK_ROPE))
    ang = pos.astype(F32)[:, None] * inv_freq[None, :]
    cos, sin, z32, z64 = jnp.cos(ang), jnp.sin(ang), jnp.zeros((T, 32), F32), jnp.zeros((T, 64), F32)
    return (jnp.concatenate([cos, cos, z64], 1), jnp.concatenate([-sin, z32, z64], 1),
            jnp.concatenate([z32, sin, z64], 1))


def _rot(x, cc, s1, s2):
    return x * cc + pltpu.roll(x, 96, 1) * s1 + pltpu.roll(x, 32, 1) * s2


def _derot(d, cc, s1, s2):
    return d * cc + pltpu.roll(d * s1, 32, 1) + pltpu.roll(d * s2, 96, 1)


def _chunk_of(pos):
    return jnp.where(pos < N_META, 0, 1 + ((pos - N_META) >> 6))


def _key_len(t):
    last_chunk = 1 + ((t + 1) * TR - 1 - N_META) // CHUNK
    return min(T, pl.cdiv(N_META + last_chunk * CHUNK, 128) * 128)


def _attn_probs(t, q_ref, kv_ref, kpe_ref, tq, tk):
    kl = _key_len(t)
    q = q_ref[...]
    qn = q[:, :QK_NOPE].astype(BF16)
    qp = _rot(q[:, QK_NOPE:], tq[0][...], tq[1][...], tq[2][...]).astype(BF16)
    kn = kv_ref[:kl, :QK_NOPE].astype(BF16)
    v = kv_ref[:kl, QK_NOPE:].astype(BF16)
    kp = _rot(kpe_ref[:kl, :], tk[0][:kl, :], tk[1][:kl, :], tk[2][:kl, :]).astype(BF16)
    s = lax.dot_general(qn, kn, _DIMS['nt'], preferred_element_type=F32)
    s = s + lax.dot_general(qp, kp, _DIMS['nt'], preferred_element_type=F32)
    qc = _chunk_of(t * TR + lax.broadcasted_iota(jnp.int32, (TR, 1), 0))
    kc = _chunk_of(lax.broadcasted_iota(jnp.int32, (1, kl), 1))
    s = jnp.where(kc <= qc, s * SM_SCALE, jnp.finfo(F32).min)
    e = jnp.exp(s - jnp.max(s, axis=-1, keepdims=True))
    return qn, qp, kn, kp, v, e, 1.0 / jnp.sum(e, axis=-1, keepdims=True)


def _per_q_tile(fn):
    for t in range(T // TR):
        pl.when(pl.program_id(1) == t)(functools.partial(fn, t))


def _attn_specs(z, tabs):
    q_spec = pl.BlockSpec((None, TR, HEAD_PAD), lambda h, i: (h, i, 0))
    kv_spec = pl.BlockSpec((None, T, HEAD_PAD), lambda h, i: (h, 0, 0))
    kpe_spec = pl.BlockSpec((T, 128), lambda h, i: (0, (D_IN - QK_ROPE) // 128))
    tq = [pl.BlockSpec((TR, 128), lambda h, i: (i, 0))] * 3
    tk = [pl.BlockSpec((T, 128), lambda h, i: (0, 0))] * 3
    return [q_spec, kv_spec, kpe_spec] + tq + tk


def _attn_fwd(q, kv, z, tabs):
    def body(q_ref, kv_ref, kpe_ref, c1, c2, c3, k1, k2, k3, o_ref):
        def tile(t):
            _, _, _, _, v, e, inv = _attn_probs(t, q_ref, kv_ref, kpe_ref, (c1, c2, c3), (k1, k2, k3))
            o_ref[...] = jnp.dot(e.astype(BF16), v, preferred_element_type=F32) * inv

        _per_q_tile(tile)

    return pl.pallas_call(
        body, name="attn_fwd", grid=(N_HEADS, T // TR), in_specs=_attn_specs(z, tabs),
        out_specs=pl.BlockSpec((TR, V_HEAD), lambda h, i: (i, h)),
        out_shape=_out_hbm((T, N_HEADS * V_HEAD), F32),
        compiler_params=_params(2, 48 << 20),
    )(_in_hbm(q), _in_hbm(kv), _in_hbm(z), *tabs, *tabs)


def _attn_bwd(q, kv, z, tabs, dyb):
    def body(q_ref, kv_ref, kpe_ref, c1, c2, c3, k1, k2, k3, do_ref, dq_ref, dkv_ref, dkpe_ref):
        @pl.when(pl.program_id(1) == 0)
        def _():
            dkv_ref[...] = jnp.zeros(dkv_ref.shape, F32)
            dkpe_ref[...] = jnp.zeros(dkpe_ref.shape, F32)

        def tile(t):
            kl = _key_len(t)
            qn, qp, kn, kp, v, e, inv = _attn_probs(t, q_ref, kv_ref, kpe_ref, (c1, c2, c3), (k1, k2, k3))
            do = do_ref[...]
            dp = lax.dot_general(do.astype(BF16), v, _DIMS['nt'], preferred_element_type=F32)
            delta = jnp.sum(e * dp, axis=-1, keepdims=True) * inv
            ds = (e * ((dp - delta) * (inv * SM_SCALE))).astype(BF16)
            pb, do = e.astype(BF16), (do * inv).astype(BF16)
            dq_ref[:, :QK_NOPE] = jnp.dot(ds, kn, preferred_element_type=F32).astype(dq_ref.dtype)
            dqp = _derot(jnp.dot(ds, kp, preferred_element_type=F32), c1[...], c2[...], c3[...])
            dq_ref[:, QK_NOPE:] = dqp.astype(dq_ref.dtype)
            dkv_ref[:kl, :QK_NOPE] += lax.dot_general(ds, qn, _DIMS['tn'], preferred_element_type=F32)
            dkv_ref[:kl, QK_NOPE:] += lax.dot_general(pb, do, _DIMS['tn'], preferred_element_type=F32)
            dkpe_ref[:kl, :] += lax.dot_general(ds, qp, _DIMS['tn'], preferred_element_type=F32)

        _per_q_tile(tile)

    return pl.pallas_call(
        body, name="attn_bwd", grid=(N_HEADS, T // TR),
        in_specs=_attn_specs(z, tabs) + [pl.BlockSpec((TR, V_HEAD), lambda h, i: (i, h))],
        out_specs=[pl.BlockSpec((None, TR, HEAD_PAD), lambda h, i: (h, i, 0)),
                   pl.BlockSpec((None, T, HEAD_PAD), lambda h, i: (h, 0, 0)),
                   pl.BlockSpec((None, T, 128), lambda h, i: (h, 0, 0))],
        out_shape=[_out_hbm((N_HEADS, T, HEAD_PAD), BF16), _out_hbm((N_HEADS, T, HEAD_PAD), F32),
                   _out_hbm((N_HEADS, T, 128), F32)],
        compiler_params=_params(2, 56 << 20),
    )(_in_hbm(q), _in_hbm(kv), _in_hbm(z), *tabs, *tabs, _in_hbm(dyb))


def _kpe_bwd(dkpe, tabs):
    def body(d_ref, c1, c2, c3, o_ref):
        d = d_ref[0]
        for h in range(1, N_HEADS):
            d = d + d_ref[h]
        o_ref[...] = _derot(d, c1[...], c2[...], c3[...]).astype(o_ref.dtype)

    tile = pl.BlockSpec((TR, 128), lambda i: (i, 0))
    return pl.pallas_call(
        body, name="kpe_bwd", grid=(T // TR,),
        in_specs=[pl.BlockSpec((N_HEADS, TR, 128), lambda i: (0, i, 0)), tile, tile, tile], out_specs=tile,
        out_shape=jax.ShapeDtypeStruct((T, 128), BF16), compiler_params=_params(1, 16 << 20),
    )(_in_hbm(dkpe), *tabs)


def _s5_prep_math(lam_re, lam_im, log_dt, bt_re, bt_im):
    dt = jnp.exp(log_dt)
    mag = jnp.exp(lam_re * dt)
    a_re, a_im = mag * jnp.cos(lam_im * dt), mag * jnp.sin(lam_im * dt)
    den = lam_re * lam_re + lam_im * lam_im
    w_re = ((a_re - 1.0) * lam_re + a_im * lam_im) / den
    w_im = (a_im * lam_re - (a_re - 1.0) * lam_im) / den
    return a_re, a_im, w_re[None] * bt_re - w_im[None] * bt_im, w_re[None] * bt_im + w_im[None] * bt_re


def _eye_groups():
    return (lax.broadcasted_iota(jnp.int32, (N_GROUPS, N_GROUPS), 0)
            == lax.broadcasted_iota(jnp.int32, (N_GROUPS, N_GROUPS), 1)).astype(F32)


def _row_to_col(row):
    return jnp.sum(_eye_groups() * row, axis=1, keepdims=True)


def _s5_prep(lam_re, lam_im, log_dt, bt_re, bt_im):
    def body(lr, li, ld, br, bi, ar, ai, bbr, bbi):
        ar[...], ai[...], bbr[...], bbi[...] = _s5_prep_math(lr[...], li[...], _row_to_col(ld[...]), br[...], bi[...])

    gp, cgp = jax.ShapeDtypeStruct((N_GROUPS, SSM_STATE), F32), jax.ShapeDtypeStruct(bt_re.shape, F32)
    return pl.pallas_call(body, name="s5_prep", out_shape=[gp, gp, cgp, cgp])(lam_re, lam_im, log_dt, bt_re, bt_im)


def _s5_prep_bwd(lam_re, lam_im, log_dt, bt_re, bt_im, da_re, da_im, dbb_re, dbb_im):
    def body(lr, li, ld, br, bi, dar, dai, dbr, dbi, o1, o2, o3, o4, o5):
        _, vjp = jax.vjp(_s5_prep_math, lr[...], li[...], _row_to_col(ld[...]), br[...], bi[...])
        o1[...], o2[...], dld, o4[...], o5[...] = vjp((dar[...], dai[...], dbr[...], dbi[...]))
        o3[...] = jnp.sum(_eye_groups() * dld, axis=0, keepdims=True)

    ins = (lam_re, lam_im, log_dt, bt_re, bt_im)
    return pl.pallas_call(body, name="s5_prep_bwd", out_shape=[jax.ShapeDtypeStruct(a.shape, F32) for a in ins])(
        *ins, da_re, da_im, dbb_re, dbb_im)


def _cmul(ar, ai, br, bi):
    return ar * br - ai * bi, ar * bi + ai * br


def _seg_rows(i):
    return pl.ds(pl.multiple_of(i * N_SEG, N_SEG), N_SEG)


def _scan(xr, xi, ar, ai, reverse):
    zero = jnp.zeros(ar.shape, F32)

    def local(j, carry):
        rows = _seg_rows(SEG - 1 - j if reverse else j)
        nr, ni = _cmul(ar, ai, *carry)
        nr, ni = nr + xr[rows, :], ni + xi[rows, :]
        xr[rows, :], xi[rows, :] = nr, ni
        return nr, ni

    er, ei = lax.fori_loop(0, SEG, local, (zero, zero))
    power, square, bits = None, (ar, ai), SEG
    while bits:
        if bits & 1:
            power = square if power is None else _cmul(*power, *square)
        square, bits = _cmul(*square, *square), bits >> 1
    pr, pi = power
    row = lax.broadcasted_iota(jnp.int32, ar.shape, 0)
    edge, shift = (N_SEG - 1, N_SEG - 1) if reverse else (0, 1)
    hr, hi = zero, zero
    for _ in range(N_SEG - 1):
        tr_, ti_ = _cmul(pr, pi, hr, hi)
        hr = jnp.where(row == edge, 0.0, pltpu.roll(tr_ + er, shift, 0))
        hi = jnp.where(row == edge, 0.0, pltpu.roll(ti_ + ei, shift, 0))

    def fix(j, carry):
        rows = _seg_rows(SEG - 1 - j if reverse else j)
        xr[rows, :] += carry[0]
        xi[rows, :] += carry[1]
        return _cmul(ar, ai, *carry)

    lax.fori_loop(0, SEG, fix, _cmul(ar, ai, hr, hi))
    return hr, hi


def _split(x):
    hi = x.astype(BF16)
    return hi, (x - hi.astype(F32)).astype(BF16)


def _dot2(xs, ys, mode='nn'):
    d = lambda p, q: lax.dot_general(p, q, _DIMS[mode], preferred_element_type=F32)
    return d(xs[0], ys[0]) + d(xs[1], ys[0])


def _group_masks():
    ns = GQ * SSM_STATE
    rep = (lax.broadcasted_iota(jnp.int32, (SSM_STATE, ns), 1) % SSM_STATE
           == lax.broadcasted_iota(jnp.int32, (SSM_STATE, ns), 0)).astype(BF16)
    own = (lax.broadcasted_iota(jnp.int32, (128, ns), 0) // SSM_GROUP
           == lax.broadcasted_iota(jnp.int32, (128, ns), 1) // SSM_STATE)
    return rep, own


def _block_diag(x, rep, own):
    dot = lambda p: jnp.dot(p, rep, preferred_element_type=F32)
    return tuple(jnp.where(own, dot(part), 0.0).astype(BF16) for part in _split(x))


def _block_rows(full, rep, own):
    masked = _split(jnp.where(own, full, 0.0))
    dot = lambda p: lax.dot_general(p, rep, _DIMS['nt'], preferred_element_type=F32)
    return dot(masked[0]) + dot(masked[1])


def _s5_specs():
    col = pl.BlockSpec((TP, 128), lambda q: (0, q))
    a_spec = pl.BlockSpec((None, 2, N_SEG, GQ * SSM_STATE), lambda q: (q, 0, 0, 0))
    rows = pl.BlockSpec((128, SSM_STATE), lambda q: (q, 0))
    d_spec = pl.BlockSpec((1, 128), lambda q: (0, q))
    return col, a_spec, rows, d_spec


def _s5_fwd(u, a8, bb_re, bb_im, c_re, c_im, dsk):
    def body(u_ref, a_ref, bre_ref, bim_ref, cre_ref, cim_ref, d_ref, y_ref, hre, him):
        u_ = u_ref[...]
        us = _split(u_)
        rep, own = _group_masks()
        hre[...] = _dot2(us, _block_diag(bre_ref[...], rep, own))
        him[...] = _dot2(us, _block_diag(bim_ref[...], rep, own))
        _scan(hre, him, a_ref[0], a_ref[1], reverse=False)
        dot = lambda h, c: lax.dot_general(h[...].astype(BF16), _block_diag(c[...], rep, own)[0], _DIMS['nt'],
                                           preferred_element_type=F32)
        y_ref[...] = dot(hre, cre_ref) - dot(him, cim_ref) + d_ref[...] * u_

    col, a_spec, rows, d_spec = _s5_specs()
    return pl.pallas_call(
        body, name="s5_fwd", grid=(N_GROUPS // GQ,), in_specs=[col, a_spec, rows, rows, rows, rows, d_spec],
        out_specs=col, out_shape=_out_hbm((TP, D_SSM), F32),
        scratch_shapes=[pltpu.VMEM((TP, GQ * SSM_STATE), F32)] * 2,
        compiler_params=_params(1, 40 << 20),
    )(_in_hbm(u), a8, bb_re, bb_im, c_re, c_im, dsk)


def _s5_bwd(u, dy, a8, bb_re, bb_im, c_re, c_im, dsk):
    def body(u_ref, dy_ref, a_ref, bre_ref, bim_ref, cre_ref, cim_ref, d_ref,
             du_ref, dbre_ref, dbim_ref, dcre_ref, dcim_ref, da_ref, dd_ref, hre, him, lre, lim):
        u_, dy_ = u_ref[...], dy_ref[...]
        us, dys = _split(u_), _split(dy_)
        ar, ai = a_ref[0], a_ref[1]
        rep, own = _group_masks()
        bres, bims = _block_diag(bre_ref[...], rep, own), _block_diag(bim_ref[...], rep, own)
        hre[...] = _dot2(us, bres)
        him[...] = _dot2(us, bims)
        h0r, h0i = _scan(hre, him, ar, ai, reverse=False)
        lre[...] = _dot2(dys, _block_diag(cre_ref[...], rep, own))
        lim[...] = -_dot2(dys, _block_diag(cim_ref[...], rep, own))
        _scan(lre, lim, ar, -ai, reverse=True)

        def acc_da(i, carry):
            lr, li = lre[_seg_rows(i), :], lim[_seg_rows(i), :]
            pr, pi = hre[_seg_rows(i - 1), :], him[_seg_rows(i - 1), :]
            return carry[0] + lr * pr + li * pi, carry[1] + li * pr - lr * pi

        lr, li = lre[_seg_rows(0), :], lim[_seg_rows(0), :]
        dar, dai = lax.fori_loop(1, SEG, acc_da, (lr * h0r + li * h0i, li * h0r - lr * h0i))
        da_ref[0:1, :] = jnp.sum(dar, axis=0, keepdims=True)
        da_ref[1:2, :] = jnp.sum(dai, axis=0, keepdims=True)
        dot = lambda p, q, mode: lax.dot_general(p, q, _DIMS[mode], preferred_element_type=F32)
        lr1, li1 = lre[...].astype(BF16), lim[...].astype(BF16)
        du_ref[...] = dy_ * d_ref[...] + dot(lr1, bres[0], 'nt') + dot(li1, bims[0], 'nt')
        dbre_ref[...] = _block_rows(dot(us[0], lr1, 'tn'), rep, own)
        dbim_ref[...] = _block_rows(dot(us[0], li1, 'tn'), rep, own)
        dcre_ref[...] = _block_rows(dot(dys[0], hre[...].astype(BF16), 'tn'), rep, own)
        dcim_ref[...] = -_block_rows(dot(dys[0], him[...].astype(BF16), 'tn'), rep, own)
        dd_ref[...] = jnp.sum(dy_ * u_, axis=0, keepdims=True)

    col, a_spec, rows, d_spec = _s5_specs()
    nq, ns = N_GROUPS // GQ, GQ * SSM_STATE
    per_row = jax.ShapeDtypeStruct((N_GROUPS * SSM_GROUP, SSM_STATE), F32)
    return pl.pallas_call(
        body, name="s5_bwd", grid=(nq,),
        in_specs=[col, col, a_spec, rows, rows, rows, rows, d_spec],
        out_specs=[col, rows, rows, rows, rows, pl.BlockSpec((None, 2, ns), lambda q: (q, 0, 0)), d_spec],
        out_shape=[_out_hbm((TP, D_SSM), F32), per_row, per_row, per_row, per_row,
                   jax.ShapeDtypeStruct((nq, 2, ns), F32), jax.ShapeDtypeStruct((1, D_SSM), F32)],
        scratch_shapes=[pltpu.VMEM((TP, ns), F32)] * 4,
        compiler_params=_params(1, 56 << 20),
    )(_in_hbm(u), _in_hbm(dy), a8, bb_re, bb_im, c_re, c_im, dsk)


def _to_segments(a, cols=None):
    a = jnp.pad(a[:, :cols], ((0, TP - T), (0, 0)))
    return a.reshape(N_SEG, SEG, a.shape[-1]).transpose(1, 0, 2).reshape(TP, a.shape[-1])


def _from_segments(a):
    return a.reshape(SEG, N_SEG, a.shape[-1]).transpose(1, 0, 2).reshape(TP, a.shape[-1])[:T]


def _ff_specs():
    up_spec = pl.BlockSpec((2, None, T, 128), lambda p, cb: (0, p, 0, cb))
    piece = pl.BlockSpec((None, T, 128), lambda p, cb: (p, 0, cb))
    cw_spec = pl.BlockSpec((None, 3, 128), lambda p, cb: (p, 0, cb))
    cb_spec = pl.BlockSpec((None, 1, 128), lambda p, cb: (p, 0, cb))
    return up_spec, piece, cw_spec, cb_spec


def _conv_gate(gate, w, cb, pad_ref):
    pad_ref[0:8, :] = jnp.zeros((8, 128), F32)
    pad_ref[8:8 + T, :] = gate
    g1, g2 = pad_ref[pl.ds(7, T), :], pad_ref[pl.ds(6, T), :]
    return w[0:1, :] * g2 + w[1:2, :] * g1 + w[2:3, :] * gate + cb, g1, g2


def _ff_act(up, cw4, cb4):
    def body(up_ref, w_ref, b_ref, o_ref, pad_ref):
        gc, _, _ = _conv_gate(up_ref[0].astype(F32), w_ref[...], b_ref[...], pad_ref)
        o_ref[...] = (jax.nn.silu(gc) * up_ref[1].astype(F32)).astype(o_ref.dtype)

    up_spec, piece, cw_spec, cb_spec = _ff_specs()
    n_cb = pl.cdiv(FF_PIECE, 128)
    return pl.pallas_call(
        body, name="ff_act", grid=(4, n_cb), in_specs=[up_spec, cw_spec, cb_spec], out_specs=piece,
        out_shape=_out_hbm((4, T, FF_PIECE), BF16),
        scratch_shapes=[pltpu.VMEM((T + 8, 128), F32)],
        compiler_params=_params(2, 24 << 20),
    )(_in_hbm(up.reshape(2, 4, T, FF_PIECE)), cw4, cb4)


def _ff_act_bwd(up, dact, cw4, cb4):
    def body(up_ref, da_ref, w_ref, b_ref, dup_ref, dw_ref, db_ref, pad_ref, pad2_ref):
        gate, val, w = up_ref[0].astype(F32), up_ref[1].astype(F32), w_ref[...]
        gc, g1, g2 = _conv_gate(gate, w, b_ref[...], pad_ref)
        sg = jax.nn.sigmoid(gc)
        da = da_ref[...].astype(F32)
        dup_ref[1] = (da * gc * sg).astype(dup_ref.dtype)
        dgc = da * val * sg * (1.0 + gc * (1.0 - sg))
        db_ref[...] = jnp.sum(dgc, axis=0, keepdims=True)
        dw_ref[0:1, :] = jnp.sum(dgc * g2, axis=0, keepdims=True)
        dw_ref[1:2, :] = jnp.sum(dgc * g1, axis=0, keepdims=True)
        dw_ref[2:3, :] = jnp.sum(dgc * gate, axis=0, keepdims=True)
        pad2_ref[0:T, :] = dgc
        pad2_ref[T:T + 8, :] = jnp.zeros((8, 128), F32)
        d1, d2 = pad2_ref[pl.ds(1, T), :], pad2_ref[pl.ds(2, T), :]
        dup_ref[0] = (w[2:3, :] * dgc + w[1:2, :] * d1 + w[0:1, :] * d2).astype(dup_ref.dtype)

    up_spec, piece, cw_spec, cb_spec = _ff_specs()
    n_cb = pl.cdiv(FF_PIECE, 128)
    dup, dw, db = pl.pallas_call(
        body, name="ff_act_bwd", grid=(4, n_cb), in_specs=[up_spec, piece, cw_spec, cb_spec],
        out_specs=[up_spec, cw_spec, cb_spec],
        out_shape=[_out_hbm((2, 4, T, FF_PIECE), BF16), jax.ShapeDtypeStruct((4, 3, FF_PIECE), F32),
                   jax.ShapeDtypeStruct((4, 1, FF_PIECE), F32)],
        scratch_shapes=[pltpu.VMEM((T + 8, 128), F32)] * 2,
        compiler_params=_params(2, 32 << 20),
    )(_in_hbm(up.reshape(2, 4, T, FF_PIECE)), _in_hbm(dact), cw4, cb4)
    return dup.reshape(8, T, FF_PIECE), dw, db


def _local_step(x, tgt, meta, p, wf, mid_weights, late_weights, send_grads, send_small):
    sm = {}
    tabs = _rope_tables()
    row = lambda a: a.reshape(1, -1)

    h0, xn, r0 = _rowwise("rms_mix", _rms_mix_fn, [(x, D, 0, N_META)], [meta, row(p['mix_norm'])],
                          [(D, F32), (D, BF16), (1, F32)], [])
    z = _mm("in_proj", xn, wf['w_in'], 'nn', TR, 640, D, F32)
    lam_re, lam_im = p['lam_re'].reshape(N_GROUPS, SSM_STATE), p['lam_im'].reshape(N_GROUPS, SSM_STATE)
    log_dt = p['log_dt'].reshape(1, N_GROUPS)
    bt_re = p['b_re'].reshape(N_GROUPS, SSM_STATE, SSM_GROUP).transpose(2, 0, 1)
    bt_im = p['b_im'].reshape(N_GROUPS, SSM_STATE, SSM_GROUP).transpose(2, 0, 1)
    c_re = p['c_re'].reshape(N_GROUPS * SSM_GROUP, SSM_STATE)
    c_im = p['c_im'].reshape(N_GROUPS * SSM_GROUP, SSM_STATE)
    a_re, a_im, bbt_re, bbt_im = _s5_prep(lam_re, lam_im, log_dt, bt_re, bt_im)
    nq, ns = N_GROUPS // GQ, GQ * SSM_STATE
    a8 = jnp.broadcast_to(jnp.stack([a_re.reshape(nq, ns), a_im.reshape(nq, ns)], 1)[:, :, None, :],
                          (nq, 2, N_SEG, ns))
    gcp = lambda t: t.transpose(1, 0, 2).reshape(N_GROUPS * SSM_GROUP, SSM_STATE)
    bb_re, bb_im = gcp(bbt_re), gcp(bbt_im)
    dsk = row(p['d_skip'])
    u_seg = _to_segments(z, D_SSM)
    y_ssm = _from_segments(_s5_fwd(u_seg, a8, bb_re, bb_im, c_re, c_im, dsk))
    wf = dict(wf, **mid_weights(y_ssm))
    g, = _rowwise("gelu", _gelu_fn, [(y_ssm, D_SSM, 0)], [], [(D_SSM, BF16)], [])
    gl = _mm("glu_proj", g, wf['w_glu'], 'nn', TR, D_SSM, D_SSM, F32)
    yan, ra = _rowwise("glu_norm", _glu_fwd_fn, [(y_ssm, D_SSM, 0), (gl, D_SSM, 0)],
                       [row(p['b_glu']), row(p['out_norm_ssm'])], [(D_SSM, BF16), (1, F32)], [])
    qn, rq, kvn, rkv = _rowwise(
        "rms_q_kv", lambda row0, zq, zkv, gq, gkv: _rms_fwd_fn(row0, zq, gq) + _rms_fwd_fn(row0, zkv, gkv),
        [(z, Q_LORA, D_SSM // Q_LORA), (z, KV_LORA, (D_SSM + Q_LORA) // KV_LORA)],
        [row(p['q_a_norm']), row(p['kv_a_norm'])], [(Q_LORA, BF16), (1, F32), (KV_LORA, BF16), (1, F32)], [])
    q = _mm("q_proj", qn, wf['w_q_b'], 'nn', T, HEAD_PAD, Q_LORA, F32, stack='b_out')
    kv = _mm("kv_proj", kvn, wf['w_kv_b'], 'nn', T, HEAD_PAD, KV_LORA, F32, stack='b_out')
    yb = _attn_fwd(q, kv, z, tabs)
    ybn, rb = _rowwise("rms_attn", _rms_fwd_fn, [(yb, D_SSM, 0)], [row(p['out_norm_attn'])],
                       [(D_SSM, BF16), (1, F32)], [])
    y = jnp.concatenate([yan, ybn], axis=1)
    wf = dict(wf)
    wf['w_out'], token, up_weights = late_weights(y)
    h1 = _mm("out_proj", y, wf['w_out'], 'nn', TR, 1024, D, F32, res=h0)
    xn2, r1 = _rowwise("rms_ffn", _rms_fwd_fn, [(h1, D, 0)], [row(p['ffn_norm']) + token[0:1, 0:1]],
                       [(D, BF16), (1, F32)], [])
    wf['w_up'], token, down_weights = up_weights(xn2)
    up = _mm("up_proj", xn2, wf['w_up'], 'nn', TR, FF_PIECE, D, BF16, stack='b_out')
    cw4 = wf['conv_w'].reshape(3, 4, FF_PIECE).transpose(1, 0, 2)
    cb4 = p['conv_b'].reshape(4, 1, FF_PIECE)
    act = _ff_act(up, cw4, cb4 + token[0:1, 0:1])
    wf['w_down'] = down_weights(act)
    wd4 = wf['w_down'].reshape(4, FF_PIECE, D)
    h2 = _mm("down_proj", act, wd4, 'nn', TR, 1024, FF_PIECE, F32, stack='ab_red', res=h1)
    dh2, dh2_b, sm['final_norm'], loss = _rowwise("loss", _loss_fn, [(h2, D, 0), (tgt, D, 0, N_META)],
                                                  [row(p['final_norm'])], [(D, F32), (D, BF16)], [D, 128])

    big = {}
    dact = _mm("down_bwd_x", dh2_b, wd4, 'nt', TR, FF_PIECE, D, BF16, stack='b_out')
    g_down = _mm("down_bwd_w", act, dh2_b, 'tn', FF_PIECE, 512, T, BF16, stack='a_out').reshape(
        N_DEV, D_FF // N_DEV, D)
    token = send_grads(['w_down'], [g_down], 'cores')
    dup, dcw4, sm['conv_b'] = _ff_act_bwd(up, dact, cw4, cb4 + token[0:1, 0:1])
    dxn2 = _mm("up_bwd_x", dup, wf['w_up'], 'nt', TR, D, FF_PIECE, F32, stack='ab_red')
    g_up = _mm("up_bwd_w", dup, xn2, 'tn', FF_PIECE, 1024, T, BF16, stack='a_out')
    token = send_grads(['w_up'], [g_up], 'cores') + send_grads(['w_down'], dxn2, 'chips')
    dh1, dh1_b, sm['ffn_norm'] = _rowwise("rms_ffn_bwd", _rms_bwd_res_twice_fn,
                                          [(dxn2, D, 0), (h1, D, 0), (r1, 1, 0), (dh2, D, 0)],
                                          [row(p['ffn_norm']) + token[0:1, 0:1]], [(D, F32), (D, BF16)], [D])
    dy = _mm("out_bwd_x", dh1_b, wf['w_out'], 'nt', TR, 1024, D, F32)
    g_out = _mm("out_bwd_w", y, dh1_b, 'tn', 1024, 1024, T, BF16).reshape(N_DEV, D // N_DEV, D)
    dgl, dgd, sm['out_norm_ssm'], sm['b_glu'] = _rowwise(
        "glu_bwd", _glu_bwd_fn, [(dy, D_SSM, 0), (y_ssm, D_SSM, 0), (gl, D_SSM, 0), (ra, 1, 0)],
        [row(p['b_glu']), row(p['out_norm_ssm'])], [(D_SSM, BF16), (D_SSM, F32)], [D_SSM, D_SSM])
    dg = _mm("glu_bwd_x", dgl, wf['w_glu'], 'nt', TR, D_SSM, D_SSM, F32, res=dgd)
    g_glu = _mm("glu_bwd_w", g, dgl, 'tn', D_SSM, D_SSM, T, BF16).reshape(N_DEV, D_SSM // N_DEV, D_SSM)
    dy_ssm, = _rowwise("gelu_bwd", _gelu_bwd_fn, [(dg, D_SSM, 0), (y_ssm, D_SSM, 0)], [], [(D_SSM, F32)], [])
    token = send_grads(['w_up'], g_glu, 'chips')
    dyb, sm['out_norm_attn'] = _rowwise("rms_attn_bwd", _rms_bwd_fn, [(dy, D_SSM, 1), (yb, D_SSM, 0), (rb, 1, 0)],
                                        [row(p['out_norm_attn']) + token[0:1, 0:1]], [(D_SSM, F32)], [D_SSM])
    dq, dkv, dkpe = _attn_bwd(q, kv, z, tabs, dyb)
    g_q = _mm("q_bwd_w", qn, dq, 'tn', Q_LORA, HEAD_PAD, T, BF16, stack='b_out')
    dqn = _mm("q_bwd_x", dq, wf['w_q_b'], 'nt', T, Q_LORA, HEAD_PAD, F32, stack='ab_red')
    g_kv = _mm("kv_bwd_w", kvn, dkv, 'tn', KV_LORA, HEAD_PAD, T, BF16, stack='b_out')
    dkvn = _mm("kv_bwd_x", dkv, wf['w_kv_b'], 'nt', T, KV_LORA, HEAD_PAD, F32, stack='ab_red')
    token = send_grads(['w_out', 'w_glu', 'w_q_b', 'w_kv_b'], [g_out, g_glu, g_q, g_kv], 'both')
    def both_bwd(row0, dq_, zq, rq_, dkv_, zkv, rkv_, gq, gkv):
        (dzq, dgq), (dzkv, dgkv) = _rms_bwd(dq_, zq, rq_, gq), _rms_bwd(dkv_, zkv, rkv_, gkv)
        return dzq, dzkv, dgq, dgkv

    dq_a, dkv_a, sm['q_a_norm'], sm['kv_a_norm'] = _rowwise(
        "rms_q_kv_bwd", both_bwd,
        [(dqn, Q_LORA, 0), (z, Q_LORA, D_SSM // Q_LORA), (rq, 1, 0),
         (dkvn, KV_LORA, 0), (z, KV_LORA, (D_SSM + Q_LORA) // KV_LORA), (rkv, 1, 0)],
        [row(p['q_a_norm']), row(p['kv_a_norm'])], [(Q_LORA, BF16), (KV_LORA, BF16)], [Q_LORA, KV_LORA])
    dk_pe = _kpe_bwd(dkpe, tabs)
    du_seg, dbre, dbim, dcre, dcim, da, sm['d_skip'] = _s5_bwd(
        u_seg, _to_segments(dy_ssm), a8, bb_re, bb_im, c_re, c_im, dsk + token[0:1, 0:1])
    du = _from_segments(du_seg).astype(BF16)
    cgp = lambda d: d.reshape(N_GROUPS, SSM_GROUP, SSM_STATE).transpose(1, 0, 2)
    dlam_re, dlam_im, dlog_dt, dbt_re, dbt_im = _s5_prep_bwd(
        lam_re, lam_im, log_dt, bt_re, bt_im, da[:, 0, :].reshape(N_GROUPS, SSM_STATE),
        da[:, 1, :].reshape(N_GROUPS, SSM_STATE), cgp(dbre), cgp(dbim))
    sm['lam_re'], sm['lam_im'], sm['log_dt'] = dlam_re, dlam_im, dlog_dt
    sm['b_re'], sm['b_im'] = gcp(dbt_re), gcp(dbt_im)
    sm['c_re'], sm['c_im'] = dcre, dcim
    token = send_small(sm, dcw4, loss)
    dz = jnp.concatenate([du, dq_a, dkv_a, dk_pe], axis=1)
    dxn = _mm("in_bwd_x", dz, wf['w_in'], 'nt', TR, 1024, D_IN_PAD, F32)
    big['w_in'] = _mm("in_bwd_w", xn, dz, 'tn', 1024, 640, T, BF16).reshape(N_DEV, D // N_DEV, D_IN_PAD)
    dh0, d_mix_norm = _rowwise("rms_mix_bwd", _rms_bwd_res_fn,
                               [(dxn, D, 0), (h0, D, 0), (r0, 1, 0), (dh1, D, 0)],
                               [row(p['mix_norm']) + token[0:1, 0:1]], [(D, F32)], [D])
    return dh0[N_META:], dh0[:N_META], d_mix_norm, big


def _place():
    x, y, c = lax.axis_index("x"), lax.axis_index("y"), lax.axis_index("c")
    return x, y, c, [(1 - x, y), (x, 1 - y), (1 - x, 1 - y)]


_HBM = pl.BlockSpec(memory_space=pltpu.HBM)


def _all_gather(name, arrs, chip_blocks=()):
    n, nc = len(arrs), len(chip_blocks)

    def body(*refs):
        ins, cins, outs, couts = refs[:n], refs[n:n + nc], refs[n + nc:2 * n + nc], refs[2 * n + nc:2 * (n + nc)]
        send, recv, loc, csend, crecv = refs[2 * (n + nc):]
        x, y, c, chips = _place()
        me, sib = (x, y, c), (x, y, 1 - c)

        def cp(a, k, blk, to, src=None):
            dst = outs[a].at[4 * blk[0] + 2 * blk[1] + blk[2]]
            return pltpu.make_async_remote_copy(src_ref=dst if src is None else src, dst_ref=dst,
                                                send_sem=send.at[a, k], recv_sem=recv.at[a, k],
                                                device_id=to, device_id_type=MESH)

        between_chips = [pltpu.make_async_remote_copy(src_ref=cins[a].at[2 * ch[0] + ch[1]], dst_ref=couts[a].at[j],
                                                      send_sem=csend.at[a, j], recv_sem=crecv.at[a, j],
                                                      device_id=(*ch, c), device_id_type=MESH)
                         for a in range(nc) for j, ch in enumerate(chips)]
        mine = [pltpu.make_async_copy(ins[a], outs[a].at[4 * x + 2 * y + c], loc.at[a]) for a in range(n)]
        first = []
        for a in range(n):
            first.append(cp(a, 0, me, sib, src=ins[a]))
            first += [cp(a, 1 + j, me, (*ch, c), src=ins[a]) for j, ch in enumerate(chips)]
        for f in mine + first + between_chips:
            f.start()
        passed = []
        for a in range(n):
            for j, ch in enumerate(chips):
                cp(a, 1 + j, (*ch, c), me).wait_recv()
                passed.append(cp(a, 4 + j, (*ch, c), sib))
                passed[-1].start()
        for a in range(n):
            cp(a, 0, sib, me).wait_recv()
            for j, ch in enumerate(chips):
                cp(a, 4 + j, (*ch, 1 - c), me).wait_recv()
        for f in first + passed:
            f.wait_send()
        for m in mine + between_chips:
            m.wait()

    out = pl.pallas_call(
        body, name=name, in_specs=[_HBM] * (n + nc), out_specs=[_HBM] * (n + nc),
        out_shape=[jax.ShapeDtypeStruct((N_DEV,) + a.shape, a.dtype) for a in arrs]
        + [jax.ShapeDtypeStruct((3,) + a.shape[1:], a.dtype) for a in chip_blocks],
        scratch_shapes=[pltpu.SemaphoreType.DMA((n, 7)), pltpu.SemaphoreType.DMA((n, 7)),
                        pltpu.SemaphoreType.DMA((n,)), pltpu.SemaphoreType.DMA((max(nc, 1), 3)),
                        pltpu.SemaphoreType.DMA((max(nc, 1), 3))],
    )(*arrs, *chip_blocks)
    return (out[:n], out[n:]) if nc else out


def _exchange_cores(name, arrs):
    n = len(arrs)

    def body(*refs):
        ins, outs = refs[:n], refs[n:2 * n]
        send, recv = refs[2 * n:]
        x, y, c, _ = _place()
        remote = [pltpu.make_async_remote_copy(src_ref=ins[a].at[2 * k + 1 - c], dst_ref=outs[a].at[k],
                                               send_sem=send.at[a, k], recv_sem=recv.at[a, k],
                                               device_id=(x, y, 1 - c), device_id_type=MESH)
                  for a in range(n) for k in range(4)]
        for d in remote:
            d.start()
        for d in remote:
            d.wait()

    return pl.pallas_call(
        body, name=name, in_specs=[_HBM] * n, out_specs=[_HBM] * n,
        out_shape=[jax.ShapeDtypeStruct((4,) + a.shape[1:], a.dtype) for a in arrs],
        scratch_shapes=[pltpu.SemaphoreType.DMA((n, 4))] * 2,
    )(*arrs)


_SEM = pl.BlockSpec(memory_space=pltpu.SEMAPHORE)
_EFFECT = pltpu.SideEffectType.DATAFLOW_SIDE_EFFECTING


def _split_exchange(name, srcs, land_shapes, copies_of, n_cp, after=None, all_of=None):
    n = len(srcs)
    in_place = land_shapes is None
    lands = [] if in_place else [pltpu.with_memory_space_constraint(lax.empty(s, a.dtype), pltpu.HBM)
                                 for s, a in zip(land_shapes, srcs)]
    nb = n + len(lands)

    per = 1 if all_of else n_cp
    n_sem = n * per

    def descriptors(refs, send, recv, mirror):
        src_refs, land_refs = refs[:n], refs[nb - n:nb]
        x, y, c, chips = _place()
        out = []
        for a in range(n):
            if mirror and all_of:
                whole = all_of(a, land_refs[a])
                out.append(pltpu.make_async_remote_copy(src_ref=whole, dst_ref=whole, send_sem=send[a],
                                                        recv_sem=recv[a], device_id=(x, y, 1 - c),
                                                        device_id_type=MESH))
                continue
            for k, (src, dst, to, back) in enumerate(copies_of(a, src_refs[a], land_refs[a], x, y, c, chips)):
                out.append(pltpu.make_async_remote_copy(src_ref=src, dst_ref=back if mirror else dst,
                                                        send_sem=send[a * per + k % per],
                                                        recv_sem=recv[a * per + k % per],
                                                        device_id=to, device_id_type=MESH))
        return out

    def start_body(*refs):
        first = nb + (after is not None)
        sems = refs[first:first + 2 * n_sem]
        for d in descriptors(refs, sems[:n_sem], sems[n_sem:], False):
            d.start()
        refs[-1][...] = jnp.zeros((8, 128), F32)

    hbm_like = lambda arrs: [pltpu.HBM(a.shape, a.dtype) for a in arrs]
    order = [] if after is None else [after]
    res = pl.pallas_call(
        start_body, name=name + "_start", in_specs=[_HBM] * nb + [pl.BlockSpec(memory_space=pl.ANY)] * len(order),
        out_specs=[_SEM] * (2 * n_sem) + [_HBM] * nb + [pl.BlockSpec(memory_space=pltpu.VMEM)],
        out_shape=[pltpu.SemaphoreType.DMA(())] * (2 * n_sem)
        + hbm_like(srcs) + hbm_like(lands) + [jax.ShapeDtypeStruct((8, 128), F32)],
        input_output_aliases={i: 2 * n_sem + i for i in range(nb)},
        compiler_params=pltpu.CompilerParams(has_side_effects=_EFFECT),
    )(*srcs, *lands, *order)
    sems, thru, token = res[:2 * n_sem], res[2 * n_sem:2 * n_sem + nb], res[-1]

    def wait(after):
        def wait_body(*refs):
            s = refs[nb:nb + 2 * n_sem]
            for d in descriptors(refs, s[:n_sem], s[n_sem:], True):
                d.wait_send()
                d.wait_recv()

        out = pl.pallas_call(
            wait_body, name=name + "_wait",
            in_specs=[_HBM] * nb + [_SEM] * (2 * n_sem) + [pl.BlockSpec(memory_space=pl.ANY)],
            out_specs=[_HBM] * nb, out_shape=hbm_like(srcs) + hbm_like(lands),
            input_output_aliases={i: i for i in range(nb)},
            compiler_params=pltpu.CompilerParams(has_side_effects=_EFFECT),
        )(*thru, *sems, after)
        return out[:n], out[nb - n:]

    return token, wait


def _gather_copies(a, src, land, x, y, c, chips):
    peers = [(x, y, 1 - c)] + [(*ch, c) for ch in chips]
    me = 4 * x + 2 * y + c
    return [(src, land.at[me], to, land.at[4 * to[0] + 2 * to[1] + to[2]]) for to in peers]


def _to_all_copies(a, src, land, x, y, c, chips):
    peers = [(x, y, 1 - c)] + [(*ch, cc) for ch in chips for cc in (c, 1 - c)]
    me = 4 * x + 2 * y + c
    return [(src, land.at[me], to, land.at[4 * to[0] + 2 * to[1] + to[2]]) for to in peers]


def _cores_copies(a, src, land, x, y, c, chips):
    return [(src.at[2 * k + 1 - c], land.at[k], (x, y, 1 - c), land.at[k]) for k in range(4)]


def _forward_copies(a, src, land, x, y, c, chips):
    slot = lambda ch, core: 4 * ch[0] + 2 * ch[1] + core
    return [(land.at[slot(ch, c)], land.at[slot(ch, c)], (x, y, 1 - c), land.at[slot(ch, 1 - c)]) for ch in chips]


def _chips_copies(a, src, land, x, y, c, chips):
    return [(src.at[2 * ch[0] + ch[1]], land.at[j], (*ch, c), land.at[j]) for j, ch in enumerate(chips)]


def _gather_forward(name, arrs):
    n = len(arrs)

    def body(*refs):
        ins, outs = refs[:n], refs[n:2 * n]
        send, recv = refs[2 * n:]
        x, y, c, chips = _place()
        sends, recvs = [], []
        for a in range(n):
            for j, ch in enumerate(chips):
                mine, theirs = 4 * ch[0] + 2 * ch[1] + c, 4 * ch[0] + 2 * ch[1] + 1 - c
                sends.append(pltpu.make_async_remote_copy(src_ref=ins[a].at[mine], dst_ref=outs[a].at[mine],
                                                          send_sem=send.at[a, j], recv_sem=recv.at[a, j],
                                                          device_id=(x, y, 1 - c), device_id_type=MESH))
                recvs.append(pltpu.make_async_remote_copy(src_ref=ins[a].at[theirs], dst_ref=outs[a].at[theirs],
                                                          send_sem=send.at[a, j], recv_sem=recv.at[a, j],
                                                          device_id=(x, y, 1 - c), device_id_type=MESH))
        for d in sends:
            d.start()
        for s, r in zip(sends, recvs):
            s.wait_send()
            r.wait_recv()

    return pl.pallas_call(
        body, name=name, in_specs=[_HBM] * n, out_specs=[_HBM] * n,
        out_shape=[jax.ShapeDtypeStruct(a.shape, a.dtype) for a in arrs],
        input_output_aliases={i: i for i in range(n)},
        scratch_shapes=[pltpu.SemaphoreType.DMA((n, 3))] * 2,
    )(*arrs)


def _blocks_of(r, c):
    if r * c * 4 <= (2 << 20):
        return r, c
    if r % 128 == 0:
        return 128, c
    return r, 256


def _pair_sum(name, own, got, core):
    _, r, c = got.shape
    rb, cb = _blocks_of(r, c)

    def body(s_ref, a_ref, b_ref, o_ref):
        o_ref[...] = (a_ref[...].astype(F32) + b_ref[...].astype(F32)).astype(o_ref.dtype)

    return pl.pallas_call(
        body, name=name, out_shape=_out_hbm((4, r, c), got.dtype),
        grid_spec=pltpu.PrefetchScalarGridSpec(
            num_scalar_prefetch=1, grid=(4, r // rb, c // cb),
            in_specs=[pl.BlockSpec((None, rb, cb), lambda k, i, j, s: (2 * k + s[0], i, j)),
                      pl.BlockSpec((None, rb, cb), lambda k, i, j, s: (k, i, j))],
            out_specs=pl.BlockSpec((None, rb, cb), lambda k, i, j, s: (k, i, j))),
        compiler_params=_params(3, 24 << 20),
    )(core, _in_hbm(own), _in_hbm(got))


def _adamw_math(w, g, m, v):
    m = B1 * m + (1.0 - B1) * g
    v = B2 * v + (1.0 - B2) * (g * g)
    m_hat = m / (1.0 - B1 ** STEP)
    v_hat = v / (1.0 - B2 ** STEP)
    return -LR * (m_hat / (jnp.sqrt(v_hat) + ADAM_EPS) + WD * w), m, v


def _adamw_big(name, own, got, chip, w, m, v, behind):
    r, c = w.shape
    rb, cb = _blocks_of(r, c)

    def body(s_ref, o_ref, p_ref, w_ref, m_ref, v_ref, behind_ref, g_ref, d_ref, nm_ref, nv_ref):
        g = o_ref[...].astype(F32)
        for k in range(3):
            g = g + p_ref[k].astype(F32)
        g_ref[...] = g
        d_ref[...], nm_ref[...], nv_ref[...] = _adamw_math(w_ref[...], g, m_ref[...], v_ref[...])

    blk = pl.BlockSpec((rb, cb), lambda i, j, s: (i, j))
    return pl.pallas_call(
        body, name=name, out_shape=[jax.ShapeDtypeStruct((r, c), F32)] * 4,
        grid_spec=pltpu.PrefetchScalarGridSpec(
            num_scalar_prefetch=1, grid=(r // rb, c // cb),
            in_specs=[pl.BlockSpec((None, rb, cb), lambda i, j, s: (s[0], i, j)),
                      pl.BlockSpec((3, rb, cb), lambda i, j, s: (0, i, j)), blk, blk, blk,
                      pl.BlockSpec(memory_space=pl.ANY)],
            out_specs=[blk] * 4),
        compiler_params=_params(2, 40 << 20),
    )(chip, *map(_in_hbm, (own, got, w, m, v)), behind)


def _adamw_multi(name, items, sums=(), row_blocks=1, own=(), dev=None):
    n, ns = len(items), len(sums)
    n_own = len(own)
    assert n_own in (0, n + ns)

    def body(*refs):
        dev_ref = refs[0] if n_own else None
        refs = refs[1:] if n_own else refs
        owns, refs = refs[:n_own], refs[n_own:]
        ins, outs = refs[:4 * n + ns], refs[4 * n + ns:]

        def total(ref, t):
            g = None
            for d in range(ref.shape[0]):
                part = jnp.where(dev_ref[0] == d, owns[t][...], ref[d]) if n_own else ref[d]
                g = part.astype(F32) if g is None else g + part.astype(F32)
            return g

        for t in range(n):
            p_ref, w_ref, m_ref, v_ref = ins[4 * t:4 * t + 4]
            g = total(p_ref, t)
            outs[4 * t][...] = g
            outs[4 * t + 1][...], outs[4 * t + 2][...], outs[4 * t + 3][...] = _adamw_math(
                w_ref[...], g, m_ref[...], v_ref[...])
        for t in range(ns):
            outs[4 * n + t][...] = total(ins[4 * n + t], n + t)

    def spec(shape, lead):
        blk = (shape[0] // row_blocks,) + tuple(shape[1:])
        nd = len(shape)
        if lead:
            return pl.BlockSpec((lead,) + blk, lambda i: (0, i) + (0,) * (nd - 1))
        return pl.BlockSpec(blk, lambda i: (i,) + (0,) * (nd - 1))

    operands, in_specs, out_specs, out_shape = [], [], [], []
    if n_own:
        operands += [dev] + list(own)
        in_specs += [pl.BlockSpec(memory_space=pltpu.SMEM)] + [spec(o.shape, 0) for o in own]
    for parts, w, m, v in items:
        assert parts.shape[1:] == w.shape == m.shape == v.shape, (name, parts.shape, w.shape)
        operands += [parts, w, m, v]
        in_specs += [spec(w.shape, parts.shape[0])] + [spec(w.shape, 0)] * 3
        out_specs += [spec(w.shape, 0)] * 4
        out_shape += [jax.ShapeDtypeStruct(w.shape, F32)] * 4
    for parts in sums:
        operands.append(parts)
        in_specs.append(spec(parts.shape[1:], parts.shape[0]))
        out_specs.append(spec(parts.shape[1:], False))
        out_shape.append(jax.ShapeDtypeStruct(parts.shape[1:], F32))
    return pl.pallas_call(
        body, name=name, grid=(row_blocks,), in_specs=in_specs, out_specs=out_specs, out_shape=out_shape,
        compiler_params=_params(1, 56 << 20),
    )(*operands)


def kernel(x, meta_tokens, mix_norm, w_in, lam_re, lam_im, log_dt, b_re, b_im, c_re, c_im, d_skip, w_glu, b_glu, q_a_norm, w_q_b, kv_a_norm, w_kv_b, out_norm_ssm, out_norm_attn, w_out, ffn_norm, w_up, conv_w, conv_b, w_down, final_norm, loss_target, m_meta_tokens, m_mix_norm, m_w_in, m_lam_re, m_lam_im, m_log_dt, m_b_re, m_b_im, m_c_re, m_c_im, m_d_skip, m_w_glu, m_b_glu, m_q_a_norm, m_w_q_b, m_kv_a_norm, m_w_kv_b, m_out_norm_ssm, m_out_norm_attn, m_w_out, m_ffn_norm, m_w_up, m_conv_w, m_conv_b, m_w_down, m_final_norm, v_meta_tokens, v_mix_norm, v_w_in, v_lam_re, v_lam_im, v_log_dt, v_b_re, v_b_im, v_c_re, v_c_im, v_d_skip, v_w_glu, v_b_glu, v_q_a_norm, v_w_q_b, v_kv_a_norm, v_w_kv_b, v_out_norm_ssm, v_out_norm_attn, v_w_out, v_ffn_norm, v_w_up, v_conv_w, v_conv_b, v_w_down, v_final_norm):
    given = dict(locals())
    w = {n: given[n] for n in WEIGHTS}
    m = {n: given['m_' + n] for n in WEIGHTS}
    v = {n: given['v_' + n] for n in WEIGHTS}
    dev = 4 * lax.axis_index("x") + 2 * lax.axis_index("y") + lax.axis_index("c")

    shard = {
        'w_in': jnp.pad(w_in[0], ((0, 0), (0, D_IN_PAD - D_IN))),
        'w_glu': w_glu[0],
        'w_q_b': jnp.pad(w_q_b[0], ((0, 0), (0, HEAD_PAD - QK_NOPE - QK_ROPE))),
        'w_kv_b': w_kv_b[0],
        'w_out': w_out[0],
        'w_up': w_up[0],
        'w_down': w_down[0],
    }
    core = lax.axis_index("c").astype(jnp.int32).reshape(1)
    chip = (2 * lax.axis_index("x") + lax.axis_index("y")).astype(jnp.int32).reshape(1)
    mid, late = ['w_q_b', 'w_kv_b', 'w_glu'], ['w_out', 'w_up', 'w_down']
    shard = {n: a.astype(BF16) for n, a in shard.items()}

    gathered = _all_gather("gather_early", [shard['w_in'], meta_tokens, jnp.pad(conv_w[0], ((0, 5), (0, 0)))])
    wf = {'w_in': gathered[0].reshape(-1, D_IN_PAD)}
    meta = gathered[1].transpose(1, 0, 2).reshape(N_META, D)
    wf['conv_w'] = gathered[2][:, :3].transpose(1, 0, 2).reshape(3, D_FF)
    mid_token, wait_mid = _split_exchange("gather_mid", [shard[n] for n in mid],
                                          [(N_DEV,) + shard[n].shape for n in mid], _gather_copies, 4,
                                          after=gathered[0])
    gather_token, late_waits = mid_token, {}
    for n in late:
        gather_token, late_waits[n] = _split_exchange("gather_" + n, [shard[n]], [(N_DEV,) + shard[n].shape],
                                                      _gather_copies, 4, after=gather_token)

    def with_mine(mine, landed):
        return [lax.dynamic_update_slice(a, s[None], (dev, 0, 0)) for s, a in zip(mine, landed)]

    def mid_weights(after):
        full = _gather_forward("gather_forward_mid", with_mine(*wait_mid(after)))
        return {'w_q_b': full[0], 'w_kv_b': full[1], 'w_glu': full[2].reshape(-1, D_SSM)}

    def late_weights(after):
        full_out, = _gather_forward("gather_forward_w_out", with_mine(*late_waits['w_out'](after)))
        token, wait_up = _split_exchange("gather_forward_w_up", with_mine(*late_waits['w_up'](after)), None,
                                         _forward_copies, 3, after=full_out)

        def up_weights(after):
            full_up, = wait_up(after)[0]
            token, wait_down = _split_exchange("gather_forward_w_down", with_mine(*late_waits['w_down'](after)),
                                               None, _forward_copies, 3, after=full_up)
            return full_up, token, lambda after: wait_down(after)[0][0].reshape(-1, D)

        return full_out.reshape(-1, D), token, up_weights

    pending, between_cores = [], {}

    def send_grads(names, arg, stage):
        if stage == 'cores':
            token, between_cores[names[0]] = _split_exchange(
                "reduce_cores_" + names[0], arg, [(4,) + a.shape[1:] for a in arg], _cores_copies, 4)
            return token
        mine, got = between_cores.pop(names[0])(arg) if stage == 'chips' else (
            arg, _exchange_cores("reduce_cores_" + names[0], arg))
        sums = [_pair_sum("sum_cores_" + n, a, g, core) for n, a, g in zip(names, mine, got)]
        token, wait = _split_exchange("reduce_chips_" + names[0], sums, [(3,) + s.shape[1:] for s in sums],
                                      _chips_copies, 3)
        pending.append((names, wait))
        return token

    small_sent = []
    s5_bc = ['b_re', 'b_im', 'c_re', 'c_im']

    def send_small(sm, dcw4, loss_part):
        names = [n for n in SMALL if n in sm]
        arrs = [sm[n].reshape(-1, 128).astype(BF16) if n in s5_bc else sm[n] for n in names] + [dcw4, loss_part]
        token, wait = _split_exchange("gather_small", arrs, [(N_DEV,) + a.shape for a in arrs], _to_all_copies, 7,
                                      all_of=lambda a, land: land.at[pl.ds(0, N_DEV - 1)])
        small_sent.append((names, wait))
        return token

    small = {n: w[n] for n in SMALL}
    small['mix_norm'] = mix_norm + gather_token[0:1, 0:1]
    grad_x, dmeta, d_mix_norm, big = _local_step(x[0], loss_target[0], meta, small, wf, mid_weights, late_weights,
                                                  send_grads, send_small)
    grads, deltas, new_m, new_v = {}, {}, {}, {}

    def keep(n, outs):
        grads[n], deltas[n], new_m[n], new_v[n] = (o.reshape(w[n].shape) for o in outs)

    rest = list(big)
    from_core = _exchange_cores("reduce_cores", [big[n] for n in rest])
    chip_sums = [_pair_sum("sum_cores_" + n, big[n], got, core) for n, got in zip(rest, from_core)]
    last = chip_sums + [d_mix_norm, dmeta]
    n_big = len(chip_sums)
    last_token, wait_last = _split_exchange(
        "reduce_last", last, [(3,) + s.shape[1:] for s in chip_sums] + [(N_DEV,) + a.shape for a in last[n_big:]],
        lambda a, *at: _chips_copies(a, *at) if a < n_big else _to_all_copies(a, *at), 7,
        all_of=lambda a, land: land if a < n_big else land.at[pl.ds(0, N_DEV - 1)])

    def update(n, own, got):
        if n == 'w_up':
            outs = _adamw_big("adamw_" + n, own, got, chip, w[n][0].T, m[n][0].T, v[n][0].T, last_token)
            return keep(n, [o.T for o in outs])
        cols = w[n].shape[2]
        keep(n, _adamw_big("adamw_" + n, own[:, :, :cols], got[:, :, :cols], chip, w[n][0], m[n][0], v[n][0],
                           last_token))

    for names, wait in pending:
        for n, own, got in zip(names, *wait(chip_sums[0])):
            update(n, own, got)
    updated = [new_v[n].reshape(-1)[:1] for names, _ in pending for n in names]
    sent, landed = wait_last(functools.reduce(jnp.add, updated))
    for n, own, got in zip(rest, sent, landed):
        update(n, own, got)

    device = dev.astype(jnp.int32).reshape(1)
    outs = _adamw_multi("adamw_last", [(landed[n_big], mix_norm, m['mix_norm'], v['mix_norm'])],
                        sums=[landed[n_big + 1]], own=sent[n_big:], dev=device)
    keep('mix_norm', outs[:4])
    g_meta = outs[4]
    names, wait = small_sent[0]
    mine, landed = wait(grads['mix_norm'])
    parts, own = dict(zip(names + ['conv_w', 'loss'], landed)), dict(zip(names + ['conv_w', 'loss'], mine))
    wide = [n for n in names if n not in s5_bc]
    summed = ['conv_w', 'loss'] + s5_bc
    outs = _adamw_multi("adamw_small",
                        [(parts[n],) + tuple(t[n].reshape(own[n].shape) for t in (w, m, v)) for n in wide],
                        sums=[parts[n] for n in summed], own=[own[n] for n in wide + summed], dev=device)
    for i, n in enumerate(wide):
        keep(n, outs[4 * i:4 * i + 4])
    g_cw4, loss = outs[-6:-4]
    rows_gc = lambda a, n: (a.reshape(N_GROUPS, SSM_STATE, SSM_GROUP).transpose(0, 2, 1) if n[0] == 'b' else a
                            ).reshape(N_GROUPS * SSM_GROUP, SSM_STATE)
    outs = _adamw_multi("adamw_s5_bc", [(g.reshape(1, N_GROUPS * SSM_GROUP, SSM_STATE),)
                                        + tuple(rows_gc(t[n], n) for t in (w, m, v))
                                        for n, g in zip(s5_bc, outs[-4:])])
    for i, n in enumerate(s5_bc):
        back = outs[4 * i:4 * i + 4]
        if n[0] == 'b':
            back = [o.reshape(N_GROUPS, SSM_GROUP, SSM_STATE).transpose(0, 2, 1) for o in back]
        keep(n, back)

    g_meta = lax.dynamic_slice(g_meta, (0, dev * (D // N_DEV)), (N_META, D // N_DEV))
    g_conv = lax.dynamic_slice(g_cw4.transpose(1, 0, 2).reshape(3, D_FF), (0, dev * (D_FF // N_DEV)),
                               (3, D_FF // N_DEV))
    rows8 = lambda a: jnp.pad(a.reshape(3, D_FF // N_DEV), ((0, 5), (0, 0)))
    outs = _adamw_multi("adamw_cols", [(g_meta[None], meta_tokens, m['meta_tokens'], v['meta_tokens']),
                                       (rows8(g_conv)[None], rows8(conv_w), rows8(m['conv_w']), rows8(v['conv_w']))])
    keep('meta_tokens', outs[:4])
    keep('conv_w', [o[:3] for o in outs[4:]])

    return (loss[0, 0], grad_x[None], *[grads[n] for n in WEIGHTS], *[deltas[n] for n in WEIGHTS],
            *[new_m[n] for n in WEIGHTS], *[new_v[n] for n in WEIGHTS])
```

```python
import functools
import math

import jax
import jax.numpy as jnp
from jax import lax
from jax.experimental import pallas as pl
from jax.experimental.pallas import tpu as pltpu

F32, BF16 = jnp.float32, jnp.bfloat16

N_DEV = 8
N_META, SEQ, D = 16, 2048, 2048
T = N_META + SEQ
TR = 688
EPS = 1e-6
D_SSM, Q_LORA, KV_LORA, QK_ROPE = 1024, 512, 256, 64
D_IN = D_SSM + Q_LORA + KV_LORA + QK_ROPE
D_IN_PAD = 1920
N_HEADS, QK_NOPE, V_HEAD = 8, 128, 128
HEAD_PAD = 256
SM_SCALE = 1.0 / math.sqrt(QK_NOPE + QK_ROPE)
CHUNK = 64
N_GROUPS, SSM_GROUP, SSM_STATE = 64, 16, 64
N_SEG = 8
SEG = 264
TP = N_SEG * SEG
GQ = 8
D_FF = 5504
FF_PIECE = 1376
ROPE_BASE = 10000.0
LR, B1, B2, ADAM_EPS, WD, STEP = 0.001, 0.9, 0.999, 1e-08, 0.01, 10
VMEM_CAP = 60 * 1024 * 1024
MESH = pl.DeviceIdType.MESH

WEIGHTS = ['meta_tokens', 'mix_norm', 'w_in', 'lam_re', 'lam_im', 'log_dt', 'b_re', 'b_im', 'c_re', 'c_im',
           'd_skip', 'w_glu', 'b_glu', 'q_a_norm', 'w_q_b', 'kv_a_norm', 'w_kv_b', 'out_norm_ssm',
           'out_norm_attn', 'w_out', 'ffn_norm', 'w_up', 'conv_w', 'conv_b', 'w_down', 'final_norm']
BIG = ['w_in', 'w_glu', 'w_q_b', 'w_kv_b', 'w_out', 'w_up', 'w_down']
SMALL = [n for n in WEIGHTS if n not in BIG and n not in ('meta_tokens', 'conv_w')]


def _nbytes(shape, dtype):
    return math.prod(shape) * jnp.dtype(dtype).itemsize


def _in_hbm(a):
    return pltpu.with_memory_space_constraint(a, pltpu.HBM) if a.size * a.dtype.itemsize >= (1 << 20) else a


def _out_hbm(shape, dtype):
    big = math.prod(shape) * jnp.dtype(dtype).itemsize >= (1 << 20)
    return pltpu.HBM(shape, dtype) if big else jax.ShapeDtypeStruct(shape, dtype)


def _params(sem, need):
    return pltpu.CompilerParams(dimension_semantics=("arbitrary",) * sem,
                                vmem_limit_bytes=int(min(VMEM_CAP, max(need, 16 * 1024 * 1024))))


_DIMS = {'nn': (((1,), (0,)), ((), ())), 'nt': (((1,), (1,)), ((), ())), 'tn': (((0,), (0,)), ((), ()))}


def _mm(name, a, b, mode, tm, tn, tk, out_dtype, stack=None, res=None):
    sa, sb, so = stack in ('a_out', 'ab_red'), stack in ('b_out', 'ab_red'), stack in ('a_out', 'b_out')
    nj = a.shape[0] if sa else (b.shape[0] if sb else 1)
    a2, b2 = a.shape[-2:], b.shape[-2:]
    if mode == 'nn':
        (m, k), (k2, n) = a2, b2
    elif mode == 'nt':
        (m, k), (n, k2) = a2, b2
    else:
        (k, m), (k2, n) = a2, b2
    assert k == k2 and m % tm == 0 and n % tn == 0 and k % tk == 0, (name, a.shape, b.shape)
    n_jo, n_jr, n_k = (nj if so else 1), (nj if stack == 'ab_red' else 1), k // tk
    grid = (n_jo, m // tm, n // tn, n_jr, n_k)
    a_blk, a_idx = ((tk, tm), lambda i, j, kk: (kk, i)) if mode == 'tn' else ((tm, tk), lambda i, j, kk: (i, kk))
    b_blk, b_idx = ((tn, tk), lambda i, j, kk: (j, kk)) if mode == 'nt' else ((tk, tn), lambda i, j, kk: (kk, j))

    def spec(blk, idx, stacked, on_out):
        if not stacked:
            return pl.BlockSpec(blk, lambda jo, i, j, jr, kk: idx(i, j, kk))
        return pl.BlockSpec((None,) + blk, lambda jo, i, j, jr, kk: ((jo if on_out else jr),) + idx(i, j, kk))

    o_idx = lambda i, j, kk: (i, j)
    in_specs = [spec(a_blk, a_idx, sa, stack == 'a_out'), spec(b_blk, b_idx, sb, stack == 'b_out')]
    operands = [a, b]
    if res is not None:
        in_specs.append(spec((tm, tn), o_idx, False, False))
        operands.append(res)
    n_red = n_jr * n_k
    dims = _DIMS[mode]

    def body(*refs):
        a_ref, b_ref = refs[0], refs[1]
        res_ref = refs[2] if res is not None else None
        o_ref = refs[3] if res is not None else refs[2]
        part = lax.dot_general(a_ref[...].astype(BF16), b_ref[...].astype(BF16), dims,
                               preferred_element_type=F32)

        def finish(total):
            if res_ref is not None:
                total = total + res_ref[...]
            o_ref[...] = total.astype(o_ref.dtype)

        if n_red == 1:
            finish(part)
        else:
            acc_ref = refs[-1]
            step = pl.program_id(3) * n_k + pl.program_id(4)

            @pl.when(step == 0)
            def _():
                acc_ref[...] = part

            @pl.when(step > 0)
            def _():
                acc_ref[...] += part

            @pl.when(step == n_red - 1)
            def _():
                finish(acc_ref[...])

    out_shape = ((nj,) if so else ()) + (m, n)
    need = 2 * (_nbytes(a_blk, a.dtype) + _nbytes(b_blk, b.dtype) + _nbytes((tm, tn), out_dtype)
                + (_nbytes((tm, tn), F32) if res is not None else 0)) + 3 * _nbytes((tm, tn), F32) + (4 << 20)
    return pl.pallas_call(
        body, name=name, grid=grid, in_specs=in_specs, out_specs=spec((tm, tn), o_idx, so, True),
        out_shape=_out_hbm(out_shape, out_dtype),
        scratch_shapes=[pltpu.VMEM((tm, tn), F32)] if n_red > 1 else [],
        compiler_params=_params(5, need),
    )(*map(_in_hbm, operands))


def _rowwise(name, fn, row_ins, const_ins, row_outs, acc_outs, tr=TR):
    n_ri, n_ci, n_ro, n_ao = len(row_ins), len(const_ins), len(row_outs), len(acc_outs)

    def body(*refs):
        ri, ci = refs[:n_ri], refs[n_ri:n_ri + n_ci]
        ro, ao = refs[n_ri + n_ci:n_ri + n_ci + n_ro], refs[n_ri + n_ci + n_ro:]
        i = pl.program_id(0)
        tiles = [jnp.where(i == 0, pltpu.roll(r[...], lead, 0), r[...]) if lead else r[...]
                 for r, lead in zip(ri, leads)]
        outs = fn(i * tr, *tiles, *[r[...] for r in ci])
        for r, v in zip(ro, outs[:n_ro]):
            r[...] = v.astype(r.dtype)
        if n_ao:
            @pl.when(i == 0)
            def _():
                for r, v in zip(ao, outs[n_ro:]):
                    r[...] = v

            @pl.when(i > 0)
            def _():
                for r, v in zip(ao, outs[n_ro:]):
                    r[...] += v

    def row_spec(a, w, cb, lead=0):
        if not lead:
            return pl.BlockSpec((tr, w), lambda i: (i, cb))
        assert a.shape[0] + lead == T and lead % 8 == 0 and tr % 8 == 0
        return pl.BlockSpec((pl.Element(tr), pl.Element(w)),
                            lambda i: (jnp.maximum(i * (tr // 8) - lead // 8, 0) * 8, cb * w))

    in_specs = [row_spec(*r) for r in row_ins]
    leads = [r[3] if len(r) > 3 else 0 for r in row_ins]
    row_ins = [r[:3] for r in row_ins]
    in_specs += [pl.BlockSpec(c.shape, functools.partial(lambda nd, i: (0,) * nd, c.ndim)) for c in const_ins]
    out_specs = [pl.BlockSpec((tr, w), lambda i: (i, 0)) for w, _ in row_outs]
    out_specs += [pl.BlockSpec((1, w), lambda i: (0, 0)) for w in acc_outs]
    out_shape = [_out_hbm((T, w), dt) for w, dt in row_outs]
    out_shape += [jax.ShapeDtypeStruct((1, w), F32) for w in acc_outs]
    need = 2 * (sum(_nbytes((tr, w), a.dtype) for a, w, _ in row_ins) + sum(_nbytes((tr, w), dt) for w, dt in row_outs))
    need = 3 * need + (8 << 20)
    return pl.pallas_call(
        body, name=name, grid=(T // tr,), in_specs=in_specs, out_specs=out_specs, out_shape=out_shape,
        compiler_params=_params(1, need),
    )(*[_in_hbm(a) for a, _, _ in row_ins], *const_ins)


def _rms_fwd_fn(row0, h, g):
    h = h.astype(F32)
    r = lax.rsqrt(jnp.mean(h * h, axis=-1, keepdims=True) + EPS)
    return (h * r) * g, r


def _rms_mix_fn(row0, x, meta, g):
    h = jnp.where(row0 == 0, jnp.concatenate([meta, x[N_META:]], axis=0), x)
    return (h,) + _rms_fwd_fn(row0, h, g)


def _rms_bwd(dxn, h, r, g):
    xhat = h * r
    dxg = dxn * g
    dh = r * (dxg - xhat * jnp.mean(dxg * xhat, axis=-1, keepdims=True))
    return dh, jnp.sum(dxn * xhat, axis=0, keepdims=True)


def _rms_bwd_fn(row0, dxn, h, r, g):
    return _rms_bwd(dxn.astype(F32), h.astype(F32), r, g)


def _rms_bwd_res_fn(row0, dxn, h, r, res, g):
    dh, dg = _rms_bwd(dxn.astype(F32), h.astype(F32), r, g)
    return dh + res, dg


def _rms_bwd_res_twice_fn(*args):
    dh, dg = _rms_bwd_res_fn(*args)
    return dh, dh, dg


def _gelu_fn(row0, y):
    return (jax.nn.gelu(y),)


def _glu_fwd_fn(row0, y, gl, b, gain):
    ya = jax.nn.gelu(y) * jax.nn.sigmoid(gl + b)
    return _rms_fwd_fn(row0, ya, gain)


def _glu_bwd_fn(row0, dyan, y, gl, ra, b, gain):
    g = jax.nn.gelu(y)
    s = jax.nn.sigmoid(gl + b)
    dya, dgain = _rms_bwd(dyan, g * s, ra, gain)
    dgl = dya * g * s * (1.0 - s)
    return dgl, dya * s, dgain, jnp.sum(dgl, axis=0, keepdims=True)


def _gelu_bwd_fn(row0, dg, y):
    return (jax.vjp(jax.nn.gelu, y)[1](dg)[0],)


def _loss_fn(row0, h, tgt, gain):
    r = lax.rsqrt(jnp.mean(h * h, axis=-1, keepdims=True) + EPS)
    xhat = h * r
    rows = row0 + lax.broadcasted_iota(jnp.int32, h.shape, 0)
    err = jnp.where(rows >= N_META, xhat * gain - tgt, 0.0)
    loss = jnp.full((1, 128), jnp.sum(err * err) * (0.5 / D), F32)
    dh, dgain = _rms_bwd(err * (1.0 / D), h, r, gain)
    return dh, dh, dgain, loss


def _rope_tables():
    pos = jnp.arange(T, dtype=jnp.int32)
    inv_freq = 1.0 / (ROPE_BASE ** (jnp.arange(0, QK_ROPE, 2, dtype=F32) / QK_ROPE))
    ang = pos.astype(F32)[:, None] * inv_freq[None, :]
    cos, sin, z32, z64 = jnp.cos(ang), jnp.sin(ang), jnp.zeros((T, 32), F32), jnp.zeros((T, 64), F32)
    return (jnp.concatenate([cos, cos, z64], 1), jnp.concatenate([-sin, z32, z64], 1),
            jnp.concatenate([z32, sin, z64], 1))


def _rot(x, cc, s1, s2):
    return x * cc + pltpu.roll(x, 96, 1) * s1 + pltpu.roll(x, 32, 1) * s2


def _derot(d, cc, s1, s2):
    return d * cc + pltpu.roll(d * s1, 32, 1) + pltpu.roll(d * s2, 96, 1)


def _chunk_of(pos):
    return jnp.where(pos < N_META, 0, 1 + ((pos - N_META) >> 6))


def _key_len(t):
    last_chunk = 1 + ((t + 1) * TR - 1 - N_META) // CHUNK
    return min(T, pl.cdiv(N_META + last_chunk * CHUNK, 128) * 128)


def _attn_probs(t, q_ref, kv_ref, kpe_ref, tq, tk):
    kl = _key_len(t)
    q = q_ref[...]
    qn = q[:, :QK_NOPE].astype(BF16)
    qp = _rot(q[:, QK_NOPE:], tq[0][...], tq[1][...], tq[2][...]).astype(BF16)
    kn = kv_ref[:kl, :QK_NOPE].astype(BF16)
    v = kv_ref[:kl, QK_NOPE:].astype(BF16)
    kp = _rot(kpe_ref[:kl, :], tk[0][:kl, :], tk[1][:kl, :], tk[2][:kl, :]).astype(BF16)
    s = lax.dot_general(qn, kn, _DIMS['nt'], preferred_element_type=F32)
    s = s + lax.dot_general(qp, kp, _DIMS['nt'], preferred_element_type=F32)
    qc = _chunk_of(t * TR + lax.broadcasted_iota(jnp.int32, (TR, 1), 0))
    kc = _chunk_of(lax.broadcasted_iota(jnp.int32, (1, kl), 1))
    s = jnp.where(kc <= qc, s * SM_SCALE, jnp.finfo(F32).min)
    e = jnp.exp(s - jnp.max(s, axis=-1, keepdims=True))
    return qn, qp, kn, kp, v, e, 1.0 / jnp.sum(e, axis=-1, keepdims=True)


def _per_q_tile(fn):
    for t in range(T // TR):
        pl.when(pl.program_id(1) == t)(functools.partial(fn, t))


def _attn_specs(z, tabs):
    q_spec = pl.BlockSpec((None, TR, HEAD_PAD), lambda h, i: (h, i, 0))
    kv_spec = pl.BlockSpec((None, T, HEAD_PAD), lambda h, i: (h, 0, 0))
    kpe_spec = pl.BlockSpec((T, 128), lambda h, i: (0, (D_IN - QK_ROPE) // 128))
    tq = [pl.BlockSpec((TR, 128), lambda h, i: (i, 0))] * 3
    tk = [pl.BlockSpec((T, 128), lambda h, i: (0, 0))] * 3
    return [q_spec, kv_spec, kpe_spec] + tq + tk


def _attn_fwd(q, kv, z, tabs):
    def body(q_ref, kv_ref, kpe_ref, c1, c2, c3, k1, k2, k3, o_ref):
        def tile(t):
            _, _, _, _, v, e, inv = _attn_probs(t, q_ref, kv_ref, kpe_ref, (c1, c2, c3), (k1, k2, k3))
            o_ref[...] = jnp.dot(e.astype(BF16), v, preferred_element_type=F32) * inv

        _per_q_tile(tile)

    return pl.pallas_call(
        body, name="attn_fwd", grid=(N_HEADS, T // TR), in_specs=_attn_specs(z, tabs),
        out_specs=pl.BlockSpec((TR, V_HEAD), lambda h, i: (i, h)),
        out_shape=_out_hbm((T, N_HEADS * V_HEAD), F32),
        compiler_params=_params(2, 48 << 20),
    )(_in_hbm(q), _in_hbm(kv), _in_hbm(z), *tabs, *tabs)


def _attn_bwd(q, kv, z, tabs, dyb):
    def body(q_ref, kv_ref, kpe_ref, c1, c2, c3, k1, k2, k3, do_ref, dq_ref, dkv_ref, dkpe_ref):
        @pl.when(pl.program_id(1) == 0)
        def _():
            dkv_ref[...] = jnp.zeros(dkv_ref.shape, F32)
            dkpe_ref[...] = jnp.zeros(dkpe_ref.shape, F32)

        def tile(t):
            kl = _key_len(t)
            qn, qp, kn, kp, v, e, inv = _attn_probs(t, q_ref, kv_ref, kpe_ref, (c1, c2, c3), (k1, k2, k3))
            do = do_ref[...]
            dp = lax.dot_general(do.astype(BF16), v, _DIMS['nt'], preferred_element_type=F32)
            delta = jnp.sum(e * dp, axis=-1, keepdims=True) * inv
            ds = (e * ((dp - delta) * (inv * SM_SCALE))).astype(BF16)
            pb, do = e.astype(BF16), (do * inv).astype(BF16)
            dq_ref[:, :QK_NOPE] = jnp.dot(ds, kn, preferred_element_type=F32).astype(dq_ref.dtype)
            dqp = _derot(jnp.dot(ds, kp, preferred_element_type=F32), c1[...], c2[...], c3[...])
            dq_ref[:, QK_NOPE:] = dqp.astype(dq_ref.dtype)
            dkv_ref[:kl, :QK_NOPE] += lax.dot_general(ds, qn, _DIMS['tn'], preferred_element_type=F32)
            dkv_ref[:kl, QK_NOPE:] += lax.dot_general(pb, do, _DIMS['tn'], preferred_element_type=F32)
            dkpe_ref[:kl, :] += lax.dot_general(ds, qp, _DIMS['tn'], preferred_element_type=F32)

        _per_q_tile(tile)

    return pl.pallas_call(
        body, name="attn_bwd", grid=(N_HEADS, T // TR),
        in_specs=_attn_specs(z, tabs) + [pl.BlockSpec((TR, V_HEAD), lambda h, i: (i, h))],
        out_specs=[pl.BlockSpec((None, TR, HEAD_PAD), lambda h, i: (h, i, 0)),
                   pl.BlockSpec((None, T, HEAD_PAD), lambda h, i: (h, 0, 0)),
                   pl.BlockSpec((None, T, 128), lambda h, i: (h, 0, 0))],
        out_shape=[_out_hbm((N_HEADS, T, HEAD_PAD), BF16), _out_hbm((N_HEADS, T, HEAD_PAD), F32),
                   _out_hbm((N_HEADS, T, 128), F32)],
        compiler_params=_params(2, 56 << 20),
    )(_in_hbm(q), _in_hbm(kv), _in_hbm(z), *tabs, *tabs, _in_hbm(dyb))


def _kpe_bwd(dkpe, tabs):
    def body(d_ref, c1, c2, c3, o_ref):
        d = d_ref[0]
        for h in range(1, N_HEADS):
            d = d + d_ref[h]
        o_ref[...] = _derot(d, c1[...], c2[...], c3[...]).astype(o_ref.dtype)

    tile = pl.BlockSpec((TR, 128), lambda i: (i, 0))
    return pl.pallas_call(
        body, name="kpe_bwd", grid=(T // TR,),
        in_specs=[pl.BlockSpec((N_HEADS, TR, 128), lambda i: (0, i, 0)), tile, tile, tile], out_specs=tile,
        out_shape=jax.ShapeDtypeStruct((T, 128), BF16), compiler_params=_params(1, 16 << 20),
    )(_in_hbm(dkpe), *tabs)


def _s5_prep_math(lam_re, lam_im, log_dt, bt_re, bt_im):
    dt = jnp.exp(log_dt)
    mag = jnp.exp(lam_re * dt)
    a_re, a_im = mag * jnp.cos(lam_im * dt), mag * jnp.sin(lam_im * dt)
    den = lam_re * lam_re + lam_im * lam_im
    w_re = ((a_re - 1.0) * lam_re + a_im * lam_im) / den
    w_im = (a_im * lam_re - (a_re - 1.0) * lam_im) / den
    return a_re, a_im, w_re[None] * bt_re - w_im[None] * bt_im, w_re[None] * bt_im + w_im[None] * bt_re


def _eye_groups():
    return (lax.broadcasted_iota(jnp.int32, (N_GROUPS, N_GROUPS), 0)
            == lax.broadcasted_iota(jnp.int32, (N_GROUPS, N_GROUPS), 1)).astype(F32)


def _row_to_col(row):
    return jnp.sum(_eye_groups() * row, axis=1, keepdims=True)


def _s5_prep(lam_re, lam_im, log_dt, bt_re, bt_im):
    def body(lr, li, ld, br, bi, ar, ai, bbr, bbi):
        ar[...], ai[...], bbr[...], bbi[...] = _s5_prep_math(lr[...], li[...], _row_to_col(ld[...]), br[...], bi[...])

    gp, cgp = jax.ShapeDtypeStruct((N_GROUPS, SSM_STATE), F32), jax.ShapeDtypeStruct(bt_re.shape, F32)
    return pl.pallas_call(body, name="s5_prep", out_shape=[gp, gp, cgp, cgp])(lam_re, lam_im, log_dt, bt_re, bt_im)


def _s5_prep_bwd(lam_re, lam_im, log_dt, bt_re, bt_im, da_re, da_im, dbb_re, dbb_im):
    def body(lr, li, ld, br, bi, dar, dai, dbr, dbi, o1, o2, o3, o4, o5):
        _, vjp = jax.vjp(_s5_prep_math, lr[...], li[...], _row_to_col(ld[...]), br[...], bi[...])
        o1[...], o2[...], dld, o4[...], o5[...] = vjp((dar[...], dai[...], dbr[...], dbi[...]))
        o3[...] = jnp.sum(_eye_groups() * dld, axis=0, keepdims=True)

    ins = (lam_re, lam_im, log_dt, bt_re, bt_im)
    return pl.pallas_call(body, name="s5_prep_bwd", out_shape=[jax.ShapeDtypeStruct(a.shape, F32) for a in ins])(
        *ins, da_re, da_im, dbb_re, dbb_im)


def _cmul(ar, ai, br, bi):
    return ar * br - ai * bi, ar * bi + ai * br


def _seg_rows(i):
    return pl.ds(pl.multiple_of(i * N_SEG, N_SEG), N_SEG)


def _scan(xr, xi, ar, ai, reverse):
    zero = jnp.zeros(ar.shape, F32)

    def local(j, carry):
        rows = _seg_rows(SEG - 1 - j if reverse else j)
        nr, ni = _cmul(ar, ai, *carry)
        nr, ni = nr + xr[rows, :], ni + xi[rows, :]
        xr[rows, :], xi[rows, :] = nr, ni
        return nr, ni

    er, ei = lax.fori_loop(0, SEG, local, (zero, zero))
    power, square, bits = None, (ar, ai), SEG
    while bits:
        if bits & 1:
            power = square if power is None else _cmul(*power, *square)
        square, bits = _cmul(*square, *square), bits >> 1
    pr, pi = power
    row = lax.broadcasted_iota(jnp.int32, ar.shape, 0)
    edge, shift = (N_SEG - 1, N_SEG - 1) if reverse else (0, 1)
    hr, hi = zero, zero
    for _ in range(N_SEG - 1):
        tr_, ti_ = _cmul(pr, pi, hr, hi)
        hr = jnp.where(row == edge, 0.0, pltpu.roll(tr_ + er, shift, 0))
        hi = jnp.where(row == edge, 0.0, pltpu.roll(ti_ + ei, shift, 0))

    def fix(j, carry):
        rows = _seg_rows(SEG - 1 - j if reverse else j)
        xr[rows, :] += carry[0]
        xi[rows, :] += carry[1]
        return _cmul(ar, ai, *carry)

    lax.fori_loop(0, SEG, fix, _cmul(ar, ai, hr, hi))
    return hr, hi


def _split(x):
    hi = x.astype(BF16)
    return hi, (x - hi.astype(F32)).astype(BF16)


def _dot2(xs, ys, mode='nn'):
    d = lambda p, q: lax.dot_general(p, q, _DIMS[mode], preferred_element_type=F32)
    return d(xs[0], ys[0]) + d(xs[1], ys[0])


def _group_masks():
    ns = GQ * SSM_STATE
    rep = (lax.broadcasted_iota(jnp.int32, (SSM_STATE, ns), 1) % SSM_STATE
           == lax.broadcasted_iota(jnp.int32, (SSM_STATE, ns), 0)).astype(BF16)
    own = (lax.broadcasted_iota(jnp.int32, (128, ns), 0) // SSM_GROUP
           == lax.broadcasted_iota(jnp.int32, (128, ns), 1) // SSM_STATE)
    return rep, own


def _block_diag(x, rep, own):
    dot = lambda p: jnp.dot(p, rep, preferred_element_type=F32)
    return tuple(jnp.where(own, dot(part), 0.0).astype(BF16) for part in _split(x))


def _block_rows(full, rep, own):
    masked = _split(jnp.where(own, full, 0.0))
    dot = lambda p: lax.dot_general(p, rep, _DIMS['nt'], preferred_element_type=F32)
    return dot(masked[0]) + dot(masked[1])


def _s5_specs():
    col = pl.BlockSpec((TP, 128), lambda q: (0, q))
    a_spec = pl.BlockSpec((None, 2, N_SEG, GQ * SSM_STATE), lambda q: (q, 0, 0, 0))
    rows = pl.BlockSpec((128, SSM_STATE), lambda q: (q, 0))
    d_spec = pl.BlockSpec((1, 128), lambda q: (0, q))
    return col, a_spec, rows, d_spec


def _s5_fwd(u, a8, bb_re, bb_im, c_re, c_im, dsk):
    def body(u_ref, a_ref, bre_ref, bim_ref, cre_ref, cim_ref, d_ref, y_ref, hre, him):
        u_ = u_ref[...]
        us = _split(u_)
        rep, own = _group_masks()
        hre[...] = _dot2(us, _block_diag(bre_ref[...], rep, own))
        him[...] = _dot2(us, _block_diag(bim_ref[...], rep, own))
        _scan(hre, him, a_ref[0], a_ref[1], reverse=False)
        dot = lambda h, c: lax.dot_general(h[...].astype(BF16), _block_diag(c[...], rep, own)[0], _DIMS['nt'],
                                           preferred_element_type=F32)
        y_ref[...] = dot(hre, cre_ref) - dot(him, cim_ref) + d_ref[...] * u_

    col, a_spec, rows, d_spec = _s5_specs()
    return pl.pallas_call(
        body, name="s5_fwd", grid=(N_GROUPS // GQ,), in_specs=[col, a_spec, rows, rows, rows, rows, d_spec],
        out_specs=col, out_shape=_out_hbm((TP, D_SSM), F32),
        scratch_shapes=[pltpu.VMEM((TP, GQ * SSM_STATE), F32)] * 2,
        compiler_params=_params(1, 40 << 20),
    )(_in_hbm(u), a8, bb_re, bb_im, c_re, c_im, dsk)


def _s5_bwd(u, dy, a8, bb_re, bb_im, c_re, c_im, dsk):
    def body(u_ref, dy_ref, a_ref, bre_ref, bim_ref, cre_ref, cim_ref, d_ref,
             du_ref, dbre_ref, dbim_ref, dcre_ref, dcim_ref, da_ref, dd_ref, hre, him, lre, lim):
        u_, dy_ = u_ref[...], dy_ref[...]
        us, dys = _split(u_), _split(dy_)
        ar, ai = a_ref[0], a_ref[1]
        rep, own = _group_masks()
        bres, bims = _block_diag(bre_ref[...], rep, own), _block_diag(bim_ref[...], rep, own)
        hre[...] = _dot2(us, bres)
        him[...] = _dot2(us, bims)
        h0r, h0i = _scan(hre, him, ar, ai, reverse=False)
        lre[...] = _dot2(dys, _block_diag(cre_ref[...], rep, own))
        lim[...] = -_dot2(dys, _block_diag(cim_ref[...], rep, own))
        _scan(lre, lim, ar, -ai, reverse=True)

        def acc_da(i, carry):
            lr, li = lre[_seg_rows(i), :], lim[_seg_rows(i), :]
            pr, pi = hre[_seg_rows(i - 1), :], him[_seg_rows(i - 1), :]
            return carry[0] + lr * pr + li * pi, carry[1] + li * pr - lr * pi

        lr, li = lre[_seg_rows(0), :], lim[_seg_rows(0), :]
        dar, dai = lax.fori_loop(1, SEG, acc_da, (lr * h0r + li * h0i, li * h0r - lr * h0i))
        da_ref[0:1, :] = jnp.sum(dar, axis=0, keepdims=True)
        da_ref[1:2, :] = jnp.sum(dai, axis=0, keepdims=True)
        dot = lambda p, q, mode: lax.dot_general(p, q, _DIMS[mode], preferred_element_type=F32)
        lr1, li1 = lre[...].astype(BF16), lim[...].astype(BF16)
        du_ref[...] = dy_ * d_ref[...] + dot(lr1, bres[0], 'nt') + dot(li1, bims[0], 'nt')
        dbre_ref[...] = _block_rows(dot(us[0], lr1, 'tn'), rep, own)
        dbim_ref[...] = _block_rows(dot(us[0], li1, 'tn'), rep, own)
        dcre_ref[...] = _block_rows(dot(dys[0], hre[...].astype(BF16), 'tn'), rep, own)
        dcim_ref[...] = -_block_rows(dot(dys[0], him[...].astype(BF16), 'tn'), rep, own)
        dd_ref[...] = jnp.sum(dy_ * u_, axis=0, keepdims=True)

    col, a_spec, rows, d_spec = _s5_specs()
    nq, ns = N_GROUPS // GQ, GQ * SSM_STATE
    per_row = jax.ShapeDtypeStruct((N_GROUPS * SSM_GROUP, SSM_STATE), F32)
    return pl.pallas_call(
        body, name="s5_bwd", grid=(nq,),
        in_specs=[col, col, a_spec, rows, rows, rows, rows, d_spec],
        out_specs=[col, rows, rows, rows, rows, pl.BlockSpec((None, 2, ns), lambda q: (q, 0, 0)), d_spec],
        out_shape=[_out_hbm((TP, D_SSM), F32), per_row, per_row, per_row, per_row,
                   jax.ShapeDtypeStruct((nq, 2, ns), F32), jax.ShapeDtypeStruct((1, D_SSM), F32)],
        scratch_shapes=[pltpu.VMEM((TP, ns), F32)] * 4,
        compiler_params=_params(1, 56 << 20),
    )(_in_hbm(u), _in_hbm(dy), a8, bb_re, bb_im, c_re, c_im, dsk)


def _to_segments(a, cols=None):
    a = jnp.pad(a[:, :cols], ((0, TP - T), (0, 0)))
    return a.reshape(N_SEG, SEG, a.shape[-1]).transpose(1, 0, 2).reshape(TP, a.shape[-1])


def _from_segments(a):
    return a.reshape(SEG, N_SEG, a.shape[-1]).transpose(1, 0, 2).reshape(TP, a.shape[-1])[:T]


def _ff_specs():
    up_spec = pl.BlockSpec((2, None, T, 128), lambda p, cb: (0, p, 0, cb))
    piece = pl.BlockSpec((None, T, 128), lambda p, cb: (p, 0, cb))
    cw_spec = pl.BlockSpec((None, 3, 128), lambda p, cb: (p, 0, cb))
    cb_spec = pl.BlockSpec((None, 1, 128), lambda p, cb: (p, 0, cb))
    return up_spec, piece, cw_spec, cb_spec


def _conv_gate(gate, w, cb, pad_ref):
    pad_ref[0:8, :] = jnp.zeros((8, 128), F32)
    pad_ref[8:8 + T, :] = gate
    g1, g2 = pad_ref[pl.ds(7, T), :], pad_ref[pl.ds(6, T), :]
    return w[0:1, :] * g2 + w[1:2, :] * g1 + w[2:3, :] * gate + cb, g1, g2


def _ff_act(up, cw4, cb4):
    def body(up_ref, w_ref, b_ref, o_ref, pad_ref):
        gc, _, _ = _conv_gate(up_ref[0].astype(F32), w_ref[...], b_ref[...], pad_ref)
        o_ref[...] = (jax.nn.silu(gc) * up_ref[1].astype(F32)).astype(o_ref.dtype)

    up_spec, piece, cw_spec, cb_spec = _ff_specs()
    n_cb = pl.cdiv(FF_PIECE, 128)
    return pl.pallas_call(
        body, name="ff_act", grid=(4, n_cb), in_specs=[up_spec, cw_spec, cb_spec], out_specs=piece,
        out_shape=_out_hbm((4, T, FF_PIECE), BF16),
        scratch_shapes=[pltpu.VMEM((T + 8, 128), F32)],
        compiler_params=_params(2, 24 << 20),
    )(_in_hbm(up.reshape(2, 4, T, FF_PIECE)), cw4, cb4)


def _ff_act_bwd(up, dact, cw4, cb4):
    def body(up_ref, da_ref, w_ref, b_ref, dup_ref, dw_ref, db_ref, pad_ref, pad2_ref):
        gate, val, w = up_ref[0].astype(F32), up_ref[1].astype(F32), w_ref[...]
        gc, g1, g2 = _conv_gate(gate, w, b_ref[...], pad_ref)
        sg = jax.nn.sigmoid(gc)
        da = da_ref[...].astype(F32)
        dup_ref[1] = (da * gc * sg).astype(dup_ref.dtype)
        dgc = da * val * sg * (1.0 + gc * (1.0 - sg))
        db_ref[...] = jnp.sum(dgc, axis=0, keepdims=True)
        dw_ref[0:1, :] = jnp.sum(dgc * g2, axis=0, keepdims=True)
        dw_ref[1:2, :] = jnp.sum(dgc * g1, axis=0, keepdims=True)
        dw_ref[2:3, :] = jnp.sum(dgc * gate, axis=0, keepdims=True)
        pad2_ref[0:T, :] = dgc
        pad2_ref[T:T + 8, :] = jnp.zeros((8, 128), F32)
        d1, d2 = pad2_ref[pl.ds(1, T), :], pad2_ref[pl.ds(2, T), :]
        dup_ref[0] = (w[2:3, :] * dgc + w[1:2, :] * d1 + w[0:1, :] * d2).astype(dup_ref.dtype)

    up_spec, piece, cw_spec, cb_spec = _ff_specs()
    n_cb = pl.cdiv(FF_PIECE, 128)
    dup, dw, db = pl.pallas_call(
        body, name="ff_act_bwd", grid=(4, n_cb), in_specs=[up_spec, piece, cw_spec, cb_spec],
        out_specs=[up_spec, cw_spec, cb_spec],
        out_shape=[_out_hbm((2, 4, T, FF_PIECE), BF16), jax.ShapeDtypeStruct((4, 3, FF_PIECE), F32),
                   jax.ShapeDtypeStruct((4, 1, FF_PIECE), F32)],
        scratch_shapes=[pltpu.VMEM((T + 8, 128), F32)] * 2,
        compiler_params=_params(2, 32 << 20),
    )(_in_hbm(up.reshape(2, 4, T, FF_PIECE)), _in_hbm(dact), cw4, cb4)
    return dup.reshape(8, T, FF_PIECE), dw, db


def _local_step(x, tgt, meta, p, wf, mid_weights, late_weights, send_grads, send_small):
    sm = {}
    tabs = _rope_tables()
    row = lambda a: a.reshape(1, -1)

    h0, xn, r0 = _rowwise("rms_mix", _rms_mix_fn, [(x, D, 0, N_META)], [meta, row(p['mix_norm'])],
                          [(D, F32), (D, BF16), (1, F32)], [])
    z = _mm("in_proj", xn, wf['w_in'], 'nn', TR, 640, D, F32)
    lam_re, lam_im = p['lam_re'].reshape(N_GROUPS, SSM_STATE), p['lam_im'].reshape(N_GROUPS, SSM_STATE)
    log_dt = p['log_dt'].reshape(1, N_GROUPS)
    bt_re = p['b_re'].reshape(N_GROUPS, SSM_STATE, SSM_GROUP).transpose(2, 0, 1)
    bt_im = p['b_im'].reshape(N_GROUPS, SSM_STATE, SSM_GROUP).transpose(2, 0, 1)
    c_re = p['c_re'].reshape(N_GROUPS * SSM_GROUP, SSM_STATE)
    c_im = p['c_im'].reshape(N_GROUPS * SSM_GROUP, SSM_STATE)
    a_re, a_im, bbt_re, bbt_im = _s5_prep(lam_re, lam_im, log_dt, bt_re, bt_im)
    nq, ns = N_GROUPS // GQ, GQ * SSM_STATE
    a8 = jnp.broadcast_to(jnp.stack([a_re.reshape(nq, ns), a_im.reshape(nq, ns)], 1)[:, :, None, :],
                          (nq, 2, N_SEG, ns))
    gcp = lambda t: t.transpose(1, 0, 2).reshape(N_GROUPS * SSM_GROUP, SSM_STATE)
    bb_re, bb_im = gcp(bbt_re), gcp(bbt_im)
    dsk = row(p['d_skip'])
    u_seg = _to_segments(z, D_SSM)
    y_ssm = _from_segments(_s5_fwd(u_seg, a8, bb_re, bb_im, c_re, c_im, dsk))
    wf = dict(wf, **mid_weights(y_ssm))
    g, = _rowwise("gelu", _gelu_fn, [(y_ssm, D_SSM, 0)], [], [(D_SSM, BF16)], [])
    gl = _mm("glu_proj", g, wf['w_glu'], 'nn', TR, D_SSM, D_SSM, F32)
    yan, ra = _rowwise("glu_norm", _glu_fwd_fn, [(y_ssm, D_SSM, 0), (gl, D_SSM, 0)],
                       [row(p['b_glu']), row(p['out_norm_ssm'])], [(D_SSM, BF16), (1, F32)], [])
    qn, rq, kvn, rkv = _rowwise(
        "rms_q_kv", lambda row0, zq, zkv, gq, gkv: _rms_fwd_fn(row0, zq, gq) + _rms_fwd_fn(row0, zkv, gkv),
        [(z, Q_LORA, D_SSM // Q_LORA), (z, KV_LORA, (D_SSM + Q_LORA) // KV_LORA)],
        [row(p['q_a_norm']), row(p['kv_a_norm'])], [(Q_LORA, BF16), (1, F32), (KV_LORA, BF16), (1, F32)], [])
    q = _mm("q_proj", qn, wf['w_q_b'], 'nn', T, HEAD_PAD, Q_LORA, F32, stack='b_out')
    kv = _mm("kv_proj", kvn, wf['w_kv_b'], 'nn', T, HEAD_PAD, KV_LORA, F32, stack='b_out')
    yb = _attn_fwd(q, kv, z, tabs)
    ybn, rb = _rowwise("rms_attn", _rms_fwd_fn, [(yb, D_SSM, 0)], [row(p['out_norm_attn'])],
                       [(D_SSM, BF16), (1, F32)], [])
    y = jnp.concatenate([yan, ybn], axis=1)
    wf = dict(wf)
    wf['w_out'], token, up_weights = late_weights(y)
    h1 = _mm("out_proj", y, wf['w_out'], 'nn', TR, 1024, D, F32, res=h0)
    xn2, r1 = _rowwise("rms_ffn", _rms_fwd_fn, [(h1, D, 0)], [row(p['ffn_norm']) + token[0:1, 0:1]],
                       [(D, BF16), (1, F32)], [])
    wf['w_up'], token, down_weights = up_weights(xn2)
    up = _mm("up_proj", xn2, wf['w_up'], 'nn', TR, FF_PIECE, D, BF16, stack='b_out')
    cw4 = wf['conv_w'].reshape(3, 4, FF_PIECE).transpose(1, 0, 2)
    cb4 = p['conv_b'].reshape(4, 1, FF_PIECE)
    act = _ff_act(up, cw4, cb4 + token[0:1, 0:1])
    wf['w_down'] = down_weights(act)
    wd4 = wf['w_down'].reshape(4, FF_PIECE, D)
    h2 = _mm("down_proj", act, wd4, 'nn', TR, 1024, FF_PIECE, F32, stack='ab_red', res=h1)
    dh2, dh2_b, sm['final_norm'], loss = _rowwise("loss", _loss_fn, [(h2, D, 0), (tgt, D, 0, N_META)],
                                                  [row(p['final_norm'])], [(D, F32), (D, BF16)], [D, 128])

    big = {}
    dact = _mm("down_bwd_x", dh2_b, wd4, 'nt', TR, FF_PIECE, D, BF16, stack='b_out')
    g_down = _mm("down_bwd_w", act, dh2_b, 'tn', FF_PIECE, 512, T, BF16, stack='a_out').reshape(
        N_DEV, D_FF // N_DEV, D)
    token = send_grads(['w_down'], [g_down], 'cores')
    dup, dcw4, sm['conv_b'] = _ff_act_bwd(up, dact, cw4, cb4 + token[0:1, 0:1])
    dxn2 = _mm("up_bwd_x", dup, wf['w_up'], 'nt', TR, D, FF_PIECE, F32, stack='ab_red')
    g_up = _mm("up_bwd_w", dup, xn2, 'tn', FF_PIECE, 1024, T, BF16, stack='a_out')
    token = send_grads(['w_up'], [g_up], 'cores') + send_grads(['w_down'], dxn2, 'chips')
    dh1, dh1_b, sm['ffn_norm'] = _rowwise("rms_ffn_bwd", _rms_bwd_res_twice_fn,
                                          [(dxn2, D, 0), (h1, D, 0), (r1, 1, 0), (dh2, D, 0)],
                                          [row(p['ffn_norm']) + token[0:1, 0:1]], [(D, F32), (D, BF16)], [D])
    dy = _mm("out_bwd_x", dh1_b, wf['w_out'], 'nt', TR, 1024, D, F32)
    g_out = _mm("out_bwd_w", y, dh1_b, 'tn', 1024, 1024, T, BF16).reshape(N_DEV, D // N_DEV, D)
    dgl, dgd, sm['out_norm_ssm'], sm['b_glu'] = _rowwise(
        "glu_bwd", _glu_bwd_fn, [(dy, D_SSM, 0), (y_ssm, D_SSM, 0), (gl, D_SSM, 0), (ra, 1, 0)],
        [row(p['b_glu']), row(p['out_norm_ssm'])], [(D_SSM, BF16), (D_SSM, F32)], [D_SSM, D_SSM])
    dg = _mm("glu_bwd_x", dgl, wf['w_glu'], 'nt', TR, D_SSM, D_SSM, F32, res=dgd)
    g_glu = _mm("glu_bwd_w", g, dgl, 'tn', D_SSM, D_SSM, T, BF16).reshape(N_DEV, D_SSM // N_DEV, D_SSM)
    dy_ssm, = _rowwise("gelu_bwd", _gelu_bwd_fn, [(dg, D_SSM, 0), (y_ssm, D_SSM, 0)], [], [(D_SSM, F32)], [])
    token = send_grads(['w_up'], g_glu, 'chips')
    dyb, sm['out_norm_attn'] = _rowwise("rms_attn_bwd", _rms_bwd_fn, [(dy, D_SSM, 1), (yb, D_SSM, 0), (rb, 1, 0)],
                                        [row(p['out_norm_attn']) + token[0:1, 0:1]], [(D_SSM, F32)], [D_SSM])
    dq, dkv, dkpe = _attn_bwd(q, kv, z, tabs, dyb)
    g_q = _mm("q_bwd_w", qn, dq, 'tn', Q_LORA, HEAD_PAD, T, BF16, stack='b_out')
    dqn = _mm("q_bwd_x", dq, wf['w_q_b'], 'nt', T, Q_LORA, HEAD_PAD, F32, stack='ab_red')
    g_kv = _mm("kv_bwd_w", kvn, dkv, 'tn', KV_LORA, HEAD_PAD, T, BF16, stack='b_out')
    dkvn = _mm("kv_bwd_x", dkv, wf['w_kv_b'], 'nt', T, KV_LORA, HEAD_PAD, F32, stack='ab_red')
    token = send_grads(['w_out', 'w_glu', 'w_q_b', 'w_kv_b'], [g_out, g_glu, g_q, g_kv], 'both')
    def both_bwd(row0, dq_, zq, rq_, dkv_, zkv, rkv_, gq, gkv):
        (dzq, dgq), (dzkv, dgkv) = _rms_bwd(dq_, zq, rq_, gq), _rms_bwd(dkv_, zkv, rkv_, gkv)
        return dzq, dzkv, dgq, dgkv

    dq_a, dkv_a, sm['q_a_norm'], sm['kv_a_norm'] = _rowwise(
        "rms_q_kv_bwd", both_bwd,
        [(dqn, Q_LORA, 0), (z, Q_LORA, D_SSM // Q_LORA), (rq, 1, 0),
         (dkvn, KV_LORA, 0), (z, KV_LORA, (D_SSM + Q_LORA) // KV_LORA), (rkv, 1, 0)],
        [row(p['q_a_norm']), row(p['kv_a_norm'])], [(Q_LORA, BF16), (KV_LORA, BF16)], [Q_LORA, KV_LORA])
    dk_pe = _kpe_bwd(dkpe, tabs)
    du_seg, dbre, dbim, dcre, dcim, da, sm['d_skip'] = _s5_bwd(
        u_seg, _to_segments(dy_ssm), a8, bb_re, bb_im, c_re, c_im, dsk + token[0:1, 0:1])
    du = _from_segments(du_seg).astype(BF16)
    cgp = lambda d: d.reshape(N_GROUPS, SSM_GROUP, SSM_STATE).transpose(1, 0, 2)
    dlam_re, dlam_im, dlog_dt, dbt_re, dbt_im = _s5_prep_bwd(
        lam_re, lam_im, log_dt, bt_re, bt_im, da[:, 0, :].reshape(N_GROUPS, SSM_STATE),
        da[:, 1, :].reshape(N_GROUPS, SSM_STATE), cgp(dbre), cgp(dbim))
    sm['lam_re'], sm['lam_im'], sm['log_dt'] = dlam_re, dlam_im, dlog_dt
    sm['b_re'], sm['b_im'] = gcp(dbt_re), gcp(dbt_im)
    sm['c_re'], sm['c_im'] = dcre, dcim
    token = send_small(sm, dcw4, loss)
    dz = jnp.concatenate([du, dq_a, dkv_a, dk_pe], axis=1)
    dxn = _mm("in_bwd_x", dz, wf['w_in'], 'nt', TR, 1024, D_IN_PAD, F32)
    big['w_in'] = _mm("in_bwd_w", xn, dz, 'tn', 1024, 640, T, BF16).reshape(N_DEV, D // N_DEV, D_IN_PAD)
    dh0, d_mix_norm = _rowwise("rms_mix_bwd", _rms_bwd_res_fn,
                               [(dxn, D, 0), (h0, D, 0), (r0, 1, 0), (dh1, D, 0)],
                               [row(p['mix_norm']) + token[0:1, 0:1]], [(D, F32)], [D])
    return dh0[N_META:], dh0[:N_META], d_mix_norm, big


def _place():
    x, y, c = lax.axis_index("x"), lax.axis_index("y"), lax.axis_index("c")
    return x, y, c, [(1 - x, y), (x, 1 - y), (1 - x, 1 - y)]


_HBM = pl.BlockSpec(memory_space=pltpu.HBM)


def _all_gather(name, arrs, chip_blocks=()):
    n, nc = len(arrs), len(chip_blocks)

    def body(*refs):
        ins, cins, outs, couts = refs[:n], refs[n:n + nc], refs[n + nc:2 * n + nc], refs[2 * n + nc:2 * (n + nc)]
        send, recv, loc, csend, crecv = refs[2 * (n + nc):]
        x, y, c, chips = _place()
        me, sib = (x, y, c), (x, y, 1 - c)

        def cp(a, k, blk, to, src=None):
            dst = outs[a].at[4 * blk[0] + 2 * blk[1] + blk[2]]
            return pltpu.make_async_remote_copy(src_ref=dst if src is None else src, dst_ref=dst,
                                                send_sem=send.at[a, k], recv_sem=recv.at[a, k],
                                                device_id=to, device_id_type=MESH)

        between_chips = [pltpu.make_async_remote_copy(src_ref=cins[a].at[2 * ch[0] + ch[1]], dst_ref=couts[a].at[j],
                                                      send_sem=csend.at[a, j], recv_sem=crecv.at[a, j],
                                                      device_id=(*ch, c), device_id_type=MESH)
                         for a in range(nc) for j, ch in enumerate(chips)]
        mine = [pltpu.make_async_copy(ins[a], outs[a].at[4 * x + 2 * y + c], loc.at[a]) for a in range(n)]
        first = []
        for a in range(n):
            first.append(cp(a, 0, me, sib, src=ins[a]))
            first += [cp(a, 1 + j, me, (*ch, c), src=ins[a]) for j, ch in enumerate(chips)]
        for f in mine + first + between_chips:
            f.start()
        passed = []
        for a in range(n):
            for j, ch in enumerate(chips):
                cp(a, 1 + j, (*ch, c), me).wait_recv()
                passed.append(cp(a, 4 + j, (*ch, c), sib))
                passed[-1].start()
        for a in range(n):
            cp(a, 0, sib, me).wait_recv()
            for j, ch in enumerate(chips):
                cp(a, 4 + j, (*ch, 1 - c), me).wait_recv()
        for f in first + passed:
            f.wait_send()
        for m in mine + between_chips:
            m.wait()

    out = pl.pallas_call(
        body, name=name, in_specs=[_HBM] * (n + nc), out_specs=[_HBM] * (n + nc),
        out_shape=[jax.ShapeDtypeStruct((N_DEV,) + a.shape, a.dtype) for a in arrs]
        + [jax.ShapeDtypeStruct((3,) + a.shape[1:], a.dtype) for a in chip_blocks],
        scratch_shapes=[pltpu.SemaphoreType.DMA((n, 7)), pltpu.SemaphoreType.DMA((n, 7)),
                        pltpu.SemaphoreType.DMA((n,)), pltpu.SemaphoreType.DMA((max(nc, 1), 3)),
                        pltpu.SemaphoreType.DMA((max(nc, 1), 3))],
    )(*arrs, *chip_blocks)
    return (out[:n], out[n:]) if nc else out


def _exchange_cores(name, arrs):
    n = len(arrs)

    def body(*refs):
        ins, outs = refs[:n], refs[n:2 * n]
        send, recv = refs[2 * n:]
        x, y, c, _ = _place()
        remote = [pltpu.make_async_remote_copy(src_ref=ins[a].at[2 * k + 1 - c], dst_ref=outs[a].at[k],
                                               send_sem=send.at[a, k], recv_sem=recv.at[a, k],
                                               device_id=(x, y, 1 - c), device_id_type=MESH)
                  for a in range(n) for k in range(4)]
        for d in remote:
            d.start()
        for d in remote:
            d.wait()

    return pl.pallas_call(
        body, name=name, in_specs=[_HBM] * n, out_specs=[_HBM] * n,
        out_shape=[jax.ShapeDtypeStruct((4,) + a.shape[1:], a.dtype) for a in arrs],
        scratch_shapes=[pltpu.SemaphoreType.DMA((n, 4))] * 2,
    )(*arrs)


_SEM = pl.BlockSpec(memory_space=pltpu.SEMAPHORE)
_EFFECT = pltpu.SideEffectType.DATAFLOW_SIDE_EFFECTING


def _split_exchange(name, srcs, land_shapes, copies_of, n_cp, after=None, all_of=None):
    n = len(srcs)
    in_place = land_shapes is None
    lands = [] if in_place else [pltpu.with_memory_space_constraint(lax.empty(s, a.dtype), pltpu.HBM)
                                 for s, a in zip(land_shapes, srcs)]
    nb = n + len(lands)

    per = 1 if all_of else n_cp
    n_sem = n * per

    def descriptors(refs, send, recv, mirror):
        src_refs, land_refs = refs[:n], refs[nb - n:nb]
        x, y, c, chips = _place()
        out = []
        for a in range(n):
            if mirror and all_of:
                whole = all_of(a, land_refs[a])
                out.append(pltpu.make_async_remote_copy(src_ref=whole, dst_ref=whole, send_sem=send[a],
                                                        recv_sem=recv[a], device_id=(x, y, 1 - c),
                                                        device_id_type=MESH))
                continue
            for k, (src, dst, to, back) in enumerate(copies_of(a, src_refs[a], land_refs[a], x, y, c, chips)):
                out.append(pltpu.make_async_remote_copy(src_ref=src, dst_ref=back if mirror else dst,
                                                        send_sem=send[a * per + k % per],
                                                        recv_sem=recv[a * per + k % per],
                                                        device_id=to, device_id_type=MESH))
        return out

    def start_body(*refs):
        first = nb + (after is not None)
        sems = refs[first:first + 2 * n_sem]
        for d in descriptors(refs, sems[:n_sem], sems[n_sem:], False):
            d.start()
        refs[-1][...] = jnp.zeros((8, 128), F32)

    hbm_like = lambda arrs: [pltpu.HBM(a.shape, a.dtype) for a in arrs]
    order = [] if after is None else [after]
    res = pl.pallas_call(
        start_body, name=name + "_start", in_specs=[_HBM] * nb + [pl.BlockSpec(memory_space=pl.ANY)] * len(order),
        out_specs=[_SEM] * (2 * n_sem) + [_HBM] * nb + [pl.BlockSpec(memory_space=pltpu.VMEM)],
        out_shape=[pltpu.SemaphoreType.DMA(())] * (2 * n_sem)
        + hbm_like(srcs) + hbm_like(lands) + [jax.ShapeDtypeStruct((8, 128), F32)],
        input_output_aliases={i: 2 * n_sem + i for i in range(nb)},
        compiler_params=pltpu.CompilerParams(has_side_effects=_EFFECT),
    )(*srcs, *lands, *order)
    sems, thru, token = res[:2 * n_sem], res[2 * n_sem:2 * n_sem + nb], res[-1]

    def wait(after):
        def wait_body(*refs):
            s = refs[nb:nb + 2 * n_sem]
            for d in descriptors(refs, s[:n_sem], s[n_sem:], True):
                d.wait_send()
                d.wait_recv()

        out = pl.pallas_call(
            wait_body, name=name + "_wait",
            in_specs=[_HBM] * nb + [_SEM] * (2 * n_sem) + [pl.BlockSpec(memory_space=pl.ANY)],
            out_specs=[_HBM] * nb, out_shape=hbm_like(srcs) + hbm_like(lands),
            input_output_aliases={i: i for i in range(nb)},
            compiler_params=pltpu.CompilerParams(has_side_effects=_EFFECT),
        )(*thru, *sems, after)
        return out[:n], out[nb - n:]

    return token, wait


def _gather_copies(a, src, land, x, y, c, chips):
    peers = [(x, y, 1 - c)] + [(*ch, c) for ch in chips]
    me = 4 * x + 2 * y + c
    return [(src, land.at[me], to, land.at[4 * to[0] + 2 * to[1] + to[2]]) for to in peers]


def _to_all_copies(a, src, land, x, y, c, chips):
    peers = [(x, y, 1 - c)] + [(*ch, cc) for ch in chips for cc in (c, 1 - c)]
    me = 4 * x + 2 * y + c
    return [(src, land.at[me], to, land.at[4 * to[0] + 2 * to[1] + to[2]]) for to in peers]


def _cores_copies(a, src, land, x, y, c, chips):
    return [(src.at[2 * k + 1 - c], land.at[k], (x, y, 1 - c), land.at[k]) for k in range(4)]


def _forward_copies(a, src, land, x, y, c, chips):
    slot = lambda ch, core: 4 * ch[0] + 2 * ch[1] + core
    return [(land.at[slot(ch, c)], land.at[slot(ch, c)], (x, y, 1 - c), land.at[slot(ch, 1 - c)]) for ch in chips]


def _chips_copies(a, src, land, x, y, c, chips):
    return [(src.at[2 * ch[0] + ch[1]], land.at[j], (*ch, c), land.at[j]) for j, ch in enumerate(chips)]


def _gather_forward(name, arrs):
    n = len(arrs)

    def body(*refs):
        ins, outs = refs[:n], refs[n:2 * n]
        send, recv = refs[2 * n:]
        x, y, c, chips = _place()
        sends, recvs = [], []
        for a in range(n):
            for j, ch in enumerate(chips):
                mine, theirs = 4 * ch[0] + 2 * ch[1] + c, 4 * ch[0] + 2 * ch[1] + 1 - c
                sends.append(pltpu.make_async_remote_copy(src_ref=ins[a].at[mine], dst_ref=outs[a].at[mine],
                                                          send_sem=send.at[a, j], recv_sem=recv.at[a, j],
                                                          device_id=(x, y, 1 - c), device_id_type=MESH))
                recvs.append(pltpu.make_async_remote_copy(src_ref=ins[a].at[theirs], dst_ref=outs[a].at[theirs],
                                                          send_sem=send.at[a, j], recv_sem=recv.at[a, j],
                                                          device_id=(x, y, 1 - c), device_id_type=MESH))
        for d in sends:
            d.start()
        for s, r in zip(sends, recvs):
            s.wait_send()
            r.wait_recv()

    return pl.pallas_call(
        body, name=name, in_specs=[_HBM] * n, out_specs=[_HBM] * n,
        out_shape=[jax.ShapeDtypeStruct(a.shape, a.dtype) for a in arrs],
        input_output_aliases={i: i for i in range(n)},
        scratch_shapes=[pltpu.SemaphoreType.DMA((n, 3))] * 2,
    )(*arrs)


def _blocks_of(r, c):
    if r * c * 4 <= (2 << 20):
        return r, c
    if r % 128 == 0:
        return 128, c
    return r, 256


def _pair_sum(name, own, got, core):
    _, r, c = got.shape
    rb, cb = next(d for d in range(r, 0, -16) if r % d == 0 and d * c * 2 <= (3 << 20)), c

    def body(s_ref, a_ref, b_ref, o_ref):
        o_ref[...] = (a_ref[...].astype(F32) + b_ref[...].astype(F32)).astype(o_ref.dtype)

    return pl.pallas_call(
        body, name=name, out_shape=_out_hbm((4, r, c), got.dtype),
        grid_spec=pltpu.PrefetchScalarGridSpec(
            num_scalar_prefetch=1, grid=(4, r // rb, c // cb),
            in_specs=[pl.BlockSpec((None, rb, cb), lambda k, i, j, s: (2 * k + s[0], i, j)),
                      pl.BlockSpec((None, rb, cb), lambda k, i, j, s: (k, i, j))],
            out_specs=pl.BlockSpec((None, rb, cb), lambda k, i, j, s: (k, i, j))),
        compiler_params=_params(3, 6 * _nbytes((rb, cb), got.dtype) + 3 * _nbytes((rb, cb), F32) + (4 << 20)),
    )(core, _in_hbm(own), _in_hbm(got))


def _adamw_math(w, g, m, v):
    m = B1 * m + (1.0 - B1) * g
    v = B2 * v + (1.0 - B2) * (g * g)
    m_hat = m / (1.0 - B1 ** STEP)
    v_hat = v / (1.0 - B2 ** STEP)
    return -LR * (m_hat / (jnp.sqrt(v_hat) + ADAM_EPS) + WD * w), m, v


def _adamw_big(name, own, got, chip, w, m, v, behind):
    r, c = w.shape
    rb, cb = _blocks_of(r, c)

    def body(s_ref, o_ref, p_ref, w_ref, m_ref, v_ref, behind_ref, g_ref, d_ref, nm_ref, nv_ref):
        g = o_ref[...].astype(F32)
        for k in range(3):
            g = g + p_ref[k].astype(F32)
        g_ref[...] = g
        d_ref[...], nm_ref[...], nv_ref[...] = _adamw_math(w_ref[...], g, m_ref[...], v_ref[...])

    blk = pl.BlockSpec((rb, cb), lambda i, j, s: (i, j))
    return pl.pallas_call(
        body, name=name, out_shape=[jax.ShapeDtypeStruct((r, c), F32)] * 4,
        grid_spec=pltpu.PrefetchScalarGridSpec(
            num_scalar_prefetch=1, grid=(r // rb, c // cb),
            in_specs=[pl.BlockSpec((None, rb, cb), lambda i, j, s: (s[0], i, j)),
                      pl.BlockSpec((3, rb, cb), lambda i, j, s: (0, i, j)), blk, blk, blk,
                      pl.BlockSpec(memory_space=pl.ANY)],
            out_specs=[blk] * 4),
        compiler_params=_params(2, 40 << 20),
    )(chip, *map(_in_hbm, (own, got, w, m, v)), behind)


def _adamw_multi(name, items, sums=(), row_blocks=1, own=(), dev=None):
    n, ns = len(items), len(sums)
    n_own = len(own)
    assert n_own in (0, n + ns)

    def body(*refs):
        dev_ref = refs[0] if n_own else None
        refs = refs[1:] if n_own else refs
        owns, refs = refs[:n_own], refs[n_own:]
        ins, outs = refs[:4 * n + ns], refs[4 * n + ns:]

        def total(ref, t):
            g = None
            for d in range(ref.shape[0]):
                part = jnp.where(dev_ref[0] == d, owns[t][...], ref[d]) if n_own else ref[d]
                g = part.astype(F32) if g is None else g + part.astype(F32)
            return g

        for t in range(n):
            p_ref, w_ref, m_ref, v_ref = ins[4 * t:4 * t + 4]
            g = total(p_ref, t)
            outs[4 * t][...] = g
            outs[4 * t + 1][...], outs[4 * t + 2][...], outs[4 * t + 3][...] = _adamw_math(
                w_ref[...], g, m_ref[...], v_ref[...])
        for t in range(ns):
            outs[4 * n + t][...] = total(ins[4 * n + t], n + t)

    def spec(shape, lead):
        blk = (shape[0] // row_blocks,) + tuple(shape[1:])
        nd = len(shape)
        if lead:
            return pl.BlockSpec((lead,) + blk, lambda i: (0, i) + (0,) * (nd - 1))
        return pl.BlockSpec(blk, lambda i: (i,) + (0,) * (nd - 1))

    operands, in_specs, out_specs, out_shape = [], [], [], []
    if n_own:
        operands += [dev] + list(own)
        in_specs += [pl.BlockSpec(memory_space=pltpu.SMEM)] + [spec(o.shape, 0) for o in own]
    for parts, w, m, v in items:
        assert parts.shape[1:] == w.shape == m.shape == v.shape, (name, parts.shape, w.shape)
        operands += [parts, w, m, v]
        in_specs += [spec(w.shape, parts.shape[0])] + [spec(w.shape, 0)] * 3
        out_specs += [spec(w.shape, 0)] * 4
        out_shape += [jax.ShapeDtypeStruct(w.shape, F32)] * 4
    for parts in sums:
        operands.append(parts)
        in_specs.append(spec(parts.shape[1:], parts.shape[0]))
        out_specs.append(spec(parts.shape[1:], False))
        out_shape.append(jax.ShapeDtypeStruct(parts.shape[1:], F32))
    return pl.pallas_call(
        body, name=name, grid=(row_blocks,), in_specs=in_specs, out_specs=out_specs, out_shape=out_shape,
        compiler_params=_params(1, 56 << 20),
    )(*operands)


def kernel(x, meta_tokens, mix_norm, w_in, lam_re, lam_im, log_dt, b_re, b_im, c_re, c_im, d_skip, w_glu, b_glu, q_a_norm, w_q_b, kv_a_norm, w_kv_b, out_norm_ssm, out_norm_attn, w_out, ffn_norm, w_up, conv_w, conv_b, w_down, final_norm, loss_target, m_meta_tokens, m_mix_norm, m_w_in, m_lam_re, m_lam_im, m_log_dt, m_b_re, m_b_im, m_c_re, m_c_im, m_d_skip, m_w_glu, m_b_glu, m_q_a_norm, m_w_q_b, m_kv_a_norm, m_w_kv_b, m_out_norm_ssm, m_out_norm_attn, m_w_out, m_ffn_norm, m_w_up, m_conv_w, m_conv_b, m_w_down, m_final_norm, v_meta_tokens, v_mix_norm, v_w_in, v_lam_re, v_lam_im, v_log_dt, v_b_re, v_b_im, v_c_re, v_c_im, v_d_skip, v_w_glu, v_b_glu, v_q_a_norm, v_w_q_b, v_kv_a_norm, v_w_kv_b, v_out_norm_ssm, v_out_norm_attn, v_w_out, v_ffn_norm, v_w_up, v_conv_w, v_conv_b, v_w_down, v_final_norm):
    given = dict(locals())
    w = {n: given[n] for n in WEIGHTS}
    m = {n: given['m_' + n] for n in WEIGHTS}
    v = {n: given['v_' + n] for n in WEIGHTS}
    dev = 4 * lax.axis_index("x") + 2 * lax.axis_index("y") + lax.axis_index("c")

    shard = {
        'w_in': jnp.pad(w_in[0], ((0, 0), (0, D_IN_PAD - D_IN))),
        'w_glu': w_glu[0],
        'w_q_b': jnp.pad(w_q_b[0], ((0, 0), (0, HEAD_PAD - QK_NOPE - QK_ROPE))),
        'w_kv_b': w_kv_b[0],
        'w_out': w_out[0],
        'w_up': w_up[0],
        'w_down': w_down[0],
    }
    core = lax.axis_index("c").astype(jnp.int32).reshape(1)
    chip = (2 * lax.axis_index("x") + lax.axis_index("y")).astype(jnp.int32).reshape(1)
    mid, late = ['w_q_b', 'w_kv_b', 'w_glu'], ['w_out', 'w_up', 'w_down']
    shard = {n: a.astype(BF16) for n, a in shard.items()}

    gathered = _all_gather("gather_early", [shard['w_in'], meta_tokens, jnp.pad(conv_w[0], ((0, 5), (0, 0)))])
    wf = {'w_in': gathered[0].reshape(-1, D_IN_PAD)}
    meta = gathered[1].transpose(1, 0, 2).reshape(N_META, D)
    wf['conv_w'] = gathered[2][:, :3].transpose(1, 0, 2).reshape(3, D_FF)
    mid_token, wait_mid = _split_exchange("gather_mid", [shard[n] for n in mid],
                                          [(N_DEV,) + shard[n].shape for n in mid], _gather_copies, 4,
                                          after=gathered[0])
    gather_token, late_waits = mid_token, {}
    for n in late:
        gather_token, late_waits[n] = _split_exchange("gather_" + n, [shard[n]], [(N_DEV,) + shard[n].shape],
                                                      _gather_copies, 4, after=gather_token)

    def with_mine(mine, landed):
        return [lax.dynamic_update_slice(a, s[None], (dev, 0, 0)) for s, a in zip(mine, landed)]

    def mid_weights(after):
        full = _gather_forward("gather_forward_mid", with_mine(*wait_mid(after)))
        return {'w_q_b': full[0], 'w_kv_b': full[1], 'w_glu': full[2].reshape(-1, D_SSM)}

    def late_weights(after):
        full_out, = _gather_forward("gather_forward_w_out", with_mine(*late_waits['w_out'](after)))
        token, wait_up = _split_exchange("gather_forward_w_up", with_mine(*late_waits['w_up'](after)), None,
                                         _forward_copies, 3, after=full_out)

        def up_weights(after):
            full_up, = wait_up(after)[0]
            token, wait_down = _split_exchange("gather_forward_w_down", with_mine(*late_waits['w_down'](after)),
                                               None, _forward_copies, 3, after=full_up)
            return full_up, token, lambda after: wait_down(after)[0][0].reshape(-1, D)

        return full_out.reshape(-1, D), token, up_weights

    pending, between_cores = [], {}

    def send_grads(names, arg, stage):
        if stage == 'cores':
            token, between_cores[names[0]] = _split_exchange(
                "reduce_cores_" + names[0], arg, [(4,) + a.shape[1:] for a in arg], _cores_copies, 4)
            return token
        mine, got = between_cores.pop(names[0])(arg) if stage == 'chips' else (
            arg, _exchange_cores("reduce_cores_" + names[0], arg))
        sums = [_pair_sum("sum_cores_" + n, a, g, core) for n, a, g in zip(names, mine, got)]
        token, wait = _split_exchange("reduce_chips_" + names[0], sums, [(3,) + s.shape[1:] for s in sums],
                                      _chips_copies, 3)
        pending.append((names, wait))
        return token

    small_sent = []
    s5_bc = ['b_re', 'b_im', 'c_re', 'c_im']

    def send_small(sm, dcw4, loss_part):
        names = [n for n in SMALL if n in sm]
        arrs = [sm[n].reshape(-1, 128).astype(BF16) if n in s5_bc else sm[n] for n in names] + [dcw4, loss_part]
        token, wait = _split_exchange("gather_small", arrs, [(N_DEV,) + a.shape for a in arrs], _to_all_copies, 7,
                                      all_of=lambda a, land: land.at[pl.ds(0, N_DEV - 1)])
        small_sent.append((names, wait))
        return token

    small = {n: w[n] for n in SMALL}
    small['mix_norm'] = mix_norm + gather_token[0:1, 0:1]
    grad_x, dmeta, d_mix_norm, big = _local_step(x[0], loss_target[0], meta, small, wf, mid_weights, late_weights,
                                                  send_grads, send_small)
    grads, deltas, new_m, new_v = {}, {}, {}, {}

    def keep(n, outs):
        grads[n], deltas[n], new_m[n], new_v[n] = (o.reshape(w[n].shape) for o in outs)

    rest = list(big)
    from_core = _exchange_cores("reduce_cores", [big[n] for n in rest])
    chip_sums = [_pair_sum("sum_cores_" + n, big[n], got, core) for n, got in zip(rest, from_core)]
    last = chip_sums + [d_mix_norm, dmeta]
    n_big = len(chip_sums)
    last_token, wait_last = _split_exchange(
        "reduce_last", last, [(3,) + s.shape[1:] for s in chip_sums] + [(N_DEV,) + a.shape for a in last[n_big:]],
        lambda a, *at: _chips_copies(a, *at) if a < n_big else _to_all_copies(a, *at), 7,
        all_of=lambda a, land: land if a < n_big else land.at[pl.ds(0, N_DEV - 1)])

    def update(n, own, got):
        if n == 'w_up':
            outs = _adamw_big("adamw_" + n, own, got, chip, w[n][0].T, m[n][0].T, v[n][0].T, last_token)
            return keep(n, [o.T for o in outs])
        cols = w[n].shape[2]
        keep(n, _adamw_big("adamw_" + n, own[:, :, :cols], got[:, :, :cols], chip, w[n][0], m[n][0], v[n][0],
                           last_token))

    for names, wait in pending:
        for n, own, got in zip(names, *wait(chip_sums[0])):
            update(n, own, got)
    updated = [new_v[n].reshape(-1)[:1] for names, _ in pending for n in names]
    sent, landed = wait_last(functools.reduce(jnp.add, updated))
    for n, own, got in zip(rest, sent, landed):
        update(n, own, got)

    device = dev.astype(jnp.int32).reshape(1)
    outs = _adamw_multi("adamw_last", [(landed[n_big], mix_norm, m['mix_norm'], v['mix_norm'])],
                        sums=[landed[n_big + 1]], own=sent[n_big:], dev=device)
    keep('mix_norm', outs[:4])
    g_meta = outs[4]
    names, wait = small_sent[0]
    mine, landed = wait(grads['mix_norm'])
    parts, own = dict(zip(names + ['conv_w', 'loss'], landed)), dict(zip(names + ['conv_w', 'loss'], mine))
    wide = [n for n in names if n not in s5_bc]
    summed = ['conv_w', 'loss'] + s5_bc
    outs = _adamw_multi("adamw_small",
                        [(parts[n],) + tuple(t[n].reshape(own[n].shape) for t in (w, m, v)) for n in wide],
                        sums=[parts[n] for n in summed], own=[own[n] for n in wide + summed], dev=device)
    for i, n in enumerate(wide):
        keep(n, outs[4 * i:4 * i + 4])
    g_cw4, loss = outs[-6:-4]
    rows_gc = lambda a, n: (a.reshape(N_GROUPS, SSM_STATE, SSM_GROUP).transpose(0, 2, 1) if n[0] == 'b' else a
                            ).reshape(N_GROUPS * SSM_GROUP, SSM_STATE)
    outs = _adamw_multi("adamw_s5_bc", [(g.reshape(1, N_GROUPS * SSM_GROUP, SSM_STATE),)
                                        + tuple(rows_gc(t[n], n) for t in (w, m, v))
                                        for n, g in zip(s5_bc, outs[-4:])])
    for i, n in enumerate(s5_bc):
        back = outs[4 * i:4 * i + 4]
        if n[0] == 'b':
            back = [o.reshape(N_GROUPS, SSM_GROUP, SSM_STATE).transpose(0, 2, 1) for o in back]
        keep(n, back)

    g_meta = lax.dynamic_slice(g_meta, (0, dev * (D // N_DEV)), (N_META, D // N_DEV))
    g_conv = lax.dynamic_slice(g_cw4.transpose(1, 0, 2).reshape(3, D_FF), (0, dev * (D_FF // N_DEV)),
                               (3, D_FF // N_DEV))
    rows8 = lambda a: jnp.pad(a.reshape(3, D_FF // N_DEV), ((0, 5), (0, 0)))
    outs = _adamw_multi("adamw_cols", [(g_meta[None], meta_tokens, m['meta_tokens'], v['meta_tokens']),
                                       (rows8(g_conv)[None], rows8(conv_w), rows8(m['conv_w']), rows8(v['conv_w']))])
    keep('meta_tokens', outs[:4])
    keep('conv_w', [o[:3] for o in outs[4:]])

    return (loss[0, 0], grad_x[None], *[grads[n] for n in WEIGHTS], *[deltas[n] for n in WEIGHTS],
            *[new_m[n] for n in WEIGHTS], *[new_v[n] for n in WEIGHTS])
```

```python
import functools
import math

import jax
import jax.numpy as jnp
from jax import lax
from jax.experimental import pallas as pl
from jax.experimental.pallas import tpu as pltpu

F32, BF16 = jnp.float32, jnp.bfloat16

N_DEV = 8
N_META, SEQ, D = 16, 2048, 2048
T = N_META + SEQ
TR = 688
EPS = 1e-6
D_SSM, Q_LORA, KV_LORA, QK_ROPE = 1024, 512, 256, 64
D_IN = D_SSM + Q_LORA + KV_LORA + QK_ROPE
D_IN_PAD = 1920
N_HEADS, QK_NOPE, V_HEAD = 8, 128, 128
HEAD_PAD = 256
SM_SCALE = 1.0 / math.sqrt(QK_NOPE + QK_ROPE)
CHUNK = 64
N_GROUPS, SSM_GROUP, SSM_STATE = 64, 16, 64
N_SEG = 8
SEG = 264
TP = N_SEG * SEG
GQ = 8
D_FF = 5504
FF_PIECE = 1376
ROPE_BASE = 10000.0
LR, B1, B2, ADAM_EPS, WD, STEP = 0.001, 0.9, 0.999, 1e-08, 0.01, 10
VMEM_CAP = 60 * 1024 * 1024
MESH = pl.DeviceIdType.MESH

WEIGHTS = ['meta_tokens', 'mix_norm', 'w_in', 'lam_re', 'lam_im', 'log_dt', 'b_re', 'b_im', 'c_re', 'c_im',
           'd_skip', 'w_glu', 'b_glu', 'q_a_norm', 'w_q_b', 'kv_a_norm', 'w_kv_b', 'out_norm_ssm',
           'out_norm_attn', 'w_out', 'ffn_norm', 'w_up', 'conv_w', 'conv_b', 'w_down', 'final_norm']
BIG = ['w_in', 'w_glu', 'w_q_b', 'w_kv_b', 'w_out', 'w_up', 'w_down']
SMALL = [n for n in WEIGHTS if n not in BIG and n not in ('meta_tokens', 'conv_w')]


def _nbytes(shape, dtype):
    return math.prod(shape) * jnp.dtype(dtype).itemsize


def _in_hbm(a):
    return pltpu.with_memory_space_constraint(a, pltpu.HBM) if a.size * a.dtype.itemsize >= (1 << 20) else a


def _out_hbm(shape, dtype):
    big = math.prod(shape) * jnp.dtype(dtype).itemsize >= (1 << 20)
    return pltpu.HBM(shape, dtype) if big else jax.ShapeDtypeStruct(shape, dtype)


def _params(sem, need):
    return pltpu.CompilerParams(dimension_semantics=("arbitrary",) * sem,
                                vmem_limit_bytes=int(min(VMEM_CAP, max(need, 16 * 1024 * 1024))))


_DIMS = {'nn': (((1,), (0,)), ((), ())), 'nt': (((1,), (1,)), ((), ())), 'tn': (((0,), (0,)), ((), ()))}


def _mm(name, a, b, mode, tm, tn, tk, out_dtype, stack=None, res=None):
    sa, sb, so = stack in ('a_out', 'ab_red'), stack in ('b_out', 'ab_red'), stack in ('a_out', 'b_out')
    nj = a.shape[0] if sa else (b.shape[0] if sb else 1)
    a2, b2 = a.shape[-2:], b.shape[-2:]
    if mode == 'nn':
        (m, k), (k2, n) = a2, b2
    elif mode == 'nt':
        (m, k), (n, k2) = a2, b2
    else:
        (k, m), (k2, n) = a2, b2
    assert k == k2 and m % tm == 0 and n % tn == 0 and k % tk == 0, (name, a.shape, b.shape)
    n_jo, n_jr, n_k = (nj if so else 1), (nj if stack == 'ab_red' else 1), k // tk
    grid = (n_jo, m // tm, n // tn, n_jr, n_k)
    a_blk, a_idx = ((tk, tm), lambda i, j, kk: (kk, i)) if mode == 'tn' else ((tm, tk), lambda i, j, kk: (i, kk))
    b_blk, b_idx = ((tn, tk), lambda i, j, kk: (j, kk)) if mode == 'nt' else ((tk, tn), lambda i, j, kk: (kk, j))

    def spec(blk, idx, stacked, on_out):
        if not stacked:
            return pl.BlockSpec(blk, lambda jo, i, j, jr, kk: idx(i, j, kk))
        return pl.BlockSpec((None,) + blk, lambda jo, i, j, jr, kk: ((jo if on_out else jr),) + idx(i, j, kk))

    o_idx = lambda i, j, kk: (i, j)
    in_specs = [spec(a_blk, a_idx, sa, stack == 'a_out'), spec(b_blk, b_idx, sb, stack == 'b_out')]
    operands = [a, b]
    if res is not None:
        in_specs.append(spec((tm, tn), o_idx, False, False))
        operands.append(res)
    n_red = n_jr * n_k
    dims = _DIMS[mode]

    def body(*refs):
        a_ref, b_ref = refs[0], refs[1]
        res_ref = refs[2] if res is not None else None
        o_ref = refs[3] if res is not None else refs[2]
        part = lax.dot_general(a_ref[...].astype(BF16), b_ref[...].astype(BF16), dims,
                               preferred_element_type=F32)

        def finish(total):
            if res_ref is not None:
                total = total + res_ref[...]
            o_ref[...] = total.astype(o_ref.dtype)

        if n_red == 1:
            finish(part)
        else:
            acc_ref = refs[-1]
            step = pl.program_id(3) * n_k + pl.program_id(4)

            @pl.when(step == 0)
            def _():
                acc_ref[...] = part

            @pl.when(step > 0)
            def _():
                acc_ref[...] += part

            @pl.when(step == n_red - 1)
            def _():
                finish(acc_ref[...])

    out_shape = ((nj,) if so else ()) + (m, n)
    need = 2 * (_nbytes(a_blk, a.dtype) + _nbytes(b_blk, b.dtype) + _nbytes((tm, tn), out_dtype)
                + (_nbytes((tm, tn), F32) if res is not None else 0)) + 3 * _nbytes((tm, tn), F32) + (4 << 20)
    return pl.pallas_call(
        body, name=name, grid=grid, in_specs=in_specs, out_specs=spec((tm, tn), o_idx, so, True),
        out_shape=_out_hbm(out_shape, out_dtype),
        scratch_shapes=[pltpu.VMEM((tm, tn), F32)] if n_red > 1 else [],
        compiler_params=_params(5, need),
    )(*map(_in_hbm, operands))


def _rowwise(name, fn, row_ins, const_ins, row_outs, acc_outs, tr=TR):
    n_ri, n_ci, n_ro, n_ao = len(row_ins), len(const_ins), len(row_outs), len(acc_outs)

    def body(*refs):
        ri, ci = refs[:n_ri], refs[n_ri:n_ri + n_ci]
        ro, ao = refs[n_ri + n_ci:n_ri + n_ci + n_ro], refs[n_ri + n_ci + n_ro:]
        i = pl.program_id(0)
        tiles = [jnp.where(i == 0, pltpu.roll(r[...], lead, 0), r[...]) if lead else r[...]
                 for r, lead in zip(ri, leads)]
        outs = fn(i * tr, *tiles, *[r[...] for r in ci])
        for r, v in zip(ro, outs[:n_ro]):
            r[...] = v.astype(r.dtype)
        if n_ao:
            @pl.when(i == 0)
            def _():
                for r, v in zip(ao, outs[n_ro:]):
                    r[...] = v

            @pl.when(i > 0)
            def _():
                for r, v in zip(ao, outs[n_ro:]):
                    r[...] += v

    def row_spec(a, w, cb, lead=0):
        if not lead:
            return pl.BlockSpec((tr, w), lambda i: (i, cb))
        assert a.shape[0] + lead == T and lead % 8 == 0 and tr % 8 == 0
        return pl.BlockSpec((pl.Element(tr), pl.Element(w)),
                            lambda i: (jnp.maximum(i * (tr // 8) - lead // 8, 0) * 8, cb * w))

    in_specs = [row_spec(*r) for r in row_ins]
    leads = [r[3] if len(r) > 3 else 0 for r in row_ins]
    row_ins = [r[:3] for r in row_ins]
    in_specs += [pl.BlockSpec(c.shape, functools.partial(lambda nd, i: (0,) * nd, c.ndim)) for c in const_ins]
    out_specs = [pl.BlockSpec((tr, w), lambda i: (i, 0)) for w, _ in row_outs]
    out_specs += [pl.BlockSpec((1, w), lambda i: (0, 0)) for w in acc_outs]
    out_shape = [_out_hbm((T, w), dt) for w, dt in row_outs]
    out_shape += [jax.ShapeDtypeStruct((1, w), F32) for w in acc_outs]
    need = 2 * (sum(_nbytes((tr, w), a.dtype) for a, w, _ in row_ins) + sum(_nbytes((tr, w), dt) for w, dt in row_outs))
    need = 3 * need + (8 << 20)
    return pl.pallas_call(
        body, name=name, grid=(T // tr,), in_specs=in_specs, out_specs=out_specs, out_shape=out_shape,
        compiler_params=_params(1, need),
    )(*[_in_hbm(a) for a, _, _ in row_ins], *const_ins)


def _rms_fwd_fn(row0, h, g):
    h = h.astype(F32)
    r = lax.rsqrt(jnp.mean(h * h, axis=-1, keepdims=True) + EPS)
    return (h * r) * g, r


def _rms_mix_fn(row0, x, meta, g):
    h = jnp.where(row0 == 0, jnp.concatenate([meta, x[N_META:]], axis=0), x)
    return (h,) + _rms_fwd_fn(row0, h, g)


def _rms_bwd(dxn, h, r, g):
    xhat = h * r
    dxg = dxn * g
    dh = r * (dxg - xhat * jnp.mean(dxg * xhat, axis=-1, keepdims=True))
    return dh, jnp.sum(dxn * xhat, axis=0, keepdims=True)


def _rms_bwd_fn(row0, dxn, h, r, g):
    return _rms_bwd(dxn.astype(F32), h.astype(F32), r, g)


def _rms_bwd_res_fn(row0, dxn, h, r, res, g):
    dh, dg = _rms_bwd(dxn.astype(F32), h.astype(F32), r, g)
    return dh + res, dg


def _rms_bwd_res_twice_fn(*args):
    dh, dg = _rms_bwd_res_fn(*args)
    return dh, dh, dg


def _gelu_fn(row0, y):
    return (jax.nn.gelu(y),)


def _glu_fwd_fn(row0, y, gl, b, gain):
    ya = jax.nn.gelu(y) * jax.nn.sigmoid(gl + b)
    return _rms_fwd_fn(row0, ya, gain)


def _glu_bwd_fn(row0, dyan, y, gl, ra, b, gain):
    g = jax.nn.gelu(y)
    s = jax.nn.sigmoid(gl + b)
    dya, dgain = _rms_bwd(dyan, g * s, ra, gain)
    dgl = dya * g * s * (1.0 - s)
    return dgl, dya * s, dgain, jnp.sum(dgl, axis=0, keepdims=True)


def _gelu_bwd_fn(row0, dg, y):
    return (jax.vjp(jax.nn.gelu, y)[1](dg)[0],)


def _loss_fn(row0, h, tgt, gain):
    r = lax.rsqrt(jnp.mean(h * h, axis=-1, keepdims=True) + EPS)
    xhat = h * r
    rows = row0 + lax.broadcasted_iota(jnp.int32, h.shape, 0)
    err = jnp.where(rows >= N_META, xhat * gain - tgt, 0.0)
    loss = jnp.full((1, 128), jnp.sum(err * err) * (0.5 / D), F32)
    dh, dgain = _rms_bwd(err * (1.0 / D), h, r, gain)
    return dh, dh, dgain, loss


def _rope_tables():
    pos = jnp.arange(T, dtype=jnp.int32)
    inv_freq = 1.0 / (ROPE_BASE ** (jnp.arange(0, QK_ROPE, 2, dtype=F32) / QK_ROPE))
    ang = pos.astype(F32)[:, None] * inv_freq[None, :]
    cos, sin, z32, z64 = jnp.cos(ang), jnp.sin(ang), jnp.zeros((T, 32), F32), jnp.zeros((T, 64), F32)
    return (jnp.concatenate([cos, cos, z64], 1), jnp.concatenate([-sin, z32, z64], 1),
            jnp.concatenate([z32, sin, z64], 1))


def _rot(x, cc, s1, s2):
    return x * cc + pltpu.roll(x, 96, 1) * s1 + pltpu.roll(x, 32, 1) * s2


def _derot(d, cc, s1, s2):
    return d * cc + pltpu.roll(d * s1, 32, 1) + pltpu.roll(d * s2, 96, 1)


def _chunk_of(pos):
    return jnp.where(pos < N_META, 0, 1 + ((pos - N_META) >> 6))


Q_PARTS = ((0, 352), (352, 336))


def _key_len(last_row):
    last_chunk = 1 + (last_row - N_META) // CHUNK
    return min(T, pl.cdiv(N_META + last_chunk * CHUNK, 128) * 128)


def _attn_probs(t, q_ref, kv_ref, kpe_ref, tq, tk):
    kl_all = _key_len(t * TR + TR - 1)
    kn_all = kv_ref[:kl_all, :QK_NOPE].astype(BF16)
    v_all = kv_ref[:kl_all, QK_NOPE:].astype(BF16)
    kp_all = _rot(kpe_ref[:kl_all, :], tk[0][:kl_all, :], tk[1][:kl_all, :], tk[2][:kl_all, :]).astype(BF16)
    for r0, nr in Q_PARTS:
        rows, kl = slice(r0, r0 + nr), _key_len(t * TR + r0 + nr - 1)
        kn, kp, v = kn_all[:kl], kp_all[:kl], v_all[:kl]
        qn = q_ref[rows, :QK_NOPE].astype(BF16)
        qp = _rot(q_ref[rows, QK_NOPE:], tq[0][rows, :], tq[1][rows, :], tq[2][rows, :]).astype(BF16)
        s = lax.dot_general(qn, kn, _DIMS['nt'], preferred_element_type=F32)
        s = s + lax.dot_general(qp, kp, _DIMS['nt'], preferred_element_type=F32)
        qc = _chunk_of(t * TR + r0 + lax.broadcasted_iota(jnp.int32, (nr, 1), 0))
        kc = _chunk_of(lax.broadcasted_iota(jnp.int32, (1, kl), 1))
        s = jnp.where(kc <= qc, s * SM_SCALE, jnp.finfo(F32).min)
        e = jnp.exp(s - jnp.max(s, axis=-1, keepdims=True))
        yield rows, kl, qn, qp, kn, kp, v, e, 1.0 / jnp.sum(e, axis=-1, keepdims=True)


def _per_q_tile(fn):
    for t in range(T // TR):
        pl.when(pl.program_id(1) == t)(functools.partial(fn, t))


def _attn_specs(z, tabs):
    q_spec = pl.BlockSpec((None, TR, HEAD_PAD), lambda h, i: (h, i, 0))
    kv_spec = pl.BlockSpec((None, T, HEAD_PAD), lambda h, i: (h, 0, 0))
    kpe_spec = pl.BlockSpec((T, 128), lambda h, i: (0, (D_IN - QK_ROPE) // 128))
    tq = [pl.BlockSpec((TR, 128), lambda h, i: (i, 0))] * 3
    tk = [pl.BlockSpec((T, 128), lambda h, i: (0, 0))] * 3
    return [q_spec, kv_spec, kpe_spec] + tq + tk


def _attn_fwd(q, kv, z, tabs):
    def body(q_ref, kv_ref, kpe_ref, c1, c2, c3, k1, k2, k3, o_ref):
        def tile(t):
            for rows, _, _, _, _, _, v, e, inv in _attn_probs(t, q_ref, kv_ref, kpe_ref, (c1, c2, c3), (k1, k2, k3)):
                o_ref[rows, :] = jnp.dot(e.astype(BF16), v, preferred_element_type=F32) * inv

        _per_q_tile(tile)

    return pl.pallas_call(
        body, name="attn_fwd", grid=(N_HEADS, T // TR), in_specs=_attn_specs(z, tabs),
        out_specs=pl.BlockSpec((TR, V_HEAD), lambda h, i: (i, h)),
        out_shape=_out_hbm((T, N_HEADS * V_HEAD), F32),
        compiler_params=_params(2, 48 << 20),
    )(_in_hbm(q), _in_hbm(kv), _in_hbm(z), *tabs, *tabs)


def _attn_bwd(q, kv, z, tabs, dyb):
    def body(q_ref, kv_ref, kpe_ref, c1, c2, c3, k1, k2, k3, do_ref, dq_ref, dkv_ref, dkpe_ref):
        @pl.when(pl.program_id(1) == 0)
        def _():
            dkv_ref[...] = jnp.zeros(dkv_ref.shape, F32)
            dkpe_ref[...] = jnp.zeros(dkpe_ref.shape, F32)

        def tile(t):
            for rows, kl, qn, qp, kn, kp, v, e, inv in _attn_probs(t, q_ref, kv_ref, kpe_ref, (c1, c2, c3),
                                                                  (k1, k2, k3)):
                do = do_ref[rows, :]
                dp = lax.dot_general(do.astype(BF16), v, _DIMS['nt'], preferred_element_type=F32)
                delta = jnp.sum(e * dp, axis=-1, keepdims=True) * inv
                ds = (e * ((dp - delta) * (inv * SM_SCALE))).astype(BF16)
                pb, do = e.astype(BF16), (do * inv).astype(BF16)
                dq_ref[rows, :QK_NOPE] = jnp.dot(ds, kn, preferred_element_type=F32).astype(dq_ref.dtype)
                dqp = _derot(jnp.dot(ds, kp, preferred_element_type=F32), c1[rows, :], c2[rows, :], c3[rows, :])
                dq_ref[rows, QK_NOPE:] = dqp.astype(dq_ref.dtype)
                dkv_ref[:kl, :QK_NOPE] += lax.dot_general(ds, qn, _DIMS['tn'], preferred_element_type=F32)
                dkv_ref[:kl, QK_NOPE:] += lax.dot_general(pb, do, _DIMS['tn'], preferred_element_type=F32)
                dkpe_ref[:kl, :] += lax.dot_general(ds, qp, _DIMS['tn'], preferred_element_type=F32)

        _per_q_tile(tile)

    return pl.pallas_call(
        body, name="attn_bwd", grid=(N_HEADS, T // TR),
        in_specs=_attn_specs(z, tabs) + [pl.BlockSpec((TR, V_HEAD), lambda h, i: (i, h))],
        out_specs=[pl.BlockSpec((None, TR, HEAD_PAD), lambda h, i: (h, i, 0)),
                   pl.BlockSpec((None, T, HEAD_PAD), lambda h, i: (h, 0, 0)),
                   pl.BlockSpec((None, T, 128), lambda h, i: (h, 0, 0))],
        out_shape=[_out_hbm((N_HEADS, T, HEAD_PAD), BF16), _out_hbm((N_HEADS, T, HEAD_PAD), F32),
                   _out_hbm((N_HEADS, T, 128), F32)],
        compiler_params=_params(2, 56 << 20),
    )(_in_hbm(q), _in_hbm(kv), _in_hbm(z), *tabs, *tabs, _in_hbm(dyb))


def _kpe_bwd(dkpe, tabs):
    def body(d_ref, c1, c2, c3, o_ref):
        d = d_ref[0]
        for h in range(1, N_HEADS):
            d = d + d_ref[h]
        o_ref[...] = _derot(d, c1[...], c2[...], c3[...]).astype(o_ref.dtype)

    tile = pl.BlockSpec((TR, 128), lambda i: (i, 0))
    return pl.pallas_call(
        body, name="kpe_bwd", grid=(T // TR,),
        in_specs=[pl.BlockSpec((N_HEADS, TR, 128), lambda i: (0, i, 0)), tile, tile, tile], out_specs=tile,
        out_shape=jax.ShapeDtypeStruct((T, 128), BF16), compiler_params=_params(1, 16 << 20),
    )(_in_hbm(dkpe), *tabs)


def _s5_prep_math(lam_re, lam_im, log_dt, bt_re, bt_im):
    dt = jnp.exp(log_dt)
    mag = jnp.exp(lam_re * dt)
    a_re, a_im = mag * jnp.cos(lam_im * dt), mag * jnp.sin(lam_im * dt)
    den = lam_re * lam_re + lam_im * lam_im
    w_re = ((a_re - 1.0) * lam_re + a_im * lam_im) / den
    w_im = (a_im * lam_re - (a_re - 1.0) * lam_im) / den
    return a_re, a_im, w_re[None] * bt_re - w_im[None] * bt_im, w_re[None] * bt_im + w_im[None] * bt_re


def _eye_groups():
    return (lax.broadcasted_iota(jnp.int32, (N_GROUPS, N_GROUPS), 0)
            == lax.broadcasted_iota(jnp.int32, (N_GROUPS, N_GROUPS), 1)).astype(F32)


def _row_to_col(row):
    return jnp.sum(_eye_groups() * row, axis=1, keepdims=True)


def _s5_prep(lam_re, lam_im, log_dt, bt_re, bt_im):
    def body(lr, li, ld, br, bi, ar, ai, bbr, bbi):
        ar[...], ai[...], bbr[...], bbi[...] = _s5_prep_math(lr[...], li[...], _row_to_col(ld[...]), br[...], bi[...])

    gp, cgp = jax.ShapeDtypeStruct((N_GROUPS, SSM_STATE), F32), jax.ShapeDtypeStruct(bt_re.shape, F32)
    return pl.pallas_call(body, name="s5_prep", out_shape=[gp, gp, cgp, cgp])(lam_re, lam_im, log_dt, bt_re, bt_im)


def _s5_prep_bwd(lam_re, lam_im, log_dt, bt_re, bt_im, da_re, da_im, dbb_re, dbb_im):
    def body(lr, li, ld, br, bi, dar, dai, dbr, dbi, o1, o2, o3, o4, o5):
        _, vjp = jax.vjp(_s5_prep_math, lr[...], li[...], _row_to_col(ld[...]), br[...], bi[...])
        o1[...], o2[...], dld, o4[...], o5[...] = vjp((dar[...], dai[...], dbr[...], dbi[...]))
        o3[...] = jnp.sum(_eye_groups() * dld, axis=0, keepdims=True)

    ins = (lam_re, lam_im, log_dt, bt_re, bt_im)
    return pl.pallas_call(body, name="s5_prep_bwd", out_shape=[jax.ShapeDtypeStruct(a.shape, F32) for a in ins])(
        *ins, da_re, da_im, dbb_re, dbb_im)


def _cmul(ar, ai, br, bi):
    return ar * br - ai * bi, ar * bi + ai * br


def _seg_rows(i):
    return pl.ds(pl.multiple_of(i * N_SEG, N_SEG), N_SEG)


def _scan(xr, xi, ar, ai, reverse):
    zero = jnp.zeros(ar.shape, F32)

    def local(j, carry):
        rows = _seg_rows(SEG - 1 - j if reverse else j)
        nr, ni = _cmul(ar, ai, *carry)
        nr, ni = nr + xr[rows, :], ni + xi[rows, :]
        xr[rows, :], xi[rows, :] = nr, ni
        return nr, ni

    er, ei = lax.fori_loop(0, SEG, local, (zero, zero))
    power, square, bits = None, (ar, ai), SEG
    while bits:
        if bits & 1:
            power = square if power is None else _cmul(*power, *square)
        square, bits = _cmul(*square, *square), bits >> 1
    pr, pi = power
    row = lax.broadcasted_iota(jnp.int32, ar.shape, 0)
    edge, shift = (N_SEG - 1, N_SEG - 1) if reverse else (0, 1)
    hr, hi = zero, zero
    for _ in range(N_SEG - 1):
        tr_, ti_ = _cmul(pr, pi, hr, hi)
        hr = jnp.where(row == edge, 0.0, pltpu.roll(tr_ + er, shift, 0))
        hi = jnp.where(row == edge, 0.0, pltpu.roll(ti_ + ei, shift, 0))

    def fix(j, carry):
        rows = _seg_rows(SEG - 1 - j if reverse else j)
        xr[rows, :] += carry[0]
        xi[rows, :] += carry[1]
        return _cmul(ar, ai, *carry)

    lax.fori_loop(0, SEG, fix, _cmul(ar, ai, hr, hi))
    return hr, hi


def _split(x):
    hi = x.astype(BF16)
    return hi, (x - hi.astype(F32)).astype(BF16)


def _dot2(xs, ys, mode='nn'):
    d = lambda p, q: lax.dot_general(p, q, _DIMS[mode], preferred_element_type=F32)
    return d(xs[0], ys[0]) + d(xs[1], ys[0])


def _group_masks():
    ns = GQ * SSM_STATE
    rep = (lax.broadcasted_iota(jnp.int32, (SSM_STATE, ns), 1) % SSM_STATE
           == lax.broadcasted_iota(jnp.int32, (SSM_STATE, ns), 0)).astype(BF16)
    own = (lax.broadcasted_iota(jnp.int32, (128, ns), 0) // SSM_GROUP
           == lax.broadcasted_iota(jnp.int32, (128, ns), 1) // SSM_STATE)
    return rep, own


def _block_diag(x, rep, own):
    dot = lambda p: jnp.dot(p, rep, preferred_element_type=F32)
    return tuple(jnp.where(own, dot(part), 0.0).astype(BF16) for part in _split(x))


def _block_rows(full, rep, own):
    masked = _split(jnp.where(own, full, 0.0))
    dot = lambda p: lax.dot_general(p, rep, _DIMS['nt'], preferred_element_type=F32)
    return dot(masked[0]) + dot(masked[1])


def _s5_specs():
    col = pl.BlockSpec((TP, 128), lambda q: (0, q))
    a_spec = pl.BlockSpec((None, 2, N_SEG, GQ * SSM_STATE), lambda q: (q, 0, 0, 0))
    rows = pl.BlockSpec((128, SSM_STATE), lambda q: (q, 0))
    d_spec = pl.BlockSpec((1, 128), lambda q: (0, q))
    return col, a_spec, rows, d_spec


def _s5_fwd(u, a8, bb_re, bb_im, c_re, c_im, dsk):
    def body(u_ref, a_ref, bre_ref, bim_ref, cre_ref, cim_ref, d_ref, y_ref, hre, him):
        u_ = u_ref[...]
        us = _split(u_)
        rep, own = _group_masks()
        hre[...] = _dot2(us, _block_diag(bre_ref[...], rep, own))
        him[...] = _dot2(us, _block_diag(bim_ref[...], rep, own))
        _scan(hre, him, a_ref[0], a_ref[1], reverse=False)
        dot = lambda h, c: lax.dot_general(h[...].astype(BF16), _block_diag(c[...], rep, own)[0], _DIMS['nt'],
                                           preferred_element_type=F32)
        y_ref[...] = dot(hre, cre_ref) - dot(him, cim_ref) + d_ref[...] * u_

    col, a_spec, rows, d_spec = _s5_specs()
    return pl.pallas_call(
        body, name="s5_fwd", grid=(N_GROUPS // GQ,), in_specs=[col, a_spec, rows, rows, rows, rows, d_spec],
        out_specs=col, out_shape=_out_hbm((TP, D_SSM), F32),
        scratch_shapes=[pltpu.VMEM((TP, GQ * SSM_STATE), F32)] * 2,
        compiler_params=_params(1, 40 << 20),
    )(_in_hbm(u), a8, bb_re, bb_im, c_re, c_im, dsk)


def _s5_bwd(u, dy, a8, bb_re, bb_im, c_re, c_im, dsk):
    def body(u_ref, dy_ref, a_ref, bre_ref, bim_ref, cre_ref, cim_ref, d_ref,
             du_ref, dbre_ref, dbim_ref, dcre_ref, dcim_ref, da_ref, dd_ref, hre, him, lre, lim):
        u_, dy_ = u_ref[...], dy_ref[...]
        us, dys = _split(u_), _split(dy_)
        ar, ai = a_ref[0], a_ref[1]
        rep, own = _group_masks()
        bres, bims = _block_diag(bre_ref[...], rep, own), _block_diag(bim_ref[...], rep, own)
        hre[...] = _dot2(us, bres)
        him[...] = _dot2(us, bims)
        h0r, h0i = _scan(hre, him, ar, ai, reverse=False)
        lre[...] = _dot2(dys, _block_diag(cre_ref[...], rep, own))
        lim[...] = -_dot2(dys, _block_diag(cim_ref[...], rep, own))
        _scan(lre, lim, ar, -ai, reverse=True)

        def acc_da(i, carry):
            lr, li = lre[_seg_rows(i), :], lim[_seg_rows(i), :]
            pr, pi = hre[_seg_rows(i - 1), :], him[_seg_rows(i - 1), :]
            return carry[0] + lr * pr + li * pi, carry[1] + li * pr - lr * pi

        lr, li = lre[_seg_rows(0), :], lim[_seg_rows(0), :]
        dar, dai = lax.fori_loop(1, SEG, acc_da, (lr * h0r + li * h0i, li * h0r - lr * h0i))
        da_ref[0:1, :] = jnp.sum(dar, axis=0, keepdims=True)
        da_ref[1:2, :] = jnp.sum(dai, axis=0, keepdims=True)
        dot = lambda p, q, mode: lax.dot_general(p, q, _DIMS[mode], preferred_element_type=F32)
        lr1, li1 = lre[...].astype(BF16), lim[...].astype(BF16)
        du_ref[...] = dy_ * d_ref[...] + dot(lr1, bres[0], 'nt') + dot(li1, bims[0], 'nt')
        dbre_ref[...] = _block_rows(dot(us[0], lr1, 'tn'), rep, own)
        dbim_ref[...] = _block_rows(dot(us[0], li1, 'tn'), rep, own)
        dcre_ref[...] = _block_rows(dot(dys[0], hre[...].astype(BF16), 'tn'), rep, own)
        dcim_ref[...] = -_block_rows(dot(dys[0], him[...].astype(BF16), 'tn'), rep, own)
        dd_ref[...] = jnp.sum(dy_ * u_, axis=0, keepdims=True)

    col, a_spec, rows, d_spec = _s5_specs()
    nq, ns = N_GROUPS // GQ, GQ * SSM_STATE
    per_row = jax.ShapeDtypeStruct((N_GROUPS * SSM_GROUP, SSM_STATE), F32)
    return pl.pallas_call(
        body, name="s5_bwd", grid=(nq,),
        in_specs=[col, col, a_spec, rows, rows, rows, rows, d_spec],
        out_specs=[col, rows, rows, rows, rows, pl.BlockSpec((None, 2, ns), lambda q: (q, 0, 0)), d_spec],
        out_shape=[_out_hbm((TP, D_SSM), F32), per_row, per_row, per_row, per_row,
                   jax.ShapeDtypeStruct((nq, 2, ns), F32), jax.ShapeDtypeStruct((1, D_SSM), F32)],
        scratch_shapes=[pltpu.VMEM((TP, ns), F32)] * 4,
        compiler_params=_params(1, 56 << 20),
    )(_in_hbm(u), _in_hbm(dy), a8, bb_re, bb_im, c_re, c_im, dsk)


def _to_segments(a, cols=None):
    a = jnp.pad(a[:, :cols], ((0, TP - T), (0, 0)))
    return a.reshape(N_SEG, SEG, a.shape[-1]).transpose(1, 0, 2).reshape(TP, a.shape[-1])


def _from_segments(a):
    return a.reshape(SEG, N_SEG, a.shape[-1]).transpose(1, 0, 2).reshape(TP, a.shape[-1])[:T]


def _ff_specs():
    up_spec = pl.BlockSpec((2, None, T, 128), lambda p, cb: (0, p, 0, cb))
    piece = pl.BlockSpec((None, T, 128), lambda p, cb: (p, 0, cb))
    cw_spec = pl.BlockSpec((None, 3, 128), lambda p, cb: (p, 0, cb))
    cb_spec = pl.BlockSpec((None, 1, 128), lambda p, cb: (p, 0, cb))
    return up_spec, piece, cw_spec, cb_spec


def _conv_gate(gate, w, cb, pad_ref):
    pad_ref[0:8, :] = jnp.zeros((8, 128), F32)
    pad_ref[8:8 + T, :] = gate
    g1, g2 = pad_ref[pl.ds(7, T), :], pad_ref[pl.ds(6, T), :]
    return w[0:1, :] * g2 + w[1:2, :] * g1 + w[2:3, :] * gate + cb, g1, g2


def _ff_act(up, cw4, cb4):
    def body(up_ref, w_ref, b_ref, o_ref, pad_ref):
        gc, _, _ = _conv_gate(up_ref[0].astype(F32), w_ref[...], b_ref[...], pad_ref)
        o_ref[...] = (jax.nn.silu(gc) * up_ref[1].astype(F32)).astype(o_ref.dtype)

    up_spec, piece, cw_spec, cb_spec = _ff_specs()
    n_cb = pl.cdiv(FF_PIECE, 128)
    return pl.pallas_call(
        body, name="ff_act", grid=(4, n_cb), in_specs=[up_spec, cw_spec, cb_spec], out_specs=piece,
        out_shape=_out_hbm((4, T, FF_PIECE), BF16),
        scratch_shapes=[pltpu.VMEM((T + 8, 128), F32)],
        compiler_params=_params(2, 24 << 20),
    )(_in_hbm(up.reshape(2, 4, T, FF_PIECE)), cw4, cb4)


def _ff_act_bwd(up, dact, cw4, cb4):
    def body(up_ref, da_ref, w_ref, b_ref, dup_ref, dw_ref, db_ref, pad_ref, pad2_ref):
        gate, val, w = up_ref[0].astype(F32), up_ref[1].astype(F32), w_ref[...]
        gc, g1, g2 = _conv_gate(gate, w, b_ref[...], pad_ref)
        sg = jax.nn.sigmoid(gc)
        da = da_ref[...].astype(F32)
        dup_ref[1] = (da * gc * sg).astype(dup_ref.dtype)
        dgc = da * val * sg * (1.0 + gc * (1.0 - sg))
        db_ref[...] = jnp.sum(dgc, axis=0, keepdims=True)
        dw_ref[0:1, :] = jnp.sum(dgc * g2, axis=0, keepdims=True)
        dw_ref[1:2, :] = jnp.sum(dgc * g1, axis=0, keepdims=True)
        dw_ref[2:3, :] = jnp.sum(dgc * gate, axis=0, keepdims=True)
        pad2_ref[0:T, :] = dgc
        pad2_ref[T:T + 8, :] = jnp.zeros((8, 128), F32)
        d1, d2 = pad2_ref[pl.ds(1, T), :], pad2_ref[pl.ds(2, T), :]
        dup_ref[0] = (w[2:3, :] * dgc + w[1:2, :] * d1 + w[0:1, :] * d2).astype(dup_ref.dtype)

    up_spec, piece, cw_spec, cb_spec = _ff_specs()
    n_cb = pl.cdiv(FF_PIECE, 128)
    dup, dw, db = pl.pallas_call(
        body, name="ff_act_bwd", grid=(4, n_cb), in_specs=[up_spec, piece, cw_spec, cb_spec],
        out_specs=[up_spec, cw_spec, cb_spec],
        out_shape=[_out_hbm((2, 4, T, FF_PIECE), BF16), jax.ShapeDtypeStruct((4, 3, FF_PIECE), F32),
                   jax.ShapeDtypeStruct((4, 1, FF_PIECE), F32)],
        scratch_shapes=[pltpu.VMEM((T + 8, 128), F32)] * 2,
        compiler_params=_params(2, 32 << 20),
    )(_in_hbm(up.reshape(2, 4, T, FF_PIECE)), _in_hbm(dact), cw4, cb4)
    return dup.reshape(8, T, FF_PIECE), dw, db


def _local_step(x, tgt, meta, p, wf, mid_weights, late_weights, send_grads, send_small):
    sm = {}
    tabs = _rope_tables()
    row = lambda a: a.reshape(1, -1)

    h0, xn, r0 = _rowwise("rms_mix", _rms_mix_fn, [(x, D, 0, N_META)], [meta, row(p['mix_norm'])],
                          [(D, F32), (D, BF16), (1, F32)], [])
    z = _mm("in_proj", xn, wf['w_in'], 'nn', TR, 640, D, F32)
    lam_re, lam_im = p['lam_re'].reshape(N_GROUPS, SSM_STATE), p['lam_im'].reshape(N_GROUPS, SSM_STATE)
    log_dt = p['log_dt'].reshape(1, N_GROUPS)
    bt_re = p['b_re'].reshape(N_GROUPS, SSM_STATE, SSM_GROUP).transpose(2, 0, 1)
    bt_im = p['b_im'].reshape(N_GROUPS, SSM_STATE, SSM_GROUP).transpose(2, 0, 1)
    c_re = p['c_re'].reshape(N_GROUPS * SSM_GROUP, SSM_STATE)
    c_im = p['c_im'].reshape(N_GROUPS * SSM_GROUP, SSM_STATE)
    a_re, a_im, bbt_re, bbt_im = _s5_prep(lam_re, lam_im, log_dt, bt_re, bt_im)
    nq, ns = N_GROUPS // GQ, GQ * SSM_STATE
    a8 = jnp.broadcast_to(jnp.stack([a_re.reshape(nq, ns), a_im.reshape(nq, ns)], 1)[:, :, None, :],
                          (nq, 2, N_SEG, ns))
    gcp = lambda t: t.transpose(1, 0, 2).reshape(N_GROUPS * SSM_GROUP, SSM_STATE)
    bb_re, bb_im = gcp(bbt_re), gcp(bbt_im)
    dsk = row(p['d_skip'])
    u_seg = _to_segments(z, D_SSM)
    y_ssm = _from_segments(_s5_fwd(u_seg, a8, bb_re, bb_im, c_re, c_im, dsk))
    wf = dict(wf, **mid_weights(y_ssm))
    g, = _rowwise("gelu", _gelu_fn, [(y_ssm, D_SSM, 0)], [], [(D_SSM, BF16)], [])
    gl = _mm("glu_proj", g, wf['w_glu'], 'nn', TR, D_SSM, D_SSM, F32)
    yan, ra = _rowwise("glu_norm", _glu_fwd_fn, [(y_ssm, D_SSM, 0), (gl, D_SSM, 0)],
                       [row(p['b_glu']), row(p['out_norm_ssm'])], [(D_SSM, BF16), (1, F32)], [])
    qn, rq, kvn, rkv = _rowwise(
        "rms_q_kv", lambda row0, zq, zkv, gq, gkv: _rms_fwd_fn(row0, zq, gq) + _rms_fwd_fn(row0, zkv, gkv),
        [(z, Q_LORA, D_SSM // Q_LORA), (z, KV_LORA, (D_SSM + Q_LORA) // KV_LORA)],
        [row(p['q_a_norm']), row(p['kv_a_norm'])], [(Q_LORA, BF16), (1, F32), (KV_LORA, BF16), (1, F32)], [])
    q = _mm("q_proj", qn, wf['w_q_b'], 'nn', T, HEAD_PAD, Q_LORA, F32, stack='b_out')
    kv = _mm("kv_proj", kvn, wf['w_kv_b'], 'nn', T, HEAD_PAD, KV_LORA, F32, stack='b_out')
    yb = _attn_fwd(q, kv, z, tabs)
    ybn, rb = _rowwise("rms_attn", _rms_fwd_fn, [(yb, D_SSM, 0)], [row(p['out_norm_attn'])],
                       [(D_SSM, BF16), (1, F32)], [])
    y = jnp.concatenate([yan, ybn], axis=1)
    wf = dict(wf)
    wf['w_out'], token, up_weights = late_weights(y)
    h1 = _mm("out_proj", y, wf['w_out'], 'nn', TR, 1024, D, F32, res=h0)
    xn2, r1 = _rowwise("rms_ffn", _rms_fwd_fn, [(h1, D, 0)], [row(p['ffn_norm']) + token[0:1, 0:1]],
                       [(D, BF16), (1, F32)], [])
    wf['w_up'], token, down_weights = up_weights(xn2)
    up = _mm("up_proj", xn2, wf['w_up'], 'nn', TR, FF_PIECE, D, BF16, stack='b_out')
    cw4 = wf['conv_w'].reshape(3, 4, FF_PIECE).transpose(1, 0, 2)
    cb4 = p['conv_b'].reshape(4, 1, FF_PIECE)
    act = _ff_act(up, cw4, cb4 + token[0:1, 0:1])
    wf['w_down'] = down_weights(act)
    wd4 = wf['w_down'].reshape(4, FF_PIECE, D)
    h2 = _mm("down_proj", act, wd4, 'nn', TR, 1024, FF_PIECE, F32, stack='ab_red', res=h1)
    dh2, dh2_b, sm['final_norm'], loss = _rowwise("loss", _loss_fn, [(h2, D, 0), (tgt, D, 0, N_META)],
                                                  [row(p['final_norm'])], [(D, F32), (D, BF16)], [D, 128])

    big = {}
    dact = _mm("down_bwd_x", dh2_b, wd4, 'nt', TR, FF_PIECE, D, BF16, stack='b_out')
    g_down = _mm("down_bwd_w", act, dh2_b, 'tn', FF_PIECE, 512, T, BF16, stack='a_out').reshape(
        N_DEV, D_FF // N_DEV, D)
    token = send_grads(['w_down'], [g_down], 'cores')
    dup, dcw4, sm['conv_b'] = _ff_act_bwd(up, dact, cw4, cb4 + token[0:1, 0:1])
    dxn2 = _mm("up_bwd_x", dup, wf['w_up'], 'nt', TR, D, FF_PIECE, F32, stack='ab_red')
    g_up = _mm("up_bwd_w", dup, xn2, 'tn', FF_PIECE, 1024, T, BF16, stack='a_out')
    token = send_grads(['w_up'], [g_up], 'cores') + send_grads(['w_down'], dxn2, 'chips')
    dh1, dh1_b, sm['ffn_norm'] = _rowwise("rms_ffn_bwd", _rms_bwd_res_twice_fn,
                                          [(dxn2, D, 0), (h1, D, 0), (r1, 1, 0), (dh2, D, 0)],
                                          [row(p['ffn_norm']) + token[0:1, 0:1]], [(D, F32), (D, BF16)], [D])
    dy = _mm("out_bwd_x", dh1_b, wf['w_out'], 'nt', TR, 1024, D, F32)
    g_out = _mm("out_bwd_w", y, dh1_b, 'tn', 1024, 1024, T, BF16).reshape(N_DEV, D // N_DEV, D)
    dgl, dgd, sm['out_norm_ssm'], sm['b_glu'] = _rowwise(
        "glu_bwd", _glu_bwd_fn, [(dy, D_SSM, 0), (y_ssm, D_SSM, 0), (gl, D_SSM, 0), (ra, 1, 0)],
        [row(p['b_glu']), row(p['out_norm_ssm'])], [(D_SSM, BF16), (D_SSM, F32)], [D_SSM, D_SSM])
    dg = _mm("glu_bwd_x", dgl, wf['w_glu'], 'nt', TR, D_SSM, D_SSM, F32, res=dgd)
    g_glu = _mm("glu_bwd_w", g, dgl, 'tn', D_SSM, D_SSM, T, BF16).reshape(N_DEV, D_SSM // N_DEV, D_SSM)
    dy_ssm, = _rowwise("gelu_bwd", _gelu_bwd_fn, [(dg, D_SSM, 0), (y_ssm, D_SSM, 0)], [], [(D_SSM, F32)], [])
    token = send_grads(['w_up'], g_glu, 'chips')
    dyb, sm['out_norm_attn'] = _rowwise("rms_attn_bwd", _rms_bwd_fn, [(dy, D_SSM, 1), (yb, D_SSM, 0), (rb, 1, 0)],
                                        [row(p['out_norm_attn']) + token[0:1, 0:1]], [(D_SSM, F32)], [D_SSM])
    dq, dkv, dkpe = _attn_bwd(q, kv, z, tabs, dyb)
    g_q = _mm("q_bwd_w", qn, dq, 'tn', Q_LORA, HEAD_PAD, T, BF16, stack='b_out')
    dqn = _mm("q_bwd_x", dq, wf['w_q_b'], 'nt', T, Q_LORA, HEAD_PAD, F32, stack='ab_red')
    g_kv = _mm("kv_bwd_w", kvn, dkv, 'tn', KV_LORA, HEAD_PAD, T, BF16, stack='b_out')
    dkvn = _mm("kv_bwd_x", dkv, wf['w_kv_b'], 'nt', T, KV_LORA, HEAD_PAD, F32, stack='ab_red')
    token = send_grads(['w_out', 'w_glu', 'w_q_b', 'w_kv_b'], [g_out, g_glu, g_q, g_kv], 'both')
    def both_bwd(row0, dq_, zq, rq_, dkv_, zkv, rkv_, gq, gkv):
        (dzq, dgq), (dzkv, dgkv) = _rms_bwd(dq_, zq, rq_, gq), _rms_bwd(dkv_, zkv, rkv_, gkv)
        return dzq, dzkv, dgq, dgkv

    dq_a, dkv_a, sm['q_a_norm'], sm['kv_a_norm'] = _rowwise(
        "rms_q_kv_bwd", both_bwd,
        [(dqn, Q_LORA, 0), (z, Q_LORA, D_SSM // Q_LORA), (rq, 1, 0),
         (dkvn, KV_LORA, 0), (z, KV_LORA, (D_SSM + Q_LORA) // KV_LORA), (rkv, 1, 0)],
        [row(p['q_a_norm']), row(p['kv_a_norm'])], [(Q_LORA, BF16), (KV_LORA, BF16)], [Q_LORA, KV_LORA])
    dk_pe = _kpe_bwd(dkpe, tabs)
    du_seg, dbre, dbim, dcre, dcim, da, sm['d_skip'] = _s5_bwd(
        u_seg, _to_segments(dy_ssm), a8, bb_re, bb_im, c_re, c_im, dsk + token[0:1, 0:1])
    du = _from_segments(du_seg).astype(BF16)
    cgp = lambda d: d.reshape(N_GROUPS, SSM_GROUP, SSM_STATE).transpose(1, 0, 2)
    dlam_re, dlam_im, dlog_dt, dbt_re, dbt_im = _s5_prep_bwd(
        lam_re, lam_im, log_dt, bt_re, bt_im, da[:, 0, :].reshape(N_GROUPS, SSM_STATE),
        da[:, 1, :].reshape(N_GROUPS, SSM_STATE), cgp(dbre), cgp(dbim))
    sm['lam_re'], sm['lam_im'], sm['log_dt'] = dlam_re, dlam_im, dlog_dt
    sm['b_re'], sm['b_im'] = gcp(dbt_re), gcp(dbt_im)
    sm['c_re'], sm['c_im'] = dcre, dcim
    token = send_small(sm, dcw4, loss)
    dz = jnp.concatenate([du, dq_a, dkv_a, dk_pe], axis=1)
    dxn = _mm("in_bwd_x", dz, wf['w_in'], 'nt', TR, 1024, D_IN_PAD, F32)
    big['w_in'] = _mm("in_bwd_w", xn, dz, 'tn', 1024, 640, T, BF16).reshape(N_DEV, D // N_DEV, D_IN_PAD)
    dh0, d_mix_norm = _rowwise("rms_mix_bwd", _rms_bwd_res_fn,
                               [(dxn, D, 0), (h0, D, 0), (r0, 1, 0), (dh1, D, 0)],
                               [row(p['mix_norm']) + token[0:1, 0:1]], [(D, F32)], [D])
    return dh0[N_META:], dh0[:N_META], d_mix_norm, big


def _place():
    x, y, c = lax.axis_index("x"), lax.axis_index("y"), lax.axis_index("c")
    return x, y, c, [(1 - x, y), (x, 1 - y), (1 - x, 1 - y)]


_HBM = pl.BlockSpec(memory_space=pltpu.HBM)


def _all_gather(name, arrs, chip_blocks=()):
    n, nc = len(arrs), len(chip_blocks)

    def body(*refs):
        ins, cins, outs, couts = refs[:n], refs[n:n + nc], refs[n + nc:2 * n + nc], refs[2 * n + nc:2 * (n + nc)]
        send, recv, loc, csend, crecv = refs[2 * (n + nc):]
        x, y, c, chips = _place()
        me, sib = (x, y, c), (x, y, 1 - c)

        def cp(a, k, blk, to, src=None):
            dst = outs[a].at[4 * blk[0] + 2 * blk[1] + blk[2]]
            return pltpu.make_async_remote_copy(src_ref=dst if src is None else src, dst_ref=dst,
                                                send_sem=send.at[a, k], recv_sem=recv.at[a, k],
                                                device_id=to, device_id_type=MESH)

        between_chips = [pltpu.make_async_remote_copy(src_ref=cins[a].at[2 * ch[0] + ch[1]], dst_ref=couts[a].at[j],
                                                      send_sem=csend.at[a, j], recv_sem=crecv.at[a, j],
                                                      device_id=(*ch, c), device_id_type=MESH)
                         for a in range(nc) for j, ch in enumerate(chips)]
        mine = [pltpu.make_async_copy(ins[a], outs[a].at[4 * x + 2 * y + c], loc.at[a]) for a in range(n)]
        first = []
        for a in range(n):
            first.append(cp(a, 0, me, sib, src=ins[a]))
            first += [cp(a, 1 + j, me, (*ch, c), src=ins[a]) for j, ch in enumerate(chips)]
        for f in mine + first + between_chips:
            f.start()
        passed = []
        for a in range(n):
            for j, ch in enumerate(chips):
                cp(a, 1 + j, (*ch, c), me).wait_recv()
                passed.append(cp(a, 4 + j, (*ch, c), sib))
                passed[-1].start()
        for a in range(n):
            cp(a, 0, sib, me).wait_recv()
            for j, ch in enumerate(chips):
                cp(a, 4 + j, (*ch, 1 - c), me).wait_recv()
        for f in first + passed:
            f.wait_send()
        for m in mine + between_chips:
            m.wait()

    out = pl.pallas_call(
        body, name=name, in_specs=[_HBM] * (n + nc), out_specs=[_HBM] * (n + nc),
        out_shape=[jax.ShapeDtypeStruct((N_DEV,) + a.shape, a.dtype) for a in arrs]
        + [jax.ShapeDtypeStruct((3,) + a.shape[1:], a.dtype) for a in chip_blocks],
        scratch_shapes=[pltpu.SemaphoreType.DMA((n, 7)), pltpu.SemaphoreType.DMA((n, 7)),
                        pltpu.SemaphoreType.DMA((n,)), pltpu.SemaphoreType.DMA((max(nc, 1), 3)),
                        pltpu.SemaphoreType.DMA((max(nc, 1), 3))],
    )(*arrs, *chip_blocks)
    return (out[:n], out[n:]) if nc else out


def _exchange_cores(name, arrs):
    n = len(arrs)

    def body(*refs):
        ins, outs = refs[:n], refs[n:2 * n]
        send, recv = refs[2 * n:]
        x, y, c, _ = _place()
        remote = [pltpu.make_async_remote_copy(src_ref=ins[a].at[2 * k + 1 - c], dst_ref=outs[a].at[k],
                                               send_sem=send.at[a, k], recv_sem=recv.at[a, k],
                                               device_id=(x, y, 1 - c), device_id_type=MESH)
                  for a in range(n) for k in range(4)]
        for d in remote:
            d.start()
        for d in remote:
            d.wait()

    return pl.pallas_call(
        body, name=name, in_specs=[_HBM] * n, out_specs=[_HBM] * n,
        out_shape=[jax.ShapeDtypeStruct((4,) + a.shape[1:], a.dtype) for a in arrs],
        scratch_shapes=[pltpu.SemaphoreType.DMA((n, 4))] * 2,
    )(*arrs)


_SEM = pl.BlockSpec(memory_space=pltpu.SEMAPHORE)
_EFFECT = pltpu.SideEffectType.DATAFLOW_SIDE_EFFECTING


def _split_exchange(name, srcs, land_shapes, copies_of, n_cp, after=None, all_of=None):
    n = len(srcs)
    in_place = land_shapes is None
    lands = [] if in_place else [pltpu.with_memory_space_constraint(lax.empty(s, a.dtype), pltpu.HBM)
                                 for s, a in zip(land_shapes, srcs)]
    nb = n + len(lands)

    per = 1 if all_of else n_cp
    n_sem = n * per

    def descriptors(refs, send, recv, mirror):
        src_refs, land_refs = refs[:n], refs[nb - n:nb]
        x, y, c, chips = _place()
        out = []
        for a in range(n):
            if mirror and all_of:
                whole = all_of(a, land_refs[a])
                out.append(pltpu.make_async_remote_copy(src_ref=whole, dst_ref=whole, send_sem=send[a],
                                                        recv_sem=recv[a], device_id=(x, y, 1 - c),
                                                        device_id_type=MESH))
                continue
            for k, (src, dst, to, back) in enumerate(copies_of(a, src_refs[a], land_refs[a], x, y, c, chips)):
                out.append(pltpu.make_async_remote_copy(src_ref=src, dst_ref=back if mirror else dst,
                                                        send_sem=send[a * per + k % per],
                                                        recv_sem=recv[a * per + k % per],
                                                        device_id=to, device_id_type=MESH))
        return out

    def start_body(*refs):
        first = nb + (after is not None)
        sems = refs[first:first + 2 * n_sem]
        for d in descriptors(refs, sems[:n_sem], sems[n_sem:], False):
            d.start()
        refs[-1][...] = jnp.zeros((8, 128), F32)

    hbm_like = lambda arrs: [pltpu.HBM(a.shape, a.dtype) for a in arrs]
    order = [] if after is None else [after]
    res = pl.pallas_call(
        start_body, name=name + "_start", in_specs=[_HBM] * nb + [pl.BlockSpec(memory_space=pl.ANY)] * len(order),
        out_specs=[_SEM] * (2 * n_sem) + [_HBM] * nb + [pl.BlockSpec(memory_space=pltpu.VMEM)],
        out_shape=[pltpu.SemaphoreType.DMA(())] * (2 * n_sem)
        + hbm_like(srcs) + hbm_like(lands) + [jax.ShapeDtypeStruct((8, 128), F32)],
        input_output_aliases={i: 2 * n_sem + i for i in range(nb)},
        compiler_params=pltpu.CompilerParams(has_side_effects=_EFFECT),
    )(*srcs, *lands, *order)
    sems, thru, token = res[:2 * n_sem], res[2 * n_sem:2 * n_sem + nb], res[-1]

    def wait(after):
        def wait_body(*refs):
            s = refs[nb:nb + 2 * n_sem]
            for d in descriptors(refs, s[:n_sem], s[n_sem:], True):
                d.wait_send()
                d.wait_recv()

        out = pl.pallas_call(
            wait_body, name=name + "_wait",
            in_specs=[_HBM] * nb + [_SEM] * (2 * n_sem) + [pl.BlockSpec(memory_space=pl.ANY)],
            out_specs=[_HBM] * nb, out_shape=hbm_like(srcs) + hbm_like(lands),
            input_output_aliases={i: i for i in range(nb)},
            compiler_params=pltpu.CompilerParams(has_side_effects=_EFFECT),
        )(*thru, *sems, after)
        return out[:n], out[nb - n:]

    return token, wait


def _gather_copies(a, src, land, x, y, c, chips):
    peers = [(x, y, 1 - c)] + [(*ch, c) for ch in chips]
    me = 4 * x + 2 * y + c
    return [(src, land.at[me], to, land.at[4 * to[0] + 2 * to[1] + to[2]]) for to in peers]


def _to_all_copies(a, src, land, x, y, c, chips):
    peers = [(x, y, 1 - c)] + [(*ch, cc) for ch in chips for cc in (c, 1 - c)]
    me = 4 * x + 2 * y + c
    return [(src, land.at[me], to, land.at[4 * to[0] + 2 * to[1] + to[2]]) for to in peers]


def _cores_copies(a, src, land, x, y, c, chips):
    return [(src.at[2 * k + 1 - c], land.at[k], (x, y, 1 - c), land.at[k]) for k in range(4)]


def _forward_copies(a, src, land, x, y, c, chips):
    slot = lambda ch, core: 4 * ch[0] + 2 * ch[1] + core
    return [(land.at[slot(ch, c)], land.at[slot(ch, c)], (x, y, 1 - c), land.at[slot(ch, 1 - c)]) for ch in chips]


def _chips_copies(a, src, land, x, y, c, chips):
    return [(src.at[2 * ch[0] + ch[1]], land.at[j], (*ch, c), land.at[j]) for j, ch in enumerate(chips)]


def _gather_forward(name, arrs):
    n = len(arrs)

    def body(*refs):
        ins, outs = refs[:n], refs[n:2 * n]
        send, recv = refs[2 * n:]
        x, y, c, chips = _place()
        sends, recvs = [], []
        for a in range(n):
            for j, ch in enumerate(chips):
                mine, theirs = 4 * ch[0] + 2 * ch[1] + c, 4 * ch[0] + 2 * ch[1] + 1 - c
                sends.append(pltpu.make_async_remote_copy(src_ref=ins[a].at[mine], dst_ref=outs[a].at[mine],
                                                          send_sem=send.at[a, j], recv_sem=recv.at[a, j],
                                                          device_id=(x, y, 1 - c), device_id_type=MESH))
                recvs.append(pltpu.make_async_remote_copy(src_ref=ins[a].at[theirs], dst_ref=outs[a].at[theirs],
                                                          send_sem=send.at[a, j], recv_sem=recv.at[a, j],
                                                          device_id=(x, y, 1 - c), device_id_type=MESH))
        for d in sends:
            d.start()
        for s, r in zip(sends, recvs):
            s.wait_send()
            r.wait_recv()

    return pl.pallas_call(
        body, name=name, in_specs=[_HBM] * n, out_specs=[_HBM] * n,
        out_shape=[jax.ShapeDtypeStruct(a.shape, a.dtype) for a in arrs],
        input_output_aliases={i: i for i in range(n)},
        scratch_shapes=[pltpu.SemaphoreType.DMA((n, 3))] * 2,
    )(*arrs)


def _blocks_of(r, c):
    if r * c * 4 <= (2 << 20):
        return r, c
    if r % 128 == 0:
        return 128, c
    return r, 256


def _pair_sum(name, own, got, core):
    _, r, c = got.shape
    rb, cb = next(d for d in range(r, 0, -16) if r % d == 0 and d * c * 2 <= (3 << 20)), c

    def body(s_ref, a_ref, b_ref, o_ref):
        o_ref[...] = (a_ref[...].astype(F32) + b_ref[...].astype(F32)).astype(o_ref.dtype)

    return pl.pallas_call(
        body, name=name, out_shape=_out_hbm((4, r, c), got.dtype),
        grid_spec=pltpu.PrefetchScalarGridSpec(
            num_scalar_prefetch=1, grid=(4, r // rb, c // cb),
            in_specs=[pl.BlockSpec((None, rb, cb), lambda k, i, j, s: (2 * k + s[0], i, j)),
                      pl.BlockSpec((None, rb, cb), lambda k, i, j, s: (k, i, j))],
            out_specs=pl.BlockSpec((None, rb, cb), lambda k, i, j, s: (k, i, j))),
        compiler_params=_params(3, 6 * _nbytes((rb, cb), got.dtype) + 3 * _nbytes((rb, cb), F32) + (4 << 20)),
    )(core, _in_hbm(own), _in_hbm(got))


def _adamw_math(w, g, m, v):
    m = B1 * m + (1.0 - B1) * g
    v = B2 * v + (1.0 - B2) * (g * g)
    m_hat = m / (1.0 - B1 ** STEP)
    v_hat = v / (1.0 - B2 ** STEP)
    return -LR * (m_hat / (jnp.sqrt(v_hat) + ADAM_EPS) + WD * w), m, v


def _adamw_big(name, own, got, chip, w, m, v, behind):
    r, c = w.shape
    rb, cb = _blocks_of(r, c)

    def body(s_ref, o_ref, p_ref, w_ref, m_ref, v_ref, behind_ref, g_ref, d_ref, nm_ref, nv_ref):
        g = o_ref[...].astype(F32)
        for k in range(3):
            g = g + p_ref[k].astype(F32)
        g_ref[...] = g
        d_ref[...], nm_ref[...], nv_ref[...] = _adamw_math(w_ref[...], g, m_ref[...], v_ref[...])

    blk = pl.BlockSpec((rb, cb), lambda i, j, s: (i, j))
    return pl.pallas_call(
        body, name=name, out_shape=[jax.ShapeDtypeStruct((r, c), F32)] * 4,
        grid_spec=pltpu.PrefetchScalarGridSpec(
            num_scalar_prefetch=1, grid=(r // rb, c // cb),
            in_specs=[pl.BlockSpec((None, rb, cb), lambda i, j, s: (s[0], i, j)),
                      pl.BlockSpec((3, rb, cb), lambda i, j, s: (0, i, j)), blk, blk, blk,
                      pl.BlockSpec(memory_space=pl.ANY)],
            out_specs=[blk] * 4),
        compiler_params=_params(2, 40 << 20),
    )(chip, *map(_in_hbm, (own, got, w, m, v)), behind)


def _adamw_multi(name, items, sums=(), row_blocks=1, own=(), dev=None):
    n, ns = len(items), len(sums)
    n_own = len(own)
    assert n_own in (0, n + ns)

    def body(*refs):
        dev_ref = refs[0] if n_own else None
        refs = refs[1:] if n_own else refs
        owns, refs = refs[:n_own], refs[n_own:]
        ins, outs = refs[:4 * n + ns], refs[4 * n + ns:]

        def total(ref, t):
            g = None
            for d in range(ref.shape[0]):
                part = jnp.where(dev_ref[0] == d, owns[t][...], ref[d]) if n_own else ref[d]
                g = part.astype(F32) if g is None else g + part.astype(F32)
            return g

        for t in range(n):
            p_ref, w_ref, m_ref, v_ref = ins[4 * t:4 * t + 4]
            g = total(p_ref, t)
            outs[4 * t][...] = g
            outs[4 * t + 1][...], outs[4 * t + 2][...], outs[4 * t + 3][...] = _adamw_math(
                w_ref[...], g, m_ref[...], v_ref[...])
        for t in range(ns):
            outs[4 * n + t][...] = total(ins[4 * n + t], n + t)

    def spec(shape, lead):
        blk = (shape[0] // row_blocks,) + tuple(shape[1:])
        nd = len(shape)
        if lead:
            return pl.BlockSpec((lead,) + blk, lambda i: (0, i) + (0,) * (nd - 1))
        return pl.BlockSpec(blk, lambda i: (i,) + (0,) * (nd - 1))

    operands, in_specs, out_specs, out_shape = [], [], [], []
    if n_own:
        operands += [dev] + list(own)
        in_specs += [pl.BlockSpec(memory_space=pltpu.SMEM)] + [spec(o.shape, 0) for o in own]
    for parts, w, m, v in items:
        assert parts.shape[1:] == w.shape == m.shape == v.shape, (name, parts.shape, w.shape)
        operands += [parts, w, m, v]
        in_specs += [spec(w.shape, parts.shape[0])] + [spec(w.shape, 0)] * 3
        out_specs += [spec(w.shape, 0)] * 4
        out_shape += [jax.ShapeDtypeStruct(w.shape, F32)] * 4
    for parts in sums:
        operands.append(parts)
        in_specs.append(spec(parts.shape[1:], parts.shape[0]))
        out_specs.append(spec(parts.shape[1:], False))
        out_shape.append(jax.ShapeDtypeStruct(parts.shape[1:], F32))
    return pl.pallas_call(
        body, name=name, grid=(row_blocks,), in_specs=in_specs, out_specs=out_specs, out_shape=out_shape,
        compiler_params=_params(1, 56 << 20),
    )(*operands)


def kernel(x, meta_tokens, mix_norm, w_in, lam_re, lam_im, log_dt, b_re, b_im, c_re, c_im, d_skip, w_glu, b_glu, q_a_norm, w_q_b, kv_a_norm, w_kv_b, out_norm_ssm, out_norm_attn, w_out, ffn_norm, w_up, conv_w, conv_b, w_down, final_norm, loss_target, m_meta_tokens, m_mix_norm, m_w_in, m_lam_re, m_lam_im, m_log_dt, m_b_re, m_b_im, m_c_re, m_c_im, m_d_skip, m_w_glu, m_b_glu, m_q_a_norm, m_w_q_b, m_kv_a_norm, m_w_kv_b, m_out_norm_ssm, m_out_norm_attn, m_w_out, m_ffn_norm, m_w_up, m_conv_w, m_conv_b, m_w_down, m_final_norm, v_meta_tokens, v_mix_norm, v_w_in, v_lam_re, v_lam_im, v_log_dt, v_b_re, v_b_im, v_c_re, v_c_im, v_d_skip, v_w_glu, v_b_glu, v_q_a_norm, v_w_q_b, v_kv_a_norm, v_w_kv_b, v_out_norm_ssm, v_out_norm_attn, v_w_out, v_ffn_norm, v_w_up, v_conv_w, v_conv_b, v_w_down, v_final_norm):
    given = dict(locals())
    w = {n: given[n] for n in WEIGHTS}
    m = {n: given['m_' + n] for n in WEIGHTS}
    v = {n: given['v_' + n] for n in WEIGHTS}
    dev = 4 * lax.axis_index("x") + 2 * lax.axis_index("y") + lax.axis_index("c")

    shard = {
        'w_in': jnp.pad(w_in[0], ((0, 0), (0, D_IN_PAD - D_IN))),
        'w_glu': w_glu[0],
        'w_q_b': jnp.pad(w_q_b[0], ((0, 0), (0, HEAD_PAD - QK_NOPE - QK_ROPE))),
        'w_kv_b': w_kv_b[0],
        'w_out': w_out[0],
        'w_up': w_up[0],
        'w_down': w_down[0],
    }
    core = lax.axis_index("c").astype(jnp.int32).reshape(1)
    chip = (2 * lax.axis_index("x") + lax.axis_index("y")).astype(jnp.int32).reshape(1)
    mid, late = ['w_q_b', 'w_kv_b', 'w_glu'], ['w_out', 'w_up', 'w_down']
    shard = {n: a.astype(BF16) for n, a in shard.items()}

    gathered = _all_gather("gather_early", [shard['w_in'], meta_tokens, jnp.pad(conv_w[0], ((0, 5), (0, 0)))])
    wf = {'w_in': gathered[0].reshape(-1, D_IN_PAD)}
    meta = gathered[1].transpose(1, 0, 2).reshape(N_META, D)
    wf['conv_w'] = gathered[2][:, :3].transpose(1, 0, 2).reshape(3, D_FF)
    mid_token, wait_mid = _split_exchange("gather_mid", [shard[n] for n in mid],
                                          [(N_DEV,) + shard[n].shape for n in mid], _gather_copies, 4,
                                          after=gathered[0])
    gather_token, late_waits = mid_token, {}
    for n in late:
        gather_token, late_waits[n] = _split_exchange("gather_" + n, [shard[n]], [(N_DEV,) + shard[n].shape],
                                                      _gather_copies, 4, after=gather_token)

    def with_mine(mine, landed):
        return [lax.dynamic_update_slice(a, s[None], (dev, 0, 0)) for s, a in zip(mine, landed)]

    def mid_weights(after):
        full = _gather_forward("gather_forward_mid", with_mine(*wait_mid(after)))
        return {'w_q_b': full[0], 'w_kv_b': full[1], 'w_glu': full[2].reshape(-1, D_SSM)}

    def late_weights(after):
        full_out, = _gather_forward("gather_forward_w_out", with_mine(*late_waits['w_out'](after)))
        token, wait_up = _split_exchange("gather_forward_w_up", with_mine(*late_waits['w_up'](after)), None,
                                         _forward_copies, 3, after=full_out)

        def up_weights(after):
            full_up, = wait_up(after)[0]
            token, wait_down = _split_exchange("gather_forward_w_down", with_mine(*late_waits['w_down'](after)),
                                               None, _forward_copies, 3, after=full_up)
            return full_up, token, lambda after: wait_down(after)[0][0].reshape(-1, D)

        return full_out.reshape(-1, D), token, up_weights

    pending, between_cores = [], {}

    def send_grads(names, arg, stage):
        if stage == 'cores':
            token, between_cores[names[0]] = _split_exchange(
                "reduce_cores_" + names[0], arg, [(4,) + a.shape[1:] for a in arg], _cores_copies, 4)
            return token
        mine, got = between_cores.pop(names[0])(arg) if stage == 'chips' else (
            arg, _exchange_cores("reduce_cores_" + names[0], arg))
        sums = [_pair_sum("sum_cores_" + n, a, g, core) for n, a, g in zip(names, mine, got)]
        token, wait = _split_exchange("reduce_chips_" + names[0], sums, [(3,) + s.shape[1:] for s in sums],
                                      _chips_copies, 3)
        pending.append((names, wait))
        return token

    small_sent = []
    s5_bc = ['b_re', 'b_im', 'c_re', 'c_im']

    def send_small(sm, dcw4, loss_part):
        names = [n for n in SMALL if n in sm]
        arrs = [sm[n].reshape(-1, 128).astype(BF16) if n in s5_bc else sm[n] for n in names] + [dcw4, loss_part]
        token, wait = _split_exchange("gather_small", arrs, [(N_DEV,) + a.shape for a in arrs], _to_all_copies, 7,
                                      all_of=lambda a, land: land.at[pl.ds(0, N_DEV - 1)])
        small_sent.append((names, wait))
        return token

    small = {n: w[n] for n in SMALL}
    small['mix_norm'] = mix_norm + gather_token[0:1, 0:1]
    grad_x, dmeta, d_mix_norm, big = _local_step(x[0], loss_target[0], meta, small, wf, mid_weights, late_weights,
                                                  send_grads, send_small)
    grads, deltas, new_m, new_v = {}, {}, {}, {}

    def keep(n, outs):
        grads[n], deltas[n], new_m[n], new_v[n] = (o.reshape(w[n].shape) for o in outs)

    rest = list(big)
    from_core = _exchange_cores("reduce_cores", [big[n] for n in rest])
    chip_sums = [_pair_sum("sum_cores_" + n, big[n], got, core) for n, got in zip(rest, from_core)]
    last = chip_sums + [d_mix_norm, dmeta]
    n_big = len(chip_sums)
    last_token, wait_last = _split_exchange(
        "reduce_last", last, [(3,) + s.shape[1:] for s in chip_sums] + [(N_DEV,) + a.shape for a in last[n_big:]],
        lambda a, *at: _chips_copies(a, *at) if a < n_big else _to_all_copies(a, *at), 7,
        all_of=lambda a, land: land if a < n_big else land.at[pl.ds(0, N_DEV - 1)])

    def update(n, own, got):
        if n == 'w_up':
            outs = _adamw_big("adamw_" + n, own, got, chip, w[n][0].T, m[n][0].T, v[n][0].T, last_token)
            return keep(n, [o.T for o in outs])
        cols = w[n].shape[2]
        keep(n, _adamw_big("adamw_" + n, own[:, :, :cols], got[:, :, :cols], chip, w[n][0], m[n][0], v[n][0],
                           last_token))

    for names, wait in pending:
        for n, own, got in zip(names, *wait(chip_sums[0])):
            update(n, own, got)
    updated = [new_v[n].reshape(-1)[:1] for names, _ in pending for n in names]
    sent, landed = wait_last(functools.reduce(jnp.add, updated))
    for n, own, got in zip(rest, sent, landed):
        update(n, own, got)

    device = dev.astype(jnp.int32).reshape(1)
    outs = _adamw_multi("adamw_last", [(landed[n_big], mix_norm, m['mix_norm'], v['mix_norm'])],
                        sums=[landed[n_big + 1]], own=sent[n_big:], dev=device)
    keep('mix_norm', outs[:4])
    g_meta = outs[4]
    names, wait = small_sent[0]
    mine, landed = wait(grads['mix_norm'])
    parts, own = dict(zip(names + ['conv_w', 'loss'], landed)), dict(zip(names + ['conv_w', 'loss'], mine))
    wide = [n for n in names if n not in s5_bc]
    summed = ['conv_w', 'loss'] + s5_bc
    outs = _adamw_multi("adamw_small",
                        [(parts[n],) + tuple(t[n].reshape(own[n].shape) for t in (w, m, v)) for n in wide],
                        sums=[parts[n] for n in summed], own=[own[n] for n in wide + summed], dev=device)
    for i, n in enumerate(wide):
        keep(n, outs[4 * i:4 * i + 4])
    g_cw4, loss = outs[-6:-4]
    rows_gc = lambda a, n: (a.reshape(N_GROUPS, SSM_STATE, SSM_GROUP).transpose(0, 2, 1) if n[0] == 'b' else a
                            ).reshape(N_GROUPS * SSM_GROUP, SSM_STATE)
    outs = _adamw_multi("adamw_s5_bc", [(g.reshape(1, N_GROUPS * SSM_GROUP, SSM_STATE),)
                                        + tuple(rows_gc(t[n], n) for t in (w, m, v))
                                        for n, g in zip(s5_bc, outs[-4:])])
    for i, n in enumerate(s5_bc):
        back = outs[4 * i:4 * i + 4]
        if n[0] == 'b':
            back = [o.reshape(N_GROUPS, SSM_GROUP, SSM_STATE).transpose(0, 2, 1) for o in back]
        keep(n, back)

    g_meta = lax.dynamic_slice(g_meta, (0, dev * (D // N_DEV)), (N_META, D // N_DEV))
    g_conv = lax.dynamic_slice(g_cw4.transpose(1, 0, 2).reshape(3, D_FF), (0, dev * (D_FF // N_DEV)),
                               (3, D_FF // N_DEV))
    rows8 = lambda a: jnp.pad(a.reshape(3, D_FF // N_DEV), ((0, 5), (0, 0)))
    outs = _adamw_multi("adamw_cols", [(g_meta[None], meta_tokens, m['meta_tokens'], v['meta_tokens']),
                                       (rows8(g_conv)[None], rows8(conv_w), rows8(m['conv_w']), rows8(v['conv_w']))])
    keep('meta_tokens', outs[:4])
    keep('conv_w', [o[:3] for o in outs[4:]])

    return (loss[0, 0], grad_x[None], *[grads[n] for n in WEIGHTS], *[deltas[n] for n in WEIGHTS],
            *[new_m[n] for n in WEIGHTS], *[new_v[n] for n in WEIGHTS])
```
